```python
import jax
import jax.numpy as jnp
from jax import lax
import numpy as np

D_MODEL = 1024
BATCH = 8
SEQ = 2048
DEPTH = 4

CHUNK = 64
Q_BLOCK = 128
N_HEADS = 4
HEAD_DIM = 128
BRANCH_W = N_HEADS * HEAD_DIM
RET_DK = HEAD_DIM
RET_DV = HEAD_DIM
GLA_DK = HEAD_DIM // 2
GLA_DV = HEAD_DIM
GLA_LOWRANK = 16
GLA_TAU = 16.0
FOX_D = HEAD_DIM
N_BRANCH = 3
D_FF = 2816
CONV_W = 3
ROPE_BASE = 10000.0
EPS = 1e-6

IN_SPLITS = (
    N_HEADS * RET_DK, N_HEADS * RET_DK, N_HEADS * RET_DV, N_HEADS * RET_DV,
    N_HEADS * GLA_DK, N_HEADS * GLA_DK, N_HEADS * GLA_DV, GLA_LOWRANK, N_HEADS * GLA_DV,
    N_HEADS * FOX_D, N_HEADS * FOX_D, N_HEADS * FOX_D, N_HEADS,
)
IN_W = sum(IN_SPLITS)

kernel_name = 'hybrid_ret_gla_fox_adaln_convffn'

F32 = jnp.float32


def rms_norm(x, g):
    xf = x.astype(F32)
    y = xf * lax.rsqrt(jnp.mean(xf * xf, axis=-1, keepdims=True) + EPS)
    return (y * g).astype(x.dtype)


def head_group_norm(x, g):
    xf = x.astype(F32)
    mu = jnp.mean(xf, axis=-1, keepdims=True)
    xc = xf - mu
    return xc * lax.rsqrt(jnp.mean(xc * xc, axis=-1, keepdims=True) + EPS) * g


def modulate(h, shift, scale):
    return h * (1.0 + scale[:, None, :]) + shift[:, None, :]


def rotary(x, pos):
    half = x.shape[-1] // 2
    inv_freq = ROPE_BASE ** (-jnp.arange(half, dtype=F32) / half)
    ang = pos[:, None] * inv_freq[None, :]
    cos = jnp.cos(ang)[None, :, None, :]
    sin = jnp.sin(ang)[None, :, None, :]
    x1, x2 = x[..., :half], x[..., half:]
    return jnp.concatenate([x1 * cos - x2 * sin, x1 * sin + x2 * cos], axis=-1)


def retention(q, k, v):
    b, s, h, dk = q.shape
    dv = v.shape[-1]
    n = s // CHUNK
    pos = jnp.arange(s, dtype=F32)
    q = rotary(q.astype(F32), pos)
    k = rotary(k.astype(F32), pos) * dk ** -0.5
    v = v.astype(F32)
    log_g = jnp.log1p(-jnp.exp2(-5.0 - jnp.arange(h, dtype=F32)))
    idx = jnp.arange(CHUNK, dtype=F32)
    d_intra = jnp.exp(jnp.abs(idx[:, None] - idx[None, :])[None] * log_g[:, None, None])
    qc = q.reshape(b, n, CHUNK, h, dk)
    kc = k.reshape(b, n, CHUNK, h, dk)
    vc = v.reshape(b, n, CHUNK, h, dv)
    scores = jnp.einsum('bnihd,bnjhd->bnhij', qc, kc) * d_intra
    o_intra = jnp.einsum('bnhij,bnjhe->bnihe', scores, vc)
    k_w = jnp.exp((CHUNK - 1.0 - idx)[:, None] * log_g[None, :])
    kv = jnp.einsum('bnjhd,bnjhe->nbhde', kc * k_w[:, :, None], vc)
    g_chunk = jnp.exp(CHUNK * log_g)[None, :, None, None]

    def step(r, kv_n):
        return g_chunk * r + kv_n, r

    _, r_prev = lax.scan(step, jnp.zeros((b, h, dk, dv), F32), kv)
    q_w = jnp.exp((idx + 1.0)[:, None] * log_g[None, :])
    o_cross = jnp.einsum('bnihd,nbhde->bnihe', qc * q_w[:, :, None], r_prev)
    return (o_intra + o_cross).reshape(b, s, h, dv)


def gla(q, k, v, log_a):
    b, s, h, dk = q.shape
    dv = v.shape[-1]
    n = s // CHUNK
    qc = (q.astype(F32) * dk ** -0.5).reshape(b, n, CHUNK, h, dk)
    kc = k.astype(F32).reshape(b, n, CHUNK, h, dk)
    vc = v.astype(F32).reshape(b, n, CHUNK, h, dv)
    la = log_a.astype(F32).reshape(b, n, CHUNK, h, dk)
    b_cum = jnp.cumsum(la, axis=2)
    b_end = b_cum[:, :, -1:]
    kv = jnp.einsum('bnjhd,bnjhe->nbhde', kc * jnp.exp(b_end - b_cum), vc)
    a = jnp.exp(b_end[:, :, 0]).transpose(1, 0, 2, 3)

    def step(st, inp):
        a_n, kv_n = inp
        st = a_n[..., None] * st + kv_n
        return st, st

    _, s_all = lax.scan(step, jnp.zeros((b, h, dk, dv), F32), (a, kv))
    o = jnp.einsum('bnihd,nbhde->bnihe', qc, s_all)
    return o.reshape(b, s, h, dv)


def forgetting_attention(q, k, v, f_logit):
    b, s, h, d = q.shape
    log_f = jax.nn.log_sigmoid(f_logit.astype(F32))
    cum = jnp.cumsum(log_f, axis=1).transpose(0, 2, 1)
    outs = []
    for blk in range(s // Q_BLOCK):
        q0 = blk * Q_BLOCK
        q1 = q0 + Q_BLOCK
        logits = jnp.einsum('bihd,bjhd->bhij', q[:, q0:q1], k[:, :q1]).astype(F32) * d ** -0.5
        logits = logits + cum[:, :, q0:q1, None] - cum[:, :, None, :q1]
        mask = jnp.arange(q0, q1)[:, None] >= jnp.arange(q1)[None, :]
        p = jax.nn.softmax(jnp.where(mask, logits, -jnp.inf), axis=-1).astype(v.dtype)
        outs.append(jnp.einsum('bhij,bjhe->bihe', p, v[:, :q1]))
    return jnp.concatenate(outs, axis=1)


def causal_dwconv(u, w, bias):
    s = u.shape[1]
    up = jnp.pad(u, ((0, 0), (CONV_W - 1, 0), (0, 0)))
    out = bias
    for j in range(CONV_W):
        out = out + w[j] * up[:, j:j + s]
    return out


def hybrid_layer(x, c_act, norm1_g, norm2_g, w_ada, b_ada, w_in, w_gla_a2, b_gla_a, b_fox_f,
                 ret_norm_g, gla_norm_g, q_norm_g, k_norm_g, w_br, w_mg, b_mg, w_o,
                 w_up, w_conv, b_conv, w_down):
    b, s = x.shape[0], x.shape[1]
    mod = c_act @ w_ada + b_ada
    shift1, scale1, gate1, shift2, scale2, gate2 = jnp.split(mod, 6, axis=-1)

    h = modulate(rms_norm(x, norm1_g), shift1, scale1)
    split_points = np.cumsum(IN_SPLITS)[:-1].tolist()
    (rq, rk, rv, rg, gq, gk, gv, glr, gg, fq, fk, fv, ff) = jnp.split(h @ w_in, split_points, axis=-1)

    def heads(t, d):
        return t.reshape(b, s, N_HEADS, d)

    ret = retention(heads(rq, RET_DK), heads(rk, RET_DK), heads(rv, RET_DV))
    ret = head_group_norm(ret, ret_norm_g.reshape(N_HEADS, RET_DV)).reshape(b, s, BRANCH_W)
    ret = jax.nn.silu(rg) * ret

    log_a = jax.nn.log_sigmoid((glr @ w_gla_a2 + b_gla_a).astype(F32)) / GLA_TAU
    gla_o = gla(heads(gq, GLA_DK), heads(gk, GLA_DK), heads(gv, GLA_DV), heads(log_a, GLA_DK))
    gla_o = rms_norm(gla_o, gla_norm_g).reshape(b, s, BRANCH_W)
    gla_o = jax.nn.silu(gg) * gla_o

    fox_o = forgetting_attention(rms_norm(heads(fq, FOX_D), q_norm_g),
                                 rms_norm(heads(fk, FOX_D), k_norm_g),
                                 heads(fv, FOX_D), ff + b_fox_f).reshape(b, s, BRANCH_W)

    branches = jnp.stack([ret, gla_o, fox_o], axis=2).astype(h.dtype)
    y_br = jnp.einsum('bsnw,nwd->bsnd', branches, w_br)
    gates = jax.nn.sigmoid(h @ w_mg + b_mg).reshape(b, s, N_BRANCH, D_MODEL)
    mixed = jnp.sum(gates * y_br, axis=2) @ w_o
    x = x + (gate1[:, None, :] * mixed).astype(x.dtype)

    h2 = modulate(rms_norm(x, norm2_g), shift2, scale2)
    u, g = jnp.split(h2 @ w_up, 2, axis=-1)
    u = causal_dwconv(u, w_conv, b_conv)
    y = (jax.nn.silu(u) * g) @ w_down
    x = x + (gate2[:, None, :] * y).astype(x.dtype)
    return x


def _fwd_setup_inputs(seed: int = 0) -> dict:
    key = jax.random.key(seed)
    ks = jax.random.split(key, 24)

    def nrm(k, shape, scale):
        return jax.random.normal(k, shape, F32) * scale

    L, D = DEPTH, D_MODEL
    return {
        'x': nrm(ks[0], (BATCH, SEQ, D), 1.0),
        'c': nrm(ks[1], (BATCH, D), 1.0),
        'norm1_g': 1.0 + nrm(ks[2], (L, D), 0.02),
        'norm2_g': 1.0 + nrm(ks[3], (L, D), 0.02),
        'w_ada': nrm(ks[4], (L, D, 6 * D), 0.5 * D ** -0.5),
        'b_ada': nrm(ks[5], (L, 6 * D), 0.02),
        'w_in': nrm(ks[6], (L, D, IN_W), D ** -0.5),
        'w_gla_a2': nrm(ks[7], (L, GLA_LOWRANK, N_HEADS * GLA_DK), GLA_LOWRANK ** -0.5),
        'b_gla_a': nrm(ks[8], (L, N_HEADS * GLA_DK), 0.1),
        'b_fox_f': 1.0 + nrm(ks[9], (L, N_HEADS), 0.1),
        'ret_norm_g': 1.0 + nrm(ks[10], (L, N_HEADS * RET_DV), 0.02),
        'gla_norm_g': 1.0 + nrm(ks[11], (L, GLA_DV), 0.02),
        'q_norm_g': 1.0 + nrm(ks[12], (L, FOX_D), 0.02),
        'k_norm_g': 1.0 + nrm(ks[13], (L, FOX_D), 0.02),
        'w_br': nrm(ks[14], (L, N_BRANCH, BRANCH_W, D), BRANCH_W ** -0.5),
        'w_mg': nrm(ks[15], (L, D, N_BRANCH * D), D ** -0.5),
        'b_mg': nrm(ks[16], (L, N_BRANCH * D), 0.02),
        'w_o': nrm(ks[17], (L, D, D), D ** -0.5),
        'w_up': nrm(ks[18], (L, D, 2 * D_FF), D ** -0.5),
        'w_conv': nrm(ks[19], (L, CONV_W, D_FF), CONV_W ** -0.5),
        'b_conv': nrm(ks[20], (L, D_FF), 0.02),
        'w_down': nrm(ks[21], (L, D_FF, D), D_FF ** -0.5),
    }


def _fwd_reference(x, c, norm1_g, norm2_g, w_ada, b_ada, w_in, w_gla_a2, b_gla_a, b_fox_f,
              ret_norm_g, gla_norm_g, q_norm_g, k_norm_g, w_br, w_mg, b_mg, w_o,
              w_up, w_conv, b_conv, w_down):
    c_act = jax.nn.silu(c)
    for l in range(DEPTH):
        x = hybrid_layer(x, c_act, norm1_g[l], norm2_g[l], w_ada[l], b_ada[l], w_in[l],
                         w_gla_a2[l], b_gla_a[l], b_fox_f[l], ret_norm_g[l], gla_norm_g[l],
                         q_norm_g[l], k_norm_g[l], w_br[l], w_mg[l], b_mg[l], w_o[l],
                         w_up[l], w_conv[l], b_conv[l], w_down[l])
    return x


import jax as _jax
import jax.numpy as _jnp

TWIN_FORMAT = 'train_step'
FWD_PARAMS = ['x', 'c', 'norm1_g', 'norm2_g', 'w_ada', 'b_ada', 'w_in', 'w_gla_a2', 'b_gla_a', 'b_fox_f', 'ret_norm_g', 'gla_norm_g', 'q_norm_g', 'k_norm_g', 'w_br', 'w_mg', 'b_mg', 'w_o', 'w_up', 'w_conv', 'b_conv', 'w_down']
TWIN_WEIGHTS = ['norm1_g', 'norm2_g', 'w_ada', 'b_ada', 'w_in', 'w_gla_a2', 'b_gla_a', 'b_fox_f', 'ret_norm_g', 'gla_norm_g', 'q_norm_g', 'k_norm_g', 'w_br', 'w_mg', 'b_mg', 'w_o', 'w_up', 'w_conv', 'b_conv', 'w_down']
TWIN_DIFF_INPUT = 'x'
TWIN_INPUTS = ['x', 'c', 'norm1_g', 'norm2_g', 'w_ada', 'b_ada', 'w_in', 'w_gla_a2', 'b_gla_a', 'b_fox_f', 'ret_norm_g', 'gla_norm_g', 'q_norm_g', 'k_norm_g', 'w_br', 'w_mg', 'b_mg', 'w_o', 'w_up', 'w_conv', 'b_conv', 'w_down', 'loss_target', 'm_norm1_g', 'm_norm2_g', 'm_w_ada', 'm_b_ada', 'm_w_in', 'm_w_gla_a2', 'm_b_gla_a', 'm_b_fox_f', 'm_ret_norm_g', 'm_gla_norm_g', 'm_q_norm_g', 'm_k_norm_g', 'm_w_br', 'm_w_mg', 'm_b_mg', 'm_w_o', 'm_w_up', 'm_w_conv', 'm_b_conv', 'm_w_down', 'v_norm1_g', 'v_norm2_g', 'v_w_ada', 'v_b_ada', 'v_w_in', 'v_w_gla_a2', 'v_b_gla_a', 'v_b_fox_f', 'v_ret_norm_g', 'v_gla_norm_g', 'v_q_norm_g', 'v_k_norm_g', 'v_w_br', 'v_w_mg', 'v_b_mg', 'v_w_o', 'v_w_up', 'v_w_conv', 'v_b_conv', 'v_w_down']
TWIN_OUTPUTS = ['loss', 'grad_x', 'grad_norm1_g', 'grad_norm2_g', 'grad_w_ada', 'grad_b_ada', 'grad_w_in', 'grad_w_gla_a2', 'grad_b_gla_a', 'grad_b_fox_f', 'grad_ret_norm_g', 'grad_gla_norm_g', 'grad_q_norm_g', 'grad_k_norm_g', 'grad_w_br', 'grad_w_mg', 'grad_b_mg', 'grad_w_o', 'grad_w_up', 'grad_w_conv', 'grad_b_conv', 'grad_w_down', 'delta_norm1_g', 'delta_norm2_g', 'delta_w_ada', 'delta_b_ada', 'delta_w_in', 'delta_w_gla_a2', 'delta_b_gla_a', 'delta_b_fox_f', 'delta_ret_norm_g', 'delta_gla_norm_g', 'delta_q_norm_g', 'delta_k_norm_g', 'delta_w_br', 'delta_w_mg', 'delta_b_mg', 'delta_w_o', 'delta_w_up', 'delta_w_conv', 'delta_b_conv', 'delta_w_down', 'new_m_norm1_g', 'new_m_norm2_g', 'new_m_w_ada', 'new_m_b_ada', 'new_m_w_in', 'new_m_w_gla_a2', 'new_m_b_gla_a', 'new_m_b_fox_f', 'new_m_ret_norm_g', 'new_m_gla_norm_g', 'new_m_q_norm_g', 'new_m_k_norm_g', 'new_m_w_br', 'new_m_w_mg', 'new_m_b_mg', 'new_m_w_o', 'new_m_w_up', 'new_m_w_conv', 'new_m_b_conv', 'new_m_w_down', 'new_v_norm1_g', 'new_v_norm2_g', 'new_v_w_ada', 'new_v_b_ada', 'new_v_w_in', 'new_v_w_gla_a2', 'new_v_b_gla_a', 'new_v_b_fox_f', 'new_v_ret_norm_g', 'new_v_gla_norm_g', 'new_v_q_norm_g', 'new_v_k_norm_g', 'new_v_w_br', 'new_v_w_mg', 'new_v_b_mg', 'new_v_w_o', 'new_v_w_up', 'new_v_w_conv', 'new_v_b_conv', 'new_v_w_down']
TWIN_LEAF_KINDS = {'loss': 'loss', 'grad_x': 'grad_x', 'grad_norm1_g': 'grad_w', 'grad_norm2_g': 'grad_w', 'grad_w_ada': 'grad_w', 'grad_b_ada': 'grad_w', 'grad_w_in': 'grad_w', 'grad_w_gla_a2': 'grad_w', 'grad_b_gla_a': 'grad_w', 'grad_b_fox_f': 'grad_w', 'grad_ret_norm_g': 'grad_w', 'grad_gla_norm_g': 'grad_w', 'grad_q_norm_g': 'grad_w', 'grad_k_norm_g': 'grad_w', 'grad_w_br': 'grad_w', 'grad_w_mg': 'grad_w', 'grad_b_mg': 'grad_w', 'grad_w_o': 'grad_w', 'grad_w_up': 'grad_w', 'grad_w_conv': 'grad_w', 'grad_b_conv': 'grad_w', 'grad_w_down': 'grad_w', 'delta_norm1_g': 'delta_w', 'delta_norm2_g': 'delta_w', 'delta_w_ada': 'delta_w', 'delta_b_ada': 'delta_w', 'delta_w_in': 'delta_w', 'delta_w_gla_a2': 'delta_w', 'delta_b_gla_a': 'delta_w', 'delta_b_fox_f': 'delta_w', 'delta_ret_norm_g': 'delta_w', 'delta_gla_norm_g': 'delta_w', 'delta_q_norm_g': 'delta_w', 'delta_k_norm_g': 'delta_w', 'delta_w_br': 'delta_w', 'delta_w_mg': 'delta_w', 'delta_b_mg': 'delta_w', 'delta_w_o': 'delta_w', 'delta_w_up': 'delta_w', 'delta_w_conv': 'delta_w', 'delta_b_conv': 'delta_w', 'delta_w_down': 'delta_w', 'new_m_norm1_g': 'new_m', 'new_m_norm2_g': 'new_m', 'new_m_w_ada': 'new_m', 'new_m_b_ada': 'new_m', 'new_m_w_in': 'new_m', 'new_m_w_gla_a2': 'new_m', 'new_m_b_gla_a': 'new_m', 'new_m_b_fox_f': 'new_m', 'new_m_ret_norm_g': 'new_m', 'new_m_gla_norm_g': 'new_m', 'new_m_q_norm_g': 'new_m', 'new_m_k_norm_g': 'new_m', 'new_m_w_br': 'new_m', 'new_m_w_mg': 'new_m', 'new_m_b_mg': 'new_m', 'new_m_w_o': 'new_m', 'new_m_w_up': 'new_m', 'new_m_w_conv': 'new_m', 'new_m_b_conv': 'new_m', 'new_m_w_down': 'new_m', 'new_v_norm1_g': 'new_v', 'new_v_norm2_g': 'new_v', 'new_v_w_ada': 'new_v', 'new_v_b_ada': 'new_v', 'new_v_w_in': 'new_v', 'new_v_w_gla_a2': 'new_v', 'new_v_b_gla_a': 'new_v', 'new_v_b_fox_f': 'new_v', 'new_v_ret_norm_g': 'new_v', 'new_v_gla_norm_g': 'new_v', 'new_v_q_norm_g': 'new_v', 'new_v_k_norm_g': 'new_v', 'new_v_w_br': 'new_v', 'new_v_w_mg': 'new_v', 'new_v_b_mg': 'new_v', 'new_v_w_o': 'new_v', 'new_v_w_up': 'new_v', 'new_v_w_conv': 'new_v', 'new_v_b_conv': 'new_v', 'new_v_w_down': 'new_v'}


def _forward(args):
    return _fwd_reference(*[args[k] for k in FWD_PARAMS])


def _output_shape():
    out = _jax.eval_shape(lambda: _forward(_fwd_setup_inputs(0)))
    return out.shape, out.dtype

N_MICROBATCH = 1
ADAM_LR = 0.001
ADAM_B1 = 0.9
ADAM_B2 = 0.999
ADAM_EPS = 1e-08
ADAM_WD = 0.01
ADAM_STEP = 10
PER_EXAMPLE_BATCH_AXIS = {'x': 0, 'c': 0, 'loss_target': 0}
SHARED_INPUTS = []
_WEIGHT_DTYPES = {'norm1_g': _jnp.float32, 'norm2_g': _jnp.float32, 'w_ada': _jnp.float32, 'b_ada': _jnp.float32, 'w_in': _jnp.float32, 'w_gla_a2': _jnp.float32, 'b_gla_a': _jnp.float32, 'b_fox_f': _jnp.float32, 'ret_norm_g': _jnp.float32, 'gla_norm_g': _jnp.float32, 'q_norm_g': _jnp.float32, 'k_norm_g': _jnp.float32, 'w_br': _jnp.float32, 'w_mg': _jnp.float32, 'b_mg': _jnp.float32, 'w_o': _jnp.float32, 'w_up': _jnp.float32, 'w_conv': _jnp.float32, 'b_conv': _jnp.float32, 'w_down': _jnp.float32}
MOMENT_SCALE = {'norm1_g': 6.640551e-01, 'norm2_g': 1.718923e+00, 'w_ada': 5.009120e-01, 'b_ada': 1.085426e+00, 'w_in': 6.369768e-02, 'w_gla_a2': 1.116629e-02, 'b_gla_a': 3.095998e-02, 'b_fox_f': 8.654185e+00, 'ret_norm_g': 4.121928e-01, 'gla_norm_g': 1.601625e+00, 'q_norm_g': 2.355477e-01, 'k_norm_g': 2.366331e-01, 'w_br': 4.375415e-02, 'w_mg': 1.471065e-02, 'b_mg': 6.519418e-02, 'w_o': 7.458424e-02, 'w_up': 5.399354e-02, 'w_conv': 2.066458e-01, 'b_conv': 2.264912e-01, 'w_down': 6.568705e-02}


def _to_microbatches(a, axis):
    t = _jnp.moveaxis(a, axis, 0)
    t = t.reshape((N_MICROBATCH, t.shape[0] // N_MICROBATCH) + t.shape[1:])
    return _jnp.moveaxis(t, 1, axis + 1)


def setup_inputs(seed: int = 0) -> dict:
    inp = _fwd_setup_inputs(seed)
    key = _jax.random.fold_in(_jax.random.key(seed), 7919)
    shape, _ = _output_shape()
    out = dict(inp)
    out["loss_target"] = _jax.random.normal(_jax.random.fold_in(key, 0), shape, _jnp.float32)
    for i, name in enumerate(TWIN_WEIGHTS):
        w = inp[name].astype(_jnp.float32)
        if MOMENT_SCALE is None:
            s = _jnp.sqrt(_jnp.mean(_jnp.square(w)) + 1e-30)
        else:
            s = MOMENT_SCALE[name]
        km, kv = _jax.random.split(_jax.random.fold_in(key, i + 1))
        out[name] = w
        out["m_" + name] = s * _jax.random.normal(km, w.shape, _jnp.float32)
        out["v_" + name] = (s * s) * _jax.random.uniform(kv, w.shape, _jnp.float32, 0.5, 1.5)
    if N_MICROBATCH > 1:
        for name, axis in PER_EXAMPLE_BATCH_AXIS.items():
            out[name] = _to_microbatches(out[name], axis)
    return {'x': out['x'], 'c': out['c'], 'norm1_g': out['norm1_g'], 'norm2_g': out['norm2_g'], 'w_ada': out['w_ada'], 'b_ada': out['b_ada'], 'w_in': out['w_in'], 'w_gla_a2': out['w_gla_a2'], 'b_gla_a': out['b_gla_a'], 'b_fox_f': out['b_fox_f'], 'ret_norm_g': out['ret_norm_g'], 'gla_norm_g': out['gla_norm_g'], 'q_norm_g': out['q_norm_g'], 'k_norm_g': out['k_norm_g'], 'w_br': out['w_br'], 'w_mg': out['w_mg'], 'b_mg': out['b_mg'], 'w_o': out['w_o'], 'w_up': out['w_up'], 'w_conv': out['w_conv'], 'b_conv': out['b_conv'], 'w_down': out['w_down'], 'loss_target': out['loss_target'], 'm_norm1_g': out['m_norm1_g'], 'm_norm2_g': out['m_norm2_g'], 'm_w_ada': out['m_w_ada'], 'm_b_ada': out['m_b_ada'], 'm_w_in': out['m_w_in'], 'm_w_gla_a2': out['m_w_gla_a2'], 'm_b_gla_a': out['m_b_gla_a'], 'm_b_fox_f': out['m_b_fox_f'], 'm_ret_norm_g': out['m_ret_norm_g'], 'm_gla_norm_g': out['m_gla_norm_g'], 'm_q_norm_g': out['m_q_norm_g'], 'm_k_norm_g': out['m_k_norm_g'], 'm_w_br': out['m_w_br'], 'm_w_mg': out['m_w_mg'], 'm_b_mg': out['m_b_mg'], 'm_w_o': out['m_w_o'], 'm_w_up': out['m_w_up'], 'm_w_conv': out['m_w_conv'], 'm_b_conv': out['m_b_conv'], 'm_w_down': out['m_w_down'], 'v_norm1_g': out['v_norm1_g'], 'v_norm2_g': out['v_norm2_g'], 'v_w_ada': out['v_w_ada'], 'v_b_ada': out['v_b_ada'], 'v_w_in': out['v_w_in'], 'v_w_gla_a2': out['v_w_gla_a2'], 'v_b_gla_a': out['v_b_gla_a'], 'v_b_fox_f': out['v_b_fox_f'], 'v_ret_norm_g': out['v_ret_norm_g'], 'v_gla_norm_g': out['v_gla_norm_g'], 'v_q_norm_g': out['v_q_norm_g'], 'v_k_norm_g': out['v_k_norm_g'], 'v_w_br': out['v_w_br'], 'v_w_mg': out['v_w_mg'], 'v_b_mg': out['v_b_mg'], 'v_w_o': out['v_w_o'], 'v_w_up': out['v_w_up'], 'v_w_conv': out['v_w_conv'], 'v_b_conv': out['v_b_conv'], 'v_w_down': out['v_w_down']}


def _loss(weights, diff, rest, loss_target):
    with _jax.named_scope("forward"):
        args = {**rest, TWIN_DIFF_INPUT: diff, **{k: w.astype(_WEIGHT_DTYPES[k]) for k, w in weights.items()}}
        y = _forward(args)
    with _jax.named_scope("loss_head"):
        err = _jnp.square(y.astype(_jnp.float32) - loss_target)
        return 0.5 * _jnp.sum(_jnp.mean(err, axis=-1)) if err.ndim else 0.5 * err


def _adamw(w, g, m, v):
    m = ADAM_B1 * m + (1.0 - ADAM_B1) * g
    v = ADAM_B2 * v + (1.0 - ADAM_B2) * _jnp.square(g)
    m_hat = m / (1.0 - ADAM_B1 ** ADAM_STEP)
    v_hat = v / (1.0 - ADAM_B2 ** ADAM_STEP)
    delta = -ADAM_LR * (m_hat / (_jnp.sqrt(v_hat) + ADAM_EPS) + ADAM_WD * w)
    return delta, m, v


def reference(x, c, norm1_g, norm2_g, w_ada, b_ada, w_in, w_gla_a2, b_gla_a, b_fox_f, ret_norm_g, gla_norm_g, q_norm_g, k_norm_g, w_br, w_mg, b_mg, w_o, w_up, w_conv, b_conv, w_down, loss_target, m_norm1_g, m_norm2_g, m_w_ada, m_b_ada, m_w_in, m_w_gla_a2, m_b_gla_a, m_b_fox_f, m_ret_norm_g, m_gla_norm_g, m_q_norm_g, m_k_norm_g, m_w_br, m_w_mg, m_b_mg, m_w_o, m_w_up, m_w_conv, m_b_conv, m_w_down, v_norm1_g, v_norm2_g, v_w_ada, v_b_ada, v_w_in, v_w_gla_a2, v_b_gla_a, v_b_fox_f, v_ret_norm_g, v_gla_norm_g, v_q_norm_g, v_k_norm_g, v_w_br, v_w_mg, v_b_mg, v_w_o, v_w_up, v_w_conv, v_b_conv, v_w_down):
    given = dict(x=x, c=c, norm1_g=norm1_g, norm2_g=norm2_g, w_ada=w_ada, b_ada=b_ada, w_in=w_in, w_gla_a2=w_gla_a2, b_gla_a=b_gla_a, b_fox_f=b_fox_f, ret_norm_g=ret_norm_g, gla_norm_g=gla_norm_g, q_norm_g=q_norm_g, k_norm_g=k_norm_g, w_br=w_br, w_mg=w_mg, b_mg=b_mg, w_o=w_o, w_up=w_up, w_conv=w_conv, b_conv=b_conv, w_down=w_down, loss_target=loss_target, m_norm1_g=m_norm1_g, m_norm2_g=m_norm2_g, m_w_ada=m_w_ada, m_b_ada=m_b_ada, m_w_in=m_w_in, m_w_gla_a2=m_w_gla_a2, m_b_gla_a=m_b_gla_a, m_b_fox_f=m_b_fox_f, m_ret_norm_g=m_ret_norm_g, m_gla_norm_g=m_gla_norm_g, m_q_norm_g=m_q_norm_g, m_k_norm_g=m_k_norm_g, m_w_br=m_w_br, m_w_mg=m_w_mg, m_b_mg=m_b_mg, m_w_o=m_w_o, m_w_up=m_w_up, m_w_conv=m_w_conv, m_b_conv=m_b_conv, m_w_down=m_w_down, v_norm1_g=v_norm1_g, v_norm2_g=v_norm2_g, v_w_ada=v_w_ada, v_b_ada=v_b_ada, v_w_in=v_w_in, v_w_gla_a2=v_w_gla_a2, v_b_gla_a=v_b_gla_a, v_b_fox_f=v_b_fox_f, v_ret_norm_g=v_ret_norm_g, v_gla_norm_g=v_gla_norm_g, v_q_norm_g=v_q_norm_g, v_k_norm_g=v_k_norm_g, v_w_br=v_w_br, v_w_mg=v_w_mg, v_b_mg=v_b_mg, v_w_o=v_w_o, v_w_up=v_w_up, v_w_conv=v_w_conv, v_b_conv=v_b_conv, v_w_down=v_w_down)
    weights = {n: given[n] for n in TWIN_WEIGHTS}
    shared = {n: given[n] for n in SHARED_INPUTS}
    per_example = {n: given[n] for n in ['x', 'c']}
    grad_fn = _jax.value_and_grad(_loss, argnums=(0, 1))

    def one_microbatch(ex, loss_target):
        ex = dict(ex)
        diff = ex.pop(TWIN_DIFF_INPUT)
        return grad_fn(weights, diff, {**shared, **ex}, loss_target)

    if N_MICROBATCH == 1:
        loss, (grad_w, grad_x) = one_microbatch(per_example, given["loss_target"])
    else:
        def body(carry, xs):
            loss_sum, grad_sum = carry
            l_k, (gw_k, gx_k) = one_microbatch(xs[0], xs[1])
            with _jax.named_scope("update"):
                return (loss_sum + l_k, _jax.tree.map(_jnp.add, grad_sum, gw_k)), gx_k

        init = (_jnp.zeros((), _jnp.float32), _jax.tree.map(_jnp.zeros_like, weights))
        (loss, grad_w), grad_x = _jax.lax.scan(body, init, (per_example, given["loss_target"]))
    with _jax.named_scope("update"):
        delta_w, new_m, new_v = {}, {}, {}
        for n in TWIN_WEIGHTS:
            delta_w[n], new_m[n], new_v[n] = _adamw(weights[n], grad_w[n], given["m_" + n], given["v_" + n])
    return (loss, grad_x, *[grad_w[n] for n in TWIN_WEIGHTS], *[delta_w[n] for n in TWIN_WEIGHTS],
            *[new_m[n] for n in TWIN_WEIGHTS], *[new_v[n] for n in TWIN_WEIGHTS])
```

```python
import functools

import numpy as np
import jax
import jax.numpy as jnp
from jax import lax
from jax.experimental import pallas as pl
from jax.experimental.pallas import tpu as pltpu

F32 = jnp.float32
BF16 = jnp.bfloat16

N_DEV = 8
T = 2048
D = 1024
DEPTH = 4
N_HEADS = 4
HD = 128
BW = 512
D_FF = 2816
CHUNK = 64
EPS = 1e-6
IN_W = 5140
NP = 5632
TAIL0 = 5120
LR_LANES = 16
FF_LANE0 = 16
PACK_W = 1024
SEG_ROWS = (704, 128, 352, 192, 384, 704)
LAYER_ROWS = sum(SEG_ROWS)
VMEM_LIMIT_V7X = 56 * 1024 * 1024

ADAM_LR, ADAM_B1, ADAM_B2, ADAM_EPS, ADAM_WD, ADAM_STEP = 0.001, 0.9, 0.999, 1e-08, 0.01, 10

MESH_ID = pl.DeviceIdType.MESH


def _cp(*sem):
    return pltpu.CompilerParams(dimension_semantics=sem if sem else None, vmem_limit_bytes=VMEM_LIMIT_V7X)


def _sigmoid(z):
    return 1.0 / (1.0 + jnp.exp(-z))


def _log_sigmoid(z):
    return jnp.minimum(z, 0.0) - jnp.log(1.0 + jnp.exp(-jnp.abs(z)))


def _sum0(a):
    return jnp.sum(a, axis=0, keepdims=True)


def _mean1(a):
    return jnp.mean(a, axis=-1, keepdims=True)


def _dot(a, b, dims):
    return lax.dot_general(a.astype(BF16), b.astype(BF16), (dims, ((), ())), preferred_element_type=F32)


NN = ((1,), (0,))
NT = ((1,), (1,))
TN = ((0,), (0,))


def _exact_dot(m01, a):
    a1 = a.astype(BF16)
    r1 = a - a1.astype(F32)
    a2 = r1.astype(BF16)
    a3 = (r1 - a2.astype(F32)).astype(BF16)
    d = lambda z: jnp.dot(m01, z, preferred_element_type=F32)
    return d(a1) + d(a2) + d(a3)


def _tri(n, upper):
    r = lax.broadcasted_iota(jnp.int32, (n, n), 0)
    c = lax.broadcasted_iota(jnp.int32, (n, n), 1)
    return jnp.where((c >= r) if upper else (c <= r), 1.0, 0.0).astype(BF16)


def _exchange(x, gather, name):
    blk = x.shape if gather else x.shape[1:]

    def body(x_ref, o_ref, send_sems, recv_sems, loc_sem):
        mx, my, mc = lax.axis_index("x"), lax.axis_index("y"), lax.axis_index("c")
        me = 4 * mx + 2 * my + mc
        loc = pltpu.make_async_copy(x_ref if gather else x_ref.at[me], o_ref.at[me], loc_sem)
        loc.start()
        copies = []
        for k in range(1, N_DEV):
            px = mx ^ (k >> 2) if (k >> 2) else mx
            py = my ^ ((k >> 1) & 1) if ((k >> 1) & 1) else my
            pc = mc ^ (k & 1) if (k & 1) else mc
            peer = 4 * px + 2 * py + pc
            cp = pltpu.make_async_remote_copy(
                src_ref=x_ref if gather else x_ref.at[peer], dst_ref=o_ref.at[me],
                send_sem=send_sems.at[k - 1], recv_sem=recv_sems.at[k - 1],
                device_id=(px, py, pc), device_id_type=MESH_ID)
            cp.start()
            copies.append(cp)
        for cp in copies:
            cp.wait()
        loc.wait()

    return pl.pallas_call(
        body, name=name,
        out_shape=jax.ShapeDtypeStruct((N_DEV,) + tuple(blk), x.dtype),
        in_specs=[pl.BlockSpec(memory_space=pl.ANY)],
        out_specs=pl.BlockSpec(memory_space=pl.ANY),
        scratch_shapes=[pltpu.SemaphoreType.DMA((N_DEV - 1,)), pltpu.SemaphoreType.DMA((N_DEV - 1,)),
                        pltpu.SemaphoreType.DMA],
        compiler_params=pltpu.CompilerParams(has_side_effects=True),
    )(x)


def _matmul(a, b, mode, name, out_dtype=F32, tm=1024, tn=512, tk=None, add=None):
    if mode == "tn":
        K, M = a.shape
        N = b.shape[1]
    else:
        M, K = a.shape
        N = b.shape[0] if mode == "nt" else b.shape[1]
    tm, tn = min(tm, M), min(tn, N)
    tk = K if tk is None else tk
    nk = K // tk
    assert M % tm == 0 and N % tn == 0 and K % tk == 0, (name, M, N, K, tm, tn, tk)
    dims = {"nn": NN, "nt": NT, "tn": TN}[mode]
    has_add = add is not None

    def body(*refs):
        if has_add:
            a_ref, b_ref, add_ref, o_ref, acc_ref = refs
        else:
            a_ref, b_ref, o_ref, acc_ref = refs
        k = pl.program_id(2)
        part = _dot(a_ref[...], b_ref[...], dims)

        @pl.when(k == 0)
        def _():
            acc_ref[...] = part

        @pl.when(k > 0)
        def _():
            acc_ref[...] += part

        @pl.when(k == nk - 1)
        def _():
            r = acc_ref[...]
            if has_add:
                r = r + add_ref[...]
            o_ref[...] = r.astype(o_ref.dtype)

    if mode == "tn":
        a_spec = pl.BlockSpec((tk, tm), lambda i, j, k: (k, i))
    else:
        a_spec = pl.BlockSpec((tm, tk), lambda i, j, k: (i, k))
    if mode == "nt":
        b_spec = pl.BlockSpec((tn, tk), lambda i, j, k: (j, k))
    else:
        b_spec = pl.BlockSpec((tk, tn), lambda i, j, k: (k, j))
    o_spec = pl.BlockSpec((tm, tn), lambda i, j, k: (i, j))
    in_specs = [a_spec, b_spec] + ([o_spec] if has_add else [])
    args = (a, b) + ((add,) if has_add else ())
    return pl.pallas_call(
        body, name=name, grid=(M // tm, N // tn, nk),
        out_shape=jax.ShapeDtypeStruct((M, N), out_dtype),
        in_specs=in_specs, out_specs=o_spec,
        scratch_shapes=[pltpu.VMEM((tm, tn), F32)],
        compiler_params=_cp("parallel", "parallel", "arbitrary"),
    )(*args)


def _ada_fwd(c_all, w_ada, b_loc):
    n = w_ada.shape[2]

    def body(c_ref, w_ref, b_ref, o_ref):
        c = c_ref[...]
        o_ref[0] = _dot(c * _sigmoid(c), w_ref[0], NN) + b_ref[0]

    return pl.pallas_call(
        body, name="ada_fwd", grid=(DEPTH,),
        out_shape=jax.ShapeDtypeStruct((DEPTH, N_DEV, n), F32),
        in_specs=[pl.BlockSpec((N_DEV, D), lambda l: (0, 0)),
                  pl.BlockSpec((1, D, n), lambda l: (l, 0, 0)),
                  pl.BlockSpec((1, 1, n), lambda l: (l, 0, 0))],
        out_specs=pl.BlockSpec((1, N_DEV, n), lambda l: (l, 0, 0)),
        compiler_params=_cp("parallel"),
    )(c_all, w_ada, b_loc)


def _ada_bwd(c_all, dmod_all):
    n = dmod_all.shape[2]

    def body(c_ref, d_ref, o_ref):
        c = c_ref[...]
        o_ref[0] = _dot(c * _sigmoid(c), d_ref[0], TN)

    return pl.pallas_call(
        body, name="ada_bwd", grid=(DEPTH,),
        out_shape=jax.ShapeDtypeStruct((DEPTH, D, n), F32),
        in_specs=[pl.BlockSpec((N_DEV, D), lambda l: (0, 0)),
                  pl.BlockSpec((1, N_DEV, n), lambda l: (l, 0, 0))],
        out_specs=pl.BlockSpec((1, D, n), lambda l: (l, 0, 0)),
        compiler_params=_cp("parallel"),
    )(c_all, dmod_all)


ROW_TILE = 256


def _row_spec(w=D, col=0):
    return pl.BlockSpec((ROW_TILE, w), lambda i: (i, col))


def _vec_spec(w=D):
    return pl.BlockSpec((1, w), lambda i: (0, 0))


def _norm_fwd(x, g, scale, shift, name, m=None, gate=None):
    has_res = m is not None

    def body(*refs):
        if has_res:
            x_ref, m_ref, gate_ref, g_ref, sc_ref, sh_ref, xo_ref, h_ref = refs
            xv = x_ref[...] + gate_ref[...] * m_ref[...]
            xo_ref[...] = xv
        else:
            x_ref, g_ref, sc_ref, sh_ref, h_ref = refs
            xv = x_ref[...]
        r = lax.rsqrt(_mean1(xv * xv) + EPS)
        h_ref[...] = ((xv * r * g_ref[...]) * (1.0 + sc_ref[...]) + sh_ref[...]).astype(BF16)

    ins = [x] + ([m, gate] if has_res else []) + [g, scale, shift]
    in_specs = [_row_spec()] + ([_row_spec(), _vec_spec()] if has_res else []) + [_vec_spec()] * 3
    out_shape = [jax.ShapeDtypeStruct((T, D), BF16)]
    out_specs = [_row_spec()]
    if has_res:
        out_shape = [jax.ShapeDtypeStruct((T, D), F32)] + out_shape
        out_specs = [_row_spec()] + out_specs
    out = pl.pallas_call(body, name=name, grid=(T // ROW_TILE,), out_shape=out_shape, in_specs=in_specs,
                         out_specs=out_specs, compiler_params=_cp("parallel"))(*ins)
    return out if has_res else out[0]


def _norm_bwd(x, dh, dres, g, scale, shift, name):
    def body(x_ref, dh_ref, dres_ref, g_ref, sc_ref, sh_ref, dx_ref, st_ref):
        xv, dh_v, gv = x_ref[...], dh_ref[...], g_ref[...]
        r = lax.rsqrt(_mean1(xv * xv) + EPS)
        n = xv * r
        dy = dh_v * (1.0 + sc_ref[...])
        dn = dy * gv
        dx_ref[...] = r * (dn - n * _mean1(dn * n)) + dres_ref[...]

        @pl.when(pl.program_id(0) == 0)
        def _():
            st_ref[...] = jnp.zeros_like(st_ref)

        st_ref[0:1, :] += _sum0(dy * n)
        st_ref[1:2, :] += _sum0(dh_v * (n * gv))
        st_ref[2:3, :] += _sum0(dh_v)

    return pl.pallas_call(
        body, name=name, grid=(T // ROW_TILE,),
        out_shape=[jax.ShapeDtypeStruct((T, D), F32), jax.ShapeDtypeStruct((8, D), F32)],
        in_specs=[_row_spec(), _row_spec(), _row_spec(), _vec_spec(), _vec_spec(), _vec_spec()],
        out_specs=[_row_spec(), pl.BlockSpec((8, D), lambda i: (0, 0))],
        compiler_params=_cp("arbitrary"),
    )(x, dh, dres, g, scale, shift)


def _axpy(x, m, gate, name):
    def body(x_ref, m_ref, gate_ref, o_ref):
        o_ref[...] = x_ref[...] + gate_ref[...] * m_ref[...]

    return pl.pallas_call(
        body, name=name, grid=(T // ROW_TILE,), out_shape=jax.ShapeDtypeStruct((T, D), F32),
        in_specs=[_row_spec(), _row_spec(), _vec_spec()], out_specs=_row_spec(),
        compiler_params=_cp("parallel"))(x, m, gate)


def _gate_bwd(dx, m, gate, name):
    def body(dx_ref, m_ref, gate_ref, dm_ref, st_ref):
        dxv = dx_ref[...]
        dm_ref[...] = (gate_ref[...] * dxv).astype(BF16)

        @pl.when(pl.program_id(0) == 0)
        def _():
            st_ref[...] = jnp.zeros_like(st_ref)

        st_ref[0:1, :] += _sum0(dxv * m_ref[...])

    return pl.pallas_call(
        body, name=name, grid=(T // ROW_TILE,),
        out_shape=[jax.ShapeDtypeStruct((T, D), BF16), jax.ShapeDtypeStruct((8, D), F32)],
        in_specs=[_row_spec(), _row_spec(), _vec_spec()],
        out_specs=[_row_spec(), pl.BlockSpec((8, D), lambda i: (0, 0))],
        compiler_params=_cp("arbitrary"))(dx, m, gate)


def _loss_fwd_bwd(y, target):
    def body(y_ref, t_ref, l_ref, d_ref):
        e = y_ref[...] - t_ref[...]
        d_ref[...] = e * (1.0 / D)

        @pl.when(pl.program_id(0) == 0)
        def _():
            l_ref[...] = jnp.zeros_like(l_ref)

        l_ref[...] += jnp.sum(_sum0(e * e), axis=1, keepdims=True) * (0.5 / D)

    return pl.pallas_call(
        body, name="loss", grid=(T // ROW_TILE,),
        out_shape=[jax.ShapeDtypeStruct((8, 128), F32), jax.ShapeDtypeStruct((T, D), F32)],
        in_specs=[_row_spec(), _row_spec()],
        out_specs=[pl.BlockSpec((8, 128), lambda i: (0, 0)), _row_spec()],
        compiler_params=_cp("arbitrary"))(y, target)


def _rope_tables():
    half = HD // 2
    inv_freq = 10000.0 ** (-jnp.arange(half, dtype=F32) / half)
    ang = jnp.arange(T, dtype=F32)[:, None] * inv_freq[None, :]
    cos, sin = jnp.cos(ang), jnp.sin(ang)
    return jnp.concatenate([cos, cos], axis=1), jnp.concatenate([-sin, sin], axis=1)


def _rope_fwd(p, cosf, sinf):
    def body(p_ref, c_ref, s_ref, o_ref):
        xv = p_ref[...]
        rot = xv * c_ref[...] + pltpu.roll(xv, HD // 2, 1) * s_ref[...]
        sc = jnp.where(pl.program_id(1) >= N_HEADS, HD ** -0.5, 1.0)
        o_ref[...] = (rot * sc).astype(BF16)

    blk = lambda: pl.BlockSpec((ROW_TILE, HD), lambda i, j: (i, j))
    tab = lambda: pl.BlockSpec((ROW_TILE, HD), lambda i, j: (i, 0))
    return pl.pallas_call(
        body, name="rope_fwd", grid=(T // ROW_TILE, 2 * N_HEADS),
        out_shape=jax.ShapeDtypeStruct((T, 2 * BW), BF16),
        in_specs=[blk(), tab(), tab()], out_specs=blk(),
        compiler_params=_cp("parallel", "parallel"))(p, cosf, sinf)


def _rope_bwd(dq, dk, cosf, sinf):
    def body(dq_ref, dk_ref, c_ref, s_ref, o_ref):
        is_k = pl.program_id(1) >= N_HEADS
        d = jnp.where(is_k, dk_ref[...] * (HD ** -0.5), dq_ref[...])
        o_ref[...] = d * c_ref[...] + pltpu.roll(d * s_ref[...], HD // 2, 1)

    tab = lambda: pl.BlockSpec((ROW_TILE, HD), lambda i, j: (i, 0))
    return pl.pallas_call(
        body, name="rope_bwd", grid=(T // ROW_TILE, 2 * N_HEADS),
        out_shape=jax.ShapeDtypeStruct((T, 2 * BW), F32),
        in_specs=[pl.BlockSpec((ROW_TILE, HD), lambda i, j: (i, jnp.minimum(j, N_HEADS - 1))),
                  pl.BlockSpec((ROW_TILE, HD), lambda i, j: (i, jnp.maximum(j - N_HEADS, 0))),
                  tab(), tab()],
        out_specs=pl.BlockSpec((ROW_TILE, HD), lambda i, j: (i, j)),
        compiler_params=_cp("parallel", "parallel"))(dq, dk, cosf, sinf)


TQ = 256
V_RET_BLK = 8


def _ret_logg():
    lg = jnp.log1p(-jnp.exp2(-5.0 - jnp.arange(N_HEADS, dtype=F32)))
    return jnp.broadcast_to(lg[:, None, None], (N_HEADS, 1, 128))


def _ret_weight(lg_ref, i):
    rows = lax.broadcasted_iota(jnp.int32, (TQ, T), 0) + i * TQ
    cols = lax.broadcasted_iota(jnp.int32, (TQ, T), 1)
    dist = jnp.abs(rows - cols).astype(F32)
    w = jnp.exp(dist * lg_ref[0][:, 0:1])
    return jnp.where((cols >> 6) <= (rows >> 6), w, 0.0)


def _ret_specs():
    q_spec = pl.BlockSpec((TQ, HD), lambda h, i: (i, h))
    k_spec = pl.BlockSpec((T, HD), lambda h, i: (0, N_HEADS + h))
    v_spec = pl.BlockSpec((T, HD), lambda h, i: (0, V_RET_BLK + h))
    lg_spec = pl.BlockSpec((1, 1, 128), lambda h, i: (h, 0, 0))
    return q_spec, k_spec, v_spec, lg_spec


def _ret_fwd(qk, p, logg):
    def body(q_ref, k_ref, v_ref, lg_ref, o_ref):
        w = _ret_weight(lg_ref, pl.program_id(1))
        s = _dot(q_ref[...], k_ref[...], NT) * w
        o_ref[...] = _dot(s, v_ref[...], NN)

    q_spec, k_spec, v_spec, lg_spec = _ret_specs()
    return pl.pallas_call(
        body, name="ret_fwd", grid=(N_HEADS, T // TQ),
        out_shape=jax.ShapeDtypeStruct((T, BW), F32),
        in_specs=[q_spec, k_spec, v_spec, lg_spec], out_specs=q_spec,
        compiler_params=_cp("parallel", "parallel"))(qk, qk, p, logg)


def _ret_bwd(qk, p, logg, do):
    def body(q_ref, k_ref, v_ref, lg_ref, do_ref, dq_ref, dk_ref, dv_ref):
        w = _ret_weight(lg_ref, pl.program_id(1))
        q, k, dov = q_ref[...], k_ref[...], do_ref[...]
        s = _dot(q, k, NT) * w
        ds = _dot(dov, v_ref[...], NT) * w
        dq_ref[...] = _dot(ds, k, NN)

        @pl.when(pl.program_id(1) == 0)
        def _():
            dk_ref[...] = jnp.zeros_like(dk_ref)
            dv_ref[...] = jnp.zeros_like(dv_ref)

        dk_ref[...] += _dot(ds, q, TN)
        dv_ref[...] += _dot(s, dov, TN)

    q_spec, k_spec, v_spec, lg_spec = _ret_specs()
    acc_spec = pl.BlockSpec((T, HD), lambda h, i: (0, h))
    sh = jax.ShapeDtypeStruct((T, BW), F32)
    return pl.pallas_call(
        body, name="ret_bwd", grid=(N_HEADS, T // TQ),
        out_shape=[sh, sh, sh],
        in_specs=[q_spec, k_spec, v_spec, lg_spec, q_spec], out_specs=[q_spec, acc_spec, acc_spec],
        compiler_params=_cp("parallel", "arbitrary"))(qk, qk, p, logg, do)


def _post_norm(xv, gv, centered):
    if centered:
        xv = xv - _mean1(xv)
    r = lax.rsqrt(_mean1(xv * xv) + EPS)
    return xv * r, r


def _branch_post_fwd(raw, p, g, gate_blk, centered, name):
    def body(raw_ref, z_ref, g_ref, o_ref):
        xh, _ = _post_norm(raw_ref[...], g_ref[...], centered)
        z = z_ref[...]
        o_ref[...] = (z * _sigmoid(z) * (xh * g_ref[...])).astype(BF16)

    blk = lambda: pl.BlockSpec((ROW_TILE, HD), lambda h, i: (i, h))
    g_spec = pl.BlockSpec((1, HD), (lambda h, i: (0, h)) if centered else (lambda h, i: (0, 0)))
    return pl.pallas_call(
        body, name=name, grid=(N_HEADS, T // ROW_TILE),
        out_shape=jax.ShapeDtypeStruct((T, BW), BF16),
        in_specs=[blk(), pl.BlockSpec((ROW_TILE, HD), lambda h, i: (i, gate_blk + h)), g_spec],
        out_specs=blk(), compiler_params=_cp("parallel", "parallel"))(raw, p, g)


def _branch_post_bwd(raw, p, g, dout, gate_blk, centered, name):
    def body(raw_ref, z_ref, g_ref, do_ref, dr_ref, dz_ref, dg_ref):
        gv, z, dov = g_ref[...], z_ref[...], do_ref[...]
        xh, r = _post_norm(raw_ref[...], gv, centered)
        sg = _sigmoid(z)
        dyn = dov * (z * sg)
        dz_ref[...] = dov * (xh * gv) * (sg * (1.0 + z * (1.0 - sg)))
        dxh = dyn * gv
        t = dxh - xh * _mean1(dxh * xh)
        if centered:
            t = t - _mean1(dxh)
        dr_ref[...] = r * t
        first = (pl.program_id(1) == 0) if centered else ((pl.program_id(1) == 0) & (pl.program_id(0) == 0))

        @pl.when(first)
        def _():
            dg_ref[...] = jnp.zeros_like(dg_ref)

        dg_ref[0:1, :] += _sum0(dyn * xh)

    blk = lambda: pl.BlockSpec((ROW_TILE, HD), lambda h, i: (i, h))
    g_map = (lambda h, i: (0, h)) if centered else (lambda h, i: (0, 0))
    gw = BW if centered else HD
    return pl.pallas_call(
        body, name=name, grid=(N_HEADS, T // ROW_TILE),
        out_shape=[jax.ShapeDtypeStruct((T, BW), F32), jax.ShapeDtypeStruct((T, BW), F32),
                   jax.ShapeDtypeStruct((8, gw), F32)],
        in_specs=[blk(), pl.BlockSpec((ROW_TILE, HD), lambda h, i: (i, gate_blk + h)),
                  pl.BlockSpec((1, HD), g_map), blk()],
        out_specs=[blk(), blk(), pl.BlockSpec((8, HD), g_map)],
        compiler_params=_cp("arbitrary", "arbitrary"))(raw, p, g, dout)


GLA_ROWS = 256
GLA_CPB = GLA_ROWS // CHUNK
GLA_DK = 64
GLA_W = N_HEADS * GLA_DK
GQ_BLK, GK_BLK, GV_BLK, GG_BLK, TAIL_BLK = 8, 9, 5, 24, 40


def _gla_chunk_common(tl, w2, bv, kv):
    pre = _dot(tl, w2, NN) + bv
    la = _log_sigmoid(pre) * (1.0 / 16.0)
    bc = _exact_dot(_tri(CHUNK, False), la)
    be = bc[CHUNK - 1:CHUNK, :]
    w = jnp.exp(be - bc)
    return pre, w, jnp.exp(be), kv * w


def _head_masks():
    lane = lax.broadcasted_iota(jnp.int32, (1, GLA_W), 1)
    return [jnp.where((lane // GLA_DK) == h, 1.0, 0.0) for h in range(N_HEADS)]


def _gla_fwd(p, w2pad, b):
    nb = T // GLA_ROWS

    def body(q_ref, k_ref, v_ref, t_ref, w2_ref, b_ref, o_ref, st_ref, s_acc):
        @pl.when(pl.program_id(0) == 0)
        def _():
            s_acc[...] = jnp.zeros_like(s_acc)

        masks = _head_masks()
        for c in range(GLA_CPB):
            rows = slice(c * CHUNK, (c + 1) * CHUNK)
            _, _, a, kd = _gla_chunk_common(t_ref[rows, :], w2_ref[...], b_ref[...], k_ref[rows, :])
            q = q_ref[rows, :] * (GLA_DK ** -0.5)
            kv = None
            for h in range(N_HEADS):
                t = _dot(v_ref[rows, h * HD:(h + 1) * HD], kd * masks[h], TN)
                kv = t if kv is None else kv + t
            s_new = s_acc[...] * a + kv
            s_acc[...] = s_new
            st_ref[c] = s_new
            for h in range(N_HEADS):
                o_ref[rows, h * HD:(h + 1) * HD] = _dot(q * masks[h], s_new, NT)

    return pl.pallas_call(
        body, name="gla_fwd", grid=(nb,),
        out_shape=[jax.ShapeDtypeStruct((T, BW), F32), jax.ShapeDtypeStruct((T // CHUNK, HD, GLA_W), F32)],
        in_specs=[pl.BlockSpec((GLA_ROWS, GLA_W), lambda i: (i, GQ_BLK)),
                  pl.BlockSpec((GLA_ROWS, GLA_W), lambda i: (i, GK_BLK)),
                  pl.BlockSpec((GLA_ROWS, BW), lambda i: (i, GV_BLK)),
                  pl.BlockSpec((GLA_ROWS, 128), lambda i: (i, TAIL_BLK)),
                  pl.BlockSpec((128, GLA_W), lambda i: (0, 0)),
                  pl.BlockSpec((1, GLA_W), lambda i: (0, 0))],
        out_specs=[pl.BlockSpec((GLA_ROWS, BW), lambda i: (i, 0)),
                   pl.BlockSpec((GLA_CPB, HD, GLA_W), lambda i: (i, 0, 0))],
        scratch_shapes=[pltpu.VMEM((HD, GLA_W), F32)],
        compiler_params=_cp("arbitrary"))(p, p, p, p, w2pad, b)


def _gla_bwd(p, w2pad, b, states, do):
    nb = T // GLA_ROWS

    def body(q_ref, k_ref, v_ref, t_ref, w2_ref, b_ref, st_ref, prev_ref, do_ref,
             dq_ref, dk_ref, dv_ref, dt_ref, dw2_ref, db_ref, ds_acc):
        step = pl.program_id(0)

        @pl.when(step == 0)
        def _():
            ds_acc[...] = jnp.zeros_like(ds_acc)
            dw2_ref[...] = jnp.zeros_like(dw2_ref)
            db_ref[...] = jnp.zeros_like(db_ref)

        masks = _head_masks()
        up = _tri(CHUNK, True)
        has_prev = jnp.where(step == nb - 1, 0.0, 1.0)
        for c in reversed(range(GLA_CPB)):
            rows = slice(c * CHUNK, (c + 1) * CHUNK)
            tl, w2, k = t_ref[rows, :], w2_ref[...], k_ref[rows, :]
            pre, w, a, kd = _gla_chunk_common(tl, w2, b_ref[...], k)
            q = q_ref[rows, :] * (GLA_DK ** -0.5)
            s_n = st_ref[c]
            s_prev = st_ref[c - 1] if c > 0 else prev_ref[0] * has_prev
            ds = ds_acc[...]
            dos = [do_ref[rows, h * HD:(h + 1) * HD] for h in range(N_HEADS)]
            for h in range(N_HEADS):
                ds = ds + _dot(dos[h], q * masks[h], TN)
            dqp = jnp.zeros((CHUNK, GLA_W), F32)
            dkd = jnp.zeros((CHUNK, GLA_W), F32)
            for h in range(N_HEADS):
                dqp = dqp + masks[h] * _dot(dos[h], s_n, NN)
                dkd = dkd + masks[h] * _dot(v_ref[rows, h * HD:(h + 1) * HD], ds, NN)
                dv_ref[rows, h * HD:(h + 1) * HD] = _dot(kd * masks[h], ds, NT)
            dq_ref[rows, :] = dqp * (GLA_DK ** -0.5)
            dk_ref[rows, :] = dkd * w
            e = dkd * k * w
            dbe = _sum0(e) + _sum0(ds * s_prev) * a
            dla = dbe - _exact_dot(up, e)
            dpre = dla * (1.0 / 16.0) * _sigmoid(-pre)
            db_ref[0:1, :] += _sum0(dpre)
            dw2_ref[...] += _dot(tl, dpre, TN)
            dt_ref[rows, :] = _dot(dpre, w2, NT)
            ds_acc[...] = ds * a

    rev = lambda i: nb - 1 - i
    sh = lambda w: jax.ShapeDtypeStruct((T, w), F32)
    return pl.pallas_call(
        body, name="gla_bwd", grid=(nb,),
        out_shape=[sh(GLA_W), sh(GLA_W), sh(BW), sh(128), jax.ShapeDtypeStruct((128, GLA_W), F32),
                   jax.ShapeDtypeStruct((8, GLA_W), F32)],
        in_specs=[pl.BlockSpec((GLA_ROWS, GLA_W), lambda i: (rev(i), GQ_BLK)),
                  pl.BlockSpec((GLA_ROWS, GLA_W), lambda i: (rev(i), GK_BLK)),
                  pl.BlockSpec((GLA_ROWS, BW), lambda i: (rev(i), GV_BLK)),
                  pl.BlockSpec((GLA_ROWS, 128), lambda i: (rev(i), TAIL_BLK)),
                  pl.BlockSpec((128, GLA_W), lambda i: (0, 0)),
                  pl.BlockSpec((1, GLA_W), lambda i: (0, 0)),
                  pl.BlockSpec((GLA_CPB, HD, GLA_W), lambda i: (rev(i), 0, 0)),
                  pl.BlockSpec((1, HD, GLA_W), lambda i: (jnp.maximum(rev(i) * GLA_CPB - 1, 0), 0, 0)),
                  pl.BlockSpec((GLA_ROWS, BW), lambda i: (rev(i), 0))],
        out_specs=[pl.BlockSpec((GLA_ROWS, GLA_W), lambda i: (rev(i), 0)),
                   pl.BlockSpec((GLA_ROWS, GLA_W), lambda i: (rev(i), 0)),
                   pl.BlockSpec((GLA_ROWS, BW), lambda i: (rev(i), 0)),
                   pl.BlockSpec((GLA_ROWS, 128), lambda i: (rev(i), 0)),
                   pl.BlockSpec((128, GLA_W), lambda i: (0, 0)),
                   pl.BlockSpec((8, GLA_W), lambda i: (0, 0))],
        scratch_shapes=[pltpu.VMEM((HD, GLA_W), F32)],
        compiler_params=_cp("arbitrary"))(p, p, p, p, w2pad, b, states, states, do)


FQ_BLK, FK_BLK = 7, 8
V_FOX_BLK = 36


def _fox_prep_fwd(p, qg, kg, btail):
    def body(q_ref, k_ref, t_ref, qg_ref, kg_ref, bt_ref, o_ref, cum_ref, carry):
        @pl.when(pl.program_id(0) == 0)
        def _():
            carry[...] = jnp.zeros_like(carry)

        for src, gr, off in ((q_ref, qg_ref, 0), (k_ref, kg_ref, BW)):
            for h in range(N_HEADS):
                xv = src[:, h * HD:(h + 1) * HD]
                r = lax.rsqrt(_mean1(xv * xv) + EPS)
                o_ref[:, off + h * HD:off + (h + 1) * HD] = (xv * r * gr[...]).astype(BF16)
        logf = _log_sigmoid(t_ref[...] + bt_ref[...])
        cum = _exact_dot(_tri(ROW_TILE, False), logf) + carry[...]
        cum_ref[...] = cum
        carry[...] = cum[ROW_TILE - 1:ROW_TILE, :]

    return pl.pallas_call(
        body, name="fox_prep_fwd", grid=(T // ROW_TILE,),
        out_shape=[jax.ShapeDtypeStruct((T, 2 * BW), BF16), jax.ShapeDtypeStruct((T, 128), F32)],
        in_specs=[_row_spec(BW, FQ_BLK), _row_spec(BW, FK_BLK), _row_spec(128, TAIL_BLK),
                  _vec_spec(HD), _vec_spec(HD), _vec_spec(128)],
        out_specs=[_row_spec(2 * BW), _row_spec(128)],
        scratch_shapes=[pltpu.VMEM((1, 128), F32)],
        compiler_params=_cp("arbitrary"))(p, p, p, qg, kg, btail)


def _fox_prep_bwd(p, qg, kg, btail, dqn, dkn, dcum):
    nt = T // ROW_TILE

    def body(q_ref, k_ref, t_ref, qg_ref, kg_ref, bt_ref, dq_ref, dk_ref, dc_ref, o_ref, dt_ref, st_ref, carry):
        @pl.when(pl.program_id(0) == 0)
        def _():
            carry[...] = jnp.zeros_like(carry)
            st_ref[...] = jnp.zeros_like(st_ref)

        for row, (src, gr, dsrc, off) in enumerate(((q_ref, qg_ref, dq_ref, 0), (k_ref, kg_ref, dk_ref, BW))):
            for h in range(N_HEADS):
                xv = src[:, h * HD:(h + 1) * HD]
                dy = dsrc[:, h * HD:(h + 1) * HD]
                r = lax.rsqrt(_mean1(xv * xv) + EPS)
                n = xv * r
                dn = dy * gr[...]
                o_ref[:, off + h * HD:off + (h + 1) * HD] = r * (dn - n * _mean1(dn * n))
                st_ref[row:row + 1, :] += _sum0(dy * n)
        z = t_ref[...] + bt_ref[...]
        dlogf = _exact_dot(_tri(ROW_TILE, True), dc_ref[...]) + carry[...]
        carry[...] = dlogf[0:1, :]
        lane = lax.broadcasted_iota(jnp.int32, (1, 128), 1)
        keep = (lane >= FF_LANE0) & (lane < FF_LANE0 + N_HEADS)
        dz = jnp.where(keep, dlogf * _sigmoid(-z), 0.0)
        dt_ref[...] = dz
        st_ref[2:3, :] += _sum0(dz)

    rs = lambda w, col=0: pl.BlockSpec((ROW_TILE, w), lambda i: (nt - 1 - i, col))
    return pl.pallas_call(
        body, name="fox_prep_bwd", grid=(nt,),
        out_shape=[jax.ShapeDtypeStruct((T, 2 * BW), F32), jax.ShapeDtypeStruct((T, 128), F32),
                   jax.ShapeDtypeStruct((8, 128), F32)],
        in_specs=[rs(BW, FQ_BLK), rs(BW, FK_BLK), rs(128, TAIL_BLK), _vec_spec(HD), _vec_spec(HD), _vec_spec(128),
                  rs(BW), rs(BW), rs(128)],
        out_specs=[rs(2 * BW), rs(128), pl.BlockSpec((8, 128), lambda i: (0, 0))],
        scratch_shapes=[pltpu.VMEM((1, 128), F32)],
        compiler_params=_cp("arbitrary"))(p, p, p, qg, kg, btail, dqn, dkn, dcum)


def _fox_logits(q_ref, k_ref, cc_ref, cr_ref, i):
    rows = lax.broadcasted_iota(jnp.int32, (TQ, T), 0) + i * TQ
    cols = lax.broadcasted_iota(jnp.int32, (TQ, T), 1)
    s = _dot(q_ref[...], k_ref[...], NT) * (HD ** -0.5) + cc_ref[0] - cr_ref[0]
    return jnp.where(cols <= rows, s, -1e30)


def _fox_specs():
    q_spec = pl.BlockSpec((TQ, HD), lambda h, i: (i, h))
    k_spec = pl.BlockSpec((T, HD), lambda h, i: (0, N_HEADS + h))
    v_spec = pl.BlockSpec((T, HD), lambda h, i: (0, V_FOX_BLK + h))
    col_spec = pl.BlockSpec((1, TQ, 1), lambda h, i: (h, i, 0))
    row_spec = pl.BlockSpec((1, 1, T), lambda h, i: (h, 0, 0))
    return q_spec, k_spec, v_spec, col_spec, row_spec


def _fox_fwd(qkn, p, cumcol, cumrow):
    def body(q_ref, k_ref, v_ref, cc_ref, cr_ref, o_ref, lse_ref):
        s = _fox_logits(q_ref, k_ref, cc_ref, cr_ref, pl.program_id(1))
        m = jnp.max(s, axis=-1, keepdims=True)
        e = jnp.exp(s - m)
        l = jnp.sum(e, axis=-1, keepdims=True)
        o_ref[...] = _dot(e / l, v_ref[...], NN)
        lse_ref[0] = m + jnp.log(l)

    q_spec, k_spec, v_spec, col_spec, row_spec = _fox_specs()
    return pl.pallas_call(
        body, name="fox_fwd", grid=(N_HEADS, T // TQ),
        out_shape=[jax.ShapeDtypeStruct((T, BW), F32), jax.ShapeDtypeStruct((N_HEADS, T, 1), F32)],
        in_specs=[q_spec, k_spec, v_spec, col_spec, row_spec], out_specs=[q_spec, col_spec],
        compiler_params=_cp("parallel", "parallel"))(qkn, qkn, p, cumcol, cumrow)


def _fox_bwd(qkn, p, cumcol, cumrow, lse, o, do):
    def body(q_ref, k_ref, v_ref, cc_ref, cr_ref, lse_ref, o_ref, do_ref, dq_ref, dk_ref, dv_ref, dr_ref, dc_ref):
        q, k, dov = q_ref[...], k_ref[...], do_ref[...]
        s = _fox_logits(q_ref, k_ref, cc_ref, cr_ref, pl.program_id(1))
        pm = jnp.exp(s - lse_ref[0])
        dp = _dot(dov, v_ref[...], NT)
        delta = jnp.sum(o_ref[...] * dov, axis=-1, keepdims=True)
        ds = pm * (dp - delta)
        dq_ref[...] = _dot(ds, k, NN) * (HD ** -0.5)
        dr_ref[0] = jnp.sum(ds, axis=-1, keepdims=True)

        @pl.when(pl.program_id(1) == 0)
        def _():
            dk_ref[...] = jnp.zeros_like(dk_ref)
            dv_ref[...] = jnp.zeros_like(dv_ref)
            dc_ref[...] = jnp.zeros_like(dc_ref)

        dk_ref[...] += _dot(ds, q, TN) * (HD ** -0.5)
        dv_ref[...] += _dot(pm, dov, TN)
        dc_ref[0] += _sum0(ds)

    q_spec, k_spec, v_spec, col_spec, row_spec = _fox_specs()
    acc_spec = pl.BlockSpec((T, HD), lambda h, i: (0, h))
    sh = jax.ShapeDtypeStruct((T, BW), F32)
    return pl.pallas_call(
        body, name="fox_bwd", grid=(N_HEADS, T // TQ),
        out_shape=[sh, sh, sh, jax.ShapeDtypeStruct((N_HEADS, T, 1), F32), jax.ShapeDtypeStruct((N_HEADS, 1, T), F32)],
        in_specs=[q_spec, k_spec, v_spec, col_spec, row_spec, col_spec, q_spec, q_spec],
        out_specs=[q_spec, acc_spec, acc_spec, col_spec, row_spec],
        compiler_params=_cp("parallel", "arbitrary"))(qkn, qkn, p, cumcol, cumrow, lse, o, do)


def _mix_fwd(gpre, b_mg, y0, y1, y2):
    def body(g_ref, b_ref, y0_ref, y1_ref, y2_ref, o_ref):
        acc = None
        for n, y_ref in enumerate((y0_ref, y1_ref, y2_ref)):
            sl = slice(n * D, (n + 1) * D)
            t = _sigmoid(g_ref[:, sl] + b_ref[:, sl]) * y_ref[...]
            acc = t if acc is None else acc + t
        o_ref[...] = acc.astype(BF16)

    return pl.pallas_call(
        body, name="mix_fwd", grid=(T // ROW_TILE,), out_shape=jax.ShapeDtypeStruct((T, D), BF16),
        in_specs=[_row_spec(3 * D), _vec_spec(3 * D), _row_spec(), _row_spec(), _row_spec()],
        out_specs=_row_spec(), compiler_params=_cp("parallel"))(gpre, b_mg, y0, y1, y2)


def _mix_bwd(gpre, b_mg, y0, y1, y2, dmi):
    def body(g_ref, b_ref, y0_ref, y1_ref, y2_ref, d_ref, dy0_ref, dy1_ref, dy2_ref, dg_ref, db_ref):
        @pl.when(pl.program_id(0) == 0)
        def _():
            db_ref[...] = jnp.zeros_like(db_ref)

        dv = d_ref[...]
        for n, (y_ref, dy_ref) in enumerate(((y0_ref, dy0_ref), (y1_ref, dy1_ref), (y2_ref, dy2_ref))):
            sl = slice(n * D, (n + 1) * D)
            sg = _sigmoid(g_ref[:, sl] + b_ref[:, sl])
            dy_ref[...] = (dv * sg).astype(BF16)
            dpre = dv * y_ref[...] * (sg * (1.0 - sg))
            dg_ref[:, sl] = dpre.astype(BF16)
            db_ref[0:1, sl] += _sum0(dpre)

    shb = jax.ShapeDtypeStruct((T, D), BF16)
    return pl.pallas_call(
        body, name="mix_bwd", grid=(T // ROW_TILE,),
        out_shape=[shb, shb, shb, jax.ShapeDtypeStruct((T, 3 * D), BF16), jax.ShapeDtypeStruct((8, 3 * D), F32)],
        in_specs=[_row_spec(3 * D), _vec_spec(3 * D), _row_spec(), _row_spec(), _row_spec(), _row_spec()],
        out_specs=[_row_spec(), _row_spec(), _row_spec(), _row_spec(3 * D), pl.BlockSpec((8, 3 * D), lambda i: (0, 0))],
        compiler_params=_cp("arbitrary"))(gpre, b_mg, y0, y1, y2, dmi)


FF_COLS = 256
FF_NBLK = D_FF // FF_COLS


def _shift_rows(a, n):
    rows = lax.broadcasted_iota(jnp.int32, a.shape, 0)
    rolled = pltpu.roll(a, n % T, 0)
    return jnp.where((rows >= n) if n > 0 else (rows < T + n), rolled, 0.0)


def _ffn_act_fwd(uu, w_conv, b_conv):
    def body(u_ref, g_ref, w_ref, b_ref, o_ref):
        u = u_ref[...]
        w = w_ref[...]
        uc = b_ref[...] + w[0:1, :] * _shift_rows(u, 2) + w[1:2, :] * _shift_rows(u, 1) + w[2:3, :] * u
        o_ref[...] = (uc * _sigmoid(uc) * g_ref[...]).astype(BF16)

    return pl.pallas_call(
        body, name="ffn_act_fwd", grid=(FF_NBLK,), out_shape=jax.ShapeDtypeStruct((T, D_FF), BF16),
        in_specs=[pl.BlockSpec((T, FF_COLS), lambda j: (0, j)), pl.BlockSpec((T, FF_COLS), lambda j: (0, FF_NBLK + j)),
                  pl.BlockSpec((3, FF_COLS), lambda j: (0, j)), pl.BlockSpec((1, FF_COLS), lambda j: (0, j))],
        out_specs=pl.BlockSpec((T, FF_COLS), lambda j: (0, j)),
        compiler_params=_cp("parallel"))(uu, uu, w_conv, b_conv)


def _ffn_act_bwd(uu, w_conv, b_conv, da):
    def body(u_ref, g_ref, w_ref, b_ref, da_ref, du_ref, dg_ref, st_ref):
        u, w, dav = u_ref[...], w_ref[...], da_ref[...]
        u1, u2 = _shift_rows(u, 1), _shift_rows(u, 2)
        uc = b_ref[...] + w[0:1, :] * u2 + w[1:2, :] * u1 + w[2:3, :] * u
        sg = _sigmoid(uc)
        dg_ref[...] = (dav * (uc * sg)).astype(BF16)
        duc = dav * g_ref[...] * (sg * (1.0 + uc * (1.0 - sg)))
        du = w[2:3, :] * duc + w[1:2, :] * _shift_rows(duc, -1) + w[0:1, :] * _shift_rows(duc, -2)
        du_ref[...] = du.astype(BF16)
        st_ref[...] = jnp.zeros_like(st_ref)
        st_ref[0:1, :] = _sum0(duc * u2)
        st_ref[1:2, :] = _sum0(duc * u1)
        st_ref[2:3, :] = _sum0(duc * u)
        st_ref[3:4, :] = _sum0(duc)

    cb = lambda rows=T, off=0: pl.BlockSpec((rows, FF_COLS), lambda j: (0, off + j))
    shb = jax.ShapeDtypeStruct((T, D_FF), BF16)
    return pl.pallas_call(
        body, name="ffn_act_bwd", grid=(FF_NBLK,),
        out_shape=[shb, shb, jax.ShapeDtypeStruct((8, D_FF), F32)],
        in_specs=[cb(), cb(T, FF_NBLK), cb(3), cb(1), cb()],
        out_specs=[cb(), cb(), cb(8)],
        compiler_params=_cp("parallel"))(uu, uu, w_conv, b_conv, da)


def _adamw(g, w, m, v, tr, name):
    partial = g.ndim == 3
    R, C = w.shape
    tr = R if tr is None else tr
    assert R % tr == 0

    def body(g_ref, w_ref, m_ref, v_ref, go_ref, d_ref, mo_ref, vo_ref):
        if partial:
            gv = g_ref[0].astype(F32)
            for j in range(1, N_DEV):
                gv = gv + g_ref[j].astype(F32)
        else:
            gv = g_ref[...]
        go_ref[...] = gv
        mn = ADAM_B1 * m_ref[...] + (1.0 - ADAM_B1) * gv
        vn = ADAM_B2 * v_ref[...] + (1.0 - ADAM_B2) * (gv * gv)
        mo_ref[...] = mn
        vo_ref[...] = vn
        m_hat = mn / (1.0 - ADAM_B1 ** ADAM_STEP)
        v_hat = vn / (1.0 - ADAM_B2 ** ADAM_STEP)
        d_ref[...] = -ADAM_LR * (m_hat / (jnp.sqrt(v_hat) + ADAM_EPS) + ADAM_WD * w_ref[...])

    spec = pl.BlockSpec((tr, C), lambda i: (i, 0))
    g_spec = pl.BlockSpec((N_DEV, tr, C), lambda i: (0, i, 0)) if partial else spec
    sh = jax.ShapeDtypeStruct((R, C), F32)
    return pl.pallas_call(
        body, name=name, grid=(R // tr,), out_shape=[sh, sh, sh, sh],
        in_specs=[g_spec, spec, spec, spec], out_specs=[spec, spec, spec, spec],
        compiler_params=_cp("parallel"))(g, w, m, v)


def _sum_partials(g, name):
    _, R, C = g.shape

    def body(g_ref, o_ref):
        acc = g_ref[0]
        for j in range(1, N_DEV):
            acc = acc + g_ref[j]
        o_ref[...] = acc

    return pl.pallas_call(
        body, name=name, out_shape=jax.ShapeDtypeStruct((R, C), F32),
        in_specs=[pl.BlockSpec(memory_space=pltpu.VMEM)], out_specs=pl.BlockSpec(memory_space=pltpu.VMEM),
        compiler_params=_cp())(g)


def _permute_in(w):
    pad = jnp.zeros(w.shape[:-1] + (NP - IN_W,), w.dtype)
    return jnp.concatenate([w[..., :3072], w[..., 3088:5136], w[..., 3072:3088], w[..., 5136:5140], pad], axis=-1)


def _unpermute_in(w):
    return jnp.concatenate([w[..., :3072], w[..., 5120:5136], w[..., 3072:5120], w[..., 5136:5140]], axis=-1)


def _pack_local(w_in, w_o, w_down, w_br, w_mg, w_up):
    segs = [_permute_in(w_in), w_o, w_down, jnp.swapaxes(w_br, 2, 3), jnp.swapaxes(w_mg, 1, 2), jnp.swapaxes(w_up, 1, 2)]
    segs = [s.reshape(DEPTH, r, PACK_W) for s, r in zip(segs, SEG_ROWS)]
    return jnp.concatenate(segs, axis=1).reshape(DEPTH * LAYER_ROWS, PACK_W)


def _unpack_local(buf):
    b = buf.reshape(DEPTH, LAYER_ROWS, PACK_W)
    offs = np.cumsum((0,) + SEG_ROWS)
    s = [b[:, offs[i]:offs[i + 1]] for i in range(len(SEG_ROWS))]
    w_in = _unpermute_in(s[0].reshape(DEPTH, 128, NP))
    w_o = s[1].reshape(DEPTH, 128, D)
    w_down = s[2].reshape(DEPTH, 352, D)
    w_br = jnp.swapaxes(s[3].reshape(DEPTH, 3, 128, BW), 2, 3)
    w_mg = jnp.swapaxes(s[4].reshape(DEPTH, 384, D), 1, 2)
    w_up = jnp.swapaxes(s[5].reshape(DEPTH, 704, D), 1, 2)
    return w_in, w_o, w_down, w_br, w_mg, w_up


def _unpack_gathered(buf):
    b = buf.reshape(N_DEV, DEPTH, LAYER_ROWS, PACK_W)
    offs = np.cumsum((0,) + SEG_ROWS)
    s = [jnp.swapaxes(b[:, :, offs[i]:offs[i + 1]], 0, 1) for i in range(len(SEG_ROWS))]
    w_in = s[0].reshape(DEPTH, D, NP)
    w_o = s[1].reshape(DEPTH, D, D)
    w_down = s[2].reshape(DEPTH, D_FF, D)
    w_br_t = jnp.swapaxes(s[3].reshape(DEPTH, N_DEV, 3, 128, BW), 1, 2).reshape(DEPTH, 3, D, BW)
    w_mg_t = s[4].reshape(DEPTH, 3 * D, D)
    w_up_t = s[5].reshape(DEPTH, 2 * D_FF, D)
    return w_in, w_o, w_down, w_br_t, w_mg_t, w_up_t


def _pack_grads_layer(d_in, d_o, d_down, d_br_t, d_mg_t, d_up_t):
    segs = [d_in.reshape(N_DEV, 704, PACK_W), d_o.reshape(N_DEV, 128, PACK_W), d_down.reshape(N_DEV, 352, PACK_W),
            jnp.swapaxes(d_br_t.reshape(3, N_DEV, 128, BW), 0, 1).reshape(N_DEV, 192, PACK_W),
            d_mg_t.reshape(N_DEV, 384, PACK_W), d_up_t.reshape(N_DEV, 704, PACK_W)]
    return jnp.concatenate(segs, axis=1)


def _flat_pack(arrs):
    flat = jnp.concatenate([a.reshape(-1).astype(F32) for a in arrs])
    n = flat.shape[0]
    rows = -(-n // 1024) * 8
    return jnp.pad(flat, (0, rows * 128 - n)).reshape(rows, 128)


def _flat_unpack(buf, shapes):
    flat = buf.reshape(-1)
    out, off = [], 0
    for s in shapes:
        n = int(np.prod(s))
        out.append(flat[off:off + n].reshape(s))
        off += n
    return out


def _layer_fwd(x0, wl, consts):
    cosf, sinf, logg = consts
    row = lambda a: a.reshape(1, -1)
    h = _norm_fwd(x0, row(wl["norm1_g"]), row(wl["scale1"]), row(wl["shift1"]), "norm1_fwd")
    p = _matmul(h, wl["w_in"], "nn", "in_proj")
    qk = _rope_fwd(p, cosf, sinf)
    ret_raw = _ret_fwd(qk, p, logg)
    br0 = _branch_post_fwd(ret_raw, p, row(wl["ret_norm_g"]), 12, True, "ret_post_fwd")
    gla_raw, states = _gla_fwd(p, wl["w2pad"], row(wl["b_gla_a"]))
    br1 = _branch_post_fwd(gla_raw, p, row(wl["gla_norm_g"]), GG_BLK, False, "gla_post_fwd")
    qkn, cum = _fox_prep_fwd(p, row(wl["q_norm_g"]), row(wl["k_norm_g"]), row(wl["btail"]))
    cum4 = cum[:, FF_LANE0:FF_LANE0 + N_HEADS].T
    cumcol, cumrow = cum4.reshape(N_HEADS, T, 1), cum4.reshape(N_HEADS, 1, T)
    fox_o, lse = _fox_fwd(qkn, p, cumcol, cumrow)
    ys = [_matmul(b, wl["w_br_t"][n], "nt", "br_proj%d" % n) for n, b in enumerate((br0, br1, fox_o))]
    gpre = _matmul(h, wl["w_mg_t"], "nt", "gate_proj")
    mixed_in = _mix_fwd(gpre, row(wl["b_mg"]), *ys)
    mixed = _matmul(mixed_in, wl["w_o"], "nn", "o_proj")
    x1, h2 = _norm_fwd(x0, row(wl["norm2_g"]), row(wl["scale2"]), row(wl["shift2"]), "norm2_fwd",
                       m=mixed, gate=row(wl["gate1"]))
    uu = _matmul(h2, wl["w_up_t"], "nt", "up_proj")
    act = _ffn_act_fwd(uu, wl["w_conv"], row(wl["b_conv"]))
    y = _matmul(act, wl["w_down"], "nn", "down_proj", tk=1408)
    x2 = _axpy(x1, y, row(wl["gate2"]), "resid2")
    saved = dict(x0=x0, h=h, p=p, qk=qk, ret_raw=ret_raw, br0=br0, gla_raw=gla_raw, states=states, br1=br1,
                 qkn=qkn, cumcol=cumcol, cumrow=cumrow, fox_o=fox_o, lse=lse, y0=ys[0], y1=ys[1], y2=ys[2],
                 gpre=gpre, mixed_in=mixed_in, mixed=mixed, x1=x1, h2=h2, uu=uu, act=act, y=y)
    return x2, saved


def _layer_bwd(dx2, wl, sv, consts):
    cosf, sinf, logg = consts
    row = lambda a: a.reshape(1, -1)
    dy, st_g2 = _gate_bwd(dx2, sv["y"], row(wl["gate2"]), "gate2_bwd")
    dact = _matmul(dy, wl["w_down"], "nt", "down_dx", tn=1408)
    d_down = _matmul(sv["act"], dy, "tn", "down_dw", out_dtype=BF16, tm=1408)
    du, dg, st_conv = _ffn_act_bwd(sv["uu"], wl["w_conv"], row(wl["b_conv"]), dact)
    dh2 = _matmul(du, wl["w_up_t"][:D_FF], "nn", "up_dx_u", tk=1408)
    dh2 = _matmul(dg, wl["w_up_t"][D_FF:], "nn", "up_dx_g", tk=1408, add=dh2)
    d_up_t = jnp.concatenate([_matmul(du, sv["h2"], "tn", "up_dw_u", out_dtype=BF16, tm=1408),
                              _matmul(dg, sv["h2"], "tn", "up_dw_g", out_dtype=BF16, tm=1408)], axis=0)
    dx1, st_n2 = _norm_bwd(sv["x1"], dh2, dx2, row(wl["norm2_g"]), row(wl["scale2"]), row(wl["shift2"]), "norm2_bwd")
    dmixed, st_g1 = _gate_bwd(dx1, sv["mixed"], row(wl["gate1"]), "gate1_bwd")
    dmi = _matmul(dmixed, wl["w_o"], "nt", "o_dx")
    d_o = _matmul(sv["mixed_in"], dmixed, "tn", "o_dw", out_dtype=BF16)
    dy0, dy1, dy2, dgpre, st_bmg = _mix_bwd(sv["gpre"], row(wl["b_mg"]), sv["y0"], sv["y1"], sv["y2"], dmi)
    brs = (sv["br0"], sv["br1"], sv["fox_o"])
    dbr = [_matmul(d, wl["w_br_t"][n], "nn", "br_dx%d" % n) for n, d in enumerate((dy0, dy1, dy2))]
    d_br_t = jnp.stack([_matmul(d, brs[n], "tn", "br_dw%d" % n, out_dtype=BF16) for n, d in enumerate((dy0, dy1, dy2))])
    dh = _matmul(dgpre, wl["w_mg_t"], "nn", "gate_dx", tk=1024)
    d_mg_t = _matmul(dgpre, sv["h"], "tn", "gate_dw", out_dtype=BF16)
    p = sv["p"]
    dqn, dkn, dfv, drow, dcol = _fox_bwd(sv["qkn"], p, sv["cumcol"], sv["cumrow"], sv["lse"], sv["fox_o"], dbr[2])
    dcum4 = drow.reshape(N_HEADS, T) - dcol.reshape(N_HEADS, T)
    dcum = jnp.pad(dcum4.T, ((0, 0), (FF_LANE0, 128 - FF_LANE0 - N_HEADS)))
    dfqk, dtail_fox, st_fox = _fox_prep_bwd(p, row(wl["q_norm_g"]), row(wl["k_norm_g"]), row(wl["btail"]), dqn, dkn, dcum)
    dgla_raw, dgg, st_gn = _branch_post_bwd(sv["gla_raw"], p, row(wl["gla_norm_g"]), dbr[1], GG_BLK, False, "gla_post_bwd")
    dgq, dgk, dgv, dtail_gla, dw2pad, st_bg = _gla_bwd(p, wl["w2pad"], row(wl["b_gla_a"]), sv["states"], dgla_raw)
    dret_raw, drg, st_rn = _branch_post_bwd(sv["ret_raw"], p, row(wl["ret_norm_g"]), dbr[0], 12, True, "ret_post_bwd")
    dqr, dkr, drv = _ret_bwd(sv["qk"], p, logg, dret_raw)
    drqk = _rope_bwd(dqr, dkr, cosf, sinf)
    dp = jnp.concatenate([a.astype(BF16) for a in (drqk, drv, drg, dgq, dgk, dgv, dgg, dfqk, dfv, dtail_fox + dtail_gla)]
                         + [jnp.zeros((T, NP - TAIL0 - 128), BF16)], axis=1)
    dh = _matmul(dp, wl["w_in"], "nt", "in_dx", tk=1408, add=dh)
    d_in = _matmul(sv["h"], dp, "tn", "in_dw", out_dtype=BF16)
    dx0, st_n1 = _norm_bwd(sv["x0"], dh, dx1, row(wl["norm1_g"]), row(wl["scale1"]), row(wl["shift1"]), "norm1_bwd")
    big = _pack_grads_layer(d_in, d_o, d_down, d_br_t, d_mg_t, d_up_t)
    dmod = jnp.concatenate([st_n1[2], st_n1[1], st_g1[0], st_n2[2], st_n2[1], st_g2[0]])
    small = dict(norm1_g=st_n1[0], norm2_g=st_n2[0], b_gla_a=st_bg[0], b_fox_f=st_fox[2, FF_LANE0:FF_LANE0 + N_HEADS],
                 ret_norm_g=st_rn[0], gla_norm_g=st_gn[0], q_norm_g=st_fox[0], k_norm_g=st_fox[1], b_mg=st_bmg[0],
                 b_conv=st_conv[3], w_gla_a2=dw2pad[:LR_LANES], w_conv=st_conv[0:3])
    return dx0, big, dmod, small


SMALL_REPL = ("norm1_g", "norm2_g", "b_ada", "b_gla_a", "b_fox_f", "ret_norm_g", "gla_norm_g", "q_norm_g", "k_norm_g",
              "b_mg", "b_conv")
SMALL_SHARDED = ("w_gla_a2", "w_conv")
BIG = ("w_in", "w_o", "w_down", "w_br", "w_mg", "w_up")
WEIGHTS = ("norm1_g", "norm2_g", "w_ada", "b_ada", "w_in", "w_gla_a2", "b_gla_a", "b_fox_f", "ret_norm_g", "gla_norm_g",
           "q_norm_g", "k_norm_g", "w_br", "w_mg", "b_mg", "w_o", "w_up", "w_conv", "b_conv", "w_down")


def kernel(x, c, norm1_g, norm2_g, w_ada, b_ada, w_in, w_gla_a2, b_gla_a, b_fox_f, ret_norm_g, gla_norm_g, q_norm_g, k_norm_g, w_br, w_mg, b_mg, w_o, w_up, w_conv, b_conv, w_down, loss_target, m_norm1_g, m_norm2_g, m_w_ada, m_b_ada, m_w_in, m_w_gla_a2, m_b_gla_a, m_b_fox_f, m_ret_norm_g, m_gla_norm_g, m_q_norm_g, m_k_norm_g, m_w_br, m_w_mg, m_b_mg, m_w_o, m_w_up, m_w_conv, m_b_conv, m_w_down, v_norm1_g, v_norm2_g, v_w_ada, v_b_ada, v_w_in, v_w_gla_a2, v_b_gla_a, v_b_fox_f, v_ret_norm_g, v_gla_norm_g, v_q_norm_g, v_k_norm_g, v_w_br, v_w_mg, v_b_mg, v_w_o, v_w_up, v_w_conv, v_b_conv, v_w_down):
    W = dict(norm1_g=norm1_g, norm2_g=norm2_g, w_ada=w_ada, b_ada=b_ada, w_in=w_in, w_gla_a2=w_gla_a2, b_gla_a=b_gla_a,
             b_fox_f=b_fox_f, ret_norm_g=ret_norm_g, gla_norm_g=gla_norm_g, q_norm_g=q_norm_g, k_norm_g=k_norm_g,
             w_br=w_br, w_mg=w_mg, b_mg=b_mg, w_o=w_o, w_up=w_up, w_conv=w_conv, b_conv=b_conv, w_down=w_down)
    M = dict(norm1_g=m_norm1_g, norm2_g=m_norm2_g, w_ada=m_w_ada, b_ada=m_b_ada, w_in=m_w_in, w_gla_a2=m_w_gla_a2,
             b_gla_a=m_b_gla_a, b_fox_f=m_b_fox_f, ret_norm_g=m_ret_norm_g, gla_norm_g=m_gla_norm_g, q_norm_g=m_q_norm_g,
             k_norm_g=m_k_norm_g, w_br=m_w_br, w_mg=m_w_mg, b_mg=m_b_mg, w_o=m_w_o, w_up=m_w_up, w_conv=m_w_conv,
             b_conv=m_b_conv, w_down=m_w_down)
    V = dict(norm1_g=v_norm1_g, norm2_g=v_norm2_g, w_ada=v_w_ada, b_ada=v_b_ada, w_in=v_w_in, w_gla_a2=v_w_gla_a2,
             b_gla_a=v_b_gla_a, b_fox_f=v_b_fox_f, ret_norm_g=v_ret_norm_g, gla_norm_g=v_gla_norm_g, q_norm_g=v_q_norm_g,
             k_norm_g=v_k_norm_g, w_br=v_w_br, w_mg=v_w_mg, b_mg=v_b_mg, w_o=v_w_o, w_up=v_w_up, w_conv=v_w_conv,
             b_conv=v_b_conv, w_down=v_w_down)
    me = 4 * lax.axis_index("x") + 2 * lax.axis_index("y") + lax.axis_index("c")
    x2d, tgt = x.reshape(T, D), loss_target.reshape(T, D)

    sm = _flat_pack([c, w_gla_a2, w_conv])
    sm_all = _exchange(sm, True, "gather_small")
    parts = [_flat_unpack(sm_all[j], [(D,), (DEPTH, LR_LANES, 32), (DEPTH, 3, 352)]) for j in range(N_DEV)]
    c_all = jnp.stack([q[0] for q in parts])
    w_gla_full = jnp.concatenate([q[1] for q in parts], axis=2)
    w_conv_full = jnp.concatenate([q[2] for q in parts], axis=2)

    n_ada = w_ada.shape[2]
    b_loc = lax.dynamic_slice_in_dim(b_ada, me * n_ada, n_ada, axis=1).reshape(DEPTH, 1, n_ada)
    mod_all = _ada_fwd(c_all, w_ada, b_loc)
    mod_recv = _exchange(jnp.swapaxes(mod_all, 0, 1), False, "a2a_mod")
    mod = jnp.swapaxes(mod_recv, 0, 1).reshape(DEPTH, 6, D)

    packed_w = _pack_local(*[W[n] for n in BIG])
    gathered = _exchange(packed_w.astype(BF16), True, "gather_weights")
    g_in, g_o, g_down, g_br_t, g_mg_t, g_up_t = _unpack_gathered(gathered)

    w2pad = jnp.pad(w_gla_full, ((0, 0), (0, 128 - LR_LANES), (0, 0)))
    btail = jnp.pad(b_fox_f, ((0, 0), (FF_LANE0, 128 - FF_LANE0 - N_HEADS)))
    layers = dict(norm1_g=norm1_g, norm2_g=norm2_g, b_gla_a=b_gla_a, ret_norm_g=ret_norm_g, gla_norm_g=gla_norm_g,
                  q_norm_g=q_norm_g, k_norm_g=k_norm_g, b_mg=b_mg, b_conv=b_conv, w_conv=w_conv_full, w2pad=w2pad,
                  btail=btail, w_in=g_in, w_o=g_o, w_down=g_down, w_br_t=g_br_t, w_mg_t=g_mg_t, w_up_t=g_up_t,
                  shift1=mod[:, 0], scale1=mod[:, 1], gate1=mod[:, 2], shift2=mod[:, 3], scale2=mod[:, 4], gate2=mod[:, 5])
    consts = _rope_tables() + (_ret_logg(),)

    def fwd_step(xc, wl):
        return _layer_fwd(xc, wl, consts)

    x_out, saved = lax.scan(fwd_step, x2d, layers)
    loss_part, dy = _loss_fwd_bwd(x_out, tgt)
    loss = lax.psum(loss_part[0, 0], ("x", "y", "c"))

    def bwd_step(dxc, inp):
        wl, sv = inp
        dx0, big, dmod, small = _layer_bwd(dxc, wl, sv, consts)
        return dx0, (big, dmod, small)

    grad_x, (big_g, dmod, small_g) = lax.scan(bwd_step, dy, (layers, saved), reverse=True)

    dmod_send = jnp.swapaxes(dmod.reshape(DEPTH, N_DEV, n_ada), 0, 1)
    dmod_all = jnp.swapaxes(_exchange(dmod_send, False, "a2a_dmod"), 0, 1)
    g_ada = _ada_bwd(c_all, dmod_all)

    send = jnp.swapaxes(big_g, 0, 1).reshape(N_DEV, DEPTH * LAYER_ROWS, PACK_W)
    recv = _exchange(send, False, "a2a_grads")
    packed_m = _pack_local(*[M[n] for n in BIG])
    packed_v = _pack_local(*[V[n] for n in BIG])
    big_out = [_unpack_local(o) for o in _adamw(recv, packed_w, packed_m, packed_v, 448, "adamw_big")]
    ada_out = [o.reshape(DEPTH, D, n_ada) for o in _adamw(
        g_ada.reshape(DEPTH * n_ada, D), w_ada.reshape(DEPTH * n_ada, D), m_w_ada.reshape(DEPTH * n_ada, D),
        v_w_ada.reshape(DEPTH * n_ada, D), 512, "adamw_ada")]

    small_g = dict(small_g, b_ada=dmod)
    names = SMALL_REPL + SMALL_SHARDED
    full_shapes = [W[n].shape for n in SMALL_REPL] + [(DEPTH, LR_LANES, 256), (DEPTH, 3, D_FF)]
    part = _flat_pack([small_g[n] for n in names])
    total = _flat_unpack(_sum_partials(_exchange(part, True, "gather_small_grads"), "sum_small"), full_shapes)
    total = dict(zip(names, total))
    total["w_gla_a2"] = lax.dynamic_slice_in_dim(total["w_gla_a2"], me * 32, 32, axis=2)
    total["w_conv"] = lax.dynamic_slice_in_dim(total["w_conv"], me * 352, 352, axis=2)
    shapes = [W[n].shape for n in names]
    small_out = _adamw(_flat_pack([total[n] for n in names]), _flat_pack([W[n] for n in names]),
                       _flat_pack([M[n] for n in names]), _flat_pack([V[n] for n in names]), None, "adamw_small")
    small_out = [dict(zip(names, _flat_unpack(o, shapes))) for o in small_out]

    outs = []
    for k in range(4):
        d = dict(small_out[k])
        d.update(zip(BIG, big_out[k]))
        d["w_ada"] = ada_out[k]
        outs.append([d[n] for n in WEIGHTS])
    return (loss, grad_x.reshape(1, T, D), *outs[0], *outs[1], *outs[2], *outs[3])
```

```python
import functools

import numpy as np
import jax
import jax.numpy as jnp
from jax import lax
from jax.experimental import pallas as pl
from jax.experimental.pallas import tpu as pltpu

F32 = jnp.float32
BF16 = jnp.bfloat16

N_DEV = 8
T = 2048
D = 1024
DEPTH = 4
N_HEADS = 4
HD = 128
BW = 512
D_FF = 2816
CHUNK = 64
EPS = 1e-6
IN_W = 5140
NP = 5632
TAIL0 = 5120
LR_LANES = 16
FF_LANE0 = 16
PACK_W = 1024
SEG_ROWS = (704, 128, 352, 192, 384, 704)
LAYER_ROWS = sum(SEG_ROWS)
VMEM_LIMIT_V7X = 56 * 1024 * 1024

ADAM_LR, ADAM_B1, ADAM_B2, ADAM_EPS, ADAM_WD, ADAM_STEP = 0.001, 0.9, 0.999, 1e-08, 0.01, 10

MESH_ID = pl.DeviceIdType.MESH


def _cp(*sem):
    return pltpu.CompilerParams(dimension_semantics=sem if sem else None, vmem_limit_bytes=VMEM_LIMIT_V7X)


def _sigmoid(z):
    return 1.0 / (1.0 + jnp.exp(-z))


def _log_sigmoid(z):
    return jnp.minimum(z, 0.0) - jnp.log(1.0 + jnp.exp(-jnp.abs(z)))


def _sum0(a):
    return jnp.sum(a, axis=0, keepdims=True)


def _mean1(a):
    return jnp.mean(a, axis=-1, keepdims=True)


def _dot(a, b, dims):
    return lax.dot_general(a.astype(BF16), b.astype(BF16), (dims, ((), ())), preferred_element_type=F32)


NN = ((1,), (0,))
NT = ((1,), (1,))
TN = ((0,), (0,))


def _exact_dot(m01, a):
    a1 = a.astype(BF16)
    r1 = a - a1.astype(F32)
    a2 = r1.astype(BF16)
    a3 = (r1 - a2.astype(F32)).astype(BF16)
    d = lambda z: jnp.dot(m01, z, preferred_element_type=F32)
    return d(a1) + d(a2) + d(a3)


def _tri(n, upper):
    r = lax.broadcasted_iota(jnp.int32, (n, n), 0)
    c = lax.broadcasted_iota(jnp.int32, (n, n), 1)
    return jnp.where((c >= r) if upper else (c <= r), 1.0, 0.0).astype(BF16)


def _exchange(x, gather, name):
    blk = x.shape if gather else x.shape[1:]

    def body(x_ref, o_ref, send_sems, recv_sems, loc_sem):
        mx, my, mc = lax.axis_index("x"), lax.axis_index("y"), lax.axis_index("c")
        me = 4 * mx + 2 * my + mc
        loc = pltpu.make_async_copy(x_ref if gather else x_ref.at[me], o_ref.at[me], loc_sem)
        loc.start()
        copies = []
        for k in range(1, N_DEV):
            px = mx ^ (k >> 2) if (k >> 2) else mx
            py = my ^ ((k >> 1) & 1) if ((k >> 1) & 1) else my
            pc = mc ^ (k & 1) if (k & 1) else mc
            peer = 4 * px + 2 * py + pc
            cp = pltpu.make_async_remote_copy(
                src_ref=x_ref if gather else x_ref.at[peer], dst_ref=o_ref.at[me],
                send_sem=send_sems.at[k - 1], recv_sem=recv_sems.at[k - 1],
                device_id=(px, py, pc), device_id_type=MESH_ID)
            cp.start()
            copies.append(cp)
        for cp in copies:
            cp.wait()
        loc.wait()

    return pl.pallas_call(
        body, name=name,
        out_shape=jax.ShapeDtypeStruct((N_DEV,) + tuple(blk), x.dtype),
        in_specs=[pl.BlockSpec(memory_space=pl.ANY)],
        out_specs=pl.BlockSpec(memory_space=pl.ANY),
        scratch_shapes=[pltpu.SemaphoreType.DMA((N_DEV - 1,)), pltpu.SemaphoreType.DMA((N_DEV - 1,)),
                        pltpu.SemaphoreType.DMA],
        compiler_params=pltpu.CompilerParams(has_side_effects=True),
    )(x)


def _blk(ref, axis, j, n):
    return ref.at[(slice(None),) * axis + (pl.ds(j * n, n),)]


def _comm_copies(items, srcs, lands, send_sems, recv_sems, loc_sems):
    mx, my, mc = lax.axis_index("x"), lax.axis_index("y"), lax.axis_index("c")
    me = 4 * mx + 2 * my + mc
    local, remote = [], []
    for t, (kind, axis, n, sel) in enumerate(items):
        mine = _blk(lands[t], axis, me, n) if kind == "gather" else lands[t].at[(me,) + tuple(sel)]
        local.append(pltpu.make_async_copy(srcs[t] if kind == "gather" else _blk(srcs[t], axis, me, n), mine,
                                           loc_sems.at[t]))
        for k in range(1, N_DEV):
            px = 1 - mx if k & 4 else mx
            py = 1 - my if k & 2 else my
            pc = 1 - mc if k & 1 else mc
            src = srcs[t] if kind == "gather" else _blk(srcs[t], axis, 4 * px + 2 * py + pc, n)
            remote.append(pltpu.make_async_remote_copy(
                src_ref=src, dst_ref=mine, send_sem=send_sems.at[t * (N_DEV - 1) + k - 1],
                recv_sem=recv_sems.at[t * (N_DEV - 1) + k - 1], device_id=(px, py, pc), device_id_type=MESH_ID))
    return local, remote


def _comm_scratch(n_items):
    return [pltpu.SemaphoreType.DMA((n_items * (N_DEV - 1),)), pltpu.SemaphoreType.DMA((n_items * (N_DEV - 1),)),
            pltpu.SemaphoreType.DMA((n_items,))]


def _comm(items, srcs, lands, name):
    n = len(items)

    def body(*refs):
        local, remote = _comm_copies(items, refs[:n], refs[2 * n:3 * n], *refs[3 * n:])
        for cp in local + remote:
            cp.start()
        for cp in remote + local:
            cp.wait()

    hbm = pl.BlockSpec(memory_space=pl.ANY)
    return pl.pallas_call(
        body, name=name, out_shape=[jax.ShapeDtypeStruct(a.shape, a.dtype) for a in lands],
        in_specs=[hbm] * (2 * n), out_specs=[hbm] * n, input_output_aliases={n + t: t for t in range(n)},
        scratch_shapes=_comm_scratch(n), compiler_params=pltpu.CompilerParams(has_side_effects=True),
    )(*srcs, *lands)


def _matmul(a, b, mode, name, out_dtype=F32, tm=1024, tn=512, tk=None, add=None):
    halves = a.ndim == 3
    if mode == "tn":
        K, M = a.shape[-2], a.shape[-1] * (2 if halves else 1)
        N = b.shape[1]
    else:
        M, K = a.shape[-2], a.shape[-1] * (2 if halves else 1)
        N = b.shape[0] if mode == "nt" else b.shape[1]
    tm, tn = min(tm, M), min(tn, N)
    tk = K if tk is None else tk
    nk = K // tk
    assert M % tm == 0 and N % tn == 0 and K % tk == 0, (name, M, N, K, tm, tn, tk)
    dims = {"nn": NN, "nt": NT, "tn": TN}[mode]
    has_add = add is not None

    def body(*refs):
        if has_add:
            a_ref, b_ref, add_ref, o_ref, acc_ref = refs
        else:
            a_ref, b_ref, o_ref, acc_ref = refs
        k = pl.program_id(2)
        part = _dot(a_ref[...], b_ref[...], dims)

        @pl.when(k == 0)
        def _():
            acc_ref[...] = part

        @pl.when(k > 0)
        def _():
            acc_ref[...] += part

        @pl.when(k == nk - 1)
        def _():
            r = acc_ref[...]
            if has_add:
                r = r + add_ref[...]
            o_ref[...] = r.astype(o_ref.dtype)

    if halves and mode == "tn":
        per = a.shape[-1] // tm
        a_spec = pl.BlockSpec((None, tk, tm), lambda i, j, k: (i // per, k, i % per))
    elif halves:
        per = a.shape[-1] // tk
        a_spec = pl.BlockSpec((None, tm, tk), lambda i, j, k: (k // per, i, k % per))
    elif mode == "tn":
        a_spec = pl.BlockSpec((tk, tm), lambda i, j, k: (k, i))
    else:
        a_spec = pl.BlockSpec((tm, tk), lambda i, j, k: (i, k))
    if mode == "nt":
        b_spec = pl.BlockSpec((tn, tk), lambda i, j, k: (j, k))
    else:
        b_spec = pl.BlockSpec((tk, tn), lambda i, j, k: (k, j))
    o_spec = pl.BlockSpec((tm, tn), lambda i, j, k: (i, j))
    in_specs = [a_spec, b_spec] + ([o_spec] if has_add else [])
    args = (a, b) + ((add,) if has_add else ())
    return pl.pallas_call(
        body, name=name, grid=(M // tm, N // tn, nk),
        out_shape=jax.ShapeDtypeStruct((M, N), out_dtype),
        in_specs=in_specs, out_specs=o_spec,
        scratch_shapes=[pltpu.VMEM((tm, tn), F32)],
        compiler_params=_cp("parallel", "parallel", "arbitrary"),
    )(*args)


def _ada_fwd(c_all, w_ada, b_loc):
    n = w_ada.shape[2]

    def body(c_ref, w_ref, b_ref, o_ref):
        c = c_ref[...]
        o_ref[0] = _dot(c * _sigmoid(c), w_ref[0], NN) + b_ref[0]

    return pl.pallas_call(
        body, name="ada_fwd", grid=(DEPTH,),
        out_shape=jax.ShapeDtypeStruct((DEPTH, N_DEV, n), F32),
        in_specs=[pl.BlockSpec((N_DEV, D), lambda l: (0, 0)),
                  pl.BlockSpec((1, D, n), lambda l: (l, 0, 0)),
                  pl.BlockSpec((1, 1, n), lambda l: (l, 0, 0))],
        out_specs=pl.BlockSpec((1, N_DEV, n), lambda l: (l, 0, 0)),
        compiler_params=_cp("parallel"),
    )(c_all, w_ada, b_loc)


def _ada_bwd(c_all, dmod_all):
    n = dmod_all.shape[2]

    def body(c_ref, d_ref, o_ref):
        c = c_ref[...]
        o_ref[0] = _dot(c * _sigmoid(c), d_ref[0], TN)

    return pl.pallas_call(
        body, name="ada_bwd", grid=(DEPTH,),
        out_shape=jax.ShapeDtypeStruct((DEPTH, D, n), F32),
        in_specs=[pl.BlockSpec((N_DEV, D), lambda l: (0, 0)),
                  pl.BlockSpec((1, N_DEV, n), lambda l: (l, 0, 0))],
        out_specs=pl.BlockSpec((1, D, n), lambda l: (l, 0, 0)),
        compiler_params=_cp("parallel"),
    )(c_all, dmod_all)


ROW_TILE = 256


def _row_spec(w=D, col=0):
    return pl.BlockSpec((ROW_TILE, w), lambda i: (i, col))


def _vec_spec(w=D):
    return pl.BlockSpec((1, w), lambda i: (0, 0))


def _norm_fwd(x, g, scale, shift, name, m=None, gate=None):
    has_res = m is not None

    def body(*refs):
        if has_res:
            x_ref, m_ref, gate_ref, g_ref, sc_ref, sh_ref, xo_ref, h_ref = refs
            xv = x_ref[...] + gate_ref[...] * m_ref[...]
            xo_ref[...] = xv
        else:
            x_ref, g_ref, sc_ref, sh_ref, h_ref = refs
            xv = x_ref[...]
        r = lax.rsqrt(_mean1(xv * xv) + EPS)
        h_ref[...] = ((xv * r * g_ref[...]) * (1.0 + sc_ref[...]) + sh_ref[...]).astype(BF16)

    ins = [x] + ([m, gate] if has_res else []) + [g, scale, shift]
    in_specs = [_row_spec()] + ([_row_spec(), _vec_spec()] if has_res else []) + [_vec_spec()] * 3
    out_shape = [jax.ShapeDtypeStruct((T, D), BF16)]
    out_specs = [_row_spec()]
    if has_res:
        out_shape = [jax.ShapeDtypeStruct((T, D), F32)] + out_shape
        out_specs = [_row_spec()] + out_specs
    out = pl.pallas_call(body, name=name, grid=(T // ROW_TILE,), out_shape=out_shape, in_specs=in_specs,
                         out_specs=out_specs, compiler_params=_cp("parallel"))(*ins)
    return out if has_res else out[0]


def _norm_bwd(x, dh, dres, g, scale, shift, name):
    def body(x_ref, dh_ref, dres_ref, g_ref, sc_ref, sh_ref, dx_ref, st_ref):
        xv, dh_v, gv = x_ref[...], dh_ref[...], g_ref[...]
        r = lax.rsqrt(_mean1(xv * xv) + EPS)
        n = xv * r
        dy = dh_v * (1.0 + sc_ref[...])
        dn = dy * gv
        dx_ref[...] = r * (dn - n * _mean1(dn * n)) + dres_ref[...]

        @pl.when(pl.program_id(0) == 0)
        def _():
            st_ref[...] = jnp.zeros_like(st_ref)

        st_ref[0:1, :] += _sum0(dy * n)
        st_ref[1:2, :] += _sum0(dh_v * (n * gv))
        st_ref[2:3, :] += _sum0(dh_v)

    return pl.pallas_call(
        body, name=name, grid=(T // ROW_TILE,),
        out_shape=[jax.ShapeDtypeStruct((T, D), F32), jax.ShapeDtypeStruct((8, D), F32)],
        in_specs=[_row_spec(), _row_spec(), _row_spec(), _vec_spec(), _vec_spec(), _vec_spec()],
        out_specs=[_row_spec(), pl.BlockSpec((8, D), lambda i: (0, 0))],
        compiler_params=_cp("arbitrary"),
    )(x, dh, dres, g, scale, shift)


def _axpy(x, m, gate, name):
    def body(x_ref, m_ref, gate_ref, o_ref):
        o_ref[...] = x_ref[...] + gate_ref[...] * m_ref[...]

    return pl.pallas_call(
        body, name=name, grid=(T // ROW_TILE,), out_shape=jax.ShapeDtypeStruct((T, D), F32),
        in_specs=[_row_spec(), _row_spec(), _vec_spec()], out_specs=_row_spec(),
        compiler_params=_cp("parallel"))(x, m, gate)


def _gate_bwd(dx, m, gate, name):
    def body(dx_ref, m_ref, gate_ref, dm_ref, st_ref):
        dxv = dx_ref[...]
        dm_ref[...] = (gate_ref[...] * dxv).astype(BF16)

        @pl.when(pl.program_id(0) == 0)
        def _():
            st_ref[...] = jnp.zeros_like(st_ref)

        st_ref[0:1, :] += _sum0(dxv * m_ref[...])

    return pl.pallas_call(
        body, name=name, grid=(T // ROW_TILE,),
        out_shape=[jax.ShapeDtypeStruct((T, D), BF16), jax.ShapeDtypeStruct((8, D), F32)],
        in_specs=[_row_spec(), _row_spec(), _vec_spec()],
        out_specs=[_row_spec(), pl.BlockSpec((8, D), lambda i: (0, 0))],
        compiler_params=_cp("arbitrary"))(dx, m, gate)


def _loss_fwd_bwd(y, target):
    def body(y_ref, t_ref, l_ref, d_ref):
        e = y_ref[...] - t_ref[...]
        d_ref[...] = e * (1.0 / D)

        @pl.when(pl.program_id(0) == 0)
        def _():
            l_ref[...] = jnp.zeros_like(l_ref)

        l_ref[...] += jnp.sum(_sum0(e * e), axis=1, keepdims=True) * (0.5 / D)

    return pl.pallas_call(
        body, name="loss", grid=(T // ROW_TILE,),
        out_shape=[jax.ShapeDtypeStruct((8, 128), F32), jax.ShapeDtypeStruct((T, D), F32)],
        in_specs=[_row_spec(), _row_spec()],
        out_specs=[pl.BlockSpec((8, 128), lambda i: (0, 0)), _row_spec()],
        compiler_params=_cp("arbitrary"))(y, target)


def _rope_tables():
    half = HD // 2
    inv_freq = 10000.0 ** (-jnp.arange(half, dtype=F32) / half)
    ang = jnp.arange(T, dtype=F32)[:, None] * inv_freq[None, :]
    cos, sin = jnp.cos(ang), jnp.sin(ang)
    return jnp.concatenate([cos, cos], axis=1), jnp.concatenate([-sin, sin], axis=1)


def _rope_fwd(p, cosf, sinf):
    def body(p_ref, c_ref, s_ref, o_ref):
        cv, sv = c_ref[...], s_ref[...]
        for j in range(2 * N_HEADS):
            xv = p_ref[:, j * HD:(j + 1) * HD]
            rot = xv * cv + pltpu.roll(xv, HD // 2, 1) * sv
            if j >= N_HEADS:
                rot = rot * (HD ** -0.5)
            o_ref[:, j * HD:(j + 1) * HD] = rot.astype(BF16)

    return pl.pallas_call(
        body, name="rope_fwd", grid=(T // ROW_TILE,),
        out_shape=jax.ShapeDtypeStruct((T, 2 * BW), BF16),
        in_specs=[_row_spec(2 * BW), _row_spec(HD), _row_spec(HD)], out_specs=_row_spec(2 * BW),
        compiler_params=_cp("parallel"))(p, cosf, sinf)


def _rope_bwd(dq, dk, cosf, sinf):
    def body(dq_ref, dk_ref, c_ref, s_ref, o_ref):
        cv, sv = c_ref[...], s_ref[...]
        for j in range(2 * N_HEADS):
            h = j % N_HEADS
            d = dq_ref[:, h * HD:(h + 1) * HD] if j < N_HEADS else dk_ref[:, h * HD:(h + 1) * HD] * (HD ** -0.5)
            o_ref[:, j * HD:(j + 1) * HD] = d * cv + pltpu.roll(d * sv, HD // 2, 1)

    return pl.pallas_call(
        body, name="rope_bwd", grid=(T // ROW_TILE,),
        out_shape=jax.ShapeDtypeStruct((T, 2 * BW), F32),
        in_specs=[_row_spec(BW), _row_spec(BW), _row_spec(HD), _row_spec(HD)], out_specs=_row_spec(2 * BW),
        compiler_params=_cp("parallel"))(dq, dk, cosf, sinf)


TQ = 256
V_RET_BLK = 8


def _ret_logg():
    lg = jnp.log1p(-jnp.exp2(-5.0 - jnp.arange(N_HEADS, dtype=F32)))
    return jnp.broadcast_to(lg[:, None, None], (N_HEADS, 1, 128))


def _block_iotas(i, j):
    rows = lax.broadcasted_iota(jnp.int32, (TQ, TQ), 0) + i * TQ
    cols = lax.broadcasted_iota(jnp.int32, (TQ, TQ), 1) + j * TQ
    return rows, cols


def _ret_weight(lg_ref, i, j):
    rows, cols = _block_iotas(i, j)
    dist = jnp.abs(rows - cols).astype(F32)
    w = jnp.exp(dist * lg_ref[0][:, 0:1])
    return jnp.where((cols >> 6) <= (rows >> 6), w, 0.0)


def _key_rows(j):
    return pl.ds(pl.multiple_of(j * TQ, TQ), TQ)


def _ret_specs():
    q_spec = pl.BlockSpec((TQ, HD), lambda h, i: (i, h))
    k_spec = pl.BlockSpec((T, HD), lambda h, i: (0, N_HEADS + h))
    v_spec = pl.BlockSpec((T, HD), lambda h, i: (0, V_RET_BLK + h))
    lg_spec = pl.BlockSpec((1, 1, 128), lambda h, i: (h, 0, 0))
    return q_spec, k_spec, v_spec, lg_spec


def _ret_fwd(qk, p, logg):
    def body(q_ref, k_ref, v_ref, lg_ref, o_ref):
        i = pl.program_id(1)
        q = q_ref[...]

        def step(j, acc):
            kr = _key_rows(j)
            s = _dot(q, k_ref[kr, :], NT) * _ret_weight(lg_ref, i, j)
            return acc + _dot(s, v_ref[kr, :], NN)

        o_ref[...] = lax.fori_loop(0, i + 1, step, jnp.zeros((TQ, HD), F32))

    q_spec, k_spec, v_spec, lg_spec = _ret_specs()
    return pl.pallas_call(
        body, name="ret_fwd", grid=(N_HEADS, T // TQ),
        out_shape=jax.ShapeDtypeStruct((T, BW), F32),
        in_specs=[q_spec, k_spec, v_spec, lg_spec], out_specs=q_spec,
        compiler_params=_cp("parallel", "parallel"))(qk, qk, p, logg)


def _ret_bwd(qk, p, logg, do):
    def body(q_ref, k_ref, v_ref, lg_ref, do_ref, dq_ref, dk_ref, dv_ref):
        i = pl.program_id(1)
        q, dov = q_ref[...], do_ref[...]

        @pl.when(i == 0)
        def _():
            dk_ref[...] = jnp.zeros_like(dk_ref)
            dv_ref[...] = jnp.zeros_like(dv_ref)

        def step(j, dq):
            kr = _key_rows(j)
            w = _ret_weight(lg_ref, i, j)
            k = k_ref[kr, :]
            s = _dot(q, k, NT) * w
            ds = _dot(dov, v_ref[kr, :], NT) * w
            dk_ref[kr, :] += _dot(ds, q, TN)
            dv_ref[kr, :] += _dot(s, dov, TN)
            return dq + _dot(ds, k, NN)

        dq_ref[...] = lax.fori_loop(0, i + 1, step, jnp.zeros((TQ, HD), F32))

    q_spec, k_spec, v_spec, lg_spec = _ret_specs()
    acc_spec = pl.BlockSpec((T, HD), lambda h, i: (0, h))
    sh = jax.ShapeDtypeStruct((T, BW), F32)
    return pl.pallas_call(
        body, name="ret_bwd", grid=(N_HEADS, T // TQ),
        out_shape=[sh, sh, sh],
        in_specs=[q_spec, k_spec, v_spec, lg_spec, q_spec], out_specs=[q_spec, acc_spec, acc_spec],
        compiler_params=_cp("parallel", "arbitrary"))(qk, qk, p, logg, do)


def _post_norm(xv, gv, centered):
    if centered:
        xv = xv - _mean1(xv)
    r = lax.rsqrt(_mean1(xv * xv) + EPS)
    return xv * r, r


def _branch_post_fwd(raw, p, g, gate_blk, centered, name):
    def body(raw_ref, z_ref, g_ref, o_ref):
        for h in range(N_HEADS):
            sl = slice(h * HD, (h + 1) * HD)
            gv = g_ref[:, sl] if centered else g_ref[...]
            xh, _ = _post_norm(raw_ref[:, sl], gv, centered)
            z = z_ref[:, sl]
            o_ref[:, sl] = (z * _sigmoid(z) * (xh * gv)).astype(BF16)

    return pl.pallas_call(
        body, name=name, grid=(T // ROW_TILE,),
        out_shape=jax.ShapeDtypeStruct((T, BW), BF16),
        in_specs=[_row_spec(BW), _row_spec(BW, gate_blk), _vec_spec(BW if centered else HD)],
        out_specs=_row_spec(BW), compiler_params=_cp("parallel"))(raw, p, g)


def _branch_post_bwd(raw, p, g, dout, gate_blk, centered, name):
    gw = BW if centered else HD

    def body(raw_ref, z_ref, g_ref, do_ref, dr_ref, dz_ref, dg_ref):
        @pl.when(pl.program_id(0) == 0)
        def _():
            dg_ref[...] = jnp.zeros_like(dg_ref)

        for h in range(N_HEADS):
            sl = slice(h * HD, (h + 1) * HD)
            gsl = sl if centered else slice(0, HD)
            gv, z, dov = g_ref[:, gsl], z_ref[:, sl], do_ref[:, sl]
            xh, r = _post_norm(raw_ref[:, sl], gv, centered)
            sg = _sigmoid(z)
            dyn = dov * (z * sg)
            dz_ref[:, sl] = dov * (xh * gv) * (sg * (1.0 + z * (1.0 - sg)))
            dxh = dyn * gv
            t = dxh - xh * _mean1(dxh * xh)
            if centered:
                t = t - _mean1(dxh)
            dr_ref[:, sl] = r * t
            dg_ref[0:1, gsl] += _sum0(dyn * xh)

    return pl.pallas_call(
        body, name=name, grid=(T // ROW_TILE,),
        out_shape=[jax.ShapeDtypeStruct((T, BW), F32), jax.ShapeDtypeStruct((T, BW), F32),
                   jax.ShapeDtypeStruct((8, gw), F32)],
        in_specs=[_row_spec(BW), _row_spec(BW, gate_blk), _vec_spec(gw), _row_spec(BW)],
        out_specs=[_row_spec(BW), _row_spec(BW), pl.BlockSpec((8, gw), lambda i: (0, 0))],
        compiler_params=_cp("arbitrary"))(raw, p, g, dout)


GLA_ROWS = 256
GLA_CPB = GLA_ROWS // CHUNK
GLA_DK = 64
GLA_W = N_HEADS * GLA_DK
GQ_BLK, GK_BLK, GV_BLK, GG_BLK, TAIL_BLK = 8, 9, 5, 6, 40
RG_BLK = 3


def _gla_chunk_common(tl, w2, bv, kv):
    pre = _dot(tl, w2, NN) + bv
    la = _log_sigmoid(pre) * (1.0 / 16.0)
    bc = _exact_dot(_tri(CHUNK, False), la)
    be = bc[CHUNK - 1:CHUNK, :]
    w = jnp.exp(be - bc)
    return pre, w, jnp.exp(be), kv * w


def _head_masks():
    lane = lax.broadcasted_iota(jnp.int32, (1, GLA_W), 1)
    return [jnp.where((lane // GLA_DK) == h, 1.0, 0.0) for h in range(N_HEADS)]


def _gla_fwd(p, w2pad, b):
    nb = T // GLA_ROWS

    def body(q_ref, k_ref, v_ref, t_ref, w2_ref, b_ref, o_ref, st_ref, s_acc):
        @pl.when(pl.program_id(0) == 0)
        def _():
            s_acc[...] = jnp.zeros_like(s_acc)

        masks = _head_masks()
        for c in range(GLA_CPB):
            rows = slice(c * CHUNK, (c + 1) * CHUNK)
            _, _, a, kd = _gla_chunk_common(t_ref[rows, :], w2_ref[...], b_ref[...], k_ref[rows, :])
            q = q_ref[rows, :] * (GLA_DK ** -0.5)
            kv = None
            for h in range(N_HEADS):
                t = _dot(v_ref[rows, h * HD:(h + 1) * HD], kd * masks[h], TN)
                kv = t if kv is None else kv + t
            s_new = s_acc[...] * a + kv
            s_acc[...] = s_new
            st_ref[c] = s_new
            for h in range(N_HEADS):
                o_ref[rows, h * HD:(h + 1) * HD] = _dot(q * masks[h], s_new, NT)

    return pl.pallas_call(
        body, name="gla_fwd", grid=(nb,),
        out_shape=[jax.ShapeDtypeStruct((T, BW), F32), jax.ShapeDtypeStruct((T // CHUNK, HD, GLA_W), F32)],
        in_specs=[pl.BlockSpec((GLA_ROWS, GLA_W), lambda i: (i, GQ_BLK)),
                  pl.BlockSpec((GLA_ROWS, GLA_W), lambda i: (i, GK_BLK)),
                  pl.BlockSpec((GLA_ROWS, BW), lambda i: (i, GV_BLK)),
                  pl.BlockSpec((GLA_ROWS, 128), lambda i: (i, TAIL_BLK)),
                  pl.BlockSpec((128, GLA_W), lambda i: (0, 0)),
                  pl.BlockSpec((1, GLA_W), lambda i: (0, 0))],
        out_specs=[pl.BlockSpec((GLA_ROWS, BW), lambda i: (i, 0)),
                   pl.BlockSpec((GLA_CPB, HD, GLA_W), lambda i: (i, 0, 0))],
        scratch_shapes=[pltpu.VMEM((HD, GLA_W), F32)],
        compiler_params=_cp("arbitrary"))(p, p, p, p, w2pad, b)


def _gla_bwd(p, w2pad, b, states, do):
    nb = T // GLA_ROWS

    def body(q_ref, k_ref, v_ref, t_ref, w2_ref, b_ref, st_ref, prev_ref, do_ref,
             dq_ref, dk_ref, dv_ref, dt_ref, dw2_ref, db_ref, ds_acc):
        step = pl.program_id(0)

        @pl.when(step == 0)
        def _():
            ds_acc[...] = jnp.zeros_like(ds_acc)
            dw2_ref[...] = jnp.zeros_like(dw2_ref)
            db_ref[...] = jnp.zeros_like(db_ref)

        masks = _head_masks()
        up = _tri(CHUNK, True)
        has_prev = jnp.where(step == nb - 1, 0.0, 1.0)
        for c in reversed(range(GLA_CPB)):
            rows = slice(c * CHUNK, (c + 1) * CHUNK)
            tl, w2, k = t_ref[rows, :], w2_ref[...], k_ref[rows, :]
            pre, w, a, kd = _gla_chunk_common(tl, w2, b_ref[...], k)
            q = q_ref[rows, :] * (GLA_DK ** -0.5)
            s_n = st_ref[c]
            s_prev = st_ref[c - 1] if c > 0 else prev_ref[0] * has_prev
            ds = ds_acc[...]
            dos = [do_ref[rows, h * HD:(h + 1) * HD] for h in range(N_HEADS)]
            for h in range(N_HEADS):
                ds = ds + _dot(dos[h], q * masks[h], TN)
            dqp = jnp.zeros((CHUNK, GLA_W), F32)
            dkd = jnp.zeros((CHUNK, GLA_W), F32)
            for h in range(N_HEADS):
                dqp = dqp + masks[h] * _dot(dos[h], s_n, NN)
                dkd = dkd + masks[h] * _dot(v_ref[rows, h * HD:(h + 1) * HD], ds, NN)
                dv_ref[rows, h * HD:(h + 1) * HD] = _dot(kd * masks[h], ds, NT)
            dq_ref[rows, :] = dqp * (GLA_DK ** -0.5)
            dk_ref[rows, :] = dkd * w
            e = dkd * k * w
            dbe = _sum0(e) + _sum0(ds * s_prev) * a
            dla = dbe - _exact_dot(up, e)
            dpre = dla * (1.0 / 16.0) * _sigmoid(-pre)
            db_ref[0:1, :] += _sum0(dpre)
            dw2_ref[...] += _dot(tl, dpre, TN)
            dt_ref[rows, :] = _dot(dpre, w2, NT)
            ds_acc[...] = ds * a

    rev = lambda i: nb - 1 - i
    sh = lambda w: jax.ShapeDtypeStruct((T, w), F32)
    return pl.pallas_call(
        body, name="gla_bwd", grid=(nb,),
        out_shape=[sh(GLA_W), sh(GLA_W), sh(BW), sh(128), jax.ShapeDtypeStruct((128, GLA_W), F32),
                   jax.ShapeDtypeStruct((8, GLA_W), F32)],
        in_specs=[pl.BlockSpec((GLA_ROWS, GLA_W), lambda i: (rev(i), GQ_BLK)),
                  pl.BlockSpec((GLA_ROWS, GLA_W), lambda i: (rev(i), GK_BLK)),
                  pl.BlockSpec((GLA_ROWS, BW), lambda i: (rev(i), GV_BLK)),
                  pl.BlockSpec((GLA_ROWS, 128), lambda i: (rev(i), TAIL_BLK)),
                  pl.BlockSpec((128, GLA_W), lambda i: (0, 0)),
                  pl.BlockSpec((1, GLA_W), lambda i: (0, 0)),
                  pl.BlockSpec((GLA_CPB, HD, GLA_W), lambda i: (rev(i), 0, 0)),
                  pl.BlockSpec((1, HD, GLA_W), lambda i: (jnp.maximum(rev(i) * GLA_CPB - 1, 0), 0, 0)),
                  pl.BlockSpec((GLA_ROWS, BW), lambda i: (rev(i), 0))],
        out_specs=[pl.BlockSpec((GLA_ROWS, GLA_W), lambda i: (rev(i), 0)),
                   pl.BlockSpec((GLA_ROWS, GLA_W), lambda i: (rev(i), 0)),
                   pl.BlockSpec((GLA_ROWS, BW), lambda i: (rev(i), 0)),
                   pl.BlockSpec((GLA_ROWS, 128), lambda i: (rev(i), 0)),
                   pl.BlockSpec((128, GLA_W), lambda i: (0, 0)),
                   pl.BlockSpec((8, GLA_W), lambda i: (0, 0))],
        scratch_shapes=[pltpu.VMEM((HD, GLA_W), F32)],
        compiler_params=_cp("arbitrary"))(p, p, p, p, w2pad, b, states, states, do)


FQ_BLK, FK_BLK = 7, 8
V_FOX_BLK = 36


def _fox_prep_fwd(p, qg, kg, btail):
    def body(q_ref, k_ref, t_ref, qg_ref, kg_ref, bt_ref, o_ref, cum_ref, carry):
        @pl.when(pl.program_id(0) == 0)
        def _():
            carry[...] = jnp.zeros_like(carry)

        for src, gr, off in ((q_ref, qg_ref, 0), (k_ref, kg_ref, BW)):
            for h in range(N_HEADS):
                xv = src[:, h * HD:(h + 1) * HD]
                r = lax.rsqrt(_mean1(xv * xv) + EPS)
                o_ref[:, off + h * HD:off + (h + 1) * HD] = (xv * r * gr[...]).astype(BF16)
        logf = _log_sigmoid(t_ref[...] + bt_ref[...])
        cum = _exact_dot(_tri(ROW_TILE, False), logf) + carry[...]
        cum_ref[...] = cum
        carry[...] = cum[ROW_TILE - 1:ROW_TILE, :]

    return pl.pallas_call(
        body, name="fox_prep_fwd", grid=(T // ROW_TILE,),
        out_shape=[jax.ShapeDtypeStruct((T, 2 * BW), BF16), jax.ShapeDtypeStruct((T, 128), F32)],
        in_specs=[_row_spec(BW, FQ_BLK), _row_spec(BW, FK_BLK), _row_spec(128, TAIL_BLK),
                  _vec_spec(HD), _vec_spec(HD), _vec_spec(128)],
        out_specs=[_row_spec(2 * BW), _row_spec(128)],
        scratch_shapes=[pltpu.VMEM((1, 128), F32)],
        compiler_params=_cp("arbitrary"))(p, p, p, qg, kg, btail)


def _fox_prep_bwd(p, qg, kg, btail, dqn, dkn, dcum):
    nt = T // ROW_TILE

    def body(q_ref, k_ref, t_ref, qg_ref, kg_ref, bt_ref, dq_ref, dk_ref, dc_ref, o_ref, dt_ref, st_ref, carry):
        @pl.when(pl.program_id(0) == 0)
        def _():
            carry[...] = jnp.zeros_like(carry)
            st_ref[...] = jnp.zeros_like(st_ref)

        for row, (src, gr, dsrc, off) in enumerate(((q_ref, qg_ref, dq_ref, 0), (k_ref, kg_ref, dk_ref, BW))):
            for h in range(N_HEADS):
                xv = src[:, h * HD:(h + 1) * HD]
                dy = dsrc[:, h * HD:(h + 1) * HD]
                r = lax.rsqrt(_mean1(xv * xv) + EPS)
                n = xv * r
                dn = dy * gr[...]
                o_ref[:, off + h * HD:off + (h + 1) * HD] = r * (dn - n * _mean1(dn * n))
                st_ref[row:row + 1, :] += _sum0(dy * n)
        z = t_ref[...] + bt_ref[...]
        dlogf = _exact_dot(_tri(ROW_TILE, True), dc_ref[...]) + carry[...]
        carry[...] = dlogf[0:1, :]
        lane = lax.broadcasted_iota(jnp.int32, (1, 128), 1)
        keep = (lane >= FF_LANE0) & (lane < FF_LANE0 + N_HEADS)
        dz = jnp.where(keep, dlogf * _sigmoid(-z), 0.0)
        dt_ref[...] = dz
        st_ref[2:3, :] += _sum0(dz)

    rs = lambda w, col=0: pl.BlockSpec((ROW_TILE, w), lambda i: (nt - 1 - i, col))
    return pl.pallas_call(
        body, name="fox_prep_bwd", grid=(nt,),
        out_shape=[jax.ShapeDtypeStruct((T, 2 * BW), F32), jax.ShapeDtypeStruct((T, 128), F32),
                   jax.ShapeDtypeStruct((8, 128), F32)],
        in_specs=[rs(BW, FQ_BLK), rs(BW, FK_BLK), rs(128, TAIL_BLK), _vec_spec(HD), _vec_spec(HD), _vec_spec(128),
                  rs(BW), rs(BW), rs(128)],
        out_specs=[rs(2 * BW), rs(128), pl.BlockSpec((8, 128), lambda i: (0, 0))],
        scratch_shapes=[pltpu.VMEM((1, 128), F32)],
        compiler_params=_cp("arbitrary"))(p, p, p, qg, kg, btail, dqn, dkn, dcum)


def _fox_logits(q, k_ref, cc, cr_ref, i, j):
    rows, cols = _block_iotas(i, j)
    kr = _key_rows(j)
    s = _dot(q, k_ref[kr, :], NT) * (HD ** -0.5) + cc - cr_ref[0, :, kr]
    return jnp.where(cols <= rows, s, -1e30)


def _fox_specs():
    q_spec = pl.BlockSpec((TQ, HD), lambda h, i: (i, h))
    k_spec = pl.BlockSpec((T, HD), lambda h, i: (0, N_HEADS + h))
    v_spec = pl.BlockSpec((T, HD), lambda h, i: (0, V_FOX_BLK + h))
    col_spec = pl.BlockSpec((1, TQ, 1), lambda h, i: (h, i, 0))
    row_spec = pl.BlockSpec((1, 1, T), lambda h, i: (h, 0, 0))
    return q_spec, k_spec, v_spec, col_spec, row_spec


def _fox_fwd(qkn, p, cumcol, cumrow):
    def body(q_ref, k_ref, v_ref, cc_ref, cr_ref, o_ref, lse_ref):
        i = pl.program_id(1)
        q, cc = q_ref[...], cc_ref[0]

        def step(j, carry):
            m, l, acc = carry
            s = _fox_logits(q, k_ref, cc, cr_ref, i, j)
            m_new = jnp.maximum(m, jnp.max(s, axis=-1, keepdims=True))
            alpha = jnp.exp(m - m_new)
            e = jnp.exp(s - m_new)
            l = alpha * l + jnp.sum(e, axis=-1, keepdims=True)
            acc = alpha * acc + _dot(e, v_ref[_key_rows(j), :], NN)
            return m_new, l, acc

        init = (jnp.full((TQ, 1), -1e30, F32), jnp.zeros((TQ, 1), F32), jnp.zeros((TQ, HD), F32))
        m, l, acc = lax.fori_loop(0, i + 1, step, init)
        o_ref[...] = acc / l
        lse_ref[0] = m + jnp.log(l)

    q_spec, k_spec, v_spec, col_spec, row_spec = _fox_specs()
    return pl.pallas_call(
        body, name="fox_fwd", grid=(N_HEADS, T // TQ),
        out_shape=[jax.ShapeDtypeStruct((T, BW), F32), jax.ShapeDtypeStruct((N_HEADS, T, 1), F32)],
        in_specs=[q_spec, k_spec, v_spec, col_spec, row_spec], out_specs=[q_spec, col_spec],
        compiler_params=_cp("parallel", "parallel"))(qkn, qkn, p, cumcol, cumrow)


def _fox_bwd(qkn, p, cumcol, cumrow, lse, o, do):
    def body(q_ref, k_ref, v_ref, cc_ref, cr_ref, lse_ref, o_ref, do_ref, dq_ref, dk_ref, dv_ref, dr_ref, dc_ref):
        i = pl.program_id(1)
        q, dov, cc, lse_v = q_ref[...], do_ref[...], cc_ref[0], lse_ref[0]
        delta = jnp.sum(o_ref[...] * dov, axis=-1, keepdims=True)

        @pl.when(i == 0)
        def _():
            dk_ref[...] = jnp.zeros_like(dk_ref)
            dv_ref[...] = jnp.zeros_like(dv_ref)
            dc_ref[...] = jnp.zeros_like(dc_ref)

        def step(j, carry):
            dq, dr = carry
            kr = _key_rows(j)
            k = k_ref[kr, :]
            pm = jnp.exp(_fox_logits(q, k_ref, cc, cr_ref, i, j) - lse_v)
            ds = pm * (_dot(dov, v_ref[kr, :], NT) - delta)
            dk_ref[kr, :] += _dot(ds, q, TN) * (HD ** -0.5)
            dv_ref[kr, :] += _dot(pm, dov, TN)
            dc_ref[0, :, kr] += _sum0(ds)
            return dq + _dot(ds, k, NN), dr + jnp.sum(ds, axis=-1, keepdims=True)

        dq, dr = lax.fori_loop(0, i + 1, step, (jnp.zeros((TQ, HD), F32), jnp.zeros((TQ, 1), F32)))
        dq_ref[...] = dq * (HD ** -0.5)
        dr_ref[0] = dr

    q_spec, k_spec, v_spec, col_spec, row_spec = _fox_specs()
    acc_spec = pl.BlockSpec((T, HD), lambda h, i: (0, h))
    sh = jax.ShapeDtypeStruct((T, BW), F32)
    return pl.pallas_call(
        body, name="fox_bwd", grid=(N_HEADS, T // TQ),
        out_shape=[sh, sh, sh, jax.ShapeDtypeStruct((N_HEADS, T, 1), F32), jax.ShapeDtypeStruct((N_HEADS, 1, T), F32)],
        in_specs=[q_spec, k_spec, v_spec, col_spec, row_spec, col_spec, q_spec, q_spec],
        out_specs=[q_spec, acc_spec, acc_spec, col_spec, row_spec],
        compiler_params=_cp("parallel", "arbitrary"))(qkn, qkn, p, cumcol, cumrow, lse, o, do)


def _mix_fwd(gpre, b_mg, y0, y1, y2):
    def body(g_ref, b_ref, y0_ref, y1_ref, y2_ref, o_ref):
        acc = None
        for n, y_ref in enumerate((y0_ref, y1_ref, y2_ref)):
            sl = slice(n * D, (n + 1) * D)
            t = _sigmoid(g_ref[:, sl] + b_ref[:, sl]) * y_ref[...]
            acc = t if acc is None else acc + t
        o_ref[...] = acc.astype(BF16)

    return pl.pallas_call(
        body, name="mix_fwd", grid=(T // ROW_TILE,), out_shape=jax.ShapeDtypeStruct((T, D), BF16),
        in_specs=[_row_spec(3 * D), _vec_spec(3 * D), _row_spec(), _row_spec(), _row_spec()],
        out_specs=_row_spec(), compiler_params=_cp("parallel"))(gpre, b_mg, y0, y1, y2)


def _mix_bwd(gpre, b_mg, y0, y1, y2, dmi):
    def body(g_ref, b_ref, y0_ref, y1_ref, y2_ref, d_ref, dy0_ref, dy1_ref, dy2_ref, dg_ref, db_ref):
        @pl.when(pl.program_id(0) == 0)
        def _():
            db_ref[...] = jnp.zeros_like(db_ref)

        dv = d_ref[...]
        for n, (y_ref, dy_ref) in enumerate(((y0_ref, dy0_ref), (y1_ref, dy1_ref), (y2_ref, dy2_ref))):
            sl = slice(n * D, (n + 1) * D)
            sg = _sigmoid(g_ref[:, sl] + b_ref[:, sl])
            dy_ref[...] = (dv * sg).astype(BF16)
            dpre = dv * y_ref[...] * (sg * (1.0 - sg))
            dg_ref[:, sl] = dpre.astype(BF16)
            db_ref[0:1, sl] += _sum0(dpre)

    shb = jax.ShapeDtypeStruct((T, D), BF16)
    return pl.pallas_call(
        body, name="mix_bwd", grid=(T // ROW_TILE,),
        out_shape=[shb, shb, shb, jax.ShapeDtypeStruct((T, 3 * D), BF16), jax.ShapeDtypeStruct((8, 3 * D), F32)],
        in_specs=[_row_spec(3 * D), _vec_spec(3 * D), _row_spec(), _row_spec(), _row_spec(), _row_spec()],
        out_specs=[_row_spec(), _row_spec(), _row_spec(), _row_spec(3 * D), pl.BlockSpec((8, 3 * D), lambda i: (0, 0))],
        compiler_params=_cp("arbitrary"))(gpre, b_mg, y0, y1, y2, dmi)


FF_COLS = 256
FF_NBLK = D_FF // FF_COLS


def _shift_rows(a, n):
    rows = lax.broadcasted_iota(jnp.int32, a.shape, 0)
    rolled = pltpu.roll(a, n % T, 0)
    return jnp.where((rows >= n) if n > 0 else (rows < T + n), rolled, 0.0)


def _ffn_act_fwd(uu, w_conv, b_conv):
    def body(u_ref, g_ref, w_ref, b_ref, o_ref):
        u = u_ref[...]
        w = w_ref[...]
        uc = b_ref[...] + w[0:1, :] * _shift_rows(u, 2) + w[1:2, :] * _shift_rows(u, 1) + w[2:3, :] * u
        o_ref[...] = (uc * _sigmoid(uc) * g_ref[...]).astype(BF16)

    return pl.pallas_call(
        body, name="ffn_act_fwd", grid=(FF_NBLK,), out_shape=jax.ShapeDtypeStruct((T, D_FF), BF16),
        in_specs=[pl.BlockSpec((T, FF_COLS), lambda j: (0, j)), pl.BlockSpec((T, FF_COLS), lambda j: (0, FF_NBLK + j)),
                  pl.BlockSpec((3, FF_COLS), lambda j: (0, j)), pl.BlockSpec((1, FF_COLS), lambda j: (0, j))],
        out_specs=pl.BlockSpec((T, FF_COLS), lambda j: (0, j)),
        compiler_params=_cp("parallel"))(uu, uu, w_conv, b_conv)


def _ffn_act_bwd(uu, w_conv, b_conv, da):
    def body(u_ref, g_ref, w_ref, b_ref, da_ref, d_ref, st_ref):
        u, w, dav = u_ref[...], w_ref[...], da_ref[...]
        u1, u2 = _shift_rows(u, 1), _shift_rows(u, 2)
        uc = b_ref[...] + w[0:1, :] * u2 + w[1:2, :] * u1 + w[2:3, :] * u
        sg = _sigmoid(uc)
        d_ref[1] = (dav * (uc * sg)).astype(BF16)
        duc = dav * g_ref[...] * (sg * (1.0 + uc * (1.0 - sg)))
        du = w[2:3, :] * duc + w[1:2, :] * _shift_rows(duc, -1) + w[0:1, :] * _shift_rows(duc, -2)
        d_ref[0] = du.astype(BF16)
        st_ref[...] = jnp.zeros_like(st_ref)
        st_ref[0:1, :] = _sum0(duc * u2)
        st_ref[1:2, :] = _sum0(duc * u1)
        st_ref[2:3, :] = _sum0(duc * u)
        st_ref[3:4, :] = _sum0(duc)

    cb = lambda rows=T, off=0: pl.BlockSpec((rows, FF_COLS), lambda j: (0, off + j))
    return pl.pallas_call(
        body, name="ffn_act_bwd", grid=(FF_NBLK,),
        out_shape=[jax.ShapeDtypeStruct((2, T, D_FF), BF16), jax.ShapeDtypeStruct((8, D_FF), F32)],
        in_specs=[cb(), cb(T, FF_NBLK), cb(3), cb(1), cb()],
        out_specs=[pl.BlockSpec((2, T, FF_COLS), lambda j: (0, 0, j)), cb(8)],
        compiler_params=_cp("parallel"))(uu, uu, w_conv, b_conv, da)


def _adamw(g, w, m, v, tr, name):
    partial = g.ndim == 3
    R, C = w.shape
    tr = R if tr is None else tr
    assert R % tr == 0

    def body(g_ref, w_ref, m_ref, v_ref, go_ref, d_ref, mo_ref, vo_ref):
        if partial:
            gv = g_ref[0].astype(F32)
            for j in range(1, N_DEV):
                gv = gv + g_ref[j].astype(F32)
        else:
            gv = g_ref[...]
        go_ref[...] = gv
        mn = ADAM_B1 * m_ref[...] + (1.0 - ADAM_B1) * gv
        vn = ADAM_B2 * v_ref[...] + (1.0 - ADAM_B2) * (gv * gv)
        mo_ref[...] = mn
        vo_ref[...] = vn
        m_hat = mn / (1.0 - ADAM_B1 ** ADAM_STEP)
        v_hat = vn / (1.0 - ADAM_B2 ** ADAM_STEP)
        d_ref[...] = -ADAM_LR * (m_hat / (jnp.sqrt(v_hat) + ADAM_EPS) + ADAM_WD * w_ref[...])

    spec = pl.BlockSpec((tr, C), lambda i: (i, 0))
    g_spec = pl.BlockSpec((N_DEV, tr, C), lambda i: (0, i, 0)) if partial else spec
    sh = jax.ShapeDtypeStruct((R, C), F32)
    return pl.pallas_call(
        body, name=name, grid=(R // tr,), out_shape=[sh, sh, sh, sh],
        in_specs=[g_spec, spec, spec, spec], out_specs=[spec, spec, spec, spec],
        compiler_params=_cp("parallel"))(g, w, m, v)


def _sum_partials(g, name, tr=None):
    _, R, C = g.shape
    tr = R if tr is None else tr
    assert R % tr == 0

    def body(g_ref, o_ref):
        acc = g_ref[0].astype(F32)
        for j in range(1, N_DEV):
            acc = acc + g_ref[j].astype(F32)
        o_ref[...] = acc

    return pl.pallas_call(
        body, name=name, grid=(R // tr,), out_shape=jax.ShapeDtypeStruct((R, C), F32),
        in_specs=[pl.BlockSpec((N_DEV, tr, C), lambda i: (0, i, 0))], out_specs=pl.BlockSpec((tr, C), lambda i: (i, 0)),
        compiler_params=_cp("parallel"))(g)


def _permute_in(w):
    pad = jnp.zeros(w.shape[:-1] + (NP - IN_W,), w.dtype)
    return jnp.concatenate([w[..., :3072], w[..., 3088:5136], w[..., 3072:3088], w[..., 5136:5140], pad], axis=-1)


def _unpermute_in(w):
    return jnp.concatenate([w[..., :3072], w[..., 5120:5136], w[..., 3072:5120], w[..., 5136:5140]], axis=-1)


def _flat_pack(arrs):
    flat = jnp.concatenate([a.reshape(-1).astype(F32) for a in arrs])
    n = flat.shape[0]
    rows = -(-n // 1024) * 8
    return jnp.pad(flat, (0, rows * 128 - n)).reshape(rows, 128)


def _flat_unpack(buf, shapes):
    flat = buf.reshape(-1)
    out, off = [], 0
    for s in shapes:
        n = int(np.prod(s))
        out.append(flat[off:off + n].reshape(s))
        off += n
    return out


def _layer_fwd(x0, wl, consts):
    cosf, sinf, logg = consts
    row = lambda a: a.reshape(1, -1)
    h = _norm_fwd(x0, row(wl["norm1_g"]), row(wl["scale1"]), row(wl["shift1"]), "norm1_fwd")
    p = _matmul(h, wl["w_in"], "nn", "in_proj")
    qk = _rope_fwd(p, cosf, sinf)
    ret_raw = _ret_fwd(qk, p, logg)
    br0 = _branch_post_fwd(ret_raw, p, row(wl["ret_norm_g"]), RG_BLK, True, "ret_post_fwd")
    gla_raw, states = _gla_fwd(p, wl["w2pad"], row(wl["b_gla_a"]))
    br1 = _branch_post_fwd(gla_raw, p, row(wl["gla_norm_g"]), GG_BLK, False, "gla_post_fwd")
    qkn, cum = _fox_prep_fwd(p, row(wl["q_norm_g"]), row(wl["k_norm_g"]), row(wl["btail"]))
    cum4 = cum[:, FF_LANE0:FF_LANE0 + N_HEADS].T
    cumcol, cumrow = cum4.reshape(N_HEADS, T, 1), cum4.reshape(N_HEADS, 1, T)
    fox_o, lse = _fox_fwd(qkn, p, cumcol, cumrow)
    ys = [_matmul(b, wl["w_br_t"][n], "nt", "br_proj%d" % n) for n, b in enumerate((br0, br1, fox_o))]
    gpre = _matmul(h, wl["w_mg_t"], "nt", "gate_proj")
    mixed_in = _mix_fwd(gpre, row(wl["b_mg"]), *ys)
    mixed = _matmul(mixed_in, wl["w_o"], "nn", "o_proj")
    x1, h2 = _norm_fwd(x0, row(wl["norm2_g"]), row(wl["scale2"]), row(wl["shift2"]), "norm2_fwd",
                       m=mixed, gate=row(wl["gate1"]))
    uu = _matmul(h2, wl["w_up_t"], "nt", "up_proj")
    act = _ffn_act_fwd(uu, wl["w_conv"], row(wl["b_conv"]))
    y = _matmul(act, wl["w_down"], "nn", "down_proj", tk=1408)
    x2 = _axpy(x1, y, row(wl["gate2"]), "resid2")
    saved = dict(x0=x0, h=h, p=p, qk=qk, ret_raw=ret_raw, br0=br0, gla_raw=gla_raw, states=states, br1=br1,
                 qkn=qkn, cumcol=cumcol, cumrow=cumrow, fox_o=fox_o, lse=lse, y0=ys[0], y1=ys[1], y2=ys[2],
                 gpre=gpre, mixed_in=mixed_in, mixed=mixed, x1=x1, h2=h2, uu=uu, act=act, y=y)
    return x2, saved


def _layer_bwd(dx2, wl, sv, consts):
    cosf, sinf, logg = consts
    row = lambda a: a.reshape(1, -1)
    dy, st_g2 = _gate_bwd(dx2, sv["y"], row(wl["gate2"]), "gate2_bwd")
    dact = _matmul(dy, wl["w_down"], "nt", "down_dx", tn=1408)
    d_down = _matmul(sv["act"], dy, "tn", "down_dw", out_dtype=BF16, tm=1408)
    duu, st_conv = _ffn_act_bwd(sv["uu"], wl["w_conv"], row(wl["b_conv"]), dact)
    dh2 = _matmul(duu, wl["w_up_t"], "nn", "up_dx", tk=1408)
    d_up_t = _matmul(duu, sv["h2"], "tn", "up_dw", out_dtype=BF16, tm=1408)
    dx1, st_n2 = _norm_bwd(sv["x1"], dh2, dx2, row(wl["norm2_g"]), row(wl["scale2"]), row(wl["shift2"]), "norm2_bwd")
    dmixed, st_g1 = _gate_bwd(dx1, sv["mixed"], row(wl["gate1"]), "gate1_bwd")
    dmi = _matmul(dmixed, wl["w_o"], "nt", "o_dx")
    d_o = _matmul(sv["mixed_in"], dmixed, "tn", "o_dw", out_dtype=BF16)
    dy0, dy1, dy2, dgpre, st_bmg = _mix_bwd(sv["gpre"], row(wl["b_mg"]), sv["y0"], sv["y1"], sv["y2"], dmi)
    brs = (sv["br0"], sv["br1"], sv["fox_o"])
    dbr = [_matmul(d, wl["w_br_t"][n], "nn", "br_dx%d" % n) for n, d in enumerate((dy0, dy1, dy2))]
    d_br_t = [_matmul(d, brs[n], "tn", "br_dw%d" % n, out_dtype=BF16) for n, d in enumerate((dy0, dy1, dy2))]
    dh = _matmul(dgpre, wl["w_mg_t"], "nn", "gate_dx", tk=1024)
    d_mg_t = _matmul(dgpre, sv["h"], "tn", "gate_dw", out_dtype=BF16)
    p = sv["p"]
    dqn, dkn, dfv, drow, dcol = _fox_bwd(sv["qkn"], p, sv["cumcol"], sv["cumrow"], sv["lse"], sv["fox_o"], dbr[2])
    dcum4 = drow.reshape(N_HEADS, T) - dcol.reshape(N_HEADS, T)
    dcum = jnp.pad(dcum4.T, ((0, 0), (FF_LANE0, 128 - FF_LANE0 - N_HEADS)))
    dfqk, dtail_fox, st_fox = _fox_prep_bwd(p, row(wl["q_norm_g"]), row(wl["k_norm_g"]), row(wl["btail"]), dqn, dkn, dcum)
    dgla_raw, dgg, st_gn = _branch_post_bwd(sv["gla_raw"], p, row(wl["gla_norm_g"]), dbr[1], GG_BLK, False, "gla_post_bwd")
    dgq, dgk, dgv, dtail_gla, dw2pad, st_bg = _gla_bwd(p, wl["w2pad"], row(wl["b_gla_a"]), sv["states"], dgla_raw)
    dret_raw, drg, st_rn = _branch_post_bwd(sv["ret_raw"], p, row(wl["ret_norm_g"]), dbr[0], RG_BLK, True, "ret_post_bwd")
    dqr, dkr, drv = _ret_bwd(sv["qk"], p, logg, dret_raw)
    drqk = _rope_bwd(dqr, dkr, cosf, sinf)
    dp = jnp.concatenate([a.astype(BF16) for a in (drqk, drv, drg, dgq, dgk, dgv, dgg, dfqk, dfv, dtail_fox + dtail_gla)]
                         + [jnp.zeros((T, NP - TAIL0 - 128), BF16)], axis=1)
    dh = _matmul(dp, wl["w_in"], "nt", "in_dx", tk=1408, add=dh)
    d_in = _matmul(sv["h"], dp, "tn", "in_dw", out_dtype=BF16)
    dx0, st_n1 = _norm_bwd(sv["x0"], dh, dx1, row(wl["norm1_g"]), row(wl["scale1"]), row(wl["shift1"]), "norm1_bwd")
    big = dict(w_in=d_in, w_o=d_o, w_down=d_down, w_br0=d_br_t[0], w_br1=d_br_t[1], w_br2=d_br_t[2], w_mg=d_mg_t,
               w_up=d_up_t)
    dmod = jnp.concatenate([st_n1[2], st_n1[1], st_g1[0], st_n2[2], st_n2[1], st_g2[0]])
    small = dict(norm1_g=st_n1[0], norm2_g=st_n2[0], b_gla_a=st_bg[0], b_fox_f=st_fox[2, FF_LANE0:FF_LANE0 + N_HEADS],
                 ret_norm_g=st_rn[0], gla_norm_g=st_gn[0], q_norm_g=st_fox[0], k_norm_g=st_fox[1], b_mg=st_bmg[0],
                 b_conv=st_conv[3], w_gla_a2=dw2pad[:LR_LANES], w_conv=st_conv[0:3])
    return dx0, big, dmod, small


SMALL_REPL = ("norm1_g", "norm2_g", "b_ada", "b_gla_a", "b_fox_f", "ret_norm_g", "gla_norm_g", "q_norm_g", "k_norm_g",
              "b_mg", "b_conv")
SMALL_SHARDED = ("w_gla_a2", "w_conv")
BIG = ("w_in", "w_o", "w_down", "w_br", "w_mg", "w_up")
WEIGHTS = ("norm1_g", "norm2_g", "w_ada", "b_ada", "w_in", "w_gla_a2", "b_gla_a", "b_fox_f", "ret_norm_g", "gla_norm_g",
           "q_norm_g", "k_norm_g", "w_br", "w_mg", "b_mg", "w_o", "w_up", "w_conv", "b_conv", "w_down")


def kernel(x, c, norm1_g, norm2_g, w_ada, b_ada, w_in, w_gla_a2, b_gla_a, b_fox_f, ret_norm_g, gla_norm_g, q_norm_g, k_norm_g, w_br, w_mg, b_mg, w_o, w_up, w_conv, b_conv, w_down, loss_target, m_norm1_g, m_norm2_g, m_w_ada, m_b_ada, m_w_in, m_w_gla_a2, m_b_gla_a, m_b_fox_f, m_ret_norm_g, m_gla_norm_g, m_q_norm_g, m_k_norm_g, m_w_br, m_w_mg, m_b_mg, m_w_o, m_w_up, m_w_conv, m_b_conv, m_w_down, v_norm1_g, v_norm2_g, v_w_ada, v_b_ada, v_w_in, v_w_gla_a2, v_b_gla_a, v_b_fox_f, v_ret_norm_g, v_gla_norm_g, v_q_norm_g, v_k_norm_g, v_w_br, v_w_mg, v_b_mg, v_w_o, v_w_up, v_w_conv, v_b_conv, v_w_down):
    W = dict(norm1_g=norm1_g, norm2_g=norm2_g, w_ada=w_ada, b_ada=b_ada, w_in=w_in, w_gla_a2=w_gla_a2, b_gla_a=b_gla_a,
             b_fox_f=b_fox_f, ret_norm_g=ret_norm_g, gla_norm_g=gla_norm_g, q_norm_g=q_norm_g, k_norm_g=k_norm_g,
             w_br=w_br, w_mg=w_mg, b_mg=b_mg, w_o=w_o, w_up=w_up, w_conv=w_conv, b_conv=b_conv, w_down=w_down)
    M = dict(norm1_g=m_norm1_g, norm2_g=m_norm2_g, w_ada=m_w_ada, b_ada=m_b_ada, w_in=m_w_in, w_gla_a2=m_w_gla_a2,
             b_gla_a=m_b_gla_a, b_fox_f=m_b_fox_f, ret_norm_g=m_ret_norm_g, gla_norm_g=m_gla_norm_g, q_norm_g=m_q_norm_g,
             k_norm_g=m_k_norm_g, w_br=m_w_br, w_mg=m_w_mg, b_mg=m_b_mg, w_o=m_w_o, w_up=m_w_up, w_conv=m_w_conv,
             b_conv=m_b_conv, w_down=m_w_down)
    V = dict(norm1_g=v_norm1_g, norm2_g=v_norm2_g, w_ada=v_w_ada, b_ada=v_b_ada, w_in=v_w_in, w_gla_a2=v_w_gla_a2,
             b_gla_a=v_b_gla_a, b_fox_f=v_b_fox_f, ret_norm_g=v_ret_norm_g, gla_norm_g=v_gla_norm_g, q_norm_g=v_q_norm_g,
             k_norm_g=v_k_norm_g, w_br=v_w_br, w_mg=v_w_mg, b_mg=v_b_mg, w_o=v_w_o, w_up=v_w_up, w_conv=v_w_conv,
             b_conv=v_b_conv, w_down=v_w_down)
    me = 4 * lax.axis_index("x") + 2 * lax.axis_index("y") + lax.axis_index("c")
    x2d, tgt = x.reshape(T, D), loss_target.reshape(T, D)

    sm = _flat_pack([c, w_gla_a2, w_conv])
    sm_all = _exchange(sm, True, "gather_small")
    parts = [_flat_unpack(sm_all[j], [(D,), (DEPTH, LR_LANES, 32), (DEPTH, 3, 352)]) for j in range(N_DEV)]
    c_all = jnp.stack([q[0] for q in parts])
    w_gla_full = jnp.concatenate([q[1] for q in parts], axis=2)
    w_conv_full = jnp.concatenate([q[2] for q in parts], axis=2)

    n_ada = w_ada.shape[2]
    b_loc = lax.dynamic_slice_in_dim(b_ada, me * n_ada, n_ada, axis=1).reshape(DEPTH, 1, n_ada)
    mod_all = _ada_fwd(c_all, w_ada, b_loc)
    mod_recv = _exchange(jnp.swapaxes(mod_all, 0, 1), False, "a2a_mod")
    mod = jnp.swapaxes(mod_recv, 0, 1).reshape(DEPTH, 6, D)

    loc = dict(w_in=_permute_in(w_in), w_o=w_o, w_down=w_down, w_br=jnp.swapaxes(w_br, 2, 3),
               w_mg=jnp.swapaxes(w_mg, 1, 2), w_up=jnp.swapaxes(w_up, 1, 2))
    loc = {k: v.astype(BF16) for k, v in loc.items()}
    full_shapes = dict(w_in=(D, NP), w_o=(D, D), w_down=(D_FF, D), w_br=(3, D, BW), w_mg=(3 * D, D), w_up=(2 * D_FF, D))
    gather_items = [("gather", 1 if k == "w_br" else 0, full_shapes[k][1 if k == "w_br" else 0] // N_DEV, ()) for k in BIG]
    full = [dict(zip(BIG, _comm(gather_items, [loc[k][l] for k in BIG],
                                [lax.empty(full_shapes[k], BF16) for k in BIG], "gather_w%d" % l)))
            for l in range(DEPTH)]

    w2pad = jnp.pad(w_gla_full, ((0, 0), (0, 128 - LR_LANES), (0, 0)))
    btail = jnp.pad(b_fox_f, ((0, 0), (FF_LANE0, 128 - FF_LANE0 - N_HEADS)))
    stacked = dict(norm1_g=norm1_g, norm2_g=norm2_g, b_gla_a=b_gla_a, ret_norm_g=ret_norm_g, gla_norm_g=gla_norm_g,
                   q_norm_g=q_norm_g, k_norm_g=k_norm_g, b_mg=b_mg, b_conv=b_conv, w_conv=w_conv_full, w2pad=w2pad,
                   btail=btail, shift1=mod[:, 0], scale1=mod[:, 1], gate1=mod[:, 2], shift2=mod[:, 3], scale2=mod[:, 4],
                   gate2=mod[:, 5])
    layers = [dict({k: v[l] for k, v in stacked.items()}, w_in=full[l]["w_in"], w_o=full[l]["w_o"],
                   w_down=full[l]["w_down"], w_br_t=full[l]["w_br"], w_mg_t=full[l]["w_mg"], w_up_t=full[l]["w_up"])
              for l in range(DEPTH)]
    consts = _rope_tables() + (_ret_logg(),)

    xc, saved = x2d, []
    for l in range(DEPTH):
        xc, sv = _layer_fwd(xc, layers[l], consts)
        saved.append(sv)
    loss_part, dxc = _loss_fwd_bwd(xc, tgt)
    loss = lax.psum(loss_part[0, 0], ("x", "y", "c"))

    grad_names = ("w_in", "w_o", "w_down", "w_br0", "w_br1", "w_br2", "w_mg", "w_up")
    blk_rows = dict(w_in=(128, NP), w_o=(128, D), w_down=(352, D), w_br0=(128, BW), w_br1=(128, BW), w_br2=(128, BW),
                    w_mg=(384, D), w_up=(704, D))
    recv = [lax.empty((N_DEV, DEPTH) + blk_rows[k], BF16) for k in grad_names]
    dmod, small_g = [None] * DEPTH, [None] * DEPTH
    for l in reversed(range(DEPTH)):
        dxc, big, dmod[l], small_g[l] = _layer_bwd(dxc, layers[l], saved[l], consts)
        items = [("a2a", 0, blk_rows[k][0], (l,)) for k in grad_names]
        recv = _comm(items, [big[k] for k in grad_names], recv, "a2a_grads%d" % l)
    grad_x = dxc
    dmod = jnp.stack(dmod)
    small_g = {k: jnp.stack([s[k] for s in small_g]) for k in small_g[0]}
    recv = dict(zip(grad_names, recv))

    dmod_send = jnp.swapaxes(dmod.reshape(DEPTH, N_DEV, n_ada), 0, 1)
    dmod_all = jnp.swapaxes(_exchange(dmod_send, False, "a2a_dmod"), 0, 1)
    g_ada = _ada_bwd(c_all, dmod_all)

    def flat(a, k):
        return a.reshape((-1, W[k].shape[-1]))

    def adam_nat(k, g, tr):
        outs = _adamw(g, flat(W[k], k), flat(M[k], k), flat(V[k], k), tr, "adamw_" + k)
        return [o.reshape(W[k].shape) for o in outs]

    def summed(k, tr):
        r = recv[k]
        return _sum_partials(r.reshape(N_DEV, DEPTH * r.shape[2], r.shape[3]), "sum_" + k, tr).reshape((DEPTH,) + r.shape[2:])

    big_out = dict(
        w_in=adam_nat("w_in", flat(_unpermute_in(summed("w_in", 64)), "w_in"), 64),
        w_o=adam_nat("w_o", recv["w_o"].reshape(N_DEV, DEPTH * 128, D), 128),
        w_down=adam_nat("w_down", recv["w_down"].reshape(N_DEV, DEPTH * 352, D), 352),
        w_br=adam_nat("w_br", flat(jnp.swapaxes(jnp.stack([summed("w_br%d" % n, 128) for n in range(3)], axis=1), 2, 3),
                                   "w_br"), 1024),
        w_mg=adam_nat("w_mg", flat(jnp.swapaxes(summed("w_mg", 384), 1, 2), "w_mg"), 512),
        w_up=adam_nat("w_up", flat(jnp.swapaxes(summed("w_up", 704), 1, 2), "w_up"), 512))
    ada_out = [o.reshape(DEPTH, D, n_ada) for o in _adamw(
        g_ada.reshape(DEPTH * D, n_ada), w_ada.reshape(DEPTH * D, n_ada), m_w_ada.reshape(DEPTH * D, n_ada),
        v_w_ada.reshape(DEPTH * D, n_ada), 512, "adamw_ada")]

    small_g = dict(small_g, b_ada=dmod)
    names = SMALL_REPL + SMALL_SHARDED
    full_shapes = [W[n].shape for n in SMALL_REPL] + [(DEPTH, LR_LANES, 256), (DEPTH, 3, D_FF)]
    part = _flat_pack([small_g[n] for n in names])
    total = _flat_unpack(_sum_partials(_exchange(part, True, "gather_small_grads"), "sum_small"), full_shapes)
    total = dict(zip(names, total))
    total["w_gla_a2"] = lax.dynamic_slice_in_dim(total["w_gla_a2"], me * 32, 32, axis=2)
    total["w_conv"] = lax.dynamic_slice_in_dim(total["w_conv"], me * 352, 352, axis=2)
    shapes = [W[n].shape for n in names]
    small_out = _adamw(_flat_pack([total[n] for n in names]), _flat_pack([W[n] for n in names]),
                       _flat_pack([M[n] for n in names]), _flat_pack([V[n] for n in names]), None, "adamw_small")
    small_out = [dict(zip(names, _flat_unpack(o, shapes))) for o in small_out]

    outs = []
    for k in range(4):
        d = dict(small_out[k])
        d.update({n: big_out[n][k] for n in BIG})
        d["w_ada"] = ada_out[k]
        outs.append([d[n] for n in WEIGHTS])
    return (loss, grad_x.reshape(1, T, D), *outs[0], *outs[1], *outs[2], *outs[3])
```

```python
import functools

import numpy as np
import jax
import jax.numpy as jnp
from jax import lax
from jax.experimental import pallas as pl
from jax.experimental.pallas import tpu as pltpu

F32 = jnp.float32
BF16 = jnp.bfloat16

N_DEV = 8
T = 2048
D = 1024
DEPTH = 4
N_HEADS = 4
HD = 128
BW = 512
D_FF = 2816
CHUNK = 64
EPS = 1e-6
IN_W = 5140
NP = 5632
TAIL0 = 5120
LR_LANES = 16
FF_LANE0 = 16
PACK_W = 1024
SEG_ROWS = (704, 128, 352, 192, 384, 704)
LAYER_ROWS = sum(SEG_ROWS)
VMEM_LIMIT_V7X = 56 * 1024 * 1024

ADAM_LR, ADAM_B1, ADAM_B2, ADAM_EPS, ADAM_WD, ADAM_STEP = 0.001, 0.9, 0.999, 1e-08, 0.01, 10

MESH_ID = pl.DeviceIdType.MESH


def _cp(*sem):
    return pltpu.CompilerParams(dimension_semantics=sem if sem else None, vmem_limit_bytes=VMEM_LIMIT_V7X)


def _sigmoid(z):
    return 1.0 / (1.0 + jnp.exp(-z))


def _log_sigmoid(z):
    return jnp.minimum(z, 0.0) - jnp.log(1.0 + jnp.exp(-jnp.abs(z)))


def _sum0(a):
    return jnp.sum(a, axis=0, keepdims=True)


def _mean1(a):
    return jnp.mean(a, axis=-1, keepdims=True)


def _dot(a, b, dims):
    return lax.dot_general(a.astype(BF16), b.astype(BF16), (dims, ((), ())), preferred_element_type=F32)


NN = ((1,), (0,))
NT = ((1,), (1,))
TN = ((0,), (0,))


def _exact_dot(m01, a):
    a1 = a.astype(BF16)
    r1 = a - a1.astype(F32)
    a2 = r1.astype(BF16)
    a3 = (r1 - a2.astype(F32)).astype(BF16)
    d = lambda z: jnp.dot(m01, z, preferred_element_type=F32)
    return d(a1) + d(a2) + d(a3)


def _tri(n, upper):
    r = lax.broadcasted_iota(jnp.int32, (n, n), 0)
    c = lax.broadcasted_iota(jnp.int32, (n, n), 1)
    return jnp.where((c >= r) if upper else (c <= r), 1.0, 0.0).astype(BF16)


def _exchange(x, gather, name):
    blk = x.shape if gather else x.shape[1:]

    def body(x_ref, o_ref, send_sems, recv_sems, loc_sem):
        mx, my, mc = lax.axis_index("x"), lax.axis_index("y"), lax.axis_index("c")
        me = 4 * mx + 2 * my + mc
        loc = pltpu.make_async_copy(x_ref if gather else x_ref.at[me], o_ref.at[me], loc_sem)
        loc.start()
        copies = []
        for k in range(1, N_DEV):
            px = mx ^ (k >> 2) if (k >> 2) else mx
            py = my ^ ((k >> 1) & 1) if ((k >> 1) & 1) else my
            pc = mc ^ (k & 1) if (k & 1) else mc
            peer = 4 * px + 2 * py + pc
            cp = pltpu.make_async_remote_copy(
                src_ref=x_ref if gather else x_ref.at[peer], dst_ref=o_ref.at[me],
                send_sem=send_sems.at[k - 1], recv_sem=recv_sems.at[k - 1],
                device_id=(px, py, pc), device_id_type=MESH_ID)
            cp.start()
            copies.append(cp)
        for cp in copies:
            cp.wait()
        loc.wait()

    return pl.pallas_call(
        body, name=name,
        out_shape=jax.ShapeDtypeStruct((N_DEV,) + tuple(blk), x.dtype),
        in_specs=[pl.BlockSpec(memory_space=pl.ANY)],
        out_specs=pl.BlockSpec(memory_space=pl.ANY),
        scratch_shapes=[pltpu.SemaphoreType.DMA((N_DEV - 1,)), pltpu.SemaphoreType.DMA((N_DEV - 1,)),
                        pltpu.SemaphoreType.DMA],
        compiler_params=pltpu.CompilerParams(has_side_effects=True),
    )(x)


def _blk(ref, axis, j, n, r0=0, nr=None):
    return ref.at[(slice(None),) * axis + (pl.ds(j * n + r0, n if nr is None else nr),)]


def _comm_copies(items, srcs, lands, send_sems, recv_sems, loc_sems):
    mx, my, mc = lax.axis_index("x"), lax.axis_index("y"), lax.axis_index("c")
    me = 4 * mx + 2 * my + mc
    local, remote = [], []
    for t, (kind, axis, n, sel, r0, nr, si, li) in enumerate(items):
        if kind == "gather":
            mine = _blk(lands[li], axis, me, n, r0, nr)
            own = _blk(srcs[si], axis, 0, n, r0, nr)
        else:
            mine = lands[li].at[(me,) + tuple(sel) + (pl.ds(r0, nr),)]
            own = _blk(srcs[si], axis, me, n, r0, nr)
        local.append(pltpu.make_async_copy(own, mine, loc_sems.at[t]))
        for k in range(1, N_DEV):
            px = 1 - mx if k & 4 else mx
            py = 1 - my if k & 2 else my
            pc = 1 - mc if k & 1 else mc
            src = own if kind == "gather" else _blk(srcs[si], axis, 4 * px + 2 * py + pc, n, r0, nr)
            remote.append(pltpu.make_async_remote_copy(
                src_ref=src, dst_ref=mine, send_sem=send_sems.at[t * (N_DEV - 1) + k - 1],
                recv_sem=recv_sems.at[t * (N_DEV - 1) + k - 1], device_id=(px, py, pc), device_id_type=MESH_ID))
    return local, remote


def _comm_scratch(n_items):
    return [pltpu.SemaphoreType.DMA((n_items * (N_DEV - 1),)), pltpu.SemaphoreType.DMA((n_items * (N_DEV - 1),)),
            pltpu.SemaphoreType.DMA((n_items,))]


class _Duty:
    def __init__(self, items, srcs, lands, done):
        self.items, self.srcs, self.lands, self.done = items, srcs, lands, done


def _pcall(body, name, grid, in_specs, out_specs, out_shape, args, scratch_shapes=(), sem=(), duty=None):
    if duty is None:
        return pl.pallas_call(body, name=name, grid=grid, in_specs=list(in_specs), out_specs=out_specs,
                              out_shape=out_shape, scratch_shapes=list(scratch_shapes), compiler_params=_cp(*sem))(*args)
    single = not isinstance(out_shape, (list, tuple))
    o_shape = [out_shape] if single else list(out_shape)
    o_specs = [out_specs] if single else list(out_specs)
    n_in, n_out, n_scr = len(in_specs), len(o_shape), len(scratch_shapes)
    n_src, n_land, n_items = len(duty.srcs), len(duty.lands), len(duty.items)
    a0 = n_in + n_src + n_land

    def wrapped(*refs):
        srcs = refs[n_in:n_in + n_src]
        lands = refs[a0 + n_out:a0 + n_out + n_land]
        core = refs[:n_in] + refs[a0:a0 + n_out] + refs[a0 + n_out + n_land:a0 + n_out + n_land + n_scr]
        sems = refs[a0 + n_out + n_land + n_scr:]
        first = functools.reduce(jnp.logical_and, [pl.program_id(a) == 0 for a in range(len(grid))])
        last = functools.reduce(jnp.logical_and, [pl.program_id(a) == g - 1 for a, g in enumerate(grid)])

        @pl.when(first)
        def _():
            local, remote = _comm_copies(duty.items, srcs, lands, *sems)
            for cp in local + remote:
                cp.start()

        body(*core)

        @pl.when(last)
        def _():
            local, remote = _comm_copies(duty.items, srcs, lands, *sems)
            for cp in remote + local:
                cp.wait()

    hbm = pl.BlockSpec(memory_space=pl.ANY)
    res = pl.pallas_call(
        wrapped, name=name, grid=grid,
        in_specs=list(in_specs) + [hbm] * (n_src + n_land), out_specs=o_specs + [hbm] * n_land,
        out_shape=o_shape + [jax.ShapeDtypeStruct(a.shape, a.dtype) for a in duty.lands],
        input_output_aliases={n_in + n_src + t: n_out + t for t in range(n_land)},
        scratch_shapes=list(scratch_shapes) + _comm_scratch(n_items),
        compiler_params=pltpu.CompilerParams(dimension_semantics=("arbitrary",) * len(grid),
                                             vmem_limit_bytes=VMEM_LIMIT_V7X, has_side_effects=True),
    )(*args, *duty.srcs, *duty.lands)
    duty.done(res[n_out:])
    return res[0] if single else res[:n_out]


def _comm(duty, name):
    n_src, n_land = len(duty.srcs), len(duty.lands)

    def body(*refs):
        local, remote = _comm_copies(duty.items, refs[:n_src], refs[n_src + n_land:n_src + 2 * n_land],
                                     *refs[n_src + 2 * n_land:])
        for cp in local + remote:
            cp.start()
        for cp in remote + local:
            cp.wait()

    hbm = pl.BlockSpec(memory_space=pl.ANY)
    duty.done(pl.pallas_call(
        body, name=name, out_shape=[jax.ShapeDtypeStruct(a.shape, a.dtype) for a in duty.lands],
        in_specs=[hbm] * (n_src + n_land), out_specs=[hbm] * n_land,
        input_output_aliases={n_src + t: t for t in range(n_land)},
        scratch_shapes=_comm_scratch(len(duty.items)), compiler_params=pltpu.CompilerParams(has_side_effects=True),
    )(*duty.srcs, *duty.lands))


class _Transfers:
    def __init__(self, name):
        self.name, self.queue, self.lands, self.flushes = name, [], {}, 0

    def add(self, key, item, src):
        self.queue.append((key, item, src))

    def take(self, count):
        units, self.queue = self.queue[:count], self.queue[count:]
        if not units:
            return None
        keys, srcs, items = [], [], []
        for key, item, src in units:
            if key not in keys:
                keys.append(key)
            if not any(src is s for s in srcs):
                srcs.append(src)
            si = [i for i, s in enumerate(srcs) if s is src][0]
            items.append(tuple(item) + (si, keys.index(key)))

        def done(new_lands):
            for key, arr in zip(keys, new_lands):
                self.lands[key] = arr

        return _Duty(items, srcs, [self.lands[k] for k in keys], done)

    def drain(self, upto=None):
        count = len(self.queue) if upto is None else upto
        if count:
            self.flushes += 1
            _comm(self.take(count), "%s_flush%d" % (self.name, self.flushes))

    def get(self, key):
        pending = [i for i, u in enumerate(self.queue) if u[0] == key]
        if pending:
            self.drain(pending[-1] + 1)
        return self.lands[key]


def _matmul(a, b, mode, name, out_dtype=F32, tm=1024, tn=512, tk=None, add=None, duty=None):
    halves = a.ndim == 3
    if mode == "tn":
        K, M = a.shape[-2], a.shape[-1] * (2 if halves else 1)
        N = b.shape[1]
    else:
        M, K = a.shape[-2], a.shape[-1] * (2 if halves else 1)
        N = b.shape[0] if mode == "nt" else b.shape[1]
    tm, tn = min(tm, M), min(tn, N)
    tk = K if tk is None else tk
    nk = K // tk
    assert M % tm == 0 and N % tn == 0 and K % tk == 0, (name, M, N, K, tm, tn, tk)
    dims = {"nn": NN, "nt": NT, "tn": TN}[mode]
    has_add = add is not None

    def body(*refs):
        if has_add:
            a_ref, b_ref, add_ref, o_ref, acc_ref = refs
        else:
            a_ref, b_ref, o_ref, acc_ref = refs
        k = pl.program_id(2)
        part = _dot(a_ref[...], b_ref[...], dims)

        @pl.when(k == 0)
        def _():
            acc_ref[...] = part

        @pl.when(k > 0)
        def _():
            acc_ref[...] += part

        @pl.when(k == nk - 1)
        def _():
            r = acc_ref[...]
            if has_add:
                r = r + add_ref[...]
            o_ref[...] = r.astype(o_ref.dtype)

    if halves and mode == "tn":
        per = a.shape[-1] // tm
        a_spec = pl.BlockSpec((None, tk, tm), lambda i, j, k: (i // per, k, i % per))
    elif halves:
        per = a.shape[-1] // tk
        a_spec = pl.BlockSpec((None, tm, tk), lambda i, j, k: (k // per, i, k % per))
    elif mode == "tn":
        a_spec = pl.BlockSpec((tk, tm), lambda i, j, k: (k, i))
    else:
        a_spec = pl.BlockSpec((tm, tk), lambda i, j, k: (i, k))
    if mode == "nt":
        b_spec = pl.BlockSpec((tn, tk), lambda i, j, k: (j, k))
    else:
        b_spec = pl.BlockSpec((tk, tn), lambda i, j, k: (k, j))
    o_spec = pl.BlockSpec((tm, tn), lambda i, j, k: (i, j))
    in_specs = [a_spec, b_spec] + ([o_spec] if has_add else [])
    args = (a, b) + ((add,) if has_add else ())
    return _pcall(
        body, name=name, grid=(M // tm, N // tn, nk),
        out_shape=jax.ShapeDtypeStruct((M, N), out_dtype),
        in_specs=in_specs, out_specs=o_spec,
        scratch_shapes=[pltpu.VMEM((tm, tn), F32)],
        sem=("parallel", "parallel", "arbitrary"), args=args, duty=duty)


def _ada_fwd(c_all, w_ada, b_loc):
    n = w_ada.shape[2]

    def body(c_ref, w_ref, b_ref, o_ref):
        c = c_ref[...]
        o_ref[0] = _dot(c * _sigmoid(c), w_ref[0], NN) + b_ref[0]

    return pl.pallas_call(
        body, name="ada_fwd", grid=(DEPTH,),
        out_shape=jax.ShapeDtypeStruct((DEPTH, N_DEV, n), F32),
        in_specs=[pl.BlockSpec((N_DEV, D), lambda l: (0, 0)),
                  pl.BlockSpec((1, D, n), lambda l: (l, 0, 0)),
                  pl.BlockSpec((1, 1, n), lambda l: (l, 0, 0))],
        out_specs=pl.BlockSpec((1, N_DEV, n), lambda l: (l, 0, 0)),
        compiler_params=_cp("parallel"),
    )(c_all, w_ada, b_loc)


def _ada_bwd(c_all, dmod_all):
    n = dmod_all.shape[2]

    def body(c_ref, d_ref, o_ref):
        c = c_ref[...]
        o_ref[0] = _dot(c * _sigmoid(c), d_ref[0], TN)

    return pl.pallas_call(
        body, name="ada_bwd", grid=(DEPTH,),
        out_shape=jax.ShapeDtypeStruct((DEPTH, D, n), F32),
        in_specs=[pl.BlockSpec((N_DEV, D), lambda l: (0, 0)),
                  pl.BlockSpec((1, N_DEV, n), lambda l: (l, 0, 0))],
        out_specs=pl.BlockSpec((1, D, n), lambda l: (l, 0, 0)),
        compiler_params=_cp("parallel"),
    )(c_all, dmod_all)


ROW_TILE = 256


def _row_spec(w=D, col=0):
    return pl.BlockSpec((ROW_TILE, w), lambda i: (i, col))


def _vec_spec(w=D):
    return pl.BlockSpec((1, w), lambda i: (0, 0))


def _norm_fwd(x, g, scale, shift, name, m=None, gate=None):
    has_res = m is not None

    def body(*refs):
        if has_res:
            x_ref, m_ref, gate_ref, g_ref, sc_ref, sh_ref, xo_ref, h_ref = refs
            xv = x_ref[...] + gate_ref[...] * m_ref[...]
            xo_ref[...] = xv
        else:
            x_ref, g_ref, sc_ref, sh_ref, h_ref = refs
            xv = x_ref[...]
        r = lax.rsqrt(_mean1(xv * xv) + EPS)
        h_ref[...] = ((xv * r * g_ref[...]) * (1.0 + sc_ref[...]) + sh_ref[...]).astype(BF16)

    ins = [x] + ([m, gate] if has_res else []) + [g, scale, shift]
    in_specs = [_row_spec()] + ([_row_spec(), _vec_spec()] if has_res else []) + [_vec_spec()] * 3
    out_shape = [jax.ShapeDtypeStruct((T, D), BF16)]
    out_specs = [_row_spec()]
    if has_res:
        out_shape = [jax.ShapeDtypeStruct((T, D), F32)] + out_shape
        out_specs = [_row_spec()] + out_specs
    out = pl.pallas_call(body, name=name, grid=(T // ROW_TILE,), out_shape=out_shape, in_specs=in_specs,
                         out_specs=out_specs, compiler_params=_cp("parallel"))(*ins)
    return out if has_res else out[0]


def _norm_bwd(x, dh, dres, g, scale, shift, name):
    def body(x_ref, dh_ref, dres_ref, g_ref, sc_ref, sh_ref, dx_ref, st_ref):
        xv, dh_v, gv = x_ref[...], dh_ref[...], g_ref[...]
        r = lax.rsqrt(_mean1(xv * xv) + EPS)
        n = xv * r
        dy = dh_v * (1.0 + sc_ref[...])
        dn = dy * gv
        dx_ref[...] = r * (dn - n * _mean1(dn * n)) + dres_ref[...]

        @pl.when(pl.program_id(0) == 0)
        def _():
            st_ref[...] = jnp.zeros_like(st_ref)

        st_ref[0:1, :] += _sum0(dy * n)
        st_ref[1:2, :] += _sum0(dh_v * (n * gv))
        st_ref[2:3, :] += _sum0(dh_v)

    return pl.pallas_call(
        body, name=name, grid=(T // ROW_TILE,),
        out_shape=[jax.ShapeDtypeStruct((T, D), F32), jax.ShapeDtypeStruct((8, D), F32)],
        in_specs=[_row_spec(), _row_spec(), _row_spec(), _vec_spec(), _vec_spec(), _vec_spec()],
        out_specs=[_row_spec(), pl.BlockSpec((8, D), lambda i: (0, 0))],
        compiler_params=_cp("arbitrary"),
    )(x, dh, dres, g, scale, shift)


def _axpy(x, m, gate, name):
    def body(x_ref, m_ref, gate_ref, o_ref):
        o_ref[...] = x_ref[...] + gate_ref[...] * m_ref[...]

    return pl.pallas_call(
        body, name=name, grid=(T // ROW_TILE,), out_shape=jax.ShapeDtypeStruct((T, D), F32),
        in_specs=[_row_spec(), _row_spec(), _vec_spec()], out_specs=_row_spec(),
        compiler_params=_cp("parallel"))(x, m, gate)


def _gate_bwd(dx, m, gate, name):
    def body(dx_ref, m_ref, gate_ref, dm_ref, st_ref):
        dxv = dx_ref[...]
        dm_ref[...] = (gate_ref[...] * dxv).astype(BF16)

        @pl.when(pl.program_id(0) == 0)
        def _():
            st_ref[...] = jnp.zeros_like(st_ref)

        st_ref[0:1, :] += _sum0(dxv * m_ref[...])

    return pl.pallas_call(
        body, name=name, grid=(T // ROW_TILE,),
        out_shape=[jax.ShapeDtypeStruct((T, D), BF16), jax.ShapeDtypeStruct((8, D), F32)],
        in_specs=[_row_spec(), _row_spec(), _vec_spec()],
        out_specs=[_row_spec(), pl.BlockSpec((8, D), lambda i: (0, 0))],
        compiler_params=_cp("arbitrary"))(dx, m, gate)


def _loss_fwd_bwd(y, target):
    def body(y_ref, t_ref, l_ref, d_ref):
        e = y_ref[...] - t_ref[...]
        d_ref[...] = e * (1.0 / D)

        @pl.when(pl.program_id(0) == 0)
        def _():
            l_ref[...] = jnp.zeros_like(l_ref)

        l_ref[...] += jnp.sum(_sum0(e * e), axis=1, keepdims=True) * (0.5 / D)

    return pl.pallas_call(
        body, name="loss", grid=(T // ROW_TILE,),
        out_shape=[jax.ShapeDtypeStruct((8, 128), F32), jax.ShapeDtypeStruct((T, D), F32)],
        in_specs=[_row_spec(), _row_spec()],
        out_specs=[pl.BlockSpec((8, 128), lambda i: (0, 0)), _row_spec()],
        compiler_params=_cp("arbitrary"))(y, target)


def _rope_tables():
    half = HD // 2
    inv_freq = 10000.0 ** (-jnp.arange(half, dtype=F32) / half)
    ang = jnp.arange(T, dtype=F32)[:, None] * inv_freq[None, :]
    cos, sin = jnp.cos(ang), jnp.sin(ang)
    return jnp.concatenate([cos, cos], axis=1), jnp.concatenate([-sin, sin], axis=1)


def _rope_fwd(p, cosf, sinf):
    def body(p_ref, c_ref, s_ref, o_ref):
        cv, sv = c_ref[...], s_ref[...]
        for j in range(2 * N_HEADS):
            xv = p_ref[:, j * HD:(j + 1) * HD]
            rot = xv * cv + pltpu.roll(xv, HD // 2, 1) * sv
            if j >= N_HEADS:
                rot = rot * (HD ** -0.5)
            o_ref[:, j * HD:(j + 1) * HD] = rot.astype(BF16)

    return pl.pallas_call(
        body, name="rope_fwd", grid=(T // ROW_TILE,),
        out_shape=jax.ShapeDtypeStruct((T, 2 * BW), BF16),
        in_specs=[_row_spec(2 * BW), _row_spec(HD), _row_spec(HD)], out_specs=_row_spec(2 * BW),
        compiler_params=_cp("parallel"))(p, cosf, sinf)


def _rope_bwd(dq, dk, cosf, sinf):
    def body(dq_ref, dk_ref, c_ref, s_ref, o_ref):
        cv, sv = c_ref[...], s_ref[...]
        for j in range(2 * N_HEADS):
            h = j % N_HEADS
            d = dq_ref[:, h * HD:(h + 1) * HD] if j < N_HEADS else dk_ref[:, h * HD:(h + 1) * HD] * (HD ** -0.5)
            o_ref[:, j * HD:(j + 1) * HD] = d * cv + pltpu.roll(d * sv, HD // 2, 1)

    return pl.pallas_call(
        body, name="rope_bwd", grid=(T // ROW_TILE,),
        out_shape=jax.ShapeDtypeStruct((T, 2 * BW), F32),
        in_specs=[_row_spec(BW), _row_spec(BW), _row_spec(HD), _row_spec(HD)], out_specs=_row_spec(2 * BW),
        compiler_params=_cp("parallel"))(dq, dk, cosf, sinf)


TQ = 256
V_RET_BLK = 8


def _ret_logg():
    lg = jnp.log1p(-jnp.exp2(-5.0 - jnp.arange(N_HEADS, dtype=F32)))
    return jnp.broadcast_to(lg[:, None, None], (N_HEADS, 1, 128))


def _block_iotas(i, kl):
    rows = lax.broadcasted_iota(jnp.int32, (TQ, kl), 0) + i * TQ
    cols = lax.broadcasted_iota(jnp.int32, (TQ, kl), 1)
    return rows, cols


def _ret_weight(lg_ref, i, kl):
    rows, cols = _block_iotas(i, kl)
    dist = jnp.abs(rows - cols).astype(F32)
    w = jnp.exp(dist * lg_ref[0][:, 0:1])
    return jnp.where((cols >> 6) <= (rows >> 6), w, 0.0)


def _per_query_block(i, fn):
    for n in range(1, T // TQ + 1):
        pl.when(i == n - 1)(functools.partial(fn, n * TQ))


def _ret_specs():
    q_spec = pl.BlockSpec((TQ, HD), lambda h, i: (i, h))
    k_spec = pl.BlockSpec((T, HD), lambda h, i: (0, N_HEADS + h))
    v_spec = pl.BlockSpec((T, HD), lambda h, i: (0, V_RET_BLK + h))
    lg_spec = pl.BlockSpec((1, 1, 128), lambda h, i: (h, 0, 0))
    return q_spec, k_spec, v_spec, lg_spec


def _ret_fwd(qk, p, logg, duty=None):
    def body(q_ref, k_ref, v_ref, lg_ref, o_ref):
        i = pl.program_id(1)

        def visible(kl):
            s = _dot(q_ref[...], k_ref[0:kl, :], NT) * _ret_weight(lg_ref, i, kl)
            o_ref[...] = _dot(s, v_ref[0:kl, :], NN)

        _per_query_block(i, visible)

    q_spec, k_spec, v_spec, lg_spec = _ret_specs()
    return _pcall(
        body, name="ret_fwd", grid=(N_HEADS, T // TQ),
        out_shape=jax.ShapeDtypeStruct((T, BW), F32),
        in_specs=[q_spec, k_spec, v_spec, lg_spec], out_specs=q_spec,
        sem=("parallel", "parallel"), args=(qk, qk, p, logg), duty=duty)


def _ret_bwd(qk, p, logg, do, duty=None):
    def body(q_ref, k_ref, v_ref, lg_ref, do_ref, dq_ref, dk_ref, dv_ref):
        i = pl.program_id(1)
        q, dov = q_ref[...], do_ref[...]

        @pl.when(i == 0)
        def _():
            dk_ref[...] = jnp.zeros_like(dk_ref)
            dv_ref[...] = jnp.zeros_like(dv_ref)

        def visible(kl):
            w = _ret_weight(lg_ref, i, kl)
            k = k_ref[0:kl, :]
            s = _dot(q, k, NT) * w
            ds = _dot(dov, v_ref[0:kl, :], NT) * w
            dk_ref[0:kl, :] += _dot(ds, q, TN)
            dv_ref[0:kl, :] += _dot(s, dov, TN)
            dq_ref[...] = _dot(ds, k, NN)

        _per_query_block(i, visible)

    q_spec, k_spec, v_spec, lg_spec = _ret_specs()
    acc_spec = pl.BlockSpec((T, HD), lambda h, i: (0, h))
    sh = jax.ShapeDtypeStruct((T, BW), F32)
    return _pcall(
        body, name="ret_bwd", grid=(N_HEADS, T // TQ),
        out_shape=[sh, sh, sh],
        in_specs=[q_spec, k_spec, v_spec, lg_spec, q_spec], out_specs=[q_spec, acc_spec, acc_spec],
        sem=("parallel", "arbitrary"), args=(qk, qk, p, logg, do), duty=duty)


def _post_norm(xv, gv, centered):
    if centered:
        xv = xv - _mean1(xv)
    r = lax.rsqrt(_mean1(xv * xv) + EPS)
    return xv * r, r


def _branch_post_fwd(raw, p, g, gate_blk, centered, name):
    def body(raw_ref, z_ref, g_ref, o_ref):
        for h in range(N_HEADS):
            sl = slice(h * HD, (h + 1) * HD)
            gv = g_ref[:, sl] if centered else g_ref[...]
            xh, _ = _post_norm(raw_ref[:, sl], gv, centered)
            z = z_ref[:, sl]
            o_ref[:, sl] = (z * _sigmoid(z) * (xh * gv)).astype(BF16)

    return pl.pallas_call(
        body, name=name, grid=(T // ROW_TILE,),
        out_shape=jax.ShapeDtypeStruct((T, BW), BF16),
        in_specs=[_row_spec(BW), _row_spec(BW, gate_blk), _vec_spec(BW if centered else HD)],
        out_specs=_row_spec(BW), compiler_params=_cp("parallel"))(raw, p, g)


def _branch_post_bwd(raw, p, g, dout, gate_blk, centered, name):
    gw = BW if centered else HD

    def body(raw_ref, z_ref, g_ref, do_ref, dr_ref, dz_ref, dg_ref):
        @pl.when(pl.program_id(0) == 0)
        def _():
            dg_ref[...] = jnp.zeros_like(dg_ref)

        for h in range(N_HEADS):
            sl = slice(h * HD, (h + 1) * HD)
            gsl = sl if centered else slice(0, HD)
            gv, z, dov = g_ref[:, gsl], z_ref[:, sl], do_ref[:, sl]
            xh, r = _post_norm(raw_ref[:, sl], gv, centered)
            sg = _sigmoid(z)
            dyn = dov * (z * sg)
            dz_ref[:, sl] = dov * (xh * gv) * (sg * (1.0 + z * (1.0 - sg)))
            dxh = dyn * gv
            t = dxh - xh * _mean1(dxh * xh)
            if centered:
                t = t - _mean1(dxh)
            dr_ref[:, sl] = r * t
            dg_ref[0:1, gsl] += _sum0(dyn * xh)

    return pl.pallas_call(
        body, name=name, grid=(T // ROW_TILE,),
        out_shape=[jax.ShapeDtypeStruct((T, BW), F32), jax.ShapeDtypeStruct((T, BW), F32),
                   jax.ShapeDtypeStruct((8, gw), F32)],
        in_specs=[_row_spec(BW), _row_spec(BW, gate_blk), _vec_spec(gw), _row_spec(BW)],
        out_specs=[_row_spec(BW), _row_spec(BW), pl.BlockSpec((8, gw), lambda i: (0, 0))],
        compiler_params=_cp("arbitrary"))(raw, p, g, dout)


GLA_ROWS = 256
GLA_CPB = GLA_ROWS // CHUNK
GLA_DK = 64
GLA_W = N_HEADS * GLA_DK
GQ_BLK, GK_BLK, GV_BLK, GG_BLK, TAIL_BLK = 8, 9, 5, 6, 40
RG_BLK = 3


def _gla_chunk_common(tl, w2, bv, kv):
    pre = _dot(tl, w2, NN) + bv
    la = _log_sigmoid(pre) * (1.0 / 16.0)
    bc = _exact_dot(_tri(CHUNK, False), la)
    be = bc[CHUNK - 1:CHUNK, :]
    w = jnp.exp(be - bc)
    return pre, w, jnp.exp(be), kv * w


def _head_masks():
    lane = lax.broadcasted_iota(jnp.int32, (1, GLA_W), 1)
    return [jnp.where((lane // GLA_DK) == h, 1.0, 0.0) for h in range(N_HEADS)]


def _gla_fwd(p, w2pad, b, duty=None):
    nb = T // GLA_ROWS

    def body(q_ref, k_ref, v_ref, t_ref, w2_ref, b_ref, o_ref, st_ref, s_acc):
        @pl.when(pl.program_id(0) == 0)
        def _():
            s_acc[...] = jnp.zeros_like(s_acc)

        masks = _head_masks()
        for c in range(GLA_CPB):
            rows = slice(c * CHUNK, (c + 1) * CHUNK)
            _, _, a, kd = _gla_chunk_common(t_ref[rows, :], w2_ref[...], b_ref[...], k_ref[rows, :])
            q = q_ref[rows, :] * (GLA_DK ** -0.5)
            kv = None
            for h in range(N_HEADS):
                t = _dot(v_ref[rows, h * HD:(h + 1) * HD], kd * masks[h], TN)
                kv = t if kv is None else kv + t
            s_new = s_acc[...] * a + kv
            s_acc[...] = s_new
            st_ref[c] = s_new
            for h in range(N_HEADS):
                o_ref[rows, h * HD:(h + 1) * HD] = _dot(q * masks[h], s_new, NT)

    return _pcall(
        body, name="gla_fwd", grid=(nb,),
        out_shape=[jax.ShapeDtypeStruct((T, BW), F32), jax.ShapeDtypeStruct((T // CHUNK, HD, GLA_W), F32)],
        in_specs=[pl.BlockSpec((GLA_ROWS, GLA_W), lambda i: (i, GQ_BLK)),
                  pl.BlockSpec((GLA_ROWS, GLA_W), lambda i: (i, GK_BLK)),
                  pl.BlockSpec((GLA_ROWS, BW), lambda i: (i, GV_BLK)),
                  pl.BlockSpec((GLA_ROWS, 128), lambda i: (i, TAIL_BLK)),
                  pl.BlockSpec((128, GLA_W), lambda i: (0, 0)),
                  pl.BlockSpec((1, GLA_W), lambda i: (0, 0))],
        out_specs=[pl.BlockSpec((GLA_ROWS, BW), lambda i: (i, 0)),
                   pl.BlockSpec((GLA_CPB, HD, GLA_W), lambda i: (i, 0, 0))],
        scratch_shapes=[pltpu.VMEM((HD, GLA_W), F32)],
        sem=("arbitrary",), args=(p, p, p, p, w2pad, b), duty=duty)


def _gla_bwd(p, w2pad, b, states, do, duty=None):
    nb = T // GLA_ROWS

    def body(q_ref, k_ref, v_ref, t_ref, w2_ref, b_ref, st_ref, prev_ref, do_ref,
             dq_ref, dk_ref, dv_ref, dt_ref, dw2_ref, db_ref, ds_acc):
        step = pl.program_id(0)

        @pl.when(step == 0)
        def _():
            ds_acc[...] = jnp.zeros_like(ds_acc)
            dw2_ref[...] = jnp.zeros_like(dw2_ref)
            db_ref[...] = jnp.zeros_like(db_ref)

        masks = _head_masks()
        up = _tri(CHUNK, True)
        has_prev = jnp.where(step == nb - 1, 0.0, 1.0)
        for c in reversed(range(GLA_CPB)):
            rows = slice(c * CHUNK, (c + 1) * CHUNK)
            tl, w2, k = t_ref[rows, :], w2_ref[...], k_ref[rows, :]
            pre, w, a, kd = _gla_chunk_common(tl, w2, b_ref[...], k)
            q = q_ref[rows, :] * (GLA_DK ** -0.5)
            s_n = st_ref[c]
            s_prev = st_ref[c - 1] if c > 0 else prev_ref[0] * has_prev
            ds = ds_acc[...]
            dos = [do_ref[rows, h * HD:(h + 1) * HD] for h in range(N_HEADS)]
            for h in range(N_HEADS):
                ds = ds + _dot(dos[h], q * masks[h], TN)
            dqp = jnp.zeros((CHUNK, GLA_W), F32)
            dkd = jnp.zeros((CHUNK, GLA_W), F32)
            for h in range(N_HEADS):
                dqp = dqp + masks[h] * _dot(dos[h], s_n, NN)
                dkd = dkd + masks[h] * _dot(v_ref[rows, h * HD:(h + 1) * HD], ds, NN)
                dv_ref[rows, h * HD:(h + 1) * HD] = _dot(kd * masks[h], ds, NT)
            dq_ref[rows, :] = dqp * (GLA_DK ** -0.5)
            dk_ref[rows, :] = dkd * w
            e = dkd * k * w
            dbe = _sum0(e) + _sum0(ds * s_prev) * a
            dla = dbe - _exact_dot(up, e)
            dpre = dla * (1.0 / 16.0) * _sigmoid(-pre)
            db_ref[0:1, :] += _sum0(dpre)
            dw2_ref[...] += _dot(tl, dpre, TN)
            dt_ref[rows, :] = _dot(dpre, w2, NT)
            ds_acc[...] = ds * a

    rev = lambda i: nb - 1 - i
    sh = lambda w: jax.ShapeDtypeStruct((T, w), F32)
    return _pcall(
        body, name="gla_bwd", grid=(nb,),
        out_shape=[sh(GLA_W), sh(GLA_W), sh(BW), sh(128), jax.ShapeDtypeStruct((128, GLA_W), F32),
                   jax.ShapeDtypeStruct((8, GLA_W), F32)],
        in_specs=[pl.BlockSpec((GLA_ROWS, GLA_W), lambda i: (rev(i), GQ_BLK)),
                  pl.BlockSpec((GLA_ROWS, GLA_W), lambda i: (rev(i), GK_BLK)),
                  pl.BlockSpec((GLA_ROWS, BW), lambda i: (rev(i), GV_BLK)),
                  pl.BlockSpec((GLA_ROWS, 128), lambda i: (rev(i), TAIL_BLK)),
                  pl.BlockSpec((128, GLA_W), lambda i: (0, 0)),
                  pl.BlockSpec((1, GLA_W), lambda i: (0, 0)),
                  pl.BlockSpec((GLA_CPB, HD, GLA_W), lambda i: (rev(i), 0, 0)),
                  pl.BlockSpec((1, HD, GLA_W), lambda i: (jnp.maximum(rev(i) * GLA_CPB - 1, 0), 0, 0)),
                  pl.BlockSpec((GLA_ROWS, BW), lambda i: (rev(i), 0))],
        out_specs=[pl.BlockSpec((GLA_ROWS, GLA_W), lambda i: (rev(i), 0)),
                   pl.BlockSpec((GLA_ROWS, GLA_W), lambda i: (rev(i), 0)),
                   pl.BlockSpec((GLA_ROWS, BW), lambda i: (rev(i), 0)),
                   pl.BlockSpec((GLA_ROWS, 128), lambda i: (rev(i), 0)),
                   pl.BlockSpec((128, GLA_W), lambda i: (0, 0)),
                   pl.BlockSpec((8, GLA_W), lambda i: (0, 0))],
        scratch_shapes=[pltpu.VMEM((HD, GLA_W), F32)],
        sem=("arbitrary",), args=(p, p, p, p, w2pad, b, states, states, do), duty=duty)


FQ_BLK, FK_BLK = 7, 8
V_FOX_BLK = 36


def _fox_prep_fwd(p, qg, kg, btail):
    def body(q_ref, k_ref, t_ref, qg_ref, kg_ref, bt_ref, o_ref, cum_ref, carry):
        @pl.when(pl.program_id(0) == 0)
        def _():
            carry[...] = jnp.zeros_like(carry)

        for src, gr, off in ((q_ref, qg_ref, 0), (k_ref, kg_ref, BW)):
            for h in range(N_HEADS):
                xv = src[:, h * HD:(h + 1) * HD]
                r = lax.rsqrt(_mean1(xv * xv) + EPS)
                o_ref[:, off + h * HD:off + (h + 1) * HD] = (xv * r * gr[...]).astype(BF16)
        logf = _log_sigmoid(t_ref[...] + bt_ref[...])
        cum = _exact_dot(_tri(ROW_TILE, False), logf) + carry[...]
        cum_ref[...] = cum
        carry[...] = cum[ROW_TILE - 1:ROW_TILE, :]

    return pl.pallas_call(
        body, name="fox_prep_fwd", grid=(T // ROW_TILE,),
        out_shape=[jax.ShapeDtypeStruct((T, 2 * BW), BF16), jax.ShapeDtypeStruct((T, 128), F32)],
        in_specs=[_row_spec(BW, FQ_BLK), _row_spec(BW, FK_BLK), _row_spec(128, TAIL_BLK),
                  _vec_spec(HD), _vec_spec(HD), _vec_spec(128)],
        out_specs=[_row_spec(2 * BW), _row_spec(128)],
        scratch_shapes=[pltpu.VMEM((1, 128), F32)],
        compiler_params=_cp("arbitrary"))(p, p, p, qg, kg, btail)


def _fox_prep_bwd(p, qg, kg, btail, dqn, dkn, dcum):
    nt = T // ROW_TILE

    def body(q_ref, k_ref, t_ref, qg_ref, kg_ref, bt_ref, dq_ref, dk_ref, dc_ref, o_ref, dt_ref, st_ref, carry):
        @pl.when(pl.program_id(0) == 0)
        def _():
            carry[...] = jnp.zeros_like(carry)
            st_ref[...] = jnp.zeros_like(st_ref)

        for row, (src, gr, dsrc, off) in enumerate(((q_ref, qg_ref, dq_ref, 0), (k_ref, kg_ref, dk_ref, BW))):
            for h in range(N_HEADS):
                xv = src[:, h * HD:(h + 1) * HD]
                dy = dsrc[:, h * HD:(h + 1) * HD]
                r = lax.rsqrt(_mean1(xv * xv) + EPS)
                n = xv * r
                dn = dy * gr[...]
                o_ref[:, off + h * HD:off + (h + 1) * HD] = r * (dn - n * _mean1(dn * n))
                st_ref[row:row + 1, :] += _sum0(dy * n)
        z = t_ref[...] + bt_ref[...]
        dlogf = _exact_dot(_tri(ROW_TILE, True), dc_ref[...]) + carry[...]
        carry[...] = dlogf[0:1, :]
        lane = lax.broadcasted_iota(jnp.int32, (1, 128), 1)
        keep = (lane >= FF_LANE0) & (lane < FF_LANE0 + N_HEADS)
        dz = jnp.where(keep, dlogf * _sigmoid(-z), 0.0)
        dt_ref[...] = dz
        st_ref[2:3, :] += _sum0(dz)

    rs = lambda w, col=0: pl.BlockSpec((ROW_TILE, w), lambda i: (nt - 1 - i, col))
    return pl.pallas_call(
        body, name="fox_prep_bwd", grid=(nt,),
        out_shape=[jax.ShapeDtypeStruct((T, 2 * BW), F32), jax.ShapeDtypeStruct((T, 128), F32),
                   jax.ShapeDtypeStruct((8, 128), F32)],
        in_specs=[rs(BW, FQ_BLK), rs(BW, FK_BLK), rs(128, TAIL_BLK), _vec_spec(HD), _vec_spec(HD), _vec_spec(128),
                  rs(BW), rs(BW), rs(128)],
        out_specs=[rs(2 * BW), rs(128), pl.BlockSpec((8, 128), lambda i: (0, 0))],
        scratch_shapes=[pltpu.VMEM((1, 128), F32)],
        compiler_params=_cp("arbitrary"))(p, p, p, qg, kg, btail, dqn, dkn, dcum)


def _fox_logits(q_ref, k_ref, cc_ref, cr_ref, i, kl):
    rows, cols = _block_iotas(i, kl)
    s = _dot(q_ref[...], k_ref[0:kl, :], NT) * (HD ** -0.5) + cc_ref[0] - cr_ref[0, :, 0:kl]
    return jnp.where(cols <= rows, s, -1e30)


def _fox_specs():
    q_spec = pl.BlockSpec((TQ, HD), lambda h, i: (i, h))
    k_spec = pl.BlockSpec((T, HD), lambda h, i: (0, N_HEADS + h))
    v_spec = pl.BlockSpec((T, HD), lambda h, i: (0, V_FOX_BLK + h))
    col_spec = pl.BlockSpec((1, TQ, 1), lambda h, i: (h, i, 0))
    row_spec = pl.BlockSpec((1, 1, T), lambda h, i: (h, 0, 0))
    return q_spec, k_spec, v_spec, col_spec, row_spec


def _fox_fwd(qkn, p, cumcol, cumrow, duty=None):
    def body(q_ref, k_ref, v_ref, cc_ref, cr_ref, o_ref, lse_ref):
        i = pl.program_id(1)

        def visible(kl):
            s = _fox_logits(q_ref, k_ref, cc_ref, cr_ref, i, kl)
            m = jnp.max(s, axis=-1, keepdims=True)
            e = jnp.exp(s - m)
            l = jnp.sum(e, axis=-1, keepdims=True)
            o_ref[...] = _dot(e / l, v_ref[0:kl, :], NN)
            lse_ref[0] = m + jnp.log(l)

        _per_query_block(i, visible)

    q_spec, k_spec, v_spec, col_spec, row_spec = _fox_specs()
    return _pcall(
        body, name="fox_fwd", grid=(N_HEADS, T // TQ),
        out_shape=[jax.ShapeDtypeStruct((T, BW), F32), jax.ShapeDtypeStruct((N_HEADS, T, 1), F32)],
        in_specs=[q_spec, k_spec, v_spec, col_spec, row_spec], out_specs=[q_spec, col_spec],
        sem=("parallel", "parallel"), args=(qkn, qkn, p, cumcol, cumrow), duty=duty)


def _fox_bwd(qkn, p, cumcol, cumrow, lse, o, do, duty=None):
    def body(q_ref, k_ref, v_ref, cc_ref, cr_ref, lse_ref, o_ref, do_ref, dq_ref, dk_ref, dv_ref, dr_ref, dc_ref):
        i = pl.program_id(1)
        @pl.when(i == 0)
        def _():
            dk_ref[...] = jnp.zeros_like(dk_ref)
            dv_ref[...] = jnp.zeros_like(dv_ref)
            dc_ref[...] = jnp.zeros_like(dc_ref)

        def visible(kl):
            q, dov = q_ref[...], do_ref[...]
            pm = jnp.exp(_fox_logits(q_ref, k_ref, cc_ref, cr_ref, i, kl) - lse_ref[0])
            delta = jnp.sum(o_ref[...] * dov, axis=-1, keepdims=True)
            ds = pm * (_dot(dov, v_ref[0:kl, :], NT) - delta)
            dq_ref[...] = _dot(ds, k_ref[0:kl, :], NN) * (HD ** -0.5)
            dr_ref[0] = jnp.sum(ds, axis=-1, keepdims=True)
            dk_ref[0:kl, :] += _dot(ds, q, TN) * (HD ** -0.5)
            dv_ref[0:kl, :] += _dot(pm, dov, TN)
            dc_ref[0, :, 0:kl] += _sum0(ds)

        _per_query_block(i, visible)

    q_spec, k_spec, v_spec, col_spec, row_spec = _fox_specs()
    acc_spec = pl.BlockSpec((T, HD), lambda h, i: (0, h))
    sh = jax.ShapeDtypeStruct((T, BW), F32)
    return _pcall(
        body, name="fox_bwd", grid=(N_HEADS, T // TQ),
        out_shape=[sh, sh, sh, jax.ShapeDtypeStruct((N_HEADS, T, 1), F32), jax.ShapeDtypeStruct((N_HEADS, 1, T), F32)],
        in_specs=[q_spec, k_spec, v_spec, col_spec, row_spec, col_spec, q_spec, q_spec],
        out_specs=[q_spec, acc_spec, acc_spec, col_spec, row_spec],
        sem=("parallel", "arbitrary"), args=(qkn, qkn, p, cumcol, cumrow, lse, o, do), duty=duty)


def _mix_fwd(gpre, b_mg, y0, y1, y2, duty=None):
    def body(g_ref, b_ref, y0_ref, y1_ref, y2_ref, o_ref):
        acc = None
        for n, y_ref in enumerate((y0_ref, y1_ref, y2_ref)):
            sl = slice(n * D, (n + 1) * D)
            t = _sigmoid(g_ref[:, sl] + b_ref[:, sl]) * y_ref[...]
            acc = t if acc is None else acc + t
        o_ref[...] = acc.astype(BF16)

    return _pcall(
        body, name="mix_fwd", grid=(T // ROW_TILE,), out_shape=jax.ShapeDtypeStruct((T, D), BF16),
        in_specs=[_row_spec(3 * D), _vec_spec(3 * D), _row_spec(), _row_spec(), _row_spec()],
        out_specs=_row_spec(), sem=("parallel",), args=(gpre, b_mg, y0, y1, y2), duty=duty)


def _mix_bwd(gpre, b_mg, y0, y1, y2, dmi, duty=None):
    def body(g_ref, b_ref, y0_ref, y1_ref, y2_ref, d_ref, dy0_ref, dy1_ref, dy2_ref, dg_ref, db_ref):
        @pl.when(pl.program_id(0) == 0)
        def _():
            db_ref[...] = jnp.zeros_like(db_ref)

        dv = d_ref[...]
        for n, (y_ref, dy_ref) in enumerate(((y0_ref, dy0_ref), (y1_ref, dy1_ref), (y2_ref, dy2_ref))):
            sl = slice(n * D, (n + 1) * D)
            sg = _sigmoid(g_ref[:, sl] + b_ref[:, sl])
            dy_ref[...] = (dv * sg).astype(BF16)
            dpre = dv * y_ref[...] * (sg * (1.0 - sg))
            dg_ref[:, sl] = dpre.astype(BF16)
            db_ref[0:1, sl] += _sum0(dpre)

    shb = jax.ShapeDtypeStruct((T, D), BF16)
    return _pcall(
        body, name="mix_bwd", grid=(T // ROW_TILE,),
        out_shape=[shb, shb, shb, jax.ShapeDtypeStruct((T, 3 * D), BF16), jax.ShapeDtypeStruct((8, 3 * D), F32)],
        in_specs=[_row_spec(3 * D), _vec_spec(3 * D), _row_spec(), _row_spec(), _row_spec(), _row_spec()],
        out_specs=[_row_spec(), _row_spec(), _row_spec(), _row_spec(3 * D), pl.BlockSpec((8, 3 * D), lambda i: (0, 0))],
        sem=("arbitrary",), args=(gpre, b_mg, y0, y1, y2, dmi), duty=duty)


FF_COLS = 256
FF_NBLK = D_FF // FF_COLS


def _shift_rows(a, n):
    rows = lax.broadcasted_iota(jnp.int32, a.shape, 0)
    rolled = pltpu.roll(a, n % T, 0)
    return jnp.where((rows >= n) if n > 0 else (rows < T + n), rolled, 0.0)


def _ffn_act_fwd(uu, w_conv, b_conv, duty=None):
    def body(u_ref, g_ref, w_ref, b_ref, o_ref):
        u = u_ref[...]
        w = w_ref[...]
        uc = b_ref[...] + w[0:1, :] * _shift_rows(u, 2) + w[1:2, :] * _shift_rows(u, 1) + w[2:3, :] * u
        o_ref[...] = (uc * _sigmoid(uc) * g_ref[...]).astype(BF16)

    return _pcall(
        body, name="ffn_act_fwd", grid=(FF_NBLK,), out_shape=jax.ShapeDtypeStruct((T, D_FF), BF16),
        in_specs=[pl.BlockSpec((T, FF_COLS), lambda j: (0, j)), pl.BlockSpec((T, FF_COLS), lambda j: (0, FF_NBLK + j)),
                  pl.BlockSpec((3, FF_COLS), lambda j: (0, j)), pl.BlockSpec((1, FF_COLS), lambda j: (0, j))],
        out_specs=pl.BlockSpec((T, FF_COLS), lambda j: (0, j)),
        sem=("parallel",), args=(uu, uu, w_conv, b_conv), duty=duty)


def _ffn_act_bwd(uu, w_conv, b_conv, da, duty=None):
    def body(u_ref, g_ref, w_ref, b_ref, da_ref, d_ref, st_ref):
        u, w, dav = u_ref[...], w_ref[...], da_ref[...]
        u1, u2 = _shift_rows(u, 1), _shift_rows(u, 2)
        uc = b_ref[...] + w[0:1, :] * u2 + w[1:2, :] * u1 + w[2:3, :] * u
        sg = _sigmoid(uc)
        d_ref[1] = (dav * (uc * sg)).astype(BF16)
        duc = dav * g_ref[...] * (sg * (1.0 + uc * (1.0 - sg)))
        du = w[2:3, :] * duc + w[1:2, :] * _shift_rows(duc, -1) + w[0:1, :] * _shift_rows(duc, -2)
        d_ref[0] = du.astype(BF16)
        st_ref[...] = jnp.zeros_like(st_ref)
        st_ref[0:1, :] = _sum0(duc * u2)
        st_ref[1:2, :] = _sum0(duc * u1)
        st_ref[2:3, :] = _sum0(duc * u)
        st_ref[3:4, :] = _sum0(duc)

    cb = lambda rows=T, off=0: pl.BlockSpec((rows, FF_COLS), lambda j: (0, off + j))
    return _pcall(
        body, name="ffn_act_bwd", grid=(FF_NBLK,),
        out_shape=[jax.ShapeDtypeStruct((2, T, D_FF), BF16), jax.ShapeDtypeStruct((8, D_FF), F32)],
        in_specs=[cb(), cb(T, FF_NBLK), cb(3), cb(1), cb()],
        out_specs=[pl.BlockSpec((2, T, FF_COLS), lambda j: (0, 0, j)), cb(8)],
        sem=("parallel",), args=(uu, uu, w_conv, b_conv, da), duty=duty)


def _adamw(g, w, m, v, tr, name):
    partial = g.ndim == 3
    R, C = w.shape
    tr = R if tr is None else tr
    assert R % tr == 0

    def body(g_ref, w_ref, m_ref, v_ref, go_ref, d_ref, mo_ref, vo_ref):
        if partial:
            gv = g_ref[0].astype(F32)
            for j in range(1, N_DEV):
                gv = gv + g_ref[j].astype(F32)
        else:
            gv = g_ref[...]
        go_ref[...] = gv
        mn = ADAM_B1 * m_ref[...] + (1.0 - ADAM_B1) * gv
        vn = ADAM_B2 * v_ref[...] + (1.0 - ADAM_B2) * (gv * gv)
        mo_ref[...] = mn
        vo_ref[...] = vn
        m_hat = mn / (1.0 - ADAM_B1 ** ADAM_STEP)
        v_hat = vn / (1.0 - ADAM_B2 ** ADAM_STEP)
        d_ref[...] = -ADAM_LR * (m_hat / (jnp.sqrt(v_hat) + ADAM_EPS) + ADAM_WD * w_ref[...])

    spec = pl.BlockSpec((tr, C), lambda i: (i, 0))
    g_spec = pl.BlockSpec((N_DEV, tr, C), lambda i: (0, i, 0)) if partial else spec
    sh = jax.ShapeDtypeStruct((R, C), F32)
    return pl.pallas_call(
        body, name=name, grid=(R // tr,), out_shape=[sh, sh, sh, sh],
        in_specs=[g_spec, spec, spec, spec], out_specs=[spec, spec, spec, spec],
        compiler_params=_cp("parallel"))(g, w, m, v)


def _sum_partials(g, name, tr=None):
    _, R, C = g.shape
    tr = R if tr is None else tr
    assert R % tr == 0

    def body(g_ref, o_ref):
        acc = g_ref[0].astype(F32)
        for j in range(1, N_DEV):
            acc = acc + g_ref[j].astype(F32)
        o_ref[...] = acc

    return pl.pallas_call(
        body, name=name, grid=(R // tr,), out_shape=jax.ShapeDtypeStruct((R, C), F32),
        in_specs=[pl.BlockSpec((N_DEV, tr, C), lambda i: (0, i, 0))], out_specs=pl.BlockSpec((tr, C), lambda i: (i, 0)),
        compiler_params=_cp("parallel"))(g)


def _permute_in(w):
    pad = jnp.zeros(w.shape[:-1] + (NP - IN_W,), w.dtype)
    return jnp.concatenate([w[..., :3072], w[..., 3088:5136], w[..., 3072:3088], w[..., 5136:5140], pad], axis=-1)


def _unpermute_in(w):
    return jnp.concatenate([w[..., :3072], w[..., 5120:5136], w[..., 3072:5120], w[..., 5136:5140]], axis=-1)


def _flat_pack(arrs):
    flat = jnp.concatenate([a.reshape(-1).astype(F32) for a in arrs])
    n = flat.shape[0]
    rows = -(-n // 1024) * 8
    return jnp.pad(flat, (0, rows * 128 - n)).reshape(rows, 128)


def _flat_unpack(buf, shapes):
    flat = buf.reshape(-1)
    out, off = [], 0
    for s in shapes:
        n = int(np.prod(s))
        out.append(flat[off:off + n].reshape(s))
        off += n
    return out


GRAD_CHUNKS = dict(w_in=(128, 4), w_o=(128, 1), w_down=(352, 2), w_br0=(128, 1), w_br1=(128, 1), w_br2=(128, 1),
                   w_mg=(384, 2), w_up=(704, 4))


def _weight(wl, k):
    return wl[k]() if callable(wl[k]) else wl[k]


def _taker(xfer):
    return (lambda count: None) if xfer is None else xfer.take


def _layer_fwd(x0, wl, consts, xfer=None):
    cosf, sinf, logg = consts
    row = lambda a: a.reshape(1, -1)
    take = _taker(xfer)
    h = _norm_fwd(x0, row(wl["norm1_g"]), row(wl["scale1"]), row(wl["shift1"]), "norm1_fwd")
    p = _matmul(h, _weight(wl, "w_in"), "nn", "in_proj", duty=take(2))
    qk = _rope_fwd(p, cosf, sinf)
    ret_raw = _ret_fwd(qk, p, logg, duty=take(2))
    br0 = _branch_post_fwd(ret_raw, p, row(wl["ret_norm_g"]), RG_BLK, True, "ret_post_fwd")
    gla_raw, states = _gla_fwd(p, wl["w2pad"], row(wl["b_gla_a"]), duty=take(1))
    br1 = _branch_post_fwd(gla_raw, p, row(wl["gla_norm_g"]), GG_BLK, False, "gla_post_fwd")
    qkn, cum = _fox_prep_fwd(p, row(wl["q_norm_g"]), row(wl["k_norm_g"]), row(wl["btail"]))
    cum4 = cum[:, FF_LANE0:FF_LANE0 + N_HEADS].T
    cumcol, cumrow = cum4.reshape(N_HEADS, T, 1), cum4.reshape(N_HEADS, 1, T)
    fox_o, lse = _fox_fwd(qkn, p, cumcol, cumrow, duty=take(2))
    w_br_t = _weight(wl, "w_br_t")
    ys = [_matmul(b, w_br_t[n], "nt", "br_proj%d" % n) for n, b in enumerate((br0, br1, fox_o))]
    gpre = _matmul(h, _weight(wl, "w_mg_t"), "nt", "gate_proj", duty=take(1))
    mixed_in = _mix_fwd(gpre, row(wl["b_mg"]), *ys, duty=take(1))
    mixed = _matmul(mixed_in, _weight(wl, "w_o"), "nn", "o_proj")
    x1, h2 = _norm_fwd(x0, row(wl["norm2_g"]), row(wl["scale2"]), row(wl["shift2"]), "norm2_fwd",
                       m=mixed, gate=row(wl["gate1"]))
    uu = _matmul(h2, _weight(wl, "w_up_t"), "nt", "up_proj", duty=take(2))
    act = _ffn_act_fwd(uu, wl["w_conv"], row(wl["b_conv"]), duty=take(1))
    y = _matmul(act, _weight(wl, "w_down"), "nn", "down_proj", tk=1408, duty=take(2))
    x2 = _axpy(x1, y, row(wl["gate2"]), "resid2")
    saved = dict(x0=x0, h=h, p=p, qk=qk, ret_raw=ret_raw, br0=br0, gla_raw=gla_raw, states=states, br1=br1,
                 qkn=qkn, cumcol=cumcol, cumrow=cumrow, fox_o=fox_o, lse=lse, y0=ys[0], y1=ys[1], y2=ys[2],
                 gpre=gpre, mixed_in=mixed_in, mixed=mixed, x1=x1, h2=h2, uu=uu, act=act, y=y)
    return x2, saved


def _layer_bwd(dx2, wl, sv, consts, xfer=None, layer=0):
    cosf, sinf, logg = consts
    row = lambda a: a.reshape(1, -1)
    take = _taker(xfer)

    def send(k, g):
        if xfer is not None:
            n, parts = GRAD_CHUNKS[k]
            for c in range(parts):
                xfer.add(k, ("a2a", 0, n, (layer,), c * (n // parts), n // parts), g)
        return g

    dy, st_g2 = _gate_bwd(dx2, sv["y"], row(wl["gate2"]), "gate2_bwd")
    dact = _matmul(dy, _weight(wl, "w_down"), "nt", "down_dx", tn=1408, duty=take(1))
    d_down = send("w_down", _matmul(sv["act"], dy, "tn", "down_dw", out_dtype=BF16, tm=1408, duty=take(1)))
    duu, st_conv = _ffn_act_bwd(sv["uu"], wl["w_conv"], row(wl["b_conv"]), dact, duty=take(1))
    dh2 = _matmul(duu, _weight(wl, "w_up_t"), "nn", "up_dx", tk=1408, duty=take(2))
    d_up_t = send("w_up", _matmul(duu, sv["h2"], "tn", "up_dw", out_dtype=BF16, tm=1408, duty=take(1)))
    dx1, st_n2 = _norm_bwd(sv["x1"], dh2, dx2, row(wl["norm2_g"]), row(wl["scale2"]), row(wl["shift2"]), "norm2_bwd")
    dmixed, st_g1 = _gate_bwd(dx1, sv["mixed"], row(wl["gate1"]), "gate1_bwd")
    dmi = _matmul(dmixed, _weight(wl, "w_o"), "nt", "o_dx")
    d_o = send("w_o", _matmul(sv["mixed_in"], dmixed, "tn", "o_dw", out_dtype=BF16))
    dy0, dy1, dy2, dgpre, st_bmg = _mix_bwd(sv["gpre"], row(wl["b_mg"]), sv["y0"], sv["y1"], sv["y2"], dmi, duty=take(1))
    brs = (sv["br0"], sv["br1"], sv["fox_o"])
    w_br_t = _weight(wl, "w_br_t")
    dbr = [_matmul(d, w_br_t[n], "nn", "br_dx%d" % n) for n, d in enumerate((dy0, dy1, dy2))]
    d_br_t = [send("w_br%d" % n, _matmul(d, brs[n], "tn", "br_dw%d" % n, out_dtype=BF16))
              for n, d in enumerate((dy0, dy1, dy2))]
    dh = _matmul(dgpre, _weight(wl, "w_mg_t"), "nn", "gate_dx", tk=1024, duty=take(1))
    d_mg_t = send("w_mg", _matmul(dgpre, sv["h"], "tn", "gate_dw", out_dtype=BF16, duty=take(1)))
    p = sv["p"]
    dqn, dkn, dfv, drow, dcol = _fox_bwd(sv["qkn"], p, sv["cumcol"], sv["cumrow"], sv["lse"], sv["fox_o"], dbr[2],
                                         duty=take(2))
    dcum4 = drow.reshape(N_HEADS, T) - dcol.reshape(N_HEADS, T)
    dcum = jnp.pad(dcum4.T, ((0, 0), (FF_LANE0, 128 - FF_LANE0 - N_HEADS)))
    dfqk, dtail_fox, st_fox = _fox_prep_bwd(p, row(wl["q_norm_g"]), row(wl["k_norm_g"]), row(wl["btail"]), dqn, dkn, dcum)
    dgla_raw, dgg, st_gn = _branch_post_bwd(sv["gla_raw"], p, row(wl["gla_norm_g"]), dbr[1], GG_BLK, False, "gla_post_bwd")
    dgq, dgk, dgv, dtail_gla, dw2pad, st_bg = _gla_bwd(p, wl["w2pad"], row(wl["b_gla_a"]), sv["states"], dgla_raw,
                                                       duty=take(1))
    dret_raw, drg, st_rn = _branch_post_bwd(sv["ret_raw"], p, row(wl["ret_norm_g"]), dbr[0], RG_BLK, True, "ret_post_bwd")
    dqr, dkr, drv = _ret_bwd(sv["qk"], p, logg, dret_raw, duty=take(2))
    drqk = _rope_bwd(dqr, dkr, cosf, sinf)
    dp = jnp.concatenate([a.astype(BF16) for a in (drqk, drv, drg, dgq, dgk, dgv, dgg, dfqk, dfv, dtail_fox + dtail_gla)]
                         + [jnp.zeros((T, NP - TAIL0 - 128), BF16)], axis=1)
    dh = _matmul(dp, _weight(wl, "w_in"), "nt", "in_dx", tk=1408, add=dh, duty=take(2))
    d_in = send("w_in", _matmul(sv["h"], dp, "tn", "in_dw", out_dtype=BF16, duty=take(1)))
    dx0, st_n1 = _norm_bwd(sv["x0"], dh, dx1, row(wl["norm1_g"]), row(wl["scale1"]), row(wl["shift1"]), "norm1_bwd")
    big = dict(w_in=d_in, w_o=d_o, w_down=d_down, w_br0=d_br_t[0], w_br1=d_br_t[1], w_br2=d_br_t[2], w_mg=d_mg_t,
               w_up=d_up_t)
    dmod = jnp.concatenate([st_n1[2], st_n1[1], st_g1[0], st_n2[2], st_n2[1], st_g2[0]])
    small = dict(norm1_g=st_n1[0], norm2_g=st_n2[0], b_gla_a=st_bg[0], b_fox_f=st_fox[2, FF_LANE0:FF_LANE0 + N_HEADS],
                 ret_norm_g=st_rn[0], gla_norm_g=st_gn[0], q_norm_g=st_fox[0], k_norm_g=st_fox[1], b_mg=st_bmg[0],
                 b_conv=st_conv[3], w_gla_a2=dw2pad[:LR_LANES], w_conv=st_conv[0:3])
    return dx0, big, dmod, small


SMALL_REPL = ("norm1_g", "norm2_g", "b_ada", "b_gla_a", "b_fox_f", "ret_norm_g", "gla_norm_g", "q_norm_g", "k_norm_g",
              "b_mg", "b_conv")
SMALL_SHARDED = ("w_gla_a2", "w_conv")
BIG = ("w_in", "w_o", "w_down", "w_br", "w_mg", "w_up")
WEIGHTS = ("norm1_g", "norm2_g", "w_ada", "b_ada", "w_in", "w_gla_a2", "b_gla_a", "b_fox_f", "ret_norm_g", "gla_norm_g",
           "q_norm_g", "k_norm_g", "w_br", "w_mg", "b_mg", "w_o", "w_up", "w_conv", "b_conv", "w_down")


def kernel(x, c, norm1_g, norm2_g, w_ada, b_ada, w_in, w_gla_a2, b_gla_a, b_fox_f, ret_norm_g, gla_norm_g, q_norm_g, k_norm_g, w_br, w_mg, b_mg, w_o, w_up, w_conv, b_conv, w_down, loss_target, m_norm1_g, m_norm2_g, m_w_ada, m_b_ada, m_w_in, m_w_gla_a2, m_b_gla_a, m_b_fox_f, m_ret_norm_g, m_gla_norm_g, m_q_norm_g, m_k_norm_g, m_w_br, m_w_mg, m_b_mg, m_w_o, m_w_up, m_w_conv, m_b_conv, m_w_down, v_norm1_g, v_norm2_g, v_w_ada, v_b_ada, v_w_in, v_w_gla_a2, v_b_gla_a, v_b_fox_f, v_ret_norm_g, v_gla_norm_g, v_q_norm_g, v_k_norm_g, v_w_br, v_w_mg, v_b_mg, v_w_o, v_w_up, v_w_conv, v_b_conv, v_w_down):
    W = dict(norm1_g=norm1_g, norm2_g=norm2_g, w_ada=w_ada, b_ada=b_ada, w_in=w_in, w_gla_a2=w_gla_a2, b_gla_a=b_gla_a,
             b_fox_f=b_fox_f, ret_norm_g=ret_norm_g, gla_norm_g=gla_norm_g, q_norm_g=q_norm_g, k_norm_g=k_norm_g,
             w_br=w_br, w_mg=w_mg, b_mg=b_mg, w_o=w_o, w_up=w_up, w_conv=w_conv, b_conv=b_conv, w_down=w_down)
    M = dict(norm1_g=m_norm1_g, norm2_g=m_norm2_g, w_ada=m_w_ada, b_ada=m_b_ada, w_in=m_w_in, w_gla_a2=m_w_gla_a2,
             b_gla_a=m_b_gla_a, b_fox_f=m_b_fox_f, ret_norm_g=m_ret_norm_g, gla_norm_g=m_gla_norm_g, q_norm_g=m_q_norm_g,
             k_norm_g=m_k_norm_g, w_br=m_w_br, w_mg=m_w_mg, b_mg=m_b_mg, w_o=m_w_o, w_up=m_w_up, w_conv=m_w_conv,
             b_conv=m_b_conv, w_down=m_w_down)
    V = dict(norm1_g=v_norm1_g, norm2_g=v_norm2_g, w_ada=v_w_ada, b_ada=v_b_ada, w_in=v_w_in, w_gla_a2=v_w_gla_a2,
             b_gla_a=v_b_gla_a, b_fox_f=v_b_fox_f, ret_norm_g=v_ret_norm_g, gla_norm_g=v_gla_norm_g, q_norm_g=v_q_norm_g,
             k_norm_g=v_k_norm_g, w_br=v_w_br, w_mg=v_w_mg, b_mg=v_b_mg, w_o=v_w_o, w_up=v_w_up, w_conv=v_w_conv,
             b_conv=v_b_conv, w_down=v_w_down)
    me = 4 * lax.axis_index("x") + 2 * lax.axis_index("y") + lax.axis_index("c")
    x2d, tgt = x.reshape(T, D), loss_target.reshape(T, D)

    sm = _flat_pack([c, w_gla_a2, w_conv])
    sm_all = _exchange(sm, True, "gather_small")
    parts = [_flat_unpack(sm_all[j], [(D,), (DEPTH, LR_LANES, 32), (DEPTH, 3, 352)]) for j in range(N_DEV)]
    c_all = jnp.stack([q[0] for q in parts])
    w_gla_full = jnp.concatenate([q[1] for q in parts], axis=2)
    w_conv_full = jnp.concatenate([q[2] for q in parts], axis=2)

    n_ada = w_ada.shape[2]
    b_loc = lax.dynamic_slice_in_dim(b_ada, me * n_ada, n_ada, axis=1).reshape(DEPTH, 1, n_ada)
    mod_all = _ada_fwd(c_all, w_ada, b_loc)
    mod_recv = _exchange(jnp.swapaxes(mod_all, 0, 1), False, "a2a_mod")
    mod = jnp.swapaxes(mod_recv, 0, 1).reshape(DEPTH, 6, D)

    loc = dict(w_in=_permute_in(w_in), w_o=w_o, w_down=w_down, w_br=jnp.swapaxes(w_br, 2, 3),
               w_mg=jnp.swapaxes(w_mg, 1, 2), w_up=jnp.swapaxes(w_up, 1, 2))
    loc = {k: v.astype(BF16) for k, v in loc.items()}
    w_full = dict(w_in=(D, NP), w_o=(D, D), w_down=(D_FF, D), w_br=(3, D, BW), w_mg=(3 * D, D), w_up=(2 * D_FF, D))
    w_parts = dict(w_in=4, w_br=1, w_mg=2, w_o=1, w_up=4, w_down=2)
    gather = _Transfers("gather")
    for l in range(DEPTH):
        for k, parts in w_parts.items():
            axis = 1 if k == "w_br" else 0
            n = w_full[k][axis] // N_DEV
            gather.lands[(l, k)] = lax.empty(w_full[k], BF16)
            shard = loc[k][l]
            for c in range(parts):
                gather.add((l, k), ("gather", axis, n, (), c * (n // parts), n // parts), shard)

    w2pad = jnp.pad(w_gla_full, ((0, 0), (0, 128 - LR_LANES), (0, 0)))
    btail = jnp.pad(b_fox_f, ((0, 0), (FF_LANE0, 128 - FF_LANE0 - N_HEADS)))
    stacked = dict(norm1_g=norm1_g, norm2_g=norm2_g, b_gla_a=b_gla_a, ret_norm_g=ret_norm_g, gla_norm_g=gla_norm_g,
                   q_norm_g=q_norm_g, k_norm_g=k_norm_g, b_mg=b_mg, b_conv=b_conv, w_conv=w_conv_full, w2pad=w2pad,
                   btail=btail, shift1=mod[:, 0], scale1=mod[:, 1], gate1=mod[:, 2], shift2=mod[:, 3], scale2=mod[:, 4],
                   gate2=mod[:, 5])
    landed = lambda l, k: functools.partial(gather.get, (l, k))
    layers = [dict({k: v[l] for k, v in stacked.items()}, w_in=landed(l, "w_in"), w_o=landed(l, "w_o"),
                   w_down=landed(l, "w_down"), w_br_t=landed(l, "w_br"), w_mg_t=landed(l, "w_mg"), w_up_t=landed(l, "w_up"))
              for l in range(DEPTH)]
    consts = _rope_tables() + (_ret_logg(),)

    xc, saved = x2d, []
    for l in range(DEPTH):
        xc, sv = _layer_fwd(xc, layers[l], consts, gather)
        saved.append(sv)
    loss_part, dxc = _loss_fwd_bwd(xc, tgt)
    loss = lax.psum(loss_part[0, 0], ("x", "y", "c"))

    grad_names = ("w_in", "w_o", "w_down", "w_br0", "w_br1", "w_br2", "w_mg", "w_up")
    blk_rows = dict(w_in=(128, NP), w_o=(128, D), w_down=(352, D), w_br0=(128, BW), w_br1=(128, BW), w_br2=(128, BW),
                    w_mg=(384, D), w_up=(704, D))
    grads = _Transfers("grads")
    for k in grad_names:
        grads.lands[k] = lax.empty((N_DEV, DEPTH) + blk_rows[k], BF16)
    dmod, small_g = [None] * DEPTH, [None] * DEPTH
    for l in reversed(range(DEPTH)):
        dxc, _, dmod[l], small_g[l] = _layer_bwd(dxc, layers[l], saved[l], consts, grads, l)
    grad_x = dxc
    dmod = jnp.stack(dmod)
    small_g = {k: jnp.stack([s[k] for s in small_g]) for k in small_g[0]}
    grads.drain()
    recv = {k: grads.get(k) for k in grad_names}

    dmod_send = jnp.swapaxes(dmod.reshape(DEPTH, N_DEV, n_ada), 0, 1)
    dmod_all = jnp.swapaxes(_exchange(dmod_send, False, "a2a_dmod"), 0, 1)
    g_ada = _ada_bwd(c_all, dmod_all)

    def flat(a, k):
        return a.reshape((-1, W[k].shape[-1]))

    def adam_nat(k, g, tr):
        outs = _adamw(g, flat(W[k], k), flat(M[k], k), flat(V[k], k), tr, "adamw_" + k)
        return [o.reshape(W[k].shape) for o in outs]

    def summed(k, tr):
        r = recv[k]
        return _sum_partials(r.reshape(N_DEV, DEPTH * r.shape[2], r.shape[3]), "sum_" + k, tr).reshape((DEPTH,) + r.shape[2:])

    big_out = dict(
        w_in=adam_nat("w_in", flat(_unpermute_in(summed("w_in", 64)), "w_in"), 64),
        w_o=adam_nat("w_o", recv["w_o"].reshape(N_DEV, DEPTH * 128, D), 128),
        w_down=adam_nat("w_down", recv["w_down"].reshape(N_DEV, DEPTH * 352, D), 352),
        w_br=adam_nat("w_br", flat(jnp.swapaxes(jnp.stack([summed("w_br%d" % n, 128) for n in range(3)], axis=1), 2, 3),
                                   "w_br"), 1024),
        w_mg=adam_nat("w_mg", flat(jnp.swapaxes(summed("w_mg", 384), 1, 2), "w_mg"), 512),
        w_up=adam_nat("w_up", flat(jnp.swapaxes(summed("w_up", 704), 1, 2), "w_up"), 512))
    ada_out = [o.reshape(DEPTH, D, n_ada) for o in _adamw(
        g_ada.reshape(DEPTH * D, n_ada), w_ada.reshape(DEPTH * D, n_ada), m_w_ada.reshape(DEPTH * D, n_ada),
        v_w_ada.reshape(DEPTH * D, n_ada), 512, "adamw_ada")]

    small_g = dict(small_g, b_ada=dmod)
    names = SMALL_REPL + SMALL_SHARDED
    full_shapes = [W[n].shape for n in SMALL_REPL] + [(DEPTH, LR_LANES, 256), (DEPTH, 3, D_FF)]
    part = _flat_pack([small_g[n] for n in names])
    total = _flat_unpack(_sum_partials(_exchange(part, True, "gather_small_grads"), "sum_small"), full_shapes)
    total = dict(zip(names, total))
    total["w_gla_a2"] = lax.dynamic_slice_in_dim(total["w_gla_a2"], me * 32, 32, axis=2)
    total["w_conv"] = lax.dynamic_slice_in_dim(total["w_conv"], me * 352, 352, axis=2)
    shapes = [W[n].shape for n in names]
    small_out = _adamw(_flat_pack([total[n] for n in names]), _flat_pack([W[n] for n in names]),
                       _flat_pack([M[n] for n in names]), _flat_pack([V[n] for n in names]), None, "adamw_small")
    small_out = [dict(zip(names, _flat_unpack(o, shapes))) for o in small_out]

    outs = []
    for k in range(4):
        d = dict(small_out[k])
        d.update({n: big_out[n][k] for n in BIG})
        d["w_ada"] = ada_out[k]
        outs.append([d[n] for n in WEIGHTS])
    return (loss, grad_x.reshape(1, T, D), *outs[0], *outs[1], *outs[2], *outs[3])
```

```python
import functools

import numpy as np
import jax
import jax.numpy as jnp
from jax import lax
from jax.experimental import pallas as pl
from jax.experimental.pallas import tpu as pltpu

F32 = jnp.float32
BF16 = jnp.bfloat16

N_DEV = 8
T = 2048
D = 1024
DEPTH = 4
N_HEADS = 4
HD = 128
BW = 512
D_FF = 2816
CHUNK = 64
EPS = 1e-6
IN_W = 5140
NP = 5632
TAIL0 = 5120
LR_LANES = 16
FF_LANE0 = 16
PACK_W = 1024
SEG_ROWS = (704, 128, 352, 192, 384, 704)
LAYER_ROWS = sum(SEG_ROWS)
VMEM_LIMIT_V7X = 56 * 1024 * 1024

ADAM_LR, ADAM_B1, ADAM_B2, ADAM_EPS, ADAM_WD, ADAM_STEP = 0.001, 0.9, 0.999, 1e-08, 0.01, 10

MESH_ID = pl.DeviceIdType.MESH


def _cp(*sem):
    return pltpu.CompilerParams(dimension_semantics=sem if sem else None, vmem_limit_bytes=VMEM_LIMIT_V7X)


def _sigmoid(z):
    return 1.0 / (1.0 + jnp.exp(-z))


def _log_sigmoid(z):
    return jnp.minimum(z, 0.0) - jnp.log(1.0 + jnp.exp(-jnp.abs(z)))


def _sum0(a):
    return jnp.sum(a, axis=0, keepdims=True)


def _mean1(a):
    return jnp.mean(a, axis=-1, keepdims=True)


def _dot(a, b, dims):
    return lax.dot_general(a.astype(BF16), b.astype(BF16), (dims, ((), ())), preferred_element_type=F32)


NN = ((1,), (0,))
NT = ((1,), (1,))
TN = ((0,), (0,))


def _exact_dot(m01, a):
    a1 = a.astype(BF16)
    r1 = a - a1.astype(F32)
    a2 = r1.astype(BF16)
    a3 = (r1 - a2.astype(F32)).astype(BF16)
    d = lambda z: jnp.dot(m01, z, preferred_element_type=F32)
    return d(a1) + d(a2) + d(a3)


def _tri(n, upper):
    r = lax.broadcasted_iota(jnp.int32, (n, n), 0)
    c = lax.broadcasted_iota(jnp.int32, (n, n), 1)
    return jnp.where((c >= r) if upper else (c <= r), 1.0, 0.0).astype(BF16)


def _exchange(x, gather, name):
    blk = x.shape if gather else x.shape[1:]

    def body(x_ref, o_ref, send_sems, recv_sems, loc_sem):
        mx, my, mc = lax.axis_index("x"), lax.axis_index("y"), lax.axis_index("c")
        me = 4 * mx + 2 * my + mc
        loc = pltpu.make_async_copy(x_ref if gather else x_ref.at[me], o_ref.at[me], loc_sem)
        loc.start()
        copies = []
        for k in range(1, N_DEV):
            px = mx ^ (k >> 2) if (k >> 2) else mx
            py = my ^ ((k >> 1) & 1) if ((k >> 1) & 1) else my
            pc = mc ^ (k & 1) if (k & 1) else mc
            peer = 4 * px + 2 * py + pc
            cp = pltpu.make_async_remote_copy(
                src_ref=x_ref if gather else x_ref.at[peer], dst_ref=o_ref.at[me],
                send_sem=send_sems.at[k - 1], recv_sem=recv_sems.at[k - 1],
                device_id=(px, py, pc), device_id_type=MESH_ID)
            cp.start()
            copies.append(cp)
        for cp in copies:
            cp.wait()
        loc.wait()

    return pl.pallas_call(
        body, name=name,
        out_shape=jax.ShapeDtypeStruct((N_DEV,) + tuple(blk), x.dtype),
        in_specs=[pl.BlockSpec(memory_space=pl.ANY)],
        out_specs=pl.BlockSpec(memory_space=pl.ANY),
        scratch_shapes=[pltpu.SemaphoreType.DMA((N_DEV - 1,)), pltpu.SemaphoreType.DMA((N_DEV - 1,)),
                        pltpu.SemaphoreType.DMA],
        compiler_params=pltpu.CompilerParams(has_side_effects=True),
    )(x)


def _blk(ref, axis, j, n, r0=0, nr=None):
    return ref.at[(slice(None),) * axis + (pl.ds(j * n + r0, n if nr is None else nr),)]


def _comm_copies(items, srcs, lands, send_sems, recv_sems, loc_sems):
    mx, my, mc = lax.axis_index("x"), lax.axis_index("y"), lax.axis_index("c")
    me = 4 * mx + 2 * my + mc
    local, remote = [], []
    for t, (kind, axis, n, sel, r0, nr, si, li) in enumerate(items):
        if kind == "pass_on":
            for q in (2, 4, 6):
                px = 1 - mx if q & 4 else mx
                py = 1 - my if q & 2 else my
                rows = _blk(lands[li], axis, 4 * px + 2 * py + mc, n, r0, nr)
                remote.append(pltpu.make_async_remote_copy(
                    src_ref=rows, dst_ref=rows, send_sem=send_sems.at[t * (N_DEV - 1) + q - 1],
                    recv_sem=recv_sems.at[t * (N_DEV - 1) + q - 1], device_id=(mx, my, 1 - mc), device_id_type=MESH_ID))
            continue
        if kind == "a2a":
            mine = lands[li].at[(me,) + tuple(sel) + (pl.ds(r0, nr),)]
            own = _blk(srcs[si], axis, me, n, r0, nr)
        else:
            mine = _blk(lands[li], axis, me, n, r0, nr)
            own = _blk(srcs[si], axis, 0, n, r0, nr)
        local.append(pltpu.make_async_copy(own, mine, loc_sems.at[t]))
        for k in ((1, 2, 4, 6) if kind == "gather_chip" else range(1, N_DEV)):
            px = 1 - mx if k & 4 else mx
            py = 1 - my if k & 2 else my
            pc = 1 - mc if k & 1 else mc
            src = _blk(srcs[si], axis, 4 * px + 2 * py + pc, n, r0, nr) if kind == "a2a" else own
            remote.append(pltpu.make_async_remote_copy(
                src_ref=src, dst_ref=mine, send_sem=send_sems.at[t * (N_DEV - 1) + k - 1],
                recv_sem=recv_sems.at[t * (N_DEV - 1) + k - 1], device_id=(px, py, pc), device_id_type=MESH_ID))
    return local, remote


def _comm_scratch(n_items):
    return [pltpu.SemaphoreType.DMA((n_items * (N_DEV - 1),)), pltpu.SemaphoreType.DMA((n_items * (N_DEV - 1),)),
            pltpu.SemaphoreType.DMA((n_items,))]


class _Duty:
    def __init__(self, items, srcs, lands, done):
        self.items, self.srcs, self.lands, self.done = items, srcs, lands, done


def _pcall(body, name, grid, in_specs, out_specs, out_shape, args, scratch_shapes=(), sem=(), duty=None):
    if duty is None:
        return pl.pallas_call(body, name=name, grid=grid, in_specs=list(in_specs), out_specs=out_specs,
                              out_shape=out_shape, scratch_shapes=list(scratch_shapes), compiler_params=_cp(*sem))(*args)
    single = not isinstance(out_shape, (list, tuple))
    o_shape = [out_shape] if single else list(out_shape)
    o_specs = [out_specs] if single else list(out_specs)
    n_in, n_out, n_scr = len(in_specs), len(o_shape), len(scratch_shapes)
    n_src, n_land, n_items = len(duty.srcs), len(duty.lands), len(duty.items)
    a0 = n_in + n_src + n_land

    def wrapped(*refs):
        srcs = refs[n_in:n_in + n_src]
        lands = refs[a0 + n_out:a0 + n_out + n_land]
        core = refs[:n_in] + refs[a0:a0 + n_out] + refs[a0 + n_out + n_land:a0 + n_out + n_land + n_scr]
        sems = refs[a0 + n_out + n_land + n_scr:]
        first = functools.reduce(jnp.logical_and, [pl.program_id(a) == 0 for a in range(len(grid))])
        last = functools.reduce(jnp.logical_and, [pl.program_id(a) == g - 1 for a, g in enumerate(grid)])

        @pl.when(first)
        def _():
            local, remote = _comm_copies(duty.items, srcs, lands, *sems)
            for cp in local + remote:
                cp.start()

        body(*core)

        @pl.when(last)
        def _():
            local, remote = _comm_copies(duty.items, srcs, lands, *sems)
            for cp in remote + local:
                cp.wait()

    hbm = pl.BlockSpec(memory_space=pl.ANY)
    res = pl.pallas_call(
        wrapped, name=name, grid=grid,
        in_specs=list(in_specs) + [hbm] * (n_src + n_land), out_specs=o_specs + [hbm] * n_land,
        out_shape=o_shape + [jax.ShapeDtypeStruct(a.shape, a.dtype) for a in duty.lands],
        input_output_aliases={n_in + n_src + t: n_out + t for t in range(n_land)},
        scratch_shapes=list(scratch_shapes) + _comm_scratch(n_items),
        compiler_params=pltpu.CompilerParams(dimension_semantics=("arbitrary",) * len(grid),
                                             vmem_limit_bytes=VMEM_LIMIT_V7X, has_side_effects=True),
    )(*args, *duty.srcs, *duty.lands)
    duty.done(res[n_out:])
    return res[0] if single else res[:n_out]


def _comm(duty, name):
    n_src, n_land = len(duty.srcs), len(duty.lands)

    def body(*refs):
        local, remote = _comm_copies(duty.items, refs[:n_src], refs[n_src + n_land:n_src + 2 * n_land],
                                     *refs[n_src + 2 * n_land:])
        for cp in local + remote:
            cp.start()
        for cp in remote + local:
            cp.wait()

    hbm = pl.BlockSpec(memory_space=pl.ANY)
    duty.done(pl.pallas_call(
        body, name=name, out_shape=[jax.ShapeDtypeStruct(a.shape, a.dtype) for a in duty.lands],
        in_specs=[hbm] * (n_src + n_land), out_specs=[hbm] * n_land,
        input_output_aliases={n_src + t: t for t in range(n_land)},
        scratch_shapes=_comm_scratch(len(duty.items)), compiler_params=pltpu.CompilerParams(has_side_effects=True),
    )(*duty.srcs, *duty.lands))


class _Transfers:
    def __init__(self, name):
        self.name, self.queue, self.lands, self.flushes = name, [], {}, 0

    def add(self, key, item, src=None, uid=None, after=None):
        self.queue.append((key, item, src, uid, after))

    def take(self, count):
        units = []
        while self.queue and len(units) < count:
            after = self.queue[0][4]
            if after is not None and any(u[3] == after for u in units):
                break
            units.append(self.queue.pop(0))
        if not units:
            return None
        keys, srcs, items = [], [], []
        for key, item, src, _, _ in units:
            if key not in keys:
                keys.append(key)
            if src is not None and not any(src is s for s in srcs):
                srcs.append(src)
            si = [i for i, s in enumerate(srcs) if s is src][0] if src is not None else -1
            items.append(tuple(item) + (si, keys.index(key)))

        def done(new_lands):
            for key, arr in zip(keys, new_lands):
                self.lands[key] = arr

        return _Duty(items, srcs, [self.lands[k] for k in keys], done)

    def drain(self, upto=None):
        count = len(self.queue) if upto is None else upto
        while count > 0:
            duty = self.take(count)
            count -= len(duty.items)
            self.flushes += 1
            _comm(duty, "%s_flush%d" % (self.name, self.flushes))

    def get(self, key):
        pending = [i for i, u in enumerate(self.queue) if u[0] == key]
        if pending:
            self.drain(pending[-1] + 1)
        return self.lands[key]


def _matmul(a, b, mode, name, out_dtype=F32, tm=1024, tn=512, tk=None, add=None, duty=None):
    halves = a.ndim == 3
    if mode == "tn":
        K, M = a.shape[-2], a.shape[-1] * (2 if halves else 1)
        N = b.shape[1]
    else:
        M, K = a.shape[-2], a.shape[-1] * (2 if halves else 1)
        N = b.shape[0] if mode == "nt" else b.shape[1]
    tm, tn = min(tm, M), min(tn, N)
    tk = K if tk is None else tk
    nk = K // tk
    assert M % tm == 0 and N % tn == 0 and K % tk == 0, (name, M, N, K, tm, tn, tk)
    dims = {"nn": NN, "nt": NT, "tn": TN}[mode]
    has_add = add is not None

    def body(*refs):
        if has_add:
            a_ref, b_ref, add_ref, o_ref, acc_ref = refs
        else:
            a_ref, b_ref, o_ref, acc_ref = refs
        k = pl.program_id(2)
        part = _dot(a_ref[...], b_ref[...], dims)

        @pl.when(k == 0)
        def _():
            acc_ref[...] = part

        @pl.when(k > 0)
        def _():
            acc_ref[...] += part

        @pl.when(k == nk - 1)
        def _():
            r = acc_ref[...]
            if has_add:
                r = r + add_ref[...]
            o_ref[...] = r.astype(o_ref.dtype)

    if halves and mode == "tn":
        per = a.shape[-1] // tm
        a_spec = pl.BlockSpec((None, tk, tm), lambda i, j, k: (i // per, k, i % per))
    elif halves:
        per = a.shape[-1] // tk
        a_spec = pl.BlockSpec((None, tm, tk), lambda i, j, k: (k // per, i, k % per))
    elif mode == "tn":
        a_spec = pl.BlockSpec((tk, tm), lambda i, j, k: (k, i))
    else:
        a_spec = pl.BlockSpec((tm, tk), lambda i, j, k: (i, k))
    if mode == "nt":
        b_spec = pl.BlockSpec((tn, tk), lambda i, j, k: (j, k))
    else:
        b_spec = pl.BlockSpec((tk, tn), lambda i, j, k: (k, j))
    o_spec = pl.BlockSpec((tm, tn), lambda i, j, k: (i, j))
    in_specs = [a_spec, b_spec] + ([o_spec] if has_add else [])
    args = (a, b) + ((add,) if has_add else ())
    return _pcall(
        body, name=name, grid=(M // tm, N // tn, nk),
        out_shape=jax.ShapeDtypeStruct((M, N), out_dtype),
        in_specs=in_specs, out_specs=o_spec,
        scratch_shapes=[pltpu.VMEM((tm, tn), F32)],
        sem=("parallel", "parallel", "arbitrary"), args=args, duty=duty)


def _ada_fwd(c_all, w_ada, b_loc):
    n = w_ada.shape[2]

    def body(c_ref, w_ref, b_ref, o_ref):
        c = c_ref[...]
        o_ref[0] = _dot(c * _sigmoid(c), w_ref[0], NN) + b_ref[0]

    return pl.pallas_call(
        body, name="ada_fwd", grid=(DEPTH,),
        out_shape=jax.ShapeDtypeStruct((DEPTH, N_DEV, n), F32),
        in_specs=[pl.BlockSpec((N_DEV, D), lambda l: (0, 0)),
                  pl.BlockSpec((1, D, n), lambda l: (l, 0, 0)),
                  pl.BlockSpec((1, 1, n), lambda l: (l, 0, 0))],
        out_specs=pl.BlockSpec((1, N_DEV, n), lambda l: (l, 0, 0)),
        compiler_params=_cp("parallel"),
    )(c_all, w_ada, b_loc)


def _ada_bwd(c_all, dmod_all):
    n = dmod_all.shape[2]

    def body(c_ref, d_ref, o_ref):
        c = c_ref[...]
        o_ref[0] = _dot(c * _sigmoid(c), d_ref[0], TN)

    return pl.pallas_call(
        body, name="ada_bwd", grid=(DEPTH,),
        out_shape=jax.ShapeDtypeStruct((DEPTH, D, n), F32),
        in_specs=[pl.BlockSpec((N_DEV, D), lambda l: (0, 0)),
                  pl.BlockSpec((1, N_DEV, n), lambda l: (l, 0, 0))],
        out_specs=pl.BlockSpec((1, D, n), lambda l: (l, 0, 0)),
        compiler_params=_cp("parallel"),
    )(c_all, dmod_all)


ROW_TILE = 256


def _row_spec(w=D, col=0):
    return pl.BlockSpec((ROW_TILE, w), lambda i: (i, col))


def _vec_spec(w=D):
    return pl.BlockSpec((1, w), lambda i: (0, 0))


def _norm_fwd(x, g, scale, shift, name, m=None, gate=None):
    has_res = m is not None

    def body(*refs):
        if has_res:
            x_ref, m_ref, gate_ref, g_ref, sc_ref, sh_ref, xo_ref, h_ref = refs
            xv = x_ref[...] + gate_ref[...] * m_ref[...]
            xo_ref[...] = xv
        else:
            x_ref, g_ref, sc_ref, sh_ref, h_ref = refs
            xv = x_ref[...]
        r = lax.rsqrt(_mean1(xv * xv) + EPS)
        h_ref[...] = ((xv * r * g_ref[...]) * (1.0 + sc_ref[...]) + sh_ref[...]).astype(BF16)

    ins = [x] + ([m, gate] if has_res else []) + [g, scale, shift]
    in_specs = [_row_spec()] + ([_row_spec(), _vec_spec()] if has_res else []) + [_vec_spec()] * 3
    out_shape = [jax.ShapeDtypeStruct((T, D), BF16)]
    out_specs = [_row_spec()]
    if has_res:
        out_shape = [jax.ShapeDtypeStruct((T, D), F32)] + out_shape
        out_specs = [_row_spec()] + out_specs
    out = pl.pallas_call(body, name=name, grid=(T // ROW_TILE,), out_shape=out_shape, in_specs=in_specs,
                         out_specs=out_specs, compiler_params=_cp("parallel"))(*ins)
    return out if has_res else out[0]


def _norm_bwd(x, dh, dres, g, scale, shift, name):
    def body(x_ref, dh_ref, dres_ref, g_ref, sc_ref, sh_ref, dx_ref, st_ref):
        xv, dh_v, gv = x_ref[...], dh_ref[...], g_ref[...]
        r = lax.rsqrt(_mean1(xv * xv) + EPS)
        n = xv * r
        dy = dh_v * (1.0 + sc_ref[...])
        dn = dy * gv
        dx_ref[...] = r * (dn - n * _mean1(dn * n)) + dres_ref[...]

        @pl.when(pl.program_id(0) == 0)
        def _():
            st_ref[...] = jnp.zeros_like(st_ref)

        st_ref[0:1, :] += _sum0(dy * n)
        st_ref[1:2, :] += _sum0(dh_v * (n * gv))
        st_ref[2:3, :] += _sum0(dh_v)

    return pl.pallas_call(
        body, name=name, grid=(T // ROW_TILE,),
        out_shape=[jax.ShapeDtypeStruct((T, D), F32), jax.ShapeDtypeStruct((8, D), F32)],
        in_specs=[_row_spec(), _row_spec(), _row_spec(), _vec_spec(), _vec_spec(), _vec_spec()],
        out_specs=[_row_spec(), pl.BlockSpec((8, D), lambda i: (0, 0))],
        compiler_params=_cp("arbitrary"),
    )(x, dh, dres, g, scale, shift)


def _axpy(x, m, gate, name):
    def body(x_ref, m_ref, gate_ref, o_ref):
        o_ref[...] = x_ref[...] + gate_ref[...] * m_ref[...]

    return pl.pallas_call(
        body, name=name, grid=(T // ROW_TILE,), out_shape=jax.ShapeDtypeStruct((T, D), F32),
        in_specs=[_row_spec(), _row_spec(), _vec_spec()], out_specs=_row_spec(),
        compiler_params=_cp("parallel"))(x, m, gate)


def _gate_bwd(dx, m, gate, name):
    def body(dx_ref, m_ref, gate_ref, dm_ref, st_ref):
        dxv = dx_ref[...]
        dm_ref[...] = (gate_ref[...] * dxv).astype(BF16)

        @pl.when(pl.program_id(0) == 0)
        def _():
            st_ref[...] = jnp.zeros_like(st_ref)

        st_ref[0:1, :] += _sum0(dxv * m_ref[...])

    return pl.pallas_call(
        body, name=name, grid=(T // ROW_TILE,),
        out_shape=[jax.ShapeDtypeStruct((T, D), BF16), jax.ShapeDtypeStruct((8, D), F32)],
        in_specs=[_row_spec(), _row_spec(), _vec_spec()],
        out_specs=[_row_spec(), pl.BlockSpec((8, D), lambda i: (0, 0))],
        compiler_params=_cp("arbitrary"))(dx, m, gate)


def _loss_fwd_bwd(y, target):
    def body(y_ref, t_ref, l_ref, d_ref):
        e = y_ref[...] - t_ref[...]
        d_ref[...] = e * (1.0 / D)

        @pl.when(pl.program_id(0) == 0)
        def _():
            l_ref[...] = jnp.zeros_like(l_ref)

        l_ref[...] += jnp.sum(_sum0(e * e), axis=1, keepdims=True) * (0.5 / D)

    return pl.pallas_call(
        body, name="loss", grid=(T // ROW_TILE,),
        out_shape=[jax.ShapeDtypeStruct((8, 128), F32), jax.ShapeDtypeStruct((T, D), F32)],
        in_specs=[_row_spec(), _row_spec()],
        out_specs=[pl.BlockSpec((8, 128), lambda i: (0, 0)), _row_spec()],
        compiler_params=_cp("arbitrary"))(y, target)


def _rope_tables():
    half = HD // 2
    inv_freq = 10000.0 ** (-jnp.arange(half, dtype=F32) / half)
    ang = jnp.arange(T, dtype=F32)[:, None] * inv_freq[None, :]
    cos, sin = jnp.cos(ang), jnp.sin(ang)
    return jnp.concatenate([cos, cos], axis=1), jnp.concatenate([-sin, sin], axis=1)


def _rope_fwd(p, cosf, sinf):
    def body(p_ref, c_ref, s_ref, o_ref):
        cv, sv = c_ref[...], s_ref[...]
        for j in range(2 * N_HEADS):
            xv = p_ref[:, j * HD:(j + 1) * HD]
            rot = xv * cv + pltpu.roll(xv, HD // 2, 1) * sv
            if j >= N_HEADS:
                rot = rot * (HD ** -0.5)
            o_ref[:, j * HD:(j + 1) * HD] = rot.astype(BF16)

    return pl.pallas_call(
        body, name="rope_fwd", grid=(T // ROW_TILE,),
        out_shape=jax.ShapeDtypeStruct((T, 2 * BW), BF16),
        in_specs=[_row_spec(2 * BW), _row_spec(HD), _row_spec(HD)], out_specs=_row_spec(2 * BW),
        compiler_params=_cp("parallel"))(p, cosf, sinf)


def _rope_bwd(dq, dk, cosf, sinf):
    def body(dq_ref, dk_ref, c_ref, s_ref, o_ref):
        cv, sv = c_ref[...], s_ref[...]
        for j in range(2 * N_HEADS):
            h = j % N_HEADS
            d = dq_ref[:, h * HD:(h + 1) * HD] if j < N_HEADS else dk_ref[:, h * HD:(h + 1) * HD] * (HD ** -0.5)
            o_ref[:, j * HD:(j + 1) * HD] = d * cv + pltpu.roll(d * sv, HD // 2, 1)

    return pl.pallas_call(
        body, name="rope_bwd", grid=(T // ROW_TILE,),
        out_shape=jax.ShapeDtypeStruct((T, 2 * BW), F32),
        in_specs=[_row_spec(BW), _row_spec(BW), _row_spec(HD), _row_spec(HD)], out_specs=_row_spec(2 * BW),
        compiler_params=_cp("parallel"))(dq, dk, cosf, sinf)


TQ = 256
V_RET_BLK = 8


def _ret_logg():
    lg = jnp.log1p(-jnp.exp2(-5.0 - jnp.arange(N_HEADS, dtype=F32)))
    return jnp.broadcast_to(lg[:, None, None], (N_HEADS, 1, 128))


def _block_iotas(i, kl):
    rows = lax.broadcasted_iota(jnp.int32, (TQ, kl), 0) + i * TQ
    cols = lax.broadcasted_iota(jnp.int32, (TQ, kl), 1)
    return rows, cols


def _ret_weight(lg_ref, i, kl):
    rows, cols = _block_iotas(i, kl)
    dist = jnp.abs(rows - cols).astype(F32)
    w = jnp.exp(dist * lg_ref[0][:, 0:1])
    return jnp.where((cols >> 6) <= (rows >> 6), w, 0.0)


def _per_query_block(i, fn):
    for n in range(1, T // TQ + 1):
        pl.when(i == n - 1)(functools.partial(fn, n * TQ))


def _ret_specs():
    q_spec = pl.BlockSpec((TQ, HD), lambda h, i: (i, h))
    k_spec = pl.BlockSpec((T, HD), lambda h, i: (0, N_HEADS + h))
    v_spec = pl.BlockSpec((T, HD), lambda h, i: (0, V_RET_BLK + h))
    lg_spec = pl.BlockSpec((1, 1, 128), lambda h, i: (h, 0, 0))
    return q_spec, k_spec, v_spec, lg_spec


def _ret_fwd(qk, p, logg, duty=None):
    def body(q_ref, k_ref, v_ref, lg_ref, o_ref):
        i = pl.program_id(1)

        def visible(kl):
            s = _dot(q_ref[...], k_ref[0:kl, :], NT) * _ret_weight(lg_ref, i, kl)
            o_ref[...] = _dot(s, v_ref[0:kl, :], NN)

        _per_query_block(i, visible)

    q_spec, k_spec, v_spec, lg_spec = _ret_specs()
    return _pcall(
        body, name="ret_fwd", grid=(N_HEADS, T // TQ),
        out_shape=jax.ShapeDtypeStruct((T, BW), F32),
        in_specs=[q_spec, k_spec, v_spec, lg_spec], out_specs=q_spec,
        sem=("parallel", "parallel"), args=(qk, qk, p, logg), duty=duty)


def _ret_bwd(qk, p, logg, do, duty=None):
    def body(q_ref, k_ref, v_ref, lg_ref, do_ref, dq_ref, dk_ref, dv_ref):
        i = pl.program_id(1)
        q, dov = q_ref[...], do_ref[...]

        @pl.when(i == 0)
        def _():
            dk_ref[...] = jnp.zeros_like(dk_ref)
            dv_ref[...] = jnp.zeros_like(dv_ref)

        def visible(kl):
            w = _ret_weight(lg_ref, i, kl)
            k = k_ref[0:kl, :]
            s = _dot(q, k, NT) * w
            ds = _dot(dov, v_ref[0:kl, :], NT) * w
            dk_ref[0:kl, :] += _dot(ds, q, TN)
            dv_ref[0:kl, :] += _dot(s, dov, TN)
            dq_ref[...] = _dot(ds, k, NN)

        _per_query_block(i, visible)

    q_spec, k_spec, v_spec, lg_spec = _ret_specs()
    acc_spec = pl.BlockSpec((T, HD), lambda h, i: (0, h))
    sh = jax.ShapeDtypeStruct((T, BW), F32)
    return _pcall(
        body, name="ret_bwd", grid=(N_HEADS, T // TQ),
        out_shape=[sh, sh, sh],
        in_specs=[q_spec, k_spec, v_spec, lg_spec, q_spec], out_specs=[q_spec, acc_spec, acc_spec],
        sem=("parallel", "arbitrary"), args=(qk, qk, p, logg, do), duty=duty)


def _post_norm(xv, gv, centered):
    if centered:
        xv = xv - _mean1(xv)
    r = lax.rsqrt(_mean1(xv * xv) + EPS)
    return xv * r, r


def _branch_post_fwd(raw, p, g, gate_blk, centered, name):
    def body(raw_ref, z_ref, g_ref, o_ref):
        for h in range(N_HEADS):
            sl = slice(h * HD, (h + 1) * HD)
            gv = g_ref[:, sl] if centered else g_ref[...]
            xh, _ = _post_norm(raw_ref[:, sl], gv, centered)
            z = z_ref[:, sl]
            o_ref[:, sl] = (z * _sigmoid(z) * (xh * gv)).astype(BF16)

    return pl.pallas_call(
        body, name=name, grid=(T // ROW_TILE,),
        out_shape=jax.ShapeDtypeStruct((T, BW), BF16),
        in_specs=[_row_spec(BW), _row_spec(BW, gate_blk), _vec_spec(BW if centered else HD)],
        out_specs=_row_spec(BW), compiler_params=_cp("parallel"))(raw, p, g)


def _branch_post_bwd(raw, p, g, dout, gate_blk, centered, name):
    gw = BW if centered else HD

    def body(raw_ref, z_ref, g_ref, do_ref, dr_ref, dz_ref, dg_ref):
        @pl.when(pl.program_id(0) == 0)
        def _():
            dg_ref[...] = jnp.zeros_like(dg_ref)

        for h in range(N_HEADS):
            sl = slice(h * HD, (h + 1) * HD)
            gsl = sl if centered else slice(0, HD)
            gv, z, dov = g_ref[:, gsl], z_ref[:, sl], do_ref[:, sl]
            xh, r = _post_norm(raw_ref[:, sl], gv, centered)
            sg = _sigmoid(z)
            dyn = dov * (z * sg)
            dz_ref[:, sl] = dov * (xh * gv) * (sg * (1.0 + z * (1.0 - sg)))
            dxh = dyn * gv
            t = dxh - xh * _mean1(dxh * xh)
            if centered:
                t = t - _mean1(dxh)
            dr_ref[:, sl] = r * t
            dg_ref[0:1, gsl] += _sum0(dyn * xh)

    return pl.pallas_call(
        body, name=name, grid=(T // ROW_TILE,),
        out_shape=[jax.ShapeDtypeStruct((T, BW), F32), jax.ShapeDtypeStruct((T, BW), F32),
                   jax.ShapeDtypeStruct((8, gw), F32)],
        in_specs=[_row_spec(BW), _row_spec(BW, gate_blk), _vec_spec(gw), _row_spec(BW)],
        out_specs=[_row_spec(BW), _row_spec(BW), pl.BlockSpec((8, gw), lambda i: (0, 0))],
        compiler_params=_cp("arbitrary"))(raw, p, g, dout)


GLA_ROWS = 256
GLA_CPB = GLA_ROWS // CHUNK
GLA_DK = 64
GLA_W = N_HEADS * GLA_DK
GQ_BLK, GK_BLK, GV_BLK, GG_BLK, TAIL_BLK = 8, 9, 5, 6, 40
RG_BLK = 3


def _gla_chunk_common(tl, w2, bv, kv):
    pre = _dot(tl, w2, NN) + bv
    la = _log_sigmoid(pre) * (1.0 / 16.0)
    bc = _exact_dot(_tri(CHUNK, False), la)
    be = bc[CHUNK - 1:CHUNK, :]
    w = jnp.exp(be - bc)
    return pre, w, jnp.exp(be), kv * w


def _head_masks():
    lane = lax.broadcasted_iota(jnp.int32, (1, GLA_W), 1)
    return [jnp.where((lane // GLA_DK) == h, 1.0, 0.0) for h in range(N_HEADS)]


def _gla_fwd(p, w2pad, b, duty=None):
    nb = T // GLA_ROWS

    def body(q_ref, k_ref, v_ref, t_ref, w2_ref, b_ref, o_ref, st_ref, s_acc):
        @pl.when(pl.program_id(0) == 0)
        def _():
            s_acc[...] = jnp.zeros_like(s_acc)

        masks = _head_masks()
        for c in range(GLA_CPB):
            rows = slice(c * CHUNK, (c + 1) * CHUNK)
            _, _, a, kd = _gla_chunk_common(t_ref[rows, :], w2_ref[...], b_ref[...], k_ref[rows, :])
            q = q_ref[rows, :] * (GLA_DK ** -0.5)
            kv = None
            for h in range(N_HEADS):
                t = _dot(v_ref[rows, h * HD:(h + 1) * HD], kd * masks[h], TN)
                kv = t if kv is None else kv + t
            s_new = s_acc[...] * a + kv
            s_acc[...] = s_new
            st_ref[c] = s_new
            for h in range(N_HEADS):
                o_ref[rows, h * HD:(h + 1) * HD] = _dot(q * masks[h], s_new, NT)

    return _pcall(
        body, name="gla_fwd", grid=(nb,),
        out_shape=[jax.ShapeDtypeStruct((T, BW), F32), jax.ShapeDtypeStruct((T // CHUNK, HD, GLA_W), F32)],
        in_specs=[pl.BlockSpec((GLA_ROWS, GLA_W), lambda i: (i, GQ_BLK)),
                  pl.BlockSpec((GLA_ROWS, GLA_W), lambda i: (i, GK_BLK)),
                  pl.BlockSpec((GLA_ROWS, BW), lambda i: (i, GV_BLK)),
                  pl.BlockSpec((GLA_ROWS, 128), lambda i: (i, TAIL_BLK)),
                  pl.BlockSpec((128, GLA_W), lambda i: (0, 0)),
                  pl.BlockSpec((1, GLA_W), lambda i: (0, 0))],
        out_specs=[pl.BlockSpec((GLA_ROWS, BW), lambda i: (i, 0)),
                   pl.BlockSpec((GLA_CPB, HD, GLA_W), lambda i: (i, 0, 0))],
        scratch_shapes=[pltpu.VMEM((HD, GLA_W), F32)],
        sem=("arbitrary",), args=(p, p, p, p, w2pad, b), duty=duty)


def _gla_bwd(p, w2pad, b, states, do, duty=None):
    nb = T // GLA_ROWS

    def body(q_ref, k_ref, v_ref, t_ref, w2_ref, b_ref, st_ref, prev_ref, do_ref,
             dq_ref, dk_ref, dv_ref, dt_ref, dw2_ref, db_ref, ds_acc):
        step = pl.program_id(0)

        @pl.when(step == 0)
        def _():
            ds_acc[...] = jnp.zeros_like(ds_acc)
            dw2_ref[...] = jnp.zeros_like(dw2_ref)
            db_ref[...] = jnp.zeros_like(db_ref)

        masks = _head_masks()
        up = _tri(CHUNK, True)
        has_prev = jnp.where(step == nb - 1, 0.0, 1.0)
        for c in reversed(range(GLA_CPB)):
            rows = slice(c * CHUNK, (c + 1) * CHUNK)
            tl, w2, k = t_ref[rows, :], w2_ref[...], k_ref[rows, :]
            pre, w, a, kd = _gla_chunk_common(tl, w2, b_ref[...], k)
            q = q_ref[rows, :] * (GLA_DK ** -0.5)
            s_n = st_ref[c]
            s_prev = st_ref[c - 1] if c > 0 else prev_ref[0] * has_prev
            ds = ds_acc[...]
            dos = [do_ref[rows, h * HD:(h + 1) * HD] for h in range(N_HEADS)]
            for h in range(N_HEADS):
                ds = ds + _dot(dos[h], q * masks[h], TN)
            dqp = jnp.zeros((CHUNK, GLA_W), F32)
            dkd = jnp.zeros((CHUNK, GLA_W), F32)
            for h in range(N_HEADS):
                dqp = dqp + masks[h] * _dot(dos[h], s_n, NN)
                dkd = dkd + masks[h] * _dot(v_ref[rows, h * HD:(h + 1) * HD], ds, NN)
                dv_ref[rows, h * HD:(h + 1) * HD] = _dot(kd * masks[h], ds, NT)
            dq_ref[rows, :] = dqp * (GLA_DK ** -0.5)
            dk_ref[rows, :] = dkd * w
            e = dkd * k * w
            dbe = _sum0(e) + _sum0(ds * s_prev) * a
            dla = dbe - _exact_dot(up, e)
            dpre = dla * (1.0 / 16.0) * _sigmoid(-pre)
            db_ref[0:1, :] += _sum0(dpre)
            dw2_ref[...] += _dot(tl, dpre, TN)
            dt_ref[rows, :] = _dot(dpre, w2, NT)
            ds_acc[...] = ds * a

    rev = lambda i: nb - 1 - i
    sh = lambda w: jax.ShapeDtypeStruct((T, w), F32)
    return _pcall(
        body, name="gla_bwd", grid=(nb,),
        out_shape=[sh(GLA_W), sh(GLA_W), sh(BW), sh(128), jax.ShapeDtypeStruct((128, GLA_W), F32),
                   jax.ShapeDtypeStruct((8, GLA_W), F32)],
        in_specs=[pl.BlockSpec((GLA_ROWS, GLA_W), lambda i: (rev(i), GQ_BLK)),
                  pl.BlockSpec((GLA_ROWS, GLA_W), lambda i: (rev(i), GK_BLK)),
                  pl.BlockSpec((GLA_ROWS, BW), lambda i: (rev(i), GV_BLK)),
                  pl.BlockSpec((GLA_ROWS, 128), lambda i: (rev(i), TAIL_BLK)),
                  pl.BlockSpec((128, GLA_W), lambda i: (0, 0)),
                  pl.BlockSpec((1, GLA_W), lambda i: (0, 0)),
                  pl.BlockSpec((GLA_CPB, HD, GLA_W), lambda i: (rev(i), 0, 0)),
                  pl.BlockSpec((1, HD, GLA_W), lambda i: (jnp.maximum(rev(i) * GLA_CPB - 1, 0), 0, 0)),
                  pl.BlockSpec((GLA_ROWS, BW), lambda i: (rev(i), 0))],
        out_specs=[pl.BlockSpec((GLA_ROWS, GLA_W), lambda i: (rev(i), 0)),
                   pl.BlockSpec((GLA_ROWS, GLA_W), lambda i: (rev(i), 0)),
                   pl.BlockSpec((GLA_ROWS, BW), lambda i: (rev(i), 0)),
                   pl.BlockSpec((GLA_ROWS, 128), lambda i: (rev(i), 0)),
                   pl.BlockSpec((128, GLA_W), lambda i: (0, 0)),
                   pl.BlockSpec((8, GLA_W), lambda i: (0, 0))],
        scratch_shapes=[pltpu.VMEM((HD, GLA_W), F32)],
        sem=("arbitrary",), args=(p, p, p, p, w2pad, b, states, states, do), duty=duty)


FQ_BLK, FK_BLK = 7, 8
V_FOX_BLK = 36


def _fox_prep_fwd(p, qg, kg, btail):
    def body(q_ref, k_ref, t_ref, qg_ref, kg_ref, bt_ref, o_ref, cum_ref, carry):
        @pl.when(pl.program_id(0) == 0)
        def _():
            carry[...] = jnp.zeros_like(carry)

        for src, gr, off in ((q_ref, qg_ref, 0), (k_ref, kg_ref, BW)):
            for h in range(N_HEADS):
                xv = src[:, h * HD:(h + 1) * HD]
                r = lax.rsqrt(_mean1(xv * xv) + EPS)
                o_ref[:, off + h * HD:off + (h + 1) * HD] = (xv * r * gr[...]).astype(BF16)
        logf = _log_sigmoid(t_ref[...] + bt_ref[...])
        cum = _exact_dot(_tri(ROW_TILE, False), logf) + carry[...]
        cum_ref[...] = cum
        carry[...] = cum[ROW_TILE - 1:ROW_TILE, :]

    return pl.pallas_call(
        body, name="fox_prep_fwd", grid=(T // ROW_TILE,),
        out_shape=[jax.ShapeDtypeStruct((T, 2 * BW), BF16), jax.ShapeDtypeStruct((T, 128), F32)],
        in_specs=[_row_spec(BW, FQ_BLK), _row_spec(BW, FK_BLK), _row_spec(128, TAIL_BLK),
                  _vec_spec(HD), _vec_spec(HD), _vec_spec(128)],
        out_specs=[_row_spec(2 * BW), _row_spec(128)],
        scratch_shapes=[pltpu.VMEM((1, 128), F32)],
        compiler_params=_cp("arbitrary"))(p, p, p, qg, kg, btail)


def _fox_prep_bwd(p, qg, kg, btail, dqn, dkn, dcum):
    nt = T // ROW_TILE

    def body(q_ref, k_ref, t_ref, qg_ref, kg_ref, bt_ref, dq_ref, dk_ref, dc_ref, o_ref, dt_ref, st_ref, carry):
        @pl.when(pl.program_id(0) == 0)
        def _():
            carry[...] = jnp.zeros_like(carry)
            st_ref[...] = jnp.zeros_like(st_ref)

        for row, (src, gr, dsrc, off) in enumerate(((q_ref, qg_ref, dq_ref, 0), (k_ref, kg_ref, dk_ref, BW))):
            for h in range(N_HEADS):
                xv = src[:, h * HD:(h + 1) * HD]
                dy = dsrc[:, h * HD:(h + 1) * HD]
                r = lax.rsqrt(_mean1(xv * xv) + EPS)
                n = xv * r
                dn = dy * gr[...]
                o_ref[:, off + h * HD:off + (h + 1) * HD] = r * (dn - n * _mean1(dn * n))
                st_ref[row:row + 1, :] += _sum0(dy * n)
        z = t_ref[...] + bt_ref[...]
        dlogf = _exact_dot(_tri(ROW_TILE, True), dc_ref[...]) + carry[...]
        carry[...] = dlogf[0:1, :]
        lane = lax.broadcasted_iota(jnp.int32, (1, 128), 1)
        keep = (lane >= FF_LANE0) & (lane < FF_LANE0 + N_HEADS)
        dz = jnp.where(keep, dlogf * _sigmoid(-z), 0.0)
        dt_ref[...] = dz
        st_ref[2:3, :] += _sum0(dz)

    rs = lambda w, col=0: pl.BlockSpec((ROW_TILE, w), lambda i: (nt - 1 - i, col))
    return pl.pallas_call(
        body, name="fox_prep_bwd", grid=(nt,),
        out_shape=[jax.ShapeDtypeStruct((T, 2 * BW), F32), jax.ShapeDtypeStruct((T, 128), F32),
                   jax.ShapeDtypeStruct((8, 128), F32)],
        in_specs=[rs(BW, FQ_BLK), rs(BW, FK_BLK), rs(128, TAIL_BLK), _vec_spec(HD), _vec_spec(HD), _vec_spec(128),
                  rs(BW), rs(BW), rs(128)],
        out_specs=[rs(2 * BW), rs(128), pl.BlockSpec((8, 128), lambda i: (0, 0))],
        scratch_shapes=[pltpu.VMEM((1, 128), F32)],
        compiler_params=_cp("arbitrary"))(p, p, p, qg, kg, btail, dqn, dkn, dcum)


def _fox_logits(q_ref, k_ref, cc_ref, cr_ref, i, kl):
    rows, cols = _block_iotas(i, kl)
    s = _dot(q_ref[...], k_ref[0:kl, :], NT) * (HD ** -0.5) + cc_ref[0] - cr_ref[0, :, 0:kl]
    return jnp.where(cols <= rows, s, -1e30)


def _fox_specs():
    q_spec = pl.BlockSpec((TQ, HD), lambda h, i: (i, h))
    k_spec = pl.BlockSpec((T, HD), lambda h, i: (0, N_HEADS + h))
    v_spec = pl.BlockSpec((T, HD), lambda h, i: (0, V_FOX_BLK + h))
    col_spec = pl.BlockSpec((1, TQ, 1), lambda h, i: (h, i, 0))
    row_spec = pl.BlockSpec((1, 1, T), lambda h, i: (h, 0, 0))
    return q_spec, k_spec, v_spec, col_spec, row_spec


def _fox_fwd(qkn, p, cumcol, cumrow, duty=None):
    def body(q_ref, k_ref, v_ref, cc_ref, cr_ref, o_ref, lse_ref):
        i = pl.program_id(1)

        def visible(kl):
            s = _fox_logits(q_ref, k_ref, cc_ref, cr_ref, i, kl)
            m = jnp.max(s, axis=-1, keepdims=True)
            e = jnp.exp(s - m)
            l = jnp.sum(e, axis=-1, keepdims=True)
            o_ref[...] = _dot(e / l, v_ref[0:kl, :], NN)
            lse_ref[0] = m + jnp.log(l)

        _per_query_block(i, visible)

    q_spec, k_spec, v_spec, col_spec, row_spec = _fox_specs()
    return _pcall(
        body, name="fox_fwd", grid=(N_HEADS, T // TQ),
        out_shape=[jax.ShapeDtypeStruct((T, BW), F32), jax.ShapeDtypeStruct((N_HEADS, T, 1), F32)],
        in_specs=[q_spec, k_spec, v_spec, col_spec, row_spec], out_specs=[q_spec, col_spec],
        sem=("parallel", "parallel"), args=(qkn, qkn, p, cumcol, cumrow), duty=duty)


def _fox_bwd(qkn, p, cumcol, cumrow, lse, o, do, duty=None):
    def body(q_ref, k_ref, v_ref, cc_ref, cr_ref, lse_ref, o_ref, do_ref, dq_ref, dk_ref, dv_ref, dr_ref, dc_ref):
        i = pl.program_id(1)
        @pl.when(i == 0)
        def _():
            dk_ref[...] = jnp.zeros_like(dk_ref)
            dv_ref[...] = jnp.zeros_like(dv_ref)
            dc_ref[...] = jnp.zeros_like(dc_ref)

        def visible(kl):
            q, dov = q_ref[...], do_ref[...]
            pm = jnp.exp(_fox_logits(q_ref, k_ref, cc_ref, cr_ref, i, kl) - lse_ref[0])
            delta = jnp.sum(o_ref[...] * dov, axis=-1, keepdims=True)
            ds = pm * (_dot(dov, v_ref[0:kl, :], NT) - delta)
            dq_ref[...] = _dot(ds, k_ref[0:kl, :], NN) * (HD ** -0.5)
            dr_ref[0] = jnp.sum(ds, axis=-1, keepdims=True)
            dk_ref[0:kl, :] += _dot(ds, q, TN) * (HD ** -0.5)
            dv_ref[0:kl, :] += _dot(pm, dov, TN)
            dc_ref[0, :, 0:kl] += _sum0(ds)

        _per_query_block(i, visible)

    q_spec, k_spec, v_spec, col_spec, row_spec = _fox_specs()
    acc_spec = pl.BlockSpec((T, HD), lambda h, i: (0, h))
    sh = jax.ShapeDtypeStruct((T, BW), F32)
    return _pcall(
        body, name="fox_bwd", grid=(N_HEADS, T // TQ),
        out_shape=[sh, sh, sh, jax.ShapeDtypeStruct((N_HEADS, T, 1), F32), jax.ShapeDtypeStruct((N_HEADS, 1, T), F32)],
        in_specs=[q_spec, k_spec, v_spec, col_spec, row_spec, col_spec, q_spec, q_spec],
        out_specs=[q_spec, acc_spec, acc_spec, col_spec, row_spec],
        sem=("parallel", "arbitrary"), args=(qkn, qkn, p, cumcol, cumrow, lse, o, do), duty=duty)


def _mix_fwd(gpre, b_mg, y0, y1, y2, duty=None):
    def body(g_ref, b_ref, y0_ref, y1_ref, y2_ref, o_ref):
        acc = None
        for n, y_ref in enumerate((y0_ref, y1_ref, y2_ref)):
            sl = slice(n * D, (n + 1) * D)
            t = _sigmoid(g_ref[:, sl] + b_ref[:, sl]) * y_ref[...]
            acc = t if acc is None else acc + t
        o_ref[...] = acc.astype(BF16)

    return _pcall(
        body, name="mix_fwd", grid=(T // ROW_TILE,), out_shape=jax.ShapeDtypeStruct((T, D), BF16),
        in_specs=[_row_spec(3 * D), _vec_spec(3 * D), _row_spec(), _row_spec(), _row_spec()],
        out_specs=_row_spec(), sem=("parallel",), args=(gpre, b_mg, y0, y1, y2), duty=duty)


def _mix_bwd(gpre, b_mg, y0, y1, y2, dmi, duty=None):
    def body(g_ref, b_ref, y0_ref, y1_ref, y2_ref, d_ref, dy0_ref, dy1_ref, dy2_ref, dg_ref, db_ref):
        @pl.when(pl.program_id(0) == 0)
        def _():
            db_ref[...] = jnp.zeros_like(db_ref)

        dv = d_ref[...]
        for n, (y_ref, dy_ref) in enumerate(((y0_ref, dy0_ref), (y1_ref, dy1_ref), (y2_ref, dy2_ref))):
            sl = slice(n * D, (n + 1) * D)
            sg = _sigmoid(g_ref[:, sl] + b_ref[:, sl])
            dy_ref[...] = (dv * sg).astype(BF16)
            dpre = dv * y_ref[...] * (sg * (1.0 - sg))
            dg_ref[:, sl] = dpre.astype(BF16)
            db_ref[0:1, sl] += _sum0(dpre)

    shb = jax.ShapeDtypeStruct((T, D), BF16)
    return _pcall(
        body, name="mix_bwd", grid=(T // ROW_TILE,),
        out_shape=[shb, shb, shb, jax.ShapeDtypeStruct((T, 3 * D), BF16), jax.ShapeDtypeStruct((8, 3 * D), F32)],
        in_specs=[_row_spec(3 * D), _vec_spec(3 * D), _row_spec(), _row_spec(), _row_spec(), _row_spec()],
        out_specs=[_row_spec(), _row_spec(), _row_spec(), _row_spec(3 * D), pl.BlockSpec((8, 3 * D), lambda i: (0, 0))],
        sem=("arbitrary",), args=(gpre, b_mg, y0, y1, y2, dmi), duty=duty)


FF_COLS = 256
FF_NBLK = D_FF // FF_COLS


def _shift_rows(a, n):
    rows = lax.broadcasted_iota(jnp.int32, a.shape, 0)
    rolled = pltpu.roll(a, n % T, 0)
    return jnp.where((rows >= n) if n > 0 else (rows < T + n), rolled, 0.0)


def _ffn_act_fwd(uu, w_conv, b_conv, duty=None):
    def body(u_ref, g_ref, w_ref, b_ref, o_ref):
        u = u_ref[...]
        w = w_ref[...]
        uc = b_ref[...] + w[0:1, :] * _shift_rows(u, 2) + w[1:2, :] * _shift_rows(u, 1) + w[2:3, :] * u
        o_ref[...] = (uc * _sigmoid(uc) * g_ref[...]).astype(BF16)

    return _pcall(
        body, name="ffn_act_fwd", grid=(FF_NBLK,), out_shape=jax.ShapeDtypeStruct((T, D_FF), BF16),
        in_specs=[pl.BlockSpec((T, FF_COLS), lambda j: (0, j)), pl.BlockSpec((T, FF_COLS), lambda j: (0, FF_NBLK + j)),
                  pl.BlockSpec((3, FF_COLS), lambda j: (0, j)), pl.BlockSpec((1, FF_COLS), lambda j: (0, j))],
        out_specs=pl.BlockSpec((T, FF_COLS), lambda j: (0, j)),
        sem=("parallel",), args=(uu, uu, w_conv, b_conv), duty=duty)


def _ffn_act_bwd(uu, w_conv, b_conv, da, duty=None):
    def body(u_ref, g_ref, w_ref, b_ref, da_ref, d_ref, st_ref):
        u, w, dav = u_ref[...], w_ref[...], da_ref[...]
        u1, u2 = _shift_rows(u, 1), _shift_rows(u, 2)
        uc = b_ref[...] + w[0:1, :] * u2 + w[1:2, :] * u1 + w[2:3, :] * u
        sg = _sigmoid(uc)
        d_ref[1] = (dav * (uc * sg)).astype(BF16)
        duc = dav * g_ref[...] * (sg * (1.0 + uc * (1.0 - sg)))
        du = w[2:3, :] * duc + w[1:2, :] * _shift_rows(duc, -1) + w[0:1, :] * _shift_rows(duc, -2)
        d_ref[0] = du.astype(BF16)
        st_ref[...] = jnp.zeros_like(st_ref)
        st_ref[0:1, :] = _sum0(duc * u2)
        st_ref[1:2, :] = _sum0(duc * u1)
        st_ref[2:3, :] = _sum0(duc * u)
        st_ref[3:4, :] = _sum0(duc)

    cb = lambda rows=T, off=0: pl.BlockSpec((rows, FF_COLS), lambda j: (0, off + j))
    return _pcall(
        body, name="ffn_act_bwd", grid=(FF_NBLK,),
        out_shape=[jax.ShapeDtypeStruct((2, T, D_FF), BF16), jax.ShapeDtypeStruct((8, D_FF), F32)],
        in_specs=[cb(), cb(T, FF_NBLK), cb(3), cb(1), cb()],
        out_specs=[pl.BlockSpec((2, T, FF_COLS), lambda j: (0, 0, j)), cb(8)],
        sem=("parallel",), args=(uu, uu, w_conv, b_conv, da), duty=duty)


def _adamw(g, w, m, v, tr, name):
    partial = g.ndim == 3
    R, C = w.shape
    tr = R if tr is None else tr
    assert R % tr == 0

    def body(g_ref, w_ref, m_ref, v_ref, go_ref, d_ref, mo_ref, vo_ref):
        if partial:
            gv = g_ref[0].astype(F32)
            for j in range(1, N_DEV):
                gv = gv + g_ref[j].astype(F32)
        else:
            gv = g_ref[...]
        go_ref[...] = gv
        mn = ADAM_B1 * m_ref[...] + (1.0 - ADAM_B1) * gv
        vn = ADAM_B2 * v_ref[...] + (1.0 - ADAM_B2) * (gv * gv)
        mo_ref[...] = mn
        vo_ref[...] = vn
        m_hat = mn / (1.0 - ADAM_B1 ** ADAM_STEP)
        v_hat = vn / (1.0 - ADAM_B2 ** ADAM_STEP)
        d_ref[...] = -ADAM_LR * (m_hat / (jnp.sqrt(v_hat) + ADAM_EPS) + ADAM_WD * w_ref[...])

    spec = pl.BlockSpec((tr, C), lambda i: (i, 0))
    g_spec = pl.BlockSpec((N_DEV, tr, C), lambda i: (0, i, 0)) if partial else spec
    sh = jax.ShapeDtypeStruct((R, C), F32)
    return pl.pallas_call(
        body, name=name, grid=(R // tr,), out_shape=[sh, sh, sh, sh],
        in_specs=[g_spec, spec, spec, spec], out_specs=[spec, spec, spec, spec],
        compiler_params=_cp("parallel"))(g, w, m, v)


def _sum_partials(g, name, tr=None):
    _, R, C = g.shape
    tr = R if tr is None else tr
    assert R % tr == 0

    def body(g_ref, o_ref):
        acc = g_ref[0].astype(F32)
        for j in range(1, N_DEV):
            acc = acc + g_ref[j].astype(F32)
        o_ref[...] = acc

    return pl.pallas_call(
        body, name=name, grid=(R // tr,), out_shape=jax.ShapeDtypeStruct((R, C), F32),
        in_specs=[pl.BlockSpec((N_DEV, tr, C), lambda i: (0, i, 0))], out_specs=pl.BlockSpec((tr, C), lambda i: (i, 0)),
        compiler_params=_cp("parallel"))(g)


def _permute_in(w):
    pad = jnp.zeros(w.shape[:-1] + (NP - IN_W,), w.dtype)
    return jnp.concatenate([w[..., :3072], w[..., 3088:5136], w[..., 3072:3088], w[..., 5136:5140], pad], axis=-1)


def _unpermute_in(w):
    return jnp.concatenate([w[..., :3072], w[..., 5120:5136], w[..., 3072:5120], w[..., 5136:5140]], axis=-1)


def _flat_pack(arrs):
    flat = jnp.concatenate([a.reshape(-1).astype(F32) for a in arrs])
    n = flat.shape[0]
    rows = -(-n // 1024) * 8
    return jnp.pad(flat, (0, rows * 128 - n)).reshape(rows, 128)


def _flat_unpack(buf, shapes):
    flat = buf.reshape(-1)
    out, off = [], 0
    for s in shapes:
        n = int(np.prod(s))
        out.append(flat[off:off + n].reshape(s))
        off += n
    return out


GRAD_CHUNKS = dict(w_in=(128, 4), w_o=(128, 1), w_down=(352, 2), w_br0=(128, 1), w_br1=(128, 1), w_br2=(128, 1),
                   w_mg=(384, 2), w_up=(704, 4))


def _weight(wl, k):
    return wl[k]() if callable(wl[k]) else wl[k]


def _taker(xfer):
    return (lambda count: None) if xfer is None else xfer.take


def _layer_fwd(x0, wl, consts, xfer=None):
    cosf, sinf, logg = consts
    row = lambda a: a.reshape(1, -1)
    take_units = _taker(xfer)
    take = lambda count: take_units(2 * count)
    h = _norm_fwd(x0, row(wl["norm1_g"]), row(wl["scale1"]), row(wl["shift1"]), "norm1_fwd")
    p = _matmul(h, _weight(wl, "w_in"), "nn", "in_proj", duty=take(2))
    qk = _rope_fwd(p, cosf, sinf)
    ret_raw = _ret_fwd(qk, p, logg, duty=take(2))
    br0 = _branch_post_fwd(ret_raw, p, row(wl["ret_norm_g"]), RG_BLK, True, "ret_post_fwd")
    gla_raw, states = _gla_fwd(p, wl["w2pad"], row(wl["b_gla_a"]), duty=take(1))
    br1 = _branch_post_fwd(gla_raw, p, row(wl["gla_norm_g"]), GG_BLK, False, "gla_post_fwd")
    qkn, cum = _fox_prep_fwd(p, row(wl["q_norm_g"]), row(wl["k_norm_g"]), row(wl["btail"]))
    cum4 = cum[:, FF_LANE0:FF_LANE0 + N_HEADS].T
    cumcol, cumrow = cum4.reshape(N_HEADS, T, 1), cum4.reshape(N_HEADS, 1, T)
    fox_o, lse = _fox_fwd(qkn, p, cumcol, cumrow, duty=take(2))
    w_br_t = _weight(wl, "w_br_t")
    ys = [_matmul(b, w_br_t[n], "nt", "br_proj%d" % n) for n, b in enumerate((br0, br1, fox_o))]
    gpre = _matmul(h, _weight(wl, "w_mg_t"), "nt", "gate_proj", duty=take(1))
    mixed_in = _mix_fwd(gpre, row(wl["b_mg"]), *ys, duty=take(1))
    mixed = _matmul(mixed_in, _weight(wl, "w_o"), "nn", "o_proj")
    x1, h2 = _norm_fwd(x0, row(wl["norm2_g"]), row(wl["scale2"]), row(wl["shift2"]), "norm2_fwd",
                       m=mixed, gate=row(wl["gate1"]))
    uu = _matmul(h2, _weight(wl, "w_up_t"), "nt", "up_proj", duty=take(2))
    act = _ffn_act_fwd(uu, wl["w_conv"], row(wl["b_conv"]), duty=take(1))
    y = _matmul(act, _weight(wl, "w_down"), "nn", "down_proj", tk=1408, duty=take(2))
    x2 = _axpy(x1, y, row(wl["gate2"]), "resid2")
    saved = dict(x0=x0, h=h, p=p, qk=qk, ret_raw=ret_raw, br0=br0, gla_raw=gla_raw, states=states, br1=br1,
                 qkn=qkn, cumcol=cumcol, cumrow=cumrow, fox_o=fox_o, lse=lse, y0=ys[0], y1=ys[1], y2=ys[2],
                 gpre=gpre, mixed_in=mixed_in, mixed=mixed, x1=x1, h2=h2, uu=uu, act=act, y=y)
    return x2, saved


def _layer_bwd(dx2, wl, sv, consts, xfer=None, layer=0):
    cosf, sinf, logg = consts
    row = lambda a: a.reshape(1, -1)
    take = _taker(xfer)

    def send(k, g):
        if xfer is not None:
            n, parts = GRAD_CHUNKS[k]
            for c in range(parts):
                xfer.add(k, ("a2a", 0, n, (layer,), c * (n // parts), n // parts), g)
        return g

    dy, st_g2 = _gate_bwd(dx2, sv["y"], row(wl["gate2"]), "gate2_bwd")
    dact = _matmul(dy, _weight(wl, "w_down"), "nt", "down_dx", tn=1408, duty=take(1))
    d_down = send("w_down", _matmul(sv["act"], dy, "tn", "down_dw", out_dtype=BF16, tm=1408, duty=take(1)))
    duu, st_conv = _ffn_act_bwd(sv["uu"], wl["w_conv"], row(wl["b_conv"]), dact, duty=take(1))
    dh2 = _matmul(duu, _weight(wl, "w_up_t"), "nn", "up_dx", tk=1408, duty=take(2))
    d_up_t = send("w_up", _matmul(duu, sv["h2"], "tn", "up_dw", out_dtype=BF16, tm=1408, duty=take(1)))
    dx1, st_n2 = _norm_bwd(sv["x1"], dh2, dx2, row(wl["norm2_g"]), row(wl["scale2"]), row(wl["shift2"]), "norm2_bwd")
    dmixed, st_g1 = _gate_bwd(dx1, sv["mixed"], row(wl["gate1"]), "gate1_bwd")
    dmi = _matmul(dmixed, _weight(wl, "w_o"), "nt", "o_dx")
    d_o = send("w_o", _matmul(sv["mixed_in"], dmixed, "tn", "o_dw", out_dtype=BF16))
    dy0, dy1, dy2, dgpre, st_bmg = _mix_bwd(sv["gpre"], row(wl["b_mg"]), sv["y0"], sv["y1"], sv["y2"], dmi, duty=take(1))
    brs = (sv["br0"], sv["br1"], sv["fox_o"])
    w_br_t = _weight(wl, "w_br_t")
    dbr = [_matmul(d, w_br_t[n], "nn", "br_dx%d" % n) for n, d in enumerate((dy0, dy1, dy2))]
    d_br_t = [send("w_br%d" % n, _matmul(d, brs[n], "tn", "br_dw%d" % n, out_dtype=BF16))
              for n, d in enumerate((dy0, dy1, dy2))]
    dh = _matmul(dgpre, _weight(wl, "w_mg_t"), "nn", "gate_dx", tk=1024, duty=take(1))
    d_mg_t = send("w_mg", _matmul(dgpre, sv["h"], "tn", "gate_dw", out_dtype=BF16, duty=take(1)))
    p = sv["p"]
    dqn, dkn, dfv, drow, dcol = _fox_bwd(sv["qkn"], p, sv["cumcol"], sv["cumrow"], sv["lse"], sv["fox_o"], dbr[2],
                                         duty=take(2))
    dcum4 = drow.reshape(N_HEADS, T) - dcol.reshape(N_HEADS, T)
    dcum = jnp.pad(dcum4.T, ((0, 0), (FF_LANE0, 128 - FF_LANE0 - N_HEADS)))
    dfqk, dtail_fox, st_fox = _fox_prep_bwd(p, row(wl["q_norm_g"]), row(wl["k_norm_g"]), row(wl["btail"]), dqn, dkn, dcum)
    dgla_raw, dgg, st_gn = _branch_post_bwd(sv["gla_raw"], p, row(wl["gla_norm_g"]), dbr[1], GG_BLK, False, "gla_post_bwd")
    dgq, dgk, dgv, dtail_gla, dw2pad, st_bg = _gla_bwd(p, wl["w2pad"], row(wl["b_gla_a"]), sv["states"], dgla_raw,
                                                       duty=take(1))
    dret_raw, drg, st_rn = _branch_post_bwd(sv["ret_raw"], p, row(wl["ret_norm_g"]), dbr[0], RG_BLK, True, "ret_post_bwd")
    dqr, dkr, drv = _ret_bwd(sv["qk"], p, logg, dret_raw, duty=take(2))
    drqk = _rope_bwd(dqr, dkr, cosf, sinf)
    dp = jnp.concatenate([a.astype(BF16) for a in (drqk, drv, drg, dgq, dgk, dgv, dgg, dfqk, dfv, dtail_fox + dtail_gla)]
                         + [jnp.zeros((T, NP - TAIL0 - 128), BF16)], axis=1)
    dh = _matmul(dp, _weight(wl, "w_in"), "nt", "in_dx", tk=1408, add=dh, duty=take(2))
    d_in = send("w_in", _matmul(sv["h"], dp, "tn", "in_dw", out_dtype=BF16, duty=take(1)))
    dx0, st_n1 = _norm_bwd(sv["x0"], dh, dx1, row(wl["norm1_g"]), row(wl["scale1"]), row(wl["shift1"]), "norm1_bwd")
    big = dict(w_in=d_in, w_o=d_o, w_down=d_down, w_br0=d_br_t[0], w_br1=d_br_t[1], w_br2=d_br_t[2], w_mg=d_mg_t,
               w_up=d_up_t)
    dmod = jnp.concatenate([st_n1[2], st_n1[1], st_g1[0], st_n2[2], st_n2[1], st_g2[0]])
    small = dict(norm1_g=st_n1[0], norm2_g=st_n2[0], b_gla_a=st_bg[0], b_fox_f=st_fox[2, FF_LANE0:FF_LANE0 + N_HEADS],
                 ret_norm_g=st_rn[0], gla_norm_g=st_gn[0], q_norm_g=st_fox[0], k_norm_g=st_fox[1], b_mg=st_bmg[0],
                 b_conv=st_conv[3], w_gla_a2=dw2pad[:LR_LANES], w_conv=st_conv[0:3])
    return dx0, big, dmod, small


SMALL_REPL = ("norm1_g", "norm2_g", "b_ada", "b_gla_a", "b_fox_f", "ret_norm_g", "gla_norm_g", "q_norm_g", "k_norm_g",
              "b_mg", "b_conv")
SMALL_SHARDED = ("w_gla_a2", "w_conv")
BIG = ("w_in", "w_o", "w_down", "w_br", "w_mg", "w_up")
WEIGHTS = ("norm1_g", "norm2_g", "w_ada", "b_ada", "w_in", "w_gla_a2", "b_gla_a", "b_fox_f", "ret_norm_g", "gla_norm_g",
           "q_norm_g", "k_norm_g", "w_br", "w_mg", "b_mg", "w_o", "w_up", "w_conv", "b_conv", "w_down")


def kernel(x, c, norm1_g, norm2_g, w_ada, b_ada, w_in, w_gla_a2, b_gla_a, b_fox_f, ret_norm_g, gla_norm_g, q_norm_g, k_norm_g, w_br, w_mg, b_mg, w_o, w_up, w_conv, b_conv, w_down, loss_target, m_norm1_g, m_norm2_g, m_w_ada, m_b_ada, m_w_in, m_w_gla_a2, m_b_gla_a, m_b_fox_f, m_ret_norm_g, m_gla_norm_g, m_q_norm_g, m_k_norm_g, m_w_br, m_w_mg, m_b_mg, m_w_o, m_w_up, m_w_conv, m_b_conv, m_w_down, v_norm1_g, v_norm2_g, v_w_ada, v_b_ada, v_w_in, v_w_gla_a2, v_b_gla_a, v_b_fox_f, v_ret_norm_g, v_gla_norm_g, v_q_norm_g, v_k_norm_g, v_w_br, v_w_mg, v_b_mg, v_w_o, v_w_up, v_w_conv, v_b_conv, v_w_down):
    W = dict(norm1_g=norm1_g, norm2_g=norm2_g, w_ada=w_ada, b_ada=b_ada, w_in=w_in, w_gla_a2=w_gla_a2, b_gla_a=b_gla_a,
             b_fox_f=b_fox_f, ret_norm_g=ret_norm_g, gla_norm_g=gla_norm_g, q_norm_g=q_norm_g, k_norm_g=k_norm_g,
             w_br=w_br, w_mg=w_mg, b_mg=b_mg, w_o=w_o, w_up=w_up, w_conv=w_conv, b_conv=b_conv, w_down=w_down)
    M = dict(norm1_g=m_norm1_g, norm2_g=m_norm2_g, w_ada=m_w_ada, b_ada=m_b_ada, w_in=m_w_in, w_gla_a2=m_w_gla_a2,
             b_gla_a=m_b_gla_a, b_fox_f=m_b_fox_f, ret_norm_g=m_ret_norm_g, gla_norm_g=m_gla_norm_g, q_norm_g=m_q_norm_g,
             k_norm_g=m_k_norm_g, w_br=m_w_br, w_mg=m_w_mg, b_mg=m_b_mg, w_o=m_w_o, w_up=m_w_up, w_conv=m_w_conv,
             b_conv=m_b_conv, w_down=m_w_down)
    V = dict(norm1_g=v_norm1_g, norm2_g=v_norm2_g, w_ada=v_w_ada, b_ada=v_b_ada, w_in=v_w_in, w_gla_a2=v_w_gla_a2,
             b_gla_a=v_b_gla_a, b_fox_f=v_b_fox_f, ret_norm_g=v_ret_norm_g, gla_norm_g=v_gla_norm_g, q_norm_g=v_q_norm_g,
             k_norm_g=v_k_norm_g, w_br=v_w_br, w_mg=v_w_mg, b_mg=v_b_mg, w_o=v_w_o, w_up=v_w_up, w_conv=v_w_conv,
             b_conv=v_b_conv, w_down=v_w_down)
    me = 4 * lax.axis_index("x") + 2 * lax.axis_index("y") + lax.axis_index("c")
    x2d, tgt = x.reshape(T, D), loss_target.reshape(T, D)

    sm = _flat_pack([c, w_gla_a2, w_conv])
    sm_all = _exchange(sm, True, "gather_small")
    parts = [_flat_unpack(sm_all[j], [(D,), (DEPTH, LR_LANES, 32), (DEPTH, 3, 352)]) for j in range(N_DEV)]
    c_all = jnp.stack([q[0] for q in parts])
    w_gla_full = jnp.concatenate([q[1] for q in parts], axis=2)
    w_conv_full = jnp.concatenate([q[2] for q in parts], axis=2)

    n_ada = w_ada.shape[2]
    b_loc = lax.dynamic_slice_in_dim(b_ada, me * n_ada, n_ada, axis=1).reshape(DEPTH, 1, n_ada)
    mod_all = _ada_fwd(c_all, w_ada, b_loc)
    mod_recv = _exchange(jnp.swapaxes(mod_all, 0, 1), False, "a2a_mod")
    mod = jnp.swapaxes(mod_recv, 0, 1).reshape(DEPTH, 6, D)

    loc = dict(w_in=_permute_in(w_in), w_o=w_o, w_down=w_down, w_br=jnp.swapaxes(w_br, 2, 3),
               w_mg=jnp.swapaxes(w_mg, 1, 2), w_up=jnp.swapaxes(w_up, 1, 2))
    loc = {k: v.astype(BF16) for k, v in loc.items()}
    w_full = dict(w_in=(D, NP), w_o=(D, D), w_down=(D_FF, D), w_br=(3, D, BW), w_mg=(3 * D, D), w_up=(2 * D_FF, D))
    w_parts = dict(w_in=4, w_br=1, w_mg=2, w_o=1, w_up=4, w_down=2)
    gather, units = _Transfers("gather"), []
    for l in range(DEPTH):
        for k, parts in w_parts.items():
            axis = 1 if k == "w_br" else 0
            n = w_full[k][axis] // N_DEV
            gather.lands[(l, k)] = lax.empty(w_full[k], BF16)
            shard = loc[k][l]
            units += [((l, k), (axis, n, (), c * (n // parts), n // parts), shard) for c in range(parts)]
    for i in range(len(units) + 2):
        if i < len(units):
            key, where, shard = units[i]
            gather.add(key, ("gather_chip",) + where, shard, uid=i)
        if i >= 2:
            key, where, _ = units[i - 2]
            gather.add(key, ("pass_on",) + where, after=i - 2)

    w2pad = jnp.pad(w_gla_full, ((0, 0), (0, 128 - LR_LANES), (0, 0)))
    btail = jnp.pad(b_fox_f, ((0, 0), (FF_LANE0, 128 - FF_LANE0 - N_HEADS)))
    stacked = dict(norm1_g=norm1_g, norm2_g=norm2_g, b_gla_a=b_gla_a, ret_norm_g=ret_norm_g, gla_norm_g=gla_norm_g,
                   q_norm_g=q_norm_g, k_norm_g=k_norm_g, b_mg=b_mg, b_conv=b_conv, w_conv=w_conv_full, w2pad=w2pad,
                   btail=btail, shift1=mod[:, 0], scale1=mod[:, 1], gate1=mod[:, 2], shift2=mod[:, 3], scale2=mod[:, 4],
                   gate2=mod[:, 5])
    landed = lambda l, k: functools.partial(gather.get, (l, k))
    layers = [dict({k: v[l] for k, v in stacked.items()}, w_in=landed(l, "w_in"), w_o=landed(l, "w_o"),
                   w_down=landed(l, "w_down"), w_br_t=landed(l, "w_br"), w_mg_t=landed(l, "w_mg"), w_up_t=landed(l, "w_up"))
              for l in range(DEPTH)]
    consts = _rope_tables() + (_ret_logg(),)

    xc, saved = x2d, []
    for l in range(DEPTH):
        xc, sv = _layer_fwd(xc, layers[l], consts, gather)
        saved.append(sv)
    loss_part, dxc = _loss_fwd_bwd(xc, tgt)
    loss = lax.psum(loss_part[0, 0], ("x", "y", "c"))

    grad_names = ("w_in", "w_o", "w_down", "w_br0", "w_br1", "w_br2", "w_mg", "w_up")
    blk_rows = dict(w_in=(128, NP), w_o=(128, D), w_down=(352, D), w_br0=(128, BW), w_br1=(128, BW), w_br2=(128, BW),
                    w_mg=(384, D), w_up=(704, D))
    grads = _Transfers("grads")
    for k in grad_names:
        grads.lands[k] = lax.empty((N_DEV, DEPTH) + blk_rows[k], BF16)
    dmod, small_g = [None] * DEPTH, [None] * DEPTH
    for l in reversed(range(DEPTH)):
        dxc, _, dmod[l], small_g[l] = _layer_bwd(dxc, layers[l], saved[l], consts, grads, l)
    grad_x = dxc
    dmod = jnp.stack(dmod)
    small_g = {k: jnp.stack([s[k] for s in small_g]) for k in small_g[0]}
    grads.drain()
    recv = {k: grads.get(k) for k in grad_names}

    dmod_send = jnp.swapaxes(dmod.reshape(DEPTH, N_DEV, n_ada), 0, 1)
    dmod_all = jnp.swapaxes(_exchange(dmod_send, False, "a2a_dmod"), 0, 1)
    g_ada = _ada_bwd(c_all, dmod_all)

    def flat(a, k):
        return a.reshape((-1, W[k].shape[-1]))

    def adam_nat(k, g, tr):
        outs = _adamw(g, flat(W[k], k), flat(M[k], k), flat(V[k], k), tr, "adamw_" + k)
        return [o.reshape(W[k].shape) for o in outs]

    def summed(k, tr):
        r = recv[k]
        return _sum_partials(r.reshape(N_DEV, DEPTH * r.shape[2], r.shape[3]), "sum_" + k, tr).reshape((DEPTH,) + r.shape[2:])

    big_out = dict(
        w_in=adam_nat("w_in", flat(_unpermute_in(summed("w_in", 64)), "w_in"), 64),
        w_o=adam_nat("w_o", recv["w_o"].reshape(N_DEV, DEPTH * 128, D), 128),
        w_down=adam_nat("w_down", recv["w_down"].reshape(N_DEV, DEPTH * 352, D), 352),
        w_br=adam_nat("w_br", flat(jnp.swapaxes(jnp.stack([summed("w_br%d" % n, 128) for n in range(3)], axis=1), 2, 3),
                                   "w_br"), 1024),
        w_mg=adam_nat("w_mg", flat(jnp.swapaxes(summed("w_mg", 384), 1, 2), "w_mg"), 512),
        w_up=adam_nat("w_up", flat(jnp.swapaxes(summed("w_up", 704), 1, 2), "w_up"), 512))
    ada_out = [o.reshape(DEPTH, D, n_ada) for o in _adamw(
        g_ada.reshape(DEPTH * D, n_ada), w_ada.reshape(DEPTH * D, n_ada), m_w_ada.reshape(DEPTH * D, n_ada),
        v_w_ada.reshape(DEPTH * D, n_ada), 512, "adamw_ada")]

    small_g = dict(small_g, b_ada=dmod)
    names = SMALL_REPL + SMALL_SHARDED
    full_shapes = [W[n].shape for n in SMALL_REPL] + [(DEPTH, LR_LANES, 256), (DEPTH, 3, D_FF)]
    part = _flat_pack([small_g[n] for n in names])
    total = _flat_unpack(_sum_partials(_exchange(part, True, "gather_small_grads"), "sum_small"), full_shapes)
    total = dict(zip(names, total))
    total["w_gla_a2"] = lax.dynamic_slice_in_dim(total["w_gla_a2"], me * 32, 32, axis=2)
    total["w_conv"] = lax.dynamic_slice_in_dim(total["w_conv"], me * 352, 352, axis=2)
    shapes = [W[n].shape for n in names]
    small_out = _adamw(_flat_pack([total[n] for n in names]), _flat_pack([W[n] for n in names]),
                       _flat_pack([M[n] for n in names]), _flat_pack([V[n] for n in names]), None, "adamw_small")
    small_out = [dict(zip(names, _flat_unpack(o, shapes))) for o in small_out]

    outs = []
    for k in range(4):
        d = dict(small_out[k])
        d.update({n: big_out[n][k] for n in BIG})
        d["w_ada"] = ada_out[k]
        outs.append([d[n] for n in WEIGHTS])
    return (loss, grad_x.reshape(1, T, D), *outs[0], *outs[1], *outs[2], *outs[3])
```

```python
import functools

import numpy as np
import jax
import jax.numpy as jnp
from jax import lax
from jax.experimental import pallas as pl
from jax.experimental.pallas import tpu as pltpu

F32 = jnp.float32
BF16 = jnp.bfloat16

N_DEV = 8
T = 2048
D = 1024
DEPTH = 4
N_HEADS = 4
HD = 128
BW = 512
D_FF = 2816
CHUNK = 64
EPS = 1e-6
IN_W = 5140
NP = 5632
TAIL0 = 5120
LR_LANES = 16
FF_LANE0 = 16
PACK_W = 1024
SEG_ROWS = (704, 128, 352, 192, 384, 704)
LAYER_ROWS = sum(SEG_ROWS)
VMEM_LIMIT_V7X = 56 * 1024 * 1024

ADAM_LR, ADAM_B1, ADAM_B2, ADAM_EPS, ADAM_WD, ADAM_STEP = 0.001, 0.9, 0.999, 1e-08, 0.01, 10

MESH_ID = pl.DeviceIdType.MESH


def _cp(*sem):
    return pltpu.CompilerParams(dimension_semantics=sem if sem else None, vmem_limit_bytes=VMEM_LIMIT_V7X)


def _sigmoid(z):
    return 1.0 / (1.0 + jnp.exp(-z))


def _log_sigmoid(z):
    return jnp.minimum(z, 0.0) - jnp.log(1.0 + jnp.exp(-jnp.abs(z)))


def _sum0(a):
    return jnp.sum(a, axis=0, keepdims=True)


def _mean1(a):
    return jnp.mean(a, axis=-1, keepdims=True)


def _dot(a, b, dims):
    return lax.dot_general(a.astype(BF16), b.astype(BF16), (dims, ((), ())), preferred_element_type=F32)


NN = ((1,), (0,))
NT = ((1,), (1,))
TN = ((0,), (0,))


def _exact_dot(m01, a):
    a1 = a.astype(BF16)
    r1 = a - a1.astype(F32)
    a2 = r1.astype(BF16)
    a3 = (r1 - a2.astype(F32)).astype(BF16)
    d = lambda z: jnp.dot(m01, z, preferred_element_type=F32)
    return d(a1) + d(a2) + d(a3)


def _tri(n, upper):
    r = lax.broadcasted_iota(jnp.int32, (n, n), 0)
    c = lax.broadcasted_iota(jnp.int32, (n, n), 1)
    return jnp.where((c >= r) if upper else (c <= r), 1.0, 0.0).astype(BF16)


def _exchange(x, gather, name):
    blk = x.shape if gather else x.shape[1:]

    def body(x_ref, o_ref, send_sems, recv_sems, loc_sem):
        mx, my, mc = lax.axis_index("x"), lax.axis_index("y"), lax.axis_index("c")
        me = 4 * mx + 2 * my + mc
        loc = pltpu.make_async_copy(x_ref if gather else x_ref.at[me], o_ref.at[me], loc_sem)
        loc.start()
        copies = []
        for k in range(1, N_DEV):
            px = mx ^ (k >> 2) if (k >> 2) else mx
            py = my ^ ((k >> 1) & 1) if ((k >> 1) & 1) else my
            pc = mc ^ (k & 1) if (k & 1) else mc
            peer = 4 * px + 2 * py + pc
            cp = pltpu.make_async_remote_copy(
                src_ref=x_ref if gather else x_ref.at[peer], dst_ref=o_ref.at[me],
                send_sem=send_sems.at[k - 1], recv_sem=recv_sems.at[k - 1],
                device_id=(px, py, pc), device_id_type=MESH_ID)
            cp.start()
            copies.append(cp)
        for cp in copies:
            cp.wait()
        loc.wait()

    return pl.pallas_call(
        body, name=name,
        out_shape=jax.ShapeDtypeStruct((N_DEV,) + tuple(blk), x.dtype),
        in_specs=[pl.BlockSpec(memory_space=pl.ANY)],
        out_specs=pl.BlockSpec(memory_space=pl.ANY),
        scratch_shapes=[pltpu.SemaphoreType.DMA((N_DEV - 1,)), pltpu.SemaphoreType.DMA((N_DEV - 1,)),
                        pltpu.SemaphoreType.DMA],
        compiler_params=pltpu.CompilerParams(has_side_effects=True),
    )(x)


def _blk(ref, axis, j, n, r0=0, nr=None):
    return ref.at[(slice(None),) * axis + (pl.ds(j * n + r0, n if nr is None else nr),)]


def _comm_copies(items, srcs, lands, send_sems, recv_sems, loc_sems):
    mx, my, mc = lax.axis_index("x"), lax.axis_index("y"), lax.axis_index("c")
    me = 4 * mx + 2 * my + mc
    local, remote = [], []
    for t, (kind, axis, n, sel, r0, nr, si, li) in enumerate(items):
        if kind == "pass_on":
            for q in (2, 4, 6):
                px = 1 - mx if q & 4 else mx
                py = 1 - my if q & 2 else my
                rows = _blk(lands[li], axis, 4 * px + 2 * py + mc, n, r0, nr)
                remote.append(pltpu.make_async_remote_copy(
                    src_ref=rows, dst_ref=rows, send_sem=send_sems.at[t * (N_DEV - 1) + q - 1],
                    recv_sem=recv_sems.at[t * (N_DEV - 1) + q - 1], device_id=(mx, my, 1 - mc), device_id_type=MESH_ID))
            continue
        if kind == "to_other_core":
            for p in range(N_DEV // 2):
                remote.append(pltpu.make_async_remote_copy(
                    src_ref=_blk(srcs[si], axis, 2 * p + 1 - mc, n, r0, nr), dst_ref=lands[li].at[p, pl.ds(r0, nr)],
                    send_sem=send_sems.at[t * (N_DEV - 1) + p], recv_sem=recv_sems.at[t * (N_DEV - 1) + p],
                    device_id=(mx, my, 1 - mc), device_id_type=MESH_ID))
            continue
        if kind == "a2a_chip":
            pm = 2 * mx + my
            mine = lands[li].at[(pm,) + tuple(sel) + (pl.ds(r0, nr),)]
            local.append(pltpu.make_async_copy(srcs[si].at[pm, pl.ds(r0, nr)], mine, loc_sems.at[t]))
            for q in (2, 4, 6):
                px = 1 - mx if q & 4 else mx
                py = 1 - my if q & 2 else my
                remote.append(pltpu.make_async_remote_copy(
                    src_ref=srcs[si].at[2 * px + py, pl.ds(r0, nr)], dst_ref=mine,
                    send_sem=send_sems.at[t * (N_DEV - 1) + q - 1], recv_sem=recv_sems.at[t * (N_DEV - 1) + q - 1],
                    device_id=(px, py, mc), device_id_type=MESH_ID))
            continue
        if kind == "a2a":
            mine = lands[li].at[(me,) + tuple(sel) + (pl.ds(r0, nr),)]
            own = _blk(srcs[si], axis, me, n, r0, nr)
        else:
            mine = _blk(lands[li], axis, me, n, r0, nr)
            own = _blk(srcs[si], axis, 0, n, r0, nr)
        local.append(pltpu.make_async_copy(own, mine, loc_sems.at[t]))
        for k in ((1, 2, 4, 6) if kind == "gather_chip" else range(1, N_DEV)):
            px = 1 - mx if k & 4 else mx
            py = 1 - my if k & 2 else my
            pc = 1 - mc if k & 1 else mc
            src = _blk(srcs[si], axis, 4 * px + 2 * py + pc, n, r0, nr) if kind == "a2a" else own
            remote.append(pltpu.make_async_remote_copy(
                src_ref=src, dst_ref=mine, send_sem=send_sems.at[t * (N_DEV - 1) + k - 1],
                recv_sem=recv_sems.at[t * (N_DEV - 1) + k - 1], device_id=(px, py, pc), device_id_type=MESH_ID))
    return local, remote


def _comm_scratch(n_items):
    return [pltpu.SemaphoreType.DMA((n_items * (N_DEV - 1),)), pltpu.SemaphoreType.DMA((n_items * (N_DEV - 1),)),
            pltpu.SemaphoreType.DMA((n_items,))]


class _Duty:
    def __init__(self, items, srcs, lands, done):
        self.items, self.srcs, self.lands, self.done = items, srcs, lands, done


def _pcall(body, name, grid, in_specs, out_specs, out_shape, args, scratch_shapes=(), sem=(), duty=None):
    if duty is None:
        return pl.pallas_call(body, name=name, grid=grid, in_specs=list(in_specs), out_specs=out_specs,
                              out_shape=out_shape, scratch_shapes=list(scratch_shapes), compiler_params=_cp(*sem))(*args)
    single = not isinstance(out_shape, (list, tuple))
    o_shape = [out_shape] if single else list(out_shape)
    o_specs = [out_specs] if single else list(out_specs)
    n_in, n_out, n_scr = len(in_specs), len(o_shape), len(scratch_shapes)
    n_src, n_land, n_items = len(duty.srcs), len(duty.lands), len(duty.items)
    a0 = n_in + n_src + n_land

    def wrapped(*refs):
        srcs = refs[n_in:n_in + n_src]
        lands = refs[a0 + n_out:a0 + n_out + n_land]
        core = refs[:n_in] + refs[a0:a0 + n_out] + refs[a0 + n_out + n_land:a0 + n_out + n_land + n_scr]
        sems = refs[a0 + n_out + n_land + n_scr:]
        first = functools.reduce(jnp.logical_and, [pl.program_id(a) == 0 for a in range(len(grid))])
        last = functools.reduce(jnp.logical_and, [pl.program_id(a) == g - 1 for a, g in enumerate(grid)])

        @pl.when(first)
        def _():
            local, remote = _comm_copies(duty.items, srcs, lands, *sems)
            for cp in local + remote:
                cp.start()

        body(*core)

        @pl.when(last)
        def _():
            local, remote = _comm_copies(duty.items, srcs, lands, *sems)
            for cp in remote + local:
                cp.wait()

    hbm = pl.BlockSpec(memory_space=pl.ANY)
    res = pl.pallas_call(
        wrapped, name=name, grid=grid,
        in_specs=list(in_specs) + [hbm] * (n_src + n_land), out_specs=o_specs + [hbm] * n_land,
        out_shape=o_shape + [jax.ShapeDtypeStruct(a.shape, a.dtype) for a in duty.lands],
        input_output_aliases={n_in + n_src + t: n_out + t for t in range(n_land)},
        scratch_shapes=list(scratch_shapes) + _comm_scratch(n_items),
        compiler_params=pltpu.CompilerParams(dimension_semantics=("arbitrary",) * len(grid),
                                             vmem_limit_bytes=VMEM_LIMIT_V7X, has_side_effects=True),
    )(*args, *duty.srcs, *duty.lands)
    duty.done(res[n_out:])
    return res[0] if single else res[:n_out]


def _comm(duty, name):
    n_src, n_land = len(duty.srcs), len(duty.lands)

    def body(*refs):
        local, remote = _comm_copies(duty.items, refs[:n_src], refs[n_src + n_land:n_src + 2 * n_land],
                                     *refs[n_src + 2 * n_land:])
        for cp in local + remote:
            cp.start()
        for cp in remote + local:
            cp.wait()

    hbm = pl.BlockSpec(memory_space=pl.ANY)
    duty.done(pl.pallas_call(
        body, name=name, out_shape=[jax.ShapeDtypeStruct(a.shape, a.dtype) for a in duty.lands],
        in_specs=[hbm] * (n_src + n_land), out_specs=[hbm] * n_land,
        input_output_aliases={n_src + t: t for t in range(n_land)},
        scratch_shapes=_comm_scratch(len(duty.items)), compiler_params=pltpu.CompilerParams(has_side_effects=True),
    )(*duty.srcs, *duty.lands))


class _Transfers:
    def __init__(self, name):
        self.name, self.queue, self.lands, self.flushes, self.groups = name, [], {}, 0, {}

    def add(self, key, item, src=None, uid=None, after=None, group=None):
        self.queue.append((key, item, src, uid, after, group))
        if group is not None:
            self.groups[group] = [self.groups.get(group, [0, None])[0] + 1, None]

    def when_done(self, group, fn):
        self.groups[group][1] = fn

    def take(self, count):
        units = []
        while self.queue and len(units) < count:
            after = self.queue[0][4]
            if after is not None and any(u[3] == after for u in units):
                break
            units.append(self.queue.pop(0))
        if not units:
            return None
        keys, srcs, items = [], [], []
        for key, item, src, _, _, _ in units:
            if key not in keys:
                keys.append(key)
            if src is not None and not any(src is s for s in srcs):
                srcs.append(src)
            si = [i for i, s in enumerate(srcs) if s is src][0] if src is not None else -1
            items.append(tuple(item) + (si, keys.index(key)))

        def done(new_lands):
            for key, arr in zip(keys, new_lands):
                self.lands[key] = arr
            for u in units:
                if u[5] is not None:
                    self.groups[u[5]][0] -= 1
                    if self.groups[u[5]][0] == 0:
                        self.groups[u[5]][1]()

        return _Duty(items, srcs, [self.lands[k] for k in keys], done)

    def drain(self, upto=None):
        count = upto
        while self.queue if upto is None else count > 0:
            duty = self.take(len(self.queue) if upto is None else count)
            count = None if upto is None else count - len(duty.items)
            self.flushes += 1
            _comm(duty, "%s_flush%d" % (self.name, self.flushes))

    def get(self, key):
        pending = [i for i, u in enumerate(self.queue) if u[0] == key]
        if pending:
            self.drain(pending[-1] + 1)
        return self.lands[key]


def _matmul(a, b, mode, name, out_dtype=F32, tm=1024, tn=512, tk=None, add=None, duty=None):
    halves = a.ndim == 3
    if mode == "tn":
        K, M = a.shape[-2], a.shape[-1] * (2 if halves else 1)
        N = b.shape[1]
    else:
        M, K = a.shape[-2], a.shape[-1] * (2 if halves else 1)
        N = b.shape[0] if mode == "nt" else b.shape[1]
    tm, tn = min(tm, M), min(tn, N)
    tk = K if tk is None else tk
    nk = K // tk
    assert M % tm == 0 and N % tn == 0 and K % tk == 0, (name, M, N, K, tm, tn, tk)
    dims = {"nn": NN, "nt": NT, "tn": TN}[mode]
    has_add = add is not None

    def body(*refs):
        if has_add:
            a_ref, b_ref, add_ref, o_ref, acc_ref = refs
        else:
            a_ref, b_ref, o_ref, acc_ref = refs
        k = pl.program_id(2)
        part = _dot(a_ref[...], b_ref[...], dims)

        @pl.when(k == 0)
        def _():
            acc_ref[...] = part

        @pl.when(k > 0)
        def _():
            acc_ref[...] += part

        @pl.when(k == nk - 1)
        def _():
            r = acc_ref[...]
            if has_add:
                r = r + add_ref[...]
            o_ref[...] = r.astype(o_ref.dtype)

    if halves and mode == "tn":
        per = a.shape[-1] // tm
        a_spec = pl.BlockSpec((None, tk, tm), lambda i, j, k: (i // per, k, i % per))
    elif halves:
        per = a.shape[-1] // tk
        a_spec = pl.BlockSpec((None, tm, tk), lambda i, j, k: (k // per, i, k % per))
    elif mode == "tn":
        a_spec = pl.BlockSpec((tk, tm), lambda i, j, k: (k, i))
    else:
        a_spec = pl.BlockSpec((tm, tk), lambda i, j, k: (i, k))
    if mode == "nt":
        b_spec = pl.BlockSpec((tn, tk), lambda i, j, k: (j, k))
    else:
        b_spec = pl.BlockSpec((tk, tn), lambda i, j, k: (k, j))
    o_spec = pl.BlockSpec((tm, tn), lambda i, j, k: (i, j))
    in_specs = [a_spec, b_spec] + ([o_spec] if has_add else [])
    args = (a, b) + ((add,) if has_add else ())
    return _pcall(
        body, name=name, grid=(M // tm, N // tn, nk),
        out_shape=jax.ShapeDtypeStruct((M, N), out_dtype),
        in_specs=in_specs, out_specs=o_spec,
        scratch_shapes=[pltpu.VMEM((tm, tn), F32)],
        sem=("parallel", "parallel", "arbitrary"), args=args, duty=duty)


def _ada_fwd(c_all, w_ada, b_loc):
    n = w_ada.shape[2]

    def body(c_ref, w_ref, b_ref, o_ref):
        c = c_ref[...]
        o_ref[0] = _dot(c * _sigmoid(c), w_ref[0], NN) + b_ref[0]

    return pl.pallas_call(
        body, name="ada_fwd", grid=(DEPTH,),
        out_shape=jax.ShapeDtypeStruct((DEPTH, N_DEV, n), F32),
        in_specs=[pl.BlockSpec((N_DEV, D), lambda l: (0, 0)),
                  pl.BlockSpec((1, D, n), lambda l: (l, 0, 0)),
                  pl.BlockSpec((1, 1, n), lambda l: (l, 0, 0))],
        out_specs=pl.BlockSpec((1, N_DEV, n), lambda l: (l, 0, 0)),
        compiler_params=_cp("parallel"),
    )(c_all, w_ada, b_loc)


def _ada_bwd(c_all, dmod_all):
    n = dmod_all.shape[2]

    def body(c_ref, d_ref, o_ref):
        c = c_ref[...]
        o_ref[0] = _dot(c * _sigmoid(c), d_ref[0], TN)

    return pl.pallas_call(
        body, name="ada_bwd", grid=(DEPTH,),
        out_shape=jax.ShapeDtypeStruct((DEPTH, D, n), F32),
        in_specs=[pl.BlockSpec((N_DEV, D), lambda l: (0, 0)),
                  pl.BlockSpec((1, N_DEV, n), lambda l: (l, 0, 0))],
        out_specs=pl.BlockSpec((1, D, n), lambda l: (l, 0, 0)),
        compiler_params=_cp("parallel"),
    )(c_all, dmod_all)


ROW_TILE = 256


def _row_spec(w=D, col=0):
    return pl.BlockSpec((ROW_TILE, w), lambda i: (i, col))


def _vec_spec(w=D):
    return pl.BlockSpec((1, w), lambda i: (0, 0))


def _norm_fwd(x, g, scale, shift, name, m=None, gate=None):
    has_res = m is not None

    def body(*refs):
        if has_res:
            x_ref, m_ref, gate_ref, g_ref, sc_ref, sh_ref, xo_ref, h_ref = refs
            xv = x_ref[...] + gate_ref[...] * m_ref[...]
            xo_ref[...] = xv
        else:
            x_ref, g_ref, sc_ref, sh_ref, h_ref = refs
            xv = x_ref[...]
        r = lax.rsqrt(_mean1(xv * xv) + EPS)
        h_ref[...] = ((xv * r * g_ref[...]) * (1.0 + sc_ref[...]) + sh_ref[...]).astype(BF16)

    ins = [x] + ([m, gate] if has_res else []) + [g, scale, shift]
    in_specs = [_row_spec()] + ([_row_spec(), _vec_spec()] if has_res else []) + [_vec_spec()] * 3
    out_shape = [jax.ShapeDtypeStruct((T, D), BF16)]
    out_specs = [_row_spec()]
    if has_res:
        out_shape = [jax.ShapeDtypeStruct((T, D), F32)] + out_shape
        out_specs = [_row_spec()] + out_specs
    out = pl.pallas_call(body, name=name, grid=(T // ROW_TILE,), out_shape=out_shape, in_specs=in_specs,
                         out_specs=out_specs, compiler_params=_cp("parallel"))(*ins)
    return out if has_res else out[0]


def _norm_bwd(x, dh, dres, g, scale, shift, name):
    def body(x_ref, dh_ref, dres_ref, g_ref, sc_ref, sh_ref, dx_ref, st_ref):
        xv, dh_v, gv = x_ref[...], dh_ref[...], g_ref[...]
        r = lax.rsqrt(_mean1(xv * xv) + EPS)
        n = xv * r
        dy = dh_v * (1.0 + sc_ref[...])
        dn = dy * gv
        dx_ref[...] = r * (dn - n * _mean1(dn * n)) + dres_ref[...]

        @pl.when(pl.program_id(0) == 0)
        def _():
            st_ref[...] = jnp.zeros_like(st_ref)

        st_ref[0:1, :] += _sum0(dy * n)
        st_ref[1:2, :] += _sum0(dh_v * (n * gv))
        st_ref[2:3, :] += _sum0(dh_v)

    return pl.pallas_call(
        body, name=name, grid=(T // ROW_TILE,),
        out_shape=[jax.ShapeDtypeStruct((T, D), F32), jax.ShapeDtypeStruct((8, D), F32)],
        in_specs=[_row_spec(), _row_spec(), _row_spec(), _vec_spec(), _vec_spec(), _vec_spec()],
        out_specs=[_row_spec(), pl.BlockSpec((8, D), lambda i: (0, 0))],
        compiler_params=_cp("arbitrary"),
    )(x, dh, dres, g, scale, shift)


def _axpy(x, m, gate, name):
    def body(x_ref, m_ref, gate_ref, o_ref):
        o_ref[...] = x_ref[...] + gate_ref[...] * m_ref[...]

    return pl.pallas_call(
        body, name=name, grid=(T // ROW_TILE,), out_shape=jax.ShapeDtypeStruct((T, D), F32),
        in_specs=[_row_spec(), _row_spec(), _vec_spec()], out_specs=_row_spec(),
        compiler_params=_cp("parallel"))(x, m, gate)


def _gate_bwd(dx, m, gate, name):
    def body(dx_ref, m_ref, gate_ref, dm_ref, st_ref):
        dxv = dx_ref[...]
        dm_ref[...] = (gate_ref[...] * dxv).astype(BF16)

        @pl.when(pl.program_id(0) == 0)
        def _():
            st_ref[...] = jnp.zeros_like(st_ref)

        st_ref[0:1, :] += _sum0(dxv * m_ref[...])

    return pl.pallas_call(
        body, name=name, grid=(T // ROW_TILE,),
        out_shape=[jax.ShapeDtypeStruct((T, D), BF16), jax.ShapeDtypeStruct((8, D), F32)],
        in_specs=[_row_spec(), _row_spec(), _vec_spec()],
        out_specs=[_row_spec(), pl.BlockSpec((8, D), lambda i: (0, 0))],
        compiler_params=_cp("arbitrary"))(dx, m, gate)


def _loss_fwd_bwd(y, target):
    def body(y_ref, t_ref, l_ref, d_ref):
        e = y_ref[...] - t_ref[...]
        d_ref[...] = e * (1.0 / D)

        @pl.when(pl.program_id(0) == 0)
        def _():
            l_ref[...] = jnp.zeros_like(l_ref)

        l_ref[...] += jnp.sum(_sum0(e * e), axis=1, keepdims=True) * (0.5 / D)

    return pl.pallas_call(
        body, name="loss", grid=(T // ROW_TILE,),
        out_shape=[jax.ShapeDtypeStruct((8, 128), F32), jax.ShapeDtypeStruct((T, D), F32)],
        in_specs=[_row_spec(), _row_spec()],
        out_specs=[pl.BlockSpec((8, 128), lambda i: (0, 0)), _row_spec()],
        compiler_params=_cp("arbitrary"))(y, target)


def _rope_tables():
    half = HD // 2
    inv_freq = 10000.0 ** (-jnp.arange(half, dtype=F32) / half)
    ang = jnp.arange(T, dtype=F32)[:, None] * inv_freq[None, :]
    cos, sin = jnp.cos(ang), jnp.sin(ang)
    return jnp.concatenate([cos, cos], axis=1), jnp.concatenate([-sin, sin], axis=1)


def _rope_fwd(p, cosf, sinf):
    def body(p_ref, c_ref, s_ref, o_ref):
        cv, sv = c_ref[...], s_ref[...]
        for j in range(2 * N_HEADS):
            xv = p_ref[:, j * HD:(j + 1) * HD]
            rot = xv * cv + pltpu.roll(xv, HD // 2, 1) * sv
            if j >= N_HEADS:
                rot = rot * (HD ** -0.5)
            o_ref[:, j * HD:(j + 1) * HD] = rot.astype(BF16)

    return pl.pallas_call(
        body, name="rope_fwd", grid=(T // ROW_TILE,),
        out_shape=jax.ShapeDtypeStruct((T, 2 * BW), BF16),
        in_specs=[_row_spec(2 * BW), _row_spec(HD), _row_spec(HD)], out_specs=_row_spec(2 * BW),
        compiler_params=_cp("parallel"))(p, cosf, sinf)


def _rope_bwd(dq, dk, cosf, sinf):
    def body(dq_ref, dk_ref, c_ref, s_ref, o_ref):
        cv, sv = c_ref[...], s_ref[...]
        for j in range(2 * N_HEADS):
            h = j % N_HEADS
            d = dq_ref[:, h * HD:(h + 1) * HD] if j < N_HEADS else dk_ref[:, h * HD:(h + 1) * HD] * (HD ** -0.5)
            o_ref[:, j * HD:(j + 1) * HD] = d * cv + pltpu.roll(d * sv, HD // 2, 1)

    return pl.pallas_call(
        body, name="rope_bwd", grid=(T // ROW_TILE,),
        out_shape=jax.ShapeDtypeStruct((T, 2 * BW), F32),
        in_specs=[_row_spec(BW), _row_spec(BW), _row_spec(HD), _row_spec(HD)], out_specs=_row_spec(2 * BW),
        compiler_params=_cp("parallel"))(dq, dk, cosf, sinf)


TQ = 256
V_RET_BLK = 8


def _ret_logg():
    lg = jnp.log1p(-jnp.exp2(-5.0 - jnp.arange(N_HEADS, dtype=F32)))
    return jnp.broadcast_to(lg[:, None, None], (N_HEADS, 1, 128))


def _block_iotas(i, kl):
    rows = lax.broadcasted_iota(jnp.int32, (TQ, kl), 0) + i * TQ
    cols = lax.broadcasted_iota(jnp.int32, (TQ, kl), 1)
    return rows, cols


def _ret_weight(lg_ref, i, kl):
    rows, cols = _block_iotas(i, kl)
    dist = jnp.abs(rows - cols).astype(F32)
    w = jnp.exp(dist * lg_ref[0][:, 0:1])
    return jnp.where((cols >> 6) <= (rows >> 6), w, 0.0)


def _per_query_block(i, fn):
    for n in range(1, T // TQ + 1):
        pl.when(i == n - 1)(functools.partial(fn, n * TQ))


def _ret_specs():
    q_spec = pl.BlockSpec((TQ, HD), lambda h, i: (i, h))
    k_spec = pl.BlockSpec((T, HD), lambda h, i: (0, N_HEADS + h))
    v_spec = pl.BlockSpec((T, HD), lambda h, i: (0, V_RET_BLK + h))
    lg_spec = pl.BlockSpec((1, 1, 128), lambda h, i: (h, 0, 0))
    return q_spec, k_spec, v_spec, lg_spec


def _ret_fwd(qk, p, logg, duty=None):
    def body(q_ref, k_ref, v_ref, lg_ref, o_ref):
        i = pl.program_id(1)

        def visible(kl):
            s = _dot(q_ref[...], k_ref[0:kl, :], NT) * _ret_weight(lg_ref, i, kl)
            o_ref[...] = _dot(s, v_ref[0:kl, :], NN)

        _per_query_block(i, visible)

    q_spec, k_spec, v_spec, lg_spec = _ret_specs()
    return _pcall(
        body, name="ret_fwd", grid=(N_HEADS, T // TQ),
        out_shape=jax.ShapeDtypeStruct((T, BW), F32),
        in_specs=[q_spec, k_spec, v_spec, lg_spec], out_specs=q_spec,
        sem=("parallel", "parallel"), args=(qk, qk, p, logg), duty=duty)


def _ret_bwd(qk, p, logg, do, duty=None):
    def body(q_ref, k_ref, v_ref, lg_ref, do_ref, dq_ref, dk_ref, dv_ref):
        i = pl.program_id(1)
        q, dov = q_ref[...], do_ref[...]

        @pl.when(i == 0)
        def _():
            dk_ref[...] = jnp.zeros_like(dk_ref)
            dv_ref[...] = jnp.zeros_like(dv_ref)

        def visible(kl):
            w = _ret_weight(lg_ref, i, kl)
            k = k_ref[0:kl, :]
            s = _dot(q, k, NT) * w
            ds = _dot(dov, v_ref[0:kl, :], NT) * w
            dk_ref[0:kl, :] += _dot(ds, q, TN)
            dv_ref[0:kl, :] += _dot(s, dov, TN)
            dq_ref[...] = _dot(ds, k, NN)

        _per_query_block(i, visible)

    q_spec, k_spec, v_spec, lg_spec = _ret_specs()
    acc_spec = pl.BlockSpec((T, HD), lambda h, i: (0, h))
    sh = jax.ShapeDtypeStruct((T, BW), F32)
    return _pcall(
        body, name="ret_bwd", grid=(N_HEADS, T // TQ),
        out_shape=[sh, sh, sh],
        in_specs=[q_spec, k_spec, v_spec, lg_spec, q_spec], out_specs=[q_spec, acc_spec, acc_spec],
        sem=("parallel", "arbitrary"), args=(qk, qk, p, logg, do), duty=duty)


def _post_norm(xv, gv, centered):
    if centered:
        xv = xv - _mean1(xv)
    r = lax.rsqrt(_mean1(xv * xv) + EPS)
    return xv * r, r


def _branch_post_fwd(raw, p, g, gate_blk, centered, name):
    def body(raw_ref, z_ref, g_ref, o_ref):
        for h in range(N_HEADS):
            sl = slice(h * HD, (h + 1) * HD)
            gv = g_ref[:, sl] if centered else g_ref[...]
            xh, _ = _post_norm(raw_ref[:, sl], gv, centered)
            z = z_ref[:, sl]
            o_ref[:, sl] = (z * _sigmoid(z) * (xh * gv)).astype(BF16)

    return pl.pallas_call(
        body, name=name, grid=(T // ROW_TILE,),
        out_shape=jax.ShapeDtypeStruct((T, BW), BF16),
        in_specs=[_row_spec(BW), _row_spec(BW, gate_blk), _vec_spec(BW if centered else HD)],
        out_specs=_row_spec(BW), compiler_params=_cp("parallel"))(raw, p, g)


def _branch_post_bwd(raw, p, g, dout, gate_blk, centered, name):
    gw = BW if centered else HD

    def body(raw_ref, z_ref, g_ref, do_ref, dr_ref, dz_ref, dg_ref):
        @pl.when(pl.program_id(0) == 0)
        def _():
            dg_ref[...] = jnp.zeros_like(dg_ref)

        for h in range(N_HEADS):
            sl = slice(h * HD, (h + 1) * HD)
            gsl = sl if centered else slice(0, HD)
            gv, z, dov = g_ref[:, gsl], z_ref[:, sl], do_ref[:, sl]
            xh, r = _post_norm(raw_ref[:, sl], gv, centered)
            sg = _sigmoid(z)
            dyn = dov * (z * sg)
            dz_ref[:, sl] = dov * (xh * gv) * (sg * (1.0 + z * (1.0 - sg)))
            dxh = dyn * gv
            t = dxh - xh * _mean1(dxh * xh)
            if centered:
                t = t - _mean1(dxh)
            dr_ref[:, sl] = r * t
            dg_ref[0:1, gsl] += _sum0(dyn * xh)

    return pl.pallas_call(
        body, name=name, grid=(T // ROW_TILE,),
        out_shape=[jax.ShapeDtypeStruct((T, BW), F32), jax.ShapeDtypeStruct((T, BW), F32),
                   jax.ShapeDtypeStruct((8, gw), F32)],
        in_specs=[_row_spec(BW), _row_spec(BW, gate_blk), _vec_spec(gw), _row_spec(BW)],
        out_specs=[_row_spec(BW), _row_spec(BW), pl.BlockSpec((8, gw), lambda i: (0, 0))],
        compiler_params=_cp("arbitrary"))(raw, p, g, dout)


GLA_ROWS = 256
GLA_CPB = GLA_ROWS // CHUNK
GLA_DK = 64
GLA_W = N_HEADS * GLA_DK
GQ_BLK, GK_BLK, GV_BLK, GG_BLK, TAIL_BLK = 8, 9, 5, 6, 40
RG_BLK = 3


def _gla_chunk_common(tl, w2, bv, kv):
    pre = _dot(tl, w2, NN) + bv
    la = _log_sigmoid(pre) * (1.0 / 16.0)
    bc = _exact_dot(_tri(CHUNK, False), la)
    be = bc[CHUNK - 1:CHUNK, :]
    w = jnp.exp(be - bc)
    return pre, w, jnp.exp(be), kv * w


def _head_masks():
    lane = lax.broadcasted_iota(jnp.int32, (1, GLA_W), 1)
    return [jnp.where((lane // GLA_DK) == h, 1.0, 0.0) for h in range(N_HEADS)]


def _gla_fwd(p, w2pad, b, duty=None):
    nb = T // GLA_ROWS

    def body(q_ref, k_ref, v_ref, t_ref, w2_ref, b_ref, o_ref, st_ref, s_acc):
        @pl.when(pl.program_id(0) == 0)
        def _():
            s_acc[...] = jnp.zeros_like(s_acc)

        masks = _head_masks()
        for c in range(GLA_CPB):
            rows = slice(c * CHUNK, (c + 1) * CHUNK)
            _, _, a, kd = _gla_chunk_common(t_ref[rows, :], w2_ref[...], b_ref[...], k_ref[rows, :])
            q = q_ref[rows, :] * (GLA_DK ** -0.5)
            kv = None
            for h in range(N_HEADS):
                t = _dot(v_ref[rows, h * HD:(h + 1) * HD], kd * masks[h], TN)
                kv = t if kv is None else kv + t
            s_new = s_acc[...] * a + kv
            s_acc[...] = s_new
            st_ref[c] = s_new
            for h in range(N_HEADS):
                o_ref[rows, h * HD:(h + 1) * HD] = _dot(q * masks[h], s_new, NT)

    return _pcall(
        body, name="gla_fwd", grid=(nb,),
        out_shape=[jax.ShapeDtypeStruct((T, BW), F32), jax.ShapeDtypeStruct((T // CHUNK, HD, GLA_W), F32)],
        in_specs=[pl.BlockSpec((GLA_ROWS, GLA_W), lambda i: (i, GQ_BLK)),
                  pl.BlockSpec((GLA_ROWS, GLA_W), lambda i: (i, GK_BLK)),
                  pl.BlockSpec((GLA_ROWS, BW), lambda i: (i, GV_BLK)),
                  pl.BlockSpec((GLA_ROWS, 128), lambda i: (i, TAIL_BLK)),
                  pl.BlockSpec((128, GLA_W), lambda i: (0, 0)),
                  pl.BlockSpec((1, GLA_W), lambda i: (0, 0))],
        out_specs=[pl.BlockSpec((GLA_ROWS, BW), lambda i: (i, 0)),
                   pl.BlockSpec((GLA_CPB, HD, GLA_W), lambda i: (i, 0, 0))],
        scratch_shapes=[pltpu.VMEM((HD, GLA_W), F32)],
        sem=("arbitrary",), args=(p, p, p, p, w2pad, b), duty=duty)


def _gla_bwd(p, w2pad, b, states, do, duty=None):
    nb = T // GLA_ROWS

    def body(q_ref, k_ref, v_ref, t_ref, w2_ref, b_ref, st_ref, prev_ref, do_ref,
             dq_ref, dk_ref, dv_ref, dt_ref, dw2_ref, db_ref, ds_acc):
        step = pl.program_id(0)

        @pl.when(step == 0)
        def _():
            ds_acc[...] = jnp.zeros_like(ds_acc)
            dw2_ref[...] = jnp.zeros_like(dw2_ref)
            db_ref[...] = jnp.zeros_like(db_ref)

        masks = _head_masks()
        up = _tri(CHUNK, True)
        has_prev = jnp.where(step == nb - 1, 0.0, 1.0)
        for c in reversed(range(GLA_CPB)):
            rows = slice(c * CHUNK, (c + 1) * CHUNK)
            tl, w2, k = t_ref[rows, :], w2_ref[...], k_ref[rows, :]
            pre, w, a, kd = _gla_chunk_common(tl, w2, b_ref[...], k)
            q = q_ref[rows, :] * (GLA_DK ** -0.5)
            s_n = st_ref[c]
            s_prev = st_ref[c - 1] if c > 0 else prev_ref[0] * has_prev
            ds = ds_acc[...]
            dos = [do_ref[rows, h * HD:(h + 1) * HD] for h in range(N_HEADS)]
            for h in range(N_HEADS):
                ds = ds + _dot(dos[h], q * masks[h], TN)
            dqp = jnp.zeros((CHUNK, GLA_W), F32)
            dkd = jnp.zeros((CHUNK, GLA_W), F32)
            for h in range(N_HEADS):
                dqp = dqp + masks[h] * _dot(dos[h], s_n, NN)
                dkd = dkd + masks[h] * _dot(v_ref[rows, h * HD:(h + 1) * HD], ds, NN)
                dv_ref[rows, h * HD:(h + 1) * HD] = _dot(kd * masks[h], ds, NT)
            dq_ref[rows, :] = dqp * (GLA_DK ** -0.5)
            dk_ref[rows, :] = dkd * w
            e = dkd * k * w
            dbe = _sum0(e) + _sum0(ds * s_prev) * a
            dla = dbe - _exact_dot(up, e)
            dpre = dla * (1.0 / 16.0) * _sigmoid(-pre)
            db_ref[0:1, :] += _sum0(dpre)
            dw2_ref[...] += _dot(tl, dpre, TN)
            dt_ref[rows, :] = _dot(dpre, w2, NT)
            ds_acc[...] = ds * a

    rev = lambda i: nb - 1 - i
    sh = lambda w: jax.ShapeDtypeStruct((T, w), F32)
    return _pcall(
        body, name="gla_bwd", grid=(nb,),
        out_shape=[sh(GLA_W), sh(GLA_W), sh(BW), sh(128), jax.ShapeDtypeStruct((128, GLA_W), F32),
                   jax.ShapeDtypeStruct((8, GLA_W), F32)],
        in_specs=[pl.BlockSpec((GLA_ROWS, GLA_W), lambda i: (rev(i), GQ_BLK)),
                  pl.BlockSpec((GLA_ROWS, GLA_W), lambda i: (rev(i), GK_BLK)),
                  pl.BlockSpec((GLA_ROWS, BW), lambda i: (rev(i), GV_BLK)),
                  pl.BlockSpec((GLA_ROWS, 128), lambda i: (rev(i), TAIL_BLK)),
                  pl.BlockSpec((128, GLA_W), lambda i: (0, 0)),
                  pl.BlockSpec((1, GLA_W), lambda i: (0, 0)),
                  pl.BlockSpec((GLA_CPB, HD, GLA_W), lambda i: (rev(i), 0, 0)),
                  pl.BlockSpec((1, HD, GLA_W), lambda i: (jnp.maximum(rev(i) * GLA_CPB - 1, 0), 0, 0)),
                  pl.BlockSpec((GLA_ROWS, BW), lambda i: (rev(i), 0))],
        out_specs=[pl.BlockSpec((GLA_ROWS, GLA_W), lambda i: (rev(i), 0)),
                   pl.BlockSpec((GLA_ROWS, GLA_W), lambda i: (rev(i), 0)),
                   pl.BlockSpec((GLA_ROWS, BW), lambda i: (rev(i), 0)),
                   pl.BlockSpec((GLA_ROWS, 128), lambda i: (rev(i), 0)),
                   pl.BlockSpec((128, GLA_W), lambda i: (0, 0)),
                   pl.BlockSpec((8, GLA_W), lambda i: (0, 0))],
        scratch_shapes=[pltpu.VMEM((HD, GLA_W), F32)],
        sem=("arbitrary",), args=(p, p, p, p, w2pad, b, states, states, do), duty=duty)


FQ_BLK, FK_BLK = 7, 8
V_FOX_BLK = 36


def _fox_prep_fwd(p, qg, kg, btail):
    def body(q_ref, k_ref, t_ref, qg_ref, kg_ref, bt_ref, o_ref, cum_ref, carry):
        @pl.when(pl.program_id(0) == 0)
        def _():
            carry[...] = jnp.zeros_like(carry)

        for src, gr, off in ((q_ref, qg_ref, 0), (k_ref, kg_ref, BW)):
            for h in range(N_HEADS):
                xv = src[:, h * HD:(h + 1) * HD]
                r = lax.rsqrt(_mean1(xv * xv) + EPS)
                o_ref[:, off + h * HD:off + (h + 1) * HD] = (xv * r * gr[...]).astype(BF16)
        logf = _log_sigmoid(t_ref[...] + bt_ref[...])
        cum = _exact_dot(_tri(ROW_TILE, False), logf) + carry[...]
        cum_ref[...] = cum
        carry[...] = cum[ROW_TILE - 1:ROW_TILE, :]

    return pl.pallas_call(
        body, name="fox_prep_fwd", grid=(T // ROW_TILE,),
        out_shape=[jax.ShapeDtypeStruct((T, 2 * BW), BF16), jax.ShapeDtypeStruct((T, 128), F32)],
        in_specs=[_row_spec(BW, FQ_BLK), _row_spec(BW, FK_BLK), _row_spec(128, TAIL_BLK),
                  _vec_spec(HD), _vec_spec(HD), _vec_spec(128)],
        out_specs=[_row_spec(2 * BW), _row_spec(128)],
        scratch_shapes=[pltpu.VMEM((1, 128), F32)],
        compiler_params=_cp("arbitrary"))(p, p, p, qg, kg, btail)


def _fox_prep_bwd(p, qg, kg, btail, dqn, dkn, dcum):
    nt = T // ROW_TILE

    def body(q_ref, k_ref, t_ref, qg_ref, kg_ref, bt_ref, dq_ref, dk_ref, dc_ref, o_ref, dt_ref, st_ref, carry):
        @pl.when(pl.program_id(0) == 0)
        def _():
            carry[...] = jnp.zeros_like(carry)
            st_ref[...] = jnp.zeros_like(st_ref)

        for row, (src, gr, dsrc, off) in enumerate(((q_ref, qg_ref, dq_ref, 0), (k_ref, kg_ref, dk_ref, BW))):
            for h in range(N_HEADS):
                xv = src[:, h * HD:(h + 1) * HD]
                dy = dsrc[:, h * HD:(h + 1) * HD]
                r = lax.rsqrt(_mean1(xv * xv) + EPS)
                n = xv * r
                dn = dy * gr[...]
                o_ref[:, off + h * HD:off + (h + 1) * HD] = r * (dn - n * _mean1(dn * n))
                st_ref[row:row + 1, :] += _sum0(dy * n)
        z = t_ref[...] + bt_ref[...]
        dlogf = _exact_dot(_tri(ROW_TILE, True), dc_ref[...]) + carry[...]
        carry[...] = dlogf[0:1, :]
        lane = lax.broadcasted_iota(jnp.int32, (1, 128), 1)
        keep = (lane >= FF_LANE0) & (lane < FF_LANE0 + N_HEADS)
        dz = jnp.where(keep, dlogf * _sigmoid(-z), 0.0)
        dt_ref[...] = dz
        st_ref[2:3, :] += _sum0(dz)

    rs = lambda w, col=0: pl.BlockSpec((ROW_TILE, w), lambda i: (nt - 1 - i, col))
    return pl.pallas_call(
        body, name="fox_prep_bwd", grid=(nt,),
        out_shape=[jax.ShapeDtypeStruct((T, 2 * BW), F32), jax.ShapeDtypeStruct((T, 128), F32),
                   jax.ShapeDtypeStruct((8, 128), F32)],
        in_specs=[rs(BW, FQ_BLK), rs(BW, FK_BLK), rs(128, TAIL_BLK), _vec_spec(HD), _vec_spec(HD), _vec_spec(128),
                  rs(BW), rs(BW), rs(128)],
        out_specs=[rs(2 * BW), rs(128), pl.BlockSpec((8, 128), lambda i: (0, 0))],
        scratch_shapes=[pltpu.VMEM((1, 128), F32)],
        compiler_params=_cp("arbitrary"))(p, p, p, qg, kg, btail, dqn, dkn, dcum)


def _fox_logits(q_ref, k_ref, cc_ref, cr_ref, i, kl):
    rows, cols = _block_iotas(i, kl)
    s = _dot(q_ref[...], k_ref[0:kl, :], NT) * (HD ** -0.5) + cc_ref[0] - cr_ref[0, :, 0:kl]
    return jnp.where(cols <= rows, s, -1e30)


def _fox_specs():
    q_spec = pl.BlockSpec((TQ, HD), lambda h, i: (i, h))
    k_spec = pl.BlockSpec((T, HD), lambda h, i: (0, N_HEADS + h))
    v_spec = pl.BlockSpec((T, HD), lambda h, i: (0, V_FOX_BLK + h))
    col_spec = pl.BlockSpec((1, TQ, 1), lambda h, i: (h, i, 0))
    row_spec = pl.BlockSpec((1, 1, T), lambda h, i: (h, 0, 0))
    return q_spec, k_spec, v_spec, col_spec, row_spec


def _fox_fwd(qkn, p, cumcol, cumrow, duty=None):
    def body(q_ref, k_ref, v_ref, cc_ref, cr_ref, o_ref, lse_ref):
        i = pl.program_id(1)

        def visible(kl):
            s = _fox_logits(q_ref, k_ref, cc_ref, cr_ref, i, kl)
            m = jnp.max(s, axis=-1, keepdims=True)
            e = jnp.exp(s - m)
            l = jnp.sum(e, axis=-1, keepdims=True)
            o_ref[...] = _dot(e / l, v_ref[0:kl, :], NN)
            lse_ref[0] = m + jnp.log(l)

        _per_query_block(i, visible)

    q_spec, k_spec, v_spec, col_spec, row_spec = _fox_specs()
    return _pcall(
        body, name="fox_fwd", grid=(N_HEADS, T // TQ),
        out_shape=[jax.ShapeDtypeStruct((T, BW), F32), jax.ShapeDtypeStruct((N_HEADS, T, 1), F32)],
        in_specs=[q_spec, k_spec, v_spec, col_spec, row_spec], out_specs=[q_spec, col_spec],
        sem=("parallel", "parallel"), args=(qkn, qkn, p, cumcol, cumrow), duty=duty)


def _fox_bwd(qkn, p, cumcol, cumrow, lse, o, do, duty=None):
    def body(q_ref, k_ref, v_ref, cc_ref, cr_ref, lse_ref, o_ref, do_ref, dq_ref, dk_ref, dv_ref, dr_ref, dc_ref):
        i = pl.program_id(1)
        @pl.when(i == 0)
        def _():
            dk_ref[...] = jnp.zeros_like(dk_ref)
            dv_ref[...] = jnp.zeros_like(dv_ref)
            dc_ref[...] = jnp.zeros_like(dc_ref)

        def visible(kl):
            q, dov = q_ref[...], do_ref[...]
            pm = jnp.exp(_fox_logits(q_ref, k_ref, cc_ref, cr_ref, i, kl) - lse_ref[0])
            delta = jnp.sum(o_ref[...] * dov, axis=-1, keepdims=True)
            ds = pm * (_dot(dov, v_ref[0:kl, :], NT) - delta)
            dq_ref[...] = _dot(ds, k_ref[0:kl, :], NN) * (HD ** -0.5)
            dr_ref[0] = jnp.sum(ds, axis=-1, keepdims=True)
            dk_ref[0:kl, :] += _dot(ds, q, TN) * (HD ** -0.5)
            dv_ref[0:kl, :] += _dot(pm, dov, TN)
            dc_ref[0, :, 0:kl] += _sum0(ds)

        _per_query_block(i, visible)

    q_spec, k_spec, v_spec, col_spec, row_spec = _fox_specs()
    acc_spec = pl.BlockSpec((T, HD), lambda h, i: (0, h))
    sh = jax.ShapeDtypeStruct((T, BW), F32)
    return _pcall(
        body, name="fox_bwd", grid=(N_HEADS, T // TQ),
        out_shape=[sh, sh, sh, jax.ShapeDtypeStruct((N_HEADS, T, 1), F32), jax.ShapeDtypeStruct((N_HEADS, 1, T), F32)],
        in_specs=[q_spec, k_spec, v_spec, col_spec, row_spec, col_spec, q_spec, q_spec],
        out_specs=[q_spec, acc_spec, acc_spec, col_spec, row_spec],
        sem=("parallel", "arbitrary"), args=(qkn, qkn, p, cumcol, cumrow, lse, o, do), duty=duty)


def _mix_fwd(gpre, b_mg, y0, y1, y2, duty=None):
    def body(g_ref, b_ref, y0_ref, y1_ref, y2_ref, o_ref):
        acc = None
        for n, y_ref in enumerate((y0_ref, y1_ref, y2_ref)):
            sl = slice(n * D, (n + 1) * D)
            t = _sigmoid(g_ref[:, sl] + b_ref[:, sl]) * y_ref[...]
            acc = t if acc is None else acc + t
        o_ref[...] = acc.astype(BF16)

    return _pcall(
        body, name="mix_fwd", grid=(T // ROW_TILE,), out_shape=jax.ShapeDtypeStruct((T, D), BF16),
        in_specs=[_row_spec(3 * D), _vec_spec(3 * D), _row_spec(), _row_spec(), _row_spec()],
        out_specs=_row_spec(), sem=("parallel",), args=(gpre, b_mg, y0, y1, y2), duty=duty)


def _mix_bwd(gpre, b_mg, y0, y1, y2, dmi, duty=None):
    def body(g_ref, b_ref, y0_ref, y1_ref, y2_ref, d_ref, dy0_ref, dy1_ref, dy2_ref, dg_ref, db_ref):
        @pl.when(pl.program_id(0) == 0)
        def _():
            db_ref[...] = jnp.zeros_like(db_ref)

        dv = d_ref[...]
        for n, (y_ref, dy_ref) in enumerate(((y0_ref, dy0_ref), (y1_ref, dy1_ref), (y2_ref, dy2_ref))):
            sl = slice(n * D, (n + 1) * D)
            sg = _sigmoid(g_ref[:, sl] + b_ref[:, sl])
            dy_ref[...] = (dv * sg).astype(BF16)
            dpre = dv * y_ref[...] * (sg * (1.0 - sg))
            dg_ref[:, sl] = dpre.astype(BF16)
            db_ref[0:1, sl] += _sum0(dpre)

    shb = jax.ShapeDtypeStruct((T, D), BF16)
    return _pcall(
        body, name="mix_bwd", grid=(T // ROW_TILE,),
        out_shape=[shb, shb, shb, jax.ShapeDtypeStruct((T, 3 * D), BF16), jax.ShapeDtypeStruct((8, 3 * D), F32)],
        in_specs=[_row_spec(3 * D), _vec_spec(3 * D), _row_spec(), _row_spec(), _row_spec(), _row_spec()],
        out_specs=[_row_spec(), _row_spec(), _row_spec(), _row_spec(3 * D), pl.BlockSpec((8, 3 * D), lambda i: (0, 0))],
        sem=("arbitrary",), args=(gpre, b_mg, y0, y1, y2, dmi), duty=duty)


FF_COLS = 256
FF_NBLK = D_FF // FF_COLS


def _shift_rows(a, n):
    rows = lax.broadcasted_iota(jnp.int32, a.shape, 0)
    rolled = pltpu.roll(a, n % T, 0)
    return jnp.where((rows >= n) if n > 0 else (rows < T + n), rolled, 0.0)


def _ffn_act_fwd(uu, w_conv, b_conv, duty=None):
    def body(u_ref, g_ref, w_ref, b_ref, o_ref):
        u = u_ref[...]
        w = w_ref[...]
        uc = b_ref[...] + w[0:1, :] * _shift_rows(u, 2) + w[1:2, :] * _shift_rows(u, 1) + w[2:3, :] * u
        o_ref[...] = (uc * _sigmoid(uc) * g_ref[...]).astype(BF16)

    return _pcall(
        body, name="ffn_act_fwd", grid=(FF_NBLK,), out_shape=jax.ShapeDtypeStruct((T, D_FF), BF16),
        in_specs=[pl.BlockSpec((T, FF_COLS), lambda j: (0, j)), pl.BlockSpec((T, FF_COLS), lambda j: (0, FF_NBLK + j)),
                  pl.BlockSpec((3, FF_COLS), lambda j: (0, j)), pl.BlockSpec((1, FF_COLS), lambda j: (0, j))],
        out_specs=pl.BlockSpec((T, FF_COLS), lambda j: (0, j)),
        sem=("parallel",), args=(uu, uu, w_conv, b_conv), duty=duty)


def _ffn_act_bwd(uu, w_conv, b_conv, da, duty=None):
    def body(u_ref, g_ref, w_ref, b_ref, da_ref, d_ref, st_ref):
        u, w, dav = u_ref[...], w_ref[...], da_ref[...]
        u1, u2 = _shift_rows(u, 1), _shift_rows(u, 2)
        uc = b_ref[...] + w[0:1, :] * u2 + w[1:2, :] * u1 + w[2:3, :] * u
        sg = _sigmoid(uc)
        d_ref[1] = (dav * (uc * sg)).astype(BF16)
        duc = dav * g_ref[...] * (sg * (1.0 + uc * (1.0 - sg)))
        du = w[2:3, :] * duc + w[1:2, :] * _shift_rows(duc, -1) + w[0:1, :] * _shift_rows(duc, -2)
        d_ref[0] = du.astype(BF16)
        st_ref[...] = jnp.zeros_like(st_ref)
        st_ref[0:1, :] = _sum0(duc * u2)
        st_ref[1:2, :] = _sum0(duc * u1)
        st_ref[2:3, :] = _sum0(duc * u)
        st_ref[3:4, :] = _sum0(duc)

    cb = lambda rows=T, off=0: pl.BlockSpec((rows, FF_COLS), lambda j: (0, off + j))
    return _pcall(
        body, name="ffn_act_bwd", grid=(FF_NBLK,),
        out_shape=[jax.ShapeDtypeStruct((2, T, D_FF), BF16), jax.ShapeDtypeStruct((8, D_FF), F32)],
        in_specs=[cb(), cb(T, FF_NBLK), cb(3), cb(1), cb()],
        out_specs=[pl.BlockSpec((2, T, FF_COLS), lambda j: (0, 0, j)), cb(8)],
        sem=("parallel",), args=(uu, uu, w_conv, b_conv, da), duty=duty)


def _adamw(g, w, m, v, tr, name):
    partial = g.ndim == 3
    R, C = w.shape
    tr = R if tr is None else tr
    assert R % tr == 0

    def body(g_ref, w_ref, m_ref, v_ref, go_ref, d_ref, mo_ref, vo_ref):
        if partial:
            gv = g_ref[0].astype(F32)
            for j in range(1, g.shape[0]):
                gv = gv + g_ref[j].astype(F32)
        else:
            gv = g_ref[...]
        go_ref[...] = gv
        mn = ADAM_B1 * m_ref[...] + (1.0 - ADAM_B1) * gv
        vn = ADAM_B2 * v_ref[...] + (1.0 - ADAM_B2) * (gv * gv)
        mo_ref[...] = mn
        vo_ref[...] = vn
        m_hat = mn / (1.0 - ADAM_B1 ** ADAM_STEP)
        v_hat = vn / (1.0 - ADAM_B2 ** ADAM_STEP)
        d_ref[...] = -ADAM_LR * (m_hat / (jnp.sqrt(v_hat) + ADAM_EPS) + ADAM_WD * w_ref[...])

    spec = pl.BlockSpec((tr, C), lambda i: (i, 0))
    g_spec = pl.BlockSpec((g.shape[0], tr, C), lambda i: (0, i, 0)) if partial else spec
    sh = jax.ShapeDtypeStruct((R, C), F32)
    return pl.pallas_call(
        body, name=name, grid=(R // tr,), out_shape=[sh, sh, sh, sh],
        in_specs=[g_spec, spec, spec, spec], out_specs=[spec, spec, spec, spec],
        compiler_params=_cp("parallel"))(g, w, m, v)


def _chip_sum(dw, stage, name):
    n_chip, n, C = stage.shape

    def body(d_ref, s_ref, o_ref):
        mc = lax.axis_index("c")
        mine = jnp.where(mc == 0, d_ref[0, 0].astype(F32), d_ref[0, 1].astype(F32))
        o_ref[0] = (mine + s_ref[0].astype(F32)).astype(BF16)

    return pl.pallas_call(
        body, name=name, grid=(n_chip,), out_shape=jax.ShapeDtypeStruct(stage.shape, BF16),
        in_specs=[pl.BlockSpec((1, 2, n, C), lambda p: (p, 0, 0, 0)), pl.BlockSpec((1, n, C), lambda p: (p, 0, 0))],
        out_specs=pl.BlockSpec((1, n, C), lambda p: (p, 0, 0)), compiler_params=_cp("parallel"),
    )(dw.reshape(n_chip, 2, n, C), stage)


def _sum_partials(g, name, tr=None):
    n_part, R, C = g.shape
    tr = R if tr is None else tr
    assert R % tr == 0

    def body(g_ref, o_ref):
        acc = g_ref[0].astype(F32)
        for j in range(1, n_part):
            acc = acc + g_ref[j].astype(F32)
        o_ref[...] = acc

    return pl.pallas_call(
        body, name=name, grid=(R // tr,), out_shape=jax.ShapeDtypeStruct((R, C), F32),
        in_specs=[pl.BlockSpec((n_part, tr, C), lambda i: (0, i, 0))], out_specs=pl.BlockSpec((tr, C), lambda i: (i, 0)),
        compiler_params=_cp("parallel"))(g)


def _permute_in(w):
    pad = jnp.zeros(w.shape[:-1] + (NP - IN_W,), w.dtype)
    return jnp.concatenate([w[..., :3072], w[..., 3088:5136], w[..., 3072:3088], w[..., 5136:5140], pad], axis=-1)


def _unpermute_in(w):
    return jnp.concatenate([w[..., :3072], w[..., 5120:5136], w[..., 3072:5120], w[..., 5136:5140]], axis=-1)


def _flat_pack(arrs):
    flat = jnp.concatenate([a.reshape(-1).astype(F32) for a in arrs])
    n = flat.shape[0]
    rows = -(-n // 1024) * 8
    return jnp.pad(flat, (0, rows * 128 - n)).reshape(rows, 128)


def _flat_unpack(buf, shapes):
    flat = buf.reshape(-1)
    out, off = [], 0
    for s in shapes:
        n = int(np.prod(s))
        out.append(flat[off:off + n].reshape(s))
        off += n
    return out


GRAD_CHUNKS = dict(w_in=(128, 4), w_o=(128, 1), w_down=(352, 2), w_br0=(128, 1), w_br1=(128, 1), w_br2=(128, 1),
                   w_mg=(384, 2), w_up=(704, 4))
GRAD_VIA_CHIP = ("w_in", "w_down", "w_mg", "w_up")


def _send_grad(xfer, layer, k, g):
    if xfer is None:
        return g
    n, parts = GRAD_CHUNKS[k]
    if k not in GRAD_VIA_CHIP:
        for c in range(parts):
            xfer.add(k, ("a2a", 0, n, (layer,), c * (n // parts), n // parts), g)
        return g
    stage = ("stage", layer, k)
    xfer.lands[stage] = lax.empty((N_DEV // 2, n, g.shape[1]), BF16)
    xfer.add(stage, ("to_other_core", 0, n, (), 0, n), g, group=stage)

    def both_halves_here():
        chip = _chip_sum(g, xfer.lands[stage], "chip_sum_" + k)
        for c in range(parts):
            xfer.add(k, ("a2a_chip", 0, n, (layer,), c * (n // parts), n // parts), chip)

    xfer.when_done(stage, both_halves_here)
    return g


def _weight(wl, k):
    return wl[k]() if callable(wl[k]) else wl[k]


def _taker(xfer):
    return (lambda count: None) if xfer is None else xfer.take


def _layer_fwd(x0, wl, consts, xfer=None):
    cosf, sinf, logg = consts
    row = lambda a: a.reshape(1, -1)
    take_units = _taker(xfer)
    take = lambda count: take_units(2 * count)
    h = _norm_fwd(x0, row(wl["norm1_g"]), row(wl["scale1"]), row(wl["shift1"]), "norm1_fwd")
    p = _matmul(h, _weight(wl, "w_in"), "nn", "in_proj", duty=take(2))
    qk = _rope_fwd(p, cosf, sinf)
    ret_raw = _ret_fwd(qk, p, logg, duty=take(2))
    br0 = _branch_post_fwd(ret_raw, p, row(wl["ret_norm_g"]), RG_BLK, True, "ret_post_fwd")
    gla_raw, states = _gla_fwd(p, wl["w2pad"], row(wl["b_gla_a"]), duty=take(1))
    br1 = _branch_post_fwd(gla_raw, p, row(wl["gla_norm_g"]), GG_BLK, False, "gla_post_fwd")
    qkn, cum = _fox_prep_fwd(p, row(wl["q_norm_g"]), row(wl["k_norm_g"]), row(wl["btail"]))
    cum4 = cum[:, FF_LANE0:FF_LANE0 + N_HEADS].T
    cumcol, cumrow = cum4.reshape(N_HEADS, T, 1), cum4.reshape(N_HEADS, 1, T)
    fox_o, lse = _fox_fwd(qkn, p, cumcol, cumrow, duty=take(2))
    w_br_t = _weight(wl, "w_br_t")
    ys = [_matmul(b, w_br_t[n], "nt", "br_proj%d" % n) for n, b in enumerate((br0, br1, fox_o))]
    gpre = _matmul(h, _weight(wl, "w_mg_t"), "nt", "gate_proj", duty=take(1))
    mixed_in = _mix_fwd(gpre, row(wl["b_mg"]), *ys, duty=take(1))
    mixed = _matmul(mixed_in, _weight(wl, "w_o"), "nn", "o_proj")
    x1, h2 = _norm_fwd(x0, row(wl["norm2_g"]), row(wl["scale2"]), row(wl["shift2"]), "norm2_fwd",
                       m=mixed, gate=row(wl["gate1"]))
    uu = _matmul(h2, _weight(wl, "w_up_t"), "nt", "up_proj", duty=take(2))
    act = _ffn_act_fwd(uu, wl["w_conv"], row(wl["b_conv"]), duty=take(1))
    y = _matmul(act, _weight(wl, "w_down"), "nn", "down_proj", tk=1408, duty=take(2))
    x2 = _axpy(x1, y, row(wl["gate2"]), "resid2")
    saved = dict(x0=x0, h=h, p=p, qk=qk, ret_raw=ret_raw, br0=br0, gla_raw=gla_raw, states=states, br1=br1,
                 qkn=qkn, cumcol=cumcol, cumrow=cumrow, fox_o=fox_o, lse=lse, y0=ys[0], y1=ys[1], y2=ys[2],
                 gpre=gpre, mixed_in=mixed_in, mixed=mixed, x1=x1, h2=h2, uu=uu, act=act, y=y)
    return x2, saved


def _layer_bwd(dx2, wl, sv, consts, xfer=None, layer=0):
    cosf, sinf, logg = consts
    row = lambda a: a.reshape(1, -1)
    take = _taker(xfer)

    send = functools.partial(_send_grad, xfer, layer)

    dy, st_g2 = _gate_bwd(dx2, sv["y"], row(wl["gate2"]), "gate2_bwd")
    dact = _matmul(dy, _weight(wl, "w_down"), "nt", "down_dx", tn=1408, duty=take(1))
    d_down = send("w_down", _matmul(sv["act"], dy, "tn", "down_dw", out_dtype=BF16, tm=1408, duty=take(1)))
    duu, st_conv = _ffn_act_bwd(sv["uu"], wl["w_conv"], row(wl["b_conv"]), dact, duty=take(2))
    dh2 = _matmul(duu, _weight(wl, "w_up_t"), "nn", "up_dx", tk=1408, duty=take(2))
    d_up_t = send("w_up", _matmul(duu, sv["h2"], "tn", "up_dw", out_dtype=BF16, tm=1408, duty=take(1)))
    dx1, st_n2 = _norm_bwd(sv["x1"], dh2, dx2, row(wl["norm2_g"]), row(wl["scale2"]), row(wl["shift2"]), "norm2_bwd")
    dmixed, st_g1 = _gate_bwd(dx1, sv["mixed"], row(wl["gate1"]), "gate1_bwd")
    dmi = _matmul(dmixed, _weight(wl, "w_o"), "nt", "o_dx")
    d_o = send("w_o", _matmul(sv["mixed_in"], dmixed, "tn", "o_dw", out_dtype=BF16))
    dy0, dy1, dy2, dgpre, st_bmg = _mix_bwd(sv["gpre"], row(wl["b_mg"]), sv["y0"], sv["y1"], sv["y2"], dmi, duty=take(1))
    brs = (sv["br0"], sv["br1"], sv["fox_o"])
    w_br_t = _weight(wl, "w_br_t")
    dbr = [_matmul(d, w_br_t[n], "nn", "br_dx%d" % n) for n, d in enumerate((dy0, dy1, dy2))]
    d_br_t = [send("w_br%d" % n, _matmul(d, brs[n], "tn", "br_dw%d" % n, out_dtype=BF16))
              for n, d in enumerate((dy0, dy1, dy2))]
    dh = _matmul(dgpre, _weight(wl, "w_mg_t"), "nn", "gate_dx", tk=1024, duty=take(1))
    d_mg_t = send("w_mg", _matmul(dgpre, sv["h"], "tn", "gate_dw", out_dtype=BF16, duty=take(1)))
    p = sv["p"]
    dqn, dkn, dfv, drow, dcol = _fox_bwd(sv["qkn"], p, sv["cumcol"], sv["cumrow"], sv["lse"], sv["fox_o"], dbr[2],
                                         duty=take(2))
    dcum4 = drow.reshape(N_HEADS, T) - dcol.reshape(N_HEADS, T)
    dcum = jnp.pad(dcum4.T, ((0, 0), (FF_LANE0, 128 - FF_LANE0 - N_HEADS)))
    dfqk, dtail_fox, st_fox = _fox_prep_bwd(p, row(wl["q_norm_g"]), row(wl["k_norm_g"]), row(wl["btail"]), dqn, dkn, dcum)
    dgla_raw, dgg, st_gn = _branch_post_bwd(sv["gla_raw"], p, row(wl["gla_norm_g"]), dbr[1], GG_BLK, False, "gla_post_bwd")
    dgq, dgk, dgv, dtail_gla, dw2pad, st_bg = _gla_bwd(p, wl["w2pad"], row(wl["b_gla_a"]), sv["states"], dgla_raw,
                                                       duty=take(2))
    dret_raw, drg, st_rn = _branch_post_bwd(sv["ret_raw"], p, row(wl["ret_norm_g"]), dbr[0], RG_BLK, True, "ret_post_bwd")
    dqr, dkr, drv = _ret_bwd(sv["qk"], p, logg, dret_raw, duty=take(2))
    drqk = _rope_bwd(dqr, dkr, cosf, sinf)
    dp = jnp.concatenate([a.astype(BF16) for a in (drqk, drv, drg, dgq, dgk, dgv, dgg, dfqk, dfv, dtail_fox + dtail_gla)]
                         + [jnp.zeros((T, NP - TAIL0 - 128), BF16)], axis=1)
    dh = _matmul(dp, _weight(wl, "w_in"), "nt", "in_dx", tk=1408, add=dh, duty=take(3))
    d_in = send("w_in", _matmul(sv["h"], dp, "tn", "in_dw", out_dtype=BF16, duty=take(1)))
    dx0, st_n1 = _norm_bwd(sv["x0"], dh, dx1, row(wl["norm1_g"]), row(wl["scale1"]), row(wl["shift1"]), "norm1_bwd")
    big = dict(w_in=d_in, w_o=d_o, w_down=d_down, w_br0=d_br_t[0], w_br1=d_br_t[1], w_br2=d_br_t[2], w_mg=d_mg_t,
               w_up=d_up_t)
    dmod = jnp.concatenate([st_n1[2], st_n1[1], st_g1[0], st_n2[2], st_n2[1], st_g2[0]])
    small = dict(norm1_g=st_n1[0], norm2_g=st_n2[0], b_gla_a=st_bg[0], b_fox_f=st_fox[2, FF_LANE0:FF_LANE0 + N_HEADS],
                 ret_norm_g=st_rn[0], gla_norm_g=st_gn[0], q_norm_g=st_fox[0], k_norm_g=st_fox[1], b_mg=st_bmg[0],
                 b_conv=st_conv[3], w_gla_a2=dw2pad[:LR_LANES], w_conv=st_conv[0:3])
    return dx0, big, dmod, small


SMALL_REPL = ("norm1_g", "norm2_g", "b_ada", "b_gla_a", "b_fox_f", "ret_norm_g", "gla_norm_g", "q_norm_g", "k_norm_g",
              "b_mg", "b_conv")
SMALL_SHARDED = ("w_gla_a2", "w_conv")
BIG = ("w_in", "w_o", "w_down", "w_br", "w_mg", "w_up")
WEIGHTS = ("norm1_g", "norm2_g", "w_ada", "b_ada", "w_in", "w_gla_a2", "b_gla_a", "b_fox_f", "ret_norm_g", "gla_norm_g",
           "q_norm_g", "k_norm_g", "w_br", "w_mg", "b_mg", "w_o", "w_up", "w_conv", "b_conv", "w_down")


def kernel(x, c, norm1_g, norm2_g, w_ada, b_ada, w_in, w_gla_a2, b_gla_a, b_fox_f, ret_norm_g, gla_norm_g, q_norm_g, k_norm_g, w_br, w_mg, b_mg, w_o, w_up, w_conv, b_conv, w_down, loss_target, m_norm1_g, m_norm2_g, m_w_ada, m_b_ada, m_w_in, m_w_gla_a2, m_b_gla_a, m_b_fox_f, m_ret_norm_g, m_gla_norm_g, m_q_norm_g, m_k_norm_g, m_w_br, m_w_mg, m_b_mg, m_w_o, m_w_up, m_w_conv, m_b_conv, m_w_down, v_norm1_g, v_norm2_g, v_w_ada, v_b_ada, v_w_in, v_w_gla_a2, v_b_gla_a, v_b_fox_f, v_ret_norm_g, v_gla_norm_g, v_q_norm_g, v_k_norm_g, v_w_br, v_w_mg, v_b_mg, v_w_o, v_w_up, v_w_conv, v_b_conv, v_w_down):
    W = dict(norm1_g=norm1_g, norm2_g=norm2_g, w_ada=w_ada, b_ada=b_ada, w_in=w_in, w_gla_a2=w_gla_a2, b_gla_a=b_gla_a,
             b_fox_f=b_fox_f, ret_norm_g=ret_norm_g, gla_norm_g=gla_norm_g, q_norm_g=q_norm_g, k_norm_g=k_norm_g,
             w_br=w_br, w_mg=w_mg, b_mg=b_mg, w_o=w_o, w_up=w_up, w_conv=w_conv, b_conv=b_conv, w_down=w_down)
    M = dict(norm1_g=m_norm1_g, norm2_g=m_norm2_g, w_ada=m_w_ada, b_ada=m_b_ada, w_in=m_w_in, w_gla_a2=m_w_gla_a2,
             b_gla_a=m_b_gla_a, b_fox_f=m_b_fox_f, ret_norm_g=m_ret_norm_g, gla_norm_g=m_gla_norm_g, q_norm_g=m_q_norm_g,
             k_norm_g=m_k_norm_g, w_br=m_w_br, w_mg=m_w_mg, b_mg=m_b_mg, w_o=m_w_o, w_up=m_w_up, w_conv=m_w_conv,
             b_conv=m_b_conv, w_down=m_w_down)
    V = dict(norm1_g=v_norm1_g, norm2_g=v_norm2_g, w_ada=v_w_ada, b_ada=v_b_ada, w_in=v_w_in, w_gla_a2=v_w_gla_a2,
             b_gla_a=v_b_gla_a, b_fox_f=v_b_fox_f, ret_norm_g=v_ret_norm_g, gla_norm_g=v_gla_norm_g, q_norm_g=v_q_norm_g,
             k_norm_g=v_k_norm_g, w_br=v_w_br, w_mg=v_w_mg, b_mg=v_b_mg, w_o=v_w_o, w_up=v_w_up, w_conv=v_w_conv,
             b_conv=v_b_conv, w_down=v_w_down)
    me = 4 * lax.axis_index("x") + 2 * lax.axis_index("y") + lax.axis_index("c")
    x2d, tgt = x.reshape(T, D), loss_target.reshape(T, D)

    sm = _flat_pack([c, w_gla_a2, w_conv])
    sm_all = _exchange(sm, True, "gather_small")
    parts = [_flat_unpack(sm_all[j], [(D,), (DEPTH, LR_LANES, 32), (DEPTH, 3, 352)]) for j in range(N_DEV)]
    c_all = jnp.stack([q[0] for q in parts])
    w_gla_full = jnp.concatenate([q[1] for q in parts], axis=2)
    w_conv_full = jnp.concatenate([q[2] for q in parts], axis=2)

    n_ada = w_ada.shape[2]
    b_loc = lax.dynamic_slice_in_dim(b_ada, me * n_ada, n_ada, axis=1).reshape(DEPTH, 1, n_ada)
    mod_all = _ada_fwd(c_all, w_ada, b_loc)
    mod_recv = _exchange(jnp.swapaxes(mod_all, 0, 1), False, "a2a_mod")
    mod = jnp.swapaxes(mod_recv, 0, 1).reshape(DEPTH, 6, D)

    loc = dict(w_in=_permute_in(w_in), w_o=w_o, w_down=w_down, w_br=jnp.swapaxes(w_br, 2, 3),
               w_mg=jnp.swapaxes(w_mg, 1, 2), w_up=jnp.swapaxes(w_up, 1, 2))
    loc = {k: v.astype(BF16) for k, v in loc.items()}
    w_full = dict(w_in=(D, NP), w_o=(D, D), w_down=(D_FF, D), w_br=(3, D, BW), w_mg=(3 * D, D), w_up=(2 * D_FF, D))
    w_parts = dict(w_in=4, w_br=1, w_mg=2, w_o=1, w_up=4, w_down=2)
    gather, units = _Transfers("gather"), []
    for l in range(DEPTH):
        for k, parts in w_parts.items():
            axis = 1 if k == "w_br" else 0
            n = w_full[k][axis] // N_DEV
            gather.lands[(l, k)] = lax.empty(w_full[k], BF16)
            shard = loc[k][l]
            units += [((l, k), (axis, n, (), c * (n // parts), n // parts), shard) for c in range(parts)]
    for i in range(len(units) + 2):
        if i < len(units):
            key, where, shard = units[i]
            gather.add(key, ("gather_chip",) + where, shard, uid=i)
        if i >= 2:
            key, where, _ = units[i - 2]
            gather.add(key, ("pass_on",) + where, after=i - 2)

    w2pad = jnp.pad(w_gla_full, ((0, 0), (0, 128 - LR_LANES), (0, 0)))
    btail = jnp.pad(b_fox_f, ((0, 0), (FF_LANE0, 128 - FF_LANE0 - N_HEADS)))
    stacked = dict(norm1_g=norm1_g, norm2_g=norm2_g, b_gla_a=b_gla_a, ret_norm_g=ret_norm_g, gla_norm_g=gla_norm_g,
                   q_norm_g=q_norm_g, k_norm_g=k_norm_g, b_mg=b_mg, b_conv=b_conv, w_conv=w_conv_full, w2pad=w2pad,
                   btail=btail, shift1=mod[:, 0], scale1=mod[:, 1], gate1=mod[:, 2], shift2=mod[:, 3], scale2=mod[:, 4],
                   gate2=mod[:, 5])
    landed = lambda l, k: functools.partial(gather.get, (l, k))
    layers = [dict({k: v[l] for k, v in stacked.items()}, w_in=landed(l, "w_in"), w_o=landed(l, "w_o"),
                   w_down=landed(l, "w_down"), w_br_t=landed(l, "w_br"), w_mg_t=landed(l, "w_mg"), w_up_t=landed(l, "w_up"))
              for l in range(DEPTH)]
    consts = _rope_tables() + (_ret_logg(),)

    xc, saved = x2d, []
    for l in range(DEPTH):
        xc, sv = _layer_fwd(xc, layers[l], consts, gather)
        saved.append(sv)
    loss_part, dxc = _loss_fwd_bwd(xc, tgt)
    loss = lax.psum(loss_part[0, 0], ("x", "y", "c"))

    grad_names = ("w_in", "w_o", "w_down", "w_br0", "w_br1", "w_br2", "w_mg", "w_up")
    blk_rows = dict(w_in=(128, NP), w_o=(128, D), w_down=(352, D), w_br0=(128, BW), w_br1=(128, BW), w_br2=(128, BW),
                    w_mg=(384, D), w_up=(704, D))
    grads = _Transfers("grads")
    for k in grad_names:
        grads.lands[k] = lax.empty((N_DEV // 2 if k in GRAD_VIA_CHIP else N_DEV, DEPTH) + blk_rows[k], BF16)
    dmod, small_g = [None] * DEPTH, [None] * DEPTH
    for l in reversed(range(DEPTH)):
        dxc, _, dmod[l], small_g[l] = _layer_bwd(dxc, layers[l], saved[l], consts, grads, l)
    grad_x = dxc
    dmod = jnp.stack(dmod)
    small_g = {k: jnp.stack([s[k] for s in small_g]) for k in small_g[0]}
    grads.drain()
    recv = {k: grads.get(k) for k in grad_names}

    dmod_send = jnp.swapaxes(dmod.reshape(DEPTH, N_DEV, n_ada), 0, 1)
    dmod_all = jnp.swapaxes(_exchange(dmod_send, False, "a2a_dmod"), 0, 1)
    g_ada = _ada_bwd(c_all, dmod_all)

    def flat(a, k):
        return a.reshape((-1, W[k].shape[-1]))

    def adam_nat(k, g, tr):
        outs = _adamw(g, flat(W[k], k), flat(M[k], k), flat(V[k], k), tr, "adamw_" + k)
        return [o.reshape(W[k].shape) for o in outs]

    def summed(k, tr):
        r = recv[k]
        return _sum_partials(r.reshape(r.shape[0], DEPTH * r.shape[2], r.shape[3]), "sum_" + k, tr).reshape((DEPTH,) + r.shape[2:])

    big_out = dict(
        w_in=adam_nat("w_in", flat(_unpermute_in(summed("w_in", 64)), "w_in"), 64),
        w_o=adam_nat("w_o", recv["w_o"].reshape(N_DEV, DEPTH * 128, D), 128),
        w_down=adam_nat("w_down", recv["w_down"].reshape(N_DEV // 2, DEPTH * 352, D), 352),
        w_br=adam_nat("w_br", flat(jnp.swapaxes(jnp.stack([summed("w_br%d" % n, 128) for n in range(3)], axis=1), 2, 3),
                                   "w_br"), 1024),
        w_mg=adam_nat("w_mg", flat(jnp.swapaxes(summed("w_mg", 384), 1, 2), "w_mg"), 512),
        w_up=adam_nat("w_up", flat(jnp.swapaxes(summed("w_up", 704), 1, 2), "w_up"), 512))
    ada_out = [o.reshape(DEPTH, D, n_ada) for o in _adamw(
        g_ada.reshape(DEPTH * D, n_ada), w_ada.reshape(DEPTH * D, n_ada), m_w_ada.reshape(DEPTH * D, n_ada),
        v_w_ada.reshape(DEPTH * D, n_ada), 512, "adamw_ada")]

    small_g = dict(small_g, b_ada=dmod)
    names = SMALL_REPL + SMALL_SHARDED
    full_shapes = [W[n].shape for n in SMALL_REPL] + [(DEPTH, LR_LANES, 256), (DEPTH, 3, D_FF)]
    part = _flat_pack([small_g[n] for n in names])
    total = _flat_unpack(_sum_partials(_exchange(part, True, "gather_small_grads"), "sum_small"), full_shapes)
    total = dict(zip(names, total))
    total["w_gla_a2"] = lax.dynamic_slice_in_dim(total["w_gla_a2"], me * 32, 32, axis=2)
    total["w_conv"] = lax.dynamic_slice_in_dim(total["w_conv"], me * 352, 352, axis=2)
    shapes = [W[n].shape for n in names]
    small_out = _adamw(_flat_pack([total[n] for n in names]), _flat_pack([W[n] for n in names]),
                       _flat_pack([M[n] for n in names]), _flat_pack([V[n] for n in names]), None, "adamw_small")
    small_out = [dict(zip(names, _flat_unpack(o, shapes))) for o in small_out]

    outs = []
    for k in range(4):
        d = dict(small_out[k])
        d.update({n: big_out[n][k] for n in BIG})
        d["w_ada"] = ada_out[k]
        outs.append([d[n] for n in WEIGHTS])
    return (loss, grad_x.reshape(1, T, D), *outs[0], *outs[1], *outs[2], *outs[3])
```

```python
import functools

import numpy as np
import jax
import jax.numpy as jnp
from jax import lax
from jax.experimental import pallas as pl
from jax.experimental.pallas import tpu as pltpu

F32 = jnp.float32
BF16 = jnp.bfloat16

N_DEV = 8
T = 2048
D = 1024
DEPTH = 4
N_HEADS = 4
HD = 128
BW = 512
D_FF = 2816
CHUNK = 64
EPS = 1e-6
IN_W = 5140
NP = 5632
TAIL0 = 5120
LR_LANES = 16
FF_LANE0 = 16
PACK_W = 1024
SEG_ROWS = (704, 128, 352, 192, 384, 704)
LAYER_ROWS = sum(SEG_ROWS)
VMEM_LIMIT_V7X = 56 * 1024 * 1024

ADAM_LR, ADAM_B1, ADAM_B2, ADAM_EPS, ADAM_WD, ADAM_STEP = 0.001, 0.9, 0.999, 1e-08, 0.01, 10

MESH_ID = pl.DeviceIdType.MESH


def _cp(*sem):
    return pltpu.CompilerParams(dimension_semantics=sem if sem else None, vmem_limit_bytes=VMEM_LIMIT_V7X)


def _sigmoid(z):
    return 1.0 / (1.0 + jnp.exp(-z))


def _log_sigmoid(z):
    return jnp.minimum(z, 0.0) - jnp.log(1.0 + jnp.exp(-jnp.abs(z)))


def _sum0(a):
    return jnp.sum(a, axis=0, keepdims=True)


def _mean1(a):
    return jnp.mean(a, axis=-1, keepdims=True)


def _dot(a, b, dims):
    return lax.dot_general(a.astype(BF16), b.astype(BF16), (dims, ((), ())), preferred_element_type=F32)


NN = ((1,), (0,))
NT = ((1,), (1,))
TN = ((0,), (0,))


def _exact_dot(m01, a):
    a1 = a.astype(BF16)
    r1 = a - a1.astype(F32)
    a2 = r1.astype(BF16)
    a3 = (r1 - a2.astype(F32)).astype(BF16)
    d = lambda z: jnp.dot(m01, z, preferred_element_type=F32)
    return d(a1) + d(a2) + d(a3)


def _tri(n, upper):
    r = lax.broadcasted_iota(jnp.int32, (n, n), 0)
    c = lax.broadcasted_iota(jnp.int32, (n, n), 1)
    return jnp.where((c >= r) if upper else (c <= r), 1.0, 0.0).astype(BF16)


def _exchange(x, gather, name):
    blk = x.shape if gather else x.shape[1:]

    def body(x_ref, o_ref, send_sems, recv_sems, loc_sem):
        mx, my, mc = lax.axis_index("x"), lax.axis_index("y"), lax.axis_index("c")
        me = 4 * mx + 2 * my + mc
        loc = pltpu.make_async_copy(x_ref if gather else x_ref.at[me], o_ref.at[me], loc_sem)
        loc.start()
        copies = []
        for k in range(1, N_DEV):
            px = mx ^ (k >> 2) if (k >> 2) else mx
            py = my ^ ((k >> 1) & 1) if ((k >> 1) & 1) else my
            pc = mc ^ (k & 1) if (k & 1) else mc
            peer = 4 * px + 2 * py + pc
            cp = pltpu.make_async_remote_copy(
                src_ref=x_ref if gather else x_ref.at[peer], dst_ref=o_ref.at[me],
                send_sem=send_sems.at[k - 1], recv_sem=recv_sems.at[k - 1],
                device_id=(px, py, pc), device_id_type=MESH_ID)
            cp.start()
            copies.append(cp)
        for cp in copies:
            cp.wait()
        loc.wait()

    return pl.pallas_call(
        body, name=name,
        out_shape=jax.ShapeDtypeStruct((N_DEV,) + tuple(blk), x.dtype),
        in_specs=[pl.BlockSpec(memory_space=pl.ANY)],
        out_specs=pl.BlockSpec(memory_space=pl.ANY),
        scratch_shapes=[pltpu.SemaphoreType.DMA((N_DEV - 1,)), pltpu.SemaphoreType.DMA((N_DEV - 1,)),
                        pltpu.SemaphoreType.DMA],
        compiler_params=pltpu.CompilerParams(has_side_effects=True),
    )(x)


def _blk(ref, axis, j, n, r0=0, nr=None):
    return ref.at[(slice(None),) * axis + (pl.ds(j * n + r0, n if nr is None else nr),)]


def _comm_copies(items, srcs, lands, send_sems, recv_sems, loc_sems):
    mx, my, mc = lax.axis_index("x"), lax.axis_index("y"), lax.axis_index("c")
    me = 4 * mx + 2 * my + mc
    local, remote = [], []
    for t, (kind, axis, n, sel, r0, nr, si, li) in enumerate(items):
        if kind == "pass_on":
            for q in (2, 4, 6):
                px = 1 - mx if q & 4 else mx
                py = 1 - my if q & 2 else my
                rows = _blk(lands[li], axis, 4 * px + 2 * py + mc, n, r0, nr)
                remote.append(pltpu.make_async_remote_copy(
                    src_ref=rows, dst_ref=rows, send_sem=send_sems.at[t * (N_DEV - 1) + q - 1],
                    recv_sem=recv_sems.at[t * (N_DEV - 1) + q - 1], device_id=(mx, my, 1 - mc), device_id_type=MESH_ID))
            continue
        if kind == "to_other_core":
            for p in range(N_DEV // 2):
                remote.append(pltpu.make_async_remote_copy(
                    src_ref=_blk(srcs[si], axis, 2 * p + 1 - mc, n, r0, nr), dst_ref=lands[li].at[p, pl.ds(r0, nr)],
                    send_sem=send_sems.at[t * (N_DEV - 1) + p], recv_sem=recv_sems.at[t * (N_DEV - 1) + p],
                    device_id=(mx, my, 1 - mc), device_id_type=MESH_ID))
            continue
        if kind == "a2a_chip":
            pm = 2 * mx + my
            mine = lands[li].at[(pm,) + tuple(sel) + (pl.ds(r0, nr),)]
            local.append(pltpu.make_async_copy(srcs[si].at[pm, pl.ds(r0, nr)], mine, loc_sems.at[t]))
            for q in (2, 4, 6):
                px = 1 - mx if q & 4 else mx
                py = 1 - my if q & 2 else my
                remote.append(pltpu.make_async_remote_copy(
                    src_ref=srcs[si].at[2 * px + py, pl.ds(r0, nr)], dst_ref=mine,
                    send_sem=send_sems.at[t * (N_DEV - 1) + q - 1], recv_sem=recv_sems.at[t * (N_DEV - 1) + q - 1],
                    device_id=(px, py, mc), device_id_type=MESH_ID))
            continue
        if kind == "a2a":
            mine = lands[li].at[(me,) + tuple(sel) + (pl.ds(r0, nr),)]
            own = _blk(srcs[si], axis, me, n, r0, nr)
        else:
            mine = _blk(lands[li], axis, me, n, r0, nr)
            own = _blk(srcs[si], axis, 0, n, r0, nr)
        local.append(pltpu.make_async_copy(own, mine, loc_sems.at[t]))
        for k in ((1, 2, 4, 6) if kind == "gather_chip" else range(1, N_DEV)):
            px = 1 - mx if k & 4 else mx
            py = 1 - my if k & 2 else my
            pc = 1 - mc if k & 1 else mc
            src = _blk(srcs[si], axis, 4 * px + 2 * py + pc, n, r0, nr) if kind == "a2a" else own
            remote.append(pltpu.make_async_remote_copy(
                src_ref=src, dst_ref=mine, send_sem=send_sems.at[t * (N_DEV - 1) + k - 1],
                recv_sem=recv_sems.at[t * (N_DEV - 1) + k - 1], device_id=(px, py, pc), device_id_type=MESH_ID))
    return local, remote


def _comm_scratch(n_items):
    return [pltpu.SemaphoreType.DMA((n_items * (N_DEV - 1),)), pltpu.SemaphoreType.DMA((n_items * (N_DEV - 1),)),
            pltpu.SemaphoreType.DMA((n_items,))]


LINK_BYTES_PER_US = dict(gather_chip=23e3, a2a_chip=23e3, a2a=11.5e3, gather=11.5e3, pass_on=200e3, to_other_core=150e3)
CALL_EXCHANGE_US = 3.0


class _Duty:
    def __init__(self, items, srcs, lands, done):
        self.items, self.srcs, self.lands, self.done = items, srcs, lands, done


def _pcall(body, name, grid, in_specs, out_specs, out_shape, args, scratch_shapes=(), sem=(), duty=None):
    if duty is None:
        return pl.pallas_call(body, name=name, grid=grid, in_specs=list(in_specs), out_specs=out_specs,
                              out_shape=out_shape, scratch_shapes=list(scratch_shapes), compiler_params=_cp(*sem))(*args)
    single = not isinstance(out_shape, (list, tuple))
    o_shape = [out_shape] if single else list(out_shape)
    o_specs = [out_specs] if single else list(out_specs)
    n_in, n_out, n_scr = len(in_specs), len(o_shape), len(scratch_shapes)
    n_src, n_land, n_items = len(duty.srcs), len(duty.lands), len(duty.items)
    a0 = n_in + n_src + n_land

    def wrapped(*refs):
        srcs = refs[n_in:n_in + n_src]
        lands = refs[a0 + n_out:a0 + n_out + n_land]
        core = refs[:n_in] + refs[a0:a0 + n_out] + refs[a0 + n_out + n_land:a0 + n_out + n_land + n_scr]
        sems = refs[a0 + n_out + n_land + n_scr:]
        first = functools.reduce(jnp.logical_and, [pl.program_id(a) == 0 for a in range(len(grid))])
        last = functools.reduce(jnp.logical_and, [pl.program_id(a) == g - 1 for a, g in enumerate(grid)])

        @pl.when(first)
        def _():
            local, remote = _comm_copies(duty.items, srcs, lands, *sems)
            for cp in local + remote:
                cp.start()

        body(*core)

        @pl.when(last)
        def _():
            local, remote = _comm_copies(duty.items, srcs, lands, *sems)
            for cp in remote + local:
                cp.wait()

    hbm = pl.BlockSpec(memory_space=pl.ANY)
    res = pl.pallas_call(
        wrapped, name=name, grid=grid,
        in_specs=list(in_specs) + [hbm] * (n_src + n_land), out_specs=o_specs + [hbm] * n_land,
        out_shape=o_shape + [jax.ShapeDtypeStruct(a.shape, a.dtype) for a in duty.lands],
        input_output_aliases={n_in + n_src + t: n_out + t for t in range(n_land)},
        scratch_shapes=list(scratch_shapes) + _comm_scratch(n_items),
        compiler_params=pltpu.CompilerParams(dimension_semantics=("arbitrary",) * len(grid),
                                             vmem_limit_bytes=VMEM_LIMIT_V7X, has_side_effects=True),
    )(*args, *duty.srcs, *duty.lands)
    duty.done(res[n_out:])
    return res[0] if single else res[:n_out]


def _comm(duty, name):
    n_src, n_land = len(duty.srcs), len(duty.lands)

    def body(*refs):
        local, remote = _comm_copies(duty.items, refs[:n_src], refs[n_src + n_land:n_src + 2 * n_land],
                                     *refs[n_src + 2 * n_land:])
        for cp in local + remote:
            cp.start()
        for cp in remote + local:
            cp.wait()

    hbm = pl.BlockSpec(memory_space=pl.ANY)
    duty.done(pl.pallas_call(
        body, name=name, out_shape=[jax.ShapeDtypeStruct(a.shape, a.dtype) for a in duty.lands],
        in_specs=[hbm] * (n_src + n_land), out_specs=[hbm] * n_land,
        input_output_aliases={n_src + t: t for t in range(n_land)},
        scratch_shapes=_comm_scratch(len(duty.items)), compiler_params=pltpu.CompilerParams(has_side_effects=True),
    )(*duty.srcs, *duty.lands))


class _Transfers:
    def __init__(self, name):
        self.name, self.queue, self.lands, self.flushes, self.groups = name, [], {}, 0, {}

    def add(self, key, item, src=None, uid=None, after=None, group=None, nbytes=0):
        self.queue.append((key, item, src, uid, after, group, nbytes / LINK_BYTES_PER_US[item[0]]))
        if group is not None:
            self.groups[group] = [self.groups.get(group, [0, None])[0] + 1, None]

    def when_done(self, group, fn):
        self.groups[group][1] = fn

    def take_for(self, us):
        count, busy = 0, CALL_EXCHANGE_US
        while count < len(self.queue) and busy + self.queue[count][6] <= us:
            busy += self.queue[count][6]
            count += 1
        return self.take(count) if count else None

    def take(self, count):
        units = []
        while self.queue and len(units) < count:
            after = self.queue[0][4]
            if after is not None and any(u[3] == after for u in units):
                break
            units.append(self.queue.pop(0))
        if not units:
            return None
        keys, srcs, items = [], [], []
        for key, item, src, _, _, _, _ in units:
            if key not in keys:
                keys.append(key)
            if src is not None and not any(src is s for s in srcs):
                srcs.append(src)
            si = [i for i, s in enumerate(srcs) if s is src][0] if src is not None else -1
            items.append(tuple(item) + (si, keys.index(key)))

        def done(new_lands):
            for key, arr in zip(keys, new_lands):
                self.lands[key] = arr
            for u in units:
                if u[5] is not None:
                    self.groups[u[5]][0] -= 1
                    if self.groups[u[5]][0] == 0:
                        self.groups[u[5]][1]()

        return _Duty(items, srcs, [self.lands[k] for k in keys], done)

    def drain(self, upto=None):
        count = upto
        while self.queue if upto is None else count > 0:
            duty = self.take(len(self.queue) if upto is None else count)
            count = None if upto is None else count - len(duty.items)
            self.flushes += 1
            _comm(duty, "%s_flush%d" % (self.name, self.flushes))

    def get(self, key):
        pending = [i for i, u in enumerate(self.queue) if u[0] == key]
        if pending:
            self.drain(pending[-1] + 1)
        return self.lands[key]


def _matmul(a, b, mode, name, out_dtype=F32, tm=1024, tn=512, tk=None, add=None, duty=None):
    halves = a.ndim == 3
    if mode == "tn":
        K, M = a.shape[-2], a.shape[-1] * (2 if halves else 1)
        N = b.shape[1]
    else:
        M, K = a.shape[-2], a.shape[-1] * (2 if halves else 1)
        N = b.shape[0] if mode == "nt" else b.shape[1]
    tm, tn = min(tm, M), min(tn, N)
    tk = K if tk is None else tk
    nk = K // tk
    assert M % tm == 0 and N % tn == 0 and K % tk == 0, (name, M, N, K, tm, tn, tk)
    dims = {"nn": NN, "nt": NT, "tn": TN}[mode]
    has_add = add is not None

    def body(*refs):
        if has_add:
            a_ref, b_ref, add_ref, o_ref, acc_ref = refs
        else:
            a_ref, b_ref, o_ref, acc_ref = refs
        k = pl.program_id(2)
        part = _dot(a_ref[...], b_ref[...], dims)

        @pl.when(k == 0)
        def _():
            acc_ref[...] = part

        @pl.when(k > 0)
        def _():
            acc_ref[...] += part

        @pl.when(k == nk - 1)
        def _():
            r = acc_ref[...]
            if has_add:
                r = r + add_ref[...]
            o_ref[...] = r.astype(o_ref.dtype)

    if halves and mode == "tn":
        per = a.shape[-1] // tm
        a_spec = pl.BlockSpec((None, tk, tm), lambda i, j, k: (i // per, k, i % per))
    elif halves:
        per = a.shape[-1] // tk
        a_spec = pl.BlockSpec((None, tm, tk), lambda i, j, k: (k // per, i, k % per))
    elif mode == "tn":
        a_spec = pl.BlockSpec((tk, tm), lambda i, j, k: (k, i))
    else:
        a_spec = pl.BlockSpec((tm, tk), lambda i, j, k: (i, k))
    if mode == "nt":
        b_spec = pl.BlockSpec((tn, tk), lambda i, j, k: (j, k))
    else:
        b_spec = pl.BlockSpec((tk, tn), lambda i, j, k: (k, j))
    o_spec = pl.BlockSpec((tm, tn), lambda i, j, k: (i, j))
    in_specs = [a_spec, b_spec] + ([o_spec] if has_add else [])
    args = (a, b) + ((add,) if has_add else ())
    return _pcall(
        body, name=name, grid=(M // tm, N // tn, nk),
        out_shape=jax.ShapeDtypeStruct((M, N), out_dtype),
        in_specs=in_specs, out_specs=o_spec,
        scratch_shapes=[pltpu.VMEM((tm, tn), F32)],
        sem=("parallel", "parallel", "arbitrary"), args=args, duty=duty)


def _ada_fwd(c_all, w_ada, b_loc):
    n = w_ada.shape[2]

    def body(c_ref, w_ref, b_ref, o_ref):
        c = c_ref[...]
        o_ref[0] = _dot(c * _sigmoid(c), w_ref[0], NN) + b_ref[0]

    return pl.pallas_call(
        body, name="ada_fwd", grid=(DEPTH,),
        out_shape=jax.ShapeDtypeStruct((DEPTH, N_DEV, n), F32),
        in_specs=[pl.BlockSpec((N_DEV, D), lambda l: (0, 0)),
                  pl.BlockSpec((1, D, n), lambda l: (l, 0, 0)),
                  pl.BlockSpec((1, 1, n), lambda l: (l, 0, 0))],
        out_specs=pl.BlockSpec((1, N_DEV, n), lambda l: (l, 0, 0)),
        compiler_params=_cp("parallel"),
    )(c_all, w_ada, b_loc)


def _ada_bwd(c_all, dmod_all):
    n = dmod_all.shape[2]

    def body(c_ref, d_ref, o_ref):
        c = c_ref[...]
        o_ref[0] = _dot(c * _sigmoid(c), d_ref[0], TN)

    return pl.pallas_call(
        body, name="ada_bwd", grid=(DEPTH,),
        out_shape=jax.ShapeDtypeStruct((DEPTH, D, n), F32),
        in_specs=[pl.BlockSpec((N_DEV, D), lambda l: (0, 0)),
                  pl.BlockSpec((1, N_DEV, n), lambda l: (l, 0, 0))],
        out_specs=pl.BlockSpec((1, D, n), lambda l: (l, 0, 0)),
        compiler_params=_cp("parallel"),
    )(c_all, dmod_all)


ROW_TILE = 256


def _row_spec(w=D, col=0):
    return pl.BlockSpec((ROW_TILE, w), lambda i: (i, col))


def _vec_spec(w=D):
    return pl.BlockSpec((1, w), lambda i: (0, 0))


def _norm_fwd(x, g, scale, shift, name, m=None, gate=None, duty=None):
    has_res = m is not None

    def body(*refs):
        if has_res:
            x_ref, m_ref, gate_ref, g_ref, sc_ref, sh_ref, xo_ref, h_ref = refs
            xv = x_ref[...] + gate_ref[...] * m_ref[...]
            xo_ref[...] = xv
        else:
            x_ref, g_ref, sc_ref, sh_ref, h_ref = refs
            xv = x_ref[...]
        r = lax.rsqrt(_mean1(xv * xv) + EPS)
        h_ref[...] = ((xv * r * g_ref[...]) * (1.0 + sc_ref[...]) + sh_ref[...]).astype(BF16)

    ins = [x] + ([m, gate] if has_res else []) + [g, scale, shift]
    in_specs = [_row_spec()] + ([_row_spec(), _vec_spec()] if has_res else []) + [_vec_spec()] * 3
    out_shape = [jax.ShapeDtypeStruct((T, D), BF16)]
    out_specs = [_row_spec()]
    if has_res:
        out_shape = [jax.ShapeDtypeStruct((T, D), F32)] + out_shape
        out_specs = [_row_spec()] + out_specs
    out = _pcall(body, name=name, grid=(T // ROW_TILE,), out_shape=out_shape, in_specs=in_specs,
                 out_specs=out_specs, sem=("parallel",), args=ins, duty=duty)
    return out if has_res else out[0]


def _norm_bwd(x, dh, dres, g, scale, shift, name, duty=None):
    def body(x_ref, dh_ref, dres_ref, g_ref, sc_ref, sh_ref, dx_ref, st_ref):
        xv, dh_v, gv = x_ref[...], dh_ref[...], g_ref[...]
        r = lax.rsqrt(_mean1(xv * xv) + EPS)
        n = xv * r
        dy = dh_v * (1.0 + sc_ref[...])
        dn = dy * gv
        dx_ref[...] = r * (dn - n * _mean1(dn * n)) + dres_ref[...]

        @pl.when(pl.program_id(0) == 0)
        def _():
            st_ref[...] = jnp.zeros_like(st_ref)

        st_ref[0:1, :] += _sum0(dy * n)
        st_ref[1:2, :] += _sum0(dh_v * (n * gv))
        st_ref[2:3, :] += _sum0(dh_v)

    return _pcall(
        body, name=name, grid=(T // ROW_TILE,),
        out_shape=[jax.ShapeDtypeStruct((T, D), F32), jax.ShapeDtypeStruct((8, D), F32)],
        in_specs=[_row_spec(), _row_spec(), _row_spec(), _vec_spec(), _vec_spec(), _vec_spec()],
        out_specs=[_row_spec(), pl.BlockSpec((8, D), lambda i: (0, 0))],
        sem=("arbitrary",), args=(x, dh, dres, g, scale, shift), duty=duty)


def _axpy(x, m, gate, name, duty=None):
    def body(x_ref, m_ref, gate_ref, o_ref):
        o_ref[...] = x_ref[...] + gate_ref[...] * m_ref[...]

    return _pcall(
        body, name=name, grid=(T // ROW_TILE,), out_shape=jax.ShapeDtypeStruct((T, D), F32),
        in_specs=[_row_spec(), _row_spec(), _vec_spec()], out_specs=_row_spec(),
        sem=("parallel",), args=(x, m, gate), duty=duty)


def _gate_bwd(dx, m, gate, name, duty=None):
    def body(dx_ref, m_ref, gate_ref, dm_ref, st_ref):
        dxv = dx_ref[...]
        dm_ref[...] = (gate_ref[...] * dxv).astype(BF16)

        @pl.when(pl.program_id(0) == 0)
        def _():
            st_ref[...] = jnp.zeros_like(st_ref)

        st_ref[0:1, :] += _sum0(dxv * m_ref[...])

    return _pcall(
        body, name=name, grid=(T // ROW_TILE,),
        out_shape=[jax.ShapeDtypeStruct((T, D), BF16), jax.ShapeDtypeStruct((8, D), F32)],
        in_specs=[_row_spec(), _row_spec(), _vec_spec()],
        out_specs=[_row_spec(), pl.BlockSpec((8, D), lambda i: (0, 0))],
        sem=("arbitrary",), args=(dx, m, gate), duty=duty)


def _loss_fwd_bwd(y, target):
    def body(y_ref, t_ref, l_ref, d_ref):
        e = y_ref[...] - t_ref[...]
        d_ref[...] = e * (1.0 / D)

        @pl.when(pl.program_id(0) == 0)
        def _():
            l_ref[...] = jnp.zeros_like(l_ref)

        l_ref[...] += jnp.sum(_sum0(e * e), axis=1, keepdims=True) * (0.5 / D)

    return pl.pallas_call(
        body, name="loss", grid=(T // ROW_TILE,),
        out_shape=[jax.ShapeDtypeStruct((8, 128), F32), jax.ShapeDtypeStruct((T, D), F32)],
        in_specs=[_row_spec(), _row_spec()],
        out_specs=[pl.BlockSpec((8, 128), lambda i: (0, 0)), _row_spec()],
        compiler_params=_cp("arbitrary"))(y, target)


def _rope_tables():
    half = HD // 2
    inv_freq = 10000.0 ** (-jnp.arange(half, dtype=F32) / half)
    ang = jnp.arange(T, dtype=F32)[:, None] * inv_freq[None, :]
    cos, sin = jnp.cos(ang), jnp.sin(ang)
    return jnp.concatenate([cos, cos], axis=1), jnp.concatenate([-sin, sin], axis=1)


def _rope_fwd(p, cosf, sinf, duty=None):
    def body(p_ref, c_ref, s_ref, o_ref):
        cv, sv = c_ref[...], s_ref[...]
        for j in range(2 * N_HEADS):
            xv = p_ref[:, j * HD:(j + 1) * HD]
            rot = xv * cv + pltpu.roll(xv, HD // 2, 1) * sv
            if j >= N_HEADS:
                rot = rot * (HD ** -0.5)
            o_ref[:, j * HD:(j + 1) * HD] = rot.astype(BF16)

    return _pcall(
        body, name="rope_fwd", grid=(T // ROW_TILE,),
        out_shape=jax.ShapeDtypeStruct((T, 2 * BW), BF16),
        in_specs=[_row_spec(2 * BW), _row_spec(HD), _row_spec(HD)], out_specs=_row_spec(2 * BW),
        sem=("parallel",), args=(p, cosf, sinf), duty=duty)


def _rope_bwd(dq, dk, cosf, sinf, duty=None):
    def body(dq_ref, dk_ref, c_ref, s_ref, o_ref):
        cv, sv = c_ref[...], s_ref[...]
        for j in range(2 * N_HEADS):
            h = j % N_HEADS
            d = dq_ref[:, h * HD:(h + 1) * HD] if j < N_HEADS else dk_ref[:, h * HD:(h + 1) * HD] * (HD ** -0.5)
            o_ref[:, j * HD:(j + 1) * HD] = d * cv + pltpu.roll(d * sv, HD // 2, 1)

    return _pcall(
        body, name="rope_bwd", grid=(T // ROW_TILE,),
        out_shape=jax.ShapeDtypeStruct((T, 2 * BW), F32),
        in_specs=[_row_spec(BW), _row_spec(BW), _row_spec(HD), _row_spec(HD)], out_specs=_row_spec(2 * BW),
        sem=("parallel",), args=(dq, dk, cosf, sinf), duty=duty)


TQ = 256
V_RET_BLK = 8


def _ret_logg():
    lg = jnp.log1p(-jnp.exp2(-5.0 - jnp.arange(N_HEADS, dtype=F32)))
    return jnp.broadcast_to(lg[:, None, None], (N_HEADS, 1, 128))


def _block_iotas(i, kl):
    rows = lax.broadcasted_iota(jnp.int32, (TQ, kl), 0) + i * TQ
    cols = lax.broadcasted_iota(jnp.int32, (TQ, kl), 1)
    return rows, cols


def _ret_weight(lg_ref, i, kl):
    rows, cols = _block_iotas(i, kl)
    dist = jnp.abs(rows - cols).astype(F32)
    w = jnp.exp(dist * lg_ref[0][:, 0:1])
    return jnp.where((cols >> 6) <= (rows >> 6), w, 0.0)


def _per_query_block(i, fn):
    for n in range(1, T // TQ + 1):
        pl.when(i == n - 1)(functools.partial(fn, n * TQ))


def _ret_specs():
    q_spec = pl.BlockSpec((TQ, HD), lambda h, i: (i, h))
    k_spec = pl.BlockSpec((T, HD), lambda h, i: (0, N_HEADS + h))
    v_spec = pl.BlockSpec((T, HD), lambda h, i: (0, V_RET_BLK + h))
    lg_spec = pl.BlockSpec((1, 1, 128), lambda h, i: (h, 0, 0))
    return q_spec, k_spec, v_spec, lg_spec


def _ret_fwd(qk, p, logg, duty=None):
    def body(q_ref, k_ref, v_ref, lg_ref, o_ref):
        i = pl.program_id(1)

        def visible(kl):
            s = _dot(q_ref[...], k_ref[0:kl, :], NT) * _ret_weight(lg_ref, i, kl)
            o_ref[...] = _dot(s, v_ref[0:kl, :], NN)

        _per_query_block(i, visible)

    q_spec, k_spec, v_spec, lg_spec = _ret_specs()
    return _pcall(
        body, name="ret_fwd", grid=(N_HEADS, T // TQ),
        out_shape=jax.ShapeDtypeStruct((T, BW), F32),
        in_specs=[q_spec, k_spec, v_spec, lg_spec], out_specs=q_spec,
        sem=("parallel", "parallel"), args=(qk, qk, p, logg), duty=duty)


def _ret_bwd(qk, p, logg, do, duty=None):
    def body(q_ref, k_ref, v_ref, lg_ref, do_ref, dq_ref, dk_ref, dv_ref):
        i = pl.program_id(1)
        q, dov = q_ref[...], do_ref[...]

        @pl.when(i == 0)
        def _():
            dk_ref[...] = jnp.zeros_like(dk_ref)
            dv_ref[...] = jnp.zeros_like(dv_ref)

        def visible(kl):
            w = _ret_weight(lg_ref, i, kl)
            k = k_ref[0:kl, :]
            s = _dot(q, k, NT) * w
            ds = _dot(dov, v_ref[0:kl, :], NT) * w
            dk_ref[0:kl, :] += _dot(ds, q, TN)
            dv_ref[0:kl, :] += _dot(s, dov, TN)
            dq_ref[...] = _dot(ds, k, NN)

        _per_query_block(i, visible)

    q_spec, k_spec, v_spec, lg_spec = _ret_specs()
    acc_spec = pl.BlockSpec((T, HD), lambda h, i: (0, h))
    sh = jax.ShapeDtypeStruct((T, BW), F32)
    return _pcall(
        body, name="ret_bwd", grid=(N_HEADS, T // TQ),
        out_shape=[sh, sh, sh],
        in_specs=[q_spec, k_spec, v_spec, lg_spec, q_spec], out_specs=[q_spec, acc_spec, acc_spec],
        sem=("parallel", "arbitrary"), args=(qk, qk, p, logg, do), duty=duty)


def _post_norm(xv, gv, centered):
    if centered:
        xv = xv - _mean1(xv)
    r = lax.rsqrt(_mean1(xv * xv) + EPS)
    return xv * r, r


def _branch_post_fwd(raw, p, g, gate_blk, centered, name, duty=None):
    def body(raw_ref, z_ref, g_ref, o_ref):
        for h in range(N_HEADS):
            sl = slice(h * HD, (h + 1) * HD)
            gv = g_ref[:, sl] if centered else g_ref[...]
            xh, _ = _post_norm(raw_ref[:, sl], gv, centered)
            z = z_ref[:, sl]
            o_ref[:, sl] = (z * _sigmoid(z) * (xh * gv)).astype(BF16)

    return _pcall(
        body, name=name, grid=(T // ROW_TILE,),
        out_shape=jax.ShapeDtypeStruct((T, BW), BF16),
        in_specs=[_row_spec(BW), _row_spec(BW, gate_blk), _vec_spec(BW if centered else HD)],
        out_specs=_row_spec(BW), sem=("parallel",), args=(raw, p, g), duty=duty)


def _branch_post_bwd(raw, p, g, dout, gate_blk, centered, name, duty=None):
    gw = BW if centered else HD

    def body(raw_ref, z_ref, g_ref, do_ref, dr_ref, dz_ref, dg_ref):
        @pl.when(pl.program_id(0) == 0)
        def _():
            dg_ref[...] = jnp.zeros_like(dg_ref)

        for h in range(N_HEADS):
            sl = slice(h * HD, (h + 1) * HD)
            gsl = sl if centered else slice(0, HD)
            gv, z, dov = g_ref[:, gsl], z_ref[:, sl], do_ref[:, sl]
            xh, r = _post_norm(raw_ref[:, sl], gv, centered)
            sg = _sigmoid(z)
            dyn = dov * (z * sg)
            dz_ref[:, sl] = dov * (xh * gv) * (sg * (1.0 + z * (1.0 - sg)))
            dxh = dyn * gv
            t = dxh - xh * _mean1(dxh * xh)
            if centered:
                t = t - _mean1(dxh)
            dr_ref[:, sl] = r * t
            dg_ref[0:1, gsl] += _sum0(dyn * xh)

    return _pcall(
        body, name=name, grid=(T // ROW_TILE,),
        out_shape=[jax.ShapeDtypeStruct((T, BW), F32), jax.ShapeDtypeStruct((T, BW), F32),
                   jax.ShapeDtypeStruct((8, gw), F32)],
        in_specs=[_row_spec(BW), _row_spec(BW, gate_blk), _vec_spec(gw), _row_spec(BW)],
        out_specs=[_row_spec(BW), _row_spec(BW), pl.BlockSpec((8, gw), lambda i: (0, 0))],
        sem=("arbitrary",), args=(raw, p, g, dout), duty=duty)


GLA_ROWS = 256
GLA_CPB = GLA_ROWS // CHUNK
GLA_DK = 64
GLA_W = N_HEADS * GLA_DK
GQ_BLK, GK_BLK, GV_BLK, GG_BLK, TAIL_BLK = 8, 9, 5, 6, 40
RG_BLK = 3


def _gla_chunk_common(tl, w2, bv, kv):
    pre = _dot(tl, w2, NN) + bv
    la = _log_sigmoid(pre) * (1.0 / 16.0)
    bc = _exact_dot(_tri(CHUNK, False), la)
    be = bc[CHUNK - 1:CHUNK, :]
    w = jnp.exp(be - bc)
    return pre, w, jnp.exp(be), kv * w


def _head_masks():
    lane = lax.broadcasted_iota(jnp.int32, (1, GLA_W), 1)
    return [jnp.where((lane // GLA_DK) == h, 1.0, 0.0) for h in range(N_HEADS)]


def _gla_fwd(p, w2pad, b, duty=None):
    nb = T // GLA_ROWS

    def body(q_ref, k_ref, v_ref, t_ref, w2_ref, b_ref, o_ref, st_ref, s_acc):
        @pl.when(pl.program_id(0) == 0)
        def _():
            s_acc[...] = jnp.zeros_like(s_acc)

        masks = _head_masks()
        for c in range(GLA_CPB):
            rows = slice(c * CHUNK, (c + 1) * CHUNK)
            _, _, a, kd = _gla_chunk_common(t_ref[rows, :], w2_ref[...], b_ref[...], k_ref[rows, :])
            q = q_ref[rows, :] * (GLA_DK ** -0.5)
            kv = None
            for h in range(N_HEADS):
                t = _dot(v_ref[rows, h * HD:(h + 1) * HD], kd * masks[h], TN)
                kv = t if kv is None else kv + t
            s_new = s_acc[...] * a + kv
            s_acc[...] = s_new
            st_ref[c] = s_new
            for h in range(N_HEADS):
                o_ref[rows, h * HD:(h + 1) * HD] = _dot(q * masks[h], s_new, NT)

    return _pcall(
        body, name="gla_fwd", grid=(nb,),
        out_shape=[jax.ShapeDtypeStruct((T, BW), F32), jax.ShapeDtypeStruct((T // CHUNK, HD, GLA_W), F32)],
        in_specs=[pl.BlockSpec((GLA_ROWS, GLA_W), lambda i: (i, GQ_BLK)),
                  pl.BlockSpec((GLA_ROWS, GLA_W), lambda i: (i, GK_BLK)),
                  pl.BlockSpec((GLA_ROWS, BW), lambda i: (i, GV_BLK)),
                  pl.BlockSpec((GLA_ROWS, 128), lambda i: (i, TAIL_BLK)),
                  pl.BlockSpec((128, GLA_W), lambda i: (0, 0)),
                  pl.BlockSpec((1, GLA_W), lambda i: (0, 0))],
        out_specs=[pl.BlockSpec((GLA_ROWS, BW), lambda i: (i, 0)),
                   pl.BlockSpec((GLA_CPB, HD, GLA_W), lambda i: (i, 0, 0))],
        scratch_shapes=[pltpu.VMEM((HD, GLA_W), F32)],
        sem=("arbitrary",), args=(p, p, p, p, w2pad, b), duty=duty)


def _gla_bwd(p, w2pad, b, states, do, duty=None):
    nb = T // GLA_ROWS

    def body(q_ref, k_ref, v_ref, t_ref, w2_ref, b_ref, st_ref, prev_ref, do_ref,
             dq_ref, dk_ref, dv_ref, dt_ref, dw2_ref, db_ref, ds_acc):
        step = pl.program_id(0)

        @pl.when(step == 0)
        def _():
            ds_acc[...] = jnp.zeros_like(ds_acc)
            dw2_ref[...] = jnp.zeros_like(dw2_ref)
            db_ref[...] = jnp.zeros_like(db_ref)

        masks = _head_masks()
        up = _tri(CHUNK, True)
        has_prev = jnp.where(step == nb - 1, 0.0, 1.0)
        for c in reversed(range(GLA_CPB)):
            rows = slice(c * CHUNK, (c + 1) * CHUNK)
            tl, w2, k = t_ref[rows, :], w2_ref[...], k_ref[rows, :]
            pre, w, a, kd = _gla_chunk_common(tl, w2, b_ref[...], k)
            q = q_ref[rows, :] * (GLA_DK ** -0.5)
            s_n = st_ref[c]
            s_prev = st_ref[c - 1] if c > 0 else prev_ref[0] * has_prev
            ds = ds_acc[...]
            dos = [do_ref[rows, h * HD:(h + 1) * HD] for h in range(N_HEADS)]
            for h in range(N_HEADS):
                ds = ds + _dot(dos[h], q * masks[h], TN)
            dqp = jnp.zeros((CHUNK, GLA_W), F32)
            dkd = jnp.zeros((CHUNK, GLA_W), F32)
            for h in range(N_HEADS):
                dqp = dqp + masks[h] * _dot(dos[h], s_n, NN)
                dkd = dkd + masks[h] * _dot(v_ref[rows, h * HD:(h + 1) * HD], ds, NN)
                dv_ref[rows, h * HD:(h + 1) * HD] = _dot(kd * masks[h], ds, NT)
            dq_ref[rows, :] = dqp * (GLA_DK ** -0.5)
            dk_ref[rows, :] = dkd * w
            e = dkd * k * w
            dbe = _sum0(e) + _sum0(ds * s_prev) * a
            dla = dbe - _exact_dot(up, e)
            dpre = dla * (1.0 / 16.0) * _sigmoid(-pre)
            db_ref[0:1, :] += _sum0(dpre)
            dw2_ref[...] += _dot(tl, dpre, TN)
            dt_ref[rows, :] = _dot(dpre, w2, NT)
            ds_acc[...] = ds * a

    rev = lambda i: nb - 1 - i
    sh = lambda w: jax.ShapeDtypeStruct((T, w), F32)
    return _pcall(
        body, name="gla_bwd", grid=(nb,),
        out_shape=[sh(GLA_W), sh(GLA_W), sh(BW), sh(128), jax.ShapeDtypeStruct((128, GLA_W), F32),
                   jax.ShapeDtypeStruct((8, GLA_W), F32)],
        in_specs=[pl.BlockSpec((GLA_ROWS, GLA_W), lambda i: (rev(i), GQ_BLK)),
                  pl.BlockSpec((GLA_ROWS, GLA_W), lambda i: (rev(i), GK_BLK)),
                  pl.BlockSpec((GLA_ROWS, BW), lambda i: (rev(i), GV_BLK)),
                  pl.BlockSpec((GLA_ROWS, 128), lambda i: (rev(i), TAIL_BLK)),
                  pl.BlockSpec((128, GLA_W), lambda i: (0, 0)),
                  pl.BlockSpec((1, GLA_W), lambda i: (0, 0)),
                  pl.BlockSpec((GLA_CPB, HD, GLA_W), lambda i: (rev(i), 0, 0)),
                  pl.BlockSpec((1, HD, GLA_W), lambda i: (jnp.maximum(rev(i) * GLA_CPB - 1, 0), 0, 0)),
                  pl.BlockSpec((GLA_ROWS, BW), lambda i: (rev(i), 0))],
        out_specs=[pl.BlockSpec((GLA_ROWS, GLA_W), lambda i: (rev(i), 0)),
                   pl.BlockSpec((GLA_ROWS, GLA_W), lambda i: (rev(i), 0)),
                   pl.BlockSpec((GLA_ROWS, BW), lambda i: (rev(i), 0)),
                   pl.BlockSpec((GLA_ROWS, 128), lambda i: (rev(i), 0)),
                   pl.BlockSpec((128, GLA_W), lambda i: (0, 0)),
                   pl.BlockSpec((8, GLA_W), lambda i: (0, 0))],
        scratch_shapes=[pltpu.VMEM((HD, GLA_W), F32)],
        sem=("arbitrary",), args=(p, p, p, p, w2pad, b, states, states, do), duty=duty)


FQ_BLK, FK_BLK = 7, 8
V_FOX_BLK = 36


def _fox_prep_fwd(p, qg, kg, btail, duty=None):
    def body(q_ref, k_ref, t_ref, qg_ref, kg_ref, bt_ref, o_ref, cum_ref, carry):
        @pl.when(pl.program_id(0) == 0)
        def _():
            carry[...] = jnp.zeros_like(carry)

        for src, gr, off in ((q_ref, qg_ref, 0), (k_ref, kg_ref, BW)):
            for h in range(N_HEADS):
                xv = src[:, h * HD:(h + 1) * HD]
                r = lax.rsqrt(_mean1(xv * xv) + EPS)
                o_ref[:, off + h * HD:off + (h + 1) * HD] = (xv * r * gr[...]).astype(BF16)
        logf = _log_sigmoid(t_ref[...] + bt_ref[...])
        cum = _exact_dot(_tri(ROW_TILE, False), logf) + carry[...]
        cum_ref[...] = cum
        carry[...] = cum[ROW_TILE - 1:ROW_TILE, :]

    return _pcall(
        body, name="fox_prep_fwd", grid=(T // ROW_TILE,),
        out_shape=[jax.ShapeDtypeStruct((T, 2 * BW), BF16), jax.ShapeDtypeStruct((T, 128), F32)],
        in_specs=[_row_spec(BW, FQ_BLK), _row_spec(BW, FK_BLK), _row_spec(128, TAIL_BLK),
                  _vec_spec(HD), _vec_spec(HD), _vec_spec(128)],
        out_specs=[_row_spec(2 * BW), _row_spec(128)],
        scratch_shapes=[pltpu.VMEM((1, 128), F32)],
        sem=("arbitrary",), args=(p, p, p, qg, kg, btail), duty=duty)


def _fox_prep_bwd(p, qg, kg, btail, dqn, dkn, dcum, duty=None):
    nt = T // ROW_TILE

    def body(q_ref, k_ref, t_ref, qg_ref, kg_ref, bt_ref, dq_ref, dk_ref, dc_ref, o_ref, dt_ref, st_ref, carry):
        @pl.when(pl.program_id(0) == 0)
        def _():
            carry[...] = jnp.zeros_like(carry)
            st_ref[...] = jnp.zeros_like(st_ref)

        for row, (src, gr, dsrc, off) in enumerate(((q_ref, qg_ref, dq_ref, 0), (k_ref, kg_ref, dk_ref, BW))):
            for h in range(N_HEADS):
                xv = src[:, h * HD:(h + 1) * HD]
                dy = dsrc[:, h * HD:(h + 1) * HD]
                r = lax.rsqrt(_mean1(xv * xv) + EPS)
                n = xv * r
                dn = dy * gr[...]
                o_ref[:, off + h * HD:off + (h + 1) * HD] = r * (dn - n * _mean1(dn * n))
                st_ref[row:row + 1, :] += _sum0(dy * n)
        z = t_ref[...] + bt_ref[...]
        dlogf = _exact_dot(_tri(ROW_TILE, True), dc_ref[...]) + carry[...]
        carry[...] = dlogf[0:1, :]
        lane = lax.broadcasted_iota(jnp.int32, (1, 128), 1)
        keep = (lane >= FF_LANE0) & (lane < FF_LANE0 + N_HEADS)
        dz = jnp.where(keep, dlogf * _sigmoid(-z), 0.0)
        dt_ref[...] = dz
        st_ref[2:3, :] += _sum0(dz)

    rs = lambda w, col=0: pl.BlockSpec((ROW_TILE, w), lambda i: (nt - 1 - i, col))
    return _pcall(
        body, name="fox_prep_bwd", grid=(nt,),
        out_shape=[jax.ShapeDtypeStruct((T, 2 * BW), F32), jax.ShapeDtypeStruct((T, 128), F32),
                   jax.ShapeDtypeStruct((8, 128), F32)],
        in_specs=[rs(BW, FQ_BLK), rs(BW, FK_BLK), rs(128, TAIL_BLK), _vec_spec(HD), _vec_spec(HD), _vec_spec(128),
                  rs(BW), rs(BW), rs(128)],
        out_specs=[rs(2 * BW), rs(128), pl.BlockSpec((8, 128), lambda i: (0, 0))],
        scratch_shapes=[pltpu.VMEM((1, 128), F32)],
        sem=("arbitrary",), args=(p, p, p, qg, kg, btail, dqn, dkn, dcum), duty=duty)


def _fox_logits(q_ref, k_ref, cc_ref, cr_ref, i, kl):
    rows, cols = _block_iotas(i, kl)
    s = _dot(q_ref[...], k_ref[0:kl, :], NT) * (HD ** -0.5) + cc_ref[0] - cr_ref[0, :, 0:kl]
    return jnp.where(cols <= rows, s, -1e30)


def _fox_specs():
    q_spec = pl.BlockSpec((TQ, HD), lambda h, i: (i, h))
    k_spec = pl.BlockSpec((T, HD), lambda h, i: (0, N_HEADS + h))
    v_spec = pl.BlockSpec((T, HD), lambda h, i: (0, V_FOX_BLK + h))
    col_spec = pl.BlockSpec((1, TQ, 1), lambda h, i: (h, i, 0))
    row_spec = pl.BlockSpec((1, 1, T), lambda h, i: (h, 0, 0))
    return q_spec, k_spec, v_spec, col_spec, row_spec


def _fox_fwd(qkn, p, cumcol, cumrow, duty=None):
    def body(q_ref, k_ref, v_ref, cc_ref, cr_ref, o_ref, lse_ref):
        i = pl.program_id(1)

        def visible(kl):
            s = _fox_logits(q_ref, k_ref, cc_ref, cr_ref, i, kl)
            m = jnp.max(s, axis=-1, keepdims=True)
            e = jnp.exp(s - m)
            l = jnp.sum(e, axis=-1, keepdims=True)
            o_ref[...] = _dot(e / l, v_ref[0:kl, :], NN)
            lse_ref[0] = m + jnp.log(l)

        _per_query_block(i, visible)

    q_spec, k_spec, v_spec, col_spec, row_spec = _fox_specs()
    return _pcall(
        body, name="fox_fwd", grid=(N_HEADS, T // TQ),
        out_shape=[jax.ShapeDtypeStruct((T, BW), F32), jax.ShapeDtypeStruct((N_HEADS, T, 1), F32)],
        in_specs=[q_spec, k_spec, v_spec, col_spec, row_spec], out_specs=[q_spec, col_spec],
        sem=("parallel", "parallel"), args=(qkn, qkn, p, cumcol, cumrow), duty=duty)


def _fox_bwd(qkn, p, cumcol, cumrow, lse, o, do, duty=None):
    def body(q_ref, k_ref, v_ref, cc_ref, cr_ref, lse_ref, o_ref, do_ref, dq_ref, dk_ref, dv_ref, dr_ref, dc_ref):
        i = pl.program_id(1)
        @pl.when(i == 0)
        def _():
            dk_ref[...] = jnp.zeros_like(dk_ref)
            dv_ref[...] = jnp.zeros_like(dv_ref)
            dc_ref[...] = jnp.zeros_like(dc_ref)

        def visible(kl):
            q, dov = q_ref[...], do_ref[...]
            pm = jnp.exp(_fox_logits(q_ref, k_ref, cc_ref, cr_ref, i, kl) - lse_ref[0])
            delta = jnp.sum(o_ref[...] * dov, axis=-1, keepdims=True)
            ds = pm * (_dot(dov, v_ref[0:kl, :], NT) - delta)
            dq_ref[...] = _dot(ds, k_ref[0:kl, :], NN) * (HD ** -0.5)
            dr_ref[0] = jnp.sum(ds, axis=-1, keepdims=True)
            dk_ref[0:kl, :] += _dot(ds, q, TN) * (HD ** -0.5)
            dv_ref[0:kl, :] += _dot(pm, dov, TN)
            dc_ref[0, :, 0:kl] += _sum0(ds)

        _per_query_block(i, visible)

    q_spec, k_spec, v_spec, col_spec, row_spec = _fox_specs()
    acc_spec = pl.BlockSpec((T, HD), lambda h, i: (0, h))
    sh = jax.ShapeDtypeStruct((T, BW), F32)
    return _pcall(
        body, name="fox_bwd", grid=(N_HEADS, T // TQ),
        out_shape=[sh, sh, sh, jax.ShapeDtypeStruct((N_HEADS, T, 1), F32), jax.ShapeDtypeStruct((N_HEADS, 1, T), F32)],
        in_specs=[q_spec, k_spec, v_spec, col_spec, row_spec, col_spec, q_spec, q_spec],
        out_specs=[q_spec, acc_spec, acc_spec, col_spec, row_spec],
        sem=("parallel", "arbitrary"), args=(qkn, qkn, p, cumcol, cumrow, lse, o, do), duty=duty)


def _mix_fwd(gpre, b_mg, y0, y1, y2, duty=None):
    def body(g_ref, b_ref, y0_ref, y1_ref, y2_ref, o_ref):
        acc = None
        for n, y_ref in enumerate((y0_ref, y1_ref, y2_ref)):
            sl = slice(n * D, (n + 1) * D)
            t = _sigmoid(g_ref[:, sl] + b_ref[:, sl]) * y_ref[...]
            acc = t if acc is None else acc + t
        o_ref[...] = acc.astype(BF16)

    return _pcall(
        body, name="mix_fwd", grid=(T // ROW_TILE,), out_shape=jax.ShapeDtypeStruct((T, D), BF16),
        in_specs=[_row_spec(3 * D), _vec_spec(3 * D), _row_spec(), _row_spec(), _row_spec()],
        out_specs=_row_spec(), sem=("parallel",), args=(gpre, b_mg, y0, y1, y2), duty=duty)


def _mix_bwd(gpre, b_mg, y0, y1, y2, dmi, duty=None):
    def body(g_ref, b_ref, y0_ref, y1_ref, y2_ref, d_ref, dy0_ref, dy1_ref, dy2_ref, dg_ref, db_ref):
        @pl.when(pl.program_id(0) == 0)
        def _():
            db_ref[...] = jnp.zeros_like(db_ref)

        dv = d_ref[...]
        for n, (y_ref, dy_ref) in enumerate(((y0_ref, dy0_ref), (y1_ref, dy1_ref), (y2_ref, dy2_ref))):
            sl = slice(n * D, (n + 1) * D)
            sg = _sigmoid(g_ref[:, sl] + b_ref[:, sl])
            dy_ref[...] = (dv * sg).astype(BF16)
            dpre = dv * y_ref[...] * (sg * (1.0 - sg))
            dg_ref[:, sl] = dpre.astype(BF16)
            db_ref[0:1, sl] += _sum0(dpre)

    shb = jax.ShapeDtypeStruct((T, D), BF16)
    return _pcall(
        body, name="mix_bwd", grid=(T // ROW_TILE,),
        out_shape=[shb, shb, shb, jax.ShapeDtypeStruct((T, 3 * D), BF16), jax.ShapeDtypeStruct((8, 3 * D), F32)],
        in_specs=[_row_spec(3 * D), _vec_spec(3 * D), _row_spec(), _row_spec(), _row_spec(), _row_spec()],
        out_specs=[_row_spec(), _row_spec(), _row_spec(), _row_spec(3 * D), pl.BlockSpec((8, 3 * D), lambda i: (0, 0))],
        sem=("arbitrary",), args=(gpre, b_mg, y0, y1, y2, dmi), duty=duty)


FF_COLS = 256
FF_NBLK = D_FF // FF_COLS


def _shift_rows(a, n):
    rows = lax.broadcasted_iota(jnp.int32, a.shape, 0)
    rolled = pltpu.roll(a, n % T, 0)
    return jnp.where((rows >= n) if n > 0 else (rows < T + n), rolled, 0.0)


def _ffn_act_fwd(uu, w_conv, b_conv, duty=None):
    def body(u_ref, g_ref, w_ref, b_ref, o_ref):
        u = u_ref[...]
        w = w_ref[...]
        uc = b_ref[...] + w[0:1, :] * _shift_rows(u, 2) + w[1:2, :] * _shift_rows(u, 1) + w[2:3, :] * u
        o_ref[...] = (uc * _sigmoid(uc) * g_ref[...]).astype(BF16)

    return _pcall(
        body, name="ffn_act_fwd", grid=(FF_NBLK,), out_shape=jax.ShapeDtypeStruct((T, D_FF), BF16),
        in_specs=[pl.BlockSpec((T, FF_COLS), lambda j: (0, j)), pl.BlockSpec((T, FF_COLS), lambda j: (0, FF_NBLK + j)),
                  pl.BlockSpec((3, FF_COLS), lambda j: (0, j)), pl.BlockSpec((1, FF_COLS), lambda j: (0, j))],
        out_specs=pl.BlockSpec((T, FF_COLS), lambda j: (0, j)),
        sem=("parallel",), args=(uu, uu, w_conv, b_conv), duty=duty)


def _ffn_act_bwd(uu, w_conv, b_conv, da, duty=None):
    def body(u_ref, g_ref, w_ref, b_ref, da_ref, d_ref, st_ref):
        u, w, dav = u_ref[...], w_ref[...], da_ref[...]
        u1, u2 = _shift_rows(u, 1), _shift_rows(u, 2)
        uc = b_ref[...] + w[0:1, :] * u2 + w[1:2, :] * u1 + w[2:3, :] * u
        sg = _sigmoid(uc)
        d_ref[1] = (dav * (uc * sg)).astype(BF16)
        duc = dav * g_ref[...] * (sg * (1.0 + uc * (1.0 - sg)))
        du = w[2:3, :] * duc + w[1:2, :] * _shift_rows(duc, -1) + w[0:1, :] * _shift_rows(duc, -2)
        d_ref[0] = du.astype(BF16)
        st_ref[...] = jnp.zeros_like(st_ref)
        st_ref[0:1, :] = _sum0(duc * u2)
        st_ref[1:2, :] = _sum0(duc * u1)
        st_ref[2:3, :] = _sum0(duc * u)
        st_ref[3:4, :] = _sum0(duc)

    cb = lambda rows=T, off=0: pl.BlockSpec((rows, FF_COLS), lambda j: (0, off + j))
    return _pcall(
        body, name="ffn_act_bwd", grid=(FF_NBLK,),
        out_shape=[jax.ShapeDtypeStruct((2, T, D_FF), BF16), jax.ShapeDtypeStruct((8, D_FF), F32)],
        in_specs=[cb(), cb(T, FF_NBLK), cb(3), cb(1), cb()],
        out_specs=[pl.BlockSpec((2, T, FF_COLS), lambda j: (0, 0, j)), cb(8)],
        sem=("parallel",), args=(uu, uu, w_conv, b_conv, da), duty=duty)


def _adamw(g, w, m, v, tr, name):
    partial = g.ndim == 3
    R, C = w.shape
    tr = R if tr is None else tr
    assert R % tr == 0

    def body(g_ref, w_ref, m_ref, v_ref, go_ref, d_ref, mo_ref, vo_ref):
        if partial:
            gv = g_ref[0].astype(F32)
            for j in range(1, g.shape[0]):
                gv = gv + g_ref[j].astype(F32)
        else:
            gv = g_ref[...]
        go_ref[...] = gv
        mn = ADAM_B1 * m_ref[...] + (1.0 - ADAM_B1) * gv
        vn = ADAM_B2 * v_ref[...] + (1.0 - ADAM_B2) * (gv * gv)
        mo_ref[...] = mn
        vo_ref[...] = vn
        m_hat = mn / (1.0 - ADAM_B1 ** ADAM_STEP)
        v_hat = vn / (1.0 - ADAM_B2 ** ADAM_STEP)
        d_ref[...] = -ADAM_LR * (m_hat / (jnp.sqrt(v_hat) + ADAM_EPS) + ADAM_WD * w_ref[...])

    spec = pl.BlockSpec((tr, C), lambda i: (i, 0))
    g_spec = pl.BlockSpec((g.shape[0], tr, C), lambda i: (0, i, 0)) if partial else spec
    sh = jax.ShapeDtypeStruct((R, C), F32)
    return pl.pallas_call(
        body, name=name, grid=(R // tr,), out_shape=[sh, sh, sh, sh],
        in_specs=[g_spec, spec, spec, spec], out_specs=[spec, spec, spec, spec],
        compiler_params=_cp("parallel"))(g, w, m, v)


def _chip_sum(dw, stage, name):
    n_chip, n, C = stage.shape

    def body(d_ref, s_ref, o_ref):
        mc = lax.axis_index("c")
        mine = jnp.where(mc == 0, d_ref[0, 0].astype(F32), d_ref[0, 1].astype(F32))
        o_ref[0] = (mine + s_ref[0].astype(F32)).astype(BF16)

    return pl.pallas_call(
        body, name=name, grid=(n_chip,), out_shape=jax.ShapeDtypeStruct(stage.shape, BF16),
        in_specs=[pl.BlockSpec((1, 2, n, C), lambda p: (p, 0, 0, 0)), pl.BlockSpec((1, n, C), lambda p: (p, 0, 0))],
        out_specs=pl.BlockSpec((1, n, C), lambda p: (p, 0, 0)), compiler_params=_cp("parallel"),
    )(dw.reshape(n_chip, 2, n, C), stage)


def _sum_partials(g, name, tr=None):
    n_part, R, C = g.shape
    tr = R if tr is None else tr
    assert R % tr == 0

    def body(g_ref, o_ref):
        acc = g_ref[0].astype(F32)
        for j in range(1, n_part):
            acc = acc + g_ref[j].astype(F32)
        o_ref[...] = acc

    return pl.pallas_call(
        body, name=name, grid=(R // tr,), out_shape=jax.ShapeDtypeStruct((R, C), F32),
        in_specs=[pl.BlockSpec((n_part, tr, C), lambda i: (0, i, 0))], out_specs=pl.BlockSpec((tr, C), lambda i: (i, 0)),
        compiler_params=_cp("parallel"))(g)


def _permute_in(w):
    pad = jnp.zeros(w.shape[:-1] + (NP - IN_W,), w.dtype)
    return jnp.concatenate([w[..., :3072], w[..., 3088:5136], w[..., 3072:3088], w[..., 5136:5140], pad], axis=-1)


def _unpermute_in(w):
    return jnp.concatenate([w[..., :3072], w[..., 5120:5136], w[..., 3072:5120], w[..., 5136:5140]], axis=-1)


def _flat_pack(arrs):
    flat = jnp.concatenate([a.reshape(-1).astype(F32) for a in arrs])
    n = flat.shape[0]
    rows = -(-n // 1024) * 8
    return jnp.pad(flat, (0, rows * 128 - n)).reshape(rows, 128)


def _flat_unpack(buf, shapes):
    flat = buf.reshape(-1)
    out, off = [], 0
    for s in shapes:
        n = int(np.prod(s))
        out.append(flat[off:off + n].reshape(s))
        off += n
    return out


GRAD_CHUNKS = dict(w_in=(128, 8), w_o=(128, 1), w_down=(352, 2), w_br0=(128, 1), w_br1=(128, 1), w_br2=(128, 1),
                   w_mg=(384, 4), w_up=(704, 11))
GRAD_VIA_CHIP = ("w_in", "w_down", "w_mg", "w_up")


def _send_grad(xfer, layer, k, g):
    if xfer is None:
        return g
    n, parts = GRAD_CHUNKS[k]
    row_bytes = g.shape[1] * 2
    if k not in GRAD_VIA_CHIP:
        for c in range(parts):
            xfer.add(k, ("a2a", 0, n, (layer,), c * (n // parts), n // parts), g, nbytes=n // parts * row_bytes)
        return g
    stage = ("stage", layer, k)
    xfer.lands[stage] = lax.empty((N_DEV // 2, n, g.shape[1]), BF16)
    xfer.add(stage, ("to_other_core", 0, n, (), 0, n), g, group=stage, nbytes=n * row_bytes)

    def both_halves_here():
        chip = _chip_sum(g, xfer.lands[stage], "chip_sum_" + k)
        for c in range(parts):
            xfer.add(k, ("a2a_chip", 0, n, (layer,), c * (n // parts), n // parts), chip, nbytes=n // parts * row_bytes)

    xfer.when_done(stage, both_halves_here)
    return g


def _weight(wl, k):
    return wl[k]() if callable(wl[k]) else wl[k]


def _taker(xfer):
    return (lambda us: None) if xfer is None else xfer.take_for


def _layer_fwd(x0, wl, consts, xfer=None):
    cosf, sinf, logg = consts
    row = lambda a: a.reshape(1, -1)
    take = _taker(xfer)
    h = _norm_fwd(x0, row(wl["norm1_g"]), row(wl["scale1"]), row(wl["shift1"]), "norm1_fwd", duty=take(9))
    p = _matmul(h, _weight(wl, "w_in"), "nn", "in_proj", duty=take(41))
    qk = _rope_fwd(p, cosf, sinf, duty=take(10))
    ret_raw = _ret_fwd(qk, p, logg, duty=take(27))
    br0 = _branch_post_fwd(ret_raw, p, row(wl["ret_norm_g"]), RG_BLK, True, "ret_post_fwd", duty=take(9))
    gla_raw, states = _gla_fwd(p, wl["w2pad"], row(wl["b_gla_a"]), duty=take(26))
    br1 = _branch_post_fwd(gla_raw, p, row(wl["gla_norm_g"]), GG_BLK, False, "gla_post_fwd", duty=take(9))
    qkn, cum = _fox_prep_fwd(p, row(wl["q_norm_g"]), row(wl["k_norm_g"]), row(wl["btail"]), duty=take(10))
    cum4 = cum[:, FF_LANE0:FF_LANE0 + N_HEADS].T
    cumcol, cumrow = cum4.reshape(N_HEADS, T, 1), cum4.reshape(N_HEADS, 1, T)
    fox_o, lse = _fox_fwd(qkn, p, cumcol, cumrow, duty=take(30))
    w_br_t = _weight(wl, "w_br_t")
    ys = [_matmul(b, w_br_t[n], "nt", "br_proj%d" % n) for n, b in enumerate((br0, br1, fox_o))]
    gpre = _matmul(h, _weight(wl, "w_mg_t"), "nt", "gate_proj", duty=take(25))
    mixed_in = _mix_fwd(gpre, row(wl["b_mg"]), *ys, duty=take(21))
    mixed = _matmul(mixed_in, _weight(wl, "w_o"), "nn", "o_proj", duty=take(10))
    x1, h2 = _norm_fwd(x0, row(wl["norm2_g"]), row(wl["scale2"]), row(wl["shift2"]), "norm2_fwd",
                       m=mixed, gate=row(wl["gate1"]), duty=take(13))
    uu = _matmul(h2, _weight(wl, "w_up_t"), "nt", "up_proj", duty=take(42))
    act = _ffn_act_fwd(uu, wl["w_conv"], row(wl["b_conv"]), duty=take(25))
    y = _matmul(act, _weight(wl, "w_down"), "nn", "down_proj", tk=1408, duty=take(24))
    x2 = _axpy(x1, y, row(wl["gate2"]), "resid2", duty=take(11))
    saved = dict(x0=x0, h=h, p=p, qk=qk, ret_raw=ret_raw, br0=br0, gla_raw=gla_raw, states=states, br1=br1,
                 qkn=qkn, cumcol=cumcol, cumrow=cumrow, fox_o=fox_o, lse=lse, y0=ys[0], y1=ys[1], y2=ys[2],
                 gpre=gpre, mixed_in=mixed_in, mixed=mixed, x1=x1, h2=h2, uu=uu, act=act, y=y)
    return x2, saved


def _layer_bwd(dx2, wl, sv, consts, xfer=None, layer=0):
    cosf, sinf, logg = consts
    row = lambda a: a.reshape(1, -1)
    take = _taker(xfer)

    send = functools.partial(_send_grad, xfer, layer)

    dy, st_g2 = _gate_bwd(dx2, sv["y"], row(wl["gate2"]), "gate2_bwd", duty=take(10))
    dact = _matmul(dy, _weight(wl, "w_down"), "nt", "down_dx", tn=1408, duty=take(21))
    d_down = send("w_down", _matmul(sv["act"], dy, "tn", "down_dw", out_dtype=BF16, tm=1408, duty=take(19)))
    duu, st_conv = _ffn_act_bwd(sv["uu"], wl["w_conv"], row(wl["b_conv"]), dact, duty=take(40))
    dh2 = _matmul(duu, _weight(wl, "w_up_t"), "nn", "up_dx", tk=1408, duty=take(42))
    d_up_t = send("w_up", _matmul(duu, sv["h2"], "tn", "up_dw", out_dtype=BF16, tm=1408, duty=take(33)))
    dx1, st_n2 = _norm_bwd(sv["x1"], dh2, dx2, row(wl["norm2_g"]), row(wl["scale2"]), row(wl["shift2"]), "norm2_bwd",
                           duty=take(15))
    dmixed, st_g1 = _gate_bwd(dx1, sv["mixed"], row(wl["gate1"]), "gate1_bwd", duty=take(10))
    dmi = _matmul(dmixed, _weight(wl, "w_o"), "nt", "o_dx", duty=take(11))
    d_o = send("w_o", _matmul(sv["mixed_in"], dmixed, "tn", "o_dw", out_dtype=BF16, duty=take(9)))
    dy0, dy1, dy2, dgpre, st_bmg = _mix_bwd(sv["gpre"], row(wl["b_mg"]), sv["y0"], sv["y1"], sv["y2"], dmi,
                                             duty=take(31))
    brs = (sv["br0"], sv["br1"], sv["fox_o"])
    w_br_t = _weight(wl, "w_br_t")
    dbr = [_matmul(d, w_br_t[n], "nn", "br_dx%d" % n) for n, d in enumerate((dy0, dy1, dy2))]
    d_br_t = [send("w_br%d" % n, _matmul(d, brs[n], "tn", "br_dw%d" % n, out_dtype=BF16))
              for n, d in enumerate((dy0, dy1, dy2))]
    dh = _matmul(dgpre, _weight(wl, "w_mg_t"), "nn", "gate_dx", tk=1024, duty=take(29))
    d_mg_t = send("w_mg", _matmul(dgpre, sv["h"], "tn", "gate_dw", out_dtype=BF16, duty=take(21)))
    p = sv["p"]
    dqn, dkn, dfv, drow, dcol = _fox_bwd(sv["qkn"], p, sv["cumcol"], sv["cumrow"], sv["lse"], sv["fox_o"], dbr[2],
                                         duty=take(50))
    dcum4 = drow.reshape(N_HEADS, T) - dcol.reshape(N_HEADS, T)
    dcum = jnp.pad(dcum4.T, ((0, 0), (FF_LANE0, 128 - FF_LANE0 - N_HEADS)))
    dfqk, dtail_fox, st_fox = _fox_prep_bwd(p, row(wl["q_norm_g"]), row(wl["k_norm_g"]), row(wl["btail"]), dqn, dkn, dcum,
                                            duty=take(15))
    dgla_raw, dgg, st_gn = _branch_post_bwd(sv["gla_raw"], p, row(wl["gla_norm_g"]), dbr[1], GG_BLK, False, "gla_post_bwd",
                                            duty=take(12))
    dgq, dgk, dgv, dtail_gla, dw2pad, st_bg = _gla_bwd(p, wl["w2pad"], row(wl["b_gla_a"]), sv["states"], dgla_raw,
                                                       duty=take(30))
    dret_raw, drg, st_rn = _branch_post_bwd(sv["ret_raw"], p, row(wl["ret_norm_g"]), dbr[0], RG_BLK, True, "ret_post_bwd",
                                            duty=take(13))
    dqr, dkr, drv = _ret_bwd(sv["qk"], p, logg, dret_raw, duty=take(50))
    drqk = _rope_bwd(dqr, dkr, cosf, sinf, duty=take(11))
    dp = jnp.concatenate([a.astype(BF16) for a in (drqk, drv, drg, dgq, dgk, dgv, dgg, dfqk, dfv, dtail_fox + dtail_gla)]
                         + [jnp.zeros((T, NP - TAIL0 - 128), BF16)], axis=1)
    dh = _matmul(dp, _weight(wl, "w_in"), "nt", "in_dx", tk=1408, add=dh, duty=take(45))
    d_in = send("w_in", _matmul(sv["h"], dp, "tn", "in_dw", out_dtype=BF16, duty=take(32)))
    dx0, st_n1 = _norm_bwd(sv["x0"], dh, dx1, row(wl["norm1_g"]), row(wl["scale1"]), row(wl["shift1"]), "norm1_bwd",
                           duty=take(15))
    big = dict(w_in=d_in, w_o=d_o, w_down=d_down, w_br0=d_br_t[0], w_br1=d_br_t[1], w_br2=d_br_t[2], w_mg=d_mg_t,
               w_up=d_up_t)
    dmod = jnp.concatenate([st_n1[2], st_n1[1], st_g1[0], st_n2[2], st_n2[1], st_g2[0]])
    small = dict(norm1_g=st_n1[0], norm2_g=st_n2[0], b_gla_a=st_bg[0], b_fox_f=st_fox[2, FF_LANE0:FF_LANE0 + N_HEADS],
                 ret_norm_g=st_rn[0], gla_norm_g=st_gn[0], q_norm_g=st_fox[0], k_norm_g=st_fox[1], b_mg=st_bmg[0],
                 b_conv=st_conv[3], w_gla_a2=dw2pad[:LR_LANES], w_conv=st_conv[0:3])
    return dx0, big, dmod, small


SMALL_REPL = ("norm1_g", "norm2_g", "b_ada", "b_gla_a", "b_fox_f", "ret_norm_g", "gla_norm_g", "q_norm_g", "k_norm_g",
              "b_mg", "b_conv")
SMALL_SHARDED = ("w_gla_a2", "w_conv")
BIG = ("w_in", "w_o", "w_down", "w_br", "w_mg", "w_up")
WEIGHTS = ("norm1_g", "norm2_g", "w_ada", "b_ada", "w_in", "w_gla_a2", "b_gla_a", "b_fox_f", "ret_norm_g", "gla_norm_g",
           "q_norm_g", "k_norm_g", "w_br", "w_mg", "b_mg", "w_o", "w_up", "w_conv", "b_conv", "w_down")


def kernel(x, c, norm1_g, norm2_g, w_ada, b_ada, w_in, w_gla_a2, b_gla_a, b_fox_f, ret_norm_g, gla_norm_g, q_norm_g, k_norm_g, w_br, w_mg, b_mg, w_o, w_up, w_conv, b_conv, w_down, loss_target, m_norm1_g, m_norm2_g, m_w_ada, m_b_ada, m_w_in, m_w_gla_a2, m_b_gla_a, m_b_fox_f, m_ret_norm_g, m_gla_norm_g, m_q_norm_g, m_k_norm_g, m_w_br, m_w_mg, m_b_mg, m_w_o, m_w_up, m_w_conv, m_b_conv, m_w_down, v_norm1_g, v_norm2_g, v_w_ada, v_b_ada, v_w_in, v_w_gla_a2, v_b_gla_a, v_b_fox_f, v_ret_norm_g, v_gla_norm_g, v_q_norm_g, v_k_norm_g, v_w_br, v_w_mg, v_b_mg, v_w_o, v_w_up, v_w_conv, v_b_conv, v_w_down):
    W = dict(norm1_g=norm1_g, norm2_g=norm2_g, w_ada=w_ada, b_ada=b_ada, w_in=w_in, w_gla_a2=w_gla_a2, b_gla_a=b_gla_a,
             b_fox_f=b_fox_f, ret_norm_g=ret_norm_g, gla_norm_g=gla_norm_g, q_norm_g=q_norm_g, k_norm_g=k_norm_g,
             w_br=w_br, w_mg=w_mg, b_mg=b_mg, w_o=w_o, w_up=w_up, w_conv=w_conv, b_conv=b_conv, w_down=w_down)
    M = dict(norm1_g=m_norm1_g, norm2_g=m_norm2_g, w_ada=m_w_ada, b_ada=m_b_ada, w_in=m_w_in, w_gla_a2=m_w_gla_a2,
             b_gla_a=m_b_gla_a, b_fox_f=m_b_fox_f, ret_norm_g=m_ret_norm_g, gla_norm_g=m_gla_norm_g, q_norm_g=m_q_norm_g,
             k_norm_g=m_k_norm_g, w_br=m_w_br, w_mg=m_w_mg, b_mg=m_b_mg, w_o=m_w_o, w_up=m_w_up, w_conv=m_w_conv,
             b_conv=m_b_conv, w_down=m_w_down)
    V = dict(norm1_g=v_norm1_g, norm2_g=v_norm2_g, w_ada=v_w_ada, b_ada=v_b_ada, w_in=v_w_in, w_gla_a2=v_w_gla_a2,
             b_gla_a=v_b_gla_a, b_fox_f=v_b_fox_f, ret_norm_g=v_ret_norm_g, gla_norm_g=v_gla_norm_g, q_norm_g=v_q_norm_g,
             k_norm_g=v_k_norm_g, w_br=v_w_br, w_mg=v_w_mg, b_mg=v_b_mg, w_o=v_w_o, w_up=v_w_up, w_conv=v_w_conv,
             b_conv=v_b_conv, w_down=v_w_down)
    me = 4 * lax.axis_index("x") + 2 * lax.axis_index("y") + lax.axis_index("c")
    x2d, tgt = x.reshape(T, D), loss_target.reshape(T, D)

    sm = _flat_pack([c, w_gla_a2, w_conv])
    sm_all = _exchange(sm, True, "gather_small")
    parts = [_flat_unpack(sm_all[j], [(D,), (DEPTH, LR_LANES, 32), (DEPTH, 3, 352)]) for j in range(N_DEV)]
    c_all = jnp.stack([q[0] for q in parts])
    w_gla_full = jnp.concatenate([q[1] for q in parts], axis=2)
    w_conv_full = jnp.concatenate([q[2] for q in parts], axis=2)

    n_ada = w_ada.shape[2]
    b_loc = lax.dynamic_slice_in_dim(b_ada, me * n_ada, n_ada, axis=1).reshape(DEPTH, 1, n_ada)
    mod_all = _ada_fwd(c_all, w_ada, b_loc)
    mod_recv = _exchange(jnp.swapaxes(mod_all, 0, 1), False, "a2a_mod")
    mod = jnp.swapaxes(mod_recv, 0, 1).reshape(DEPTH, 6, D)

    loc = dict(w_in=_permute_in(w_in), w_o=w_o, w_down=w_down, w_br=jnp.swapaxes(w_br, 2, 3),
               w_mg=jnp.swapaxes(w_mg, 1, 2), w_up=jnp.swapaxes(w_up, 1, 2))
    loc = {k: v.astype(BF16) for k, v in loc.items()}
    w_full = dict(w_in=(D, NP), w_o=(D, D), w_down=(D_FF, D), w_br=(3, D, BW), w_mg=(3 * D, D), w_up=(2 * D_FF, D))
    w_parts = dict(w_in=8, w_br=2, w_mg=4, w_o=1, w_up=11, w_down=2)
    gather, units = _Transfers("gather"), []
    for l in range(DEPTH):
        for k, parts in w_parts.items():
            axis = 1 if k == "w_br" else 0
            n = w_full[k][axis] // N_DEV
            gather.lands[(l, k)] = lax.empty(w_full[k], BF16)
            shard = loc[k][l]
            nbytes = shard.size * 2 // parts
            units += [((l, k), (axis, n, (), c * (n // parts), n // parts), shard, nbytes) for c in range(parts)]
    first, lag = w_parts["w_in"], 4
    order = [("cross", i) for i in range(first)] + [("pass", i) for i in range(first)]
    for i in range(first, len(units) + lag):
        order += [("cross", i)] if i < len(units) else []
        order += [("pass", i - lag)] if i - lag >= first else []
    for what, i in order:
        key, where, shard, nbytes = units[i]
        if what == "cross":
            gather.add(key, ("gather_chip",) + where, shard, uid=i, nbytes=nbytes)
        else:
            gather.add(key, ("pass_on",) + where, after=i, nbytes=nbytes)

    w2pad = jnp.pad(w_gla_full, ((0, 0), (0, 128 - LR_LANES), (0, 0)))
    btail = jnp.pad(b_fox_f, ((0, 0), (FF_LANE0, 128 - FF_LANE0 - N_HEADS)))
    stacked = dict(norm1_g=norm1_g, norm2_g=norm2_g, b_gla_a=b_gla_a, ret_norm_g=ret_norm_g, gla_norm_g=gla_norm_g,
                   q_norm_g=q_norm_g, k_norm_g=k_norm_g, b_mg=b_mg, b_conv=b_conv, w_conv=w_conv_full, w2pad=w2pad,
                   btail=btail, shift1=mod[:, 0], scale1=mod[:, 1], gate1=mod[:, 2], shift2=mod[:, 3], scale2=mod[:, 4],
                   gate2=mod[:, 5])
    landed = lambda l, k: functools.partial(gather.get, (l, k))
    layers = [dict({k: v[l] for k, v in stacked.items()}, w_in=landed(l, "w_in"), w_o=landed(l, "w_o"),
                   w_down=landed(l, "w_down"), w_br_t=landed(l, "w_br"), w_mg_t=landed(l, "w_mg"), w_up_t=landed(l, "w_up"))
              for l in range(DEPTH)]
    consts = _rope_tables() + (_ret_logg(),)

    xc, saved = x2d, []
    for l in range(DEPTH):
        xc, sv = _layer_fwd(xc, layers[l], consts, gather)
        saved.append(sv)
    loss_part, dxc = _loss_fwd_bwd(xc, tgt)
    loss = lax.psum(loss_part[0, 0], ("x", "y", "c"))

    grad_names = ("w_in", "w_o", "w_down", "w_br0", "w_br1", "w_br2", "w_mg", "w_up")
    blk_rows = dict(w_in=(128, NP), w_o=(128, D), w_down=(352, D), w_br0=(128, BW), w_br1=(128, BW), w_br2=(128, BW),
                    w_mg=(384, D), w_up=(704, D))
    grads = _Transfers("grads")
    for k in grad_names:
        grads.lands[k] = lax.empty((N_DEV // 2 if k in GRAD_VIA_CHIP else N_DEV, DEPTH) + blk_rows[k], BF16)
    dmod, small_g = [None] * DEPTH, [None] * DEPTH
    for l in reversed(range(DEPTH)):
        dxc, _, dmod[l], small_g[l] = _layer_bwd(dxc, layers[l], saved[l], consts, grads, l)
    grad_x = dxc
    dmod = jnp.stack(dmod)
    small_g = {k: jnp.stack([s[k] for s in small_g]) for k in small_g[0]}
    grads.drain()
    recv = {k: grads.get(k) for k in grad_names}

    dmod_send = jnp.swapaxes(dmod.reshape(DEPTH, N_DEV, n_ada), 0, 1)
    dmod_all = jnp.swapaxes(_exchange(dmod_send, False, "a2a_dmod"), 0, 1)
    g_ada = _ada_bwd(c_all, dmod_all)

    def flat(a, k):
        return a.reshape((-1, W[k].shape[-1]))

    def adam_nat(k, g, tr):
        outs = _adamw(g, flat(W[k], k), flat(M[k], k), flat(V[k], k), tr, "adamw_" + k)
        return [o.reshape(W[k].shape) for o in outs]

    def summed(k, tr):
        r = recv[k]
        return _sum_partials(r.reshape(r.shape[0], DEPTH * r.shape[2], r.shape[3]), "sum_" + k, tr).reshape((DEPTH,) + r.shape[2:])

    big_out = dict(
        w_in=adam_nat("w_in", flat(_unpermute_in(summed("w_in", 64)), "w_in"), 64),
        w_o=adam_nat("w_o", recv["w_o"].reshape(N_DEV, DEPTH * 128, D), 128),
        w_down=adam_nat("w_down", recv["w_down"].reshape(N_DEV // 2, DEPTH * 352, D), 352),
        w_br=adam_nat("w_br", flat(jnp.swapaxes(jnp.stack([summed("w_br%d" % n, 128) for n in range(3)], axis=1), 2, 3),
                                   "w_br"), 1024),
        w_mg=adam_nat("w_mg", flat(jnp.swapaxes(summed("w_mg", 384), 1, 2), "w_mg"), 512),
        w_up=adam_nat("w_up", flat(jnp.swapaxes(summed("w_up", 704), 1, 2), "w_up"), 512))
    ada_out = [o.reshape(DEPTH, D, n_ada) for o in _adamw(
        g_ada.reshape(DEPTH * D, n_ada), w_ada.reshape(DEPTH * D, n_ada), m_w_ada.reshape(DEPTH * D, n_ada),
        v_w_ada.reshape(DEPTH * D, n_ada), 512, "adamw_ada")]

    small_g = dict(small_g, b_ada=dmod)
    names = SMALL_REPL + SMALL_SHARDED
    full_shapes = [W[n].shape for n in SMALL_REPL] + [(DEPTH, LR_LANES, 256), (DEPTH, 3, D_FF)]
    part = _flat_pack([small_g[n] for n in names])
    total = _flat_unpack(_sum_partials(_exchange(part, True, "gather_small_grads"), "sum_small"), full_shapes)
    total = dict(zip(names, total))
    total["w_gla_a2"] = lax.dynamic_slice_in_dim(total["w_gla_a2"], me * 32, 32, axis=2)
    total["w_conv"] = lax.dynamic_slice_in_dim(total["w_conv"], me * 352, 352, axis=2)
    shapes = [W[n].shape for n in names]
    small_out = _adamw(_flat_pack([total[n] for n in names]), _flat_pack([W[n] for n in names]),
                       _flat_pack([M[n] for n in names]), _flat_pack([V[n] for n in names]), None, "adamw_small")
    small_out = [dict(zip(names, _flat_unpack(o, shapes))) for o in small_out]

    outs = []
    for k in range(4):
        d = dict(small_out[k])
        d.update({n: big_out[n][k] for n in BIG})
        d["w_ada"] = ada_out[k]
        outs.append([d[n] for n in WEIGHTS])
    return (loss, grad_x.reshape(1, T, D), *outs[0], *outs[1], *outs[2], *outs[3])
```

```python
import functools

import numpy as np
import jax
import jax.numpy as jnp
from jax import lax
from jax.experimental import pallas as pl
from jax.experimental.pallas import tpu as pltpu

F32 = jnp.float32
BF16 = jnp.bfloat16

N_DEV = 8
T = 2048
D = 1024
DEPTH = 4
N_HEADS = 4
HD = 128
BW = 512
D_FF = 2816
CHUNK = 64
EPS = 1e-6
IN_W = 5140
NP = 5632
TAIL0 = 5120
LR_LANES = 16
FF_LANE0 = 16
PACK_W = 1024
SEG_ROWS = (704, 128, 352, 192, 384, 704)
LAYER_ROWS = sum(SEG_ROWS)
VMEM_LIMIT_V7X = 56 * 1024 * 1024

ADAM_LR, ADAM_B1, ADAM_B2, ADAM_EPS, ADAM_WD, ADAM_STEP = 0.001, 0.9, 0.999, 1e-08, 0.01, 10

MESH_ID = pl.DeviceIdType.MESH


def _cp(*sem):
    return pltpu.CompilerParams(dimension_semantics=sem if sem else None, vmem_limit_bytes=VMEM_LIMIT_V7X)


def _sigmoid(z):
    return 1.0 / (1.0 + jnp.exp(-z))


def _log_sigmoid(z):
    return jnp.minimum(z, 0.0) - jnp.log(1.0 + jnp.exp(-jnp.abs(z)))


def _sum0(a):
    return jnp.sum(a, axis=0, keepdims=True)


def _mean1(a):
    return jnp.mean(a, axis=-1, keepdims=True)


def _dot(a, b, dims):
    return lax.dot_general(a.astype(BF16), b.astype(BF16), (dims, ((), ())), preferred_element_type=F32)


NN = ((1,), (0,))
NT = ((1,), (1,))
TN = ((0,), (0,))


def _exact_dot(m01, a):
    a1 = a.astype(BF16)
    r1 = a - a1.astype(F32)
    a2 = r1.astype(BF16)
    a3 = (r1 - a2.astype(F32)).astype(BF16)
    d = lambda z: jnp.dot(m01, z, preferred_element_type=F32)
    return d(a1) + d(a2) + d(a3)


def _tri(n, upper):
    r = lax.broadcasted_iota(jnp.int32, (n, n), 0)
    c = lax.broadcasted_iota(jnp.int32, (n, n), 1)
    return jnp.where((c >= r) if upper else (c <= r), 1.0, 0.0).astype(BF16)


def _exchange(x, gather, name):
    blk = x.shape if gather else x.shape[1:]

    def body(x_ref, o_ref, send_sems, recv_sems, loc_sem):
        mx, my, mc = lax.axis_index("x"), lax.axis_index("y"), lax.axis_index("c")
        me = 4 * mx + 2 * my + mc
        loc = pltpu.make_async_copy(x_ref if gather else x_ref.at[me], o_ref.at[me], loc_sem)
        loc.start()
        copies = []
        for k in range(1, N_DEV):
            px = mx ^ (k >> 2) if (k >> 2) else mx
            py = my ^ ((k >> 1) & 1) if ((k >> 1) & 1) else my
            pc = mc ^ (k & 1) if (k & 1) else mc
            peer = 4 * px + 2 * py + pc
            cp = pltpu.make_async_remote_copy(
                src_ref=x_ref if gather else x_ref.at[peer], dst_ref=o_ref.at[me],
                send_sem=send_sems.at[k - 1], recv_sem=recv_sems.at[k - 1],
                device_id=(px, py, pc), device_id_type=MESH_ID)
            cp.start()
            copies.append(cp)
        for cp in copies:
            cp.wait()
        loc.wait()

    return pl.pallas_call(
        body, name=name,
        out_shape=jax.ShapeDtypeStruct((N_DEV,) + tuple(blk), x.dtype),
        in_specs=[pl.BlockSpec(memory_space=pl.ANY)],
        out_specs=pl.BlockSpec(memory_space=pl.ANY),
        scratch_shapes=[pltpu.SemaphoreType.DMA((N_DEV - 1,)), pltpu.SemaphoreType.DMA((N_DEV - 1,)),
                        pltpu.SemaphoreType.DMA],
        compiler_params=pltpu.CompilerParams(has_side_effects=True),
    )(x)


def _blk(ref, axis, j, n, r0=0, nr=None):
    return ref.at[(slice(None),) * axis + (pl.ds(j * n + r0, n if nr is None else nr),)]


def _comm_copies(items, srcs, lands, send_sems, recv_sems, loc_sems):
    mx, my, mc = lax.axis_index("x"), lax.axis_index("y"), lax.axis_index("c")
    me = 4 * mx + 2 * my + mc
    local, remote = [], []
    for t, (kind, axis, n, sel, r0, nr, si, li) in enumerate(items):
        if kind == "pass_on":
            for q in (2, 4, 6):
                px = 1 - mx if q & 4 else mx
                py = 1 - my if q & 2 else my
                rows = _blk(lands[li], axis, 4 * px + 2 * py + mc, n, r0, nr)
                remote.append(pltpu.make_async_remote_copy(
                    src_ref=rows, dst_ref=rows, send_sem=send_sems.at[t * (N_DEV - 1) + q - 1],
                    recv_sem=recv_sems.at[t * (N_DEV - 1) + q - 1], device_id=(mx, my, 1 - mc), device_id_type=MESH_ID))
            continue
        if kind == "to_other_core":
            for p in range(N_DEV // 2):
                remote.append(pltpu.make_async_remote_copy(
                    src_ref=_blk(srcs[si], axis, 2 * p + 1 - mc, n, r0, nr), dst_ref=lands[li].at[p, pl.ds(r0, nr)],
                    send_sem=send_sems.at[t * (N_DEV - 1) + p], recv_sem=recv_sems.at[t * (N_DEV - 1) + p],
                    device_id=(mx, my, 1 - mc), device_id_type=MESH_ID))
            continue
        if kind == "a2a_chip":
            pm = 2 * mx + my
            mine = lands[li].at[(pm,) + tuple(sel) + (pl.ds(r0, nr),)]
            local.append(pltpu.make_async_copy(srcs[si].at[pm, pl.ds(r0, nr)], mine, loc_sems.at[t]))
            for q in (2, 4, 6):
                px = 1 - mx if q & 4 else mx
                py = 1 - my if q & 2 else my
                remote.append(pltpu.make_async_remote_copy(
                    src_ref=srcs[si].at[2 * px + py, pl.ds(r0, nr)], dst_ref=mine,
                    send_sem=send_sems.at[t * (N_DEV - 1) + q - 1], recv_sem=recv_sems.at[t * (N_DEV - 1) + q - 1],
                    device_id=(px, py, mc), device_id_type=MESH_ID))
            continue
        if kind == "a2a":
            mine = lands[li].at[(me,) + tuple(sel) + (pl.ds(r0, nr),)]
            own = _blk(srcs[si], axis, me, n, r0, nr)
        else:
            mine = _blk(lands[li], axis, me, n, r0, nr)
            own = _blk(srcs[si], axis, 0, n, r0, nr)
        local.append(pltpu.make_async_copy(own, mine, loc_sems.at[t]))
        for k in ((1, 2, 4, 6) if kind == "gather_chip" else range(1, N_DEV)):
            px = 1 - mx if k & 4 else mx
            py = 1 - my if k & 2 else my
            pc = 1 - mc if k & 1 else mc
            src = _blk(srcs[si], axis, 4 * px + 2 * py + pc, n, r0, nr) if kind == "a2a" else own
            remote.append(pltpu.make_async_remote_copy(
                src_ref=src, dst_ref=mine, send_sem=send_sems.at[t * (N_DEV - 1) + k - 1],
                recv_sem=recv_sems.at[t * (N_DEV - 1) + k - 1], device_id=(px, py, pc), device_id_type=MESH_ID))
    return local, remote


def _comm_scratch(n_items):
    return [pltpu.SemaphoreType.DMA((n_items * (N_DEV - 1),)), pltpu.SemaphoreType.DMA((n_items * (N_DEV - 1),)),
            pltpu.SemaphoreType.DMA((n_items,))]


LINK_BYTES_PER_US = dict(gather_chip=23e3, a2a_chip=23e3, a2a=11.5e3, gather=11.5e3, pass_on=200e3, to_other_core=150e3)
CALL_EXCHANGE_US = 3.0


class _Duty:
    def __init__(self, items, srcs, lands, done):
        self.items, self.srcs, self.lands, self.done = items, srcs, lands, done


def _pcall(body, name, grid, in_specs, out_specs, out_shape, args, scratch_shapes=(), sem=(), duty=None):
    if duty is None:
        return pl.pallas_call(body, name=name, grid=grid, in_specs=list(in_specs), out_specs=out_specs,
                              out_shape=out_shape, scratch_shapes=list(scratch_shapes), compiler_params=_cp(*sem))(*args)
    single = not isinstance(out_shape, (list, tuple))
    o_shape = [out_shape] if single else list(out_shape)
    o_specs = [out_specs] if single else list(out_specs)
    n_in, n_out, n_scr = len(in_specs), len(o_shape), len(scratch_shapes)
    n_src, n_land, n_items = len(duty.srcs), len(duty.lands), len(duty.items)
    a0 = n_in + n_src + n_land

    def wrapped(*refs):
        srcs = refs[n_in:n_in + n_src]
        lands = refs[a0 + n_out:a0 + n_out + n_land]
        core = refs[:n_in] + refs[a0:a0 + n_out] + refs[a0 + n_out + n_land:a0 + n_out + n_land + n_scr]
        sems = refs[a0 + n_out + n_land + n_scr:]
        first = functools.reduce(jnp.logical_and, [pl.program_id(a) == 0 for a in range(len(grid))])
        last = functools.reduce(jnp.logical_and, [pl.program_id(a) == g - 1 for a, g in enumerate(grid)])

        @pl.when(first)
        def _():
            local, remote = _comm_copies(duty.items, srcs, lands, *sems)
            for cp in local + remote:
                cp.start()

        body(*core)

        @pl.when(last)
        def _():
            local, remote = _comm_copies(duty.items, srcs, lands, *sems)
            for cp in remote + local:
                cp.wait()

    hbm = pl.BlockSpec(memory_space=pl.ANY)
    res = pl.pallas_call(
        wrapped, name=name, grid=grid,
        in_specs=list(in_specs) + [hbm] * (n_src + n_land), out_specs=o_specs + [hbm] * n_land,
        out_shape=o_shape + [jax.ShapeDtypeStruct(a.shape, a.dtype) for a in duty.lands],
        input_output_aliases={n_in + n_src + t: n_out + t for t in range(n_land)},
        scratch_shapes=list(scratch_shapes) + _comm_scratch(n_items),
        compiler_params=pltpu.CompilerParams(dimension_semantics=("arbitrary",) * len(grid),
                                             vmem_limit_bytes=VMEM_LIMIT_V7X, has_side_effects=True),
    )(*args, *duty.srcs, *duty.lands)
    duty.done(res[n_out:])
    return res[0] if single else res[:n_out]


def _comm(duty, name):
    n_src, n_land = len(duty.srcs), len(duty.lands)

    def body(*refs):
        local, remote = _comm_copies(duty.items, refs[:n_src], refs[n_src + n_land:n_src + 2 * n_land],
                                     *refs[n_src + 2 * n_land:])
        for cp in local + remote:
            cp.start()
        for cp in remote + local:
            cp.wait()

    hbm = pl.BlockSpec(memory_space=pl.ANY)
    duty.done(pl.pallas_call(
        body, name=name, out_shape=[jax.ShapeDtypeStruct(a.shape, a.dtype) for a in duty.lands],
        in_specs=[hbm] * (n_src + n_land), out_specs=[hbm] * n_land,
        input_output_aliases={n_src + t: t for t in range(n_land)},
        scratch_shapes=_comm_scratch(len(duty.items)), compiler_params=pltpu.CompilerParams(has_side_effects=True),
    )(*duty.srcs, *duty.lands))


class _Transfers:
    def __init__(self, name):
        self.name, self.queue, self.lands, self.flushes, self.groups = name, [], {}, 0, {}

    def add(self, key, item, src=None, uid=None, after=None, group=None, nbytes=0):
        self.queue.append((key, item, src, uid, after, group, nbytes / LINK_BYTES_PER_US[item[0]]))
        if group is not None:
            self.groups[group] = [self.groups.get(group, [0, None])[0] + 1, None]

    def when_done(self, group, fn):
        self.groups[group][1] = fn

    def take_for(self, us):
        count, busy = 0, CALL_EXCHANGE_US
        while count < len(self.queue) and busy + self.queue[count][6] <= us:
            busy += self.queue[count][6]
            count += 1
        return self.take(count) if count else None

    def take(self, count):
        units = []
        while self.queue and len(units) < count:
            after = self.queue[0][4]
            if after is not None and any(u[3] == after for u in units):
                break
            units.append(self.queue.pop(0))
        if not units:
            return None
        keys, srcs, items = [], [], []
        for key, item, src, _, _, _, _ in units:
            if key not in keys:
                keys.append(key)
            if src is not None and not any(src is s for s in srcs):
                srcs.append(src)
            si = [i for i, s in enumerate(srcs) if s is src][0] if src is not None else -1
            items.append(tuple(item) + (si, keys.index(key)))

        def done(new_lands):
            for key, arr in zip(keys, new_lands):
                self.lands[key] = arr
            for u in units:
                if u[5] is not None:
                    self.groups[u[5]][0] -= 1
                    if self.groups[u[5]][0] == 0:
                        self.groups[u[5]][1]()

        return _Duty(items, srcs, [self.lands[k] for k in keys], done)

    def drain(self, upto=None):
        count = upto
        while self.queue if upto is None else count > 0:
            duty = self.take(len(self.queue) if upto is None else count)
            count = None if upto is None else count - len(duty.items)
            self.flushes += 1
            _comm(duty, "%s_flush%d" % (self.name, self.flushes))

    def get(self, key):
        pending = [i for i, u in enumerate(self.queue) if u[0] == key]
        if pending:
            self.drain(pending[-1] + 1)
        return self.lands[key]


def _matmul(a, b, mode, name, out_dtype=F32, tm=1024, tn=512, tk=None, add=None, duty=None):
    halves = a.ndim == 3
    if mode == "tn":
        K, M = a.shape[-2], a.shape[-1] * (2 if halves else 1)
        N = b.shape[1]
    else:
        M, K = a.shape[-2], a.shape[-1] * (2 if halves else 1)
        N = b.shape[0] if mode == "nt" else b.shape[1]
    tm, tn = min(tm, M), min(tn, N)
    tk = K if tk is None else tk
    nk = K // tk
    assert M % tm == 0 and N % tn == 0 and K % tk == 0, (name, M, N, K, tm, tn, tk)
    dims = {"nn": NN, "nt": NT, "tn": TN}[mode]
    has_add = add is not None

    def body(*refs):
        if has_add:
            a_ref, b_ref, add_ref, o_ref, acc_ref = refs
        else:
            a_ref, b_ref, o_ref, acc_ref = refs
        k = pl.program_id(2)
        part = _dot(a_ref[...], b_ref[...], dims)

        @pl.when(k == 0)
        def _():
            acc_ref[...] = part

        @pl.when(k > 0)
        def _():
            acc_ref[...] += part

        @pl.when(k == nk - 1)
        def _():
            r = acc_ref[...]
            if has_add:
                r = r + add_ref[...]
            o_ref[...] = r.astype(o_ref.dtype)

    if halves and mode == "tn":
        per = a.shape[-1] // tm
        a_spec = pl.BlockSpec((None, tk, tm), lambda i, j, k: (i // per, k, i % per))
    elif halves:
        per = a.shape[-1] // tk
        a_spec = pl.BlockSpec((None, tm, tk), lambda i, j, k: (k // per, i, k % per))
    elif mode == "tn":
        a_spec = pl.BlockSpec((tk, tm), lambda i, j, k: (k, i))
    else:
        a_spec = pl.BlockSpec((tm, tk), lambda i, j, k: (i, k))
    if mode == "nt":
        b_spec = pl.BlockSpec((tn, tk), lambda i, j, k: (j, k))
    else:
        b_spec = pl.BlockSpec((tk, tn), lambda i, j, k: (k, j))
    o_spec = pl.BlockSpec((tm, tn), lambda i, j, k: (i, j))
    in_specs = [a_spec, b_spec] + ([o_spec] if has_add else [])
    args = (a, b) + ((add,) if has_add else ())
    return _pcall(
        body, name=name, grid=(M // tm, N // tn, nk),
        out_shape=jax.ShapeDtypeStruct((M, N), out_dtype),
        in_specs=in_specs, out_specs=o_spec,
        scratch_shapes=[pltpu.VMEM((tm, tn), F32)],
        sem=("parallel", "parallel", "arbitrary"), args=args, duty=duty)


def _ada_fwd(c_all, w_ada, b_loc):
    n = w_ada.shape[2]

    def body(c_ref, w_ref, b_ref, o_ref):
        c = c_ref[...]
        o_ref[0] = _dot(c * _sigmoid(c), w_ref[0], NN) + b_ref[0]

    return pl.pallas_call(
        body, name="ada_fwd", grid=(DEPTH,),
        out_shape=jax.ShapeDtypeStruct((DEPTH, N_DEV, n), F32),
        in_specs=[pl.BlockSpec((N_DEV, D), lambda l: (0, 0)),
                  pl.BlockSpec((1, D, n), lambda l: (l, 0, 0)),
                  pl.BlockSpec((1, 1, n), lambda l: (l, 0, 0))],
        out_specs=pl.BlockSpec((1, N_DEV, n), lambda l: (l, 0, 0)),
        compiler_params=_cp("parallel"),
    )(c_all, w_ada, b_loc)


def _ada_bwd(c_all, dmod_all):
    n = dmod_all.shape[2]

    def body(c_ref, d_ref, o_ref):
        c = c_ref[...]
        o_ref[0] = _dot(c * _sigmoid(c), d_ref[0], TN)

    return pl.pallas_call(
        body, name="ada_bwd", grid=(DEPTH,),
        out_shape=jax.ShapeDtypeStruct((DEPTH, D, n), F32),
        in_specs=[pl.BlockSpec((N_DEV, D), lambda l: (0, 0)),
                  pl.BlockSpec((1, N_DEV, n), lambda l: (l, 0, 0))],
        out_specs=pl.BlockSpec((1, D, n), lambda l: (l, 0, 0)),
        compiler_params=_cp("parallel"),
    )(c_all, dmod_all)


ROW_TILE = 256


def _row_spec(w=D, col=0):
    return pl.BlockSpec((ROW_TILE, w), lambda i: (i, col))


def _vec_spec(w=D):
    return pl.BlockSpec((1, w), lambda i: (0, 0))


def _norm_fwd(x, g, scale, shift, name, m=None, gate=None, duty=None):
    has_res = m is not None

    def body(*refs):
        if has_res:
            x_ref, m_ref, gate_ref, g_ref, sc_ref, sh_ref, xo_ref, h_ref = refs
            xv = x_ref[...] + gate_ref[...] * m_ref[...]
            xo_ref[...] = xv
        else:
            x_ref, g_ref, sc_ref, sh_ref, h_ref = refs
            xv = x_ref[...]
        r = lax.rsqrt(_mean1(xv * xv) + EPS)
        h_ref[...] = ((xv * r * g_ref[...]) * (1.0 + sc_ref[...]) + sh_ref[...]).astype(BF16)

    ins = [x] + ([m, gate] if has_res else []) + [g, scale, shift]
    in_specs = [_row_spec()] + ([_row_spec(), _vec_spec()] if has_res else []) + [_vec_spec()] * 3
    out_shape = [jax.ShapeDtypeStruct((T, D), BF16)]
    out_specs = [_row_spec()]
    if has_res:
        out_shape = [jax.ShapeDtypeStruct((T, D), F32)] + out_shape
        out_specs = [_row_spec()] + out_specs
    out = _pcall(body, name=name, grid=(T // ROW_TILE,), out_shape=out_shape, in_specs=in_specs,
                 out_specs=out_specs, sem=("parallel",), args=ins, duty=duty)
    return out if has_res else out[0]


def _norm_bwd(x, dh, dres, g, scale, shift, name, duty=None):
    def body(x_ref, dh_ref, dres_ref, g_ref, sc_ref, sh_ref, dx_ref, st_ref):
        xv, dh_v, gv = x_ref[...], dh_ref[...], g_ref[...]
        r = lax.rsqrt(_mean1(xv * xv) + EPS)
        n = xv * r
        dy = dh_v * (1.0 + sc_ref[...])
        dn = dy * gv
        dx_ref[...] = r * (dn - n * _mean1(dn * n)) + dres_ref[...]

        @pl.when(pl.program_id(0) == 0)
        def _():
            st_ref[...] = jnp.zeros_like(st_ref)

        st_ref[0:1, :] += _sum0(dy * n)
        st_ref[1:2, :] += _sum0(dh_v * (n * gv))
        st_ref[2:3, :] += _sum0(dh_v)

    return _pcall(
        body, name=name, grid=(T // ROW_TILE,),
        out_shape=[jax.ShapeDtypeStruct((T, D), F32), jax.ShapeDtypeStruct((8, D), F32)],
        in_specs=[_row_spec(), _row_spec(), _row_spec(), _vec_spec(), _vec_spec(), _vec_spec()],
        out_specs=[_row_spec(), pl.BlockSpec((8, D), lambda i: (0, 0))],
        sem=("arbitrary",), args=(x, dh, dres, g, scale, shift), duty=duty)


def _axpy(x, m, gate, name, duty=None):
    def body(x_ref, m_ref, gate_ref, o_ref):
        o_ref[...] = x_ref[...] + gate_ref[...] * m_ref[...]

    return _pcall(
        body, name=name, grid=(T // ROW_TILE,), out_shape=jax.ShapeDtypeStruct((T, D), F32),
        in_specs=[_row_spec(), _row_spec(), _vec_spec()], out_specs=_row_spec(),
        sem=("parallel",), args=(x, m, gate), duty=duty)


def _gate_bwd(dx, m, gate, name, duty=None):
    def body(dx_ref, m_ref, gate_ref, dm_ref, st_ref):
        dxv = dx_ref[...]
        dm_ref[...] = (gate_ref[...] * dxv).astype(BF16)

        @pl.when(pl.program_id(0) == 0)
        def _():
            st_ref[...] = jnp.zeros_like(st_ref)

        st_ref[0:1, :] += _sum0(dxv * m_ref[...])

    return _pcall(
        body, name=name, grid=(T // ROW_TILE,),
        out_shape=[jax.ShapeDtypeStruct((T, D), BF16), jax.ShapeDtypeStruct((8, D), F32)],
        in_specs=[_row_spec(), _row_spec(), _vec_spec()],
        out_specs=[_row_spec(), pl.BlockSpec((8, D), lambda i: (0, 0))],
        sem=("arbitrary",), args=(dx, m, gate), duty=duty)


def _loss_fwd_bwd(y, target):
    def body(y_ref, t_ref, l_ref, d_ref):
        e = y_ref[...] - t_ref[...]
        d_ref[...] = e * (1.0 / D)

        @pl.when(pl.program_id(0) == 0)
        def _():
            l_ref[...] = jnp.zeros_like(l_ref)

        l_ref[...] += jnp.sum(_sum0(e * e), axis=1, keepdims=True) * (0.5 / D)

    return pl.pallas_call(
        body, name="loss", grid=(T // ROW_TILE,),
        out_shape=[jax.ShapeDtypeStruct((8, 128), F32), jax.ShapeDtypeStruct((T, D), F32)],
        in_specs=[_row_spec(), _row_spec()],
        out_specs=[pl.BlockSpec((8, 128), lambda i: (0, 0)), _row_spec()],
        compiler_params=_cp("arbitrary"))(y, target)


def _rope_tables():
    half = HD // 2
    inv_freq = 10000.0 ** (-jnp.arange(half, dtype=F32) / half)
    ang = jnp.arange(T, dtype=F32)[:, None] * inv_freq[None, :]
    cos, sin = jnp.cos(ang), jnp.sin(ang)
    return jnp.concatenate([cos, cos], axis=1), jnp.concatenate([-sin, sin], axis=1)


def _rope_fwd(p, cosf, sinf, duty=None):
    def body(p_ref, c_ref, s_ref, o_ref):
        cv, sv = c_ref[...], s_ref[...]
        for j in range(2 * N_HEADS):
            xv = p_ref[:, j * HD:(j + 1) * HD]
            rot = xv * cv + pltpu.roll(xv, HD // 2, 1) * sv
            if j >= N_HEADS:
                rot = rot * (HD ** -0.5)
            o_ref[:, j * HD:(j + 1) * HD] = rot.astype(BF16)

    return _pcall(
        body, name="rope_fwd", grid=(T // ROW_TILE,),
        out_shape=jax.ShapeDtypeStruct((T, 2 * BW), BF16),
        in_specs=[_row_spec(2 * BW), _row_spec(HD), _row_spec(HD)], out_specs=_row_spec(2 * BW),
        sem=("parallel",), args=(p, cosf, sinf), duty=duty)


def _rope_bwd(dq, dk, cosf, sinf, duty=None):
    def body(dq_ref, dk_ref, c_ref, s_ref, o_ref):
        cv, sv = c_ref[...], s_ref[...]
        for j in range(2 * N_HEADS):
            h = j % N_HEADS
            d = dq_ref[:, h * HD:(h + 1) * HD] if j < N_HEADS else dk_ref[:, h * HD:(h + 1) * HD] * (HD ** -0.5)
            o_ref[:, j * HD:(j + 1) * HD] = d * cv + pltpu.roll(d * sv, HD // 2, 1)

    return _pcall(
        body, name="rope_bwd", grid=(T // ROW_TILE,),
        out_shape=jax.ShapeDtypeStruct((T, 2 * BW), F32),
        in_specs=[_row_spec(BW), _row_spec(BW), _row_spec(HD), _row_spec(HD)], out_specs=_row_spec(2 * BW),
        sem=("parallel",), args=(dq, dk, cosf, sinf), duty=duty)


TQ = 256
V_RET_BLK = 8


def _ret_logg():
    lg = jnp.log1p(-jnp.exp2(-5.0 - jnp.arange(N_HEADS, dtype=F32)))
    return jnp.broadcast_to(lg[:, None, None], (N_HEADS, 1, 128))


def _block_iotas(i, kl):
    rows = lax.broadcasted_iota(jnp.int32, (TQ, kl), 0) + i * TQ
    cols = lax.broadcasted_iota(jnp.int32, (TQ, kl), 1)
    return rows, cols


def _ret_weight(lg_ref, i, kl):
    rows, cols = _block_iotas(i, kl)
    dist = jnp.abs(rows - cols).astype(F32)
    w = jnp.exp(dist * lg_ref[0][:, 0:1])
    return jnp.where((cols >> 6) <= (rows >> 6), w, 0.0)


def _per_query_block(i, fn):
    for n in range(1, T // TQ + 1):
        pl.when(i == n - 1)(functools.partial(fn, n * TQ))


def _ret_specs():
    q_spec = pl.BlockSpec((TQ, HD), lambda h, i: (i, h))
    k_spec = pl.BlockSpec((T, HD), lambda h, i: (0, N_HEADS + h))
    v_spec = pl.BlockSpec((T, HD), lambda h, i: (0, V_RET_BLK + h))
    lg_spec = pl.BlockSpec((1, 1, 128), lambda h, i: (h, 0, 0))
    return q_spec, k_spec, v_spec, lg_spec


def _ret_fwd(qk, p, logg, duty=None):
    def body(q_ref, k_ref, v_ref, lg_ref, o_ref):
        i = pl.program_id(1)

        def visible(kl):
            s = _dot(q_ref[...], k_ref[0:kl, :], NT) * _ret_weight(lg_ref, i, kl)
            o_ref[...] = _dot(s, v_ref[0:kl, :], NN)

        _per_query_block(i, visible)

    q_spec, k_spec, v_spec, lg_spec = _ret_specs()
    return _pcall(
        body, name="ret_fwd", grid=(N_HEADS, T // TQ),
        out_shape=jax.ShapeDtypeStruct((T, BW), F32),
        in_specs=[q_spec, k_spec, v_spec, lg_spec], out_specs=q_spec,
        sem=("parallel", "parallel"), args=(qk, qk, p, logg), duty=duty)


def _ret_bwd(qk, p, logg, do, duty=None):
    def body(q_ref, k_ref, v_ref, lg_ref, do_ref, dq_ref, dk_ref, dv_ref):
        i = pl.program_id(1)
        q, dov = q_ref[...], do_ref[...]

        @pl.when(i == 0)
        def _():
            dk_ref[...] = jnp.zeros_like(dk_ref)
            dv_ref[...] = jnp.zeros_like(dv_ref)

        def visible(kl):
            w = _ret_weight(lg_ref, i, kl)
            k = k_ref[0:kl, :]
            s = _dot(q, k, NT) * w
            ds = _dot(dov, v_ref[0:kl, :], NT) * w
            dk_ref[0:kl, :] += _dot(ds, q, TN)
            dv_ref[0:kl, :] += _dot(s, dov, TN)
            dq_ref[...] = _dot(ds, k, NN)

        _per_query_block(i, visible)

    q_spec, k_spec, v_spec, lg_spec = _ret_specs()
    acc_spec = pl.BlockSpec((T, HD), lambda h, i: (0, h))
    sh = jax.ShapeDtypeStruct((T, BW), F32)
    return _pcall(
        body, name="ret_bwd", grid=(N_HEADS, T // TQ),
        out_shape=[sh, sh, sh],
        in_specs=[q_spec, k_spec, v_spec, lg_spec, q_spec], out_specs=[q_spec, acc_spec, acc_spec],
        sem=("parallel", "arbitrary"), args=(qk, qk, p, logg, do), duty=duty)


def _post_norm(xv, gv, centered):
    if centered:
        xv = xv - _mean1(xv)
    r = lax.rsqrt(_mean1(xv * xv) + EPS)
    return xv * r, r


def _branch_post_fwd(raw, p, g, gate_blk, centered, name, duty=None):
    def body(raw_ref, z_ref, g_ref, o_ref):
        for h in range(N_HEADS):
            sl = slice(h * HD, (h + 1) * HD)
            gv = g_ref[:, sl] if centered else g_ref[...]
            xh, _ = _post_norm(raw_ref[:, sl], gv, centered)
            z = z_ref[:, sl]
            o_ref[:, sl] = (z * _sigmoid(z) * (xh * gv)).astype(BF16)

    return _pcall(
        body, name=name, grid=(T // ROW_TILE,),
        out_shape=jax.ShapeDtypeStruct((T, BW), BF16),
        in_specs=[_row_spec(BW), _row_spec(BW, gate_blk), _vec_spec(BW if centered else HD)],
        out_specs=_row_spec(BW), sem=("parallel",), args=(raw, p, g), duty=duty)


def _branch_post_bwd(raw, p, g, dout, gate_blk, centered, name, duty=None):
    gw = BW if centered else HD

    def body(raw_ref, z_ref, g_ref, do_ref, dr_ref, dz_ref, dg_ref):
        @pl.when(pl.program_id(0) == 0)
        def _():
            dg_ref[...] = jnp.zeros_like(dg_ref)

        for h in range(N_HEADS):
            sl = slice(h * HD, (h + 1) * HD)
            gsl = sl if centered else slice(0, HD)
            gv, z, dov = g_ref[:, gsl], z_ref[:, sl], do_ref[:, sl]
            xh, r = _post_norm(raw_ref[:, sl], gv, centered)
            sg = _sigmoid(z)
            dyn = dov * (z * sg)
            dz_ref[:, sl] = dov * (xh * gv) * (sg * (1.0 + z * (1.0 - sg)))
            dxh = dyn * gv
            t = dxh - xh * _mean1(dxh * xh)
            if centered:
                t = t - _mean1(dxh)
            dr_ref[:, sl] = r * t
            dg_ref[0:1, gsl] += _sum0(dyn * xh)

    return _pcall(
        body, name=name, grid=(T // ROW_TILE,),
        out_shape=[jax.ShapeDtypeStruct((T, BW), F32), jax.ShapeDtypeStruct((T, BW), F32),
                   jax.ShapeDtypeStruct((8, gw), F32)],
        in_specs=[_row_spec(BW), _row_spec(BW, gate_blk), _vec_spec(gw), _row_spec(BW)],
        out_specs=[_row_spec(BW), _row_spec(BW), pl.BlockSpec((8, gw), lambda i: (0, 0))],
        sem=("arbitrary",), args=(raw, p, g, dout), duty=duty)


GLA_ROWS = 256
GLA_CPB = GLA_ROWS // CHUNK
GLA_DK = 64
GLA_W = N_HEADS * GLA_DK
GQ_BLK, GK_BLK, GV_BLK, GG_BLK, TAIL_BLK = 8, 9, 5, 6, 40
RG_BLK = 3


def _gla_chunk_common(tl, w2, bv, kv):
    pre = _dot(tl, w2, NN) + bv
    la = _log_sigmoid(pre) * (1.0 / 16.0)
    bc = _exact_dot(_tri(CHUNK, False), la)
    be = bc[CHUNK - 1:CHUNK, :]
    w = jnp.exp(be - bc)
    return pre, w, jnp.exp(be), kv * w


def _head_masks():
    lane = lax.broadcasted_iota(jnp.int32, (1, GLA_W), 1)
    return [jnp.where((lane // GLA_DK) == h, 1.0, 0.0) for h in range(N_HEADS)]


def _gla_fwd(p, w2pad, b, duty=None):
    nb = T // GLA_ROWS

    def body(q_ref, k_ref, v_ref, t_ref, w2_ref, b_ref, o_ref, st_ref, s_acc):
        @pl.when(pl.program_id(0) == 0)
        def _():
            s_acc[...] = jnp.zeros_like(s_acc)

        masks = _head_masks()
        for c in range(GLA_CPB):
            rows = slice(c * CHUNK, (c + 1) * CHUNK)
            _, _, a, kd = _gla_chunk_common(t_ref[rows, :], w2_ref[...], b_ref[...], k_ref[rows, :])
            q = q_ref[rows, :] * (GLA_DK ** -0.5)
            kv = None
            for h in range(N_HEADS):
                t = _dot(v_ref[rows, h * HD:(h + 1) * HD], kd * masks[h], TN)
                kv = t if kv is None else kv + t
            s_new = s_acc[...] * a + kv
            s_acc[...] = s_new
            st_ref[c] = s_new
            for h in range(N_HEADS):
                o_ref[rows, h * HD:(h + 1) * HD] = _dot(q * masks[h], s_new, NT)

    return _pcall(
        body, name="gla_fwd", grid=(nb,),
        out_shape=[jax.ShapeDtypeStruct((T, BW), F32), jax.ShapeDtypeStruct((T // CHUNK, HD, GLA_W), F32)],
        in_specs=[pl.BlockSpec((GLA_ROWS, GLA_W), lambda i: (i, GQ_BLK)),
                  pl.BlockSpec((GLA_ROWS, GLA_W), lambda i: (i, GK_BLK)),
                  pl.BlockSpec((GLA_ROWS, BW), lambda i: (i, GV_BLK)),
                  pl.BlockSpec((GLA_ROWS, 128), lambda i: (i, TAIL_BLK)),
                  pl.BlockSpec((128, GLA_W), lambda i: (0, 0)),
                  pl.BlockSpec((1, GLA_W), lambda i: (0, 0))],
        out_specs=[pl.BlockSpec((GLA_ROWS, BW), lambda i: (i, 0)),
                   pl.BlockSpec((GLA_CPB, HD, GLA_W), lambda i: (i, 0, 0))],
        scratch_shapes=[pltpu.VMEM((HD, GLA_W), F32)],
        sem=("arbitrary",), args=(p, p, p, p, w2pad, b), duty=duty)


def _gla_bwd(p, w2pad, b, states, do, duty=None):
    nb = T // GLA_ROWS

    def body(q_ref, k_ref, v_ref, t_ref, w2_ref, b_ref, st_ref, prev_ref, do_ref,
             dq_ref, dk_ref, dv_ref, dt_ref, dw2_ref, db_ref, ds_acc):
        step = pl.program_id(0)

        @pl.when(step == 0)
        def _():
            ds_acc[...] = jnp.zeros_like(ds_acc)
            dw2_ref[...] = jnp.zeros_like(dw2_ref)
            db_ref[...] = jnp.zeros_like(db_ref)

        masks = _head_masks()
        up = _tri(CHUNK, True)
        has_prev = jnp.where(step == nb - 1, 0.0, 1.0)
        for c in reversed(range(GLA_CPB)):
            rows = slice(c * CHUNK, (c + 1) * CHUNK)
            tl, w2, k = t_ref[rows, :], w2_ref[...], k_ref[rows, :]
            pre, w, a, kd = _gla_chunk_common(tl, w2, b_ref[...], k)
            q = q_ref[rows, :] * (GLA_DK ** -0.5)
            s_n = st_ref[c]
            s_prev = st_ref[c - 1] if c > 0 else prev_ref[0] * has_prev
            ds = ds_acc[...]
            dos = [do_ref[rows, h * HD:(h + 1) * HD] for h in range(N_HEADS)]
            for h in range(N_HEADS):
                ds = ds + _dot(dos[h], q * masks[h], TN)
            dqp = jnp.zeros((CHUNK, GLA_W), F32)
            dkd = jnp.zeros((CHUNK, GLA_W), F32)
            for h in range(N_HEADS):
                dqp = dqp + masks[h] * _dot(dos[h], s_n, NN)
                dkd = dkd + masks[h] * _dot(v_ref[rows, h * HD:(h + 1) * HD], ds, NN)
                dv_ref[rows, h * HD:(h + 1) * HD] = _dot(kd * masks[h], ds, NT)
            dq_ref[rows, :] = dqp * (GLA_DK ** -0.5)
            dk_ref[rows, :] = dkd * w
            e = dkd * k * w
            dbe = _sum0(e) + _sum0(ds * s_prev) * a
            dla = dbe - _exact_dot(up, e)
            dpre = dla * (1.0 / 16.0) * _sigmoid(-pre)
            db_ref[0:1, :] += _sum0(dpre)
            dw2_ref[...] += _dot(tl, dpre, TN)
            dt_ref[rows, :] = _dot(dpre, w2, NT)
            ds_acc[...] = ds * a

    rev = lambda i: nb - 1 - i
    sh = lambda w: jax.ShapeDtypeStruct((T, w), F32)
    return _pcall(
        body, name="gla_bwd", grid=(nb,),
        out_shape=[sh(GLA_W), sh(GLA_W), sh(BW), sh(128), jax.ShapeDtypeStruct((128, GLA_W), F32),
                   jax.ShapeDtypeStruct((8, GLA_W), F32)],
        in_specs=[pl.BlockSpec((GLA_ROWS, GLA_W), lambda i: (rev(i), GQ_BLK)),
                  pl.BlockSpec((GLA_ROWS, GLA_W), lambda i: (rev(i), GK_BLK)),
                  pl.BlockSpec((GLA_ROWS, BW), lambda i: (rev(i), GV_BLK)),
                  pl.BlockSpec((GLA_ROWS, 128), lambda i: (rev(i), TAIL_BLK)),
                  pl.BlockSpec((128, GLA_W), lambda i: (0, 0)),
                  pl.BlockSpec((1, GLA_W), lambda i: (0, 0)),
                  pl.BlockSpec((GLA_CPB, HD, GLA_W), lambda i: (rev(i), 0, 0)),
                  pl.BlockSpec((1, HD, GLA_W), lambda i: (jnp.maximum(rev(i) * GLA_CPB - 1, 0), 0, 0)),
                  pl.BlockSpec((GLA_ROWS, BW), lambda i: (rev(i), 0))],
        out_specs=[pl.BlockSpec((GLA_ROWS, GLA_W), lambda i: (rev(i), 0)),
                   pl.BlockSpec((GLA_ROWS, GLA_W), lambda i: (rev(i), 0)),
                   pl.BlockSpec((GLA_ROWS, BW), lambda i: (rev(i), 0)),
                   pl.BlockSpec((GLA_ROWS, 128), lambda i: (rev(i), 0)),
                   pl.BlockSpec((128, GLA_W), lambda i: (0, 0)),
                   pl.BlockSpec((8, GLA_W), lambda i: (0, 0))],
        scratch_shapes=[pltpu.VMEM((HD, GLA_W), F32)],
        sem=("arbitrary",), args=(p, p, p, p, w2pad, b, states, states, do), duty=duty)


FQ_BLK, FK_BLK = 7, 8
V_FOX_BLK = 36


def _fox_prep_fwd(p, qg, kg, btail, duty=None):
    def body(q_ref, k_ref, t_ref, qg_ref, kg_ref, bt_ref, o_ref, cum_ref, carry):
        @pl.when(pl.program_id(0) == 0)
        def _():
            carry[...] = jnp.zeros_like(carry)

        for src, gr, off in ((q_ref, qg_ref, 0), (k_ref, kg_ref, BW)):
            for h in range(N_HEADS):
                xv = src[:, h * HD:(h + 1) * HD]
                r = lax.rsqrt(_mean1(xv * xv) + EPS)
                o_ref[:, off + h * HD:off + (h + 1) * HD] = (xv * r * gr[...]).astype(BF16)
        logf = _log_sigmoid(t_ref[...] + bt_ref[...])
        cum = _exact_dot(_tri(ROW_TILE, False), logf) + carry[...]
        cum_ref[...] = cum
        carry[...] = cum[ROW_TILE - 1:ROW_TILE, :]

    return _pcall(
        body, name="fox_prep_fwd", grid=(T // ROW_TILE,),
        out_shape=[jax.ShapeDtypeStruct((T, 2 * BW), BF16), jax.ShapeDtypeStruct((T, 128), F32)],
        in_specs=[_row_spec(BW, FQ_BLK), _row_spec(BW, FK_BLK), _row_spec(128, TAIL_BLK),
                  _vec_spec(HD), _vec_spec(HD), _vec_spec(128)],
        out_specs=[_row_spec(2 * BW), _row_spec(128)],
        scratch_shapes=[pltpu.VMEM((1, 128), F32)],
        sem=("arbitrary",), args=(p, p, p, qg, kg, btail), duty=duty)


def _fox_prep_bwd(p, qg, kg, btail, dqn, dkn, dcum, duty=None):
    nt = T // ROW_TILE

    def body(q_ref, k_ref, t_ref, qg_ref, kg_ref, bt_ref, dq_ref, dk_ref, dc_ref, o_ref, dt_ref, st_ref, carry):
        @pl.when(pl.program_id(0) == 0)
        def _():
            carry[...] = jnp.zeros_like(carry)
            st_ref[...] = jnp.zeros_like(st_ref)

        for row, (src, gr, dsrc, off) in enumerate(((q_ref, qg_ref, dq_ref, 0), (k_ref, kg_ref, dk_ref, BW))):
            for h in range(N_HEADS):
                xv = src[:, h * HD:(h + 1) * HD]
                dy = dsrc[:, h * HD:(h + 1) * HD]
                r = lax.rsqrt(_mean1(xv * xv) + EPS)
                n = xv * r
                dn = dy * gr[...]
                o_ref[:, off + h * HD:off + (h + 1) * HD] = r * (dn - n * _mean1(dn * n))
                st_ref[row:row + 1, :] += _sum0(dy * n)
        z = t_ref[...] + bt_ref[...]
        dlogf = _exact_dot(_tri(ROW_TILE, True), dc_ref[...]) + carry[...]
        carry[...] = dlogf[0:1, :]
        lane = lax.broadcasted_iota(jnp.int32, (1, 128), 1)
        keep = (lane >= FF_LANE0) & (lane < FF_LANE0 + N_HEADS)
        dz = jnp.where(keep, dlogf * _sigmoid(-z), 0.0)
        dt_ref[...] = dz
        st_ref[2:3, :] += _sum0(dz)

    rs = lambda w, col=0: pl.BlockSpec((ROW_TILE, w), lambda i: (nt - 1 - i, col))
    return _pcall(
        body, name="fox_prep_bwd", grid=(nt,),
        out_shape=[jax.ShapeDtypeStruct((T, 2 * BW), F32), jax.ShapeDtypeStruct((T, 128), F32),
                   jax.ShapeDtypeStruct((8, 128), F32)],
        in_specs=[rs(BW, FQ_BLK), rs(BW, FK_BLK), rs(128, TAIL_BLK), _vec_spec(HD), _vec_spec(HD), _vec_spec(128),
                  rs(BW), rs(BW), rs(128)],
        out_specs=[rs(2 * BW), rs(128), pl.BlockSpec((8, 128), lambda i: (0, 0))],
        scratch_shapes=[pltpu.VMEM((1, 128), F32)],
        sem=("arbitrary",), args=(p, p, p, qg, kg, btail, dqn, dkn, dcum), duty=duty)


def _fox_logits(q_ref, k_ref, cc_ref, cr_ref, i, kl):
    rows, cols = _block_iotas(i, kl)
    s = _dot(q_ref[...], k_ref[0:kl, :], NT) * (HD ** -0.5) + cc_ref[0] - cr_ref[0, :, 0:kl]
    return jnp.where(cols <= rows, s, -1e30)


def _fox_specs():
    q_spec = pl.BlockSpec((TQ, HD), lambda h, i: (i, h))
    k_spec = pl.BlockSpec((T, HD), lambda h, i: (0, N_HEADS + h))
    v_spec = pl.BlockSpec((T, HD), lambda h, i: (0, V_FOX_BLK + h))
    col_spec = pl.BlockSpec((1, TQ, 1), lambda h, i: (h, i, 0))
    row_spec = pl.BlockSpec((1, 1, T), lambda h, i: (h, 0, 0))
    return q_spec, k_spec, v_spec, col_spec, row_spec


def _fox_fwd(qkn, p, cumcol, cumrow, duty=None):
    def body(q_ref, k_ref, v_ref, cc_ref, cr_ref, o_ref, lse_ref):
        i = pl.program_id(1)

        def visible(kl):
            s = _fox_logits(q_ref, k_ref, cc_ref, cr_ref, i, kl)
            m = jnp.max(s, axis=-1, keepdims=True)
            e = jnp.exp(s - m)
            l = jnp.sum(e, axis=-1, keepdims=True)
            o_ref[...] = _dot(e / l, v_ref[0:kl, :], NN)
            lse_ref[0] = m + jnp.log(l)

        _per_query_block(i, visible)

    q_spec, k_spec, v_spec, col_spec, row_spec = _fox_specs()
    return _pcall(
        body, name="fox_fwd", grid=(N_HEADS, T // TQ),
        out_shape=[jax.ShapeDtypeStruct((T, BW), F32), jax.ShapeDtypeStruct((N_HEADS, T, 1), F32)],
        in_specs=[q_spec, k_spec, v_spec, col_spec, row_spec], out_specs=[q_spec, col_spec],
        sem=("parallel", "parallel"), args=(qkn, qkn, p, cumcol, cumrow), duty=duty)


def _fox_bwd(qkn, p, cumcol, cumrow, lse, o, do, duty=None):
    def body(q_ref, k_ref, v_ref, cc_ref, cr_ref, lse_ref, o_ref, do_ref, dq_ref, dk_ref, dv_ref, dr_ref, dc_ref):
        i = pl.program_id(1)
        @pl.when(i == 0)
        def _():
            dk_ref[...] = jnp.zeros_like(dk_ref)
            dv_ref[...] = jnp.zeros_like(dv_ref)
            dc_ref[...] = jnp.zeros_like(dc_ref)

        def visible(kl):
            q, dov = q_ref[...], do_ref[...]
            pm = jnp.exp(_fox_logits(q_ref, k_ref, cc_ref, cr_ref, i, kl) - lse_ref[0])
            delta = jnp.sum(o_ref[...] * dov, axis=-1, keepdims=True)
            ds = pm * (_dot(dov, v_ref[0:kl, :], NT) - delta)
            dq_ref[...] = _dot(ds, k_ref[0:kl, :], NN) * (HD ** -0.5)
            dr_ref[0] = jnp.sum(ds, axis=-1, keepdims=True)
            dk_ref[0:kl, :] += _dot(ds, q, TN) * (HD ** -0.5)
            dv_ref[0:kl, :] += _dot(pm, dov, TN)
            dc_ref[0, :, 0:kl] += _sum0(ds)

        _per_query_block(i, visible)

    q_spec, k_spec, v_spec, col_spec, row_spec = _fox_specs()
    acc_spec = pl.BlockSpec((T, HD), lambda h, i: (0, h))
    sh = jax.ShapeDtypeStruct((T, BW), F32)
    return _pcall(
        body, name="fox_bwd", grid=(N_HEADS, T // TQ),
        out_shape=[sh, sh, sh, jax.ShapeDtypeStruct((N_HEADS, T, 1), F32), jax.ShapeDtypeStruct((N_HEADS, 1, T), F32)],
        in_specs=[q_spec, k_spec, v_spec, col_spec, row_spec, col_spec, q_spec, q_spec],
        out_specs=[q_spec, acc_spec, acc_spec, col_spec, row_spec],
        sem=("parallel", "arbitrary"), args=(qkn, qkn, p, cumcol, cumrow, lse, o, do), duty=duty)


def _mix_fwd(gpre, b_mg, y0, y1, y2, duty=None):
    def body(g_ref, b_ref, y0_ref, y1_ref, y2_ref, o_ref):
        acc = None
        for n, y_ref in enumerate((y0_ref, y1_ref, y2_ref)):
            sl = slice(n * D, (n + 1) * D)
            t = _sigmoid(g_ref[:, sl] + b_ref[:, sl]) * y_ref[...]
            acc = t if acc is None else acc + t
        o_ref[...] = acc.astype(BF16)

    return _pcall(
        body, name="mix_fwd", grid=(T // ROW_TILE,), out_shape=jax.ShapeDtypeStruct((T, D), BF16),
        in_specs=[_row_spec(3 * D), _vec_spec(3 * D), _row_spec(), _row_spec(), _row_spec()],
        out_specs=_row_spec(), sem=("parallel",), args=(gpre, b_mg, y0, y1, y2), duty=duty)


def _mix_bwd(gpre, b_mg, y0, y1, y2, dmi, duty=None):
    def body(g_ref, b_ref, y0_ref, y1_ref, y2_ref, d_ref, dy0_ref, dy1_ref, dy2_ref, dg_ref, db_ref):
        @pl.when(pl.program_id(0) == 0)
        def _():
            db_ref[...] = jnp.zeros_like(db_ref)

        dv = d_ref[...]
        for n, (y_ref, dy_ref) in enumerate(((y0_ref, dy0_ref), (y1_ref, dy1_ref), (y2_ref, dy2_ref))):
            sl = slice(n * D, (n + 1) * D)
            sg = _sigmoid(g_ref[:, sl] + b_ref[:, sl])
            dy_ref[...] = (dv * sg).astype(BF16)
            dpre = dv * y_ref[...] * (sg * (1.0 - sg))
            dg_ref[:, sl] = dpre.astype(BF16)
            db_ref[0:1, sl] += _sum0(dpre)

    shb = jax.ShapeDtypeStruct((T, D), BF16)
    return _pcall(
        body, name="mix_bwd", grid=(T // ROW_TILE,),
        out_shape=[shb, shb, shb, jax.ShapeDtypeStruct((T, 3 * D), BF16), jax.ShapeDtypeStruct((8, 3 * D), F32)],
        in_specs=[_row_spec(3 * D), _vec_spec(3 * D), _row_spec(), _row_spec(), _row_spec(), _row_spec()],
        out_specs=[_row_spec(), _row_spec(), _row_spec(), _row_spec(3 * D), pl.BlockSpec((8, 3 * D), lambda i: (0, 0))],
        sem=("arbitrary",), args=(gpre, b_mg, y0, y1, y2, dmi), duty=duty)


FF_COLS = 256
FF_NBLK = D_FF // FF_COLS


def _shift_rows(a, n):
    rows = lax.broadcasted_iota(jnp.int32, a.shape, 0)
    rolled = pltpu.roll(a, n % T, 0)
    return jnp.where((rows >= n) if n > 0 else (rows < T + n), rolled, 0.0)


def _ffn_act_fwd(uu, w_conv, b_conv, duty=None):
    def body(u_ref, g_ref, w_ref, b_ref, o_ref):
        u = u_ref[...]
        w = w_ref[...]
        uc = b_ref[...] + w[0:1, :] * _shift_rows(u, 2) + w[1:2, :] * _shift_rows(u, 1) + w[2:3, :] * u
        o_ref[...] = (uc * _sigmoid(uc) * g_ref[...]).astype(BF16)

    return _pcall(
        body, name="ffn_act_fwd", grid=(FF_NBLK,), out_shape=jax.ShapeDtypeStruct((T, D_FF), BF16),
        in_specs=[pl.BlockSpec((T, FF_COLS), lambda j: (0, j)), pl.BlockSpec((T, FF_COLS), lambda j: (0, FF_NBLK + j)),
                  pl.BlockSpec((3, FF_COLS), lambda j: (0, j)), pl.BlockSpec((1, FF_COLS), lambda j: (0, j))],
        out_specs=pl.BlockSpec((T, FF_COLS), lambda j: (0, j)),
        sem=("parallel",), args=(uu, uu, w_conv, b_conv), duty=duty)


def _ffn_act_bwd(uu, w_conv, b_conv, da, duty=None):
    def body(u_ref, g_ref, w_ref, b_ref, da_ref, d_ref, st_ref):
        u, w, dav = u_ref[...], w_ref[...], da_ref[...]
        u1, u2 = _shift_rows(u, 1), _shift_rows(u, 2)
        uc = b_ref[...] + w[0:1, :] * u2 + w[1:2, :] * u1 + w[2:3, :] * u
        sg = _sigmoid(uc)
        d_ref[1] = (dav * (uc * sg)).astype(BF16)
        duc = dav * g_ref[...] * (sg * (1.0 + uc * (1.0 - sg)))
        du = w[2:3, :] * duc + w[1:2, :] * _shift_rows(duc, -1) + w[0:1, :] * _shift_rows(duc, -2)
        d_ref[0] = du.astype(BF16)
        st_ref[...] = jnp.zeros_like(st_ref)
        st_ref[0:1, :] = _sum0(duc * u2)
        st_ref[1:2, :] = _sum0(duc * u1)
        st_ref[2:3, :] = _sum0(duc * u)
        st_ref[3:4, :] = _sum0(duc)

    cb = lambda rows=T, off=0: pl.BlockSpec((rows, FF_COLS), lambda j: (0, off + j))
    return _pcall(
        body, name="ffn_act_bwd", grid=(FF_NBLK,),
        out_shape=[jax.ShapeDtypeStruct((2, T, D_FF), BF16), jax.ShapeDtypeStruct((8, D_FF), F32)],
        in_specs=[cb(), cb(T, FF_NBLK), cb(3), cb(1), cb()],
        out_specs=[pl.BlockSpec((2, T, FF_COLS), lambda j: (0, 0, j)), cb(8)],
        sem=("parallel",), args=(uu, uu, w_conv, b_conv, da), duty=duty)


def _adamw(g, w, m, v, tr, name):
    partial = g.ndim == 3
    R, C = w.shape
    tr = R if tr is None else tr
    assert R % tr == 0

    def body(g_ref, w_ref, m_ref, v_ref, go_ref, d_ref, mo_ref, vo_ref):
        if partial:
            gv = g_ref[0].astype(F32)
            for j in range(1, g.shape[0]):
                gv = gv + g_ref[j].astype(F32)
        else:
            gv = g_ref[...]
        go_ref[...] = gv
        mn = ADAM_B1 * m_ref[...] + (1.0 - ADAM_B1) * gv
        vn = ADAM_B2 * v_ref[...] + (1.0 - ADAM_B2) * (gv * gv)
        mo_ref[...] = mn
        vo_ref[...] = vn
        m_hat = mn / (1.0 - ADAM_B1 ** ADAM_STEP)
        v_hat = vn / (1.0 - ADAM_B2 ** ADAM_STEP)
        d_ref[...] = -ADAM_LR * (m_hat / (jnp.sqrt(v_hat) + ADAM_EPS) + ADAM_WD * w_ref[...])

    spec = pl.BlockSpec((tr, C), lambda i: (i, 0))
    g_spec = pl.BlockSpec((g.shape[0], tr, C), lambda i: (0, i, 0)) if partial else spec
    sh = jax.ShapeDtypeStruct((R, C), F32)
    return pl.pallas_call(
        body, name=name, grid=(R // tr,), out_shape=[sh, sh, sh, sh],
        in_specs=[g_spec, spec, spec, spec], out_specs=[spec, spec, spec, spec],
        compiler_params=_cp("parallel"))(g, w, m, v)


def _chip_sum(dw, stage, name):
    n_chip, n, C = stage.shape

    def body(d_ref, s_ref, o_ref):
        mc = lax.axis_index("c")
        mine = jnp.where(mc == 0, d_ref[0, 0].astype(F32), d_ref[0, 1].astype(F32))
        o_ref[0] = (mine + s_ref[0].astype(F32)).astype(BF16)

    return pl.pallas_call(
        body, name=name, grid=(n_chip,), out_shape=jax.ShapeDtypeStruct(stage.shape, BF16),
        in_specs=[pl.BlockSpec((1, 2, n, C), lambda p: (p, 0, 0, 0)), pl.BlockSpec((1, n, C), lambda p: (p, 0, 0))],
        out_specs=pl.BlockSpec((1, n, C), lambda p: (p, 0, 0)), compiler_params=_cp("parallel"),
    )(dw.reshape(n_chip, 2, n, C), stage)


def _sum_partials(g, name, tr=None):
    n_part, R, C = g.shape
    tr = R if tr is None else tr
    assert R % tr == 0

    def body(g_ref, o_ref):
        acc = g_ref[0].astype(F32)
        for j in range(1, n_part):
            acc = acc + g_ref[j].astype(F32)
        o_ref[...] = acc

    return pl.pallas_call(
        body, name=name, grid=(R // tr,), out_shape=jax.ShapeDtypeStruct((R, C), F32),
        in_specs=[pl.BlockSpec((n_part, tr, C), lambda i: (0, i, 0))], out_specs=pl.BlockSpec((tr, C), lambda i: (i, 0)),
        compiler_params=_cp("parallel"))(g)


def _permute_in(w):
    pad = jnp.zeros(w.shape[:-1] + (NP - IN_W,), w.dtype)
    return jnp.concatenate([w[..., :3072], w[..., 3088:5136], w[..., 3072:3088], w[..., 5136:5140], pad], axis=-1)


def _unpermute_in(w):
    return jnp.concatenate([w[..., :3072], w[..., 5120:5136], w[..., 3072:5120], w[..., 5136:5140]], axis=-1)


def _flat_pack(arrs):
    flat = jnp.concatenate([a.reshape(-1).astype(F32) for a in arrs])
    n = flat.shape[0]
    rows = -(-n // 1024) * 8
    return jnp.pad(flat, (0, rows * 128 - n)).reshape(rows, 128)


def _flat_unpack(buf, shapes):
    flat = buf.reshape(-1)
    out, off = [], 0
    for s in shapes:
        n = int(np.prod(s))
        out.append(flat[off:off + n].reshape(s))
        off += n
    return out


GRAD_CHUNKS = dict(w_in=(128, 8), w_o=(128, 1), w_down=(352, 2), w_br0=(128, 1), w_br1=(128, 1), w_br2=(128, 1),
                   w_mg=(384, 4), w_up=(704, 11))
GRAD_VIA_CHIP = ("w_in", "w_down", "w_mg", "w_up")


def _send_grad(xfer, layer, k, g):
    if xfer is None:
        return g
    n, parts = GRAD_CHUNKS[k]
    row_bytes = g.shape[1] * 2
    if k not in GRAD_VIA_CHIP:
        for c in range(parts):
            xfer.add(k, ("a2a", 0, n, (layer,), c * (n // parts), n // parts), g, nbytes=n // parts * row_bytes)
        return g
    stage = ("stage", layer, k)
    xfer.lands[stage] = lax.empty((N_DEV // 2, n, g.shape[1]), BF16)
    xfer.add(stage, ("to_other_core", 0, n, (), 0, n), g, group=stage, nbytes=n * row_bytes)

    def both_halves_here():
        chip = _chip_sum(g, xfer.lands[stage], "chip_sum_" + k)
        for c in range(parts):
            xfer.add(k, ("a2a_chip", 0, n, (layer,), c * (n // parts), n // parts), chip, nbytes=n // parts * row_bytes)

    xfer.when_done(stage, both_halves_here)
    return g


def _weight(wl, k):
    return wl[k]() if callable(wl[k]) else wl[k]


MIN_CARRIER_US = 19.0


def _taker(xfer, fill=1.0):
    if xfer is None:
        return lambda us: None
    return lambda us: xfer.take_for(us * fill) if us >= MIN_CARRIER_US else None


def _layer_fwd(x0, wl, consts, xfer=None, fill=1.25):
    cosf, sinf, logg = consts
    row = lambda a: a.reshape(1, -1)
    take = _taker(xfer, fill)
    h = _norm_fwd(x0, row(wl["norm1_g"]), row(wl["scale1"]), row(wl["shift1"]), "norm1_fwd", duty=take(9))
    p = _matmul(h, _weight(wl, "w_in"), "nn", "in_proj", duty=take(41))
    qk = _rope_fwd(p, cosf, sinf, duty=take(10))
    ret_raw = _ret_fwd(qk, p, logg, duty=take(27))
    br0 = _branch_post_fwd(ret_raw, p, row(wl["ret_norm_g"]), RG_BLK, True, "ret_post_fwd", duty=take(9))
    gla_raw, states = _gla_fwd(p, wl["w2pad"], row(wl["b_gla_a"]), duty=take(26))
    br1 = _branch_post_fwd(gla_raw, p, row(wl["gla_norm_g"]), GG_BLK, False, "gla_post_fwd", duty=take(9))
    qkn, cum = _fox_prep_fwd(p, row(wl["q_norm_g"]), row(wl["k_norm_g"]), row(wl["btail"]), duty=take(10))
    cum4 = cum[:, FF_LANE0:FF_LANE0 + N_HEADS].T
    cumcol, cumrow = cum4.reshape(N_HEADS, T, 1), cum4.reshape(N_HEADS, 1, T)
    fox_o, lse = _fox_fwd(qkn, p, cumcol, cumrow, duty=take(30))
    w_br_t = _weight(wl, "w_br_t")
    ys = [_matmul(b, w_br_t[n], "nt", "br_proj%d" % n) for n, b in enumerate((br0, br1, fox_o))]
    gpre = _matmul(h, _weight(wl, "w_mg_t"), "nt", "gate_proj", duty=take(25))
    mixed_in = _mix_fwd(gpre, row(wl["b_mg"]), *ys, duty=take(21))
    mixed = _matmul(mixed_in, _weight(wl, "w_o"), "nn", "o_proj", duty=take(10))
    x1, h2 = _norm_fwd(x0, row(wl["norm2_g"]), row(wl["scale2"]), row(wl["shift2"]), "norm2_fwd",
                       m=mixed, gate=row(wl["gate1"]), duty=take(13))
    uu = _matmul(h2, _weight(wl, "w_up_t"), "nt", "up_proj", duty=take(42))
    act = _ffn_act_fwd(uu, wl["w_conv"], row(wl["b_conv"]), duty=take(25))
    y = _matmul(act, _weight(wl, "w_down"), "nn", "down_proj", tk=1408, duty=take(24))
    x2 = _axpy(x1, y, row(wl["gate2"]), "resid2", duty=take(11))
    saved = dict(x0=x0, h=h, p=p, qk=qk, ret_raw=ret_raw, br0=br0, gla_raw=gla_raw, states=states, br1=br1,
                 qkn=qkn, cumcol=cumcol, cumrow=cumrow, fox_o=fox_o, lse=lse, y0=ys[0], y1=ys[1], y2=ys[2],
                 gpre=gpre, mixed_in=mixed_in, mixed=mixed, x1=x1, h2=h2, uu=uu, act=act, y=y)
    return x2, saved


def _layer_bwd(dx2, wl, sv, consts, xfer=None, layer=0):
    cosf, sinf, logg = consts
    row = lambda a: a.reshape(1, -1)
    take = _taker(xfer)

    send = functools.partial(_send_grad, xfer, layer)

    dy, st_g2 = _gate_bwd(dx2, sv["y"], row(wl["gate2"]), "gate2_bwd", duty=take(10))
    dact = _matmul(dy, _weight(wl, "w_down"), "nt", "down_dx", tn=1408, duty=take(21))
    d_down = send("w_down", _matmul(sv["act"], dy, "tn", "down_dw", out_dtype=BF16, tm=1408, duty=take(19)))
    duu, st_conv = _ffn_act_bwd(sv["uu"], wl["w_conv"], row(wl["b_conv"]), dact, duty=take(40))
    dh2 = _matmul(duu, _weight(wl, "w_up_t"), "nn", "up_dx", tk=1408, duty=take(42))
    d_up_t = send("w_up", _matmul(duu, sv["h2"], "tn", "up_dw", out_dtype=BF16, tm=1408, duty=take(33)))
    dx1, st_n2 = _norm_bwd(sv["x1"], dh2, dx2, row(wl["norm2_g"]), row(wl["scale2"]), row(wl["shift2"]), "norm2_bwd",
                           duty=take(15))
    dmixed, st_g1 = _gate_bwd(dx1, sv["mixed"], row(wl["gate1"]), "gate1_bwd", duty=take(10))
    dmi = _matmul(dmixed, _weight(wl, "w_o"), "nt", "o_dx", duty=take(11))
    d_o = send("w_o", _matmul(sv["mixed_in"], dmixed, "tn", "o_dw", out_dtype=BF16, duty=take(9)))
    dy0, dy1, dy2, dgpre, st_bmg = _mix_bwd(sv["gpre"], row(wl["b_mg"]), sv["y0"], sv["y1"], sv["y2"], dmi,
                                             duty=take(31))
    brs = (sv["br0"], sv["br1"], sv["fox_o"])
    w_br_t = _weight(wl, "w_br_t")
    dbr = [_matmul(d, w_br_t[n], "nn", "br_dx%d" % n) for n, d in enumerate((dy0, dy1, dy2))]
    d_br_t = [send("w_br%d" % n, _matmul(d, brs[n], "tn", "br_dw%d" % n, out_dtype=BF16))
              for n, d in enumerate((dy0, dy1, dy2))]
    dh = _matmul(dgpre, _weight(wl, "w_mg_t"), "nn", "gate_dx", tk=1024, duty=take(29))
    d_mg_t = send("w_mg", _matmul(dgpre, sv["h"], "tn", "gate_dw", out_dtype=BF16, duty=take(21)))
    p = sv["p"]
    dqn, dkn, dfv, drow, dcol = _fox_bwd(sv["qkn"], p, sv["cumcol"], sv["cumrow"], sv["lse"], sv["fox_o"], dbr[2],
                                         duty=take(50))
    dcum4 = drow.reshape(N_HEADS, T) - dcol.reshape(N_HEADS, T)
    dcum = jnp.pad(dcum4.T, ((0, 0), (FF_LANE0, 128 - FF_LANE0 - N_HEADS)))
    dfqk, dtail_fox, st_fox = _fox_prep_bwd(p, row(wl["q_norm_g"]), row(wl["k_norm_g"]), row(wl["btail"]), dqn, dkn, dcum,
                                            duty=take(15))
    dgla_raw, dgg, st_gn = _branch_post_bwd(sv["gla_raw"], p, row(wl["gla_norm_g"]), dbr[1], GG_BLK, False, "gla_post_bwd",
                                            duty=take(12))
    dgq, dgk, dgv, dtail_gla, dw2pad, st_bg = _gla_bwd(p, wl["w2pad"], row(wl["b_gla_a"]), sv["states"], dgla_raw,
                                                       duty=take(30))
    dret_raw, drg, st_rn = _branch_post_bwd(sv["ret_raw"], p, row(wl["ret_norm_g"]), dbr[0], RG_BLK, True, "ret_post_bwd",
                                            duty=take(13))
    dqr, dkr, drv = _ret_bwd(sv["qk"], p, logg, dret_raw, duty=take(50))
    drqk = _rope_bwd(dqr, dkr, cosf, sinf, duty=take(11))
    dp = jnp.concatenate([a.astype(BF16) for a in (drqk, drv, drg, dgq, dgk, dgv, dgg, dfqk, dfv, dtail_fox + dtail_gla)]
                         + [jnp.zeros((T, NP - TAIL0 - 128), BF16)], axis=1)
    dh = _matmul(dp, _weight(wl, "w_in"), "nt", "in_dx", tk=1408, add=dh, duty=take(45))
    d_in = send("w_in", _matmul(sv["h"], dp, "tn", "in_dw", out_dtype=BF16, duty=take(32)))
    dx0, st_n1 = _norm_bwd(sv["x0"], dh, dx1, row(wl["norm1_g"]), row(wl["scale1"]), row(wl["shift1"]), "norm1_bwd",
                           duty=take(15))
    big = dict(w_in=d_in, w_o=d_o, w_down=d_down, w_br0=d_br_t[0], w_br1=d_br_t[1], w_br2=d_br_t[2], w_mg=d_mg_t,
               w_up=d_up_t)
    dmod = jnp.concatenate([st_n1[2], st_n1[1], st_g1[0], st_n2[2], st_n2[1], st_g2[0]])
    small = dict(norm1_g=st_n1[0], norm2_g=st_n2[0], b_gla_a=st_bg[0], b_fox_f=st_fox[2, FF_LANE0:FF_LANE0 + N_HEADS],
                 ret_norm_g=st_rn[0], gla_norm_g=st_gn[0], q_norm_g=st_fox[0], k_norm_g=st_fox[1], b_mg=st_bmg[0],
                 b_conv=st_conv[3], w_gla_a2=dw2pad[:LR_LANES], w_conv=st_conv[0:3])
    return dx0, big, dmod, small


SMALL_REPL = ("norm1_g", "norm2_g", "b_ada", "b_gla_a", "b_fox_f", "ret_norm_g", "gla_norm_g", "q_norm_g", "k_norm_g",
              "b_mg", "b_conv")
SMALL_SHARDED = ("w_gla_a2", "w_conv")
BIG = ("w_in", "w_o", "w_down", "w_br", "w_mg", "w_up")
WEIGHTS = ("norm1_g", "norm2_g", "w_ada", "b_ada", "w_in", "w_gla_a2", "b_gla_a", "b_fox_f", "ret_norm_g", "gla_norm_g",
           "q_norm_g", "k_norm_g", "w_br", "w_mg", "b_mg", "w_o", "w_up", "w_conv", "b_conv", "w_down")


def kernel(x, c, norm1_g, norm2_g, w_ada, b_ada, w_in, w_gla_a2, b_gla_a, b_fox_f, ret_norm_g, gla_norm_g, q_norm_g, k_norm_g, w_br, w_mg, b_mg, w_o, w_up, w_conv, b_conv, w_down, loss_target, m_norm1_g, m_norm2_g, m_w_ada, m_b_ada, m_w_in, m_w_gla_a2, m_b_gla_a, m_b_fox_f, m_ret_norm_g, m_gla_norm_g, m_q_norm_g, m_k_norm_g, m_w_br, m_w_mg, m_b_mg, m_w_o, m_w_up, m_w_conv, m_b_conv, m_w_down, v_norm1_g, v_norm2_g, v_w_ada, v_b_ada, v_w_in, v_w_gla_a2, v_b_gla_a, v_b_fox_f, v_ret_norm_g, v_gla_norm_g, v_q_norm_g, v_k_norm_g, v_w_br, v_w_mg, v_b_mg, v_w_o, v_w_up, v_w_conv, v_b_conv, v_w_down):
    W = dict(norm1_g=norm1_g, norm2_g=norm2_g, w_ada=w_ada, b_ada=b_ada, w_in=w_in, w_gla_a2=w_gla_a2, b_gla_a=b_gla_a,
             b_fox_f=b_fox_f, ret_norm_g=ret_norm_g, gla_norm_g=gla_norm_g, q_norm_g=q_norm_g, k_norm_g=k_norm_g,
             w_br=w_br, w_mg=w_mg, b_mg=b_mg, w_o=w_o, w_up=w_up, w_conv=w_conv, b_conv=b_conv, w_down=w_down)
    M = dict(norm1_g=m_norm1_g, norm2_g=m_norm2_g, w_ada=m_w_ada, b_ada=m_b_ada, w_in=m_w_in, w_gla_a2=m_w_gla_a2,
             b_gla_a=m_b_gla_a, b_fox_f=m_b_fox_f, ret_norm_g=m_ret_norm_g, gla_norm_g=m_gla_norm_g, q_norm_g=m_q_norm_g,
             k_norm_g=m_k_norm_g, w_br=m_w_br, w_mg=m_w_mg, b_mg=m_b_mg, w_o=m_w_o, w_up=m_w_up, w_conv=m_w_conv,
             b_conv=m_b_conv, w_down=m_w_down)
    V = dict(norm1_g=v_norm1_g, norm2_g=v_norm2_g, w_ada=v_w_ada, b_ada=v_b_ada, w_in=v_w_in, w_gla_a2=v_w_gla_a2,
             b_gla_a=v_b_gla_a, b_fox_f=v_b_fox_f, ret_norm_g=v_ret_norm_g, gla_norm_g=v_gla_norm_g, q_norm_g=v_q_norm_g,
             k_norm_g=v_k_norm_g, w_br=v_w_br, w_mg=v_w_mg, b_mg=v_b_mg, w_o=v_w_o, w_up=v_w_up, w_conv=v_w_conv,
             b_conv=v_b_conv, w_down=v_w_down)
    me = 4 * lax.axis_index("x") + 2 * lax.axis_index("y") + lax.axis_index("c")
    x2d, tgt = x.reshape(T, D), loss_target.reshape(T, D)

    sm = _flat_pack([c, w_gla_a2, w_conv])
    sm_all = _exchange(sm, True, "gather_small")
    parts = [_flat_unpack(sm_all[j], [(D,), (DEPTH, LR_LANES, 32), (DEPTH, 3, 352)]) for j in range(N_DEV)]
    c_all = jnp.stack([q[0] for q in parts])
    w_gla_full = jnp.concatenate([q[1] for q in parts], axis=2)
    w_conv_full = jnp.concatenate([q[2] for q in parts], axis=2)

    n_ada = w_ada.shape[2]
    b_loc = lax.dynamic_slice_in_dim(b_ada, me * n_ada, n_ada, axis=1).reshape(DEPTH, 1, n_ada)
    mod_all = _ada_fwd(c_all, w_ada, b_loc)
    mod_recv = _exchange(jnp.swapaxes(mod_all, 0, 1), False, "a2a_mod")
    mod = jnp.swapaxes(mod_recv, 0, 1).reshape(DEPTH, 6, D)

    loc = dict(w_in=_permute_in(w_in), w_o=w_o, w_down=w_down, w_br=jnp.swapaxes(w_br, 2, 3),
               w_mg=jnp.swapaxes(w_mg, 1, 2), w_up=jnp.swapaxes(w_up, 1, 2))
    loc = {k: v.astype(BF16) for k, v in loc.items()}
    w_full = dict(w_in=(D, NP), w_o=(D, D), w_down=(D_FF, D), w_br=(3, D, BW), w_mg=(3 * D, D), w_up=(2 * D_FF, D))
    w_parts = dict(w_in=8, w_br=2, w_mg=4, w_o=1, w_up=11, w_down=2)
    gather, units = _Transfers("gather"), []
    for l in range(DEPTH):
        for k, parts in w_parts.items():
            axis = 1 if k == "w_br" else 0
            n = w_full[k][axis] // N_DEV
            gather.lands[(l, k)] = lax.empty(w_full[k], BF16)
            shard = loc[k][l]
            nbytes = shard.size * 2 // parts
            units += [((l, k), (axis, n, (), c * (n // parts), n // parts), shard, nbytes) for c in range(parts)]
    first, lag = w_parts["w_in"], 4
    order = [("cross", i) for i in range(first)] + [("pass", i) for i in range(first)]
    for i in range(first, len(units) + lag):
        order += [("cross", i)] if i < len(units) else []
        order += [("pass", i - lag)] if i - lag >= first else []
    for what, i in order:
        key, where, shard, nbytes = units[i]
        if what == "cross":
            gather.add(key, ("gather_chip",) + where, shard, uid=i, nbytes=nbytes)
        else:
            gather.add(key, ("pass_on",) + where, after=i, nbytes=nbytes)

    w2pad = jnp.pad(w_gla_full, ((0, 0), (0, 128 - LR_LANES), (0, 0)))
    btail = jnp.pad(b_fox_f, ((0, 0), (FF_LANE0, 128 - FF_LANE0 - N_HEADS)))
    stacked = dict(norm1_g=norm1_g, norm2_g=norm2_g, b_gla_a=b_gla_a, ret_norm_g=ret_norm_g, gla_norm_g=gla_norm_g,
                   q_norm_g=q_norm_g, k_norm_g=k_norm_g, b_mg=b_mg, b_conv=b_conv, w_conv=w_conv_full, w2pad=w2pad,
                   btail=btail, shift1=mod[:, 0], scale1=mod[:, 1], gate1=mod[:, 2], shift2=mod[:, 3], scale2=mod[:, 4],
                   gate2=mod[:, 5])
    landed = lambda l, k: functools.partial(gather.get, (l, k))
    layers = [dict({k: v[l] for k, v in stacked.items()}, w_in=landed(l, "w_in"), w_o=landed(l, "w_o"),
                   w_down=landed(l, "w_down"), w_br_t=landed(l, "w_br"), w_mg_t=landed(l, "w_mg"), w_up_t=landed(l, "w_up"))
              for l in range(DEPTH)]
    consts = _rope_tables() + (_ret_logg(),)

    xc, saved = x2d, []
    for l in range(DEPTH):
        xc, sv = _layer_fwd(xc, layers[l], consts, gather, 1.5 if l == 0 else 1.25)
        saved.append(sv)
    loss_part, dxc = _loss_fwd_bwd(xc, tgt)
    loss = lax.psum(loss_part[0, 0], ("x", "y", "c"))

    grad_names = ("w_in", "w_o", "w_down", "w_br0", "w_br1", "w_br2", "w_mg", "w_up")
    blk_rows = dict(w_in=(128, NP), w_o=(128, D), w_down=(352, D), w_br0=(128, BW), w_br1=(128, BW), w_br2=(128, BW),
                    w_mg=(384, D), w_up=(704, D))
    grads = _Transfers("grads")
    for k in grad_names:
        grads.lands[k] = lax.empty((N_DEV // 2 if k in GRAD_VIA_CHIP else N_DEV, DEPTH) + blk_rows[k], BF16)
    dmod, small_g = [None] * DEPTH, [None] * DEPTH
    for l in reversed(range(DEPTH)):
        dxc, _, dmod[l], small_g[l] = _layer_bwd(dxc, layers[l], saved[l], consts, grads, l)
    grad_x = dxc
    dmod = jnp.stack(dmod)
    small_g = {k: jnp.stack([s[k] for s in small_g]) for k in small_g[0]}
    grads.drain()
    recv = {k: grads.get(k) for k in grad_names}

    dmod_send = jnp.swapaxes(dmod.reshape(DEPTH, N_DEV, n_ada), 0, 1)
    dmod_all = jnp.swapaxes(_exchange(dmod_send, False, "a2a_dmod"), 0, 1)
    g_ada = _ada_bwd(c_all, dmod_all)

    def flat(a, k):
        return a.reshape((-1, W[k].shape[-1]))

    def adam_nat(k, g, tr):
        outs = _adamw(g, flat(W[k], k), flat(M[k], k), flat(V[k], k), tr, "adamw_" + k)
        return [o.reshape(W[k].shape) for o in outs]

    def summed(k, tr):
        r = recv[k]
        return _sum_partials(r.reshape(r.shape[0], DEPTH * r.shape[2], r.shape[3]), "sum_" + k, tr).reshape((DEPTH,) + r.shape[2:])

    big_out = dict(
        w_in=adam_nat("w_in", flat(_unpermute_in(summed("w_in", 64)), "w_in"), 64),
        w_o=adam_nat("w_o", recv["w_o"].reshape(N_DEV, DEPTH * 128, D), 128),
        w_down=adam_nat("w_down", recv["w_down"].reshape(N_DEV // 2, DEPTH * 352, D), 352),
        w_br=adam_nat("w_br", flat(jnp.swapaxes(jnp.stack([summed("w_br%d" % n, 128) for n in range(3)], axis=1), 2, 3),
                                   "w_br"), 1024),
        w_mg=adam_nat("w_mg", flat(jnp.swapaxes(summed("w_mg", 384), 1, 2), "w_mg"), 512),
        w_up=adam_nat("w_up", flat(jnp.swapaxes(summed("w_up", 704), 1, 2), "w_up"), 512))
    ada_out = [o.reshape(DEPTH, D, n_ada) for o in _adamw(
        g_ada.reshape(DEPTH * D, n_ada), w_ada.reshape(DEPTH * D, n_ada), m_w_ada.reshape(DEPTH * D, n_ada),
        v_w_ada.reshape(DEPTH * D, n_ada), 512, "adamw_ada")]

    small_g = dict(small_g, b_ada=dmod)
    names = SMALL_REPL + SMALL_SHARDED
    full_shapes = [W[n].shape for n in SMALL_REPL] + [(DEPTH, LR_LANES, 256), (DEPTH, 3, D_FF)]
    part = _flat_pack([small_g[n] for n in names])
    total = _flat_unpack(_sum_partials(_exchange(part, True, "gather_small_grads"), "sum_small"), full_shapes)
    total = dict(zip(names, total))
    total["w_gla_a2"] = lax.dynamic_slice_in_dim(total["w_gla_a2"], me * 32, 32, axis=2)
    total["w_conv"] = lax.dynamic_slice_in_dim(total["w_conv"], me * 352, 352, axis=2)
    shapes = [W[n].shape for n in names]
    small_out = _adamw(_flat_pack([total[n] for n in names]), _flat_pack([W[n] for n in names]),
                       _flat_pack([M[n] for n in names]), _flat_pack([V[n] for n in names]), None, "adamw_small")
    small_out = [dict(zip(names, _flat_unpack(o, shapes))) for o in small_out]

    outs = []
    for k in range(4):
        d = dict(small_out[k])
        d.update({n: big_out[n][k] for n in BIG})
        d["w_ada"] = ada_out[k]
        outs.append([d[n] for n in WEIGHTS])
    return (loss, grad_x.reshape(1, T, D), *outs[0], *outs[1], *outs[2], *outs[3])
```

```python
import functools

import numpy as np
import jax
import jax.numpy as jnp
from jax import lax
from jax.experimental import pallas as pl
from jax.experimental.pallas import tpu as pltpu

F32 = jnp.float32
BF16 = jnp.bfloat16

N_DEV = 8
T = 2048
D = 1024
DEPTH = 4
N_HEADS = 4
HD = 128
BW = 512
D_FF = 2816
CHUNK = 64
EPS = 1e-6
IN_W = 5140
NP = 5632
TAIL0 = 5120
LR_LANES = 16
FF_LANE0 = 16
PACK_W = 1024
SEG_ROWS = (704, 128, 352, 192, 384, 704)
LAYER_ROWS = sum(SEG_ROWS)
VMEM_LIMIT_V7X = 56 * 1024 * 1024

ADAM_LR, ADAM_B1, ADAM_B2, ADAM_EPS, ADAM_WD, ADAM_STEP = 0.001, 0.9, 0.999, 1e-08, 0.01, 10

MESH_ID = pl.DeviceIdType.MESH


def _cp(*sem):
    return pltpu.CompilerParams(dimension_semantics=sem if sem else None, vmem_limit_bytes=VMEM_LIMIT_V7X)


def _sigmoid(z):
    return 1.0 / (1.0 + jnp.exp(-z))


def _log_sigmoid(z):
    return jnp.minimum(z, 0.0) - jnp.log(1.0 + jnp.exp(-jnp.abs(z)))


def _sum0(a):
    return jnp.sum(a, axis=0, keepdims=True)


def _mean1(a):
    return jnp.mean(a, axis=-1, keepdims=True)


def _dot(a, b, dims):
    return lax.dot_general(a.astype(BF16), b.astype(BF16), (dims, ((), ())), preferred_element_type=F32)


NN = ((1,), (0,))
NT = ((1,), (1,))
TN = ((0,), (0,))


def _exact_dot(m01, a):
    a1 = a.astype(BF16)
    r1 = a - a1.astype(F32)
    a2 = r1.astype(BF16)
    a3 = (r1 - a2.astype(F32)).astype(BF16)
    d = lambda z: jnp.dot(m01, z, preferred_element_type=F32)
    return d(a1) + d(a2) + d(a3)


def _tri(n, upper):
    r = lax.broadcasted_iota(jnp.int32, (n, n), 0)
    c = lax.broadcasted_iota(jnp.int32, (n, n), 1)
    return jnp.where((c >= r) if upper else (c <= r), 1.0, 0.0).astype(BF16)


def _exchange(x, gather, name):
    blk = x.shape if gather else x.shape[1:]

    def body(x_ref, o_ref, send_sems, recv_sems, loc_sem):
        mx, my, mc = lax.axis_index("x"), lax.axis_index("y"), lax.axis_index("c")
        me = 4 * mx + 2 * my + mc
        loc = pltpu.make_async_copy(x_ref if gather else x_ref.at[me], o_ref.at[me], loc_sem)
        loc.start()
        copies = []
        for k in range(1, N_DEV):
            px = mx ^ (k >> 2) if (k >> 2) else mx
            py = my ^ ((k >> 1) & 1) if ((k >> 1) & 1) else my
            pc = mc ^ (k & 1) if (k & 1) else mc
            peer = 4 * px + 2 * py + pc
            cp = pltpu.make_async_remote_copy(
                src_ref=x_ref if gather else x_ref.at[peer], dst_ref=o_ref.at[me],
                send_sem=send_sems.at[k - 1], recv_sem=recv_sems.at[k - 1],
                device_id=(px, py, pc), device_id_type=MESH_ID)
            cp.start()
            copies.append(cp)
        for cp in copies:
            cp.wait()
        loc.wait()

    return pl.pallas_call(
        body, name=name,
        out_shape=jax.ShapeDtypeStruct((N_DEV,) + tuple(blk), x.dtype),
        in_specs=[pl.BlockSpec(memory_space=pl.ANY)],
        out_specs=pl.BlockSpec(memory_space=pl.ANY),
        scratch_shapes=[pltpu.SemaphoreType.DMA((N_DEV - 1,)), pltpu.SemaphoreType.DMA((N_DEV - 1,)),
                        pltpu.SemaphoreType.DMA],
        compiler_params=pltpu.CompilerParams(has_side_effects=True),
    )(x)


def _blk(ref, axis, j, n, r0=0, nr=None):
    return ref.at[(slice(None),) * axis + (pl.ds(j * n + r0, n if nr is None else nr),)]


def _comm_copies(items, srcs, lands, send_sems, recv_sems, loc_sems):
    mx, my, mc = lax.axis_index("x"), lax.axis_index("y"), lax.axis_index("c")
    me = 4 * mx + 2 * my + mc
    local, remote = [], []
    for t, (kind, axis, n, sel, r0, nr, si, li) in enumerate(items):
        if kind == "pass_on":
            for q in (2, 4, 6):
                px = 1 - mx if q & 4 else mx
                py = 1 - my if q & 2 else my
                rows = _blk(lands[li], axis, 4 * px + 2 * py + mc, n, r0, nr)
                remote.append(pltpu.make_async_remote_copy(
                    src_ref=rows, dst_ref=rows, send_sem=send_sems.at[t * (N_DEV - 1) + q - 1],
                    recv_sem=recv_sems.at[t * (N_DEV - 1) + q - 1], device_id=(mx, my, 1 - mc), device_id_type=MESH_ID))
            continue
        if kind == "to_other_core":
            for p in range(N_DEV // 2):
                remote.append(pltpu.make_async_remote_copy(
                    src_ref=_blk(srcs[si], axis, 2 * p + 1 - mc, n, r0, nr), dst_ref=lands[li].at[p, pl.ds(r0, nr)],
                    send_sem=send_sems.at[t * (N_DEV - 1) + p], recv_sem=recv_sems.at[t * (N_DEV - 1) + p],
                    device_id=(mx, my, 1 - mc), device_id_type=MESH_ID))
            continue
        if kind == "a2a_chip":
            pm = 2 * mx + my
            mine = lands[li].at[(pm,) + tuple(sel) + (pl.ds(r0, nr),)]
            local.append(pltpu.make_async_copy(srcs[si].at[pm, pl.ds(r0, nr)], mine, loc_sems.at[t]))
            for q in (2, 4, 6):
                px = 1 - mx if q & 4 else mx
                py = 1 - my if q & 2 else my
                remote.append(pltpu.make_async_remote_copy(
                    src_ref=srcs[si].at[2 * px + py, pl.ds(r0, nr)], dst_ref=mine,
                    send_sem=send_sems.at[t * (N_DEV - 1) + q - 1], recv_sem=recv_sems.at[t * (N_DEV - 1) + q - 1],
                    device_id=(px, py, mc), device_id_type=MESH_ID))
            continue
        if kind == "a2a":
            mine = lands[li].at[(me,) + tuple(sel) + (pl.ds(r0, nr),)]
            own = _blk(srcs[si], axis, me, n, r0, nr)
        else:
            mine = _blk(lands[li], axis, me, n, r0, nr)
            own = _blk(srcs[si], axis, 0, n, r0, nr)
        local.append(pltpu.make_async_copy(own, mine, loc_sems.at[t]))
        for k in ((1, 2, 4, 6) if kind == "gather_chip" else range(1, N_DEV)):
            px = 1 - mx if k & 4 else mx
            py = 1 - my if k & 2 else my
            pc = 1 - mc if k & 1 else mc
            src = _blk(srcs[si], axis, 4 * px + 2 * py + pc, n, r0, nr) if kind == "a2a" else own
            remote.append(pltpu.make_async_remote_copy(
                src_ref=src, dst_ref=mine, send_sem=send_sems.at[t * (N_DEV - 1) + k - 1],
                recv_sem=recv_sems.at[t * (N_DEV - 1) + k - 1], device_id=(px, py, pc), device_id_type=MESH_ID))
    return local, remote


def _comm_scratch(n_items):
    return [pltpu.SemaphoreType.DMA((n_items * (N_DEV - 1),)), pltpu.SemaphoreType.DMA((n_items * (N_DEV - 1),)),
            pltpu.SemaphoreType.DMA((n_items,))]


LINK_BYTES_PER_US = dict(gather_chip=23e3, a2a_chip=23e3, a2a=11.5e3, gather=11.5e3, pass_on=200e3, to_other_core=150e3)
CALL_EXCHANGE_US = 3.0


class _Duty:
    def __init__(self, items, srcs, lands, done):
        self.items, self.srcs, self.lands, self.done = items, srcs, lands, done


def _pcall(body, name, grid, in_specs, out_specs, out_shape, args, scratch_shapes=(), sem=(), duty=None):
    if duty is None:
        return pl.pallas_call(body, name=name, grid=grid, in_specs=list(in_specs), out_specs=out_specs,
                              out_shape=out_shape, scratch_shapes=list(scratch_shapes), compiler_params=_cp(*sem))(*args)
    single = not isinstance(out_shape, (list, tuple))
    o_shape = [out_shape] if single else list(out_shape)
    o_specs = [out_specs] if single else list(out_specs)
    n_in, n_out, n_scr = len(in_specs), len(o_shape), len(scratch_shapes)
    n_src, n_land, n_items = len(duty.srcs), len(duty.lands), len(duty.items)
    a0 = n_in + n_src + n_land

    def wrapped(*refs):
        srcs = refs[n_in:n_in + n_src]
        lands = refs[a0 + n_out:a0 + n_out + n_land]
        core = refs[:n_in] + refs[a0:a0 + n_out] + refs[a0 + n_out + n_land:a0 + n_out + n_land + n_scr]
        sems = refs[a0 + n_out + n_land + n_scr:]
        first = functools.reduce(jnp.logical_and, [pl.program_id(a) == 0 for a in range(len(grid))])
        last = functools.reduce(jnp.logical_and, [pl.program_id(a) == g - 1 for a, g in enumerate(grid)])

        @pl.when(first)
        def _():
            local, remote = _comm_copies(duty.items, srcs, lands, *sems)
            for cp in local + remote:
                cp.start()

        body(*core)

        @pl.when(last)
        def _():
            local, remote = _comm_copies(duty.items, srcs, lands, *sems)
            for cp in remote + local:
                cp.wait()

    hbm = pl.BlockSpec(memory_space=pl.ANY)
    res = pl.pallas_call(
        wrapped, name=name, grid=grid,
        in_specs=list(in_specs) + [hbm] * (n_src + n_land), out_specs=o_specs + [hbm] * n_land,
        out_shape=o_shape + [jax.ShapeDtypeStruct(a.shape, a.dtype) for a in duty.lands],
        input_output_aliases={n_in + n_src + t: n_out + t for t in range(n_land)},
        scratch_shapes=list(scratch_shapes) + _comm_scratch(n_items),
        compiler_params=pltpu.CompilerParams(dimension_semantics=("arbitrary",) * len(grid),
                                             vmem_limit_bytes=VMEM_LIMIT_V7X, has_side_effects=True),
    )(*args, *duty.srcs, *duty.lands)
    duty.done(res[n_out:])
    return res[0] if single else res[:n_out]


def _comm(duty, name):
    n_src, n_land = len(duty.srcs), len(duty.lands)

    def body(*refs):
        local, remote = _comm_copies(duty.items, refs[:n_src], refs[n_src + n_land:n_src + 2 * n_land],
                                     *refs[n_src + 2 * n_land:])
        for cp in local + remote:
            cp.start()
        for cp in remote + local:
            cp.wait()

    hbm = pl.BlockSpec(memory_space=pl.ANY)
    duty.done(pl.pallas_call(
        body, name=name, out_shape=[jax.ShapeDtypeStruct(a.shape, a.dtype) for a in duty.lands],
        in_specs=[hbm] * (n_src + n_land), out_specs=[hbm] * n_land,
        input_output_aliases={n_src + t: t for t in range(n_land)},
        scratch_shapes=_comm_scratch(len(duty.items)), compiler_params=pltpu.CompilerParams(has_side_effects=True),
    )(*duty.srcs, *duty.lands))


class _Transfers:
    def __init__(self, name):
        self.name, self.queue, self.lands, self.flushes, self.groups = name, [], {}, 0, {}

    def add(self, key, item, src=None, uid=None, after=None, group=None, nbytes=0):
        self.queue.append((key, item, src, uid, after, group, nbytes / LINK_BYTES_PER_US[item[0]]))
        if group is not None:
            self.groups[group] = [self.groups.get(group, [0, None])[0] + 1, None]

    def when_done(self, group, fn):
        self.groups[group][1] = fn

    def take_for(self, us):
        count, busy = 0, CALL_EXCHANGE_US
        while count < len(self.queue) and busy + self.queue[count][6] <= us:
            busy += self.queue[count][6]
            count += 1
        return self.take(count) if count else None

    def take(self, count):
        units = []
        while self.queue and len(units) < count:
            after = self.queue[0][4]
            if after is not None and any(u[3] == after for u in units):
                break
            units.append(self.queue.pop(0))
        if not units:
            return None
        keys, srcs, items = [], [], []
        for key, item, src, _, _, _, _ in units:
            if key not in keys:
                keys.append(key)
            if src is not None and not any(src is s for s in srcs):
                srcs.append(src)
            si = [i for i, s in enumerate(srcs) if s is src][0] if src is not None else -1
            items.append(tuple(item) + (si, keys.index(key)))

        def done(new_lands):
            for key, arr in zip(keys, new_lands):
                self.lands[key] = arr
            for u in units:
                if u[5] is not None:
                    self.groups[u[5]][0] -= 1
                    if self.groups[u[5]][0] == 0:
                        self.groups[u[5]][1]()

        return _Duty(items, srcs, [self.lands[k] for k in keys], done)

    def drain(self, upto=None):
        count = upto
        while self.queue if upto is None else count > 0:
            duty = self.take(len(self.queue) if upto is None else count)
            count = None if upto is None else count - len(duty.items)
            self.flushes += 1
            _comm(duty, "%s_flush%d" % (self.name, self.flushes))

    def get(self, key):
        pending = [i for i, u in enumerate(self.queue) if u[0] == key]
        if pending:
            self.drain(pending[-1] + 1)
        return self.lands[key]


def _matmul(a, b, mode, name, out_dtype=F32, tm=1024, tn=512, tk=None, add=None, duty=None):
    halves = a.ndim == 3
    if mode == "tn":
        K, M = a.shape[-2], a.shape[-1] * (2 if halves else 1)
        N = b.shape[1]
    else:
        M, K = a.shape[-2], a.shape[-1] * (2 if halves else 1)
        N = b.shape[0] if mode == "nt" else b.shape[1]
    tm, tn = min(tm, M), min(tn, N)
    tk = K if tk is None else tk
    nk = K // tk
    assert M % tm == 0 and N % tn == 0 and K % tk == 0, (name, M, N, K, tm, tn, tk)
    dims = {"nn": NN, "nt": NT, "tn": TN}[mode]
    has_add = add is not None

    def body(*refs):
        a_ref, b_ref = refs[:2]
        add_ref = refs[2] if has_add else None
        o_ref = refs[3 if has_add else 2]
        part = _dot(a_ref[...], b_ref[...], dims)

        def finish(total):
            if has_add:
                total = total + add_ref[...]
            o_ref[...] = total.astype(o_ref.dtype)

        if nk == 1:
            finish(part)
            return
        acc_ref = refs[-1]
        k = pl.program_id(2)

        @pl.when(k == 0)
        def _():
            acc_ref[...] = part

        @pl.when((k > 0) & (k < nk - 1))
        def _():
            acc_ref[...] += part

        @pl.when(k == nk - 1)
        def _():
            finish(acc_ref[...] + part)

    if halves and mode == "tn":
        per = a.shape[-1] // tm
        a_spec = pl.BlockSpec((None, tk, tm), lambda i, j, k: (i // per, k, i % per))
    elif halves:
        per = a.shape[-1] // tk
        a_spec = pl.BlockSpec((None, tm, tk), lambda i, j, k: (k // per, i, k % per))
    elif mode == "tn":
        a_spec = pl.BlockSpec((tk, tm), lambda i, j, k: (k, i))
    else:
        a_spec = pl.BlockSpec((tm, tk), lambda i, j, k: (i, k))
    if mode == "nt":
        b_spec = pl.BlockSpec((tn, tk), lambda i, j, k: (j, k))
    else:
        b_spec = pl.BlockSpec((tk, tn), lambda i, j, k: (k, j))
    o_spec = pl.BlockSpec((tm, tn), lambda i, j, k: (i, j))
    in_specs = [a_spec, b_spec] + ([o_spec] if has_add else [])
    args = (a, b) + ((add,) if has_add else ())
    return _pcall(
        body, name=name, grid=(M // tm, N // tn, nk),
        out_shape=jax.ShapeDtypeStruct((M, N), out_dtype),
        in_specs=in_specs, out_specs=o_spec,
        scratch_shapes=[pltpu.VMEM((tm, tn), F32)] if nk > 1 else [],
        sem=("parallel", "parallel", "arbitrary"), args=args, duty=duty)


def _ada_fwd(c_all, w_ada, b_loc):
    n = w_ada.shape[2]

    def body(c_ref, w_ref, b_ref, o_ref):
        c = c_ref[...]
        o_ref[0] = _dot(c * _sigmoid(c), w_ref[0], NN) + b_ref[0]

    return pl.pallas_call(
        body, name="ada_fwd", grid=(DEPTH,),
        out_shape=jax.ShapeDtypeStruct((DEPTH, N_DEV, n), F32),
        in_specs=[pl.BlockSpec((N_DEV, D), lambda l: (0, 0)),
                  pl.BlockSpec((1, D, n), lambda l: (l, 0, 0)),
                  pl.BlockSpec((1, 1, n), lambda l: (l, 0, 0))],
        out_specs=pl.BlockSpec((1, N_DEV, n), lambda l: (l, 0, 0)),
        compiler_params=_cp("parallel"),
    )(c_all, w_ada, b_loc)


def _ada_bwd(c_all, dmod_all):
    n = dmod_all.shape[2]

    def body(c_ref, d_ref, o_ref):
        c = c_ref[...]
        o_ref[0] = _dot(c * _sigmoid(c), d_ref[0], TN)

    return pl.pallas_call(
        body, name="ada_bwd", grid=(DEPTH,),
        out_shape=jax.ShapeDtypeStruct((DEPTH, D, n), F32),
        in_specs=[pl.BlockSpec((N_DEV, D), lambda l: (0, 0)),
                  pl.BlockSpec((1, N_DEV, n), lambda l: (l, 0, 0))],
        out_specs=pl.BlockSpec((1, D, n), lambda l: (l, 0, 0)),
        compiler_params=_cp("parallel"),
    )(c_all, dmod_all)


ROW_TILE = 256


def _row_spec(w=D, col=0):
    return pl.BlockSpec((ROW_TILE, w), lambda i: (i, col))


def _vec_spec(w=D):
    return pl.BlockSpec((1, w), lambda i: (0, 0))


def _norm_fwd(x, g, scale, shift, name, m=None, gate=None, duty=None):
    has_res = m is not None

    def body(*refs):
        if has_res:
            x_ref, m_ref, gate_ref, g_ref, sc_ref, sh_ref, xo_ref, h_ref = refs
            xv = x_ref[...] + gate_ref[...] * m_ref[...]
            xo_ref[...] = xv
        else:
            x_ref, g_ref, sc_ref, sh_ref, h_ref = refs
            xv = x_ref[...]
        r = lax.rsqrt(_mean1(xv * xv) + EPS)
        h_ref[...] = ((xv * r * g_ref[...]) * (1.0 + sc_ref[...]) + sh_ref[...]).astype(BF16)

    ins = [x] + ([m, gate] if has_res else []) + [g, scale, shift]
    in_specs = [_row_spec()] + ([_row_spec(), _vec_spec()] if has_res else []) + [_vec_spec()] * 3
    out_shape = [jax.ShapeDtypeStruct((T, D), BF16)]
    out_specs = [_row_spec()]
    if has_res:
        out_shape = [jax.ShapeDtypeStruct((T, D), F32)] + out_shape
        out_specs = [_row_spec()] + out_specs
    out = _pcall(body, name=name, grid=(T // ROW_TILE,), out_shape=out_shape, in_specs=in_specs,
                 out_specs=out_specs, sem=("parallel",), args=ins, duty=duty)
    return out if has_res else out[0]


def _norm_bwd(x, dh, dres, g, scale, shift, name, duty=None):
    def body(x_ref, dh_ref, dres_ref, g_ref, sc_ref, sh_ref, dx_ref, st_ref):
        xv, dh_v, gv = x_ref[...], dh_ref[...], g_ref[...]
        r = lax.rsqrt(_mean1(xv * xv) + EPS)
        n = xv * r
        dy = dh_v * (1.0 + sc_ref[...])
        dn = dy * gv
        dx_ref[...] = r * (dn - n * _mean1(dn * n)) + dres_ref[...]

        @pl.when(pl.program_id(0) == 0)
        def _():
            st_ref[...] = jnp.zeros_like(st_ref)

        st_ref[0:1, :] += _sum0(dy * n)
        st_ref[1:2, :] += _sum0(dh_v * (n * gv))
        st_ref[2:3, :] += _sum0(dh_v)

    return _pcall(
        body, name=name, grid=(T // ROW_TILE,),
        out_shape=[jax.ShapeDtypeStruct((T, D), F32), jax.ShapeDtypeStruct((8, D), F32)],
        in_specs=[_row_spec(), _row_spec(), _row_spec(), _vec_spec(), _vec_spec(), _vec_spec()],
        out_specs=[_row_spec(), pl.BlockSpec((8, D), lambda i: (0, 0))],
        sem=("arbitrary",), args=(x, dh, dres, g, scale, shift), duty=duty)


def _axpy(x, m, gate, name, duty=None):
    def body(x_ref, m_ref, gate_ref, o_ref):
        o_ref[...] = x_ref[...] + gate_ref[...] * m_ref[...]

    return _pcall(
        body, name=name, grid=(T // ROW_TILE,), out_shape=jax.ShapeDtypeStruct((T, D), F32),
        in_specs=[_row_spec(), _row_spec(), _vec_spec()], out_specs=_row_spec(),
        sem=("parallel",), args=(x, m, gate), duty=duty)


def _gate_bwd(dx, m, gate, name, duty=None):
    def body(dx_ref, m_ref, gate_ref, dm_ref, st_ref):
        dxv = dx_ref[...]
        dm_ref[...] = (gate_ref[...] * dxv).astype(BF16)

        @pl.when(pl.program_id(0) == 0)
        def _():
            st_ref[...] = jnp.zeros_like(st_ref)

        st_ref[0:1, :] += _sum0(dxv * m_ref[...])

    return _pcall(
        body, name=name, grid=(T // ROW_TILE,),
        out_shape=[jax.ShapeDtypeStruct((T, D), BF16), jax.ShapeDtypeStruct((8, D), F32)],
        in_specs=[_row_spec(), _row_spec(), _vec_spec()],
        out_specs=[_row_spec(), pl.BlockSpec((8, D), lambda i: (0, 0))],
        sem=("arbitrary",), args=(dx, m, gate), duty=duty)


def _loss_fwd_bwd(y, target):
    def body(y_ref, t_ref, l_ref, d_ref):
        e = y_ref[...] - t_ref[...]
        d_ref[...] = e * (1.0 / D)

        @pl.when(pl.program_id(0) == 0)
        def _():
            l_ref[...] = jnp.zeros_like(l_ref)

        l_ref[...] += jnp.sum(_sum0(e * e), axis=1, keepdims=True) * (0.5 / D)

    return pl.pallas_call(
        body, name="loss", grid=(T // ROW_TILE,),
        out_shape=[jax.ShapeDtypeStruct((8, 128), F32), jax.ShapeDtypeStruct((T, D), F32)],
        in_specs=[_row_spec(), _row_spec()],
        out_specs=[pl.BlockSpec((8, 128), lambda i: (0, 0)), _row_spec()],
        compiler_params=_cp("arbitrary"))(y, target)


def _rope_tables():
    half = HD // 2
    inv_freq = 10000.0 ** (-jnp.arange(half, dtype=F32) / half)
    ang = jnp.arange(T, dtype=F32)[:, None] * inv_freq[None, :]
    cos, sin = jnp.cos(ang), jnp.sin(ang)
    return jnp.concatenate([cos, cos], axis=1), jnp.concatenate([-sin, sin], axis=1)


def _rope_fwd(p, cosf, sinf, duty=None):
    def body(p_ref, c_ref, s_ref, o_ref):
        cv, sv = c_ref[...], s_ref[...]
        for j in range(2 * N_HEADS):
            xv = p_ref[:, j * HD:(j + 1) * HD]
            rot = xv * cv + pltpu.roll(xv, HD // 2, 1) * sv
            if j >= N_HEADS:
                rot = rot * (HD ** -0.5)
            o_ref[:, j * HD:(j + 1) * HD] = rot.astype(BF16)

    return _pcall(
        body, name="rope_fwd", grid=(T // ROW_TILE,),
        out_shape=jax.ShapeDtypeStruct((T, 2 * BW), BF16),
        in_specs=[_row_spec(2 * BW), _row_spec(HD), _row_spec(HD)], out_specs=_row_spec(2 * BW),
        sem=("parallel",), args=(p, cosf, sinf), duty=duty)


def _rope_bwd(dq, dk, cosf, sinf, duty=None):
    def body(dq_ref, dk_ref, c_ref, s_ref, o_ref):
        cv, sv = c_ref[...], s_ref[...]
        for j in range(2 * N_HEADS):
            h = j % N_HEADS
            d = dq_ref[:, h * HD:(h + 1) * HD] if j < N_HEADS else dk_ref[:, h * HD:(h + 1) * HD] * (HD ** -0.5)
            o_ref[:, j * HD:(j + 1) * HD] = d * cv + pltpu.roll(d * sv, HD // 2, 1)

    return _pcall(
        body, name="rope_bwd", grid=(T // ROW_TILE,),
        out_shape=jax.ShapeDtypeStruct((T, 2 * BW), F32),
        in_specs=[_row_spec(BW), _row_spec(BW), _row_spec(HD), _row_spec(HD)], out_specs=_row_spec(2 * BW),
        sem=("parallel",), args=(dq, dk, cosf, sinf), duty=duty)


TQ = 256
V_RET_BLK = 8


def _ret_logg():
    lg = jnp.log1p(-jnp.exp2(-5.0 - jnp.arange(N_HEADS, dtype=F32)))
    return jnp.broadcast_to(lg[:, None, None], (N_HEADS, 1, 128))


def _block_iotas(i, kl):
    rows = lax.broadcasted_iota(jnp.int32, (TQ, kl), 0) + i * TQ
    cols = lax.broadcasted_iota(jnp.int32, (TQ, kl), 1)
    return rows, cols


def _ret_weight(lg_ref, i, kl):
    rows, cols = _block_iotas(i, kl)
    dist = jnp.abs(rows - cols).astype(F32)
    w = jnp.exp(dist * lg_ref[0][:, 0:1])
    return jnp.where((cols >> 6) <= (rows >> 6), w, 0.0)


def _per_query_block(i, fn):
    for n in range(1, T // TQ + 1):
        pl.when(i == n - 1)(functools.partial(fn, n * TQ))


def _ret_specs():
    q_spec = pl.BlockSpec((TQ, HD), lambda h, i: (i, h))
    k_spec = pl.BlockSpec((T, HD), lambda h, i: (0, N_HEADS + h))
    v_spec = pl.BlockSpec((T, HD), lambda h, i: (0, V_RET_BLK + h))
    lg_spec = pl.BlockSpec((1, 1, 128), lambda h, i: (h, 0, 0))
    return q_spec, k_spec, v_spec, lg_spec


def _ret_fwd(qk, p, logg, duty=None):
    def body(q_ref, k_ref, v_ref, lg_ref, o_ref):
        i = pl.program_id(1)

        def visible(kl):
            s = _dot(q_ref[...], k_ref[0:kl, :], NT) * _ret_weight(lg_ref, i, kl)
            o_ref[...] = _dot(s, v_ref[0:kl, :], NN)

        _per_query_block(i, visible)

    q_spec, k_spec, v_spec, lg_spec = _ret_specs()
    return _pcall(
        body, name="ret_fwd", grid=(N_HEADS, T // TQ),
        out_shape=jax.ShapeDtypeStruct((T, BW), F32),
        in_specs=[q_spec, k_spec, v_spec, lg_spec], out_specs=q_spec,
        sem=("parallel", "parallel"), args=(qk, qk, p, logg), duty=duty)


def _ret_bwd(qk, p, logg, do, duty=None):
    def body(q_ref, k_ref, v_ref, lg_ref, do_ref, dq_ref, dk_ref, dv_ref):
        i = pl.program_id(1)
        q, dov = q_ref[...], do_ref[...]

        @pl.when(i == 0)
        def _():
            dk_ref[...] = jnp.zeros_like(dk_ref)
            dv_ref[...] = jnp.zeros_like(dv_ref)

        def visible(kl):
            w = _ret_weight(lg_ref, i, kl)
            k = k_ref[0:kl, :]
            s = _dot(q, k, NT) * w
            ds = _dot(dov, v_ref[0:kl, :], NT) * w
            dk_ref[0:kl, :] += _dot(ds, q, TN)
            dv_ref[0:kl, :] += _dot(s, dov, TN)
            dq_ref[...] = _dot(ds, k, NN)

        _per_query_block(i, visible)

    q_spec, k_spec, v_spec, lg_spec = _ret_specs()
    acc_spec = pl.BlockSpec((T, HD), lambda h, i: (0, h))
    sh = jax.ShapeDtypeStruct((T, BW), F32)
    return _pcall(
        body, name="ret_bwd", grid=(N_HEADS, T // TQ),
        out_shape=[sh, sh, sh],
        in_specs=[q_spec, k_spec, v_spec, lg_spec, q_spec], out_specs=[q_spec, acc_spec, acc_spec],
        sem=("parallel", "arbitrary"), args=(qk, qk, p, logg, do), duty=duty)


def _post_norm(xv, gv, centered):
    if centered:
        xv = xv - _mean1(xv)
    r = lax.rsqrt(_mean1(xv * xv) + EPS)
    return xv * r, r


def _branch_post_fwd(raw, p, g, gate_blk, centered, name, duty=None):
    def body(raw_ref, z_ref, g_ref, o_ref):
        for h in range(N_HEADS):
            sl = slice(h * HD, (h + 1) * HD)
            gv = g_ref[:, sl] if centered else g_ref[...]
            xh, _ = _post_norm(raw_ref[:, sl], gv, centered)
            z = z_ref[:, sl]
            o_ref[:, sl] = (z * _sigmoid(z) * (xh * gv)).astype(BF16)

    return _pcall(
        body, name=name, grid=(T // ROW_TILE,),
        out_shape=jax.ShapeDtypeStruct((T, BW), BF16),
        in_specs=[_row_spec(BW), _row_spec(BW, gate_blk), _vec_spec(BW if centered else HD)],
        out_specs=_row_spec(BW), sem=("parallel",), args=(raw, p, g), duty=duty)


def _branch_post_bwd(raw, p, g, dout, gate_blk, centered, name, duty=None):
    gw = BW if centered else HD

    def body(raw_ref, z_ref, g_ref, do_ref, dr_ref, dz_ref, dg_ref):
        @pl.when(pl.program_id(0) == 0)
        def _():
            dg_ref[...] = jnp.zeros_like(dg_ref)

        for h in range(N_HEADS):
            sl = slice(h * HD, (h + 1) * HD)
            gsl = sl if centered else slice(0, HD)
            gv, z, dov = g_ref[:, gsl], z_ref[:, sl], do_ref[:, sl]
            xh, r = _post_norm(raw_ref[:, sl], gv, centered)
            sg = _sigmoid(z)
            dyn = dov * (z * sg)
            dz_ref[:, sl] = dov * (xh * gv) * (sg * (1.0 + z * (1.0 - sg)))
            dxh = dyn * gv
            t = dxh - xh * _mean1(dxh * xh)
            if centered:
                t = t - _mean1(dxh)
            dr_ref[:, sl] = r * t
            dg_ref[0:1, gsl] += _sum0(dyn * xh)

    return _pcall(
        body, name=name, grid=(T // ROW_TILE,),
        out_shape=[jax.ShapeDtypeStruct((T, BW), F32), jax.ShapeDtypeStruct((T, BW), F32),
                   jax.ShapeDtypeStruct((8, gw), F32)],
        in_specs=[_row_spec(BW), _row_spec(BW, gate_blk), _vec_spec(gw), _row_spec(BW)],
        out_specs=[_row_spec(BW), _row_spec(BW), pl.BlockSpec((8, gw), lambda i: (0, 0))],
        sem=("arbitrary",), args=(raw, p, g, dout), duty=duty)


GLA_ROWS = 256
GLA_CPB = GLA_ROWS // CHUNK
GLA_DK = 64
GLA_W = N_HEADS * GLA_DK
GQ_BLK, GK_BLK, GV_BLK, GG_BLK, TAIL_BLK = 8, 9, 5, 6, 40
RG_BLK = 3


def _gla_chunk_common(tl, w2, bv, kv):
    pre = _dot(tl, w2, NN) + bv
    la = _log_sigmoid(pre) * (1.0 / 16.0)
    bc = _exact_dot(_tri(CHUNK, False), la)
    be = bc[CHUNK - 1:CHUNK, :]
    w = jnp.exp(be - bc)
    return pre, w, jnp.exp(be), kv * w


def _head_masks():
    lane = lax.broadcasted_iota(jnp.int32, (1, GLA_W), 1)
    return [jnp.where((lane // GLA_DK) == h, 1.0, 0.0) for h in range(N_HEADS)]


def _gla_fwd(p, w2pad, b, duty=None):
    nb = T // GLA_ROWS

    def body(q_ref, k_ref, v_ref, t_ref, w2_ref, b_ref, o_ref, st_ref, s_acc):
        @pl.when(pl.program_id(0) == 0)
        def _():
            s_acc[...] = jnp.zeros_like(s_acc)

        masks = _head_masks()
        for c in range(GLA_CPB):
            rows = slice(c * CHUNK, (c + 1) * CHUNK)
            _, _, a, kd = _gla_chunk_common(t_ref[rows, :], w2_ref[...], b_ref[...], k_ref[rows, :])
            q = q_ref[rows, :] * (GLA_DK ** -0.5)
            kv = None
            for h in range(N_HEADS):
                t = _dot(v_ref[rows, h * HD:(h + 1) * HD], kd * masks[h], TN)
                kv = t if kv is None else kv + t
            s_new = s_acc[...] * a + kv
            s_acc[...] = s_new
            st_ref[c] = s_new
            for h in range(N_HEADS):
                o_ref[rows, h * HD:(h + 1) * HD] = _dot(q * masks[h], s_new, NT)

    return _pcall(
        body, name="gla_fwd", grid=(nb,),
        out_shape=[jax.ShapeDtypeStruct((T, BW), F32), jax.ShapeDtypeStruct((T // CHUNK, HD, GLA_W), F32)],
        in_specs=[pl.BlockSpec((GLA_ROWS, GLA_W), lambda i: (i, GQ_BLK)),
                  pl.BlockSpec((GLA_ROWS, GLA_W), lambda i: (i, GK_BLK)),
                  pl.BlockSpec((GLA_ROWS, BW), lambda i: (i, GV_BLK)),
                  pl.BlockSpec((GLA_ROWS, 128), lambda i: (i, TAIL_BLK)),
                  pl.BlockSpec((128, GLA_W), lambda i: (0, 0)),
                  pl.BlockSpec((1, GLA_W), lambda i: (0, 0))],
        out_specs=[pl.BlockSpec((GLA_ROWS, BW), lambda i: (i, 0)),
                   pl.BlockSpec((GLA_CPB, HD, GLA_W), lambda i: (i, 0, 0))],
        scratch_shapes=[pltpu.VMEM((HD, GLA_W), F32)],
        sem=("arbitrary",), args=(p, p, p, p, w2pad, b), duty=duty)


def _gla_bwd(p, w2pad, b, states, do, duty=None):
    nb = T // GLA_ROWS

    def body(q_ref, k_ref, v_ref, t_ref, w2_ref, b_ref, st_ref, prev_ref, do_ref,
             dq_ref, dk_ref, dv_ref, dt_ref, dw2_ref, db_ref, ds_acc):
        step = pl.program_id(0)

        @pl.when(step == 0)
        def _():
            ds_acc[...] = jnp.zeros_like(ds_acc)
            dw2_ref[...] = jnp.zeros_like(dw2_ref)
            db_ref[...] = jnp.zeros_like(db_ref)

        masks = _head_masks()
        up = _tri(CHUNK, True)
        has_prev = jnp.where(step == nb - 1, 0.0, 1.0)
        for c in reversed(range(GLA_CPB)):
            rows = slice(c * CHUNK, (c + 1) * CHUNK)
            tl, w2, k = t_ref[rows, :], w2_ref[...], k_ref[rows, :]
            pre, w, a, kd = _gla_chunk_common(tl, w2, b_ref[...], k)
            q = q_ref[rows, :] * (GLA_DK ** -0.5)
            s_n = st_ref[c]
            s_prev = st_ref[c - 1] if c > 0 else prev_ref[0] * has_prev
            ds = ds_acc[...]
            dos = [do_ref[rows, h * HD:(h + 1) * HD] for h in range(N_HEADS)]
            for h in range(N_HEADS):
                ds = ds + _dot(dos[h], q * masks[h], TN)
            dqp = jnp.zeros((CHUNK, GLA_W), F32)
            dkd = jnp.zeros((CHUNK, GLA_W), F32)
            for h in range(N_HEADS):
                dqp = dqp + masks[h] * _dot(dos[h], s_n, NN)
                dkd = dkd + masks[h] * _dot(v_ref[rows, h * HD:(h + 1) * HD], ds, NN)
                dv_ref[rows, h * HD:(h + 1) * HD] = _dot(kd * masks[h], ds, NT)
            dq_ref[rows, :] = dqp * (GLA_DK ** -0.5)
            dk_ref[rows, :] = dkd * w
            e = dkd * k * w
            dbe = _sum0(e) + _sum0(ds * s_prev) * a
            dla = dbe - _exact_dot(up, e)
            dpre = dla * (1.0 / 16.0) * _sigmoid(-pre)
            db_ref[0:1, :] += _sum0(dpre)
            dw2_ref[...] += _dot(tl, dpre, TN)
            dt_ref[rows, :] = _dot(dpre, w2, NT)
            ds_acc[...] = ds * a

    rev = lambda i: nb - 1 - i
    sh = lambda w: jax.ShapeDtypeStruct((T, w), F32)
    return _pcall(
        body, name="gla_bwd", grid=(nb,),
        out_shape=[sh(GLA_W), sh(GLA_W), sh(BW), sh(128), jax.ShapeDtypeStruct((128, GLA_W), F32),
                   jax.ShapeDtypeStruct((8, GLA_W), F32)],
        in_specs=[pl.BlockSpec((GLA_ROWS, GLA_W), lambda i: (rev(i), GQ_BLK)),
                  pl.BlockSpec((GLA_ROWS, GLA_W), lambda i: (rev(i), GK_BLK)),
                  pl.BlockSpec((GLA_ROWS, BW), lambda i: (rev(i), GV_BLK)),
                  pl.BlockSpec((GLA_ROWS, 128), lambda i: (rev(i), TAIL_BLK)),
                  pl.BlockSpec((128, GLA_W), lambda i: (0, 0)),
                  pl.BlockSpec((1, GLA_W), lambda i: (0, 0)),
                  pl.BlockSpec((GLA_CPB, HD, GLA_W), lambda i: (rev(i), 0, 0)),
                  pl.BlockSpec((1, HD, GLA_W), lambda i: (jnp.maximum(rev(i) * GLA_CPB - 1, 0), 0, 0)),
                  pl.BlockSpec((GLA_ROWS, BW), lambda i: (rev(i), 0))],
        out_specs=[pl.BlockSpec((GLA_ROWS, GLA_W), lambda i: (rev(i), 0)),
                   pl.BlockSpec((GLA_ROWS, GLA_W), lambda i: (rev(i), 0)),
                   pl.BlockSpec((GLA_ROWS, BW), lambda i: (rev(i), 0)),
                   pl.BlockSpec((GLA_ROWS, 128), lambda i: (rev(i), 0)),
                   pl.BlockSpec((128, GLA_W), lambda i: (0, 0)),
                   pl.BlockSpec((8, GLA_W), lambda i: (0, 0))],
        scratch_shapes=[pltpu.VMEM((HD, GLA_W), F32)],
        sem=("arbitrary",), args=(p, p, p, p, w2pad, b, states, states, do), duty=duty)


FQ_BLK, FK_BLK = 7, 8
V_FOX_BLK = 36


def _fox_prep_fwd(p, qg, kg, btail, duty=None):
    def body(q_ref, k_ref, t_ref, qg_ref, kg_ref, bt_ref, o_ref, cum_ref, carry):
        @pl.when(pl.program_id(0) == 0)
        def _():
            carry[...] = jnp.zeros_like(carry)

        for src, gr, off in ((q_ref, qg_ref, 0), (k_ref, kg_ref, BW)):
            for h in range(N_HEADS):
                xv = src[:, h * HD:(h + 1) * HD]
                r = lax.rsqrt(_mean1(xv * xv) + EPS)
                o_ref[:, off + h * HD:off + (h + 1) * HD] = (xv * r * gr[...]).astype(BF16)
        logf = _log_sigmoid(t_ref[...] + bt_ref[...])
        cum = _exact_dot(_tri(ROW_TILE, False), logf) + carry[...]
        cum_ref[...] = cum
        carry[...] = cum[ROW_TILE - 1:ROW_TILE, :]

    return _pcall(
        body, name="fox_prep_fwd", grid=(T // ROW_TILE,),
        out_shape=[jax.ShapeDtypeStruct((T, 2 * BW), BF16), jax.ShapeDtypeStruct((T, 128), F32)],
        in_specs=[_row_spec(BW, FQ_BLK), _row_spec(BW, FK_BLK), _row_spec(128, TAIL_BLK),
                  _vec_spec(HD), _vec_spec(HD), _vec_spec(128)],
        out_specs=[_row_spec(2 * BW), _row_spec(128)],
        scratch_shapes=[pltpu.VMEM((1, 128), F32)],
        sem=("arbitrary",), args=(p, p, p, qg, kg, btail), duty=duty)


def _fox_prep_bwd(p, qg, kg, btail, dqn, dkn, dcum, duty=None):
    nt = T // ROW_TILE

    def body(q_ref, k_ref, t_ref, qg_ref, kg_ref, bt_ref, dq_ref, dk_ref, dc_ref, o_ref, dt_ref, st_ref, carry):
        @pl.when(pl.program_id(0) == 0)
        def _():
            carry[...] = jnp.zeros_like(carry)
            st_ref[...] = jnp.zeros_like(st_ref)

        for row, (src, gr, dsrc, off) in enumerate(((q_ref, qg_ref, dq_ref, 0), (k_ref, kg_ref, dk_ref, BW))):
            for h in range(N_HEADS):
                xv = src[:, h * HD:(h + 1) * HD]
                dy = dsrc[:, h * HD:(h + 1) * HD]
                r = lax.rsqrt(_mean1(xv * xv) + EPS)
                n = xv * r
                dn = dy * gr[...]
                o_ref[:, off + h * HD:off + (h + 1) * HD] = r * (dn - n * _mean1(dn * n))
                st_ref[row:row + 1, :] += _sum0(dy * n)
        z = t_ref[...] + bt_ref[...]
        dlogf = _exact_dot(_tri(ROW_TILE, True), dc_ref[...]) + carry[...]
        carry[...] = dlogf[0:1, :]
        lane = lax.broadcasted_iota(jnp.int32, (1, 128), 1)
        keep = (lane >= FF_LANE0) & (lane < FF_LANE0 + N_HEADS)
        dz = jnp.where(keep, dlogf * _sigmoid(-z), 0.0)
        dt_ref[...] = dz
        st_ref[2:3, :] += _sum0(dz)

    rs = lambda w, col=0: pl.BlockSpec((ROW_TILE, w), lambda i: (nt - 1 - i, col))
    return _pcall(
        body, name="fox_prep_bwd", grid=(nt,),
        out_shape=[jax.ShapeDtypeStruct((T, 2 * BW), F32), jax.ShapeDtypeStruct((T, 128), F32),
                   jax.ShapeDtypeStruct((8, 128), F32)],
        in_specs=[rs(BW, FQ_BLK), rs(BW, FK_BLK), rs(128, TAIL_BLK), _vec_spec(HD), _vec_spec(HD), _vec_spec(128),
                  rs(BW), rs(BW), rs(128)],
        out_specs=[rs(2 * BW), rs(128), pl.BlockSpec((8, 128), lambda i: (0, 0))],
        scratch_shapes=[pltpu.VMEM((1, 128), F32)],
        sem=("arbitrary",), args=(p, p, p, qg, kg, btail, dqn, dkn, dcum), duty=duty)


def _fox_logits(q_ref, k_ref, cc_ref, cr_ref, i, kl):
    rows, cols = _block_iotas(i, kl)
    s = _dot(q_ref[...], k_ref[0:kl, :], NT) * (HD ** -0.5) + cc_ref[0] - cr_ref[0, :, 0:kl]
    return jnp.where(cols <= rows, s, -1e30)


def _fox_specs():
    q_spec = pl.BlockSpec((TQ, HD), lambda h, i: (i, h))
    k_spec = pl.BlockSpec((T, HD), lambda h, i: (0, N_HEADS + h))
    v_spec = pl.BlockSpec((T, HD), lambda h, i: (0, V_FOX_BLK + h))
    col_spec = pl.BlockSpec((1, TQ, 1), lambda h, i: (h, i, 0))
    row_spec = pl.BlockSpec((1, 1, T), lambda h, i: (h, 0, 0))
    return q_spec, k_spec, v_spec, col_spec, row_spec


def _fox_fwd(qkn, p, cumcol, cumrow, duty=None):
    def body(q_ref, k_ref, v_ref, cc_ref, cr_ref, o_ref, lse_ref):
        i = pl.program_id(1)

        def visible(kl):
            s = _fox_logits(q_ref, k_ref, cc_ref, cr_ref, i, kl)
            m = jnp.max(s, axis=-1, keepdims=True)
            e = jnp.exp(s - m)
            l = jnp.sum(e, axis=-1, keepdims=True)
            o_ref[...] = _dot(e / l, v_ref[0:kl, :], NN)
            lse_ref[0] = m + jnp.log(l)

        _per_query_block(i, visible)

    q_spec, k_spec, v_spec, col_spec, row_spec = _fox_specs()
    return _pcall(
        body, name="fox_fwd", grid=(N_HEADS, T // TQ),
        out_shape=[jax.ShapeDtypeStruct((T, BW), F32), jax.ShapeDtypeStruct((N_HEADS, T, 1), F32)],
        in_specs=[q_spec, k_spec, v_spec, col_spec, row_spec], out_specs=[q_spec, col_spec],
        sem=("parallel", "parallel"), args=(qkn, qkn, p, cumcol, cumrow), duty=duty)


def _fox_bwd(qkn, p, cumcol, cumrow, lse, o, do, duty=None):
    def body(q_ref, k_ref, v_ref, cc_ref, cr_ref, lse_ref, o_ref, do_ref, dq_ref, dk_ref, dv_ref, dr_ref, dc_ref):
        i = pl.program_id(1)
        @pl.when(i == 0)
        def _():
            dk_ref[...] = jnp.zeros_like(dk_ref)
            dv_ref[...] = jnp.zeros_like(dv_ref)
            dc_ref[...] = jnp.zeros_like(dc_ref)

        def visible(kl):
            q, dov = q_ref[...], do_ref[...]
            pm = jnp.exp(_fox_logits(q_ref, k_ref, cc_ref, cr_ref, i, kl) - lse_ref[0])
            delta = jnp.sum(o_ref[...] * dov, axis=-1, keepdims=True)
            ds = pm * (_dot(dov, v_ref[0:kl, :], NT) - delta)
            dq_ref[...] = _dot(ds, k_ref[0:kl, :], NN) * (HD ** -0.5)
            dr_ref[0] = jnp.sum(ds, axis=-1, keepdims=True)
            dk_ref[0:kl, :] += _dot(ds, q, TN) * (HD ** -0.5)
            dv_ref[0:kl, :] += _dot(pm, dov, TN)
            dc_ref[0, :, 0:kl] += _sum0(ds)

        _per_query_block(i, visible)

    q_spec, k_spec, v_spec, col_spec, row_spec = _fox_specs()
    acc_spec = pl.BlockSpec((T, HD), lambda h, i: (0, h))
    sh = jax.ShapeDtypeStruct((T, BW), F32)
    return _pcall(
        body, name="fox_bwd", grid=(N_HEADS, T // TQ),
        out_shape=[sh, sh, sh, jax.ShapeDtypeStruct((N_HEADS, T, 1), F32), jax.ShapeDtypeStruct((N_HEADS, 1, T), F32)],
        in_specs=[q_spec, k_spec, v_spec, col_spec, row_spec, col_spec, q_spec, q_spec],
        out_specs=[q_spec, acc_spec, acc_spec, col_spec, row_spec],
        sem=("parallel", "arbitrary"), args=(qkn, qkn, p, cumcol, cumrow, lse, o, do), duty=duty)


def _mix_fwd(gpre, b_mg, y0, y1, y2, duty=None):
    def body(g_ref, b_ref, y0_ref, y1_ref, y2_ref, o_ref):
        acc = None
        for n, y_ref in enumerate((y0_ref, y1_ref, y2_ref)):
            sl = slice(n * D, (n + 1) * D)
            t = _sigmoid(g_ref[:, sl] + b_ref[:, sl]) * y_ref[...]
            acc = t if acc is None else acc + t
        o_ref[...] = acc.astype(BF16)

    return _pcall(
        body, name="mix_fwd", grid=(T // ROW_TILE,), out_shape=jax.ShapeDtypeStruct((T, D), BF16),
        in_specs=[_row_spec(3 * D), _vec_spec(3 * D), _row_spec(), _row_spec(), _row_spec()],
        out_specs=_row_spec(), sem=("parallel",), args=(gpre, b_mg, y0, y1, y2), duty=duty)


def _mix_bwd(gpre, b_mg, y0, y1, y2, dmi, duty=None):
    def body(g_ref, b_ref, y0_ref, y1_ref, y2_ref, d_ref, dy0_ref, dy1_ref, dy2_ref, dg_ref, db_ref):
        @pl.when(pl.program_id(0) == 0)
        def _():
            db_ref[...] = jnp.zeros_like(db_ref)

        dv = d_ref[...]
        for n, (y_ref, dy_ref) in enumerate(((y0_ref, dy0_ref), (y1_ref, dy1_ref), (y2_ref, dy2_ref))):
            sl = slice(n * D, (n + 1) * D)
            sg = _sigmoid(g_ref[:, sl] + b_ref[:, sl])
            dy_ref[...] = (dv * sg).astype(BF16)
            dpre = dv * y_ref[...] * (sg * (1.0 - sg))
            dg_ref[:, sl] = dpre.astype(BF16)
            db_ref[0:1, sl] += _sum0(dpre)

    shb = jax.ShapeDtypeStruct((T, D), BF16)
    return _pcall(
        body, name="mix_bwd", grid=(T // ROW_TILE,),
        out_shape=[shb, shb, shb, jax.ShapeDtypeStruct((T, 3 * D), BF16), jax.ShapeDtypeStruct((8, 3 * D), F32)],
        in_specs=[_row_spec(3 * D), _vec_spec(3 * D), _row_spec(), _row_spec(), _row_spec(), _row_spec()],
        out_specs=[_row_spec(), _row_spec(), _row_spec(), _row_spec(3 * D), pl.BlockSpec((8, 3 * D), lambda i: (0, 0))],
        sem=("arbitrary",), args=(gpre, b_mg, y0, y1, y2, dmi), duty=duty)


FF_COLS = 256
FF_NBLK = D_FF // FF_COLS


def _shift_rows(a, n):
    rows = lax.broadcasted_iota(jnp.int32, a.shape, 0)
    rolled = pltpu.roll(a, n % T, 0)
    return jnp.where((rows >= n) if n > 0 else (rows < T + n), rolled, 0.0)


def _ffn_act_fwd(uu, w_conv, b_conv, duty=None):
    def body(u_ref, g_ref, w_ref, b_ref, o_ref):
        u = u_ref[...]
        w = w_ref[...]
        uc = b_ref[...] + w[0:1, :] * _shift_rows(u, 2) + w[1:2, :] * _shift_rows(u, 1) + w[2:3, :] * u
        o_ref[...] = (uc * _sigmoid(uc) * g_ref[...]).astype(BF16)

    return _pcall(
        body, name="ffn_act_fwd", grid=(FF_NBLK,), out_shape=jax.ShapeDtypeStruct((T, D_FF), BF16),
        in_specs=[pl.BlockSpec((T, FF_COLS), lambda j: (0, j)), pl.BlockSpec((T, FF_COLS), lambda j: (0, FF_NBLK + j)),
                  pl.BlockSpec((3, FF_COLS), lambda j: (0, j)), pl.BlockSpec((1, FF_COLS), lambda j: (0, j))],
        out_specs=pl.BlockSpec((T, FF_COLS), lambda j: (0, j)),
        sem=("parallel",), args=(uu, uu, w_conv, b_conv), duty=duty)


def _ffn_act_bwd(uu, w_conv, b_conv, da, duty=None):
    def body(u_ref, g_ref, w_ref, b_ref, da_ref, d_ref, st_ref):
        u, w, dav = u_ref[...], w_ref[...], da_ref[...]
        u1, u2 = _shift_rows(u, 1), _shift_rows(u, 2)
        uc = b_ref[...] + w[0:1, :] * u2 + w[1:2, :] * u1 + w[2:3, :] * u
        sg = _sigmoid(uc)
        d_ref[1] = (dav * (uc * sg)).astype(BF16)
        duc = dav * g_ref[...] * (sg * (1.0 + uc * (1.0 - sg)))
        du = w[2:3, :] * duc + w[1:2, :] * _shift_rows(duc, -1) + w[0:1, :] * _shift_rows(duc, -2)
        d_ref[0] = du.astype(BF16)
        st_ref[...] = jnp.zeros_like(st_ref)
        st_ref[0:1, :] = _sum0(duc * u2)
        st_ref[1:2, :] = _sum0(duc * u1)
        st_ref[2:3, :] = _sum0(duc * u)
        st_ref[3:4, :] = _sum0(duc)

    cb = lambda rows=T, off=0: pl.BlockSpec((rows, FF_COLS), lambda j: (0, off + j))
    return _pcall(
        body, name="ffn_act_bwd", grid=(FF_NBLK,),
        out_shape=[jax.ShapeDtypeStruct((2, T, D_FF), BF16), jax.ShapeDtypeStruct((8, D_FF), F32)],
        in_specs=[cb(), cb(T, FF_NBLK), cb(3), cb(1), cb()],
        out_specs=[pl.BlockSpec((2, T, FF_COLS), lambda j: (0, 0, j)), cb(8)],
        sem=("parallel",), args=(uu, uu, w_conv, b_conv, da), duty=duty)


def _adamw(g, w, m, v, tr, name):
    partial = g.ndim == 3
    R, C = w.shape
    tr = R if tr is None else tr
    assert R % tr == 0

    def body(g_ref, w_ref, m_ref, v_ref, go_ref, d_ref, mo_ref, vo_ref):
        if partial:
            gv = g_ref[0].astype(F32)
            for j in range(1, g.shape[0]):
                gv = gv + g_ref[j].astype(F32)
        else:
            gv = g_ref[...]
        go_ref[...] = gv
        mn = ADAM_B1 * m_ref[...] + (1.0 - ADAM_B1) * gv
        vn = ADAM_B2 * v_ref[...] + (1.0 - ADAM_B2) * (gv * gv)
        mo_ref[...] = mn
        vo_ref[...] = vn
        m_hat = mn / (1.0 - ADAM_B1 ** ADAM_STEP)
        v_hat = vn / (1.0 - ADAM_B2 ** ADAM_STEP)
        d_ref[...] = -ADAM_LR * (m_hat / (jnp.sqrt(v_hat) + ADAM_EPS) + ADAM_WD * w_ref[...])

    spec = pl.BlockSpec((tr, C), lambda i: (i, 0))
    g_spec = pl.BlockSpec((g.shape[0], tr, C), lambda i: (0, i, 0)) if partial else spec
    sh = jax.ShapeDtypeStruct((R, C), F32)
    return pl.pallas_call(
        body, name=name, grid=(R // tr,), out_shape=[sh, sh, sh, sh],
        in_specs=[g_spec, spec, spec, spec], out_specs=[spec, spec, spec, spec],
        compiler_params=_cp("parallel"))(g, w, m, v)


def _chip_sum(dw, stage, name):
    n_chip, n, C = stage.shape

    def body(d_ref, s_ref, o_ref):
        mc = lax.axis_index("c")
        mine = jnp.where(mc == 0, d_ref[0, 0].astype(F32), d_ref[0, 1].astype(F32))
        o_ref[0] = (mine + s_ref[0].astype(F32)).astype(BF16)

    return pl.pallas_call(
        body, name=name, grid=(n_chip,), out_shape=jax.ShapeDtypeStruct(stage.shape, BF16),
        in_specs=[pl.BlockSpec((1, 2, n, C), lambda p: (p, 0, 0, 0)), pl.BlockSpec((1, n, C), lambda p: (p, 0, 0))],
        out_specs=pl.BlockSpec((1, n, C), lambda p: (p, 0, 0)), compiler_params=_cp("parallel"),
    )(dw.reshape(n_chip, 2, n, C), stage)


def _sum_partials(g, name, tr=None):
    n_part, R, C = g.shape
    tr = R if tr is None else tr
    assert R % tr == 0

    def body(g_ref, o_ref):
        acc = g_ref[0].astype(F32)
        for j in range(1, n_part):
            acc = acc + g_ref[j].astype(F32)
        o_ref[...] = acc

    return pl.pallas_call(
        body, name=name, grid=(R // tr,), out_shape=jax.ShapeDtypeStruct((R, C), F32),
        in_specs=[pl.BlockSpec((n_part, tr, C), lambda i: (0, i, 0))], out_specs=pl.BlockSpec((tr, C), lambda i: (i, 0)),
        compiler_params=_cp("parallel"))(g)


def _permute_in(w):
    pad = jnp.zeros(w.shape[:-1] + (NP - IN_W,), w.dtype)
    return jnp.concatenate([w[..., :3072], w[..., 3088:5136], w[..., 3072:3088], w[..., 5136:5140], pad], axis=-1)


def _unpermute_in(w):
    return jnp.concatenate([w[..., :3072], w[..., 5120:5136], w[..., 3072:5120], w[..., 5136:5140]], axis=-1)


def _flat_pack(arrs):
    flat = jnp.concatenate([a.reshape(-1).astype(F32) for a in arrs])
    n = flat.shape[0]
    rows = -(-n // 1024) * 8
    return jnp.pad(flat, (0, rows * 128 - n)).reshape(rows, 128)


def _flat_unpack(buf, shapes):
    flat = buf.reshape(-1)
    out, off = [], 0
    for s in shapes:
        n = int(np.prod(s))
        out.append(flat[off:off + n].reshape(s))
        off += n
    return out


GRAD_CHUNKS = dict(w_in=(128, 8), w_o=(128, 1), w_down=(352, 2), w_br0=(128, 1), w_br1=(128, 1), w_br2=(128, 1),
                   w_mg=(384, 4), w_up=(704, 11))
GRAD_VIA_CHIP = ("w_in", "w_down", "w_mg", "w_up")


def _send_grad(xfer, layer, k, g):
    if xfer is None:
        return g
    n, parts = GRAD_CHUNKS[k]
    row_bytes = g.shape[1] * 2
    if k not in GRAD_VIA_CHIP:
        for c in range(parts):
            xfer.add(k, ("a2a", 0, n, (layer,), c * (n // parts), n // parts), g, nbytes=n // parts * row_bytes)
        return g
    stage = ("stage", layer, k)
    xfer.lands[stage] = lax.empty((N_DEV // 2, n, g.shape[1]), BF16)
    xfer.add(stage, ("to_other_core", 0, n, (), 0, n), g, group=stage, nbytes=n * row_bytes)

    def both_halves_here():
        chip = _chip_sum(g, xfer.lands[stage], "chip_sum_" + k)
        for c in range(parts):
            xfer.add(k, ("a2a_chip", 0, n, (layer,), c * (n // parts), n // parts), chip, nbytes=n // parts * row_bytes)

    xfer.when_done(stage, both_halves_here)
    return g


def _weight(wl, k):
    return wl[k]() if callable(wl[k]) else wl[k]


MIN_CARRIER_US = 19.0


def _taker(xfer, fill=1.0):
    if xfer is None:
        return lambda us: None
    return lambda us: xfer.take_for(us * fill) if us >= MIN_CARRIER_US else None


def _layer_fwd(x0, wl, consts, xfer=None, fill=1.25):
    cosf, sinf, logg = consts
    row = lambda a: a.reshape(1, -1)
    take = _taker(xfer, fill)
    h = _norm_fwd(x0, row(wl["norm1_g"]), row(wl["scale1"]), row(wl["shift1"]), "norm1_fwd", duty=take(9))
    p = _matmul(h, _weight(wl, "w_in"), "nn", "in_proj", duty=take(41))
    qk = _rope_fwd(p, cosf, sinf, duty=take(10))
    ret_raw = _ret_fwd(qk, p, logg, duty=take(27))
    br0 = _branch_post_fwd(ret_raw, p, row(wl["ret_norm_g"]), RG_BLK, True, "ret_post_fwd", duty=take(9))
    gla_raw, states = _gla_fwd(p, wl["w2pad"], row(wl["b_gla_a"]), duty=take(26))
    br1 = _branch_post_fwd(gla_raw, p, row(wl["gla_norm_g"]), GG_BLK, False, "gla_post_fwd", duty=take(9))
    qkn, cum = _fox_prep_fwd(p, row(wl["q_norm_g"]), row(wl["k_norm_g"]), row(wl["btail"]), duty=take(10))
    cum4 = cum[:, FF_LANE0:FF_LANE0 + N_HEADS].T
    cumcol, cumrow = cum4.reshape(N_HEADS, T, 1), cum4.reshape(N_HEADS, 1, T)
    fox_o, lse = _fox_fwd(qkn, p, cumcol, cumrow, duty=take(30))
    w_br_t = _weight(wl, "w_br_t")
    ys = [_matmul(b, w_br_t[n], "nt", "br_proj%d" % n) for n, b in enumerate((br0, br1, fox_o))]
    gpre = _matmul(h, _weight(wl, "w_mg_t"), "nt", "gate_proj", duty=take(25))
    mixed_in = _mix_fwd(gpre, row(wl["b_mg"]), *ys, duty=take(21))
    mixed = _matmul(mixed_in, _weight(wl, "w_o"), "nn", "o_proj", duty=take(10))
    x1, h2 = _norm_fwd(x0, row(wl["norm2_g"]), row(wl["scale2"]), row(wl["shift2"]), "norm2_fwd",
                       m=mixed, gate=row(wl["gate1"]), duty=take(13))
    uu = _matmul(h2, _weight(wl, "w_up_t"), "nt", "up_proj", duty=take(42))
    act = _ffn_act_fwd(uu, wl["w_conv"], row(wl["b_conv"]), duty=take(25))
    y = _matmul(act, _weight(wl, "w_down"), "nn", "down_proj", tk=1408, duty=take(24))
    x2 = _axpy(x1, y, row(wl["gate2"]), "resid2", duty=take(11))
    saved = dict(x0=x0, h=h, p=p, qk=qk, ret_raw=ret_raw, br0=br0, gla_raw=gla_raw, states=states, br1=br1,
                 qkn=qkn, cumcol=cumcol, cumrow=cumrow, fox_o=fox_o, lse=lse, y0=ys[0], y1=ys[1], y2=ys[2],
                 gpre=gpre, mixed_in=mixed_in, mixed=mixed, x1=x1, h2=h2, uu=uu, act=act, y=y)
    return x2, saved


def _layer_bwd(dx2, wl, sv, consts, xfer=None, layer=0):
    cosf, sinf, logg = consts
    row = lambda a: a.reshape(1, -1)
    take = _taker(xfer)

    send = functools.partial(_send_grad, xfer, layer)

    dy, st_g2 = _gate_bwd(dx2, sv["y"], row(wl["gate2"]), "gate2_bwd", duty=take(10))
    dact = _matmul(dy, _weight(wl, "w_down"), "nt", "down_dx", tn=1408, duty=take(21))
    d_down = send("w_down", _matmul(sv["act"], dy, "tn", "down_dw", out_dtype=BF16, tm=1408, duty=take(19)))
    duu, st_conv = _ffn_act_bwd(sv["uu"], wl["w_conv"], row(wl["b_conv"]), dact, duty=take(40))
    dh2 = _matmul(duu, _weight(wl, "w_up_t"), "nn", "up_dx", tk=1408, duty=take(42))
    d_up_t = send("w_up", _matmul(duu, sv["h2"], "tn", "up_dw", out_dtype=BF16, tm=1408, duty=take(33)))
    dx1, st_n2 = _norm_bwd(sv["x1"], dh2, dx2, row(wl["norm2_g"]), row(wl["scale2"]), row(wl["shift2"]), "norm2_bwd",
                           duty=take(15))
    dmixed, st_g1 = _gate_bwd(dx1, sv["mixed"], row(wl["gate1"]), "gate1_bwd", duty=take(10))
    dmi = _matmul(dmixed, _weight(wl, "w_o"), "nt", "o_dx", duty=take(11))
    d_o = send("w_o", _matmul(sv["mixed_in"], dmixed, "tn", "o_dw", out_dtype=BF16, duty=take(9)))
    dy0, dy1, dy2, dgpre, st_bmg = _mix_bwd(sv["gpre"], row(wl["b_mg"]), sv["y0"], sv["y1"], sv["y2"], dmi,
                                             duty=take(31))
    brs = (sv["br0"], sv["br1"], sv["fox_o"])
    w_br_t = _weight(wl, "w_br_t")
    dbr = [_matmul(d, w_br_t[n], "nn", "br_dx%d" % n) for n, d in enumerate((dy0, dy1, dy2))]
    d_br_t = [send("w_br%d" % n, _matmul(d, brs[n], "tn", "br_dw%d" % n, out_dtype=BF16))
              for n, d in enumerate((dy0, dy1, dy2))]
    dh = _matmul(dgpre, _weight(wl, "w_mg_t"), "nn", "gate_dx", tk=1024, duty=take(29))
    d_mg_t = send("w_mg", _matmul(dgpre, sv["h"], "tn", "gate_dw", out_dtype=BF16, duty=take(21)))
    p = sv["p"]
    dqn, dkn, dfv, drow, dcol = _fox_bwd(sv["qkn"], p, sv["cumcol"], sv["cumrow"], sv["lse"], sv["fox_o"], dbr[2],
                                         duty=take(50))
    dcum4 = drow.reshape(N_HEADS, T) - dcol.reshape(N_HEADS, T)
    dcum = jnp.pad(dcum4.T, ((0, 0), (FF_LANE0, 128 - FF_LANE0 - N_HEADS)))
    dfqk, dtail_fox, st_fox = _fox_prep_bwd(p, row(wl["q_norm_g"]), row(wl["k_norm_g"]), row(wl["btail"]), dqn, dkn, dcum,
                                            duty=take(15))
    dgla_raw, dgg, st_gn = _branch_post_bwd(sv["gla_raw"], p, row(wl["gla_norm_g"]), dbr[1], GG_BLK, False, "gla_post_bwd",
                                            duty=take(12))
    dgq, dgk, dgv, dtail_gla, dw2pad, st_bg = _gla_bwd(p, wl["w2pad"], row(wl["b_gla_a"]), sv["states"], dgla_raw,
                                                       duty=take(30))
    dret_raw, drg, st_rn = _branch_post_bwd(sv["ret_raw"], p, row(wl["ret_norm_g"]), dbr[0], RG_BLK, True, "ret_post_bwd",
                                            duty=take(13))
    dqr, dkr, drv = _ret_bwd(sv["qk"], p, logg, dret_raw, duty=take(50))
    drqk = _rope_bwd(dqr, dkr, cosf, sinf, duty=take(11))
    dp = jnp.concatenate([a.astype(BF16) for a in (drqk, drv, drg, dgq, dgk, dgv, dgg, dfqk, dfv, dtail_fox + dtail_gla)]
                         + [jnp.zeros((T, NP - TAIL0 - 128), BF16)], axis=1)
    dh = _matmul(dp, _weight(wl, "w_in"), "nt", "in_dx", tk=1408, add=dh, duty=take(45))
    d_in = send("w_in", _matmul(sv["h"], dp, "tn", "in_dw", out_dtype=BF16, duty=take(32)))
    dx0, st_n1 = _norm_bwd(sv["x0"], dh, dx1, row(wl["norm1_g"]), row(wl["scale1"]), row(wl["shift1"]), "norm1_bwd",
                           duty=take(15))
    big = dict(w_in=d_in, w_o=d_o, w_down=d_down, w_br0=d_br_t[0], w_br1=d_br_t[1], w_br2=d_br_t[2], w_mg=d_mg_t,
               w_up=d_up_t)
    dmod = jnp.concatenate([st_n1[2], st_n1[1], st_g1[0], st_n2[2], st_n2[1], st_g2[0]])
    small = dict(norm1_g=st_n1[0], norm2_g=st_n2[0], b_gla_a=st_bg[0], b_fox_f=st_fox[2, FF_LANE0:FF_LANE0 + N_HEADS],
                 ret_norm_g=st_rn[0], gla_norm_g=st_gn[0], q_norm_g=st_fox[0], k_norm_g=st_fox[1], b_mg=st_bmg[0],
                 b_conv=st_conv[3], w_gla_a2=dw2pad[:LR_LANES], w_conv=st_conv[0:3])
    return dx0, big, dmod, small


SMALL_REPL = ("norm1_g", "norm2_g", "b_ada", "b_gla_a", "b_fox_f", "ret_norm_g", "gla_norm_g", "q_norm_g", "k_norm_g",
              "b_mg", "b_conv")
SMALL_SHARDED = ("w_gla_a2", "w_conv")
BIG = ("w_in", "w_o", "w_down", "w_br", "w_mg", "w_up")
WEIGHTS = ("norm1_g", "norm2_g", "w_ada", "b_ada", "w_in", "w_gla_a2", "b_gla_a", "b_fox_f", "ret_norm_g", "gla_norm_g",
           "q_norm_g", "k_norm_g", "w_br", "w_mg", "b_mg", "w_o", "w_up", "w_conv", "b_conv", "w_down")


def kernel(x, c, norm1_g, norm2_g, w_ada, b_ada, w_in, w_gla_a2, b_gla_a, b_fox_f, ret_norm_g, gla_norm_g, q_norm_g, k_norm_g, w_br, w_mg, b_mg, w_o, w_up, w_conv, b_conv, w_down, loss_target, m_norm1_g, m_norm2_g, m_w_ada, m_b_ada, m_w_in, m_w_gla_a2, m_b_gla_a, m_b_fox_f, m_ret_norm_g, m_gla_norm_g, m_q_norm_g, m_k_norm_g, m_w_br, m_w_mg, m_b_mg, m_w_o, m_w_up, m_w_conv, m_b_conv, m_w_down, v_norm1_g, v_norm2_g, v_w_ada, v_b_ada, v_w_in, v_w_gla_a2, v_b_gla_a, v_b_fox_f, v_ret_norm_g, v_gla_norm_g, v_q_norm_g, v_k_norm_g, v_w_br, v_w_mg, v_b_mg, v_w_o, v_w_up, v_w_conv, v_b_conv, v_w_down):
    W = dict(norm1_g=norm1_g, norm2_g=norm2_g, w_ada=w_ada, b_ada=b_ada, w_in=w_in, w_gla_a2=w_gla_a2, b_gla_a=b_gla_a,
             b_fox_f=b_fox_f, ret_norm_g=ret_norm_g, gla_norm_g=gla_norm_g, q_norm_g=q_norm_g, k_norm_g=k_norm_g,
             w_br=w_br, w_mg=w_mg, b_mg=b_mg, w_o=w_o, w_up=w_up, w_conv=w_conv, b_conv=b_conv, w_down=w_down)
    M = dict(norm1_g=m_norm1_g, norm2_g=m_norm2_g, w_ada=m_w_ada, b_ada=m_b_ada, w_in=m_w_in, w_gla_a2=m_w_gla_a2,
             b_gla_a=m_b_gla_a, b_fox_f=m_b_fox_f, ret_norm_g=m_ret_norm_g, gla_norm_g=m_gla_norm_g, q_norm_g=m_q_norm_g,
             k_norm_g=m_k_norm_g, w_br=m_w_br, w_mg=m_w_mg, b_mg=m_b_mg, w_o=m_w_o, w_up=m_w_up, w_conv=m_w_conv,
             b_conv=m_b_conv, w_down=m_w_down)
    V = dict(norm1_g=v_norm1_g, norm2_g=v_norm2_g, w_ada=v_w_ada, b_ada=v_b_ada, w_in=v_w_in, w_gla_a2=v_w_gla_a2,
             b_gla_a=v_b_gla_a, b_fox_f=v_b_fox_f, ret_norm_g=v_ret_norm_g, gla_norm_g=v_gla_norm_g, q_norm_g=v_q_norm_g,
             k_norm_g=v_k_norm_g, w_br=v_w_br, w_mg=v_w_mg, b_mg=v_b_mg, w_o=v_w_o, w_up=v_w_up, w_conv=v_w_conv,
             b_conv=v_b_conv, w_down=v_w_down)
    me = 4 * lax.axis_index("x") + 2 * lax.axis_index("y") + lax.axis_index("c")
    x2d, tgt = x.reshape(T, D), loss_target.reshape(T, D)

    sm = _flat_pack([c, w_gla_a2, w_conv])
    sm_all = _exchange(sm, True, "gather_small")
    parts = [_flat_unpack(sm_all[j], [(D,), (DEPTH, LR_LANES, 32), (DEPTH, 3, 352)]) for j in range(N_DEV)]
    c_all = jnp.stack([q[0] for q in parts])
    w_gla_full = jnp.concatenate([q[1] for q in parts], axis=2)
    w_conv_full = jnp.concatenate([q[2] for q in parts], axis=2)

    n_ada = w_ada.shape[2]
    b_loc = lax.dynamic_slice_in_dim(b_ada, me * n_ada, n_ada, axis=1).reshape(DEPTH, 1, n_ada)
    mod_all = _ada_fwd(c_all, w_ada, b_loc)
    mod_recv = _exchange(jnp.swapaxes(mod_all, 0, 1), False, "a2a_mod")
    mod = jnp.swapaxes(mod_recv, 0, 1).reshape(DEPTH, 6, D)

    loc = dict(w_in=_permute_in(w_in), w_o=w_o, w_down=w_down, w_br=jnp.swapaxes(w_br, 2, 3),
               w_mg=jnp.swapaxes(w_mg, 1, 2), w_up=jnp.swapaxes(w_up, 1, 2))
    loc = {k: v.astype(BF16) for k, v in loc.items()}
    w_full = dict(w_in=(D, NP), w_o=(D, D), w_down=(D_FF, D), w_br=(3, D, BW), w_mg=(3 * D, D), w_up=(2 * D_FF, D))
    w_parts = dict(w_in=8, w_br=2, w_mg=4, w_o=1, w_up=11, w_down=2)
    gather, units = _Transfers("gather"), []
    for l in range(DEPTH):
        for k, parts in w_parts.items():
            axis = 1 if k == "w_br" else 0
            n = w_full[k][axis] // N_DEV
            gather.lands[(l, k)] = lax.empty(w_full[k], BF16)
            shard = loc[k][l]
            nbytes = shard.size * 2 // parts
            units += [((l, k), (axis, n, (), c * (n // parts), n // parts), shard, nbytes) for c in range(parts)]
    first, lag = w_parts["w_in"], 4
    order = [("cross", i) for i in range(first)] + [("pass", i) for i in range(first)]
    for i in range(first, len(units) + lag):
        order += [("cross", i)] if i < len(units) else []
        order += [("pass", i - lag)] if i - lag >= first else []
    for what, i in order:
        key, where, shard, nbytes = units[i]
        if what == "cross":
            gather.add(key, ("gather_chip",) + where, shard, uid=i, nbytes=nbytes)
        else:
            gather.add(key, ("pass_on",) + where, after=i, nbytes=nbytes)

    w2pad = jnp.pad(w_gla_full, ((0, 0), (0, 128 - LR_LANES), (0, 0)))
    btail = jnp.pad(b_fox_f, ((0, 0), (FF_LANE0, 128 - FF_LANE0 - N_HEADS)))
    stacked = dict(norm1_g=norm1_g, norm2_g=norm2_g, b_gla_a=b_gla_a, ret_norm_g=ret_norm_g, gla_norm_g=gla_norm_g,
                   q_norm_g=q_norm_g, k_norm_g=k_norm_g, b_mg=b_mg, b_conv=b_conv, w_conv=w_conv_full, w2pad=w2pad,
                   btail=btail, shift1=mod[:, 0], scale1=mod[:, 1], gate1=mod[:, 2], shift2=mod[:, 3], scale2=mod[:, 4],
                   gate2=mod[:, 5])
    landed = lambda l, k: functools.partial(gather.get, (l, k))
    layers = [dict({k: v[l] for k, v in stacked.items()}, w_in=landed(l, "w_in"), w_o=landed(l, "w_o"),
                   w_down=landed(l, "w_down"), w_br_t=landed(l, "w_br"), w_mg_t=landed(l, "w_mg"), w_up_t=landed(l, "w_up"))
              for l in range(DEPTH)]
    consts = _rope_tables() + (_ret_logg(),)

    xc, saved = x2d, []
    for l in range(DEPTH):
        xc, sv = _layer_fwd(xc, layers[l], consts, gather, 1.5 if l == 0 else 1.25)
        saved.append(sv)
    loss_part, dxc = _loss_fwd_bwd(xc, tgt)
    loss = lax.psum(loss_part[0, 0], ("x", "y", "c"))

    grad_names = ("w_in", "w_o", "w_down", "w_br0", "w_br1", "w_br2", "w_mg", "w_up")
    blk_rows = dict(w_in=(128, NP), w_o=(128, D), w_down=(352, D), w_br0=(128, BW), w_br1=(128, BW), w_br2=(128, BW),
                    w_mg=(384, D), w_up=(704, D))
    grads = _Transfers("grads")
    for k in grad_names:
        grads.lands[k] = lax.empty((N_DEV // 2 if k in GRAD_VIA_CHIP else N_DEV, DEPTH) + blk_rows[k], BF16)
    dmod, small_g = [None] * DEPTH, [None] * DEPTH
    for l in reversed(range(DEPTH)):
        dxc, _, dmod[l], small_g[l] = _layer_bwd(dxc, layers[l], saved[l], consts, grads, l)
    grad_x = dxc
    dmod = jnp.stack(dmod)
    small_g = {k: jnp.stack([s[k] for s in small_g]) for k in small_g[0]}
    grads.drain()
    recv = {k: grads.get(k) for k in grad_names}

    dmod_send = jnp.swapaxes(dmod.reshape(DEPTH, N_DEV, n_ada), 0, 1)
    dmod_all = jnp.swapaxes(_exchange(dmod_send, False, "a2a_dmod"), 0, 1)
    g_ada = _ada_bwd(c_all, dmod_all)

    def flat(a, k):
        return a.reshape((-1, W[k].shape[-1]))

    def adam_nat(k, g, tr):
        outs = _adamw(g, flat(W[k], k), flat(M[k], k), flat(V[k], k), tr, "adamw_" + k)
        return [o.reshape(W[k].shape) for o in outs]

    def summed(k, tr):
        r = recv[k]
        return _sum_partials(r.reshape(r.shape[0], DEPTH * r.shape[2], r.shape[3]), "sum_" + k, tr).reshape((DEPTH,) + r.shape[2:])

    big_out = dict(
        w_in=adam_nat("w_in", flat(_unpermute_in(summed("w_in", 64)), "w_in"), 64),
        w_o=adam_nat("w_o", recv["w_o"].reshape(N_DEV, DEPTH * 128, D), 128),
        w_down=adam_nat("w_down", recv["w_down"].reshape(N_DEV // 2, DEPTH * 352, D), 352),
        w_br=adam_nat("w_br", flat(jnp.swapaxes(jnp.stack([summed("w_br%d" % n, 128) for n in range(3)], axis=1), 2, 3),
                                   "w_br"), 1024),
        w_mg=adam_nat("w_mg", flat(jnp.swapaxes(summed("w_mg", 384), 1, 2), "w_mg"), 512),
        w_up=adam_nat("w_up", flat(jnp.swapaxes(summed("w_up", 704), 1, 2), "w_up"), 512))
    ada_out = [o.reshape(DEPTH, D, n_ada) for o in _adamw(
        g_ada.reshape(DEPTH * D, n_ada), w_ada.reshape(DEPTH * D, n_ada), m_w_ada.reshape(DEPTH * D, n_ada),
        v_w_ada.reshape(DEPTH * D, n_ada), 512, "adamw_ada")]

    small_g = dict(small_g, b_ada=dmod)
    names = SMALL_REPL + SMALL_SHARDED
    full_shapes = [W[n].shape for n in SMALL_REPL] + [(DEPTH, LR_LANES, 256), (DEPTH, 3, D_FF)]
    part = _flat_pack([small_g[n] for n in names])
    total = _flat_unpack(_sum_partials(_exchange(part, True, "gather_small_grads"), "sum_small"), full_shapes)
    total = dict(zip(names, total))
    total["w_gla_a2"] = lax.dynamic_slice_in_dim(total["w_gla_a2"], me * 32, 32, axis=2)
    total["w_conv"] = lax.dynamic_slice_in_dim(total["w_conv"], me * 352, 352, axis=2)
    shapes = [W[n].shape for n in names]
    small_out = _adamw(_flat_pack([total[n] for n in names]), _flat_pack([W[n] for n in names]),
                       _flat_pack([M[n] for n in names]), _flat_pack([V[n] for n in names]), None, "adamw_small")
    small_out = [dict(zip(names, _flat_unpack(o, shapes))) for o in small_out]

    outs = []
    for k in range(4):
        d = dict(small_out[k])
        d.update({n: big_out[n][k] for n in BIG})
        d["w_ada"] = ada_out[k]
        outs.append([d[n] for n in WEIGHTS])
    return (loss, grad_x.reshape(1, T, D), *outs[0], *outs[1], *outs[2], *outs[3])
```

```python
import functools

import numpy as np
import jax
import jax.numpy as jnp
from jax import lax
from jax.experimental import pallas as pl
from jax.experimental.pallas import tpu as pltpu

F32 = jnp.float32
BF16 = jnp.bfloat16

N_DEV = 8
T = 2048
D = 1024
DEPTH = 4
N_HEADS = 4
HD = 128
BW = 512
D_FF = 2816
CHUNK = 64
EPS = 1e-6
IN_W = 5140
NP = 5632
TAIL0 = 5120
LR_LANES = 16
FF_LANE0 = 16
PACK_W = 1024
SEG_ROWS = (704, 128, 352, 192, 384, 704)
LAYER_ROWS = sum(SEG_ROWS)
VMEM_LIMIT_V7X = 56 * 1024 * 1024

ADAM_LR, ADAM_B1, ADAM_B2, ADAM_EPS, ADAM_WD, ADAM_STEP = 0.001, 0.9, 0.999, 1e-08, 0.01, 10

MESH_ID = pl.DeviceIdType.MESH


def _cp(*sem):
    return pltpu.CompilerParams(dimension_semantics=sem if sem else None, vmem_limit_bytes=VMEM_LIMIT_V7X)


def _sigmoid(z):
    return 1.0 / (1.0 + jnp.exp(-z))


def _log_sigmoid(z):
    return jnp.minimum(z, 0.0) - jnp.log(1.0 + jnp.exp(-jnp.abs(z)))


def _sum0(a):
    return jnp.sum(a, axis=0, keepdims=True)


def _mean1(a):
    return jnp.mean(a, axis=-1, keepdims=True)


def _dot(a, b, dims):
    return lax.dot_general(a.astype(BF16), b.astype(BF16), (dims, ((), ())), preferred_element_type=F32)


NN = ((1,), (0,))
NT = ((1,), (1,))
TN = ((0,), (0,))


def _exact_dot(m01, a):
    a1 = a.astype(BF16)
    r1 = a - a1.astype(F32)
    a2 = r1.astype(BF16)
    a3 = (r1 - a2.astype(F32)).astype(BF16)
    d = lambda z: jnp.dot(m01, z, preferred_element_type=F32)
    return d(a1) + d(a2) + d(a3)


def _tri(n, upper):
    r = lax.broadcasted_iota(jnp.int32, (n, n), 0)
    c = lax.broadcasted_iota(jnp.int32, (n, n), 1)
    return jnp.where((c >= r) if upper else (c <= r), 1.0, 0.0).astype(BF16)


def _exchange(x, gather, name):
    blk = x.shape if gather else x.shape[1:]

    def body(x_ref, o_ref, send_sems, recv_sems, loc_sem):
        mx, my, mc = lax.axis_index("x"), lax.axis_index("y"), lax.axis_index("c")
        me = 4 * mx + 2 * my + mc
        loc = pltpu.make_async_copy(x_ref if gather else x_ref.at[me], o_ref.at[me], loc_sem)
        loc.start()
        copies = []
        for k in range(1, N_DEV):
            px = mx ^ (k >> 2) if (k >> 2) else mx
            py = my ^ ((k >> 1) & 1) if ((k >> 1) & 1) else my
            pc = mc ^ (k & 1) if (k & 1) else mc
            peer = 4 * px + 2 * py + pc
            cp = pltpu.make_async_remote_copy(
                src_ref=x_ref if gather else x_ref.at[peer], dst_ref=o_ref.at[me],
                send_sem=send_sems.at[k - 1], recv_sem=recv_sems.at[k - 1],
                device_id=(px, py, pc), device_id_type=MESH_ID)
            cp.start()
            copies.append(cp)
        for cp in copies:
            cp.wait()
        loc.wait()

    return pl.pallas_call(
        body, name=name,
        out_shape=jax.ShapeDtypeStruct((N_DEV,) + tuple(blk), x.dtype),
        in_specs=[pl.BlockSpec(memory_space=pl.ANY)],
        out_specs=pl.BlockSpec(memory_space=pl.ANY),
        scratch_shapes=[pltpu.SemaphoreType.DMA((N_DEV - 1,)), pltpu.SemaphoreType.DMA((N_DEV - 1,)),
                        pltpu.SemaphoreType.DMA],
        compiler_params=pltpu.CompilerParams(has_side_effects=True),
    )(x)


def _blk(ref, axis, j, n, r0=0, nr=None):
    return ref.at[(slice(None),) * axis + (pl.ds(j * n + r0, n if nr is None else nr),)]


def _comm_copies(items, srcs, lands, send_sems, recv_sems, loc_sems):
    mx, my, mc = lax.axis_index("x"), lax.axis_index("y"), lax.axis_index("c")
    me = 4 * mx + 2 * my + mc
    local, remote = [], []
    for t, (kind, axis, n, sel, r0, nr, si, li) in enumerate(items):
        if kind == "pass_on":
            for q in (2, 4, 6):
                px = 1 - mx if q & 4 else mx
                py = 1 - my if q & 2 else my
                rows = _blk(lands[li], axis, 4 * px + 2 * py + mc, n, r0, nr)
                remote.append(pltpu.make_async_remote_copy(
                    src_ref=rows, dst_ref=rows, send_sem=send_sems.at[t * (N_DEV - 1) + q - 1],
                    recv_sem=recv_sems.at[t * (N_DEV - 1) + q - 1], device_id=(mx, my, 1 - mc), device_id_type=MESH_ID))
            continue
        if kind == "to_other_core":
            for p in range(N_DEV // 2):
                remote.append(pltpu.make_async_remote_copy(
                    src_ref=_blk(srcs[si], axis, 2 * p + 1 - mc, n, r0, nr), dst_ref=lands[li].at[p, pl.ds(r0, nr)],
                    send_sem=send_sems.at[t * (N_DEV - 1) + p], recv_sem=recv_sems.at[t * (N_DEV - 1) + p],
                    device_id=(mx, my, 1 - mc), device_id_type=MESH_ID))
            continue
        if kind == "a2a_chip":
            pm = 2 * mx + my
            mine = lands[li].at[(pm,) + tuple(sel) + (pl.ds(r0, nr),)]
            local.append(pltpu.make_async_copy(srcs[si].at[pm, pl.ds(r0, nr)], mine, loc_sems.at[t]))
            for q in (2, 4, 6):
                px = 1 - mx if q & 4 else mx
                py = 1 - my if q & 2 else my
                remote.append(pltpu.make_async_remote_copy(
                    src_ref=srcs[si].at[2 * px + py, pl.ds(r0, nr)], dst_ref=mine,
                    send_sem=send_sems.at[t * (N_DEV - 1) + q - 1], recv_sem=recv_sems.at[t * (N_DEV - 1) + q - 1],
                    device_id=(px, py, mc), device_id_type=MESH_ID))
            continue
        if kind == "a2a":
            mine = lands[li].at[(me,) + tuple(sel) + (pl.ds(r0, nr),)]
            own = _blk(srcs[si], axis, me, n, r0, nr)
        else:
            mine = _blk(lands[li], axis, me, n, r0, nr)
            own = _blk(srcs[si], axis, 0, n, r0, nr)
        local.append(pltpu.make_async_copy(own, mine, loc_sems.at[t]))
        for k in ((1, 2, 4, 6) if kind == "gather_chip" else range(1, N_DEV)):
            px = 1 - mx if k & 4 else mx
            py = 1 - my if k & 2 else my
            pc = 1 - mc if k & 1 else mc
            src = _blk(srcs[si], axis, 4 * px + 2 * py + pc, n, r0, nr) if kind == "a2a" else own
            remote.append(pltpu.make_async_remote_copy(
                src_ref=src, dst_ref=mine, send_sem=send_sems.at[t * (N_DEV - 1) + k - 1],
                recv_sem=recv_sems.at[t * (N_DEV - 1) + k - 1], device_id=(px, py, pc), device_id_type=MESH_ID))
    return local, remote


def _comm_scratch(n_items):
    return [pltpu.SemaphoreType.DMA((n_items * (N_DEV - 1),)), pltpu.SemaphoreType.DMA((n_items * (N_DEV - 1),)),
            pltpu.SemaphoreType.DMA((n_items,))]


LINK_BYTES_PER_US = dict(gather_chip=23e3, a2a_chip=23e3, a2a=11.5e3, gather=11.5e3, pass_on=200e3, to_other_core=150e3)
CALL_EXCHANGE_US = 3.0


class _Duty:
    def __init__(self, items, srcs, lands, done):
        self.items, self.srcs, self.lands, self.done = items, srcs, lands, done


def _pcall(body, name, grid, in_specs, out_specs, out_shape, args, scratch_shapes=(), sem=(), duty=None):
    if duty is None:
        return pl.pallas_call(body, name=name, grid=grid, in_specs=list(in_specs), out_specs=out_specs,
                              out_shape=out_shape, scratch_shapes=list(scratch_shapes), compiler_params=_cp(*sem))(*args)
    single = not isinstance(out_shape, (list, tuple))
    o_shape = [out_shape] if single else list(out_shape)
    o_specs = [out_specs] if single else list(out_specs)
    n_in, n_out, n_scr = len(in_specs), len(o_shape), len(scratch_shapes)
    n_src, n_land, n_items = len(duty.srcs), len(duty.lands), len(duty.items)
    a0 = n_in + n_src + n_land

    def wrapped(*refs):
        srcs = refs[n_in:n_in + n_src]
        lands = refs[a0 + n_out:a0 + n_out + n_land]
        core = refs[:n_in] + refs[a0:a0 + n_out] + refs[a0 + n_out + n_land:a0 + n_out + n_land + n_scr]
        sems = refs[a0 + n_out + n_land + n_scr:]
        first = functools.reduce(jnp.logical_and, [pl.program_id(a) == 0 for a in range(len(grid))])
        last = functools.reduce(jnp.logical_and, [pl.program_id(a) == g - 1 for a, g in enumerate(grid)])

        @pl.when(first)
        def _():
            local, remote = _comm_copies(duty.items, srcs, lands, *sems)
            for cp in local + remote:
                cp.start()

        body(*core)

        @pl.when(last)
        def _():
            local, remote = _comm_copies(duty.items, srcs, lands, *sems)
            for cp in remote + local:
                cp.wait()

    hbm = pl.BlockSpec(memory_space=pl.ANY)
    res = pl.pallas_call(
        wrapped, name=name, grid=grid,
        in_specs=list(in_specs) + [hbm] * (n_src + n_land), out_specs=o_specs + [hbm] * n_land,
        out_shape=o_shape + [jax.ShapeDtypeStruct(a.shape, a.dtype) for a in duty.lands],
        input_output_aliases={n_in + n_src + t: n_out + t for t in range(n_land)},
        scratch_shapes=list(scratch_shapes) + _comm_scratch(n_items),
        compiler_params=pltpu.CompilerParams(dimension_semantics=("arbitrary",) * len(grid),
                                             vmem_limit_bytes=VMEM_LIMIT_V7X, has_side_effects=True),
    )(*args, *duty.srcs, *duty.lands)
    duty.done(res[n_out:])
    return res[0] if single else res[:n_out]


def _comm(duty, name):
    n_src, n_land = len(duty.srcs), len(duty.lands)

    def body(*refs):
        local, remote = _comm_copies(duty.items, refs[:n_src], refs[n_src + n_land:n_src + 2 * n_land],
                                     *refs[n_src + 2 * n_land:])
        for cp in local + remote:
            cp.start()
        for cp in remote + local:
            cp.wait()

    hbm = pl.BlockSpec(memory_space=pl.ANY)
    duty.done(pl.pallas_call(
        body, name=name, out_shape=[jax.ShapeDtypeStruct(a.shape, a.dtype) for a in duty.lands],
        in_specs=[hbm] * (n_src + n_land), out_specs=[hbm] * n_land,
        input_output_aliases={n_src + t: t for t in range(n_land)},
        scratch_shapes=_comm_scratch(len(duty.items)), compiler_params=pltpu.CompilerParams(has_side_effects=True),
    )(*duty.srcs, *duty.lands))


class _Transfers:
    def __init__(self, name):
        self.name, self.queue, self.lands, self.flushes, self.groups = name, [], {}, 0, {}

    def add(self, key, item, src=None, uid=None, after=None, group=None, nbytes=0):
        self.queue.append((key, item, src, uid, after, group, nbytes / LINK_BYTES_PER_US[item[0]]))
        if group is not None:
            self.groups[group] = [self.groups.get(group, [0, None])[0] + 1, None]

    def when_done(self, group, fn):
        self.groups[group][1] = fn

    def take_for(self, us):
        count, busy = 0, CALL_EXCHANGE_US
        while count < len(self.queue) and busy + self.queue[count][6] <= us:
            busy += self.queue[count][6]
            count += 1
        return self.take(count) if count else None

    def take(self, count):
        units = []
        while self.queue and len(units) < count:
            after = self.queue[0][4]
            if after is not None and any(u[3] == after for u in units):
                break
            units.append(self.queue.pop(0))
        if not units:
            return None
        keys, srcs, items = [], [], []
        for key, item, src, _, _, _, _ in units:
            if key not in keys:
                keys.append(key)
            if src is not None and not any(src is s for s in srcs):
                srcs.append(src)
            si = [i for i, s in enumerate(srcs) if s is src][0] if src is not None else -1
            items.append(tuple(item) + (si, keys.index(key)))

        def done(new_lands):
            for key, arr in zip(keys, new_lands):
                self.lands[key] = arr
            for u in units:
                if u[5] is not None:
                    self.groups[u[5]][0] -= 1
                    if self.groups[u[5]][0] == 0:
                        self.groups[u[5]][1]()

        return _Duty(items, srcs, [self.lands[k] for k in keys], done)

    def drain(self, upto=None):
        count = upto
        while self.queue if upto is None else count > 0:
            duty = self.take(len(self.queue) if upto is None else count)
            count = None if upto is None else count - len(duty.items)
            self.flushes += 1
            _comm(duty, "%s_flush%d" % (self.name, self.flushes))

    def get(self, key):
        pending = [i for i, u in enumerate(self.queue) if u[0] == key]
        if pending:
            self.drain(pending[-1] + 1)
        return self.lands[key]


def _matmul(a, b, mode, name, out_dtype=F32, tm=1024, tn=512, tk=None, add=None, n_blocks=None, duty=None):
    halves = a.ndim == 3
    if mode == "tn":
        K, M = a.shape[-2], a.shape[-1] * (2 if halves else 1)
        N = b.shape[1]
    else:
        M, K = a.shape[-2], a.shape[-1] * (2 if halves else 1)
        N = b.shape[0] if mode == "nt" else b.shape[1]
    tm, tn = min(tm, M), min(tn, N)
    j0 = 0
    if n_blocks is not None:
        j0, N = n_blocks[0], n_blocks[1] * tn
    tk = K if tk is None else tk
    nk = K // tk
    assert M % tm == 0 and N % tn == 0 and K % tk == 0, (name, M, N, K, tm, tn, tk)
    dims = {"nn": NN, "nt": NT, "tn": TN}[mode]
    has_add = add is not None

    def body(*refs):
        a_ref, b_ref = refs[:2]
        add_ref = refs[2] if has_add else None
        o_ref = refs[3 if has_add else 2]
        part = _dot(a_ref[...], b_ref[...], dims)

        def finish(total):
            if has_add:
                total = total + add_ref[...]
            o_ref[...] = total.astype(o_ref.dtype)

        if nk == 1:
            finish(part)
            return
        acc_ref = refs[-1]
        k = pl.program_id(2)

        @pl.when(k == 0)
        def _():
            acc_ref[...] = part

        @pl.when((k > 0) & (k < nk - 1))
        def _():
            acc_ref[...] += part

        @pl.when(k == nk - 1)
        def _():
            finish(acc_ref[...] + part)

    if halves and mode == "tn":
        per = a.shape[-1] // tm
        a_spec = pl.BlockSpec((None, tk, tm), lambda i, j, k: (i // per, k, i % per))
    elif halves:
        per = a.shape[-1] // tk
        a_spec = pl.BlockSpec((None, tm, tk), lambda i, j, k: (k // per, i, k % per))
    elif mode == "tn":
        a_spec = pl.BlockSpec((tk, tm), lambda i, j, k: (k, i))
    else:
        a_spec = pl.BlockSpec((tm, tk), lambda i, j, k: (i, k))
    if mode == "nt":
        b_spec = pl.BlockSpec((tn, tk), lambda i, j, k: (j0 + j, k))
    else:
        b_spec = pl.BlockSpec((tk, tn), lambda i, j, k: (k, j0 + j))
    o_spec = pl.BlockSpec((tm, tn), lambda i, j, k: (i, j))
    in_specs = [a_spec, b_spec] + ([o_spec] if has_add else [])
    args = (a, b) + ((add,) if has_add else ())
    return _pcall(
        body, name=name, grid=(M // tm, N // tn, nk),
        out_shape=jax.ShapeDtypeStruct((M, N), out_dtype),
        in_specs=in_specs, out_specs=o_spec,
        scratch_shapes=[pltpu.VMEM((tm, tn), F32)] if nk > 1 else [],
        sem=("parallel", "parallel", "arbitrary"), args=args, duty=duty)


def _ada_fwd(c_all, w_ada, b_loc):
    n = w_ada.shape[2]

    def body(c_ref, w_ref, b_ref, o_ref):
        c = c_ref[...]
        o_ref[0] = _dot(c * _sigmoid(c), w_ref[0], NN) + b_ref[0]

    return pl.pallas_call(
        body, name="ada_fwd", grid=(DEPTH,),
        out_shape=jax.ShapeDtypeStruct((DEPTH, N_DEV, n), F32),
        in_specs=[pl.BlockSpec((N_DEV, D), lambda l: (0, 0)),
                  pl.BlockSpec((1, D, n), lambda l: (l, 0, 0)),
                  pl.BlockSpec((1, 1, n), lambda l: (l, 0, 0))],
        out_specs=pl.BlockSpec((1, N_DEV, n), lambda l: (l, 0, 0)),
        compiler_params=_cp("parallel"),
    )(c_all, w_ada, b_loc)


def _ada_bwd(c_all, dmod_all):
    n = dmod_all.shape[2]

    def body(c_ref, d_ref, o_ref):
        c = c_ref[...]
        o_ref[0] = _dot(c * _sigmoid(c), d_ref[0], TN)

    return pl.pallas_call(
        body, name="ada_bwd", grid=(DEPTH,),
        out_shape=jax.ShapeDtypeStruct((DEPTH, D, n), F32),
        in_specs=[pl.BlockSpec((N_DEV, D), lambda l: (0, 0)),
                  pl.BlockSpec((1, N_DEV, n), lambda l: (l, 0, 0))],
        out_specs=pl.BlockSpec((1, D, n), lambda l: (l, 0, 0)),
        compiler_params=_cp("parallel"),
    )(c_all, dmod_all)


ROW_TILE = 256


def _row_spec(w=D, col=0):
    return pl.BlockSpec((ROW_TILE, w), lambda i: (i, col))


def _vec_spec(w=D):
    return pl.BlockSpec((1, w), lambda i: (0, 0))


def _norm_fwd(x, g, scale, shift, name, m=None, gate=None, duty=None):
    has_res = m is not None

    def body(*refs):
        if has_res:
            x_ref, m_ref, gate_ref, g_ref, sc_ref, sh_ref, xo_ref, h_ref = refs
            xv = x_ref[...] + gate_ref[...] * m_ref[...]
            xo_ref[...] = xv
        else:
            x_ref, g_ref, sc_ref, sh_ref, h_ref = refs
            xv = x_ref[...]
        r = lax.rsqrt(_mean1(xv * xv) + EPS)
        h_ref[...] = ((xv * r * g_ref[...]) * (1.0 + sc_ref[...]) + sh_ref[...]).astype(BF16)

    ins = [x] + ([m, gate] if has_res else []) + [g, scale, shift]
    in_specs = [_row_spec()] + ([_row_spec(), _vec_spec()] if has_res else []) + [_vec_spec()] * 3
    out_shape = [jax.ShapeDtypeStruct((T, D), BF16)]
    out_specs = [_row_spec()]
    if has_res:
        out_shape = [jax.ShapeDtypeStruct((T, D), F32)] + out_shape
        out_specs = [_row_spec()] + out_specs
    out = _pcall(body, name=name, grid=(T // ROW_TILE,), out_shape=out_shape, in_specs=in_specs,
                 out_specs=out_specs, sem=("parallel",), args=ins, duty=duty)
    return out if has_res else out[0]


def _norm_bwd(x, dh, dres, g, scale, shift, name, duty=None):
    def body(x_ref, dh_ref, dres_ref, g_ref, sc_ref, sh_ref, dx_ref, st_ref):
        xv, dh_v, gv = x_ref[...], dh_ref[...], g_ref[...]
        r = lax.rsqrt(_mean1(xv * xv) + EPS)
        n = xv * r
        dy = dh_v * (1.0 + sc_ref[...])
        dn = dy * gv
        dx_ref[...] = r * (dn - n * _mean1(dn * n)) + dres_ref[...]

        @pl.when(pl.program_id(0) == 0)
        def _():
            st_ref[...] = jnp.zeros_like(st_ref)

        st_ref[0:1, :] += _sum0(dy * n)
        st_ref[1:2, :] += _sum0(dh_v * (n * gv))
        st_ref[2:3, :] += _sum0(dh_v)

    return _pcall(
        body, name=name, grid=(T // ROW_TILE,),
        out_shape=[jax.ShapeDtypeStruct((T, D), F32), jax.ShapeDtypeStruct((8, D), F32)],
        in_specs=[_row_spec(), _row_spec(), _row_spec(), _vec_spec(), _vec_spec(), _vec_spec()],
        out_specs=[_row_spec(), pl.BlockSpec((8, D), lambda i: (0, 0))],
        sem=("arbitrary",), args=(x, dh, dres, g, scale, shift), duty=duty)


def _axpy(x, m, gate, name, duty=None):
    def body(x_ref, m_ref, gate_ref, o_ref):
        o_ref[...] = x_ref[...] + gate_ref[...] * m_ref[...]

    return _pcall(
        body, name=name, grid=(T // ROW_TILE,), out_shape=jax.ShapeDtypeStruct((T, D), F32),
        in_specs=[_row_spec(), _row_spec(), _vec_spec()], out_specs=_row_spec(),
        sem=("parallel",), args=(x, m, gate), duty=duty)


def _gate_bwd(dx, m, gate, name, duty=None):
    def body(dx_ref, m_ref, gate_ref, dm_ref, st_ref):
        dxv = dx_ref[...]
        dm_ref[...] = (gate_ref[...] * dxv).astype(BF16)

        @pl.when(pl.program_id(0) == 0)
        def _():
            st_ref[...] = jnp.zeros_like(st_ref)

        st_ref[0:1, :] += _sum0(dxv * m_ref[...])

    return _pcall(
        body, name=name, grid=(T // ROW_TILE,),
        out_shape=[jax.ShapeDtypeStruct((T, D), BF16), jax.ShapeDtypeStruct((8, D), F32)],
        in_specs=[_row_spec(), _row_spec(), _vec_spec()],
        out_specs=[_row_spec(), pl.BlockSpec((8, D), lambda i: (0, 0))],
        sem=("arbitrary",), args=(dx, m, gate), duty=duty)


def _loss_fwd_bwd(y, target):
    def body(y_ref, t_ref, l_ref, d_ref):
        e = y_ref[...] - t_ref[...]
        d_ref[...] = e * (1.0 / D)

        @pl.when(pl.program_id(0) == 0)
        def _():
            l_ref[...] = jnp.zeros_like(l_ref)

        l_ref[...] += jnp.sum(_sum0(e * e), axis=1, keepdims=True) * (0.5 / D)

    return pl.pallas_call(
        body, name="loss", grid=(T // ROW_TILE,),
        out_shape=[jax.ShapeDtypeStruct((8, 128), F32), jax.ShapeDtypeStruct((T, D), F32)],
        in_specs=[_row_spec(), _row_spec()],
        out_specs=[pl.BlockSpec((8, 128), lambda i: (0, 0)), _row_spec()],
        compiler_params=_cp("arbitrary"))(y, target)


def _rope_tables():
    half = HD // 2
    inv_freq = 10000.0 ** (-jnp.arange(half, dtype=F32) / half)
    ang = jnp.arange(T, dtype=F32)[:, None] * inv_freq[None, :]
    cos, sin = jnp.cos(ang), jnp.sin(ang)
    return jnp.concatenate([cos, cos], axis=1), jnp.concatenate([-sin, sin], axis=1)


def _rope_fwd(p, cosf, sinf, duty=None):
    def body(p_ref, c_ref, s_ref, o_ref):
        cv, sv = c_ref[...], s_ref[...]
        for j in range(2 * N_HEADS):
            xv = p_ref[:, j * HD:(j + 1) * HD].astype(F32)
            rot = xv * cv + pltpu.roll(xv, HD // 2, 1) * sv
            if j >= N_HEADS:
                rot = rot * (HD ** -0.5)
            o_ref[:, j * HD:(j + 1) * HD] = rot.astype(BF16)

    return _pcall(
        body, name="rope_fwd", grid=(T // ROW_TILE,),
        out_shape=jax.ShapeDtypeStruct((T, 2 * BW), BF16),
        in_specs=[_row_spec(2 * BW), _row_spec(HD), _row_spec(HD)], out_specs=_row_spec(2 * BW),
        sem=("parallel",), args=(p, cosf, sinf), duty=duty)


def _rope_bwd(dq, dk, cosf, sinf, duty=None):
    def body(dq_ref, dk_ref, c_ref, s_ref, o_ref):
        cv, sv = c_ref[...], s_ref[...]
        for j in range(2 * N_HEADS):
            h = j % N_HEADS
            d = dq_ref[:, h * HD:(h + 1) * HD] if j < N_HEADS else dk_ref[:, h * HD:(h + 1) * HD] * (HD ** -0.5)
            o_ref[:, j * HD:(j + 1) * HD] = d * cv + pltpu.roll(d * sv, HD // 2, 1)

    return _pcall(
        body, name="rope_bwd", grid=(T // ROW_TILE,),
        out_shape=jax.ShapeDtypeStruct((T, 2 * BW), F32),
        in_specs=[_row_spec(BW), _row_spec(BW), _row_spec(HD), _row_spec(HD)], out_specs=_row_spec(2 * BW),
        sem=("parallel",), args=(dq, dk, cosf, sinf), duty=duty)


TQ = 256
V_RET_BLK = 8


def _ret_logg():
    lg = jnp.log1p(-jnp.exp2(-5.0 - jnp.arange(N_HEADS, dtype=F32)))
    return jnp.broadcast_to(lg[:, None, None], (N_HEADS, 1, 128))


def _block_iotas(i, kl):
    rows = lax.broadcasted_iota(jnp.int32, (TQ, kl), 0) + i * TQ
    cols = lax.broadcasted_iota(jnp.int32, (TQ, kl), 1)
    return rows, cols


def _ret_weight(lg_ref, i, kl):
    rows, cols = _block_iotas(i, kl)
    dist = jnp.abs(rows - cols).astype(F32)
    w = jnp.exp(dist * lg_ref[0][:, 0:1])
    return jnp.where((cols >> 6) <= (rows >> 6), w, 0.0)


def _per_query_block(i, fn):
    for n in range(1, T // TQ + 1):
        pl.when(i == n - 1)(functools.partial(fn, n * TQ))


def _ret_specs():
    q_spec = pl.BlockSpec((TQ, HD), lambda h, i: (i, h))
    k_spec = pl.BlockSpec((T, HD), lambda h, i: (0, N_HEADS + h))
    v_spec = pl.BlockSpec((T, HD), lambda h, i: (0, V_RET_BLK + h))
    lg_spec = pl.BlockSpec((1, 1, 128), lambda h, i: (h, 0, 0))
    return q_spec, k_spec, v_spec, lg_spec


def _ret_fwd(qk, p, logg, duty=None):
    def body(q_ref, k_ref, v_ref, lg_ref, o_ref):
        i = pl.program_id(1)

        def visible(kl):
            s = _dot(q_ref[...], k_ref[0:kl, :], NT) * _ret_weight(lg_ref, i, kl)
            o_ref[...] = _dot(s, v_ref[0:kl, :], NN)

        _per_query_block(i, visible)

    q_spec, k_spec, v_spec, lg_spec = _ret_specs()
    return _pcall(
        body, name="ret_fwd", grid=(N_HEADS, T // TQ),
        out_shape=jax.ShapeDtypeStruct((T, BW), F32),
        in_specs=[q_spec, k_spec, v_spec, lg_spec], out_specs=q_spec,
        sem=("parallel", "parallel"), args=(qk, qk, p, logg), duty=duty)


def _ret_bwd(qk, p, logg, do, duty=None):
    def body(q_ref, k_ref, v_ref, lg_ref, do_ref, dq_ref, dk_ref, dv_ref):
        i = pl.program_id(1)
        q, dov = q_ref[...], do_ref[...]

        @pl.when(i == 0)
        def _():
            dk_ref[...] = jnp.zeros_like(dk_ref)
            dv_ref[...] = jnp.zeros_like(dv_ref)

        def visible(kl):
            w = _ret_weight(lg_ref, i, kl)
            k = k_ref[0:kl, :]
            s = _dot(q, k, NT) * w
            ds = _dot(dov, v_ref[0:kl, :], NT) * w
            dk_ref[0:kl, :] += _dot(ds, q, TN)
            dv_ref[0:kl, :] += _dot(s, dov, TN)
            dq_ref[...] = _dot(ds, k, NN)

        _per_query_block(i, visible)

    q_spec, k_spec, v_spec, lg_spec = _ret_specs()
    acc_spec = pl.BlockSpec((T, HD), lambda h, i: (0, h))
    sh = jax.ShapeDtypeStruct((T, BW), F32)
    return _pcall(
        body, name="ret_bwd", grid=(N_HEADS, T // TQ),
        out_shape=[sh, sh, sh],
        in_specs=[q_spec, k_spec, v_spec, lg_spec, q_spec], out_specs=[q_spec, acc_spec, acc_spec],
        sem=("parallel", "arbitrary"), args=(qk, qk, p, logg, do), duty=duty)


def _post_norm(xv, gv, centered):
    if centered:
        xv = xv - _mean1(xv)
    r = lax.rsqrt(_mean1(xv * xv) + EPS)
    return xv * r, r


def _branch_post_fwd(raw, p, g, gate_blk, centered, name, duty=None):
    def body(raw_ref, z_ref, g_ref, o_ref):
        for h in range(N_HEADS):
            sl = slice(h * HD, (h + 1) * HD)
            gv = g_ref[:, sl] if centered else g_ref[...]
            xh, _ = _post_norm(raw_ref[:, sl], gv, centered)
            z = z_ref[:, sl].astype(F32)
            o_ref[:, sl] = (z * _sigmoid(z) * (xh * gv)).astype(BF16)

    return _pcall(
        body, name=name, grid=(T // ROW_TILE,),
        out_shape=jax.ShapeDtypeStruct((T, BW), BF16),
        in_specs=[_row_spec(BW), _row_spec(BW, gate_blk), _vec_spec(BW if centered else HD)],
        out_specs=_row_spec(BW), sem=("parallel",), args=(raw, p, g), duty=duty)


def _branch_post_bwd(raw, p, g, dout, gate_blk, centered, name, duty=None):
    gw = BW if centered else HD

    def body(raw_ref, z_ref, g_ref, do_ref, dr_ref, dz_ref, dg_ref):
        @pl.when(pl.program_id(0) == 0)
        def _():
            dg_ref[...] = jnp.zeros_like(dg_ref)

        for h in range(N_HEADS):
            sl = slice(h * HD, (h + 1) * HD)
            gsl = sl if centered else slice(0, HD)
            gv, z, dov = g_ref[:, gsl], z_ref[:, sl].astype(F32), do_ref[:, sl]
            xh, r = _post_norm(raw_ref[:, sl], gv, centered)
            sg = _sigmoid(z)
            dyn = dov * (z * sg)
            dz_ref[:, sl] = dov * (xh * gv) * (sg * (1.0 + z * (1.0 - sg)))
            dxh = dyn * gv
            t = dxh - xh * _mean1(dxh * xh)
            if centered:
                t = t - _mean1(dxh)
            dr_ref[:, sl] = r * t
            dg_ref[0:1, gsl] += _sum0(dyn * xh)

    return _pcall(
        body, name=name, grid=(T // ROW_TILE,),
        out_shape=[jax.ShapeDtypeStruct((T, BW), F32), jax.ShapeDtypeStruct((T, BW), F32),
                   jax.ShapeDtypeStruct((8, gw), F32)],
        in_specs=[_row_spec(BW), _row_spec(BW, gate_blk), _vec_spec(gw), _row_spec(BW)],
        out_specs=[_row_spec(BW), _row_spec(BW), pl.BlockSpec((8, gw), lambda i: (0, 0))],
        sem=("arbitrary",), args=(raw, p, g, dout), duty=duty)


GLA_ROWS = 256
GLA_CPB = GLA_ROWS // CHUNK
GLA_DK = 64
GLA_W = N_HEADS * GLA_DK
GQ_BLK, GK_BLK, GV_BLK, GG_BLK = 8, 9, 5, 6
TAIL_BLK = TAIL0 // 128
RG_BLK = 3


def _gla_chunk_common(tl, w2, bv, kv):
    pre = _dot(tl, w2, NN) + bv
    la = _log_sigmoid(pre) * (1.0 / 16.0)
    bc = _exact_dot(_tri(CHUNK, False), la)
    be = bc[CHUNK - 1:CHUNK, :]
    w = jnp.exp(be - bc)
    return pre, w, jnp.exp(be), kv * w


def _head_masks():
    lane = lax.broadcasted_iota(jnp.int32, (1, GLA_W), 1)
    return [jnp.where((lane // GLA_DK) == h, 1.0, 0.0) for h in range(N_HEADS)]


def _gla_fwd(p, tail, w2pad, b, duty=None):
    nb = T // GLA_ROWS

    def body(q_ref, k_ref, v_ref, t_ref, w2_ref, b_ref, o_ref, st_ref, s_acc):
        @pl.when(pl.program_id(0) == 0)
        def _():
            s_acc[...] = jnp.zeros_like(s_acc)

        masks = _head_masks()
        for c in range(GLA_CPB):
            rows = slice(c * CHUNK, (c + 1) * CHUNK)
            _, _, a, kd = _gla_chunk_common(t_ref[rows, :], w2_ref[...], b_ref[...], k_ref[rows, :].astype(F32))
            q = q_ref[rows, :].astype(F32) * (GLA_DK ** -0.5)
            kv = None
            for h in range(N_HEADS):
                t = _dot(v_ref[rows, h * HD:(h + 1) * HD], kd * masks[h], TN)
                kv = t if kv is None else kv + t
            s_new = s_acc[...] * a + kv
            s_acc[...] = s_new
            st_ref[c] = s_new
            for h in range(N_HEADS):
                o_ref[rows, h * HD:(h + 1) * HD] = _dot(q * masks[h], s_new, NT)

    return _pcall(
        body, name="gla_fwd", grid=(nb,),
        out_shape=[jax.ShapeDtypeStruct((T, BW), F32), jax.ShapeDtypeStruct((T // CHUNK, HD, GLA_W), F32)],
        in_specs=[pl.BlockSpec((GLA_ROWS, GLA_W), lambda i: (i, GQ_BLK)),
                  pl.BlockSpec((GLA_ROWS, GLA_W), lambda i: (i, GK_BLK)),
                  pl.BlockSpec((GLA_ROWS, BW), lambda i: (i, GV_BLK)),
                  pl.BlockSpec((GLA_ROWS, 128), lambda i: (i, 0)),
                  pl.BlockSpec((128, GLA_W), lambda i: (0, 0)),
                  pl.BlockSpec((1, GLA_W), lambda i: (0, 0))],
        out_specs=[pl.BlockSpec((GLA_ROWS, BW), lambda i: (i, 0)),
                   pl.BlockSpec((GLA_CPB, HD, GLA_W), lambda i: (i, 0, 0))],
        scratch_shapes=[pltpu.VMEM((HD, GLA_W), F32)],
        sem=("arbitrary",), args=(p, p, p, tail, w2pad, b), duty=duty)


def _gla_bwd(p, tail, w2pad, b, states, do, duty=None):
    nb = T // GLA_ROWS

    def body(q_ref, k_ref, v_ref, t_ref, w2_ref, b_ref, st_ref, prev_ref, do_ref,
             dq_ref, dk_ref, dv_ref, dt_ref, dw2_ref, db_ref, ds_acc):
        step = pl.program_id(0)

        @pl.when(step == 0)
        def _():
            ds_acc[...] = jnp.zeros_like(ds_acc)
            dw2_ref[...] = jnp.zeros_like(dw2_ref)
            db_ref[...] = jnp.zeros_like(db_ref)

        masks = _head_masks()
        up = _tri(CHUNK, True)
        has_prev = jnp.where(step == nb - 1, 0.0, 1.0)
        for c in reversed(range(GLA_CPB)):
            rows = slice(c * CHUNK, (c + 1) * CHUNK)
            tl, w2, k = t_ref[rows, :], w2_ref[...], k_ref[rows, :].astype(F32)
            pre, w, a, kd = _gla_chunk_common(tl, w2, b_ref[...], k)
            q = q_ref[rows, :].astype(F32) * (GLA_DK ** -0.5)
            s_n = st_ref[c]
            s_prev = st_ref[c - 1] if c > 0 else prev_ref[0] * has_prev
            ds = ds_acc[...]
            dos = [do_ref[rows, h * HD:(h + 1) * HD] for h in range(N_HEADS)]
            for h in range(N_HEADS):
                ds = ds + _dot(dos[h], q * masks[h], TN)
            dqp = jnp.zeros((CHUNK, GLA_W), F32)
            dkd = jnp.zeros((CHUNK, GLA_W), F32)
            for h in range(N_HEADS):
                dqp = dqp + masks[h] * _dot(dos[h], s_n, NN)
                dkd = dkd + masks[h] * _dot(v_ref[rows, h * HD:(h + 1) * HD], ds, NN)
                dv_ref[rows, h * HD:(h + 1) * HD] = _dot(kd * masks[h], ds, NT)
            dq_ref[rows, :] = dqp * (GLA_DK ** -0.5)
            dk_ref[rows, :] = dkd * w
            e = dkd * k * w
            dbe = _sum0(e) + _sum0(ds * s_prev) * a
            dla = dbe - _exact_dot(up, e)
            dpre = dla * (1.0 / 16.0) * _sigmoid(-pre)
            db_ref[0:1, :] += _sum0(dpre)
            dw2_ref[...] += _dot(tl, dpre, TN)
            dt_ref[rows, :] = _dot(dpre, w2, NT)
            ds_acc[...] = ds * a

    rev = lambda i: nb - 1 - i
    sh = lambda w: jax.ShapeDtypeStruct((T, w), F32)
    return _pcall(
        body, name="gla_bwd", grid=(nb,),
        out_shape=[sh(GLA_W), sh(GLA_W), sh(BW), sh(128), jax.ShapeDtypeStruct((128, GLA_W), F32),
                   jax.ShapeDtypeStruct((8, GLA_W), F32)],
        in_specs=[pl.BlockSpec((GLA_ROWS, GLA_W), lambda i: (rev(i), GQ_BLK)),
                  pl.BlockSpec((GLA_ROWS, GLA_W), lambda i: (rev(i), GK_BLK)),
                  pl.BlockSpec((GLA_ROWS, BW), lambda i: (rev(i), GV_BLK)),
                  pl.BlockSpec((GLA_ROWS, 128), lambda i: (rev(i), 0)),
                  pl.BlockSpec((128, GLA_W), lambda i: (0, 0)),
                  pl.BlockSpec((1, GLA_W), lambda i: (0, 0)),
                  pl.BlockSpec((GLA_CPB, HD, GLA_W), lambda i: (rev(i), 0, 0)),
                  pl.BlockSpec((1, HD, GLA_W), lambda i: (jnp.maximum(rev(i) * GLA_CPB - 1, 0), 0, 0)),
                  pl.BlockSpec((GLA_ROWS, BW), lambda i: (rev(i), 0))],
        out_specs=[pl.BlockSpec((GLA_ROWS, GLA_W), lambda i: (rev(i), 0)),
                   pl.BlockSpec((GLA_ROWS, GLA_W), lambda i: (rev(i), 0)),
                   pl.BlockSpec((GLA_ROWS, BW), lambda i: (rev(i), 0)),
                   pl.BlockSpec((GLA_ROWS, 128), lambda i: (rev(i), 0)),
                   pl.BlockSpec((128, GLA_W), lambda i: (0, 0)),
                   pl.BlockSpec((8, GLA_W), lambda i: (0, 0))],
        scratch_shapes=[pltpu.VMEM((HD, GLA_W), F32)],
        sem=("arbitrary",), args=(p, p, p, tail, w2pad, b, states, states, do), duty=duty)


FQ_BLK, FK_BLK = 7, 8
V_FOX_BLK = 36


def _fox_prep_fwd(p, tail, qg, kg, btail, duty=None):
    def body(q_ref, k_ref, t_ref, qg_ref, kg_ref, bt_ref, o_ref, cum_ref, carry):
        @pl.when(pl.program_id(0) == 0)
        def _():
            carry[...] = jnp.zeros_like(carry)

        for src, gr, off in ((q_ref, qg_ref, 0), (k_ref, kg_ref, BW)):
            for h in range(N_HEADS):
                xv = src[:, h * HD:(h + 1) * HD].astype(F32)
                r = lax.rsqrt(_mean1(xv * xv) + EPS)
                o_ref[:, off + h * HD:off + (h + 1) * HD] = (xv * r * gr[...]).astype(BF16)
        logf = _log_sigmoid(t_ref[...] + bt_ref[...])
        cum = _exact_dot(_tri(ROW_TILE, False), logf) + carry[...]
        cum_ref[...] = cum
        carry[...] = cum[ROW_TILE - 1:ROW_TILE, :]

    return _pcall(
        body, name="fox_prep_fwd", grid=(T // ROW_TILE,),
        out_shape=[jax.ShapeDtypeStruct((T, 2 * BW), BF16), jax.ShapeDtypeStruct((T, 128), F32)],
        in_specs=[_row_spec(BW, FQ_BLK), _row_spec(BW, FK_BLK), _row_spec(128),
                  _vec_spec(HD), _vec_spec(HD), _vec_spec(128)],
        out_specs=[_row_spec(2 * BW), _row_spec(128)],
        scratch_shapes=[pltpu.VMEM((1, 128), F32)],
        sem=("arbitrary",), args=(p, p, tail, qg, kg, btail), duty=duty)


def _fox_prep_bwd(p, tail, qg, kg, btail, dqn, dkn, dcum, duty=None):
    nt = T // ROW_TILE

    def body(q_ref, k_ref, t_ref, qg_ref, kg_ref, bt_ref, dq_ref, dk_ref, dc_ref, o_ref, dt_ref, st_ref, carry):
        @pl.when(pl.program_id(0) == 0)
        def _():
            carry[...] = jnp.zeros_like(carry)
            st_ref[...] = jnp.zeros_like(st_ref)

        for row, (src, gr, dsrc, off) in enumerate(((q_ref, qg_ref, dq_ref, 0), (k_ref, kg_ref, dk_ref, BW))):
            for h in range(N_HEADS):
                xv = src[:, h * HD:(h + 1) * HD].astype(F32)
                dy = dsrc[:, h * HD:(h + 1) * HD]
                r = lax.rsqrt(_mean1(xv * xv) + EPS)
                n = xv * r
                dn = dy * gr[...]
                o_ref[:, off + h * HD:off + (h + 1) * HD] = r * (dn - n * _mean1(dn * n))
                st_ref[row:row + 1, :] += _sum0(dy * n)
        z = t_ref[...] + bt_ref[...]
        dlogf = _exact_dot(_tri(ROW_TILE, True), dc_ref[...]) + carry[...]
        carry[...] = dlogf[0:1, :]
        lane = lax.broadcasted_iota(jnp.int32, (1, 128), 1)
        keep = (lane >= FF_LANE0) & (lane < FF_LANE0 + N_HEADS)
        dz = jnp.where(keep, dlogf * _sigmoid(-z), 0.0)
        dt_ref[...] = dz
        st_ref[2:3, :] += _sum0(dz)

    rs = lambda w, col=0: pl.BlockSpec((ROW_TILE, w), lambda i: (nt - 1 - i, col))
    return _pcall(
        body, name="fox_prep_bwd", grid=(nt,),
        out_shape=[jax.ShapeDtypeStruct((T, 2 * BW), F32), jax.ShapeDtypeStruct((T, 128), F32),
                   jax.ShapeDtypeStruct((8, 128), F32)],
        in_specs=[rs(BW, FQ_BLK), rs(BW, FK_BLK), rs(128), _vec_spec(HD), _vec_spec(HD), _vec_spec(128),
                  rs(BW), rs(BW), rs(128)],
        out_specs=[rs(2 * BW), rs(128), pl.BlockSpec((8, 128), lambda i: (0, 0))],
        scratch_shapes=[pltpu.VMEM((1, 128), F32)],
        sem=("arbitrary",), args=(p, p, tail, qg, kg, btail, dqn, dkn, dcum), duty=duty)


def _fox_logits(q_ref, k_ref, cc_ref, cr_ref, i, kl):
    rows, cols = _block_iotas(i, kl)
    s = _dot(q_ref[...], k_ref[0:kl, :], NT) * (HD ** -0.5) + cc_ref[0] - cr_ref[0, :, 0:kl]
    return jnp.where(cols <= rows, s, -1e30)


def _fox_specs():
    q_spec = pl.BlockSpec((TQ, HD), lambda h, i: (i, h))
    k_spec = pl.BlockSpec((T, HD), lambda h, i: (0, N_HEADS + h))
    v_spec = pl.BlockSpec((T, HD), lambda h, i: (0, V_FOX_BLK + h))
    col_spec = pl.BlockSpec((1, TQ, 1), lambda h, i: (h, i, 0))
    row_spec = pl.BlockSpec((1, 1, T), lambda h, i: (h, 0, 0))
    return q_spec, k_spec, v_spec, col_spec, row_spec


def _fox_fwd(qkn, p, cumcol, cumrow, duty=None):
    def body(q_ref, k_ref, v_ref, cc_ref, cr_ref, o_ref, lse_ref):
        i = pl.program_id(1)

        def visible(kl):
            s = _fox_logits(q_ref, k_ref, cc_ref, cr_ref, i, kl)
            m = jnp.max(s, axis=-1, keepdims=True)
            e = jnp.exp(s - m)
            l = jnp.sum(e, axis=-1, keepdims=True)
            o_ref[...] = _dot(e / l, v_ref[0:kl, :], NN)
            lse_ref[0] = m + jnp.log(l)

        _per_query_block(i, visible)

    q_spec, k_spec, v_spec, col_spec, row_spec = _fox_specs()
    return _pcall(
        body, name="fox_fwd", grid=(N_HEADS, T // TQ),
        out_shape=[jax.ShapeDtypeStruct((T, BW), F32), jax.ShapeDtypeStruct((N_HEADS, T, 1), F32)],
        in_specs=[q_spec, k_spec, v_spec, col_spec, row_spec], out_specs=[q_spec, col_spec],
        sem=("parallel", "parallel"), args=(qkn, qkn, p, cumcol, cumrow), duty=duty)


def _fox_bwd(qkn, p, cumcol, cumrow, lse, o, do, duty=None):
    def body(q_ref, k_ref, v_ref, cc_ref, cr_ref, lse_ref, o_ref, do_ref, dq_ref, dk_ref, dv_ref, dr_ref, dc_ref):
        i = pl.program_id(1)
        @pl.when(i == 0)
        def _():
            dk_ref[...] = jnp.zeros_like(dk_ref)
            dv_ref[...] = jnp.zeros_like(dv_ref)
            dc_ref[...] = jnp.zeros_like(dc_ref)

        def visible(kl):
            q, dov = q_ref[...], do_ref[...]
            pm = jnp.exp(_fox_logits(q_ref, k_ref, cc_ref, cr_ref, i, kl) - lse_ref[0])
            delta = jnp.sum(o_ref[...] * dov, axis=-1, keepdims=True)
            ds = pm * (_dot(dov, v_ref[0:kl, :], NT) - delta)
            dq_ref[...] = _dot(ds, k_ref[0:kl, :], NN) * (HD ** -0.5)
            dr_ref[0] = jnp.sum(ds, axis=-1, keepdims=True)
            dk_ref[0:kl, :] += _dot(ds, q, TN) * (HD ** -0.5)
            dv_ref[0:kl, :] += _dot(pm, dov, TN)
            dc_ref[0, :, 0:kl] += _sum0(ds)

        _per_query_block(i, visible)

    q_spec, k_spec, v_spec, col_spec, row_spec = _fox_specs()
    acc_spec = pl.BlockSpec((T, HD), lambda h, i: (0, h))
    sh = jax.ShapeDtypeStruct((T, BW), F32)
    return _pcall(
        body, name="fox_bwd", grid=(N_HEADS, T // TQ),
        out_shape=[sh, sh, sh, jax.ShapeDtypeStruct((N_HEADS, T, 1), F32), jax.ShapeDtypeStruct((N_HEADS, 1, T), F32)],
        in_specs=[q_spec, k_spec, v_spec, col_spec, row_spec, col_spec, q_spec, q_spec],
        out_specs=[q_spec, acc_spec, acc_spec, col_spec, row_spec],
        sem=("parallel", "arbitrary"), args=(qkn, qkn, p, cumcol, cumrow, lse, o, do), duty=duty)


def _mix_fwd(gpre, b_mg, y0, y1, y2, duty=None):
    def body(g_ref, b_ref, y0_ref, y1_ref, y2_ref, o_ref):
        acc = None
        for n, y_ref in enumerate((y0_ref, y1_ref, y2_ref)):
            sl = slice(n * D, (n + 1) * D)
            t = _sigmoid(g_ref[:, sl].astype(F32) + b_ref[:, sl]) * y_ref[...].astype(F32)
            acc = t if acc is None else acc + t
        o_ref[...] = acc.astype(BF16)

    return _pcall(
        body, name="mix_fwd", grid=(T // ROW_TILE,), out_shape=jax.ShapeDtypeStruct((T, D), BF16),
        in_specs=[_row_spec(3 * D), _vec_spec(3 * D), _row_spec(), _row_spec(), _row_spec()],
        out_specs=_row_spec(), sem=("parallel",), args=(gpre, b_mg, y0, y1, y2), duty=duty)


def _mix_bwd(gpre, b_mg, y0, y1, y2, dmi, duty=None):
    def body(g_ref, b_ref, y0_ref, y1_ref, y2_ref, d_ref, dy0_ref, dy1_ref, dy2_ref, dg_ref, db_ref):
        @pl.when(pl.program_id(0) == 0)
        def _():
            db_ref[...] = jnp.zeros_like(db_ref)

        dv = d_ref[...]
        for n, (y_ref, dy_ref) in enumerate(((y0_ref, dy0_ref), (y1_ref, dy1_ref), (y2_ref, dy2_ref))):
            sl = slice(n * D, (n + 1) * D)
            sg = _sigmoid(g_ref[:, sl].astype(F32) + b_ref[:, sl])
            dy_ref[...] = (dv * sg).astype(BF16)
            dpre = dv * y_ref[...].astype(F32) * (sg * (1.0 - sg))
            dg_ref[:, sl] = dpre.astype(BF16)
            db_ref[0:1, sl] += _sum0(dpre)

    shb = jax.ShapeDtypeStruct((T, D), BF16)
    return _pcall(
        body, name="mix_bwd", grid=(T // ROW_TILE,),
        out_shape=[shb, shb, shb, jax.ShapeDtypeStruct((T, 3 * D), BF16), jax.ShapeDtypeStruct((8, 3 * D), F32)],
        in_specs=[_row_spec(3 * D), _vec_spec(3 * D), _row_spec(), _row_spec(), _row_spec(), _row_spec()],
        out_specs=[_row_spec(), _row_spec(), _row_spec(), _row_spec(3 * D), pl.BlockSpec((8, 3 * D), lambda i: (0, 0))],
        sem=("arbitrary",), args=(gpre, b_mg, y0, y1, y2, dmi), duty=duty)


FF_COLS = 256
FF_NBLK = D_FF // FF_COLS


def _shift_rows(a, n):
    rows = lax.broadcasted_iota(jnp.int32, a.shape, 0)
    rolled = pltpu.roll(a, n % T, 0)
    return jnp.where((rows >= n) if n > 0 else (rows < T + n), rolled, 0.0)


def _ffn_act_fwd(uu, w_conv, b_conv, duty=None):
    def body(u_ref, g_ref, w_ref, b_ref, o_ref):
        u = u_ref[...].astype(F32)
        w = w_ref[...]
        uc = b_ref[...] + w[0:1, :] * _shift_rows(u, 2) + w[1:2, :] * _shift_rows(u, 1) + w[2:3, :] * u
        o_ref[...] = (uc * _sigmoid(uc) * g_ref[...].astype(F32)).astype(BF16)

    return _pcall(
        body, name="ffn_act_fwd", grid=(FF_NBLK,), out_shape=jax.ShapeDtypeStruct((T, D_FF), BF16),
        in_specs=[pl.BlockSpec((T, FF_COLS), lambda j: (0, j)), pl.BlockSpec((T, FF_COLS), lambda j: (0, FF_NBLK + j)),
                  pl.BlockSpec((3, FF_COLS), lambda j: (0, j)), pl.BlockSpec((1, FF_COLS), lambda j: (0, j))],
        out_specs=pl.BlockSpec((T, FF_COLS), lambda j: (0, j)),
        sem=("parallel",), args=(uu, uu, w_conv, b_conv), duty=duty)


def _ffn_act_bwd(uu, w_conv, b_conv, da, duty=None):
    def body(u_ref, g_ref, w_ref, b_ref, da_ref, d_ref, st_ref):
        u, w, dav = u_ref[...].astype(F32), w_ref[...], da_ref[...]
        u1, u2 = _shift_rows(u, 1), _shift_rows(u, 2)
        uc = b_ref[...] + w[0:1, :] * u2 + w[1:2, :] * u1 + w[2:3, :] * u
        sg = _sigmoid(uc)
        d_ref[1] = (dav * (uc * sg)).astype(BF16)
        duc = dav * g_ref[...].astype(F32) * (sg * (1.0 + uc * (1.0 - sg)))
        du = w[2:3, :] * duc + w[1:2, :] * _shift_rows(duc, -1) + w[0:1, :] * _shift_rows(duc, -2)
        d_ref[0] = du.astype(BF16)
        st_ref[...] = jnp.zeros_like(st_ref)
        st_ref[0:1, :] = _sum0(duc * u2)
        st_ref[1:2, :] = _sum0(duc * u1)
        st_ref[2:3, :] = _sum0(duc * u)
        st_ref[3:4, :] = _sum0(duc)

    cb = lambda rows=T, off=0: pl.BlockSpec((rows, FF_COLS), lambda j: (0, off + j))
    return _pcall(
        body, name="ffn_act_bwd", grid=(FF_NBLK,),
        out_shape=[jax.ShapeDtypeStruct((2, T, D_FF), BF16), jax.ShapeDtypeStruct((8, D_FF), F32)],
        in_specs=[cb(), cb(T, FF_NBLK), cb(3), cb(1), cb()],
        out_specs=[pl.BlockSpec((2, T, FF_COLS), lambda j: (0, 0, j)), cb(8)],
        sem=("parallel",), args=(uu, uu, w_conv, b_conv, da), duty=duty)


def _adamw(g, w, m, v, tr, name):
    partial = g.ndim == 3
    R, C = w.shape
    tr = R if tr is None else tr
    assert R % tr == 0

    def body(g_ref, w_ref, m_ref, v_ref, go_ref, d_ref, mo_ref, vo_ref):
        if partial:
            gv = g_ref[0].astype(F32)
            for j in range(1, g.shape[0]):
                gv = gv + g_ref[j].astype(F32)
        else:
            gv = g_ref[...]
        go_ref[...] = gv
        mn = ADAM_B1 * m_ref[...] + (1.0 - ADAM_B1) * gv
        vn = ADAM_B2 * v_ref[...] + (1.0 - ADAM_B2) * (gv * gv)
        mo_ref[...] = mn
        vo_ref[...] = vn
        m_hat = mn / (1.0 - ADAM_B1 ** ADAM_STEP)
        v_hat = vn / (1.0 - ADAM_B2 ** ADAM_STEP)
        d_ref[...] = -ADAM_LR * (m_hat / (jnp.sqrt(v_hat) + ADAM_EPS) + ADAM_WD * w_ref[...])

    spec = pl.BlockSpec((tr, C), lambda i: (i, 0))
    g_spec = pl.BlockSpec((g.shape[0], tr, C), lambda i: (0, i, 0)) if partial else spec
    sh = jax.ShapeDtypeStruct((R, C), F32)
    return pl.pallas_call(
        body, name=name, grid=(R // tr,), out_shape=[sh, sh, sh, sh],
        in_specs=[g_spec, spec, spec, spec], out_specs=[spec, spec, spec, spec],
        compiler_params=_cp("parallel"))(g, w, m, v)


def _chip_sum(dw, stage, name):
    n_chip, n, C = stage.shape

    def body(d_ref, s_ref, o_ref):
        mc = lax.axis_index("c")
        mine = jnp.where(mc == 0, d_ref[0, 0].astype(F32), d_ref[0, 1].astype(F32))
        o_ref[0] = (mine + s_ref[0].astype(F32)).astype(BF16)

    return pl.pallas_call(
        body, name=name, grid=(n_chip,), out_shape=jax.ShapeDtypeStruct(stage.shape, BF16),
        in_specs=[pl.BlockSpec((1, 2, n, C), lambda p: (p, 0, 0, 0)), pl.BlockSpec((1, n, C), lambda p: (p, 0, 0))],
        out_specs=pl.BlockSpec((1, n, C), lambda p: (p, 0, 0)), compiler_params=_cp("parallel"),
    )(dw.reshape(n_chip, 2, n, C), stage)


def _sum_partials(g, name, tr=None):
    n_part, R, C = g.shape
    tr = R if tr is None else tr
    assert R % tr == 0

    def body(g_ref, o_ref):
        acc = g_ref[0].astype(F32)
        for j in range(1, n_part):
            acc = acc + g_ref[j].astype(F32)
        o_ref[...] = acc

    return pl.pallas_call(
        body, name=name, grid=(R // tr,), out_shape=jax.ShapeDtypeStruct((R, C), F32),
        in_specs=[pl.BlockSpec((n_part, tr, C), lambda i: (0, i, 0))], out_specs=pl.BlockSpec((tr, C), lambda i: (i, 0)),
        compiler_params=_cp("parallel"))(g)


def _permute_in(w):
    pad = jnp.zeros(w.shape[:-1] + (NP - IN_W,), w.dtype)
    return jnp.concatenate([w[..., :3072], w[..., 3088:5136], w[..., 3072:3088], w[..., 5136:5140], pad], axis=-1)


def _unpermute_in(w):
    return jnp.concatenate([w[..., :3072], w[..., 5120:5136], w[..., 3072:5120], w[..., 5136:5140]], axis=-1)


def _flat_pack(arrs):
    flat = jnp.concatenate([a.reshape(-1).astype(F32) for a in arrs])
    n = flat.shape[0]
    rows = -(-n // 1024) * 8
    return jnp.pad(flat, (0, rows * 128 - n)).reshape(rows, 128)


def _flat_unpack(buf, shapes):
    flat = buf.reshape(-1)
    out, off = [], 0
    for s in shapes:
        n = int(np.prod(s))
        out.append(flat[off:off + n].reshape(s))
        off += n
    return out


GRAD_CHUNKS = dict(w_in=(128, 8), w_o=(128, 1), w_down=(352, 2), w_br0=(128, 1), w_br1=(128, 1), w_br2=(128, 1),
                   w_mg=(384, 4), w_up=(704, 11))
GRAD_VIA_CHIP = ("w_in", "w_down", "w_mg", "w_up")


def _send_grad(xfer, layer, k, g):
    if xfer is None:
        return g
    n, parts = GRAD_CHUNKS[k]
    row_bytes = g.shape[1] * 2
    if k not in GRAD_VIA_CHIP:
        for c in range(parts):
            xfer.add(k, ("a2a", 0, n, (layer,), c * (n // parts), n // parts), g, nbytes=n // parts * row_bytes)
        return g
    stage = ("stage", layer, k)
    xfer.lands[stage] = lax.empty((N_DEV // 2, n, g.shape[1]), BF16)
    xfer.add(stage, ("to_other_core", 0, n, (), 0, n), g, group=stage, nbytes=n * row_bytes)

    def both_halves_here():
        chip = _chip_sum(g, xfer.lands[stage], "chip_sum_" + k)
        for c in range(parts):
            xfer.add(k, ("a2a_chip", 0, n, (layer,), c * (n // parts), n // parts), chip, nbytes=n // parts * row_bytes)

    xfer.when_done(stage, both_halves_here)
    return g


def _weight(wl, k):
    return wl[k]() if callable(wl[k]) else wl[k]


MIN_CARRIER_US = 19.0


def _taker(xfer, fill=1.0):
    if xfer is None:
        return lambda us: None
    return lambda us: xfer.take_for(us * fill) if us >= MIN_CARRIER_US else None


def _layer_fwd(x0, wl, consts, xfer=None, fill=1.25):
    cosf, sinf, logg = consts
    row = lambda a: a.reshape(1, -1)
    take = _taker(xfer, fill)
    h = _norm_fwd(x0, row(wl["norm1_g"]), row(wl["scale1"]), row(wl["shift1"]), "norm1_fwd", duty=take(9))
    w_in = _weight(wl, "w_in")
    p = _matmul(h, w_in, "nn", "in_proj", out_dtype=BF16, n_blocks=(0, TAIL0 // 512), duty=take(30))
    tail = _matmul(h, w_in, "nn", "in_tail", tn=128, n_blocks=(TAIL_BLK, 1))
    qk = _rope_fwd(p, cosf, sinf, duty=take(10))
    ret_raw = _ret_fwd(qk, p, logg, duty=take(27))
    br0 = _branch_post_fwd(ret_raw, p, row(wl["ret_norm_g"]), RG_BLK, True, "ret_post_fwd", duty=take(9))
    gla_raw, states = _gla_fwd(p, tail, wl["w2pad"], row(wl["b_gla_a"]), duty=take(26))
    br1 = _branch_post_fwd(gla_raw, p, row(wl["gla_norm_g"]), GG_BLK, False, "gla_post_fwd", duty=take(9))
    qkn, cum = _fox_prep_fwd(p, tail, row(wl["q_norm_g"]), row(wl["k_norm_g"]), row(wl["btail"]), duty=take(10))
    cum4 = cum[:, FF_LANE0:FF_LANE0 + N_HEADS].T
    cumcol, cumrow = cum4.reshape(N_HEADS, T, 1), cum4.reshape(N_HEADS, 1, T)
    fox_o, lse = _fox_fwd(qkn, p, cumcol, cumrow, duty=take(30))
    w_br_t = _weight(wl, "w_br_t")
    ys = [_matmul(b, w_br_t[n], "nt", "br_proj%d" % n, out_dtype=BF16) for n, b in enumerate((br0, br1, fox_o))]
    gpre = _matmul(h, _weight(wl, "w_mg_t"), "nt", "gate_proj", out_dtype=BF16, duty=take(20))
    mixed_in = _mix_fwd(gpre, row(wl["b_mg"]), *ys, duty=take(21))
    mixed = _matmul(mixed_in, _weight(wl, "w_o"), "nn", "o_proj", duty=take(10))
    x1, h2 = _norm_fwd(x0, row(wl["norm2_g"]), row(wl["scale2"]), row(wl["shift2"]), "norm2_fwd",
                       m=mixed, gate=row(wl["gate1"]), duty=take(13))
    uu = _matmul(h2, _weight(wl, "w_up_t"), "nt", "up_proj", out_dtype=BF16, duty=take(30))
    act = _ffn_act_fwd(uu, wl["w_conv"], row(wl["b_conv"]), duty=take(25))
    y = _matmul(act, _weight(wl, "w_down"), "nn", "down_proj", tk=1408, duty=take(24))
    x2 = _axpy(x1, y, row(wl["gate2"]), "resid2", duty=take(11))
    saved = dict(x0=x0, h=h, p=p, tail=tail, qk=qk, ret_raw=ret_raw, br0=br0, gla_raw=gla_raw, states=states, br1=br1,
                 qkn=qkn, cumcol=cumcol, cumrow=cumrow, fox_o=fox_o, lse=lse, y0=ys[0], y1=ys[1], y2=ys[2],
                 gpre=gpre, mixed_in=mixed_in, mixed=mixed, x1=x1, h2=h2, uu=uu, act=act, y=y)
    return x2, saved


def _layer_bwd(dx2, wl, sv, consts, xfer=None, layer=0):
    cosf, sinf, logg = consts
    row = lambda a: a.reshape(1, -1)
    take = _taker(xfer)

    send = functools.partial(_send_grad, xfer, layer)

    dy, st_g2 = _gate_bwd(dx2, sv["y"], row(wl["gate2"]), "gate2_bwd", duty=take(10))
    dact = _matmul(dy, _weight(wl, "w_down"), "nt", "down_dx", tn=1408, duty=take(21))
    d_down = send("w_down", _matmul(sv["act"], dy, "tn", "down_dw", out_dtype=BF16, tm=1408, duty=take(19)))
    duu, st_conv = _ffn_act_bwd(sv["uu"], wl["w_conv"], row(wl["b_conv"]), dact, duty=take(40))
    dh2 = _matmul(duu, _weight(wl, "w_up_t"), "nn", "up_dx", tk=1408, duty=take(42))
    d_up_t = send("w_up", _matmul(duu, sv["h2"], "tn", "up_dw", out_dtype=BF16, tm=1408, duty=take(33)))
    dx1, st_n2 = _norm_bwd(sv["x1"], dh2, dx2, row(wl["norm2_g"]), row(wl["scale2"]), row(wl["shift2"]), "norm2_bwd",
                           duty=take(15))
    dmixed, st_g1 = _gate_bwd(dx1, sv["mixed"], row(wl["gate1"]), "gate1_bwd", duty=take(10))
    dmi = _matmul(dmixed, _weight(wl, "w_o"), "nt", "o_dx", duty=take(11))
    d_o = send("w_o", _matmul(sv["mixed_in"], dmixed, "tn", "o_dw", out_dtype=BF16, duty=take(9)))
    dy0, dy1, dy2, dgpre, st_bmg = _mix_bwd(sv["gpre"], row(wl["b_mg"]), sv["y0"], sv["y1"], sv["y2"], dmi,
                                             duty=take(31))
    brs = (sv["br0"], sv["br1"], sv["fox_o"])
    w_br_t = _weight(wl, "w_br_t")
    dbr = [_matmul(d, w_br_t[n], "nn", "br_dx%d" % n) for n, d in enumerate((dy0, dy1, dy2))]
    d_br_t = [send("w_br%d" % n, _matmul(d, brs[n], "tn", "br_dw%d" % n, out_dtype=BF16))
              for n, d in enumerate((dy0, dy1, dy2))]
    dh = _matmul(dgpre, _weight(wl, "w_mg_t"), "nn", "gate_dx", tk=1024, duty=take(29))
    d_mg_t = send("w_mg", _matmul(dgpre, sv["h"], "tn", "gate_dw", out_dtype=BF16, duty=take(21)))
    p, tail = sv["p"], sv["tail"]
    dqn, dkn, dfv, drow, dcol = _fox_bwd(sv["qkn"], p, sv["cumcol"], sv["cumrow"], sv["lse"], sv["fox_o"], dbr[2],
                                         duty=take(50))
    dcum4 = drow.reshape(N_HEADS, T) - dcol.reshape(N_HEADS, T)
    dcum = jnp.pad(dcum4.T, ((0, 0), (FF_LANE0, 128 - FF_LANE0 - N_HEADS)))
    dfqk, dtail_fox, st_fox = _fox_prep_bwd(p, tail, row(wl["q_norm_g"]), row(wl["k_norm_g"]), row(wl["btail"]), dqn, dkn, dcum,
                                            duty=take(15))
    dgla_raw, dgg, st_gn = _branch_post_bwd(sv["gla_raw"], p, row(wl["gla_norm_g"]), dbr[1], GG_BLK, False, "gla_post_bwd",
                                            duty=take(12))
    dgq, dgk, dgv, dtail_gla, dw2pad, st_bg = _gla_bwd(p, tail, wl["w2pad"], row(wl["b_gla_a"]), sv["states"], dgla_raw,
                                                       duty=take(30))
    dret_raw, drg, st_rn = _branch_post_bwd(sv["ret_raw"], p, row(wl["ret_norm_g"]), dbr[0], RG_BLK, True, "ret_post_bwd",
                                            duty=take(13))
    dqr, dkr, drv = _ret_bwd(sv["qk"], p, logg, dret_raw, duty=take(50))
    drqk = _rope_bwd(dqr, dkr, cosf, sinf, duty=take(11))
    dp = jnp.concatenate([a.astype(BF16) for a in (drqk, drv, drg, dgq, dgk, dgv, dgg, dfqk, dfv, dtail_fox + dtail_gla)]
                         + [jnp.zeros((T, NP - TAIL0 - 128), BF16)], axis=1)
    dh = _matmul(dp, _weight(wl, "w_in"), "nt", "in_dx", tk=1408, add=dh, duty=take(45))
    d_in = send("w_in", _matmul(sv["h"], dp, "tn", "in_dw", out_dtype=BF16, duty=take(32)))
    dx0, st_n1 = _norm_bwd(sv["x0"], dh, dx1, row(wl["norm1_g"]), row(wl["scale1"]), row(wl["shift1"]), "norm1_bwd",
                           duty=take(15))
    big = dict(w_in=d_in, w_o=d_o, w_down=d_down, w_br0=d_br_t[0], w_br1=d_br_t[1], w_br2=d_br_t[2], w_mg=d_mg_t,
               w_up=d_up_t)
    dmod = jnp.concatenate([st_n1[2], st_n1[1], st_g1[0], st_n2[2], st_n2[1], st_g2[0]])
    small = dict(norm1_g=st_n1[0], norm2_g=st_n2[0], b_gla_a=st_bg[0], b_fox_f=st_fox[2, FF_LANE0:FF_LANE0 + N_HEADS],
                 ret_norm_g=st_rn[0], gla_norm_g=st_gn[0], q_norm_g=st_fox[0], k_norm_g=st_fox[1], b_mg=st_bmg[0],
                 b_conv=st_conv[3], w_gla_a2=dw2pad[:LR_LANES], w_conv=st_conv[0:3])
    return dx0, big, dmod, small


SMALL_REPL = ("norm1_g", "norm2_g", "b_ada", "b_gla_a", "b_fox_f", "ret_norm_g", "gla_norm_g", "q_norm_g", "k_norm_g",
              "b_mg", "b_conv")
SMALL_SHARDED = ("w_gla_a2", "w_conv")
BIG = ("w_in", "w_o", "w_down", "w_br", "w_mg", "w_up")
WEIGHTS = ("norm1_g", "norm2_g", "w_ada", "b_ada", "w_in", "w_gla_a2", "b_gla_a", "b_fox_f", "ret_norm_g", "gla_norm_g",
           "q_norm_g", "k_norm_g", "w_br", "w_mg", "b_mg", "w_o", "w_up", "w_conv", "b_conv", "w_down")


def kernel(x, c, norm1_g, norm2_g, w_ada, b_ada, w_in, w_gla_a2, b_gla_a, b_fox_f, ret_norm_g, gla_norm_g, q_norm_g, k_norm_g, w_br, w_mg, b_mg, w_o, w_up, w_conv, b_conv, w_down, loss_target, m_norm1_g, m_norm2_g, m_w_ada, m_b_ada, m_w_in, m_w_gla_a2, m_b_gla_a, m_b_fox_f, m_ret_norm_g, m_gla_norm_g, m_q_norm_g, m_k_norm_g, m_w_br, m_w_mg, m_b_mg, m_w_o, m_w_up, m_w_conv, m_b_conv, m_w_down, v_norm1_g, v_norm2_g, v_w_ada, v_b_ada, v_w_in, v_w_gla_a2, v_b_gla_a, v_b_fox_f, v_ret_norm_g, v_gla_norm_g, v_q_norm_g, v_k_norm_g, v_w_br, v_w_mg, v_b_mg, v_w_o, v_w_up, v_w_conv, v_b_conv, v_w_down):
    W = dict(norm1_g=norm1_g, norm2_g=norm2_g, w_ada=w_ada, b_ada=b_ada, w_in=w_in, w_gla_a2=w_gla_a2, b_gla_a=b_gla_a,
             b_fox_f=b_fox_f, ret_norm_g=ret_norm_g, gla_norm_g=gla_norm_g, q_norm_g=q_norm_g, k_norm_g=k_norm_g,
             w_br=w_br, w_mg=w_mg, b_mg=b_mg, w_o=w_o, w_up=w_up, w_conv=w_conv, b_conv=b_conv, w_down=w_down)
    M = dict(norm1_g=m_norm1_g, norm2_g=m_norm2_g, w_ada=m_w_ada, b_ada=m_b_ada, w_in=m_w_in, w_gla_a2=m_w_gla_a2,
             b_gla_a=m_b_gla_a, b_fox_f=m_b_fox_f, ret_norm_g=m_ret_norm_g, gla_norm_g=m_gla_norm_g, q_norm_g=m_q_norm_g,
             k_norm_g=m_k_norm_g, w_br=m_w_br, w_mg=m_w_mg, b_mg=m_b_mg, w_o=m_w_o, w_up=m_w_up, w_conv=m_w_conv,
             b_conv=m_b_conv, w_down=m_w_down)
    V = dict(norm1_g=v_norm1_g, norm2_g=v_norm2_g, w_ada=v_w_ada, b_ada=v_b_ada, w_in=v_w_in, w_gla_a2=v_w_gla_a2,
             b_gla_a=v_b_gla_a, b_fox_f=v_b_fox_f, ret_norm_g=v_ret_norm_g, gla_norm_g=v_gla_norm_g, q_norm_g=v_q_norm_g,
             k_norm_g=v_k_norm_g, w_br=v_w_br, w_mg=v_w_mg, b_mg=v_b_mg, w_o=v_w_o, w_up=v_w_up, w_conv=v_w_conv,
             b_conv=v_b_conv, w_down=v_w_down)
    me = 4 * lax.axis_index("x") + 2 * lax.axis_index("y") + lax.axis_index("c")
    x2d, tgt = x.reshape(T, D), loss_target.reshape(T, D)

    sm = _flat_pack([c, w_gla_a2, w_conv])
    sm_all = _exchange(sm, True, "gather_small")
    parts = [_flat_unpack(sm_all[j], [(D,), (DEPTH, LR_LANES, 32), (DEPTH, 3, 352)]) for j in range(N_DEV)]
    c_all = jnp.stack([q[0] for q in parts])
    w_gla_full = jnp.concatenate([q[1] for q in parts], axis=2)
    w_conv_full = jnp.concatenate([q[2] for q in parts], axis=2)

    n_ada = w_ada.shape[2]
    b_loc = lax.dynamic_slice_in_dim(b_ada, me * n_ada, n_ada, axis=1).reshape(DEPTH, 1, n_ada)
    mod_all = _ada_fwd(c_all, w_ada, b_loc)
    mod_recv = _exchange(jnp.swapaxes(mod_all, 0, 1), False, "a2a_mod")
    mod = jnp.swapaxes(mod_recv, 0, 1).reshape(DEPTH, 6, D)

    loc = dict(w_in=_permute_in(w_in), w_o=w_o, w_down=w_down, w_br=jnp.swapaxes(w_br, 2, 3),
               w_mg=jnp.swapaxes(w_mg, 1, 2), w_up=jnp.swapaxes(w_up, 1, 2))
    loc = {k: v.astype(BF16) for k, v in loc.items()}
    w_full = dict(w_in=(D, NP), w_o=(D, D), w_down=(D_FF, D), w_br=(3, D, BW), w_mg=(3 * D, D), w_up=(2 * D_FF, D))
    w_parts = dict(w_in=8, w_br=2, w_mg=4, w_o=1, w_up=11, w_down=2)
    gather, units = _Transfers("gather"), []
    for l in range(DEPTH):
        for k, parts in w_parts.items():
            axis = 1 if k == "w_br" else 0
            n = w_full[k][axis] // N_DEV
            gather.lands[(l, k)] = lax.empty(w_full[k], BF16)
            shard = loc[k][l]
            nbytes = shard.size * 2 // parts
            units += [((l, k), (axis, n, (), c * (n // parts), n // parts), shard, nbytes) for c in range(parts)]
    first, lag = w_parts["w_in"], 4
    order = [("cross", i) for i in range(first)] + [("pass", i) for i in range(first)]
    for i in range(first, len(units) + lag):
        order += [("cross", i)] if i < len(units) else []
        order += [("pass", i - lag)] if i - lag >= first else []
    for what, i in order:
        key, where, shard, nbytes = units[i]
        if what == "cross":
            gather.add(key, ("gather_chip",) + where, shard, uid=i, nbytes=nbytes)
        else:
            gather.add(key, ("pass_on",) + where, after=i, nbytes=nbytes)

    w2pad = jnp.pad(w_gla_full, ((0, 0), (0, 128 - LR_LANES), (0, 0)))
    btail = jnp.pad(b_fox_f, ((0, 0), (FF_LANE0, 128 - FF_LANE0 - N_HEADS)))
    stacked = dict(norm1_g=norm1_g, norm2_g=norm2_g, b_gla_a=b_gla_a, ret_norm_g=ret_norm_g, gla_norm_g=gla_norm_g,
                   q_norm_g=q_norm_g, k_norm_g=k_norm_g, b_mg=b_mg, b_conv=b_conv, w_conv=w_conv_full, w2pad=w2pad,
                   btail=btail, shift1=mod[:, 0], scale1=mod[:, 1], gate1=mod[:, 2], shift2=mod[:, 3], scale2=mod[:, 4],
                   gate2=mod[:, 5])
    landed = lambda l, k: functools.partial(gather.get, (l, k))
    layers = [dict({k: v[l] for k, v in stacked.items()}, w_in=landed(l, "w_in"), w_o=landed(l, "w_o"),
                   w_down=landed(l, "w_down"), w_br_t=landed(l, "w_br"), w_mg_t=landed(l, "w_mg"), w_up_t=landed(l, "w_up"))
              for l in range(DEPTH)]
    consts = _rope_tables() + (_ret_logg(),)

    xc, saved = x2d, []
    for l in range(DEPTH):
        xc, sv = _layer_fwd(xc, layers[l], consts, gather, 1.5 if l == 0 else 1.25)
        saved.append(sv)
    loss_part, dxc = _loss_fwd_bwd(xc, tgt)
    loss = lax.psum(loss_part[0, 0], ("x", "y", "c"))

    grad_names = ("w_in", "w_o", "w_down", "w_br0", "w_br1", "w_br2", "w_mg", "w_up")
    blk_rows = dict(w_in=(128, NP), w_o=(128, D), w_down=(352, D), w_br0=(128, BW), w_br1=(128, BW), w_br2=(128, BW),
                    w_mg=(384, D), w_up=(704, D))
    grads = _Transfers("grads")
    for k in grad_names:
        grads.lands[k] = lax.empty((N_DEV // 2 if k in GRAD_VIA_CHIP else N_DEV, DEPTH) + blk_rows[k], BF16)
    dmod, small_g = [None] * DEPTH, [None] * DEPTH
    for l in reversed(range(DEPTH)):
        dxc, _, dmod[l], small_g[l] = _layer_bwd(dxc, layers[l], saved[l], consts, grads, l)
    grad_x = dxc
    dmod = jnp.stack(dmod)
    small_g = {k: jnp.stack([s[k] for s in small_g]) for k in small_g[0]}
    grads.drain()
    recv = {k: grads.get(k) for k in grad_names}

    dmod_send = jnp.swapaxes(dmod.reshape(DEPTH, N_DEV, n_ada), 0, 1)
    dmod_all = jnp.swapaxes(_exchange(dmod_send, False, "a2a_dmod"), 0, 1)
    g_ada = _ada_bwd(c_all, dmod_all)

    def flat(a, k):
        return a.reshape((-1, W[k].shape[-1]))

    def adam_nat(k, g, tr):
        outs = _adamw(g, flat(W[k], k), flat(M[k], k), flat(V[k], k), tr, "adamw_" + k)
        return [o.reshape(W[k].shape) for o in outs]

    def summed(k, tr):
        r = recv[k]
        return _sum_partials(r.reshape(r.shape[0], DEPTH * r.shape[2], r.shape[3]), "sum_" + k, tr).reshape((DEPTH,) + r.shape[2:])

    big_out = dict(
        w_in=adam_nat("w_in", flat(_unpermute_in(summed("w_in", 64)), "w_in"), 64),
        w_o=adam_nat("w_o", recv["w_o"].reshape(N_DEV, DEPTH * 128, D), 128),
        w_down=adam_nat("w_down", recv["w_down"].reshape(N_DEV // 2, DEPTH * 352, D), 352),
        w_br=adam_nat("w_br", flat(jnp.swapaxes(jnp.stack([summed("w_br%d" % n, 128) for n in range(3)], axis=1), 2, 3),
                                   "w_br"), 1024),
        w_mg=adam_nat("w_mg", flat(jnp.swapaxes(summed("w_mg", 384), 1, 2), "w_mg"), 512),
        w_up=adam_nat("w_up", flat(jnp.swapaxes(summed("w_up", 704), 1, 2), "w_up"), 512))
    ada_out = [o.reshape(DEPTH, D, n_ada) for o in _adamw(
        g_ada.reshape(DEPTH * D, n_ada), w_ada.reshape(DEPTH * D, n_ada), m_w_ada.reshape(DEPTH * D, n_ada),
        v_w_ada.reshape(DEPTH * D, n_ada), 512, "adamw_ada")]

    small_g = dict(small_g, b_ada=dmod)
    names = SMALL_REPL + SMALL_SHARDED
    full_shapes = [W[n].shape for n in SMALL_REPL] + [(DEPTH, LR_LANES, 256), (DEPTH, 3, D_FF)]
    part = _flat_pack([small_g[n] for n in names])
    total = _flat_unpack(_sum_partials(_exchange(part, True, "gather_small_grads"), "sum_small"), full_shapes)
    total = dict(zip(names, total))
    total["w_gla_a2"] = lax.dynamic_slice_in_dim(total["w_gla_a2"], me * 32, 32, axis=2)
    total["w_conv"] = lax.dynamic_slice_in_dim(total["w_conv"], me * 352, 352, axis=2)
    shapes = [W[n].shape for n in names]
    small_out = _adamw(_flat_pack([total[n] for n in names]), _flat_pack([W[n] for n in names]),
                       _flat_pack([M[n] for n in names]), _flat_pack([V[n] for n in names]), None, "adamw_small")
    small_out = [dict(zip(names, _flat_unpack(o, shapes))) for o in small_out]

    outs = []
    for k in range(4):
        d = dict(small_out[k])
        d.update({n: big_out[n][k] for n in BIG})
        d["w_ada"] = ada_out[k]
        outs.append([d[n] for n in WEIGHTS])
    return (loss, grad_x.reshape(1, T, D), *outs[0], *outs[1], *outs[2], *outs[3])
```

```python
import functools

import numpy as np
import jax
import jax.numpy as jnp
from jax import lax
from jax.experimental import pallas as pl
from jax.experimental.pallas import tpu as pltpu

F32 = jnp.float32
BF16 = jnp.bfloat16

N_DEV = 8
T = 2048
D = 1024
DEPTH = 4
N_HEADS = 4
HD = 128
BW = 512
D_FF = 2816
CHUNK = 64
EPS = 1e-6
IN_W = 5140
NP = 5632
TAIL0 = 5120
LR_LANES = 16
FF_LANE0 = 16
PACK_W = 1024
SEG_ROWS = (704, 128, 352, 192, 384, 704)
LAYER_ROWS = sum(SEG_ROWS)
VMEM_LIMIT_V7X = 56 * 1024 * 1024

ADAM_LR, ADAM_B1, ADAM_B2, ADAM_EPS, ADAM_WD, ADAM_STEP = 0.001, 0.9, 0.999, 1e-08, 0.01, 10

MESH_ID = pl.DeviceIdType.MESH


def _cp(*sem):
    return pltpu.CompilerParams(dimension_semantics=sem if sem else None, vmem_limit_bytes=VMEM_LIMIT_V7X)


def _sigmoid(z):
    return 1.0 / (1.0 + jnp.exp(-z))


def _log_sigmoid(z):
    return jnp.minimum(z, 0.0) - jnp.log(1.0 + jnp.exp(-jnp.abs(z)))


def _sum0(a):
    return jnp.sum(a, axis=0, keepdims=True)


def _mean1(a):
    return jnp.mean(a, axis=-1, keepdims=True)


def _dot(a, b, dims):
    return lax.dot_general(a.astype(BF16), b.astype(BF16), (dims, ((), ())), preferred_element_type=F32)


NN = ((1,), (0,))
NT = ((1,), (1,))
TN = ((0,), (0,))


def _exact_dot(m01, a):
    a1 = a.astype(BF16)
    r1 = a - a1.astype(F32)
    a2 = r1.astype(BF16)
    a3 = (r1 - a2.astype(F32)).astype(BF16)
    d = lambda z: jnp.dot(m01, z, preferred_element_type=F32)
    return d(a1) + d(a2) + d(a3)


def _tri(n, upper):
    r = lax.broadcasted_iota(jnp.int32, (n, n), 0)
    c = lax.broadcasted_iota(jnp.int32, (n, n), 1)
    return jnp.where((c >= r) if upper else (c <= r), 1.0, 0.0).astype(BF16)


def _exchange(x, gather, name):
    blk = x.shape if gather else x.shape[1:]

    def body(x_ref, o_ref, send_sems, recv_sems, loc_sem):
        mx, my, mc = lax.axis_index("x"), lax.axis_index("y"), lax.axis_index("c")
        me = 4 * mx + 2 * my + mc
        loc = pltpu.make_async_copy(x_ref if gather else x_ref.at[me], o_ref.at[me], loc_sem)
        loc.start()
        copies = []
        for k in range(1, N_DEV):
            px = mx ^ (k >> 2) if (k >> 2) else mx
            py = my ^ ((k >> 1) & 1) if ((k >> 1) & 1) else my
            pc = mc ^ (k & 1) if (k & 1) else mc
            peer = 4 * px + 2 * py + pc
            cp = pltpu.make_async_remote_copy(
                src_ref=x_ref if gather else x_ref.at[peer], dst_ref=o_ref.at[me],
                send_sem=send_sems.at[k - 1], recv_sem=recv_sems.at[k - 1],
                device_id=(px, py, pc), device_id_type=MESH_ID)
            cp.start()
            copies.append(cp)
        for cp in copies:
            cp.wait()
        loc.wait()

    return pl.pallas_call(
        body, name=name,
        out_shape=jax.ShapeDtypeStruct((N_DEV,) + tuple(blk), x.dtype),
        in_specs=[pl.BlockSpec(memory_space=pl.ANY)],
        out_specs=pl.BlockSpec(memory_space=pl.ANY),
        scratch_shapes=[pltpu.SemaphoreType.DMA((N_DEV - 1,)), pltpu.SemaphoreType.DMA((N_DEV - 1,)),
                        pltpu.SemaphoreType.DMA],
        compiler_params=pltpu.CompilerParams(has_side_effects=True),
    )(x)


def _blk(ref, axis, j, n, r0=0, nr=None):
    return ref.at[(slice(None),) * axis + (pl.ds(j * n + r0, n if nr is None else nr),)]


def _comm_copies(items, srcs, lands, send_sems, recv_sems, loc_sems):
    mx, my, mc = lax.axis_index("x"), lax.axis_index("y"), lax.axis_index("c")
    me = 4 * mx + 2 * my + mc
    local, remote = [], []
    for t, (kind, axis, n, sel, r0, nr, si, li) in enumerate(items):
        if kind == "pass_on":
            for q in (2, 4, 6):
                px = 1 - mx if q & 4 else mx
                py = 1 - my if q & 2 else my
                rows = _blk(lands[li], axis, 4 * px + 2 * py + mc, n, r0, nr)
                remote.append(pltpu.make_async_remote_copy(
                    src_ref=rows, dst_ref=rows, send_sem=send_sems.at[t * (N_DEV - 1) + q - 1],
                    recv_sem=recv_sems.at[t * (N_DEV - 1) + q - 1], device_id=(mx, my, 1 - mc), device_id_type=MESH_ID))
            continue
        if kind == "to_other_core":
            for p in range(N_DEV // 2):
                remote.append(pltpu.make_async_remote_copy(
                    src_ref=_blk(srcs[si], axis, 2 * p + 1 - mc, n, r0, nr), dst_ref=lands[li].at[p, pl.ds(r0, nr)],
                    send_sem=send_sems.at[t * (N_DEV - 1) + p], recv_sem=recv_sems.at[t * (N_DEV - 1) + p],
                    device_id=(mx, my, 1 - mc), device_id_type=MESH_ID))
            continue
        if kind == "a2a_chip":
            pm = 2 * mx + my
            mine = lands[li].at[(pm,) + tuple(sel) + (pl.ds(r0, nr),)]
            local.append(pltpu.make_async_copy(srcs[si].at[pm, pl.ds(r0, nr)], mine, loc_sems.at[t]))
            for q in (2, 4, 6):
                px = 1 - mx if q & 4 else mx
                py = 1 - my if q & 2 else my
                remote.append(pltpu.make_async_remote_copy(
                    src_ref=srcs[si].at[2 * px + py, pl.ds(r0, nr)], dst_ref=mine,
                    send_sem=send_sems.at[t * (N_DEV - 1) + q - 1], recv_sem=recv_sems.at[t * (N_DEV - 1) + q - 1],
                    device_id=(px, py, mc), device_id_type=MESH_ID))
            continue
        if kind == "a2a":
            mine = lands[li].at[(me,) + tuple(sel) + (pl.ds(r0, nr),)]
            own = _blk(srcs[si], axis, me, n, r0, nr)
        else:
            mine = _blk(lands[li], axis, me, n, r0, nr)
            own = _blk(srcs[si], axis, 0, n, r0, nr)
        local.append(pltpu.make_async_copy(own, mine, loc_sems.at[t]))
        for k in ((1, 2, 4, 6) if kind == "gather_chip" else range(1, N_DEV)):
            px = 1 - mx if k & 4 else mx
            py = 1 - my if k & 2 else my
            pc = 1 - mc if k & 1 else mc
            src = _blk(srcs[si], axis, 4 * px + 2 * py + pc, n, r0, nr) if kind == "a2a" else own
            remote.append(pltpu.make_async_remote_copy(
                src_ref=src, dst_ref=mine, send_sem=send_sems.at[t * (N_DEV - 1) + k - 1],
                recv_sem=recv_sems.at[t * (N_DEV - 1) + k - 1], device_id=(px, py, pc), device_id_type=MESH_ID))
    return local, remote


def _comm_scratch(n_items):
    return [pltpu.SemaphoreType.DMA((n_items * (N_DEV - 1),)), pltpu.SemaphoreType.DMA((n_items * (N_DEV - 1),)),
            pltpu.SemaphoreType.DMA((n_items,))]


LINK_BYTES_PER_US = dict(gather_chip=23e3, a2a_chip=23e3, a2a=11.5e3, gather=11.5e3, pass_on=200e3, to_other_core=150e3)
CALL_EXCHANGE_US = 3.0


class _Duty:
    def __init__(self, items, srcs, lands, done):
        self.items, self.srcs, self.lands, self.done = items, srcs, lands, done


def _pcall(body, name, grid, in_specs, out_specs, out_shape, args, scratch_shapes=(), sem=(), duty=None):
    if duty is None:
        return pl.pallas_call(body, name=name, grid=grid, in_specs=list(in_specs), out_specs=out_specs,
                              out_shape=out_shape, scratch_shapes=list(scratch_shapes), compiler_params=_cp(*sem))(*args)
    single = not isinstance(out_shape, (list, tuple))
    o_shape = [out_shape] if single else list(out_shape)
    o_specs = [out_specs] if single else list(out_specs)
    n_in, n_out, n_scr = len(in_specs), len(o_shape), len(scratch_shapes)
    n_src, n_land, n_items = len(duty.srcs), len(duty.lands), len(duty.items)
    a0 = n_in + n_src + n_land

    def wrapped(*refs):
        srcs = refs[n_in:n_in + n_src]
        lands = refs[a0 + n_out:a0 + n_out + n_land]
        core = refs[:n_in] + refs[a0:a0 + n_out] + refs[a0 + n_out + n_land:a0 + n_out + n_land + n_scr]
        sems = refs[a0 + n_out + n_land + n_scr:]
        first = functools.reduce(jnp.logical_and, [pl.program_id(a) == 0 for a in range(len(grid))])
        last = functools.reduce(jnp.logical_and, [pl.program_id(a) == g - 1 for a, g in enumerate(grid)])

        @pl.when(first)
        def _():
            local, remote = _comm_copies(duty.items, srcs, lands, *sems)
            for cp in local + remote:
                cp.start()

        body(*core)

        @pl.when(last)
        def _():
            local, remote = _comm_copies(duty.items, srcs, lands, *sems)
            for cp in remote + local:
                cp.wait()

    hbm = pl.BlockSpec(memory_space=pl.ANY)
    res = pl.pallas_call(
        wrapped, name=name, grid=grid,
        in_specs=list(in_specs) + [hbm] * (n_src + n_land), out_specs=o_specs + [hbm] * n_land,
        out_shape=o_shape + [jax.ShapeDtypeStruct(a.shape, a.dtype) for a in duty.lands],
        input_output_aliases={n_in + n_src + t: n_out + t for t in range(n_land)},
        scratch_shapes=list(scratch_shapes) + _comm_scratch(n_items),
        compiler_params=pltpu.CompilerParams(dimension_semantics=("arbitrary",) * len(grid),
                                             vmem_limit_bytes=VMEM_LIMIT_V7X, has_side_effects=True),
    )(*args, *duty.srcs, *duty.lands)
    duty.done(res[n_out:])
    return res[0] if single else res[:n_out]


def _comm(duty, name):
    n_src, n_land = len(duty.srcs), len(duty.lands)

    def body(*refs):
        local, remote = _comm_copies(duty.items, refs[:n_src], refs[n_src + n_land:n_src + 2 * n_land],
                                     *refs[n_src + 2 * n_land:])
        for cp in local + remote:
            cp.start()
        for cp in remote + local:
            cp.wait()

    hbm = pl.BlockSpec(memory_space=pl.ANY)
    duty.done(pl.pallas_call(
        body, name=name, out_shape=[jax.ShapeDtypeStruct(a.shape, a.dtype) for a in duty.lands],
        in_specs=[hbm] * (n_src + n_land), out_specs=[hbm] * n_land,
        input_output_aliases={n_src + t: t for t in range(n_land)},
        scratch_shapes=_comm_scratch(len(duty.items)), compiler_params=pltpu.CompilerParams(has_side_effects=True),
    )(*duty.srcs, *duty.lands))


class _Transfers:
    def __init__(self, name):
        self.name, self.queue, self.lands, self.flushes, self.groups = name, [], {}, 0, {}

    def add(self, key, item, src=None, uid=None, after=None, group=None, nbytes=0):
        self.queue.append((key, item, src, uid, after, group, nbytes / LINK_BYTES_PER_US[item[0]]))
        if group is not None:
            self.groups[group] = [self.groups.get(group, [0, None])[0] + 1, None]

    def when_done(self, group, fn):
        self.groups[group][1] = fn

    def take_for(self, us):
        count, busy = 0, CALL_EXCHANGE_US
        while count < len(self.queue) and busy + self.queue[count][6] <= us:
            busy += self.queue[count][6]
            count += 1
        return self.take(count) if count else None

    def take(self, count):
        units = []
        while self.queue and len(units) < count:
            after = self.queue[0][4]
            if after is not None and any(u[3] == after for u in units):
                break
            units.append(self.queue.pop(0))
        if not units:
            return None
        keys, srcs, items = [], [], []
        for key, item, src, _, _, _, _ in units:
            if key not in keys:
                keys.append(key)
            if src is not None and not any(src is s for s in srcs):
                srcs.append(src)
            si = [i for i, s in enumerate(srcs) if s is src][0] if src is not None else -1
            items.append(tuple(item) + (si, keys.index(key)))

        def done(new_lands):
            for key, arr in zip(keys, new_lands):
                self.lands[key] = arr
            for u in units:
                if u[5] is not None:
                    self.groups[u[5]][0] -= 1
                    if self.groups[u[5]][0] == 0:
                        self.groups[u[5]][1]()

        return _Duty(items, srcs, [self.lands[k] for k in keys], done)

    def drain(self, upto=None):
        count = upto
        while self.queue if upto is None else count > 0:
            duty = self.take(len(self.queue) if upto is None else count)
            count = None if upto is None else count - len(duty.items)
            self.flushes += 1
            _comm(duty, "%s_flush%d" % (self.name, self.flushes))

    def get(self, key):
        pending = [i for i, u in enumerate(self.queue) if u[0] == key]
        if pending:
            self.drain(pending[-1] + 1)
        return self.lands[key]


def _matmul(a, b, mode, name, out_dtype=F32, tm=1024, tn=512, tk=None, add=None, n_blocks=None, duty=None):
    halves = a.ndim == 3
    if mode == "tn":
        K, M = a.shape[-2], a.shape[-1] * (2 if halves else 1)
        N = b.shape[1]
    else:
        M, K = a.shape[-2], a.shape[-1] * (2 if halves else 1)
        N = b.shape[0] if mode == "nt" else b.shape[1]
    tm, tn = min(tm, M), min(tn, N)
    j0 = 0
    if n_blocks is not None:
        j0, N = n_blocks[0], n_blocks[1] * tn
    tk = K if tk is None else tk
    nk = K // tk
    assert M % tm == 0 and N % tn == 0 and K % tk == 0, (name, M, N, K, tm, tn, tk)
    dims = {"nn": NN, "nt": NT, "tn": TN}[mode]
    has_add = add is not None

    def body(*refs):
        a_ref, b_ref = refs[:2]
        add_ref = refs[2] if has_add else None
        o_ref = refs[3 if has_add else 2]
        part = _dot(a_ref[...], b_ref[...], dims)

        def finish(total):
            if has_add:
                total = total + add_ref[...]
            o_ref[...] = total.astype(o_ref.dtype)

        if nk == 1:
            finish(part)
            return
        acc_ref = refs[-1]
        k = pl.program_id(2)

        @pl.when(k == 0)
        def _():
            acc_ref[...] = part

        @pl.when((k > 0) & (k < nk - 1))
        def _():
            acc_ref[...] += part

        @pl.when(k == nk - 1)
        def _():
            finish(acc_ref[...] + part)

    if halves and mode == "tn":
        per = a.shape[-1] // tm
        a_spec = pl.BlockSpec((None, tk, tm), lambda i, j, k: (i // per, k, i % per))
    elif halves:
        per = a.shape[-1] // tk
        a_spec = pl.BlockSpec((None, tm, tk), lambda i, j, k: (k // per, i, k % per))
    elif mode == "tn":
        a_spec = pl.BlockSpec((tk, tm), lambda i, j, k: (k, i))
    else:
        a_spec = pl.BlockSpec((tm, tk), lambda i, j, k: (i, k))
    if mode == "nt":
        b_spec = pl.BlockSpec((tn, tk), lambda i, j, k: (j0 + j, k))
    else:
        b_spec = pl.BlockSpec((tk, tn), lambda i, j, k: (k, j0 + j))
    o_spec = pl.BlockSpec((tm, tn), lambda i, j, k: (i, j))
    in_specs = [a_spec, b_spec] + ([o_spec] if has_add else [])
    args = (a, b) + ((add,) if has_add else ())
    return _pcall(
        body, name=name, grid=(M // tm, N // tn, nk),
        out_shape=jax.ShapeDtypeStruct((M, N), out_dtype),
        in_specs=in_specs, out_specs=o_spec,
        scratch_shapes=[pltpu.VMEM((tm, tn), F32)] if nk > 1 else [],
        sem=("parallel", "parallel", "arbitrary"), args=args, duty=duty)


def _ada_fwd(c_all, w_ada, b_loc):
    n = w_ada.shape[2]

    def body(c_ref, w_ref, b_ref, o_ref):
        c = c_ref[...]
        o_ref[0] = _dot(c * _sigmoid(c), w_ref[0], NN) + b_ref[0]

    return pl.pallas_call(
        body, name="ada_fwd", grid=(DEPTH,),
        out_shape=jax.ShapeDtypeStruct((DEPTH, N_DEV, n), F32),
        in_specs=[pl.BlockSpec((N_DEV, D), lambda l: (0, 0)),
                  pl.BlockSpec((1, D, n), lambda l: (l, 0, 0)),
                  pl.BlockSpec((1, 1, n), lambda l: (l, 0, 0))],
        out_specs=pl.BlockSpec((1, N_DEV, n), lambda l: (l, 0, 0)),
        compiler_params=_cp("parallel"),
    )(c_all, w_ada, b_loc)


def _ada_bwd(c_all, dmod_all):
    n = dmod_all.shape[2]

    def body(c_ref, d_ref, o_ref):
        c = c_ref[...]
        o_ref[0] = _dot(c * _sigmoid(c), d_ref[0], TN)

    return pl.pallas_call(
        body, name="ada_bwd", grid=(DEPTH,),
        out_shape=jax.ShapeDtypeStruct((DEPTH, D, n), F32),
        in_specs=[pl.BlockSpec((N_DEV, D), lambda l: (0, 0)),
                  pl.BlockSpec((1, N_DEV, n), lambda l: (l, 0, 0))],
        out_specs=pl.BlockSpec((1, D, n), lambda l: (l, 0, 0)),
        compiler_params=_cp("parallel"),
    )(c_all, dmod_all)


ROW_TILE = 256


def _row_spec(w=D, col=0):
    return pl.BlockSpec((ROW_TILE, w), lambda i: (i, col))


def _vec_spec(w=D):
    return pl.BlockSpec((1, w), lambda i: (0, 0))


def _norm_fwd(x, g, scale, shift, name, m=None, gate=None, duty=None):
    has_res = m is not None

    def body(*refs):
        if has_res:
            x_ref, m_ref, gate_ref, g_ref, sc_ref, sh_ref, xo_ref, h_ref = refs
            xv = x_ref[...] + gate_ref[...] * m_ref[...]
            xo_ref[...] = xv
        else:
            x_ref, g_ref, sc_ref, sh_ref, h_ref = refs
            xv = x_ref[...]
        r = lax.rsqrt(_mean1(xv * xv) + EPS)
        h_ref[...] = ((xv * r * g_ref[...]) * (1.0 + sc_ref[...]) + sh_ref[...]).astype(BF16)

    ins = [x] + ([m, gate] if has_res else []) + [g, scale, shift]
    in_specs = [_row_spec()] + ([_row_spec(), _vec_spec()] if has_res else []) + [_vec_spec()] * 3
    out_shape = [jax.ShapeDtypeStruct((T, D), BF16)]
    out_specs = [_row_spec()]
    if has_res:
        out_shape = [jax.ShapeDtypeStruct((T, D), F32)] + out_shape
        out_specs = [_row_spec()] + out_specs
    out = _pcall(body, name=name, grid=(T // ROW_TILE,), out_shape=out_shape, in_specs=in_specs,
                 out_specs=out_specs, sem=("parallel",), args=ins, duty=duty)
    return out if has_res else out[0]


def _norm_bwd(x, dh, dres, g, scale, shift, name, duty=None):
    def body(x_ref, dh_ref, dres_ref, g_ref, sc_ref, sh_ref, dx_ref, st_ref):
        xv, dh_v, gv = x_ref[...], dh_ref[...], g_ref[...]
        r = lax.rsqrt(_mean1(xv * xv) + EPS)
        n = xv * r
        dy = dh_v * (1.0 + sc_ref[...])
        dn = dy * gv
        dx_ref[...] = r * (dn - n * _mean1(dn * n)) + dres_ref[...]

        @pl.when(pl.program_id(0) == 0)
        def _():
            st_ref[...] = jnp.zeros_like(st_ref)

        st_ref[0:1, :] += _sum0(dy * n)
        st_ref[1:2, :] += _sum0(dh_v * (n * gv))
        st_ref[2:3, :] += _sum0(dh_v)

    return _pcall(
        body, name=name, grid=(T // ROW_TILE,),
        out_shape=[jax.ShapeDtypeStruct((T, D), F32), jax.ShapeDtypeStruct((8, D), F32)],
        in_specs=[_row_spec(), _row_spec(), _row_spec(), _vec_spec(), _vec_spec(), _vec_spec()],
        out_specs=[_row_spec(), pl.BlockSpec((8, D), lambda i: (0, 0))],
        sem=("arbitrary",), args=(x, dh, dres, g, scale, shift), duty=duty)


def _axpy(x, m, gate, name, duty=None):
    def body(x_ref, m_ref, gate_ref, o_ref):
        o_ref[...] = x_ref[...] + gate_ref[...] * m_ref[...]

    return _pcall(
        body, name=name, grid=(T // ROW_TILE,), out_shape=jax.ShapeDtypeStruct((T, D), F32),
        in_specs=[_row_spec(), _row_spec(), _vec_spec()], out_specs=_row_spec(),
        sem=("parallel",), args=(x, m, gate), duty=duty)


def _gate_bwd(dx, m, gate, name, duty=None):
    def body(dx_ref, m_ref, gate_ref, dm_ref, st_ref):
        dxv = dx_ref[...]
        dm_ref[...] = (gate_ref[...] * dxv).astype(BF16)

        @pl.when(pl.program_id(0) == 0)
        def _():
            st_ref[...] = jnp.zeros_like(st_ref)

        st_ref[0:1, :] += _sum0(dxv * m_ref[...])

    return _pcall(
        body, name=name, grid=(T // ROW_TILE,),
        out_shape=[jax.ShapeDtypeStruct((T, D), BF16), jax.ShapeDtypeStruct((8, D), F32)],
        in_specs=[_row_spec(), _row_spec(), _vec_spec()],
        out_specs=[_row_spec(), pl.BlockSpec((8, D), lambda i: (0, 0))],
        sem=("arbitrary",), args=(dx, m, gate), duty=duty)


def _loss_fwd_bwd(y, target):
    def body(y_ref, t_ref, l_ref, d_ref):
        e = y_ref[...] - t_ref[...]
        d_ref[...] = e * (1.0 / D)

        @pl.when(pl.program_id(0) == 0)
        def _():
            l_ref[...] = jnp.zeros_like(l_ref)

        l_ref[...] += jnp.sum(_sum0(e * e), axis=1, keepdims=True) * (0.5 / D)

    return pl.pallas_call(
        body, name="loss", grid=(T // ROW_TILE,),
        out_shape=[jax.ShapeDtypeStruct((8, 128), F32), jax.ShapeDtypeStruct((T, D), F32)],
        in_specs=[_row_spec(), _row_spec()],
        out_specs=[pl.BlockSpec((8, 128), lambda i: (0, 0)), _row_spec()],
        compiler_params=_cp("arbitrary"))(y, target)


def _rope_tables():
    half = HD // 2
    inv_freq = 10000.0 ** (-jnp.arange(half, dtype=F32) / half)
    ang = jnp.arange(T, dtype=F32)[:, None] * inv_freq[None, :]
    cos, sin = jnp.cos(ang), jnp.sin(ang)
    return jnp.concatenate([cos, cos], axis=1), jnp.concatenate([-sin, sin], axis=1)


def _rope_fwd(p, cosf, sinf, duty=None):
    def body(p_ref, c_ref, s_ref, o_ref):
        cv, sv = c_ref[...], s_ref[...]
        for j in range(2 * N_HEADS):
            xv = p_ref[:, j * HD:(j + 1) * HD].astype(F32)
            rot = xv * cv + pltpu.roll(xv, HD // 2, 1) * sv
            if j >= N_HEADS:
                rot = rot * (HD ** -0.5)
            o_ref[:, j * HD:(j + 1) * HD] = rot.astype(BF16)

    return _pcall(
        body, name="rope_fwd", grid=(T // ROW_TILE,),
        out_shape=jax.ShapeDtypeStruct((T, 2 * BW), BF16),
        in_specs=[_row_spec(2 * BW), _row_spec(HD), _row_spec(HD)], out_specs=_row_spec(2 * BW),
        sem=("parallel",), args=(p, cosf, sinf), duty=duty)


def _rope_bwd(dq, dk, cosf, sinf, duty=None):
    def body(dq_ref, dk_ref, c_ref, s_ref, o_ref):
        cv, sv = c_ref[...], s_ref[...]
        for j in range(2 * N_HEADS):
            h = j % N_HEADS
            d = dq_ref[:, h * HD:(h + 1) * HD] if j < N_HEADS else dk_ref[:, h * HD:(h + 1) * HD] * (HD ** -0.5)
            o_ref[:, j * HD:(j + 1) * HD] = d * cv + pltpu.roll(d * sv, HD // 2, 1)

    return _pcall(
        body, name="rope_bwd", grid=(T // ROW_TILE,),
        out_shape=jax.ShapeDtypeStruct((T, 2 * BW), F32),
        in_specs=[_row_spec(BW), _row_spec(BW), _row_spec(HD), _row_spec(HD)], out_specs=_row_spec(2 * BW),
        sem=("parallel",), args=(dq, dk, cosf, sinf), duty=duty)


TQ = 256
V_RET_BLK = 8


def _ret_logg():
    lg = jnp.log1p(-jnp.exp2(-5.0 - jnp.arange(N_HEADS, dtype=F32)))
    return jnp.broadcast_to(lg[:, None, None], (N_HEADS, 1, 128))


def _block_iotas(i, kl):
    rows = lax.broadcasted_iota(jnp.int32, (TQ, kl), 0) + i * TQ
    cols = lax.broadcasted_iota(jnp.int32, (TQ, kl), 1)
    return rows, cols


def _ret_weight(lg_ref, i, kl):
    rows, cols = _block_iotas(i, kl)
    dist = jnp.abs(rows - cols).astype(F32)
    w = jnp.exp(dist * lg_ref[0][:, 0:1])
    return jnp.where((cols >> 6) <= (rows >> 6), w, 0.0)


def _per_query_block(i, fn):
    for n in range(1, T // TQ + 1):
        pl.when(i == n - 1)(functools.partial(fn, n * TQ))


def _ret_specs():
    q_spec = pl.BlockSpec((TQ, HD), lambda h, i: (i, h))
    k_spec = pl.BlockSpec((T, HD), lambda h, i: (0, N_HEADS + h))
    v_spec = pl.BlockSpec((T, HD), lambda h, i: (0, V_RET_BLK + h))
    lg_spec = pl.BlockSpec((1, 1, 128), lambda h, i: (h, 0, 0))
    return q_spec, k_spec, v_spec, lg_spec


def _ret_fwd(qk, p, logg, duty=None):
    def body(q_ref, k_ref, v_ref, lg_ref, o_ref):
        i = pl.program_id(1)

        def visible(kl):
            s = _dot(q_ref[...], k_ref[0:kl, :], NT) * _ret_weight(lg_ref, i, kl)
            o_ref[...] = _dot(s, v_ref[0:kl, :], NN)

        _per_query_block(i, visible)

    q_spec, k_spec, v_spec, lg_spec = _ret_specs()
    return _pcall(
        body, name="ret_fwd", grid=(N_HEADS, T // TQ),
        out_shape=jax.ShapeDtypeStruct((T, BW), F32),
        in_specs=[q_spec, k_spec, v_spec, lg_spec], out_specs=q_spec,
        sem=("parallel", "parallel"), args=(qk, qk, p, logg), duty=duty)


def _ret_bwd(qk, p, logg, do, duty=None):
    def body(q_ref, k_ref, v_ref, lg_ref, do_ref, dq_ref, dk_ref, dv_ref):
        i = pl.program_id(1)
        q, dov = q_ref[...], do_ref[...]

        @pl.when(i == 0)
        def _():
            dk_ref[...] = jnp.zeros_like(dk_ref)
            dv_ref[...] = jnp.zeros_like(dv_ref)

        def visible(kl):
            w = _ret_weight(lg_ref, i, kl)
            k = k_ref[0:kl, :]
            s = _dot(q, k, NT) * w
            ds = _dot(dov, v_ref[0:kl, :], NT) * w
            dk_ref[0:kl, :] += _dot(ds, q, TN)
            dv_ref[0:kl, :] += _dot(s, dov, TN)
            dq_ref[...] = _dot(ds, k, NN)

        _per_query_block(i, visible)

    q_spec, k_spec, v_spec, lg_spec = _ret_specs()
    acc_spec = pl.BlockSpec((T, HD), lambda h, i: (0, h))
    sh = jax.ShapeDtypeStruct((T, BW), F32)
    return _pcall(
        body, name="ret_bwd", grid=(N_HEADS, T // TQ),
        out_shape=[sh, sh, sh],
        in_specs=[q_spec, k_spec, v_spec, lg_spec, q_spec], out_specs=[q_spec, acc_spec, acc_spec],
        sem=("parallel", "arbitrary"), args=(qk, qk, p, logg, do), duty=duty)


def _post_norm(xv, gv, centered):
    if centered:
        xv = xv - _mean1(xv)
    r = lax.rsqrt(_mean1(xv * xv) + EPS)
    return xv * r, r


def _branch_post_fwd(raw, p, g, gate_blk, centered, name, duty=None):
    def body(raw_ref, z_ref, g_ref, o_ref):
        for h in range(N_HEADS):
            sl = slice(h * HD, (h + 1) * HD)
            gv = g_ref[:, sl] if centered else g_ref[...]
            xh, _ = _post_norm(raw_ref[:, sl], gv, centered)
            z = z_ref[:, sl].astype(F32)
            o_ref[:, sl] = (z * _sigmoid(z) * (xh * gv)).astype(BF16)

    return _pcall(
        body, name=name, grid=(T // ROW_TILE,),
        out_shape=jax.ShapeDtypeStruct((T, BW), BF16),
        in_specs=[_row_spec(BW), _row_spec(BW, gate_blk), _vec_spec(BW if centered else HD)],
        out_specs=_row_spec(BW), sem=("parallel",), args=(raw, p, g), duty=duty)


def _branch_post_bwd(raw, p, g, dout, gate_blk, centered, name, duty=None):
    gw = BW if centered else HD

    def body(raw_ref, z_ref, g_ref, do_ref, dr_ref, dz_ref, dg_ref):
        @pl.when(pl.program_id(0) == 0)
        def _():
            dg_ref[...] = jnp.zeros_like(dg_ref)

        for h in range(N_HEADS):
            sl = slice(h * HD, (h + 1) * HD)
            gsl = sl if centered else slice(0, HD)
            gv, z, dov = g_ref[:, gsl], z_ref[:, sl].astype(F32), do_ref[:, sl]
            xh, r = _post_norm(raw_ref[:, sl], gv, centered)
            sg = _sigmoid(z)
            dyn = dov * (z * sg)
            dz_ref[:, sl] = dov * (xh * gv) * (sg * (1.0 + z * (1.0 - sg)))
            dxh = dyn * gv
            t = dxh - xh * _mean1(dxh * xh)
            if centered:
                t = t - _mean1(dxh)
            dr_ref[:, sl] = r * t
            dg_ref[0:1, gsl] += _sum0(dyn * xh)

    return _pcall(
        body, name=name, grid=(T // ROW_TILE,),
        out_shape=[jax.ShapeDtypeStruct((T, BW), F32), jax.ShapeDtypeStruct((T, BW), F32),
                   jax.ShapeDtypeStruct((8, gw), F32)],
        in_specs=[_row_spec(BW), _row_spec(BW, gate_blk), _vec_spec(gw), _row_spec(BW)],
        out_specs=[_row_spec(BW), _row_spec(BW), pl.BlockSpec((8, gw), lambda i: (0, 0))],
        sem=("arbitrary",), args=(raw, p, g, dout), duty=duty)


GLA_ROWS = 256
GLA_CPB = GLA_ROWS // CHUNK
GLA_DK = 64
GLA_W = N_HEADS * GLA_DK
GQ_BLK, GK_BLK, GV_BLK, GG_BLK = 8, 9, 5, 6
TAIL_BLK = TAIL0 // 128
RG_BLK = 3


def _gla_chunk_common(tl, w2, bv, kv):
    pre = _dot(tl, w2, NN) + bv
    la = _log_sigmoid(pre) * (1.0 / 16.0)
    bc = _exact_dot(_tri(CHUNK, False), la)
    be = bc[CHUNK - 1:CHUNK, :]
    w = jnp.exp(be - bc)
    return pre, w, jnp.exp(be), kv * w


def _head_masks():
    lane = lax.broadcasted_iota(jnp.int32, (1, GLA_W), 1)
    return [jnp.where((lane // GLA_DK) == h, 1.0, 0.0) for h in range(N_HEADS)]


def _gla_fwd(p, tail, w2pad, b, duty=None):
    nb = T // GLA_ROWS

    def body(q_ref, k_ref, v_ref, t_ref, w2_ref, b_ref, o_ref, st_ref, s_acc):
        @pl.when(pl.program_id(0) == 0)
        def _():
            s_acc[...] = jnp.zeros_like(s_acc)

        masks = _head_masks()
        for c in range(GLA_CPB):
            rows = slice(c * CHUNK, (c + 1) * CHUNK)
            _, _, a, kd = _gla_chunk_common(t_ref[rows, :], w2_ref[...], b_ref[...], k_ref[rows, :].astype(F32))
            q = q_ref[rows, :].astype(F32) * (GLA_DK ** -0.5)
            kv = None
            for h in range(N_HEADS):
                t = _dot(v_ref[rows, h * HD:(h + 1) * HD], kd * masks[h], TN)
                kv = t if kv is None else kv + t
            s_new = s_acc[...] * a + kv
            s_acc[...] = s_new
            st_ref[c] = s_new
            for h in range(N_HEADS):
                o_ref[rows, h * HD:(h + 1) * HD] = _dot(q * masks[h], s_new, NT)

    return _pcall(
        body, name="gla_fwd", grid=(nb,),
        out_shape=[jax.ShapeDtypeStruct((T, BW), F32), jax.ShapeDtypeStruct((T // CHUNK, HD, GLA_W), F32)],
        in_specs=[pl.BlockSpec((GLA_ROWS, GLA_W), lambda i: (i, GQ_BLK)),
                  pl.BlockSpec((GLA_ROWS, GLA_W), lambda i: (i, GK_BLK)),
                  pl.BlockSpec((GLA_ROWS, BW), lambda i: (i, GV_BLK)),
                  pl.BlockSpec((GLA_ROWS, 128), lambda i: (i, 0)),
                  pl.BlockSpec((128, GLA_W), lambda i: (0, 0)),
                  pl.BlockSpec((1, GLA_W), lambda i: (0, 0))],
        out_specs=[pl.BlockSpec((GLA_ROWS, BW), lambda i: (i, 0)),
                   pl.BlockSpec((GLA_CPB, HD, GLA_W), lambda i: (i, 0, 0))],
        scratch_shapes=[pltpu.VMEM((HD, GLA_W), F32)],
        sem=("arbitrary",), args=(p, p, p, tail, w2pad, b), duty=duty)


def _gla_bwd(p, tail, w2pad, b, states, do, duty=None):
    nb = T // GLA_ROWS

    def body(q_ref, k_ref, v_ref, t_ref, w2_ref, b_ref, st_ref, prev_ref, do_ref,
             dq_ref, dk_ref, dv_ref, dt_ref, dw2_ref, db_ref, ds_acc):
        step = pl.program_id(0)

        @pl.when(step == 0)
        def _():
            ds_acc[...] = jnp.zeros_like(ds_acc)
            dw2_ref[...] = jnp.zeros_like(dw2_ref)
            db_ref[...] = jnp.zeros_like(db_ref)

        masks = _head_masks()
        up = _tri(CHUNK, True)
        has_prev = jnp.where(step == nb - 1, 0.0, 1.0)
        for c in reversed(range(GLA_CPB)):
            rows = slice(c * CHUNK, (c + 1) * CHUNK)
            tl, w2, k = t_ref[rows, :], w2_ref[...], k_ref[rows, :].astype(F32)
            pre, w, a, kd = _gla_chunk_common(tl, w2, b_ref[...], k)
            q = q_ref[rows, :].astype(F32) * (GLA_DK ** -0.5)
            s_n = st_ref[c]
            s_prev = st_ref[c - 1] if c > 0 else prev_ref[0] * has_prev
            ds = ds_acc[...]
            dos = [do_ref[rows, h * HD:(h + 1) * HD] for h in range(N_HEADS)]
            for h in range(N_HEADS):
                ds = ds + _dot(dos[h], q * masks[h], TN)
            dqp = jnp.zeros((CHUNK, GLA_W), F32)
            dkd = jnp.zeros((CHUNK, GLA_W), F32)
            for h in range(N_HEADS):
                dqp = dqp + masks[h] * _dot(dos[h], s_n, NN)
                dkd = dkd + masks[h] * _dot(v_ref[rows, h * HD:(h + 1) * HD], ds, NN)
                dv_ref[rows, h * HD:(h + 1) * HD] = _dot(kd * masks[h], ds, NT)
            dq_ref[rows, :] = dqp * (GLA_DK ** -0.5)
            dk_ref[rows, :] = dkd * w
            e = dkd * k * w
            dbe = _sum0(e) + _sum0(ds * s_prev) * a
            dla = dbe - _exact_dot(up, e)
            dpre = dla * (1.0 / 16.0) * _sigmoid(-pre)
            db_ref[0:1, :] += _sum0(dpre)
            dw2_ref[...] += _dot(tl, dpre, TN)
            dt_ref[rows, :] = _dot(dpre, w2, NT)
            ds_acc[...] = ds * a

    rev = lambda i: nb - 1 - i
    sh = lambda w: jax.ShapeDtypeStruct((T, w), F32)
    return _pcall(
        body, name="gla_bwd", grid=(nb,),
        out_shape=[sh(GLA_W), sh(GLA_W), sh(BW), sh(128), jax.ShapeDtypeStruct((128, GLA_W), F32),
                   jax.ShapeDtypeStruct((8, GLA_W), F32)],
        in_specs=[pl.BlockSpec((GLA_ROWS, GLA_W), lambda i: (rev(i), GQ_BLK)),
                  pl.BlockSpec((GLA_ROWS, GLA_W), lambda i: (rev(i), GK_BLK)),
                  pl.BlockSpec((GLA_ROWS, BW), lambda i: (rev(i), GV_BLK)),
                  pl.BlockSpec((GLA_ROWS, 128), lambda i: (rev(i), 0)),
                  pl.BlockSpec((128, GLA_W), lambda i: (0, 0)),
                  pl.BlockSpec((1, GLA_W), lambda i: (0, 0)),
                  pl.BlockSpec((GLA_CPB, HD, GLA_W), lambda i: (rev(i), 0, 0)),
                  pl.BlockSpec((1, HD, GLA_W), lambda i: (jnp.maximum(rev(i) * GLA_CPB - 1, 0), 0, 0)),
                  pl.BlockSpec((GLA_ROWS, BW), lambda i: (rev(i), 0))],
        out_specs=[pl.BlockSpec((GLA_ROWS, GLA_W), lambda i: (rev(i), 0)),
                   pl.BlockSpec((GLA_ROWS, GLA_W), lambda i: (rev(i), 0)),
                   pl.BlockSpec((GLA_ROWS, BW), lambda i: (rev(i), 0)),
                   pl.BlockSpec((GLA_ROWS, 128), lambda i: (rev(i), 0)),
                   pl.BlockSpec((128, GLA_W), lambda i: (0, 0)),
                   pl.BlockSpec((8, GLA_W), lambda i: (0, 0))],
        scratch_shapes=[pltpu.VMEM((HD, GLA_W), F32)],
        sem=("arbitrary",), args=(p, p, p, tail, w2pad, b, states, states, do), duty=duty)


FQ_BLK, FK_BLK = 7, 8
V_FOX_BLK = 36


def _fox_prep_fwd(p, tail, qg, kg, btail, duty=None):
    def body(q_ref, k_ref, t_ref, qg_ref, kg_ref, bt_ref, o_ref, cum_ref, carry):
        @pl.when(pl.program_id(0) == 0)
        def _():
            carry[...] = jnp.zeros_like(carry)

        for src, gr, off in ((q_ref, qg_ref, 0), (k_ref, kg_ref, BW)):
            for h in range(N_HEADS):
                xv = src[:, h * HD:(h + 1) * HD].astype(F32)
                r = lax.rsqrt(_mean1(xv * xv) + EPS)
                o_ref[:, off + h * HD:off + (h + 1) * HD] = (xv * r * gr[...]).astype(BF16)
        logf = _log_sigmoid(t_ref[...] + bt_ref[...])
        cum = _exact_dot(_tri(ROW_TILE, False), logf) + carry[...]
        cum_ref[...] = cum
        carry[...] = cum[ROW_TILE - 1:ROW_TILE, :]

    return _pcall(
        body, name="fox_prep_fwd", grid=(T // ROW_TILE,),
        out_shape=[jax.ShapeDtypeStruct((T, 2 * BW), BF16), jax.ShapeDtypeStruct((T, 128), F32)],
        in_specs=[_row_spec(BW, FQ_BLK), _row_spec(BW, FK_BLK), _row_spec(128),
                  _vec_spec(HD), _vec_spec(HD), _vec_spec(128)],
        out_specs=[_row_spec(2 * BW), _row_spec(128)],
        scratch_shapes=[pltpu.VMEM((1, 128), F32)],
        sem=("arbitrary",), args=(p, p, tail, qg, kg, btail), duty=duty)


def _fox_prep_bwd(p, tail, qg, kg, btail, dqn, dkn, dcum, duty=None):
    nt = T // ROW_TILE

    def body(q_ref, k_ref, t_ref, qg_ref, kg_ref, bt_ref, dq_ref, dk_ref, dc_ref, o_ref, dt_ref, st_ref, carry):
        @pl.when(pl.program_id(0) == 0)
        def _():
            carry[...] = jnp.zeros_like(carry)
            st_ref[...] = jnp.zeros_like(st_ref)

        for row, (src, gr, dsrc, off) in enumerate(((q_ref, qg_ref, dq_ref, 0), (k_ref, kg_ref, dk_ref, BW))):
            for h in range(N_HEADS):
                xv = src[:, h * HD:(h + 1) * HD].astype(F32)
                dy = dsrc[:, h * HD:(h + 1) * HD]
                r = lax.rsqrt(_mean1(xv * xv) + EPS)
                n = xv * r
                dn = dy * gr[...]
                o_ref[:, off + h * HD:off + (h + 1) * HD] = r * (dn - n * _mean1(dn * n))
                st_ref[row:row + 1, :] += _sum0(dy * n)
        z = t_ref[...] + bt_ref[...]
        dlogf = _exact_dot(_tri(ROW_TILE, True), dc_ref[...]) + carry[...]
        carry[...] = dlogf[0:1, :]
        lane = lax.broadcasted_iota(jnp.int32, (1, 128), 1)
        keep = (lane >= FF_LANE0) & (lane < FF_LANE0 + N_HEADS)
        dz = jnp.where(keep, dlogf * _sigmoid(-z), 0.0)
        dt_ref[...] = dz
        st_ref[2:3, :] += _sum0(dz)

    rs = lambda w, col=0: pl.BlockSpec((ROW_TILE, w), lambda i: (nt - 1 - i, col))
    return _pcall(
        body, name="fox_prep_bwd", grid=(nt,),
        out_shape=[jax.ShapeDtypeStruct((T, 2 * BW), F32), jax.ShapeDtypeStruct((T, 128), F32),
                   jax.ShapeDtypeStruct((8, 128), F32)],
        in_specs=[rs(BW, FQ_BLK), rs(BW, FK_BLK), rs(128), _vec_spec(HD), _vec_spec(HD), _vec_spec(128),
                  rs(BW), rs(BW), rs(128)],
        out_specs=[rs(2 * BW), rs(128), pl.BlockSpec((8, 128), lambda i: (0, 0))],
        scratch_shapes=[pltpu.VMEM((1, 128), F32)],
        sem=("arbitrary",), args=(p, p, tail, qg, kg, btail, dqn, dkn, dcum), duty=duty)


def _fox_logits(q_ref, k_ref, cc_ref, cr_ref, i, kl):
    rows, cols = _block_iotas(i, kl)
    s = _dot(q_ref[...], k_ref[0:kl, :], NT) * (HD ** -0.5) + cc_ref[0] - cr_ref[0, :, 0:kl]
    return jnp.where(cols <= rows, s, -1e30)


def _fox_specs():
    q_spec = pl.BlockSpec((TQ, HD), lambda h, i: (i, h))
    k_spec = pl.BlockSpec((T, HD), lambda h, i: (0, N_HEADS + h))
    v_spec = pl.BlockSpec((T, HD), lambda h, i: (0, V_FOX_BLK + h))
    col_spec = pl.BlockSpec((1, TQ, 1), lambda h, i: (h, i, 0))
    row_spec = pl.BlockSpec((1, 1, T), lambda h, i: (h, 0, 0))
    return q_spec, k_spec, v_spec, col_spec, row_spec


def _fox_fwd(qkn, p, cumcol, cumrow, duty=None):
    def body(q_ref, k_ref, v_ref, cc_ref, cr_ref, o_ref, lse_ref):
        i = pl.program_id(1)

        def visible(kl):
            s = _fox_logits(q_ref, k_ref, cc_ref, cr_ref, i, kl)
            m = jnp.max(s, axis=-1, keepdims=True)
            e = jnp.exp(s - m)
            l = jnp.sum(e, axis=-1, keepdims=True)
            o_ref[...] = _dot(e / l, v_ref[0:kl, :], NN)
            lse_ref[0] = m + jnp.log(l)

        _per_query_block(i, visible)

    q_spec, k_spec, v_spec, col_spec, row_spec = _fox_specs()
    return _pcall(
        body, name="fox_fwd", grid=(N_HEADS, T // TQ),
        out_shape=[jax.ShapeDtypeStruct((T, BW), F32), jax.ShapeDtypeStruct((N_HEADS, T, 1), F32)],
        in_specs=[q_spec, k_spec, v_spec, col_spec, row_spec], out_specs=[q_spec, col_spec],
        sem=("parallel", "parallel"), args=(qkn, qkn, p, cumcol, cumrow), duty=duty)


def _fox_bwd(qkn, p, cumcol, cumrow, lse, o, do, duty=None):
    def body(q_ref, k_ref, v_ref, cc_ref, cr_ref, lse_ref, o_ref, do_ref, dq_ref, dk_ref, dv_ref, dr_ref, dc_ref):
        i = pl.program_id(1)
        @pl.when(i == 0)
        def _():
            dk_ref[...] = jnp.zeros_like(dk_ref)
            dv_ref[...] = jnp.zeros_like(dv_ref)
            dc_ref[...] = jnp.zeros_like(dc_ref)

        def visible(kl):
            q, dov = q_ref[...], do_ref[...]
            pm = jnp.exp(_fox_logits(q_ref, k_ref, cc_ref, cr_ref, i, kl) - lse_ref[0])
            delta = jnp.sum(o_ref[...] * dov, axis=-1, keepdims=True)
            ds = pm * (_dot(dov, v_ref[0:kl, :], NT) - delta)
            dq_ref[...] = _dot(ds, k_ref[0:kl, :], NN) * (HD ** -0.5)
            dr_ref[0] = jnp.sum(ds, axis=-1, keepdims=True)
            dk_ref[0:kl, :] += _dot(ds, q, TN) * (HD ** -0.5)
            dv_ref[0:kl, :] += _dot(pm, dov, TN)
            dc_ref[0, :, 0:kl] += _sum0(ds)

        _per_query_block(i, visible)

    q_spec, k_spec, v_spec, col_spec, row_spec = _fox_specs()
    acc_spec = pl.BlockSpec((T, HD), lambda h, i: (0, h))
    sh = jax.ShapeDtypeStruct((T, BW), F32)
    return _pcall(
        body, name="fox_bwd", grid=(N_HEADS, T // TQ),
        out_shape=[sh, sh, sh, jax.ShapeDtypeStruct((N_HEADS, T, 1), F32), jax.ShapeDtypeStruct((N_HEADS, 1, T), F32)],
        in_specs=[q_spec, k_spec, v_spec, col_spec, row_spec, col_spec, q_spec, q_spec],
        out_specs=[q_spec, acc_spec, acc_spec, col_spec, row_spec],
        sem=("parallel", "arbitrary"), args=(qkn, qkn, p, cumcol, cumrow, lse, o, do), duty=duty)


def _mix_fwd(gpre, b_mg, y0, y1, y2, duty=None):
    def body(g_ref, b_ref, y0_ref, y1_ref, y2_ref, o_ref):
        acc = None
        for n, y_ref in enumerate((y0_ref, y1_ref, y2_ref)):
            sl = slice(n * D, (n + 1) * D)
            t = _sigmoid(g_ref[:, sl].astype(F32) + b_ref[:, sl]) * y_ref[...].astype(F32)
            acc = t if acc is None else acc + t
        o_ref[...] = acc.astype(BF16)

    return _pcall(
        body, name="mix_fwd", grid=(T // ROW_TILE,), out_shape=jax.ShapeDtypeStruct((T, D), BF16),
        in_specs=[_row_spec(3 * D), _vec_spec(3 * D), _row_spec(), _row_spec(), _row_spec()],
        out_specs=_row_spec(), sem=("parallel",), args=(gpre, b_mg, y0, y1, y2), duty=duty)


def _mix_bwd(gpre, b_mg, y0, y1, y2, dmi, duty=None):
    def body(g_ref, b_ref, y0_ref, y1_ref, y2_ref, d_ref, dy0_ref, dy1_ref, dy2_ref, dg_ref, db_ref):
        @pl.when(pl.program_id(0) == 0)
        def _():
            db_ref[...] = jnp.zeros_like(db_ref)

        dv = d_ref[...]
        for n, (y_ref, dy_ref) in enumerate(((y0_ref, dy0_ref), (y1_ref, dy1_ref), (y2_ref, dy2_ref))):
            sl = slice(n * D, (n + 1) * D)
            sg = _sigmoid(g_ref[:, sl].astype(F32) + b_ref[:, sl])
            dy_ref[...] = (dv * sg).astype(BF16)
            dpre = dv * y_ref[...].astype(F32) * (sg * (1.0 - sg))
            dg_ref[:, sl] = dpre.astype(BF16)
            db_ref[0:1, sl] += _sum0(dpre)

    shb = jax.ShapeDtypeStruct((T, D), BF16)
    return _pcall(
        body, name="mix_bwd", grid=(T // ROW_TILE,),
        out_shape=[shb, shb, shb, jax.ShapeDtypeStruct((T, 3 * D), BF16), jax.ShapeDtypeStruct((8, 3 * D), F32)],
        in_specs=[_row_spec(3 * D), _vec_spec(3 * D), _row_spec(), _row_spec(), _row_spec(), _row_spec()],
        out_specs=[_row_spec(), _row_spec(), _row_spec(), _row_spec(3 * D), pl.BlockSpec((8, 3 * D), lambda i: (0, 0))],
        sem=("arbitrary",), args=(gpre, b_mg, y0, y1, y2, dmi), duty=duty)


FF_COLS = 256
FF_NBLK = D_FF // FF_COLS


def _shift_rows(a, n):
    rows = lax.broadcasted_iota(jnp.int32, a.shape, 0)
    rolled = pltpu.roll(a, n % T, 0)
    return jnp.where((rows >= n) if n > 0 else (rows < T + n), rolled, 0.0)


def _ffn_act_fwd(uu, w_conv, b_conv, duty=None):
    def body(u_ref, g_ref, w_ref, b_ref, o_ref):
        u = u_ref[...].astype(F32)
        w = w_ref[...]
        uc = b_ref[...] + w[0:1, :] * _shift_rows(u, 2) + w[1:2, :] * _shift_rows(u, 1) + w[2:3, :] * u
        o_ref[...] = (uc * _sigmoid(uc) * g_ref[...].astype(F32)).astype(BF16)

    return _pcall(
        body, name="ffn_act_fwd", grid=(FF_NBLK,), out_shape=jax.ShapeDtypeStruct((T, D_FF), BF16),
        in_specs=[pl.BlockSpec((T, FF_COLS), lambda j: (0, j)), pl.BlockSpec((T, FF_COLS), lambda j: (0, FF_NBLK + j)),
                  pl.BlockSpec((3, FF_COLS), lambda j: (0, j)), pl.BlockSpec((1, FF_COLS), lambda j: (0, j))],
        out_specs=pl.BlockSpec((T, FF_COLS), lambda j: (0, j)),
        sem=("parallel",), args=(uu, uu, w_conv, b_conv), duty=duty)


def _ffn_act_bwd(uu, w_conv, b_conv, da, duty=None):
    def body(u_ref, g_ref, w_ref, b_ref, da_ref, d_ref, st_ref):
        u, w, dav = u_ref[...].astype(F32), w_ref[...], da_ref[...]
        u1, u2 = _shift_rows(u, 1), _shift_rows(u, 2)
        uc = b_ref[...] + w[0:1, :] * u2 + w[1:2, :] * u1 + w[2:3, :] * u
        sg = _sigmoid(uc)
        d_ref[1] = (dav * (uc * sg)).astype(BF16)
        duc = dav * g_ref[...].astype(F32) * (sg * (1.0 + uc * (1.0 - sg)))
        du = w[2:3, :] * duc + w[1:2, :] * _shift_rows(duc, -1) + w[0:1, :] * _shift_rows(duc, -2)
        d_ref[0] = du.astype(BF16)
        st_ref[...] = jnp.zeros_like(st_ref)
        st_ref[0:1, :] = _sum0(duc * u2)
        st_ref[1:2, :] = _sum0(duc * u1)
        st_ref[2:3, :] = _sum0(duc * u)
        st_ref[3:4, :] = _sum0(duc)

    cb = lambda rows=T, off=0: pl.BlockSpec((rows, FF_COLS), lambda j: (0, off + j))
    return _pcall(
        body, name="ffn_act_bwd", grid=(FF_NBLK,),
        out_shape=[jax.ShapeDtypeStruct((2, T, D_FF), BF16), jax.ShapeDtypeStruct((8, D_FF), F32)],
        in_specs=[cb(), cb(T, FF_NBLK), cb(3), cb(1), cb()],
        out_specs=[pl.BlockSpec((2, T, FF_COLS), lambda j: (0, 0, j)), cb(8)],
        sem=("parallel",), args=(uu, uu, w_conv, b_conv, da), duty=duty)


def _adamw(g, w, m, v, tr, name):
    partial = g.ndim == 3
    R, C = w.shape
    tr = R if tr is None else tr
    assert R % tr == 0

    def body(g_ref, w_ref, m_ref, v_ref, go_ref, d_ref, mo_ref, vo_ref):
        if partial:
            gv = g_ref[0].astype(F32)
            for j in range(1, g.shape[0]):
                gv = gv + g_ref[j].astype(F32)
        else:
            gv = g_ref[...]
        go_ref[...] = gv
        mn = ADAM_B1 * m_ref[...] + (1.0 - ADAM_B1) * gv
        vn = ADAM_B2 * v_ref[...] + (1.0 - ADAM_B2) * (gv * gv)
        mo_ref[...] = mn
        vo_ref[...] = vn
        m_hat = mn / (1.0 - ADAM_B1 ** ADAM_STEP)
        v_hat = vn / (1.0 - ADAM_B2 ** ADAM_STEP)
        d_ref[...] = -ADAM_LR * (m_hat / (jnp.sqrt(v_hat) + ADAM_EPS) + ADAM_WD * w_ref[...])

    spec = pl.BlockSpec((tr, C), lambda i: (i, 0))
    g_spec = pl.BlockSpec((g.shape[0], tr, C), lambda i: (0, i, 0)) if partial else spec
    sh = jax.ShapeDtypeStruct((R, C), F32)
    return pl.pallas_call(
        body, name=name, grid=(R // tr,), out_shape=[sh, sh, sh, sh],
        in_specs=[g_spec, spec, spec, spec], out_specs=[spec, spec, spec, spec],
        compiler_params=_cp("parallel"))(g, w, m, v)


def _chip_sum(dw, stage, name):
    n_chip, n, C = stage.shape

    def body(d_ref, s_ref, o_ref):
        mc = lax.axis_index("c")
        mine = jnp.where(mc == 0, d_ref[0, 0].astype(F32), d_ref[0, 1].astype(F32))
        o_ref[0] = (mine + s_ref[0].astype(F32)).astype(BF16)

    return pl.pallas_call(
        body, name=name, grid=(n_chip,), out_shape=jax.ShapeDtypeStruct(stage.shape, BF16),
        in_specs=[pl.BlockSpec((1, 2, n, C), lambda p: (p, 0, 0, 0)), pl.BlockSpec((1, n, C), lambda p: (p, 0, 0))],
        out_specs=pl.BlockSpec((1, n, C), lambda p: (p, 0, 0)), compiler_params=_cp("parallel"),
    )(dw.reshape(n_chip, 2, n, C), stage)


def _sum_partials(g, name, tr=None):
    n_part, R, C = g.shape
    tr = R if tr is None else tr
    assert R % tr == 0

    def body(g_ref, o_ref):
        acc = g_ref[0].astype(F32)
        for j in range(1, n_part):
            acc = acc + g_ref[j].astype(F32)
        o_ref[...] = acc

    return pl.pallas_call(
        body, name=name, grid=(R // tr,), out_shape=jax.ShapeDtypeStruct((R, C), F32),
        in_specs=[pl.BlockSpec((n_part, tr, C), lambda i: (0, i, 0))], out_specs=pl.BlockSpec((tr, C), lambda i: (i, 0)),
        compiler_params=_cp("parallel"))(g)


def _permute_in(w):
    pad = jnp.zeros(w.shape[:-1] + (NP - IN_W,), w.dtype)
    return jnp.concatenate([w[..., :3072], w[..., 3088:5136], w[..., 3072:3088], w[..., 5136:5140], pad], axis=-1)


def _unpermute_in(w):
    return jnp.concatenate([w[..., :3072], w[..., 5120:5136], w[..., 3072:5120], w[..., 5136:5140]], axis=-1)


def _flat_pack(arrs):
    flat = jnp.concatenate([a.reshape(-1).astype(F32) for a in arrs])
    n = flat.shape[0]
    rows = -(-n // 1024) * 8
    return jnp.pad(flat, (0, rows * 128 - n)).reshape(rows, 128)


def _flat_unpack(buf, shapes):
    flat = buf.reshape(-1)
    out, off = [], 0
    for s in shapes:
        n = int(np.prod(s))
        out.append(flat[off:off + n].reshape(s))
        off += n
    return out


GRAD_CHUNKS = dict(w_in=(128, 8), w_o=(128, 1), w_down=(352, 2), w_br0=(128, 1), w_br1=(128, 1), w_br2=(128, 1),
                   w_mg=(384, 4), w_up=(704, 11))
GRAD_VIA_CHIP = ("w_in", "w_down", "w_mg", "w_up")


def _send_grad(xfer, layer, k, g):
    if xfer is None:
        return g
    n, parts = GRAD_CHUNKS[k]
    row_bytes = g.shape[1] * 2
    if k not in GRAD_VIA_CHIP:
        for c in range(parts):
            xfer.add(k, ("a2a", 0, n, (layer,), c * (n // parts), n // parts), g, nbytes=n // parts * row_bytes)
        return g
    stage = ("stage", layer, k)
    xfer.lands[stage] = lax.empty((N_DEV // 2, n, g.shape[1]), BF16)
    xfer.add(stage, ("to_other_core", 0, n, (), 0, n), g, group=stage, nbytes=n * row_bytes)

    def both_halves_here():
        chip = _chip_sum(g, xfer.lands[stage], "chip_sum_" + k)
        for c in range(parts):
            xfer.add(k, ("a2a_chip", 0, n, (layer,), c * (n // parts), n // parts), chip, nbytes=n // parts * row_bytes)

    xfer.when_done(stage, both_halves_here)
    return g


def _weight(wl, k):
    return wl[k]() if callable(wl[k]) else wl[k]


MIN_CARRIER_US = 19.0


def _taker(xfer, fill=1.0):
    if xfer is None:
        return lambda us: None
    return lambda us: xfer.take_for(us * fill) if us >= MIN_CARRIER_US else None


def _layer_fwd(x0, wl, consts, xfer=None, fill=1.25):
    cosf, sinf, logg = consts
    row = lambda a: a.reshape(1, -1)
    take = _taker(xfer, fill)
    h = _norm_fwd(x0, row(wl["norm1_g"]), row(wl["scale1"]), row(wl["shift1"]), "norm1_fwd", duty=take(9))
    w_in = _weight(wl, "w_in")
    p = _matmul(h, w_in, "nn", "in_proj", out_dtype=BF16, n_blocks=(0, TAIL0 // 512), duty=take(30))
    tail = _matmul(h, w_in, "nn", "in_tail", tn=128, n_blocks=(TAIL_BLK, 1))
    qk = _rope_fwd(p, cosf, sinf, duty=take(10))
    ret_raw = _ret_fwd(qk, p, logg, duty=take(27))
    br0 = _branch_post_fwd(ret_raw, p, row(wl["ret_norm_g"]), RG_BLK, True, "ret_post_fwd", duty=take(9))
    gla_raw, states = _gla_fwd(p, tail, wl["w2pad"], row(wl["b_gla_a"]), duty=take(26))
    br1 = _branch_post_fwd(gla_raw, p, row(wl["gla_norm_g"]), GG_BLK, False, "gla_post_fwd", duty=take(9))
    qkn, cum = _fox_prep_fwd(p, tail, row(wl["q_norm_g"]), row(wl["k_norm_g"]), row(wl["btail"]), duty=take(10))
    cum4 = cum[:, FF_LANE0:FF_LANE0 + N_HEADS].T
    cumcol, cumrow = cum4.reshape(N_HEADS, T, 1), cum4.reshape(N_HEADS, 1, T)
    fox_o, lse = _fox_fwd(qkn, p, cumcol, cumrow, duty=take(30))
    w_br_t = _weight(wl, "w_br_t")
    ys = [_matmul(b, w_br_t[n], "nt", "br_proj%d" % n, out_dtype=BF16) for n, b in enumerate((br0, br1, fox_o))]
    gpre = _matmul(h, _weight(wl, "w_mg_t"), "nt", "gate_proj", out_dtype=BF16, duty=take(20))
    mixed_in = _mix_fwd(gpre, row(wl["b_mg"]), *ys, duty=take(21))
    mixed = _matmul(mixed_in, _weight(wl, "w_o"), "nn", "o_proj", duty=take(10))
    x1, h2 = _norm_fwd(x0, row(wl["norm2_g"]), row(wl["scale2"]), row(wl["shift2"]), "norm2_fwd",
                       m=mixed, gate=row(wl["gate1"]), duty=take(13))
    uu = _matmul(h2, _weight(wl, "w_up_t"), "nt", "up_proj", out_dtype=BF16, duty=take(30))
    act = _ffn_act_fwd(uu, wl["w_conv"], row(wl["b_conv"]), duty=take(25))
    y = _matmul(act, _weight(wl, "w_down"), "nn", "down_proj", tk=1408, duty=take(24))
    x2 = _axpy(x1, y, row(wl["gate2"]), "resid2", duty=take(11))
    saved = dict(x0=x0, h=h, p=p, tail=tail, qk=qk, ret_raw=ret_raw, br0=br0, gla_raw=gla_raw, states=states, br1=br1,
                 qkn=qkn, cumcol=cumcol, cumrow=cumrow, fox_o=fox_o, lse=lse, y0=ys[0], y1=ys[1], y2=ys[2],
                 gpre=gpre, mixed_in=mixed_in, mixed=mixed, x1=x1, h2=h2, uu=uu, act=act, y=y)
    return x2, saved


def _layer_bwd(dx2, wl, sv, consts, xfer=None, layer=0):
    cosf, sinf, logg = consts
    row = lambda a: a.reshape(1, -1)
    take = _taker(xfer)

    send = functools.partial(_send_grad, xfer, layer)

    dy, st_g2 = _gate_bwd(dx2, sv["y"], row(wl["gate2"]), "gate2_bwd", duty=take(10))
    dact = _matmul(dy, _weight(wl, "w_down"), "nt", "down_dx", tn=1408, duty=take(21))
    d_down = send("w_down", _matmul(sv["act"], dy, "tn", "down_dw", out_dtype=BF16, tm=1408, duty=take(19)))
    duu, st_conv = _ffn_act_bwd(sv["uu"], wl["w_conv"], row(wl["b_conv"]), dact, duty=take(40))
    dh2 = _matmul(duu, _weight(wl, "w_up_t"), "nn", "up_dx", tk=1408, duty=take(42))
    d_up_t = send("w_up", _matmul(duu, sv["h2"], "tn", "up_dw", out_dtype=BF16, tm=1408, duty=take(33)))
    dx1, st_n2 = _norm_bwd(sv["x1"], dh2, dx2, row(wl["norm2_g"]), row(wl["scale2"]), row(wl["shift2"]), "norm2_bwd",
                           duty=take(15))
    dmixed, st_g1 = _gate_bwd(dx1, sv["mixed"], row(wl["gate1"]), "gate1_bwd", duty=take(10))
    dmi = _matmul(dmixed, _weight(wl, "w_o"), "nt", "o_dx", duty=take(11))
    d_o = send("w_o", _matmul(sv["mixed_in"], dmixed, "tn", "o_dw", out_dtype=BF16, duty=take(9)))
    dy0, dy1, dy2, dgpre, st_bmg = _mix_bwd(sv["gpre"], row(wl["b_mg"]), sv["y0"], sv["y1"], sv["y2"], dmi,
                                             duty=take(31))
    brs = (sv["br0"], sv["br1"], sv["fox_o"])
    w_br_t = _weight(wl, "w_br_t")
    dbr = [_matmul(d, w_br_t[n], "nn", "br_dx%d" % n) for n, d in enumerate((dy0, dy1, dy2))]
    d_br_t = [send("w_br%d" % n, _matmul(d, brs[n], "tn", "br_dw%d" % n, out_dtype=BF16))
              for n, d in enumerate((dy0, dy1, dy2))]
    dh = _matmul(dgpre, _weight(wl, "w_mg_t"), "nn", "gate_dx", tk=1024, duty=take(29))
    d_mg_t = send("w_mg", _matmul(dgpre, sv["h"], "tn", "gate_dw", out_dtype=BF16, duty=take(21)))
    p, tail = sv["p"], sv["tail"]
    dqn, dkn, dfv, drow, dcol = _fox_bwd(sv["qkn"], p, sv["cumcol"], sv["cumrow"], sv["lse"], sv["fox_o"], dbr[2],
                                         duty=take(50))
    dcum4 = drow.reshape(N_HEADS, T) - dcol.reshape(N_HEADS, T)
    dcum = jnp.pad(dcum4.T, ((0, 0), (FF_LANE0, 128 - FF_LANE0 - N_HEADS)))
    dfqk, dtail_fox, st_fox = _fox_prep_bwd(p, tail, row(wl["q_norm_g"]), row(wl["k_norm_g"]), row(wl["btail"]), dqn, dkn, dcum,
                                            duty=take(15))
    dgla_raw, dgg, st_gn = _branch_post_bwd(sv["gla_raw"], p, row(wl["gla_norm_g"]), dbr[1], GG_BLK, False, "gla_post_bwd",
                                            duty=take(12))
    dgq, dgk, dgv, dtail_gla, dw2pad, st_bg = _gla_bwd(p, tail, wl["w2pad"], row(wl["b_gla_a"]), sv["states"], dgla_raw,
                                                       duty=take(30))
    dret_raw, drg, st_rn = _branch_post_bwd(sv["ret_raw"], p, row(wl["ret_norm_g"]), dbr[0], RG_BLK, True, "ret_post_bwd",
                                            duty=take(13))
    dqr, dkr, drv = _ret_bwd(sv["qk"], p, logg, dret_raw, duty=take(50))
    drqk = _rope_bwd(dqr, dkr, cosf, sinf, duty=take(11))
    dp = jnp.concatenate([a.astype(BF16) for a in (drqk, drv, drg, dgq, dgk, dgv, dgg, dfqk, dfv, dtail_fox + dtail_gla)]
                         + [jnp.zeros((T, NP - TAIL0 - 128), BF16)], axis=1)
    dh = _matmul(dp, _weight(wl, "w_in"), "nt", "in_dx", tk=1408, add=dh, duty=take(45))
    d_in = send("w_in", _matmul(sv["h"], dp, "tn", "in_dw", out_dtype=BF16, duty=take(32)))
    dx0, st_n1 = _norm_bwd(sv["x0"], dh, dx1, row(wl["norm1_g"]), row(wl["scale1"]), row(wl["shift1"]), "norm1_bwd",
                           duty=take(15))
    big = dict(w_in=d_in, w_o=d_o, w_down=d_down, w_br0=d_br_t[0], w_br1=d_br_t[1], w_br2=d_br_t[2], w_mg=d_mg_t,
               w_up=d_up_t)
    dmod = jnp.concatenate([st_n1[2], st_n1[1], st_g1[0], st_n2[2], st_n2[1], st_g2[0]])
    small = dict(norm1_g=st_n1[0], norm2_g=st_n2[0], b_gla_a=st_bg[0], b_fox_f=st_fox[2, FF_LANE0:FF_LANE0 + N_HEADS],
                 ret_norm_g=st_rn[0], gla_norm_g=st_gn[0], q_norm_g=st_fox[0], k_norm_g=st_fox[1], b_mg=st_bmg[0],
                 b_conv=st_conv[3], w_gla_a2=dw2pad[:LR_LANES], w_conv=st_conv[0:3])
    return dx0, big, dmod, small


SMALL_REPL = ("norm1_g", "norm2_g", "b_ada", "b_gla_a", "b_fox_f", "ret_norm_g", "gla_norm_g", "q_norm_g", "k_norm_g",
              "b_mg", "b_conv")
SMALL_SHARDED = ("w_gla_a2", "w_conv")
BIG = ("w_in", "w_o", "w_down", "w_br", "w_mg", "w_up")
WEIGHTS = ("norm1_g", "norm2_g", "w_ada", "b_ada", "w_in", "w_gla_a2", "b_gla_a", "b_fox_f", "ret_norm_g", "gla_norm_g",
           "q_norm_g", "k_norm_g", "w_br", "w_mg", "b_mg", "w_o", "w_up", "w_conv", "b_conv", "w_down")


def kernel(x, c, norm1_g, norm2_g, w_ada, b_ada, w_in, w_gla_a2, b_gla_a, b_fox_f, ret_norm_g, gla_norm_g, q_norm_g, k_norm_g, w_br, w_mg, b_mg, w_o, w_up, w_conv, b_conv, w_down, loss_target, m_norm1_g, m_norm2_g, m_w_ada, m_b_ada, m_w_in, m_w_gla_a2, m_b_gla_a, m_b_fox_f, m_ret_norm_g, m_gla_norm_g, m_q_norm_g, m_k_norm_g, m_w_br, m_w_mg, m_b_mg, m_w_o, m_w_up, m_w_conv, m_b_conv, m_w_down, v_norm1_g, v_norm2_g, v_w_ada, v_b_ada, v_w_in, v_w_gla_a2, v_b_gla_a, v_b_fox_f, v_ret_norm_g, v_gla_norm_g, v_q_norm_g, v_k_norm_g, v_w_br, v_w_mg, v_b_mg, v_w_o, v_w_up, v_w_conv, v_b_conv, v_w_down):
    W = dict(norm1_g=norm1_g, norm2_g=norm2_g, w_ada=w_ada, b_ada=b_ada, w_in=w_in, w_gla_a2=w_gla_a2, b_gla_a=b_gla_a,
             b_fox_f=b_fox_f, ret_norm_g=ret_norm_g, gla_norm_g=gla_norm_g, q_norm_g=q_norm_g, k_norm_g=k_norm_g,
             w_br=w_br, w_mg=w_mg, b_mg=b_mg, w_o=w_o, w_up=w_up, w_conv=w_conv, b_conv=b_conv, w_down=w_down)
    M = dict(norm1_g=m_norm1_g, norm2_g=m_norm2_g, w_ada=m_w_ada, b_ada=m_b_ada, w_in=m_w_in, w_gla_a2=m_w_gla_a2,
             b_gla_a=m_b_gla_a, b_fox_f=m_b_fox_f, ret_norm_g=m_ret_norm_g, gla_norm_g=m_gla_norm_g, q_norm_g=m_q_norm_g,
             k_norm_g=m_k_norm_g, w_br=m_w_br, w_mg=m_w_mg, b_mg=m_b_mg, w_o=m_w_o, w_up=m_w_up, w_conv=m_w_conv,
             b_conv=m_b_conv, w_down=m_w_down)
    V = dict(norm1_g=v_norm1_g, norm2_g=v_norm2_g, w_ada=v_w_ada, b_ada=v_b_ada, w_in=v_w_in, w_gla_a2=v_w_gla_a2,
             b_gla_a=v_b_gla_a, b_fox_f=v_b_fox_f, ret_norm_g=v_ret_norm_g, gla_norm_g=v_gla_norm_g, q_norm_g=v_q_norm_g,
             k_norm_g=v_k_norm_g, w_br=v_w_br, w_mg=v_w_mg, b_mg=v_b_mg, w_o=v_w_o, w_up=v_w_up, w_conv=v_w_conv,
             b_conv=v_b_conv, w_down=v_w_down)
    me = 4 * lax.axis_index("x") + 2 * lax.axis_index("y") + lax.axis_index("c")
    x2d, tgt = x.reshape(T, D), loss_target.reshape(T, D)

    sm = _flat_pack([c, w_gla_a2, w_conv])
    sm_all = _exchange(sm, True, "gather_small")
    parts = [_flat_unpack(sm_all[j], [(D,), (DEPTH, LR_LANES, 32), (DEPTH, 3, 352)]) for j in range(N_DEV)]
    c_all = jnp.stack([q[0] for q in parts])
    w_gla_full = jnp.concatenate([q[1] for q in parts], axis=2)
    w_conv_full = jnp.concatenate([q[2] for q in parts], axis=2)

    n_ada = w_ada.shape[2]
    b_loc = lax.dynamic_slice_in_dim(b_ada, me * n_ada, n_ada, axis=1).reshape(DEPTH, 1, n_ada)
    mod_all = _ada_fwd(c_all, w_ada, b_loc)
    mod_recv = _exchange(jnp.swapaxes(mod_all, 0, 1), False, "a2a_mod")
    mod = jnp.swapaxes(mod_recv, 0, 1).reshape(DEPTH, 6, D)

    loc = dict(w_in=_permute_in(w_in), w_o=w_o, w_down=w_down, w_br=jnp.swapaxes(w_br, 2, 3),
               w_mg=jnp.swapaxes(w_mg, 1, 2), w_up=jnp.swapaxes(w_up, 1, 2))
    loc = {k: v.astype(BF16) for k, v in loc.items()}
    w_full = dict(w_in=(D, NP), w_o=(D, D), w_down=(D_FF, D), w_br=(3, D, BW), w_mg=(3 * D, D), w_up=(2 * D_FF, D))
    w_parts = dict(w_in=8, w_br=2, w_mg=4, w_o=1, w_up=11, w_down=2)
    gather, units = _Transfers("gather"), []
    for l in range(DEPTH):
        for k, parts in w_parts.items():
            axis = 1 if k == "w_br" else 0
            n = w_full[k][axis] // N_DEV
            gather.lands[(l, k)] = lax.empty(w_full[k], BF16)
            shard = loc[k][l]
            nbytes = shard.size * 2 // parts
            units += [((l, k), (axis, n, (), c * (n // parts), n // parts), shard, nbytes) for c in range(parts)]
    first, lag = w_parts["w_in"], 4
    order = [("cross", i) for i in range(first)] + [("pass", i) for i in range(first)]
    for i in range(first, len(units) + lag):
        order += [("cross", i)] if i < len(units) else []
        order += [("pass", i - lag)] if i - lag >= first else []
    for what, i in order:
        key, where, shard, nbytes = units[i]
        if what == "cross":
            gather.add(key, ("gather_chip",) + where, shard, uid=i, nbytes=nbytes)
        else:
            gather.add(key, ("pass_on",) + where, after=i, nbytes=nbytes)

    w2pad = jnp.pad(w_gla_full, ((0, 0), (0, 128 - LR_LANES), (0, 0)))
    btail = jnp.pad(b_fox_f, ((0, 0), (FF_LANE0, 128 - FF_LANE0 - N_HEADS)))
    stacked = dict(norm1_g=norm1_g, norm2_g=norm2_g, b_gla_a=b_gla_a, ret_norm_g=ret_norm_g, gla_norm_g=gla_norm_g,
                   q_norm_g=q_norm_g, k_norm_g=k_norm_g, b_mg=b_mg, b_conv=b_conv, w_conv=w_conv_full, w2pad=w2pad,
                   btail=btail, shift1=mod[:, 0], scale1=mod[:, 1], gate1=mod[:, 2], shift2=mod[:, 3], scale2=mod[:, 4],
                   gate2=mod[:, 5])
    landed = lambda l, k: functools.partial(gather.get, (l, k))
    layers = [dict({k: v[l] for k, v in stacked.items()}, w_in=landed(l, "w_in"), w_o=landed(l, "w_o"),
                   w_down=landed(l, "w_down"), w_br_t=landed(l, "w_br"), w_mg_t=landed(l, "w_mg"), w_up_t=landed(l, "w_up"))
              for l in range(DEPTH)]
    consts = _rope_tables() + (_ret_logg(),)

    xc, saved = x2d, []
    for l in range(DEPTH):
        xc, sv = _layer_fwd(xc, layers[l], consts, gather, 1.7 if l == 0 else 1.45)
        saved.append(sv)
    loss_part, dxc = _loss_fwd_bwd(xc, tgt)
    loss = lax.psum(loss_part[0, 0], ("x", "y", "c"))

    grad_names = ("w_in", "w_o", "w_down", "w_br0", "w_br1", "w_br2", "w_mg", "w_up")
    blk_rows = dict(w_in=(128, NP), w_o=(128, D), w_down=(352, D), w_br0=(128, BW), w_br1=(128, BW), w_br2=(128, BW),
                    w_mg=(384, D), w_up=(704, D))
    grads = _Transfers("grads")
    for k in grad_names:
        grads.lands[k] = lax.empty((N_DEV // 2 if k in GRAD_VIA_CHIP else N_DEV, DEPTH) + blk_rows[k], BF16)
    dmod, small_g = [None] * DEPTH, [None] * DEPTH
    for l in reversed(range(DEPTH)):
        dxc, _, dmod[l], small_g[l] = _layer_bwd(dxc, layers[l], saved[l], consts, grads, l)
    grad_x = dxc
    dmod = jnp.stack(dmod)
    small_g = {k: jnp.stack([s[k] for s in small_g]) for k in small_g[0]}
    grads.drain()
    recv = {k: grads.get(k) for k in grad_names}

    dmod_send = jnp.swapaxes(dmod.reshape(DEPTH, N_DEV, n_ada), 0, 1)
    dmod_all = jnp.swapaxes(_exchange(dmod_send, False, "a2a_dmod"), 0, 1)
    g_ada = _ada_bwd(c_all, dmod_all)

    def flat(a, k):
        return a.reshape((-1, W[k].shape[-1]))

    def adam_nat(k, g, tr):
        outs = _adamw(g, flat(W[k], k), flat(M[k], k), flat(V[k], k), tr, "adamw_" + k)
        return [o.reshape(W[k].shape) for o in outs]

    def summed(k, tr):
        r = recv[k]
        return _sum_partials(r.reshape(r.shape[0], DEPTH * r.shape[2], r.shape[3]), "sum_" + k, tr).reshape((DEPTH,) + r.shape[2:])

    big_out = dict(
        w_in=adam_nat("w_in", flat(_unpermute_in(summed("w_in", 64)), "w_in"), 64),
        w_o=adam_nat("w_o", recv["w_o"].reshape(N_DEV, DEPTH * 128, D), 128),
        w_down=adam_nat("w_down", recv["w_down"].reshape(N_DEV // 2, DEPTH * 352, D), 352),
        w_br=adam_nat("w_br", flat(jnp.swapaxes(jnp.stack([summed("w_br%d" % n, 128) for n in range(3)], axis=1), 2, 3),
                                   "w_br"), 1024),
        w_mg=adam_nat("w_mg", flat(jnp.swapaxes(summed("w_mg", 384), 1, 2), "w_mg"), 512),
        w_up=adam_nat("w_up", flat(jnp.swapaxes(summed("w_up", 704), 1, 2), "w_up"), 512))
    ada_out = [o.reshape(DEPTH, D, n_ada) for o in _adamw(
        g_ada.reshape(DEPTH * D, n_ada), w_ada.reshape(DEPTH * D, n_ada), m_w_ada.reshape(DEPTH * D, n_ada),
        v_w_ada.reshape(DEPTH * D, n_ada), 512, "adamw_ada")]

    small_g = dict(small_g, b_ada=dmod)
    names = SMALL_REPL + SMALL_SHARDED
    full_shapes = [W[n].shape for n in SMALL_REPL] + [(DEPTH, LR_LANES, 256), (DEPTH, 3, D_FF)]
    part = _flat_pack([small_g[n] for n in names])
    total = _flat_unpack(_sum_partials(_exchange(part, True, "gather_small_grads"), "sum_small"), full_shapes)
    total = dict(zip(names, total))
    total["w_gla_a2"] = lax.dynamic_slice_in_dim(total["w_gla_a2"], me * 32, 32, axis=2)
    total["w_conv"] = lax.dynamic_slice_in_dim(total["w_conv"], me * 352, 352, axis=2)
    shapes = [W[n].shape for n in names]
    small_out = _adamw(_flat_pack([total[n] for n in names]), _flat_pack([W[n] for n in names]),
                       _flat_pack([M[n] for n in names]), _flat_pack([V[n] for n in names]), None, "adamw_small")
    small_out = [dict(zip(names, _flat_unpack(o, shapes))) for o in small_out]

    outs = []
    for k in range(4):
        d = dict(small_out[k])
        d.update({n: big_out[n][k] for n in BIG})
        d["w_ada"] = ada_out[k]
        outs.append([d[n] for n in WEIGHTS])
    return (loss, grad_x.reshape(1, T, D), *outs[0], *outs[1], *outs[2], *outs[3])
```

```python
import functools

import numpy as np
import jax
import jax.numpy as jnp
from jax import lax
from jax.experimental import pallas as pl
from jax.experimental.pallas import tpu as pltpu

F32 = jnp.float32
BF16 = jnp.bfloat16

N_DEV = 8
T = 2048
D = 1024
DEPTH = 4
N_HEADS = 4
HD = 128
BW = 512
D_FF = 2816
CHUNK = 64
EPS = 1e-6
IN_W = 5140
NP = 5632
TAIL0 = 5120
LR_LANES = 16
FF_LANE0 = 16
PACK_W = 1024
SEG_ROWS = (704, 128, 352, 192, 384, 704)
LAYER_ROWS = sum(SEG_ROWS)
VMEM_LIMIT_V7X = 56 * 1024 * 1024

ADAM_LR, ADAM_B1, ADAM_B2, ADAM_EPS, ADAM_WD, ADAM_STEP = 0.001, 0.9, 0.999, 1e-08, 0.01, 10

MESH_ID = pl.DeviceIdType.MESH


def _cp(*sem):
    return pltpu.CompilerParams(dimension_semantics=sem if sem else None, vmem_limit_bytes=VMEM_LIMIT_V7X)


def _sigmoid(z):
    return 1.0 / (1.0 + jnp.exp(-z))


def _log_sigmoid(z):
    return jnp.minimum(z, 0.0) - jnp.log(1.0 + jnp.exp(-jnp.abs(z)))


def _sum0(a):
    return jnp.sum(a, axis=0, keepdims=True)


def _mean1(a):
    return jnp.mean(a, axis=-1, keepdims=True)


def _dot(a, b, dims):
    return lax.dot_general(a.astype(BF16), b.astype(BF16), (dims, ((), ())), preferred_element_type=F32)


NN = ((1,), (0,))
NT = ((1,), (1,))
TN = ((0,), (0,))


def _exact_dot(m01, a):
    a1 = a.astype(BF16)
    r1 = a - a1.astype(F32)
    a2 = r1.astype(BF16)
    a3 = (r1 - a2.astype(F32)).astype(BF16)
    d = lambda z: jnp.dot(m01, z, preferred_element_type=F32)
    return d(a1) + d(a2) + d(a3)


def _tri(n, upper):
    r = lax.broadcasted_iota(jnp.int32, (n, n), 0)
    c = lax.broadcasted_iota(jnp.int32, (n, n), 1)
    return jnp.where((c >= r) if upper else (c <= r), 1.0, 0.0).astype(BF16)


def _exchange(x, gather, name):
    blk = x.shape if gather else x.shape[1:]

    def body(x_ref, o_ref, send_sems, recv_sems, loc_sem):
        mx, my, mc = lax.axis_index("x"), lax.axis_index("y"), lax.axis_index("c")
        me = 4 * mx + 2 * my + mc
        loc = pltpu.make_async_copy(x_ref if gather else x_ref.at[me], o_ref.at[me], loc_sem)
        loc.start()
        copies = []
        for k in range(1, N_DEV):
            px = mx ^ (k >> 2) if (k >> 2) else mx
            py = my ^ ((k >> 1) & 1) if ((k >> 1) & 1) else my
            pc = mc ^ (k & 1) if (k & 1) else mc
            peer = 4 * px + 2 * py + pc
            cp = pltpu.make_async_remote_copy(
                src_ref=x_ref if gather else x_ref.at[peer], dst_ref=o_ref.at[me],
                send_sem=send_sems.at[k - 1], recv_sem=recv_sems.at[k - 1],
                device_id=(px, py, pc), device_id_type=MESH_ID)
            cp.start()
            copies.append(cp)
        for cp in copies:
            cp.wait()
        loc.wait()

    return pl.pallas_call(
        body, name=name,
        out_shape=jax.ShapeDtypeStruct((N_DEV,) + tuple(blk), x.dtype),
        in_specs=[pl.BlockSpec(memory_space=pl.ANY)],
        out_specs=pl.BlockSpec(memory_space=pl.ANY),
        scratch_shapes=[pltpu.SemaphoreType.DMA((N_DEV - 1,)), pltpu.SemaphoreType.DMA((N_DEV - 1,)),
                        pltpu.SemaphoreType.DMA],
        compiler_params=pltpu.CompilerParams(has_side_effects=True),
    )(x)


def _blk(ref, axis, j, n, r0=0, nr=None):
    return ref.at[(slice(None),) * axis + (pl.ds(j * n + r0, n if nr is None else nr),)]


def _comm_copies(items, srcs, lands, send_sems, recv_sems, loc_sems):
    mx, my, mc = lax.axis_index("x"), lax.axis_index("y"), lax.axis_index("c")
    me = 4 * mx + 2 * my + mc
    local, remote = [], []
    for t, (kind, axis, n, sel, r0, nr, si, li) in enumerate(items):
        if kind == "pass_on":
            for q in (2, 4, 6):
                px = 1 - mx if q & 4 else mx
                py = 1 - my if q & 2 else my
                rows = _blk(lands[li], axis, 4 * px + 2 * py + mc, n, r0, nr)
                remote.append(pltpu.make_async_remote_copy(
                    src_ref=rows, dst_ref=rows, send_sem=send_sems.at[t * (N_DEV - 1) + q - 1],
                    recv_sem=recv_sems.at[t * (N_DEV - 1) + q - 1], device_id=(mx, my, 1 - mc), device_id_type=MESH_ID))
            continue
        if kind == "to_other_core":
            for p in range(N_DEV // 2):
                remote.append(pltpu.make_async_remote_copy(
                    src_ref=_blk(srcs[si], axis, 2 * p + 1 - mc, n, r0, nr), dst_ref=lands[li].at[p, pl.ds(r0, nr)],
                    send_sem=send_sems.at[t * (N_DEV - 1) + p], recv_sem=recv_sems.at[t * (N_DEV - 1) + p],
                    device_id=(mx, my, 1 - mc), device_id_type=MESH_ID))
            continue
        if kind == "a2a_chip":
            pm = 2 * mx + my
            mine = lands[li].at[(pm,) + tuple(sel) + (pl.ds(r0, nr),)]
            local.append(pltpu.make_async_copy(srcs[si].at[pm, pl.ds(r0, nr)], mine, loc_sems.at[t]))
            for q in (2, 4, 6):
                px = 1 - mx if q & 4 else mx
                py = 1 - my if q & 2 else my
                remote.append(pltpu.make_async_remote_copy(
                    src_ref=srcs[si].at[2 * px + py, pl.ds(r0, nr)], dst_ref=mine,
                    send_sem=send_sems.at[t * (N_DEV - 1) + q - 1], recv_sem=recv_sems.at[t * (N_DEV - 1) + q - 1],
                    device_id=(px, py, mc), device_id_type=MESH_ID))
            continue
        if kind == "a2a":
            mine = lands[li].at[(me,) + tuple(sel) + (pl.ds(r0, nr),)]
            own = _blk(srcs[si], axis, me, n, r0, nr)
        else:
            mine = _blk(lands[li], axis, me, n, r0, nr)
            own = _blk(srcs[si], axis, 0, n, r0, nr)
        local.append(pltpu.make_async_copy(own, mine, loc_sems.at[t]))
        for k in ((1, 2, 4, 6) if kind == "gather_chip" else range(1, N_DEV)):
            px = 1 - mx if k & 4 else mx
            py = 1 - my if k & 2 else my
            pc = 1 - mc if k & 1 else mc
            src = _blk(srcs[si], axis, 4 * px + 2 * py + pc, n, r0, nr) if kind == "a2a" else own
            remote.append(pltpu.make_async_remote_copy(
                src_ref=src, dst_ref=mine, send_sem=send_sems.at[t * (N_DEV - 1) + k - 1],
                recv_sem=recv_sems.at[t * (N_DEV - 1) + k - 1], device_id=(px, py, pc), device_id_type=MESH_ID))
    return local, remote


def _comm_scratch(n_items):
    return [pltpu.SemaphoreType.DMA((n_items * (N_DEV - 1),)), pltpu.SemaphoreType.DMA((n_items * (N_DEV - 1),)),
            pltpu.SemaphoreType.DMA((n_items,))]


LINK_BYTES_PER_US = dict(gather_chip=23e3, a2a_chip=23e3, a2a=11.5e3, gather=11.5e3, pass_on=200e3, to_other_core=150e3)
CALL_EXCHANGE_US = 3.0


class _Duty:
    def __init__(self, items, srcs, lands, done):
        self.items, self.srcs, self.lands, self.done = items, srcs, lands, done


def _pcall(body, name, grid, in_specs, out_specs, out_shape, args, scratch_shapes=(), sem=(), duty=None):
    if duty is None:
        return pl.pallas_call(body, name=name, grid=grid, in_specs=list(in_specs), out_specs=out_specs,
                              out_shape=out_shape, scratch_shapes=list(scratch_shapes), compiler_params=_cp(*sem))(*args)
    single = not isinstance(out_shape, (list, tuple))
    o_shape = [out_shape] if single else list(out_shape)
    o_specs = [out_specs] if single else list(out_specs)
    n_in, n_out, n_scr = len(in_specs), len(o_shape), len(scratch_shapes)
    n_src, n_land, n_items = len(duty.srcs), len(duty.lands), len(duty.items)
    a0 = n_in + n_src + n_land

    def wrapped(*refs):
        srcs = refs[n_in:n_in + n_src]
        lands = refs[a0 + n_out:a0 + n_out + n_land]
        core = refs[:n_in] + refs[a0:a0 + n_out] + refs[a0 + n_out + n_land:a0 + n_out + n_land + n_scr]
        sems = refs[a0 + n_out + n_land + n_scr:]
        first = functools.reduce(jnp.logical_and, [pl.program_id(a) == 0 for a in range(len(grid))])
        last = functools.reduce(jnp.logical_and, [pl.program_id(a) == g - 1 for a, g in enumerate(grid)])

        @pl.when(first)
        def _():
            local, remote = _comm_copies(duty.items, srcs, lands, *sems)
            for cp in local + remote:
                cp.start()

        body(*core)

        @pl.when(last)
        def _():
            local, remote = _comm_copies(duty.items, srcs, lands, *sems)
            for cp in remote + local:
                cp.wait()

    hbm = pl.BlockSpec(memory_space=pl.ANY)
    res = pl.pallas_call(
        wrapped, name=name, grid=grid,
        in_specs=list(in_specs) + [hbm] * (n_src + n_land), out_specs=o_specs + [hbm] * n_land,
        out_shape=o_shape + [jax.ShapeDtypeStruct(a.shape, a.dtype) for a in duty.lands],
        input_output_aliases={n_in + n_src + t: n_out + t for t in range(n_land)},
        scratch_shapes=list(scratch_shapes) + _comm_scratch(n_items),
        compiler_params=pltpu.CompilerParams(dimension_semantics=("arbitrary",) * len(grid),
                                             vmem_limit_bytes=VMEM_LIMIT_V7X, has_side_effects=True),
    )(*args, *duty.srcs, *duty.lands)
    duty.done(res[n_out:])
    return res[0] if single else res[:n_out]


def _comm(duty, name):
    n_src, n_land = len(duty.srcs), len(duty.lands)

    def body(*refs):
        local, remote = _comm_copies(duty.items, refs[:n_src], refs[n_src + n_land:n_src + 2 * n_land],
                                     *refs[n_src + 2 * n_land:])
        for cp in local + remote:
            cp.start()
        for cp in remote + local:
            cp.wait()

    hbm = pl.BlockSpec(memory_space=pl.ANY)
    duty.done(pl.pallas_call(
        body, name=name, out_shape=[jax.ShapeDtypeStruct(a.shape, a.dtype) for a in duty.lands],
        in_specs=[hbm] * (n_src + n_land), out_specs=[hbm] * n_land,
        input_output_aliases={n_src + t: t for t in range(n_land)},
        scratch_shapes=_comm_scratch(len(duty.items)), compiler_params=pltpu.CompilerParams(has_side_effects=True),
    )(*duty.srcs, *duty.lands))


class _Transfers:
    def __init__(self, name):
        self.name, self.queue, self.lands, self.flushes, self.groups = name, [], {}, 0, {}

    def add(self, key, item, src=None, uid=None, after=None, group=None, nbytes=0):
        self.queue.append((key, item, src, uid, after, group, nbytes / LINK_BYTES_PER_US[item[0]]))
        if group is not None:
            self.groups[group] = [self.groups.get(group, [0, None])[0] + 1, None]

    def when_done(self, group, fn):
        self.groups[group][1] = fn

    def take_for(self, us):
        count, busy = 0, CALL_EXCHANGE_US
        while count < len(self.queue) and busy + self.queue[count][6] <= us:
            busy += self.queue[count][6]
            count += 1
        return self.take(count) if count else None

    def take(self, count):
        units = []
        while self.queue and len(units) < count:
            after = self.queue[0][4]
            if after is not None and any(u[3] == after for u in units):
                break
            units.append(self.queue.pop(0))
        if not units:
            return None
        keys, srcs, items = [], [], []
        for key, item, src, _, _, _, _ in units:
            if key not in keys:
                keys.append(key)
            if src is not None and not any(src is s for s in srcs):
                srcs.append(src)
            si = [i for i, s in enumerate(srcs) if s is src][0] if src is not None else -1
            items.append(tuple(item) + (si, keys.index(key)))

        def done(new_lands):
            for key, arr in zip(keys, new_lands):
                self.lands[key] = arr
            for u in units:
                if u[5] is not None:
                    self.groups[u[5]][0] -= 1
                    if self.groups[u[5]][0] == 0:
                        self.groups[u[5]][1]()

        return _Duty(items, srcs, [self.lands[k] for k in keys], done)

    def drain(self, upto=None):
        count = upto
        while self.queue if upto is None else count > 0:
            duty = self.take(len(self.queue) if upto is None else count)
            count = None if upto is None else count - len(duty.items)
            self.flushes += 1
            _comm(duty, "%s_flush%d" % (self.name, self.flushes))

    def get(self, key):
        pending = [i for i, u in enumerate(self.queue) if u[0] == key]
        if pending:
            self.drain(pending[-1] + 1)
        return self.lands[key]


def _matmul(a, b, mode, name, out_dtype=F32, tm=1024, tn=512, tk=None, add=None, n_blocks=None, duty=None):
    halves = a.ndim == 3
    if mode == "tn":
        K, M = a.shape[-2], a.shape[-1] * (2 if halves else 1)
        N = b.shape[1]
    else:
        M, K = a.shape[-2], a.shape[-1] * (2 if halves else 1)
        N = b.shape[0] if mode == "nt" else b.shape[1]
    tm, tn = min(tm, M), min(tn, N)
    j0 = 0
    if n_blocks is not None:
        j0, N = n_blocks[0], n_blocks[1] * tn
    tk = K if tk is None else tk
    nk = K // tk
    assert M % tm == 0 and N % tn == 0 and K % tk == 0, (name, M, N, K, tm, tn, tk)
    dims = {"nn": NN, "nt": NT, "tn": TN}[mode]
    has_add = add is not None

    def body(*refs):
        a_ref, b_ref = refs[:2]
        add_ref = refs[2] if has_add else None
        o_ref = refs[3 if has_add else 2]
        part = _dot(a_ref[...], b_ref[...], dims)

        def finish(total):
            if has_add:
                total = total + add_ref[...]
            o_ref[...] = total.astype(o_ref.dtype)

        if nk == 1:
            finish(part)
            return
        acc_ref = refs[-1]
        k = pl.program_id(2)

        @pl.when(k == 0)
        def _():
            acc_ref[...] = part

        @pl.when((k > 0) & (k < nk - 1))
        def _():
            acc_ref[...] += part

        @pl.when(k == nk - 1)
        def _():
            finish(acc_ref[...] + part)

    if halves and mode == "tn":
        per = a.shape[-1] // tm
        a_spec = pl.BlockSpec((None, tk, tm), lambda i, j, k: (i // per, k, i % per))
    elif halves:
        per = a.shape[-1] // tk
        a_spec = pl.BlockSpec((None, tm, tk), lambda i, j, k: (k // per, i, k % per))
    elif mode == "tn":
        a_spec = pl.BlockSpec((tk, tm), lambda i, j, k: (k, i))
    else:
        a_spec = pl.BlockSpec((tm, tk), lambda i, j, k: (i, k))
    if mode == "nt":
        b_spec = pl.BlockSpec((tn, tk), lambda i, j, k: (j0 + j, k))
    else:
        b_spec = pl.BlockSpec((tk, tn), lambda i, j, k: (k, j0 + j))
    o_spec = pl.BlockSpec((tm, tn), lambda i, j, k: (i, j))
    in_specs = [a_spec, b_spec] + ([o_spec] if has_add else [])
    args = (a, b) + ((add,) if has_add else ())
    return _pcall(
        body, name=name, grid=(M // tm, N // tn, nk),
        out_shape=jax.ShapeDtypeStruct((M, N), out_dtype),
        in_specs=in_specs, out_specs=o_spec,
        scratch_shapes=[pltpu.VMEM((tm, tn), F32)] if nk > 1 else [],
        sem=("parallel", "parallel", "arbitrary"), args=args, duty=duty)


def _ada_fwd(c_all, w_ada, b_loc):
    n = w_ada.shape[2]

    def body(c_ref, w_ref, b_ref, o_ref):
        c = c_ref[...]
        o_ref[0] = _dot(c * _sigmoid(c), w_ref[0], NN) + b_ref[0]

    return pl.pallas_call(
        body, name="ada_fwd", grid=(DEPTH,),
        out_shape=jax.ShapeDtypeStruct((DEPTH, N_DEV, n), F32),
        in_specs=[pl.BlockSpec((N_DEV, D), lambda l: (0, 0)),
                  pl.BlockSpec((1, D, n), lambda l: (l, 0, 0)),
                  pl.BlockSpec((1, 1, n), lambda l: (l, 0, 0))],
        out_specs=pl.BlockSpec((1, N_DEV, n), lambda l: (l, 0, 0)),
        compiler_params=_cp("parallel"),
    )(c_all, w_ada, b_loc)


def _ada_bwd(c_all, dmod_all):
    n = dmod_all.shape[2]

    def body(c_ref, d_ref, o_ref):
        c = c_ref[...]
        o_ref[0] = _dot(c * _sigmoid(c), d_ref[0], TN)

    return pl.pallas_call(
        body, name="ada_bwd", grid=(DEPTH,),
        out_shape=jax.ShapeDtypeStruct((DEPTH, D, n), F32),
        in_specs=[pl.BlockSpec((N_DEV, D), lambda l: (0, 0)),
                  pl.BlockSpec((1, N_DEV, n), lambda l: (l, 0, 0))],
        out_specs=pl.BlockSpec((1, D, n), lambda l: (l, 0, 0)),
        compiler_params=_cp("parallel"),
    )(c_all, dmod_all)


ROW_TILE = 256


def _row_spec(w=D, col=0):
    return pl.BlockSpec((ROW_TILE, w), lambda i: (i, col))


def _vec_spec(w=D):
    return pl.BlockSpec((1, w), lambda i: (0, 0))


def _norm_fwd(x, g, scale, shift, name, m=None, gate=None, duty=None):
    has_res = m is not None

    def body(*refs):
        if has_res:
            x_ref, m_ref, gate_ref, g_ref, sc_ref, sh_ref, xo_ref, h_ref = refs
            xv = x_ref[...] + gate_ref[...] * m_ref[...]
            xo_ref[...] = xv
        else:
            x_ref, g_ref, sc_ref, sh_ref, h_ref = refs
            xv = x_ref[...]
        r = lax.rsqrt(_mean1(xv * xv) + EPS)
        h_ref[...] = ((xv * r * g_ref[...]) * (1.0 + sc_ref[...]) + sh_ref[...]).astype(BF16)

    ins = [x] + ([m, gate] if has_res else []) + [g, scale, shift]
    in_specs = [_row_spec()] + ([_row_spec(), _vec_spec()] if has_res else []) + [_vec_spec()] * 3
    out_shape = [jax.ShapeDtypeStruct((T, D), BF16)]
    out_specs = [_row_spec()]
    if has_res:
        out_shape = [jax.ShapeDtypeStruct((T, D), F32)] + out_shape
        out_specs = [_row_spec()] + out_specs
    out = _pcall(body, name=name, grid=(T // ROW_TILE,), out_shape=out_shape, in_specs=in_specs,
                 out_specs=out_specs, sem=("parallel",), args=ins, duty=duty)
    return out if has_res else out[0]


def _norm_bwd(x, dh, dres, g, scale, shift, name, duty=None):
    def body(x_ref, dh_ref, dres_ref, g_ref, sc_ref, sh_ref, dx_ref, st_ref):
        xv, dh_v, gv = x_ref[...], dh_ref[...], g_ref[...]
        r = lax.rsqrt(_mean1(xv * xv) + EPS)
        n = xv * r
        dy = dh_v * (1.0 + sc_ref[...])
        dn = dy * gv
        dx_ref[...] = r * (dn - n * _mean1(dn * n)) + dres_ref[...]

        @pl.when(pl.program_id(0) == 0)
        def _():
            st_ref[...] = jnp.zeros_like(st_ref)

        st_ref[0:1, :] += _sum0(dy * n)
        st_ref[1:2, :] += _sum0(dh_v * (n * gv))
        st_ref[2:3, :] += _sum0(dh_v)

    return _pcall(
        body, name=name, grid=(T // ROW_TILE,),
        out_shape=[jax.ShapeDtypeStruct((T, D), F32), jax.ShapeDtypeStruct((8, D), F32)],
        in_specs=[_row_spec(), _row_spec(), _row_spec(), _vec_spec(), _vec_spec(), _vec_spec()],
        out_specs=[_row_spec(), pl.BlockSpec((8, D), lambda i: (0, 0))],
        sem=("arbitrary",), args=(x, dh, dres, g, scale, shift), duty=duty)


def _axpy(x, m, gate, name, duty=None):
    def body(x_ref, m_ref, gate_ref, o_ref):
        o_ref[...] = x_ref[...] + gate_ref[...] * m_ref[...]

    return _pcall(
        body, name=name, grid=(T // ROW_TILE,), out_shape=jax.ShapeDtypeStruct((T, D), F32),
        in_specs=[_row_spec(), _row_spec(), _vec_spec()], out_specs=_row_spec(),
        sem=("parallel",), args=(x, m, gate), duty=duty)


def _gate_bwd(dx, m, gate, name, duty=None):
    def body(dx_ref, m_ref, gate_ref, dm_ref, st_ref):
        dxv = dx_ref[...]
        dm_ref[...] = (gate_ref[...] * dxv).astype(BF16)

        @pl.when(pl.program_id(0) == 0)
        def _():
            st_ref[...] = jnp.zeros_like(st_ref)

        st_ref[0:1, :] += _sum0(dxv * m_ref[...])

    return _pcall(
        body, name=name, grid=(T // ROW_TILE,),
        out_shape=[jax.ShapeDtypeStruct((T, D), BF16), jax.ShapeDtypeStruct((8, D), F32)],
        in_specs=[_row_spec(), _row_spec(), _vec_spec()],
        out_specs=[_row_spec(), pl.BlockSpec((8, D), lambda i: (0, 0))],
        sem=("arbitrary",), args=(dx, m, gate), duty=duty)


def _loss_fwd_bwd(y, target):
    def body(y_ref, t_ref, l_ref, d_ref):
        e = y_ref[...] - t_ref[...]
        d_ref[...] = e * (1.0 / D)

        @pl.when(pl.program_id(0) == 0)
        def _():
            l_ref[...] = jnp.zeros_like(l_ref)

        l_ref[...] += jnp.sum(_sum0(e * e), axis=1, keepdims=True) * (0.5 / D)

    return pl.pallas_call(
        body, name="loss", grid=(T // ROW_TILE,),
        out_shape=[jax.ShapeDtypeStruct((8, 128), F32), jax.ShapeDtypeStruct((T, D), F32)],
        in_specs=[_row_spec(), _row_spec()],
        out_specs=[pl.BlockSpec((8, 128), lambda i: (0, 0)), _row_spec()],
        compiler_params=_cp("arbitrary"))(y, target)


def _rope_tables():
    half = HD // 2
    inv_freq = 10000.0 ** (-jnp.arange(half, dtype=F32) / half)
    ang = jnp.arange(T, dtype=F32)[:, None] * inv_freq[None, :]
    cos, sin = jnp.cos(ang), jnp.sin(ang)
    return jnp.concatenate([cos, cos], axis=1), jnp.concatenate([-sin, sin], axis=1)


def _rope_fwd(p, cosf, sinf, duty=None):
    def body(p_ref, c_ref, s_ref, o_ref):
        cv, sv = c_ref[...], s_ref[...]
        for j in range(2 * N_HEADS):
            xv = p_ref[:, j * HD:(j + 1) * HD].astype(F32)
            rot = xv * cv + pltpu.roll(xv, HD // 2, 1) * sv
            if j >= N_HEADS:
                rot = rot * (HD ** -0.5)
            o_ref[:, j * HD:(j + 1) * HD] = rot.astype(BF16)

    return _pcall(
        body, name="rope_fwd", grid=(T // ROW_TILE,),
        out_shape=jax.ShapeDtypeStruct((T, 2 * BW), BF16),
        in_specs=[_row_spec(2 * BW), _row_spec(HD), _row_spec(HD)], out_specs=_row_spec(2 * BW),
        sem=("parallel",), args=(p, cosf, sinf), duty=duty)


def _rope_bwd(dq, dk, cosf, sinf, duty=None):
    def body(dq_ref, dk_ref, c_ref, s_ref, o_ref):
        cv, sv = c_ref[...], s_ref[...]
        for j in range(2 * N_HEADS):
            h = j % N_HEADS
            d = dq_ref[:, h * HD:(h + 1) * HD] if j < N_HEADS else dk_ref[:, h * HD:(h + 1) * HD] * (HD ** -0.5)
            o_ref[:, j * HD:(j + 1) * HD] = d * cv + pltpu.roll(d * sv, HD // 2, 1)

    return _pcall(
        body, name="rope_bwd", grid=(T // ROW_TILE,),
        out_shape=jax.ShapeDtypeStruct((T, 2 * BW), F32),
        in_specs=[_row_spec(BW), _row_spec(BW), _row_spec(HD), _row_spec(HD)], out_specs=_row_spec(2 * BW),
        sem=("parallel",), args=(dq, dk, cosf, sinf), duty=duty)


TQ = 256
V_RET_BLK = 8


def _ret_logg():
    lg = jnp.log1p(-jnp.exp2(-5.0 - jnp.arange(N_HEADS, dtype=F32)))
    return jnp.broadcast_to(lg[:, None, None], (N_HEADS, 1, 128))


def _block_iotas(i, kl):
    rows = lax.broadcasted_iota(jnp.int32, (TQ, kl), 0) + i * TQ
    cols = lax.broadcasted_iota(jnp.int32, (TQ, kl), 1)
    return rows, cols


def _ret_weight(lg_ref, i, kl):
    rows, cols = _block_iotas(i, kl)
    dist = jnp.abs(rows - cols).astype(F32)
    w = jnp.exp(dist * lg_ref[0][:, 0:1])
    return jnp.where((cols >> 6) <= (rows >> 6), w, 0.0)


def _per_query_block(i, fn):
    for n in range(1, T // TQ + 1):
        pl.when(i == n - 1)(functools.partial(fn, n * TQ))


def _ret_specs():
    q_spec = pl.BlockSpec((TQ, HD), lambda h, i: (i, h))
    k_spec = pl.BlockSpec((T, HD), lambda h, i: (0, N_HEADS + h))
    v_spec = pl.BlockSpec((T, HD), lambda h, i: (0, V_RET_BLK + h))
    lg_spec = pl.BlockSpec((1, 1, 128), lambda h, i: (h, 0, 0))
    return q_spec, k_spec, v_spec, lg_spec


def _ret_fwd(qk, p, logg, duty=None):
    def body(q_ref, k_ref, v_ref, lg_ref, o_ref):
        i = pl.program_id(1)

        def visible(kl):
            s = _dot(q_ref[...], k_ref[0:kl, :], NT) * _ret_weight(lg_ref, i, kl)
            o_ref[...] = _dot(s, v_ref[0:kl, :], NN)

        _per_query_block(i, visible)

    q_spec, k_spec, v_spec, lg_spec = _ret_specs()
    return _pcall(
        body, name="ret_fwd", grid=(N_HEADS, T // TQ),
        out_shape=jax.ShapeDtypeStruct((T, BW), F32),
        in_specs=[q_spec, k_spec, v_spec, lg_spec], out_specs=q_spec,
        sem=("parallel", "parallel"), args=(qk, qk, p, logg), duty=duty)


def _ret_bwd(qk, p, logg, do, duty=None):
    def body(q_ref, k_ref, v_ref, lg_ref, do_ref, dq_ref, dk_ref, dv_ref):
        i = pl.program_id(1)
        q, dov = q_ref[...], do_ref[...]

        @pl.when(i == 0)
        def _():
            dk_ref[...] = jnp.zeros_like(dk_ref)
            dv_ref[...] = jnp.zeros_like(dv_ref)

        def visible(kl):
            w = _ret_weight(lg_ref, i, kl)
            k = k_ref[0:kl, :]
            s = _dot(q, k, NT) * w
            ds = _dot(dov, v_ref[0:kl, :], NT) * w
            dk_ref[0:kl, :] += _dot(ds, q, TN)
            dv_ref[0:kl, :] += _dot(s, dov, TN)
            dq_ref[...] = _dot(ds, k, NN)

        _per_query_block(i, visible)

    q_spec, k_spec, v_spec, lg_spec = _ret_specs()
    acc_spec = pl.BlockSpec((T, HD), lambda h, i: (0, h))
    sh = jax.ShapeDtypeStruct((T, BW), F32)
    return _pcall(
        body, name="ret_bwd", grid=(N_HEADS, T // TQ),
        out_shape=[sh, sh, sh],
        in_specs=[q_spec, k_spec, v_spec, lg_spec, q_spec], out_specs=[q_spec, acc_spec, acc_spec],
        sem=("parallel", "arbitrary"), args=(qk, qk, p, logg, do), duty=duty)


def _post_norm(xv, gv, centered):
    if centered:
        xv = xv - _mean1(xv)
    r = lax.rsqrt(_mean1(xv * xv) + EPS)
    return xv * r, r


def _branch_post_fwd(raw, p, g, gate_blk, centered, name, duty=None):
    def body(raw_ref, z_ref, g_ref, o_ref):
        for h in range(N_HEADS):
            sl = slice(h * HD, (h + 1) * HD)
            gv = g_ref[:, sl] if centered else g_ref[...]
            xh, _ = _post_norm(raw_ref[:, sl], gv, centered)
            z = z_ref[:, sl].astype(F32)
            o_ref[:, sl] = (z * _sigmoid(z) * (xh * gv)).astype(BF16)

    return _pcall(
        body, name=name, grid=(T // ROW_TILE,),
        out_shape=jax.ShapeDtypeStruct((T, BW), BF16),
        in_specs=[_row_spec(BW), _row_spec(BW, gate_blk), _vec_spec(BW if centered else HD)],
        out_specs=_row_spec(BW), sem=("parallel",), args=(raw, p, g), duty=duty)


def _branch_post_bwd(raw, p, g, dout, gate_blk, centered, name, duty=None):
    gw = BW if centered else HD

    def body(raw_ref, z_ref, g_ref, do_ref, dr_ref, dz_ref, dg_ref):
        @pl.when(pl.program_id(0) == 0)
        def _():
            dg_ref[...] = jnp.zeros_like(dg_ref)

        for h in range(N_HEADS):
            sl = slice(h * HD, (h + 1) * HD)
            gsl = sl if centered else slice(0, HD)
            gv, z, dov = g_ref[:, gsl], z_ref[:, sl].astype(F32), do_ref[:, sl]
            xh, r = _post_norm(raw_ref[:, sl], gv, centered)
            sg = _sigmoid(z)
            dyn = dov * (z * sg)
            dz_ref[:, sl] = dov * (xh * gv) * (sg * (1.0 + z * (1.0 - sg)))
            dxh = dyn * gv
            t = dxh - xh * _mean1(dxh * xh)
            if centered:
                t = t - _mean1(dxh)
            dr_ref[:, sl] = r * t
            dg_ref[0:1, gsl] += _sum0(dyn * xh)

    return _pcall(
        body, name=name, grid=(T // ROW_TILE,),
        out_shape=[jax.ShapeDtypeStruct((T, BW), F32), jax.ShapeDtypeStruct((T, BW), F32),
                   jax.ShapeDtypeStruct((8, gw), F32)],
        in_specs=[_row_spec(BW), _row_spec(BW, gate_blk), _vec_spec(gw), _row_spec(BW)],
        out_specs=[_row_spec(BW), _row_spec(BW), pl.BlockSpec((8, gw), lambda i: (0, 0))],
        sem=("arbitrary",), args=(raw, p, g, dout), duty=duty)


GLA_ROWS = 256
GLA_CPB = GLA_ROWS // CHUNK
GLA_DK = 64
GLA_W = N_HEADS * GLA_DK
GQ_BLK, GK_BLK, GV_BLK, GG_BLK = 8, 9, 5, 6
TAIL_BLK = TAIL0 // 128
RG_BLK = 3


def _gla_chunk_common(tl, w2, bv, kv):
    pre = _dot(tl, w2, NN) + bv
    la = _log_sigmoid(pre) * (1.0 / 16.0)
    bc = _exact_dot(_tri(CHUNK, False), la)
    be = bc[CHUNK - 1:CHUNK, :]
    w = jnp.exp(be - bc)
    return pre, w, jnp.exp(be), kv * w


def _head_masks():
    lane = lax.broadcasted_iota(jnp.int32, (1, GLA_W), 1)
    return [jnp.where((lane // GLA_DK) == h, 1.0, 0.0) for h in range(N_HEADS)]


def _gla_fwd(p, tail, w2pad, b, duty=None):
    nb = T // GLA_ROWS

    def body(q_ref, k_ref, v_ref, t_ref, w2_ref, b_ref, o_ref, st_ref, s_acc):
        @pl.when(pl.program_id(0) == 0)
        def _():
            s_acc[...] = jnp.zeros_like(s_acc)

        masks = _head_masks()
        for c in range(GLA_CPB):
            rows = slice(c * CHUNK, (c + 1) * CHUNK)
            _, _, a, kd = _gla_chunk_common(t_ref[rows, :], w2_ref[...], b_ref[...], k_ref[rows, :].astype(F32))
            q = q_ref[rows, :].astype(F32) * (GLA_DK ** -0.5)
            kv = None
            for h in range(N_HEADS):
                t = _dot(v_ref[rows, h * HD:(h + 1) * HD], kd * masks[h], TN)
                kv = t if kv is None else kv + t
            s_new = s_acc[...] * a + kv
            s_acc[...] = s_new
            st_ref[c] = s_new
            for h in range(N_HEADS):
                o_ref[rows, h * HD:(h + 1) * HD] = _dot(q * masks[h], s_new, NT)

    return _pcall(
        body, name="gla_fwd", grid=(nb,),
        out_shape=[jax.ShapeDtypeStruct((T, BW), F32), jax.ShapeDtypeStruct((T // CHUNK, HD, GLA_W), F32)],
        in_specs=[pl.BlockSpec((GLA_ROWS, GLA_W), lambda i: (i, GQ_BLK)),
                  pl.BlockSpec((GLA_ROWS, GLA_W), lambda i: (i, GK_BLK)),
                  pl.BlockSpec((GLA_ROWS, BW), lambda i: (i, GV_BLK)),
                  pl.BlockSpec((GLA_ROWS, 128), lambda i: (i, 0)),
                  pl.BlockSpec((128, GLA_W), lambda i: (0, 0)),
                  pl.BlockSpec((1, GLA_W), lambda i: (0, 0))],
        out_specs=[pl.BlockSpec((GLA_ROWS, BW), lambda i: (i, 0)),
                   pl.BlockSpec((GLA_CPB, HD, GLA_W), lambda i: (i, 0, 0))],
        scratch_shapes=[pltpu.VMEM((HD, GLA_W), F32)],
        sem=("arbitrary",), args=(p, p, p, tail, w2pad, b), duty=duty)


def _gla_bwd(p, tail, w2pad, b, states, do, duty=None):
    nb = T // GLA_ROWS

    def body(q_ref, k_ref, v_ref, t_ref, w2_ref, b_ref, st_ref, prev_ref, do_ref,
             dq_ref, dk_ref, dv_ref, dt_ref, dw2_ref, db_ref, ds_acc):
        step = pl.program_id(0)

        @pl.when(step == 0)
        def _():
            ds_acc[...] = jnp.zeros_like(ds_acc)
            dw2_ref[...] = jnp.zeros_like(dw2_ref)
            db_ref[...] = jnp.zeros_like(db_ref)

        masks = _head_masks()
        up = _tri(CHUNK, True)
        has_prev = jnp.where(step == nb - 1, 0.0, 1.0)
        for c in reversed(range(GLA_CPB)):
            rows = slice(c * CHUNK, (c + 1) * CHUNK)
            tl, w2, k = t_ref[rows, :], w2_ref[...], k_ref[rows, :].astype(F32)
            pre, w, a, kd = _gla_chunk_common(tl, w2, b_ref[...], k)
            q = q_ref[rows, :].astype(F32) * (GLA_DK ** -0.5)
            s_n = st_ref[c]
            s_prev = st_ref[c - 1] if c > 0 else prev_ref[0] * has_prev
            ds = ds_acc[...]
            dos = [do_ref[rows, h * HD:(h + 1) * HD] for h in range(N_HEADS)]
            for h in range(N_HEADS):
                ds = ds + _dot(dos[h], q * masks[h], TN)
            dqp = jnp.zeros((CHUNK, GLA_W), F32)
            dkd = jnp.zeros((CHUNK, GLA_W), F32)
            for h in range(N_HEADS):
                dqp = dqp + masks[h] * _dot(dos[h], s_n, NN)
                dkd = dkd + masks[h] * _dot(v_ref[rows, h * HD:(h + 1) * HD], ds, NN)
                dv_ref[rows, h * HD:(h + 1) * HD] = _dot(kd * masks[h], ds, NT)
            dq_ref[rows, :] = dqp * (GLA_DK ** -0.5)
            dk_ref[rows, :] = dkd * w
            e = dkd * k * w
            dbe = _sum0(e) + _sum0(ds * s_prev) * a
            dla = dbe - _exact_dot(up, e)
            dpre = dla * (1.0 / 16.0) * _sigmoid(-pre)
            db_ref[0:1, :] += _sum0(dpre)
            dw2_ref[...] += _dot(tl, dpre, TN)
            dt_ref[rows, :] = _dot(dpre, w2, NT)
            ds_acc[...] = ds * a

    rev = lambda i: nb - 1 - i
    sh = lambda w: jax.ShapeDtypeStruct((T, w), F32)
    return _pcall(
        body, name="gla_bwd", grid=(nb,),
        out_shape=[sh(GLA_W), sh(GLA_W), sh(BW), sh(128), jax.ShapeDtypeStruct((128, GLA_W), F32),
                   jax.ShapeDtypeStruct((8, GLA_W), F32)],
        in_specs=[pl.BlockSpec((GLA_ROWS, GLA_W), lambda i: (rev(i), GQ_BLK)),
                  pl.BlockSpec((GLA_ROWS, GLA_W), lambda i: (rev(i), GK_BLK)),
                  pl.BlockSpec((GLA_ROWS, BW), lambda i: (rev(i), GV_BLK)),
                  pl.BlockSpec((GLA_ROWS, 128), lambda i: (rev(i), 0)),
                  pl.BlockSpec((128, GLA_W), lambda i: (0, 0)),
                  pl.BlockSpec((1, GLA_W), lambda i: (0, 0)),
                  pl.BlockSpec((GLA_CPB, HD, GLA_W), lambda i: (rev(i), 0, 0)),
                  pl.BlockSpec((1, HD, GLA_W), lambda i: (jnp.maximum(rev(i) * GLA_CPB - 1, 0), 0, 0)),
                  pl.BlockSpec((GLA_ROWS, BW), lambda i: (rev(i), 0))],
        out_specs=[pl.BlockSpec((GLA_ROWS, GLA_W), lambda i: (rev(i), 0)),
                   pl.BlockSpec((GLA_ROWS, GLA_W), lambda i: (rev(i), 0)),
                   pl.BlockSpec((GLA_ROWS, BW), lambda i: (rev(i), 0)),
                   pl.BlockSpec((GLA_ROWS, 128), lambda i: (rev(i), 0)),
                   pl.BlockSpec((128, GLA_W), lambda i: (0, 0)),
                   pl.BlockSpec((8, GLA_W), lambda i: (0, 0))],
        scratch_shapes=[pltpu.VMEM((HD, GLA_W), F32)],
        sem=("arbitrary",), args=(p, p, p, tail, w2pad, b, states, states, do), duty=duty)


FQ_BLK, FK_BLK = 7, 8
V_FOX_BLK = 36


def _fox_prep_fwd(p, tail, qg, kg, btail, duty=None):
    def body(q_ref, k_ref, t_ref, qg_ref, kg_ref, bt_ref, o_ref, cum_ref, carry):
        @pl.when(pl.program_id(0) == 0)
        def _():
            carry[...] = jnp.zeros_like(carry)

        for src, gr, off in ((q_ref, qg_ref, 0), (k_ref, kg_ref, BW)):
            for h in range(N_HEADS):
                xv = src[:, h * HD:(h + 1) * HD].astype(F32)
                r = lax.rsqrt(_mean1(xv * xv) + EPS)
                o_ref[:, off + h * HD:off + (h + 1) * HD] = (xv * r * gr[...]).astype(BF16)
        logf = _log_sigmoid(t_ref[...] + bt_ref[...])
        cum = _exact_dot(_tri(ROW_TILE, False), logf) + carry[...]
        cum_ref[...] = cum
        carry[...] = cum[ROW_TILE - 1:ROW_TILE, :]

    return _pcall(
        body, name="fox_prep_fwd", grid=(T // ROW_TILE,),
        out_shape=[jax.ShapeDtypeStruct((T, 2 * BW), BF16), jax.ShapeDtypeStruct((T, 128), F32)],
        in_specs=[_row_spec(BW, FQ_BLK), _row_spec(BW, FK_BLK), _row_spec(128),
                  _vec_spec(HD), _vec_spec(HD), _vec_spec(128)],
        out_specs=[_row_spec(2 * BW), _row_spec(128)],
        scratch_shapes=[pltpu.VMEM((1, 128), F32)],
        sem=("arbitrary",), args=(p, p, tail, qg, kg, btail), duty=duty)


def _fox_prep_bwd(p, tail, qg, kg, btail, dqn, dkn, dcum, duty=None):
    nt = T // ROW_TILE

    def body(q_ref, k_ref, t_ref, qg_ref, kg_ref, bt_ref, dq_ref, dk_ref, dc_ref, o_ref, dt_ref, st_ref, carry):
        @pl.when(pl.program_id(0) == 0)
        def _():
            carry[...] = jnp.zeros_like(carry)
            st_ref[...] = jnp.zeros_like(st_ref)

        for row, (src, gr, dsrc, off) in enumerate(((q_ref, qg_ref, dq_ref, 0), (k_ref, kg_ref, dk_ref, BW))):
            for h in range(N_HEADS):
                xv = src[:, h * HD:(h + 1) * HD].astype(F32)
                dy = dsrc[:, h * HD:(h + 1) * HD]
                r = lax.rsqrt(_mean1(xv * xv) + EPS)
                n = xv * r
                dn = dy * gr[...]
                o_ref[:, off + h * HD:off + (h + 1) * HD] = r * (dn - n * _mean1(dn * n))
                st_ref[row:row + 1, :] += _sum0(dy * n)
        z = t_ref[...] + bt_ref[...]
        dlogf = _exact_dot(_tri(ROW_TILE, True), dc_ref[...]) + carry[...]
        carry[...] = dlogf[0:1, :]
        lane = lax.broadcasted_iota(jnp.int32, (1, 128), 1)
        keep = (lane >= FF_LANE0) & (lane < FF_LANE0 + N_HEADS)
        dz = jnp.where(keep, dlogf * _sigmoid(-z), 0.0)
        dt_ref[...] = dz
        st_ref[2:3, :] += _sum0(dz)

    rs = lambda w, col=0: pl.BlockSpec((ROW_TILE, w), lambda i: (nt - 1 - i, col))
    return _pcall(
        body, name="fox_prep_bwd", grid=(nt,),
        out_shape=[jax.ShapeDtypeStruct((T, 2 * BW), F32), jax.ShapeDtypeStruct((T, 128), F32),
                   jax.ShapeDtypeStruct((8, 128), F32)],
        in_specs=[rs(BW, FQ_BLK), rs(BW, FK_BLK), rs(128), _vec_spec(HD), _vec_spec(HD), _vec_spec(128),
                  rs(BW), rs(BW), rs(128)],
        out_specs=[rs(2 * BW), rs(128), pl.BlockSpec((8, 128), lambda i: (0, 0))],
        scratch_shapes=[pltpu.VMEM((1, 128), F32)],
        sem=("arbitrary",), args=(p, p, tail, qg, kg, btail, dqn, dkn, dcum), duty=duty)


def _fox_logits(q_ref, k_ref, cc_ref, cr_ref, i, kl):
    rows, cols = _block_iotas(i, kl)
    s = _dot(q_ref[...], k_ref[0:kl, :], NT) * (HD ** -0.5) + cc_ref[0] - cr_ref[0, :, 0:kl]
    return jnp.where(cols <= rows, s, -1e30)


def _fox_specs():
    q_spec = pl.BlockSpec((TQ, HD), lambda h, i: (i, h))
    k_spec = pl.BlockSpec((T, HD), lambda h, i: (0, N_HEADS + h))
    v_spec = pl.BlockSpec((T, HD), lambda h, i: (0, V_FOX_BLK + h))
    col_spec = pl.BlockSpec((1, TQ, 1), lambda h, i: (h, i, 0))
    row_spec = pl.BlockSpec((1, 1, T), lambda h, i: (h, 0, 0))
    return q_spec, k_spec, v_spec, col_spec, row_spec


def _fox_fwd(qkn, p, cumcol, cumrow, duty=None):
    def body(q_ref, k_ref, v_ref, cc_ref, cr_ref, o_ref, lse_ref):
        i = pl.program_id(1)

        def visible(kl):
            s = _fox_logits(q_ref, k_ref, cc_ref, cr_ref, i, kl)
            m = jnp.max(s, axis=-1, keepdims=True)
            e = jnp.exp(s - m)
            l = jnp.sum(e, axis=-1, keepdims=True)
            o_ref[...] = _dot(e / l, v_ref[0:kl, :], NN)
            lse_ref[0] = m + jnp.log(l)

        _per_query_block(i, visible)

    q_spec, k_spec, v_spec, col_spec, row_spec = _fox_specs()
    return _pcall(
        body, name="fox_fwd", grid=(N_HEADS, T // TQ),
        out_shape=[jax.ShapeDtypeStruct((T, BW), F32), jax.ShapeDtypeStruct((N_HEADS, T, 1), F32)],
        in_specs=[q_spec, k_spec, v_spec, col_spec, row_spec], out_specs=[q_spec, col_spec],
        sem=("parallel", "parallel"), args=(qkn, qkn, p, cumcol, cumrow), duty=duty)


def _fox_bwd(qkn, p, cumcol, cumrow, lse, o, do, duty=None):
    def body(q_ref, k_ref, v_ref, cc_ref, cr_ref, lse_ref, o_ref, do_ref, dq_ref, dk_ref, dv_ref, dr_ref, dc_ref):
        i = pl.program_id(1)
        @pl.when(i == 0)
        def _():
            dk_ref[...] = jnp.zeros_like(dk_ref)
            dv_ref[...] = jnp.zeros_like(dv_ref)
            dc_ref[...] = jnp.zeros_like(dc_ref)

        def visible(kl):
            q, dov = q_ref[...], do_ref[...]
            pm = jnp.exp(_fox_logits(q_ref, k_ref, cc_ref, cr_ref, i, kl) - lse_ref[0])
            delta = jnp.sum(o_ref[...] * dov, axis=-1, keepdims=True)
            ds = pm * (_dot(dov, v_ref[0:kl, :], NT) - delta)
            dq_ref[...] = _dot(ds, k_ref[0:kl, :], NN) * (HD ** -0.5)
            dr_ref[0] = jnp.sum(ds, axis=-1, keepdims=True)
            dk_ref[0:kl, :] += _dot(ds, q, TN) * (HD ** -0.5)
            dv_ref[0:kl, :] += _dot(pm, dov, TN)
            dc_ref[0, :, 0:kl] += _sum0(ds)

        _per_query_block(i, visible)

    q_spec, k_spec, v_spec, col_spec, row_spec = _fox_specs()
    acc_spec = pl.BlockSpec((T, HD), lambda h, i: (0, h))
    sh = jax.ShapeDtypeStruct((T, BW), F32)
    return _pcall(
        body, name="fox_bwd", grid=(N_HEADS, T // TQ),
        out_shape=[sh, sh, sh, jax.ShapeDtypeStruct((N_HEADS, T, 1), F32), jax.ShapeDtypeStruct((N_HEADS, 1, T), F32)],
        in_specs=[q_spec, k_spec, v_spec, col_spec, row_spec, col_spec, q_spec, q_spec],
        out_specs=[q_spec, acc_spec, acc_spec, col_spec, row_spec],
        sem=("parallel", "arbitrary"), args=(qkn, qkn, p, cumcol, cumrow, lse, o, do), duty=duty)


def _mix_fwd(gpre, b_mg, y0, y1, y2, duty=None):
    def body(g_ref, b_ref, y0_ref, y1_ref, y2_ref, o_ref):
        acc = None
        for n, y_ref in enumerate((y0_ref, y1_ref, y2_ref)):
            sl = slice(n * D, (n + 1) * D)
            t = _sigmoid(g_ref[:, sl].astype(F32) + b_ref[:, sl]) * y_ref[...].astype(F32)
            acc = t if acc is None else acc + t
        o_ref[...] = acc.astype(BF16)

    return _pcall(
        body, name="mix_fwd", grid=(T // ROW_TILE,), out_shape=jax.ShapeDtypeStruct((T, D), BF16),
        in_specs=[_row_spec(3 * D), _vec_spec(3 * D), _row_spec(), _row_spec(), _row_spec()],
        out_specs=_row_spec(), sem=("parallel",), args=(gpre, b_mg, y0, y1, y2), duty=duty)


def _mix_bwd(gpre, b_mg, y0, y1, y2, dmi, duty=None):
    def body(g_ref, b_ref, y0_ref, y1_ref, y2_ref, d_ref, dy0_ref, dy1_ref, dy2_ref, dg_ref, db_ref):
        @pl.when(pl.program_id(0) == 0)
        def _():
            db_ref[...] = jnp.zeros_like(db_ref)

        dv = d_ref[...]
        for n, (y_ref, dy_ref) in enumerate(((y0_ref, dy0_ref), (y1_ref, dy1_ref), (y2_ref, dy2_ref))):
            sl = slice(n * D, (n + 1) * D)
            sg = _sigmoid(g_ref[:, sl].astype(F32) + b_ref[:, sl])
            dy_ref[...] = (dv * sg).astype(BF16)
            dpre = dv * y_ref[...].astype(F32) * (sg * (1.0 - sg))
            dg_ref[:, sl] = dpre.astype(BF16)
            db_ref[0:1, sl] += _sum0(dpre)

    shb = jax.ShapeDtypeStruct((T, D), BF16)
    return _pcall(
        body, name="mix_bwd", grid=(T // ROW_TILE,),
        out_shape=[shb, shb, shb, jax.ShapeDtypeStruct((T, 3 * D), BF16), jax.ShapeDtypeStruct((8, 3 * D), F32)],
        in_specs=[_row_spec(3 * D), _vec_spec(3 * D), _row_spec(), _row_spec(), _row_spec(), _row_spec()],
        out_specs=[_row_spec(), _row_spec(), _row_spec(), _row_spec(3 * D), pl.BlockSpec((8, 3 * D), lambda i: (0, 0))],
        sem=("arbitrary",), args=(gpre, b_mg, y0, y1, y2, dmi), duty=duty)


FF_COLS = 256
FF_NBLK = D_FF // FF_COLS


def _shift_rows(a, n):
    rows = lax.broadcasted_iota(jnp.int32, a.shape, 0)
    rolled = pltpu.roll(a, n % T, 0)
    return jnp.where((rows >= n) if n > 0 else (rows < T + n), rolled, 0.0)


def _ffn_act_fwd(uu, w_conv, b_conv, duty=None):
    def body(u_ref, g_ref, w_ref, b_ref, o_ref):
        u = u_ref[...].astype(F32)
        w = w_ref[...]
        uc = b_ref[...] + w[0:1, :] * _shift_rows(u, 2) + w[1:2, :] * _shift_rows(u, 1) + w[2:3, :] * u
        o_ref[...] = (uc * _sigmoid(uc) * g_ref[...].astype(F32)).astype(BF16)

    return _pcall(
        body, name="ffn_act_fwd", grid=(FF_NBLK,), out_shape=jax.ShapeDtypeStruct((T, D_FF), BF16),
        in_specs=[pl.BlockSpec((T, FF_COLS), lambda j: (0, j)), pl.BlockSpec((T, FF_COLS), lambda j: (0, FF_NBLK + j)),
                  pl.BlockSpec((3, FF_COLS), lambda j: (0, j)), pl.BlockSpec((1, FF_COLS), lambda j: (0, j))],
        out_specs=pl.BlockSpec((T, FF_COLS), lambda j: (0, j)),
        sem=("parallel",), args=(uu, uu, w_conv, b_conv), duty=duty)


def _ffn_act_bwd(uu, w_conv, b_conv, da, duty=None):
    def body(u_ref, g_ref, w_ref, b_ref, da_ref, d_ref, st_ref):
        u, w, dav = u_ref[...].astype(F32), w_ref[...], da_ref[...]
        u1, u2 = _shift_rows(u, 1), _shift_rows(u, 2)
        uc = b_ref[...] + w[0:1, :] * u2 + w[1:2, :] * u1 + w[2:3, :] * u
        sg = _sigmoid(uc)
        d_ref[1] = (dav * (uc * sg)).astype(BF16)
        duc = dav * g_ref[...].astype(F32) * (sg * (1.0 + uc * (1.0 - sg)))
        du = w[2:3, :] * duc + w[1:2, :] * _shift_rows(duc, -1) + w[0:1, :] * _shift_rows(duc, -2)
        d_ref[0] = du.astype(BF16)
        st_ref[...] = jnp.zeros_like(st_ref)
        st_ref[0:1, :] = _sum0(duc * u2)
        st_ref[1:2, :] = _sum0(duc * u1)
        st_ref[2:3, :] = _sum0(duc * u)
        st_ref[3:4, :] = _sum0(duc)

    cb = lambda rows=T, off=0: pl.BlockSpec((rows, FF_COLS), lambda j: (0, off + j))
    return _pcall(
        body, name="ffn_act_bwd", grid=(FF_NBLK,),
        out_shape=[jax.ShapeDtypeStruct((2, T, D_FF), BF16), jax.ShapeDtypeStruct((8, D_FF), F32)],
        in_specs=[cb(), cb(T, FF_NBLK), cb(3), cb(1), cb()],
        out_specs=[pl.BlockSpec((2, T, FF_COLS), lambda j: (0, 0, j)), cb(8)],
        sem=("parallel",), args=(uu, uu, w_conv, b_conv, da), duty=duty)


def _adamw(g, w, m, v, tr, name, tc=None):
    partial = g.ndim == 3
    R, C = w.shape
    tr = R if tr is None else tr
    tc = C if tc is None else tc
    assert R % tr == 0 and C % tc == 0 and (tr == R or tc == C)

    def body(g_ref, w_ref, m_ref, v_ref, go_ref, d_ref, mo_ref, vo_ref):
        if partial:
            gv = g_ref[0].astype(F32)
            for j in range(1, g.shape[0]):
                gv = gv + g_ref[j].astype(F32)
        else:
            gv = g_ref[...]
        go_ref[...] = gv
        mn = ADAM_B1 * m_ref[...] + (1.0 - ADAM_B1) * gv
        vn = ADAM_B2 * v_ref[...] + (1.0 - ADAM_B2) * (gv * gv)
        mo_ref[...] = mn
        vo_ref[...] = vn
        m_hat = mn / (1.0 - ADAM_B1 ** ADAM_STEP)
        v_hat = vn / (1.0 - ADAM_B2 ** ADAM_STEP)
        d_ref[...] = -ADAM_LR * (m_hat / (jnp.sqrt(v_hat) + ADAM_EPS) + ADAM_WD * w_ref[...])

    by_rows = tc == C
    spec = pl.BlockSpec((tr, tc), (lambda i: (i, 0)) if by_rows else (lambda i: (0, i)))
    g_spec = pl.BlockSpec((g.shape[0], tr, tc), (lambda i: (0, i, 0)) if by_rows else (lambda i: (0, 0, i))) if partial else spec
    sh = jax.ShapeDtypeStruct((R, C), F32)
    return pl.pallas_call(
        body, name=name, grid=(R // tr if by_rows else C // tc,), out_shape=[sh, sh, sh, sh],
        in_specs=[g_spec, spec, spec, spec], out_specs=[spec, spec, spec, spec],
        compiler_params=_cp("parallel"))(g, w, m, v)


def _chip_sum(dw, stage, name):
    n_chip, n, C = stage.shape

    def body(d_ref, s_ref, o_ref):
        mc = lax.axis_index("c")
        mine = jnp.where(mc == 0, d_ref[0, 0].astype(F32), d_ref[0, 1].astype(F32))
        o_ref[0] = (mine + s_ref[0].astype(F32)).astype(BF16)

    return pl.pallas_call(
        body, name=name, grid=(n_chip,), out_shape=jax.ShapeDtypeStruct(stage.shape, BF16),
        in_specs=[pl.BlockSpec((1, 2, n, C), lambda p: (p, 0, 0, 0)), pl.BlockSpec((1, n, C), lambda p: (p, 0, 0))],
        out_specs=pl.BlockSpec((1, n, C), lambda p: (p, 0, 0)), compiler_params=_cp("parallel"),
    )(dw.reshape(n_chip, 2, n, C), stage)


def _sum_partials(g, name, tr=None):
    n_part, R, C = g.shape
    tr = R if tr is None else tr
    assert R % tr == 0

    def body(g_ref, o_ref):
        acc = g_ref[0].astype(F32)
        for j in range(1, n_part):
            acc = acc + g_ref[j].astype(F32)
        o_ref[...] = acc

    return pl.pallas_call(
        body, name=name, grid=(R // tr,), out_shape=jax.ShapeDtypeStruct((R, C), F32),
        in_specs=[pl.BlockSpec((n_part, tr, C), lambda i: (0, i, 0))], out_specs=pl.BlockSpec((tr, C), lambda i: (i, 0)),
        compiler_params=_cp("parallel"))(g)


def _permute_in(w):
    pad = jnp.zeros(w.shape[:-1] + (NP - IN_W,), w.dtype)
    return jnp.concatenate([w[..., :3072], w[..., 3088:5136], w[..., 3072:3088], w[..., 5136:5140], pad], axis=-1)


def _unpermute_in(w):
    return jnp.concatenate([w[..., :3072], w[..., 5120:5136], w[..., 3072:5120], w[..., 5136:5140]], axis=-1)


def _flat_pack(arrs):
    flat = jnp.concatenate([a.reshape(-1).astype(F32) for a in arrs])
    n = flat.shape[0]
    rows = -(-n // 1024) * 8
    return jnp.pad(flat, (0, rows * 128 - n)).reshape(rows, 128)


def _flat_unpack(buf, shapes):
    flat = buf.reshape(-1)
    out, off = [], 0
    for s in shapes:
        n = int(np.prod(s))
        out.append(flat[off:off + n].reshape(s))
        off += n
    return out


GRAD_CHUNKS = dict(w_in=(128, 8), w_o=(128, 1), w_down=(352, 2), w_br0=(128, 1), w_br1=(128, 1), w_br2=(128, 1),
                   w_mg=(384, 4), w_up=(704, 11))
GRAD_VIA_CHIP = ("w_in", "w_down", "w_mg", "w_up")


def _send_grad(xfer, layer, k, g):
    if xfer is None:
        return g
    n, parts = GRAD_CHUNKS[k]
    row_bytes = g.shape[1] * 2
    if k not in GRAD_VIA_CHIP:
        for c in range(parts):
            xfer.add(k, ("a2a", 0, n, (layer,), c * (n // parts), n // parts), g, nbytes=n // parts * row_bytes)
        return g
    stage = ("stage", layer, k)
    xfer.lands[stage] = lax.empty((N_DEV // 2, n, g.shape[1]), BF16)
    xfer.add(stage, ("to_other_core", 0, n, (), 0, n), g, group=stage, nbytes=n * row_bytes)

    def both_halves_here():
        chip = _chip_sum(g, xfer.lands[stage], "chip_sum_" + k)
        for c in range(parts):
            xfer.add(k, ("a2a_chip", 0, n, (layer,), c * (n // parts), n // parts), chip, nbytes=n // parts * row_bytes)

    xfer.when_done(stage, both_halves_here)
    return g


def _weight(wl, k):
    return wl[k]() if callable(wl[k]) else wl[k]


MIN_CARRIER_US = 19.0


def _taker(xfer, fill=1.0):
    if xfer is None:
        return lambda us: None
    return lambda us: xfer.take_for(us * fill) if us >= MIN_CARRIER_US else None


def _layer_fwd(x0, wl, consts, xfer=None, fill=1.25):
    cosf, sinf, logg = consts
    row = lambda a: a.reshape(1, -1)
    take = _taker(xfer, fill)
    h = _norm_fwd(x0, row(wl["norm1_g"]), row(wl["scale1"]), row(wl["shift1"]), "norm1_fwd", duty=take(9))
    w_in = _weight(wl, "w_in")
    p = _matmul(h, w_in, "nn", "in_proj", out_dtype=BF16, n_blocks=(0, TAIL0 // 512), duty=take(30))
    tail = _matmul(h, w_in, "nn", "in_tail", tn=128, n_blocks=(TAIL_BLK, 1))
    qk = _rope_fwd(p, cosf, sinf, duty=take(10))
    ret_raw = _ret_fwd(qk, p, logg, duty=take(27))
    br0 = _branch_post_fwd(ret_raw, p, row(wl["ret_norm_g"]), RG_BLK, True, "ret_post_fwd", duty=take(9))
    gla_raw, states = _gla_fwd(p, tail, wl["w2pad"], row(wl["b_gla_a"]), duty=take(26))
    br1 = _branch_post_fwd(gla_raw, p, row(wl["gla_norm_g"]), GG_BLK, False, "gla_post_fwd", duty=take(9))
    qkn, cum = _fox_prep_fwd(p, tail, row(wl["q_norm_g"]), row(wl["k_norm_g"]), row(wl["btail"]), duty=take(10))
    cum4 = cum[:, FF_LANE0:FF_LANE0 + N_HEADS].T
    cumcol, cumrow = cum4.reshape(N_HEADS, T, 1), cum4.reshape(N_HEADS, 1, T)
    fox_o, lse = _fox_fwd(qkn, p, cumcol, cumrow, duty=take(30))
    w_br_t = _weight(wl, "w_br_t")
    ys = [_matmul(b, w_br_t[n], "nt", "br_proj%d" % n, out_dtype=BF16) for n, b in enumerate((br0, br1, fox_o))]
    gpre = _matmul(h, _weight(wl, "w_mg_t"), "nt", "gate_proj", out_dtype=BF16, duty=take(20))
    mixed_in = _mix_fwd(gpre, row(wl["b_mg"]), *ys, duty=take(21))
    mixed = _matmul(mixed_in, _weight(wl, "w_o"), "nn", "o_proj", duty=take(10))
    x1, h2 = _norm_fwd(x0, row(wl["norm2_g"]), row(wl["scale2"]), row(wl["shift2"]), "norm2_fwd",
                       m=mixed, gate=row(wl["gate1"]), duty=take(13))
    uu = _matmul(h2, _weight(wl, "w_up_t"), "nt", "up_proj", out_dtype=BF16, duty=take(30))
    act = _ffn_act_fwd(uu, wl["w_conv"], row(wl["b_conv"]), duty=take(25))
    y = _matmul(act, _weight(wl, "w_down"), "nn", "down_proj", tk=1408, duty=take(24))
    x2 = _axpy(x1, y, row(wl["gate2"]), "resid2", duty=take(11))
    saved = dict(x0=x0, h=h, p=p, tail=tail, qk=qk, ret_raw=ret_raw, br0=br0, gla_raw=gla_raw, states=states, br1=br1,
                 qkn=qkn, cumcol=cumcol, cumrow=cumrow, fox_o=fox_o, lse=lse, y0=ys[0], y1=ys[1], y2=ys[2],
                 gpre=gpre, mixed_in=mixed_in, mixed=mixed, x1=x1, h2=h2, uu=uu, act=act, y=y)
    return x2, saved


def _layer_bwd(dx2, wl, sv, consts, xfer=None, layer=0):
    cosf, sinf, logg = consts
    row = lambda a: a.reshape(1, -1)
    take = _taker(xfer)

    send = functools.partial(_send_grad, xfer, layer)

    dy, st_g2 = _gate_bwd(dx2, sv["y"], row(wl["gate2"]), "gate2_bwd", duty=take(10))
    dact = _matmul(dy, _weight(wl, "w_down"), "nt", "down_dx", tn=1408, duty=take(21))
    d_down = send("w_down", _matmul(sv["act"], dy, "tn", "down_dw", out_dtype=BF16, tm=1408, duty=take(19)))
    duu, st_conv = _ffn_act_bwd(sv["uu"], wl["w_conv"], row(wl["b_conv"]), dact, duty=take(40))
    dh2 = _matmul(duu, _weight(wl, "w_up_t"), "nn", "up_dx", tk=1408, duty=take(42))
    d_up_t = send("w_up", _matmul(duu, sv["h2"], "tn", "up_dw", out_dtype=BF16, tm=1408, duty=take(33)))
    dx1, st_n2 = _norm_bwd(sv["x1"], dh2, dx2, row(wl["norm2_g"]), row(wl["scale2"]), row(wl["shift2"]), "norm2_bwd",
                           duty=take(15))
    dmixed, st_g1 = _gate_bwd(dx1, sv["mixed"], row(wl["gate1"]), "gate1_bwd", duty=take(10))
    dmi = _matmul(dmixed, _weight(wl, "w_o"), "nt", "o_dx", duty=take(11))
    d_o = send("w_o", _matmul(sv["mixed_in"], dmixed, "tn", "o_dw", out_dtype=BF16, duty=take(9)))
    dy0, dy1, dy2, dgpre, st_bmg = _mix_bwd(sv["gpre"], row(wl["b_mg"]), sv["y0"], sv["y1"], sv["y2"], dmi,
                                             duty=take(31))
    brs = (sv["br0"], sv["br1"], sv["fox_o"])
    w_br_t = _weight(wl, "w_br_t")
    dbr = [_matmul(d, w_br_t[n], "nn", "br_dx%d" % n) for n, d in enumerate((dy0, dy1, dy2))]
    d_br_t = [send("w_br%d" % n, _matmul(d, brs[n], "tn", "br_dw%d" % n, out_dtype=BF16))
              for n, d in enumerate((dy0, dy1, dy2))]
    dh = _matmul(dgpre, _weight(wl, "w_mg_t"), "nn", "gate_dx", tk=1024, duty=take(29))
    d_mg_t = send("w_mg", _matmul(dgpre, sv["h"], "tn", "gate_dw", out_dtype=BF16, duty=take(21)))
    p, tail = sv["p"], sv["tail"]
    dqn, dkn, dfv, drow, dcol = _fox_bwd(sv["qkn"], p, sv["cumcol"], sv["cumrow"], sv["lse"], sv["fox_o"], dbr[2],
                                         duty=take(50))
    dcum4 = drow.reshape(N_HEADS, T) - dcol.reshape(N_HEADS, T)
    dcum = jnp.pad(dcum4.T, ((0, 0), (FF_LANE0, 128 - FF_LANE0 - N_HEADS)))
    dfqk, dtail_fox, st_fox = _fox_prep_bwd(p, tail, row(wl["q_norm_g"]), row(wl["k_norm_g"]), row(wl["btail"]), dqn, dkn, dcum,
                                            duty=take(15))
    dgla_raw, dgg, st_gn = _branch_post_bwd(sv["gla_raw"], p, row(wl["gla_norm_g"]), dbr[1], GG_BLK, False, "gla_post_bwd",
                                            duty=take(12))
    dgq, dgk, dgv, dtail_gla, dw2pad, st_bg = _gla_bwd(p, tail, wl["w2pad"], row(wl["b_gla_a"]), sv["states"], dgla_raw,
                                                       duty=take(30))
    dret_raw, drg, st_rn = _branch_post_bwd(sv["ret_raw"], p, row(wl["ret_norm_g"]), dbr[0], RG_BLK, True, "ret_post_bwd",
                                            duty=take(13))
    dqr, dkr, drv = _ret_bwd(sv["qk"], p, logg, dret_raw, duty=take(50))
    drqk = _rope_bwd(dqr, dkr, cosf, sinf, duty=take(11))
    dp = jnp.concatenate([a.astype(BF16) for a in (drqk, drv, drg, dgq, dgk, dgv, dgg, dfqk, dfv, dtail_fox + dtail_gla)]
                         + [jnp.zeros((T, NP - TAIL0 - 128), BF16)], axis=1)
    dh = _matmul(dp, _weight(wl, "w_in"), "nt", "in_dx", tk=1408, add=dh, duty=take(45))
    d_in = send("w_in", _matmul(sv["h"], dp, "tn", "in_dw", out_dtype=BF16, duty=take(32)))
    dx0, st_n1 = _norm_bwd(sv["x0"], dh, dx1, row(wl["norm1_g"]), row(wl["scale1"]), row(wl["shift1"]), "norm1_bwd",
                           duty=take(15))
    big = dict(w_in=d_in, w_o=d_o, w_down=d_down, w_br0=d_br_t[0], w_br1=d_br_t[1], w_br2=d_br_t[2], w_mg=d_mg_t,
               w_up=d_up_t)
    dmod = jnp.concatenate([st_n1[2], st_n1[1], st_g1[0], st_n2[2], st_n2[1], st_g2[0]])
    small = dict(norm1_g=st_n1[0], norm2_g=st_n2[0], b_gla_a=st_bg[0], b_fox_f=st_fox[2, FF_LANE0:FF_LANE0 + N_HEADS],
                 ret_norm_g=st_rn[0], gla_norm_g=st_gn[0], q_norm_g=st_fox[0], k_norm_g=st_fox[1], b_mg=st_bmg[0],
                 b_conv=st_conv[3], w_gla_a2=dw2pad[:LR_LANES], w_conv=st_conv[0:3])
    return dx0, big, dmod, small


SMALL_REPL = ("norm1_g", "norm2_g", "b_ada", "b_gla_a", "b_fox_f", "ret_norm_g", "gla_norm_g", "q_norm_g", "k_norm_g",
              "b_mg", "b_conv")
SMALL_SHARDED = ("w_gla_a2", "w_conv")
BIG = ("w_in", "w_o", "w_down", "w_br", "w_mg", "w_up")
WEIGHTS = ("norm1_g", "norm2_g", "w_ada", "b_ada", "w_in", "w_gla_a2", "b_gla_a", "b_fox_f", "ret_norm_g", "gla_norm_g",
           "q_norm_g", "k_norm_g", "w_br", "w_mg", "b_mg", "w_o", "w_up", "w_conv", "b_conv", "w_down")


def kernel(x, c, norm1_g, norm2_g, w_ada, b_ada, w_in, w_gla_a2, b_gla_a, b_fox_f, ret_norm_g, gla_norm_g, q_norm_g, k_norm_g, w_br, w_mg, b_mg, w_o, w_up, w_conv, b_conv, w_down, loss_target, m_norm1_g, m_norm2_g, m_w_ada, m_b_ada, m_w_in, m_w_gla_a2, m_b_gla_a, m_b_fox_f, m_ret_norm_g, m_gla_norm_g, m_q_norm_g, m_k_norm_g, m_w_br, m_w_mg, m_b_mg, m_w_o, m_w_up, m_w_conv, m_b_conv, m_w_down, v_norm1_g, v_norm2_g, v_w_ada, v_b_ada, v_w_in, v_w_gla_a2, v_b_gla_a, v_b_fox_f, v_ret_norm_g, v_gla_norm_g, v_q_norm_g, v_k_norm_g, v_w_br, v_w_mg, v_b_mg, v_w_o, v_w_up, v_w_conv, v_b_conv, v_w_down):
    W = dict(norm1_g=norm1_g, norm2_g=norm2_g, w_ada=w_ada, b_ada=b_ada, w_in=w_in, w_gla_a2=w_gla_a2, b_gla_a=b_gla_a,
             b_fox_f=b_fox_f, ret_norm_g=ret_norm_g, gla_norm_g=gla_norm_g, q_norm_g=q_norm_g, k_norm_g=k_norm_g,
             w_br=w_br, w_mg=w_mg, b_mg=b_mg, w_o=w_o, w_up=w_up, w_conv=w_conv, b_conv=b_conv, w_down=w_down)
    M = dict(norm1_g=m_norm1_g, norm2_g=m_norm2_g, w_ada=m_w_ada, b_ada=m_b_ada, w_in=m_w_in, w_gla_a2=m_w_gla_a2,
             b_gla_a=m_b_gla_a, b_fox_f=m_b_fox_f, ret_norm_g=m_ret_norm_g, gla_norm_g=m_gla_norm_g, q_norm_g=m_q_norm_g,
             k_norm_g=m_k_norm_g, w_br=m_w_br, w_mg=m_w_mg, b_mg=m_b_mg, w_o=m_w_o, w_up=m_w_up, w_conv=m_w_conv,
             b_conv=m_b_conv, w_down=m_w_down)
    V = dict(norm1_g=v_norm1_g, norm2_g=v_norm2_g, w_ada=v_w_ada, b_ada=v_b_ada, w_in=v_w_in, w_gla_a2=v_w_gla_a2,
             b_gla_a=v_b_gla_a, b_fox_f=v_b_fox_f, ret_norm_g=v_ret_norm_g, gla_norm_g=v_gla_norm_g, q_norm_g=v_q_norm_g,
             k_norm_g=v_k_norm_g, w_br=v_w_br, w_mg=v_w_mg, b_mg=v_b_mg, w_o=v_w_o, w_up=v_w_up, w_conv=v_w_conv,
             b_conv=v_b_conv, w_down=v_w_down)
    me = 4 * lax.axis_index("x") + 2 * lax.axis_index("y") + lax.axis_index("c")
    x2d, tgt = x.reshape(T, D), loss_target.reshape(T, D)

    sm = _flat_pack([c, w_gla_a2, w_conv])
    sm_all = _exchange(sm, True, "gather_small")
    parts = [_flat_unpack(sm_all[j], [(D,), (DEPTH, LR_LANES, 32), (DEPTH, 3, 352)]) for j in range(N_DEV)]
    c_all = jnp.stack([q[0] for q in parts])
    w_gla_full = jnp.concatenate([q[1] for q in parts], axis=2)
    w_conv_full = jnp.concatenate([q[2] for q in parts], axis=2)

    n_ada = w_ada.shape[2]
    b_loc = lax.dynamic_slice_in_dim(b_ada, me * n_ada, n_ada, axis=1).reshape(DEPTH, 1, n_ada)
    mod_all = _ada_fwd(c_all, w_ada, b_loc)
    mod_recv = _exchange(jnp.swapaxes(mod_all, 0, 1), False, "a2a_mod")
    mod = jnp.swapaxes(mod_recv, 0, 1).reshape(DEPTH, 6, D)

    loc = dict(w_in=_permute_in(w_in), w_o=w_o, w_down=w_down, w_br=jnp.swapaxes(w_br, 2, 3),
               w_mg=jnp.swapaxes(w_mg, 1, 2), w_up=jnp.swapaxes(w_up, 1, 2))
    loc = {k: v.astype(BF16) for k, v in loc.items()}
    w_full = dict(w_in=(D, NP), w_o=(D, D), w_down=(D_FF, D), w_br=(3, D, BW), w_mg=(3 * D, D), w_up=(2 * D_FF, D))
    w_parts = dict(w_in=8, w_br=2, w_mg=4, w_o=1, w_up=11, w_down=2)
    gather, units = _Transfers("gather"), []
    for l in range(DEPTH):
        for k, parts in w_parts.items():
            axis = 1 if k == "w_br" else 0
            n = w_full[k][axis] // N_DEV
            gather.lands[(l, k)] = lax.empty(w_full[k], BF16)
            shard = loc[k][l]
            nbytes = shard.size * 2 // parts
            units += [((l, k), (axis, n, (), c * (n // parts), n // parts), shard, nbytes) for c in range(parts)]
    first, lag = w_parts["w_in"], 4
    order = [("cross", i) for i in range(first)] + [("pass", i) for i in range(first)]
    for i in range(first, len(units) + lag):
        order += [("cross", i)] if i < len(units) else []
        order += [("pass", i - lag)] if i - lag >= first else []
    for what, i in order:
        key, where, shard, nbytes = units[i]
        if what == "cross":
            gather.add(key, ("gather_chip",) + where, shard, uid=i, nbytes=nbytes)
        else:
            gather.add(key, ("pass_on",) + where, after=i, nbytes=nbytes)

    w2pad = jnp.pad(w_gla_full, ((0, 0), (0, 128 - LR_LANES), (0, 0)))
    btail = jnp.pad(b_fox_f, ((0, 0), (FF_LANE0, 128 - FF_LANE0 - N_HEADS)))
    stacked = dict(norm1_g=norm1_g, norm2_g=norm2_g, b_gla_a=b_gla_a, ret_norm_g=ret_norm_g, gla_norm_g=gla_norm_g,
                   q_norm_g=q_norm_g, k_norm_g=k_norm_g, b_mg=b_mg, b_conv=b_conv, w_conv=w_conv_full, w2pad=w2pad,
                   btail=btail, shift1=mod[:, 0], scale1=mod[:, 1], gate1=mod[:, 2], shift2=mod[:, 3], scale2=mod[:, 4],
                   gate2=mod[:, 5])
    landed = lambda l, k: functools.partial(gather.get, (l, k))
    layers = [dict({k: v[l] for k, v in stacked.items()}, w_in=landed(l, "w_in"), w_o=landed(l, "w_o"),
                   w_down=landed(l, "w_down"), w_br_t=landed(l, "w_br"), w_mg_t=landed(l, "w_mg"), w_up_t=landed(l, "w_up"))
              for l in range(DEPTH)]
    consts = _rope_tables() + (_ret_logg(),)

    xc, saved = x2d, []
    for l in range(DEPTH):
        xc, sv = _layer_fwd(xc, layers[l], consts, gather, 1.7 if l == 0 else 1.45)
        saved.append(sv)
    loss_part, dxc = _loss_fwd_bwd(xc, tgt)
    loss = lax.psum(loss_part[0, 0], ("x", "y", "c"))

    grad_names = ("w_in", "w_o", "w_down", "w_br0", "w_br1", "w_br2", "w_mg", "w_up")
    blk_rows = dict(w_in=(128, NP), w_o=(128, D), w_down=(352, D), w_br0=(128, BW), w_br1=(128, BW), w_br2=(128, BW),
                    w_mg=(384, D), w_up=(704, D))
    grads = _Transfers("grads")
    for k in grad_names:
        grads.lands[k] = lax.empty((N_DEV // 2 if k in GRAD_VIA_CHIP else N_DEV, DEPTH) + blk_rows[k], BF16)
    dmod, small_g = [None] * DEPTH, [None] * DEPTH
    for l in reversed(range(DEPTH)):
        dxc, _, dmod[l], small_g[l] = _layer_bwd(dxc, layers[l], saved[l], consts, grads, l)
    grad_x = dxc
    dmod = jnp.stack(dmod)
    small_g = {k: jnp.stack([s[k] for s in small_g]) for k in small_g[0]}
    grads.drain()
    recv = {k: grads.get(k) for k in grad_names}

    dmod_send = jnp.swapaxes(dmod.reshape(DEPTH, N_DEV, n_ada), 0, 1)
    dmod_all = jnp.swapaxes(_exchange(dmod_send, False, "a2a_dmod"), 0, 1)
    g_ada = _ada_bwd(c_all, dmod_all)

    def flat(a, k):
        return a.reshape((-1, W[k].shape[-1]))

    def adam_nat(k, g, tr):
        outs = _adamw(g, flat(W[k], k), flat(M[k], k), flat(V[k], k), tr, "adamw_" + k)
        return [o.reshape(W[k].shape) for o in outs]

    def summed(k, tr):
        r = recv[k]
        return _sum_partials(r.reshape(r.shape[0], DEPTH * r.shape[2], r.shape[3]), "sum_" + k, tr).reshape((DEPTH,) + r.shape[2:])

    def adam_as_stored(k, g, perm, tr, tc=None):
        shape_t = tuple(W[k].shape[a] for a in perm)
        view = lambda a: jnp.transpose(a, perm).reshape(g.shape[-2:])
        outs = _adamw(g, view(W[k]), view(M[k]), view(V[k]), tr, "adamw_" + k, tc)
        return [jnp.transpose(o.reshape(shape_t), tuple(np.argsort(perm))) for o in outs]

    g_in = jnp.transpose(_unpermute_in(summed("w_in", 64)), (2, 0, 1)).reshape(IN_W, DEPTH * 128)
    big_out = dict(
        w_in=adam_as_stored("w_in", g_in, (2, 0, 1), None, 128),
        w_o=adam_nat("w_o", recv["w_o"].reshape(N_DEV, DEPTH * 128, D), 128),
        w_down=adam_nat("w_down", recv["w_down"].reshape(N_DEV // 2, DEPTH * 352, D), 352),
        w_br=adam_nat("w_br", flat(jnp.swapaxes(jnp.stack([summed("w_br%d" % n, 128) for n in range(3)], axis=1), 2, 3),
                                   "w_br"), 1024),
        w_mg=adam_nat("w_mg", flat(jnp.swapaxes(summed("w_mg", 384), 1, 2), "w_mg"), 512),
        w_up=adam_as_stored("w_up", recv["w_up"].reshape(N_DEV // 2, DEPTH * 704, D), (0, 2, 1), 352))
    ada_out = [o.reshape(DEPTH, D, n_ada) for o in _adamw(
        g_ada.reshape(DEPTH * D, n_ada), w_ada.reshape(DEPTH * D, n_ada), m_w_ada.reshape(DEPTH * D, n_ada),
        v_w_ada.reshape(DEPTH * D, n_ada), 512, "adamw_ada")]

    small_g = dict(small_g, b_ada=dmod)
    names = SMALL_REPL + SMALL_SHARDED
    full_shapes = [W[n].shape for n in SMALL_REPL] + [(DEPTH, LR_LANES, 256), (DEPTH, 3, D_FF)]
    part = _flat_pack([small_g[n] for n in names])
    total = _flat_unpack(_sum_partials(_exchange(part, True, "gather_small_grads"), "sum_small"), full_shapes)
    total = dict(zip(names, total))
    total["w_gla_a2"] = lax.dynamic_slice_in_dim(total["w_gla_a2"], me * 32, 32, axis=2)
    total["w_conv"] = lax.dynamic_slice_in_dim(total["w_conv"], me * 352, 352, axis=2)
    shapes = [W[n].shape for n in names]
    small_out = _adamw(_flat_pack([total[n] for n in names]), _flat_pack([W[n] for n in names]),
                       _flat_pack([M[n] for n in names]), _flat_pack([V[n] for n in names]), None, "adamw_small")
    small_out = [dict(zip(names, _flat_unpack(o, shapes))) for o in small_out]

    outs = []
    for k in range(4):
        d = dict(small_out[k])
        d.update({n: big_out[n][k] for n in BIG})
        d["w_ada"] = ada_out[k]
        outs.append([d[n] for n in WEIGHTS])
    return (loss, grad_x.reshape(1, T, D), *outs[0], *outs[1], *outs[2], *outs[3])
```

```python
import functools

import numpy as np
import jax
import jax.numpy as jnp
from jax import lax
from jax.experimental import pallas as pl
from jax.experimental.pallas import tpu as pltpu

F32 = jnp.float32
BF16 = jnp.bfloat16

N_DEV = 8
T = 2048
D = 1024
DEPTH = 4
N_HEADS = 4
HD = 128
BW = 512
D_FF = 2816
CHUNK = 64
EPS = 1e-6
IN_W = 5140
NP = 5632
TAIL0 = 5120
LR_LANES = 16
FF_LANE0 = 16
PACK_W = 1024
SEG_ROWS = (704, 128, 352, 192, 384, 704)
LAYER_ROWS = sum(SEG_ROWS)
VMEM_LIMIT_V7X = 56 * 1024 * 1024

ADAM_LR, ADAM_B1, ADAM_B2, ADAM_EPS, ADAM_WD, ADAM_STEP = 0.001, 0.9, 0.999, 1e-08, 0.01, 10

MESH_ID = pl.DeviceIdType.MESH


def _cp(*sem):
    return pltpu.CompilerParams(dimension_semantics=sem if sem else None, vmem_limit_bytes=VMEM_LIMIT_V7X)


def _sigmoid(z):
    return 1.0 / (1.0 + jnp.exp(-z))


def _log_sigmoid(z):
    return jnp.minimum(z, 0.0) - jnp.log(1.0 + jnp.exp(-jnp.abs(z)))


def _sum0(a):
    return jnp.sum(a, axis=0, keepdims=True)


def _mean1(a):
    return jnp.mean(a, axis=-1, keepdims=True)


def _dot(a, b, dims):
    return lax.dot_general(a.astype(BF16), b.astype(BF16), (dims, ((), ())), preferred_element_type=F32)


NN = ((1,), (0,))
NT = ((1,), (1,))
TN = ((0,), (0,))


def _exact_dot(m01, a):
    a1 = a.astype(BF16)
    r1 = a - a1.astype(F32)
    a2 = r1.astype(BF16)
    a3 = (r1 - a2.astype(F32)).astype(BF16)
    d = lambda z: jnp.dot(m01, z, preferred_element_type=F32)
    return d(a1) + d(a2) + d(a3)


def _tri(n, upper):
    r = lax.broadcasted_iota(jnp.int32, (n, n), 0)
    c = lax.broadcasted_iota(jnp.int32, (n, n), 1)
    return jnp.where((c >= r) if upper else (c <= r), 1.0, 0.0).astype(BF16)


def _exchange(x, gather, name):
    blk = x.shape if gather else x.shape[1:]

    def body(x_ref, o_ref, send_sems, recv_sems, loc_sem):
        mx, my, mc = lax.axis_index("x"), lax.axis_index("y"), lax.axis_index("c")
        me = 4 * mx + 2 * my + mc
        loc = pltpu.make_async_copy(x_ref if gather else x_ref.at[me], o_ref.at[me], loc_sem)
        loc.start()
        copies = []
        for k in range(1, N_DEV):
            px = mx ^ (k >> 2) if (k >> 2) else mx
            py = my ^ ((k >> 1) & 1) if ((k >> 1) & 1) else my
            pc = mc ^ (k & 1) if (k & 1) else mc
            peer = 4 * px + 2 * py + pc
            cp = pltpu.make_async_remote_copy(
                src_ref=x_ref if gather else x_ref.at[peer], dst_ref=o_ref.at[me],
                send_sem=send_sems.at[k - 1], recv_sem=recv_sems.at[k - 1],
                device_id=(px, py, pc), device_id_type=MESH_ID)
            cp.start()
            copies.append(cp)
        for cp in copies:
            cp.wait()
        loc.wait()

    return pl.pallas_call(
        body, name=name,
        out_shape=jax.ShapeDtypeStruct((N_DEV,) + tuple(blk), x.dtype),
        in_specs=[pl.BlockSpec(memory_space=pl.ANY)],
        out_specs=pl.BlockSpec(memory_space=pl.ANY),
        scratch_shapes=[pltpu.SemaphoreType.DMA((N_DEV - 1,)), pltpu.SemaphoreType.DMA((N_DEV - 1,)),
                        pltpu.SemaphoreType.DMA],
        compiler_params=pltpu.CompilerParams(has_side_effects=True),
    )(x)


def _blk(ref, axis, j, n, r0=0, nr=None):
    return ref.at[(slice(None),) * axis + (pl.ds(j * n + r0, n if nr is None else nr),)]


def _comm_copies(items, srcs, lands, send_sems, recv_sems, loc_sems):
    mx, my, mc = lax.axis_index("x"), lax.axis_index("y"), lax.axis_index("c")
    me = 4 * mx + 2 * my + mc
    local, remote = [], []
    for t, (kind, axis, n, sel, r0, nr, si, li) in enumerate(items):
        if kind == "pass_on":
            for q in (2, 4, 6):
                px = 1 - mx if q & 4 else mx
                py = 1 - my if q & 2 else my
                rows = _blk(lands[li], axis, 4 * px + 2 * py + mc, n, r0, nr)
                remote.append(pltpu.make_async_remote_copy(
                    src_ref=rows, dst_ref=rows, send_sem=send_sems.at[t * (N_DEV - 1) + q - 1],
                    recv_sem=recv_sems.at[t * (N_DEV - 1) + q - 1], device_id=(mx, my, 1 - mc), device_id_type=MESH_ID))
            continue
        if kind == "to_other_core":
            for p in range(N_DEV // 2):
                remote.append(pltpu.make_async_remote_copy(
                    src_ref=_blk(srcs[si], axis, 2 * p + 1 - mc, n, r0, nr), dst_ref=lands[li].at[p, pl.ds(r0, nr)],
                    send_sem=send_sems.at[t * (N_DEV - 1) + p], recv_sem=recv_sems.at[t * (N_DEV - 1) + p],
                    device_id=(mx, my, 1 - mc), device_id_type=MESH_ID))
            continue
        if kind == "a2a_chip":
            pm = 2 * mx + my
            mine = lands[li].at[(pm,) + tuple(sel) + (pl.ds(r0, nr),)]
            local.append(pltpu.make_async_copy(srcs[si].at[pm, pl.ds(r0, nr)], mine, loc_sems.at[t]))
            for q in (2, 4, 6):
                px = 1 - mx if q & 4 else mx
                py = 1 - my if q & 2 else my
                remote.append(pltpu.make_async_remote_copy(
                    src_ref=srcs[si].at[2 * px + py, pl.ds(r0, nr)], dst_ref=mine,
                    send_sem=send_sems.at[t * (N_DEV - 1) + q - 1], recv_sem=recv_sems.at[t * (N_DEV - 1) + q - 1],
                    device_id=(px, py, mc), device_id_type=MESH_ID))
            continue
        if kind == "a2a":
            mine = lands[li].at[(me,) + tuple(sel) + (pl.ds(r0, nr),)]
            own = _blk(srcs[si], axis, me, n, r0, nr)
        else:
            mine = _blk(lands[li], axis, me, n, r0, nr)
            own = _blk(srcs[si], axis, 0, n, r0, nr)
        local.append(pltpu.make_async_copy(own, mine, loc_sems.at[t]))
        for k in ((1, 2, 4, 6) if kind == "gather_chip" else range(1, N_DEV)):
            px = 1 - mx if k & 4 else mx
            py = 1 - my if k & 2 else my
            pc = 1 - mc if k & 1 else mc
            src = _blk(srcs[si], axis, 4 * px + 2 * py + pc, n, r0, nr) if kind == "a2a" else own
            remote.append(pltpu.make_async_remote_copy(
                src_ref=src, dst_ref=mine, send_sem=send_sems.at[t * (N_DEV - 1) + k - 1],
                recv_sem=recv_sems.at[t * (N_DEV - 1) + k - 1], device_id=(px, py, pc), device_id_type=MESH_ID))
    return local, remote


def _comm_scratch(n_items):
    return [pltpu.SemaphoreType.DMA((n_items * (N_DEV - 1),)), pltpu.SemaphoreType.DMA((n_items * (N_DEV - 1),)),
            pltpu.SemaphoreType.DMA((n_items,))]


LINK_BYTES_PER_US = dict(gather_chip=23e3, a2a_chip=23e3, a2a=11.5e3, gather=11.5e3, pass_on=200e3, to_other_core=150e3)
CALL_EXCHANGE_US = 3.0


class _Duty:
    def __init__(self, items, srcs, lands, done):
        self.items, self.srcs, self.lands, self.done = items, srcs, lands, done


def _pcall(body, name, grid, in_specs, out_specs, out_shape, args, scratch_shapes=(), sem=(), duty=None):
    if duty is None:
        return pl.pallas_call(body, name=name, grid=grid, in_specs=list(in_specs), out_specs=out_specs,
                              out_shape=out_shape, scratch_shapes=list(scratch_shapes), compiler_params=_cp(*sem))(*args)
    single = not isinstance(out_shape, (list, tuple))
    o_shape = [out_shape] if single else list(out_shape)
    o_specs = [out_specs] if single else list(out_specs)
    n_in, n_out, n_scr = len(in_specs), len(o_shape), len(scratch_shapes)
    n_src, n_land, n_items = len(duty.srcs), len(duty.lands), len(duty.items)
    a0 = n_in + n_src + n_land

    def wrapped(*refs):
        srcs = refs[n_in:n_in + n_src]
        lands = refs[a0 + n_out:a0 + n_out + n_land]
        core = refs[:n_in] + refs[a0:a0 + n_out] + refs[a0 + n_out + n_land:a0 + n_out + n_land + n_scr]
        sems = refs[a0 + n_out + n_land + n_scr:]
        first = functools.reduce(jnp.logical_and, [pl.program_id(a) == 0 for a in range(len(grid))])
        last = functools.reduce(jnp.logical_and, [pl.program_id(a) == g - 1 for a, g in enumerate(grid)])

        @pl.when(first)
        def _():
            local, remote = _comm_copies(duty.items, srcs, lands, *sems)
            for cp in local + remote:
                cp.start()

        body(*core)

        @pl.when(last)
        def _():
            local, remote = _comm_copies(duty.items, srcs, lands, *sems)
            for cp in remote + local:
                cp.wait()

    hbm = pl.BlockSpec(memory_space=pl.ANY)
    res = pl.pallas_call(
        wrapped, name=name, grid=grid,
        in_specs=list(in_specs) + [hbm] * (n_src + n_land), out_specs=o_specs + [hbm] * n_land,
        out_shape=o_shape + [jax.ShapeDtypeStruct(a.shape, a.dtype) for a in duty.lands],
        input_output_aliases={n_in + n_src + t: n_out + t for t in range(n_land)},
        scratch_shapes=list(scratch_shapes) + _comm_scratch(n_items),
        compiler_params=pltpu.CompilerParams(dimension_semantics=("arbitrary",) * len(grid),
                                             vmem_limit_bytes=VMEM_LIMIT_V7X, has_side_effects=True),
    )(*args, *duty.srcs, *duty.lands)
    duty.done(res[n_out:])
    return res[0] if single else res[:n_out]


def _comm(duty, name):
    n_src, n_land = len(duty.srcs), len(duty.lands)

    def body(*refs):
        local, remote = _comm_copies(duty.items, refs[:n_src], refs[n_src + n_land:n_src + 2 * n_land],
                                     *refs[n_src + 2 * n_land:])
        for cp in local + remote:
            cp.start()
        for cp in remote + local:
            cp.wait()

    hbm = pl.BlockSpec(memory_space=pl.ANY)
    duty.done(pl.pallas_call(
        body, name=name, out_shape=[jax.ShapeDtypeStruct(a.shape, a.dtype) for a in duty.lands],
        in_specs=[hbm] * (n_src + n_land), out_specs=[hbm] * n_land,
        input_output_aliases={n_src + t: t for t in range(n_land)},
        scratch_shapes=_comm_scratch(len(duty.items)), compiler_params=pltpu.CompilerParams(has_side_effects=True),
    )(*duty.srcs, *duty.lands))


class _Transfers:
    def __init__(self, name):
        self.name, self.queue, self.lands, self.flushes, self.groups = name, [], {}, 0, {}

    def add(self, key, item, src=None, uid=None, after=None, group=None, nbytes=0):
        self.queue.append((key, item, src, uid, after, group, nbytes / LINK_BYTES_PER_US[item[0]]))
        if group is not None:
            self.groups[group] = [self.groups.get(group, [0, None])[0] + 1, None]

    def when_done(self, group, fn):
        self.groups[group][1] = fn

    def take_for(self, us):
        count, busy = 0, CALL_EXCHANGE_US
        while count < len(self.queue) and busy + self.queue[count][6] <= us:
            busy += self.queue[count][6]
            count += 1
        return self.take(count) if count else None

    def take(self, count):
        units = []
        while self.queue and len(units) < count:
            after = self.queue[0][4]
            if after is not None and any(u[3] == after for u in units):
                break
            units.append(self.queue.pop(0))
        if not units:
            return None
        keys, srcs, items = [], [], []
        for key, item, src, _, _, _, _ in units:
            if key not in keys:
                keys.append(key)
            if src is not None and not any(src is s for s in srcs):
                srcs.append(src)
            si = [i for i, s in enumerate(srcs) if s is src][0] if src is not None else -1
            items.append(tuple(item) + (si, keys.index(key)))

        def done(new_lands):
            for key, arr in zip(keys, new_lands):
                self.lands[key] = arr
            for u in units:
                if u[5] is not None:
                    self.groups[u[5]][0] -= 1
                    if self.groups[u[5]][0] == 0:
                        self.groups[u[5]][1]()

        return _Duty(items, srcs, [self.lands[k] for k in keys], done)

    def drain(self, upto=None):
        count = upto
        while self.queue if upto is None else count > 0:
            duty = self.take(len(self.queue) if upto is None else count)
            count = None if upto is None else count - len(duty.items)
            self.flushes += 1
            _comm(duty, "%s_flush%d" % (self.name, self.flushes))

    def get(self, key):
        pending = [i for i, u in enumerate(self.queue) if u[0] == key]
        if pending:
            self.drain(pending[-1] + 1)
        return self.lands[key]


def _matmul(a, b, mode, name, out_dtype=F32, tm=1024, tn=512, tk=None, add=None, n_blocks=None, duty=None):
    halves = a.ndim == 3
    if mode == "tn":
        K, M = a.shape[-2], a.shape[-1] * (2 if halves else 1)
        N = b.shape[1]
    else:
        M, K = a.shape[-2], a.shape[-1] * (2 if halves else 1)
        N = b.shape[0] if mode == "nt" else b.shape[1]
    tm, tn = min(tm, M), min(tn, N)
    j0 = 0
    if n_blocks is not None:
        j0, N = n_blocks[0], n_blocks[1] * tn
    tk = K if tk is None else tk
    nk = K // tk
    assert M % tm == 0 and N % tn == 0 and K % tk == 0, (name, M, N, K, tm, tn, tk)
    dims = {"nn": NN, "nt": NT, "tn": TN}[mode]
    has_add = add is not None

    def body(*refs):
        a_ref, b_ref = refs[:2]
        add_ref = refs[2] if has_add else None
        o_ref = refs[3 if has_add else 2]
        part = _dot(a_ref[...], b_ref[...], dims)

        def finish(total):
            if has_add:
                total = total + add_ref[...]
            o_ref[...] = total.astype(o_ref.dtype)

        if nk == 1:
            finish(part)
            return
        acc_ref = refs[-1]
        k = pl.program_id(2)

        @pl.when(k == 0)
        def _():
            acc_ref[...] = part

        @pl.when((k > 0) & (k < nk - 1))
        def _():
            acc_ref[...] += part

        @pl.when(k == nk - 1)
        def _():
            finish(acc_ref[...] + part)

    if halves and mode == "tn":
        per = a.shape[-1] // tm
        a_spec = pl.BlockSpec((None, tk, tm), lambda i, j, k: (i // per, k, i % per))
    elif halves:
        per = a.shape[-1] // tk
        a_spec = pl.BlockSpec((None, tm, tk), lambda i, j, k: (k // per, i, k % per))
    elif mode == "tn":
        a_spec = pl.BlockSpec((tk, tm), lambda i, j, k: (k, i))
    else:
        a_spec = pl.BlockSpec((tm, tk), lambda i, j, k: (i, k))
    if mode == "nt":
        b_spec = pl.BlockSpec((tn, tk), lambda i, j, k: (j0 + j, k))
    else:
        b_spec = pl.BlockSpec((tk, tn), lambda i, j, k: (k, j0 + j))
    o_spec = pl.BlockSpec((tm, tn), lambda i, j, k: (i, j))
    in_specs = [a_spec, b_spec] + ([o_spec] if has_add else [])
    args = (a, b) + ((add,) if has_add else ())
    return _pcall(
        body, name=name, grid=(M // tm, N // tn, nk),
        out_shape=jax.ShapeDtypeStruct((M, N), out_dtype),
        in_specs=in_specs, out_specs=o_spec,
        scratch_shapes=[pltpu.VMEM((tm, tn), F32)] if nk > 1 else [],
        sem=("parallel", "parallel", "arbitrary"), args=args, duty=duty)


def _ada_fwd(c_all, w_ada, b_loc):
    n = w_ada.shape[2]

    def body(c_ref, w_ref, b_ref, o_ref):
        c = c_ref[...]
        o_ref[0] = _dot(c * _sigmoid(c), w_ref[0], NN) + b_ref[0]

    return pl.pallas_call(
        body, name="ada_fwd", grid=(DEPTH,),
        out_shape=jax.ShapeDtypeStruct((DEPTH, N_DEV, n), F32),
        in_specs=[pl.BlockSpec((N_DEV, D), lambda l: (0, 0)),
                  pl.BlockSpec((1, D, n), lambda l: (l, 0, 0)),
                  pl.BlockSpec((1, 1, n), lambda l: (l, 0, 0))],
        out_specs=pl.BlockSpec((1, N_DEV, n), lambda l: (l, 0, 0)),
        compiler_params=_cp("parallel"),
    )(c_all, w_ada, b_loc)


def _ada_bwd(c_all, dmod_all):
    n = dmod_all.shape[2]

    def body(c_ref, d_ref, o_ref):
        c = c_ref[...]
        o_ref[0] = _dot(c * _sigmoid(c), d_ref[0], TN)

    return pl.pallas_call(
        body, name="ada_bwd", grid=(DEPTH,),
        out_shape=jax.ShapeDtypeStruct((DEPTH, D, n), F32),
        in_specs=[pl.BlockSpec((N_DEV, D), lambda l: (0, 0)),
                  pl.BlockSpec((1, N_DEV, n), lambda l: (l, 0, 0))],
        out_specs=pl.BlockSpec((1, D, n), lambda l: (l, 0, 0)),
        compiler_params=_cp("parallel"),
    )(c_all, dmod_all)


ROW_TILE = 256


def _row_spec(w=D, col=0):
    return pl.BlockSpec((ROW_TILE, w), lambda i: (i, col))


def _vec_spec(w=D):
    return pl.BlockSpec((1, w), lambda i: (0, 0))


def _norm_fwd(x, g, scale, shift, name, m=None, gate=None, duty=None):
    has_res = m is not None

    def body(*refs):
        if has_res:
            x_ref, m_ref, gate_ref, g_ref, sc_ref, sh_ref, xo_ref, h_ref = refs
            xv = x_ref[...] + gate_ref[...] * m_ref[...]
            xo_ref[...] = xv
        else:
            x_ref, g_ref, sc_ref, sh_ref, h_ref = refs
            xv = x_ref[...]
        r = lax.rsqrt(_mean1(xv * xv) + EPS)
        h_ref[...] = ((xv * r * g_ref[...]) * (1.0 + sc_ref[...]) + sh_ref[...]).astype(BF16)

    ins = [x] + ([m, gate] if has_res else []) + [g, scale, shift]
    in_specs = [_row_spec()] + ([_row_spec(), _vec_spec()] if has_res else []) + [_vec_spec()] * 3
    out_shape = [jax.ShapeDtypeStruct((T, D), BF16)]
    out_specs = [_row_spec()]
    if has_res:
        out_shape = [jax.ShapeDtypeStruct((T, D), F32)] + out_shape
        out_specs = [_row_spec()] + out_specs
    out = _pcall(body, name=name, grid=(T // ROW_TILE,), out_shape=out_shape, in_specs=in_specs,
                 out_specs=out_specs, sem=("parallel",), args=ins, duty=duty)
    return out if has_res else out[0]


def _norm_bwd(x, dh, dres, g, scale, shift, name, duty=None):
    def body(x_ref, dh_ref, dres_ref, g_ref, sc_ref, sh_ref, dx_ref, st_ref):
        xv, dh_v, gv = x_ref[...], dh_ref[...], g_ref[...]
        r = lax.rsqrt(_mean1(xv * xv) + EPS)
        n = xv * r
        dy = dh_v * (1.0 + sc_ref[...])
        dn = dy * gv
        dx_ref[...] = r * (dn - n * _mean1(dn * n)) + dres_ref[...]

        @pl.when(pl.program_id(0) == 0)
        def _():
            st_ref[...] = jnp.zeros_like(st_ref)

        st_ref[0:1, :] += _sum0(dy * n)
        st_ref[1:2, :] += _sum0(dh_v * (n * gv))
        st_ref[2:3, :] += _sum0(dh_v)

    return _pcall(
        body, name=name, grid=(T // ROW_TILE,),
        out_shape=[jax.ShapeDtypeStruct((T, D), F32), jax.ShapeDtypeStruct((8, D), F32)],
        in_specs=[_row_spec(), _row_spec(), _row_spec(), _vec_spec(), _vec_spec(), _vec_spec()],
        out_specs=[_row_spec(), pl.BlockSpec((8, D), lambda i: (0, 0))],
        sem=("arbitrary",), args=(x, dh, dres, g, scale, shift), duty=duty)


def _axpy(x, m, gate, name, duty=None):
    def body(x_ref, m_ref, gate_ref, o_ref):
        o_ref[...] = x_ref[...] + gate_ref[...] * m_ref[...]

    return _pcall(
        body, name=name, grid=(T // ROW_TILE,), out_shape=jax.ShapeDtypeStruct((T, D), F32),
        in_specs=[_row_spec(), _row_spec(), _vec_spec()], out_specs=_row_spec(),
        sem=("parallel",), args=(x, m, gate), duty=duty)


def _gate_bwd(dx, m, gate, name, duty=None):
    def body(dx_ref, m_ref, gate_ref, dm_ref, st_ref):
        dxv = dx_ref[...]
        dm_ref[...] = (gate_ref[...] * dxv).astype(BF16)

        @pl.when(pl.program_id(0) == 0)
        def _():
            st_ref[...] = jnp.zeros_like(st_ref)

        st_ref[0:1, :] += _sum0(dxv * m_ref[...])

    return _pcall(
        body, name=name, grid=(T // ROW_TILE,),
        out_shape=[jax.ShapeDtypeStruct((T, D), BF16), jax.ShapeDtypeStruct((8, D), F32)],
        in_specs=[_row_spec(), _row_spec(), _vec_spec()],
        out_specs=[_row_spec(), pl.BlockSpec((8, D), lambda i: (0, 0))],
        sem=("arbitrary",), args=(dx, m, gate), duty=duty)


def _loss_fwd_bwd(y, target):
    def body(y_ref, t_ref, l_ref, d_ref):
        e = y_ref[...] - t_ref[...]
        d_ref[...] = e * (1.0 / D)

        @pl.when(pl.program_id(0) == 0)
        def _():
            l_ref[...] = jnp.zeros_like(l_ref)

        l_ref[...] += jnp.sum(_sum0(e * e), axis=1, keepdims=True) * (0.5 / D)

    return pl.pallas_call(
        body, name="loss", grid=(T // ROW_TILE,),
        out_shape=[jax.ShapeDtypeStruct((8, 128), F32), jax.ShapeDtypeStruct((T, D), F32)],
        in_specs=[_row_spec(), _row_spec()],
        out_specs=[pl.BlockSpec((8, 128), lambda i: (0, 0)), _row_spec()],
        compiler_params=_cp("arbitrary"))(y, target)


def _rope_tables():
    half = HD // 2
    inv_freq = 10000.0 ** (-jnp.arange(half, dtype=F32) / half)
    ang = jnp.arange(T, dtype=F32)[:, None] * inv_freq[None, :]
    cos, sin = jnp.cos(ang), jnp.sin(ang)
    return jnp.concatenate([cos, cos], axis=1), jnp.concatenate([-sin, sin], axis=1)


def _rope_fwd(p, cosf, sinf, duty=None):
    def body(p_ref, c_ref, s_ref, o_ref):
        cv, sv = c_ref[...], s_ref[...]
        for j in range(2 * N_HEADS):
            xv = p_ref[:, j * HD:(j + 1) * HD].astype(F32)
            rot = xv * cv + pltpu.roll(xv, HD // 2, 1) * sv
            if j >= N_HEADS:
                rot = rot * (HD ** -0.5)
            o_ref[:, j * HD:(j + 1) * HD] = rot.astype(BF16)

    return _pcall(
        body, name="rope_fwd", grid=(T // ROW_TILE,),
        out_shape=jax.ShapeDtypeStruct((T, 2 * BW), BF16),
        in_specs=[_row_spec(2 * BW), _row_spec(HD), _row_spec(HD)], out_specs=_row_spec(2 * BW),
        sem=("parallel",), args=(p, cosf, sinf), duty=duty)


def _rope_bwd(dq, dk, cosf, sinf, duty=None):
    def body(dq_ref, dk_ref, c_ref, s_ref, o_ref):
        cv, sv = c_ref[...], s_ref[...]
        for j in range(2 * N_HEADS):
            h = j % N_HEADS
            d = dq_ref[:, h * HD:(h + 1) * HD] if j < N_HEADS else dk_ref[:, h * HD:(h + 1) * HD] * (HD ** -0.5)
            o_ref[:, j * HD:(j + 1) * HD] = d * cv + pltpu.roll(d * sv, HD // 2, 1)

    return _pcall(
        body, name="rope_bwd", grid=(T // ROW_TILE,),
        out_shape=jax.ShapeDtypeStruct((T, 2 * BW), F32),
        in_specs=[_row_spec(BW), _row_spec(BW), _row_spec(HD), _row_spec(HD)], out_specs=_row_spec(2 * BW),
        sem=("parallel",), args=(dq, dk, cosf, sinf), duty=duty)


TQ = 256
V_RET_BLK = 8


def _ret_logg():
    lg = jnp.log1p(-jnp.exp2(-5.0 - jnp.arange(N_HEADS, dtype=F32)))
    return jnp.broadcast_to(lg[:, None, None], (N_HEADS, 1, 128))


def _block_iotas(i, kl):
    rows = lax.broadcasted_iota(jnp.int32, (TQ, kl), 0) + i * TQ
    cols = lax.broadcasted_iota(jnp.int32, (TQ, kl), 1)
    return rows, cols


def _ret_weight(lg_ref, i, kl):
    rows, cols = _block_iotas(i, kl)
    dist = jnp.abs(rows - cols).astype(F32)
    w = jnp.exp(dist * lg_ref[0][:, 0:1])
    return jnp.where((cols >> 6) <= (rows >> 6), w, 0.0)


def _per_query_block(i, fn):
    for n in range(1, T // TQ + 1):
        pl.when(i == n - 1)(functools.partial(fn, n * TQ))


def _ret_specs():
    q_spec = pl.BlockSpec((TQ, HD), lambda h, i: (i, h))
    k_spec = pl.BlockSpec((T, HD), lambda h, i: (0, N_HEADS + h))
    v_spec = pl.BlockSpec((T, HD), lambda h, i: (0, V_RET_BLK + h))
    lg_spec = pl.BlockSpec((1, 1, 128), lambda h, i: (h, 0, 0))
    return q_spec, k_spec, v_spec, lg_spec


def _ret_fwd(qk, p, logg, duty=None):
    def body(q_ref, k_ref, v_ref, lg_ref, o_ref):
        i = pl.program_id(1)

        def visible(kl):
            s = _dot(q_ref[...], k_ref[0:kl, :], NT) * _ret_weight(lg_ref, i, kl)
            o_ref[...] = _dot(s, v_ref[0:kl, :], NN)

        _per_query_block(i, visible)

    q_spec, k_spec, v_spec, lg_spec = _ret_specs()
    return _pcall(
        body, name="ret_fwd", grid=(N_HEADS, T // TQ),
        out_shape=jax.ShapeDtypeStruct((T, BW), F32),
        in_specs=[q_spec, k_spec, v_spec, lg_spec], out_specs=q_spec,
        sem=("parallel", "parallel"), args=(qk, qk, p, logg), duty=duty)


def _ret_bwd(qk, p, logg, do, duty=None):
    def body(q_ref, k_ref, v_ref, lg_ref, do_ref, dq_ref, dk_ref, dv_ref):
        i = pl.program_id(1)
        q, dov = q_ref[...], do_ref[...]

        @pl.when(i == 0)
        def _():
            dk_ref[...] = jnp.zeros_like(dk_ref)
            dv_ref[...] = jnp.zeros_like(dv_ref)

        def visible(kl):
            w = _ret_weight(lg_ref, i, kl)
            k = k_ref[0:kl, :]
            s = _dot(q, k, NT) * w
            ds = _dot(dov, v_ref[0:kl, :], NT) * w
            dk_ref[0:kl, :] += _dot(ds, q, TN)
            dv_ref[0:kl, :] += _dot(s, dov, TN)
            dq_ref[...] = _dot(ds, k, NN)

        _per_query_block(i, visible)

    q_spec, k_spec, v_spec, lg_spec = _ret_specs()
    acc_spec = pl.BlockSpec((T, HD), lambda h, i: (0, h))
    sh = jax.ShapeDtypeStruct((T, BW), F32)
    return _pcall(
        body, name="ret_bwd", grid=(N_HEADS, T // TQ),
        out_shape=[sh, sh, sh],
        in_specs=[q_spec, k_spec, v_spec, lg_spec, q_spec], out_specs=[q_spec, acc_spec, acc_spec],
        sem=("parallel", "arbitrary"), args=(qk, qk, p, logg, do), duty=duty)


def _post_norm(xv, gv, centered):
    if centered:
        xv = xv - _mean1(xv)
    r = lax.rsqrt(_mean1(xv * xv) + EPS)
    return xv * r, r


def _branch_post_fwd(raw, p, g, gate_blk, centered, name, duty=None):
    def body(raw_ref, z_ref, g_ref, o_ref):
        for h in range(N_HEADS):
            sl = slice(h * HD, (h + 1) * HD)
            gv = g_ref[:, sl] if centered else g_ref[...]
            xh, _ = _post_norm(raw_ref[:, sl], gv, centered)
            z = z_ref[:, sl].astype(F32)
            o_ref[:, sl] = (z * _sigmoid(z) * (xh * gv)).astype(BF16)

    return _pcall(
        body, name=name, grid=(T // ROW_TILE,),
        out_shape=jax.ShapeDtypeStruct((T, BW), BF16),
        in_specs=[_row_spec(BW), _row_spec(BW, gate_blk), _vec_spec(BW if centered else HD)],
        out_specs=_row_spec(BW), sem=("parallel",), args=(raw, p, g), duty=duty)


def _branch_post_bwd(raw, p, g, dout, gate_blk, centered, name, duty=None):
    gw = BW if centered else HD

    def body(raw_ref, z_ref, g_ref, do_ref, dr_ref, dz_ref, dg_ref):
        @pl.when(pl.program_id(0) == 0)
        def _():
            dg_ref[...] = jnp.zeros_like(dg_ref)

        for h in range(N_HEADS):
            sl = slice(h * HD, (h + 1) * HD)
            gsl = sl if centered else slice(0, HD)
            gv, z, dov = g_ref[:, gsl], z_ref[:, sl].astype(F32), do_ref[:, sl]
            xh, r = _post_norm(raw_ref[:, sl], gv, centered)
            sg = _sigmoid(z)
            dyn = dov * (z * sg)
            dz_ref[:, sl] = dov * (xh * gv) * (sg * (1.0 + z * (1.0 - sg)))
            dxh = dyn * gv
            t = dxh - xh * _mean1(dxh * xh)
            if centered:
                t = t - _mean1(dxh)
            dr_ref[:, sl] = r * t
            dg_ref[0:1, gsl] += _sum0(dyn * xh)

    return _pcall(
        body, name=name, grid=(T // ROW_TILE,),
        out_shape=[jax.ShapeDtypeStruct((T, BW), F32), jax.ShapeDtypeStruct((T, BW), F32),
                   jax.ShapeDtypeStruct((8, gw), F32)],
        in_specs=[_row_spec(BW), _row_spec(BW, gate_blk), _vec_spec(gw), _row_spec(BW)],
        out_specs=[_row_spec(BW), _row_spec(BW), pl.BlockSpec((8, gw), lambda i: (0, 0))],
        sem=("arbitrary",), args=(raw, p, g, dout), duty=duty)


GLA_ROWS = 256
GLA_CPB = GLA_ROWS // CHUNK
GLA_DK = 64
GLA_W = N_HEADS * GLA_DK
GQ_BLK, GK_BLK, GV_BLK, GG_BLK = 8, 9, 5, 6
TAIL_BLK = TAIL0 // 128
RG_BLK = 3


def _gla_chunk_common(tl, w2, bv, kv):
    pre = _dot(tl, w2, NN) + bv
    la = _log_sigmoid(pre) * (1.0 / 16.0)
    bc = _exact_dot(_tri(CHUNK, False), la)
    be = bc[CHUNK - 1:CHUNK, :]
    w = jnp.exp(be - bc)
    return pre, w, jnp.exp(be), kv * w


def _head_masks():
    lane = lax.broadcasted_iota(jnp.int32, (1, GLA_W), 1)
    return [jnp.where((lane // GLA_DK) == h, 1.0, 0.0) for h in range(N_HEADS)]


def _gla_fwd(p, tail, w2pad, b, duty=None):
    nb = T // GLA_ROWS

    def body(q_ref, k_ref, v_ref, t_ref, w2_ref, b_ref, o_ref, st_ref, s_acc):
        @pl.when(pl.program_id(0) == 0)
        def _():
            s_acc[...] = jnp.zeros_like(s_acc)

        masks = _head_masks()
        for c in range(GLA_CPB):
            rows = slice(c * CHUNK, (c + 1) * CHUNK)
            _, _, a, kd = _gla_chunk_common(t_ref[rows, :], w2_ref[...], b_ref[...], k_ref[rows, :].astype(F32))
            q = q_ref[rows, :].astype(F32) * (GLA_DK ** -0.5)
            kv = None
            for h in range(N_HEADS):
                t = _dot(v_ref[rows, h * HD:(h + 1) * HD], kd * masks[h], TN)
                kv = t if kv is None else kv + t
            s_new = s_acc[...] * a + kv
            s_acc[...] = s_new
            st_ref[c] = s_new
            for h in range(N_HEADS):
                o_ref[rows, h * HD:(h + 1) * HD] = _dot(q * masks[h], s_new, NT)

    return _pcall(
        body, name="gla_fwd", grid=(nb,),
        out_shape=[jax.ShapeDtypeStruct((T, BW), F32), jax.ShapeDtypeStruct((T // CHUNK, HD, GLA_W), F32)],
        in_specs=[pl.BlockSpec((GLA_ROWS, GLA_W), lambda i: (i, GQ_BLK)),
                  pl.BlockSpec((GLA_ROWS, GLA_W), lambda i: (i, GK_BLK)),
                  pl.BlockSpec((GLA_ROWS, BW), lambda i: (i, GV_BLK)),
                  pl.BlockSpec((GLA_ROWS, 128), lambda i: (i, 0)),
                  pl.BlockSpec((128, GLA_W), lambda i: (0, 0)),
                  pl.BlockSpec((1, GLA_W), lambda i: (0, 0))],
        out_specs=[pl.BlockSpec((GLA_ROWS, BW), lambda i: (i, 0)),
                   pl.BlockSpec((GLA_CPB, HD, GLA_W), lambda i: (i, 0, 0))],
        scratch_shapes=[pltpu.VMEM((HD, GLA_W), F32)],
        sem=("arbitrary",), args=(p, p, p, tail, w2pad, b), duty=duty)


def _gla_bwd(p, tail, w2pad, b, states, do, duty=None):
    nb = T // GLA_ROWS

    def body(q_ref, k_ref, v_ref, t_ref, w2_ref, b_ref, st_ref, prev_ref, do_ref,
             dq_ref, dk_ref, dv_ref, dt_ref, dw2_ref, db_ref, ds_acc):
        step = pl.program_id(0)

        @pl.when(step == 0)
        def _():
            ds_acc[...] = jnp.zeros_like(ds_acc)
            dw2_ref[...] = jnp.zeros_like(dw2_ref)
            db_ref[...] = jnp.zeros_like(db_ref)

        masks = _head_masks()
        up = _tri(CHUNK, True)
        has_prev = jnp.where(step == nb - 1, 0.0, 1.0)
        for c in reversed(range(GLA_CPB)):
            rows = slice(c * CHUNK, (c + 1) * CHUNK)
            tl, w2, k = t_ref[rows, :], w2_ref[...], k_ref[rows, :].astype(F32)
            pre, w, a, kd = _gla_chunk_common(tl, w2, b_ref[...], k)
            q = q_ref[rows, :].astype(F32) * (GLA_DK ** -0.5)
            s_n = st_ref[c]
            s_prev = st_ref[c - 1] if c > 0 else prev_ref[0] * has_prev
            ds = ds_acc[...]
            dos = [do_ref[rows, h * HD:(h + 1) * HD] for h in range(N_HEADS)]
            for h in range(N_HEADS):
                ds = ds + _dot(dos[h], q * masks[h], TN)
            dqp = jnp.zeros((CHUNK, GLA_W), F32)
            dkd = jnp.zeros((CHUNK, GLA_W), F32)
            for h in range(N_HEADS):
                dqp = dqp + masks[h] * _dot(dos[h], s_n, NN)
                dkd = dkd + masks[h] * _dot(v_ref[rows, h * HD:(h + 1) * HD], ds, NN)
                dv_ref[rows, h * HD:(h + 1) * HD] = _dot(kd * masks[h], ds, NT)
            dq_ref[rows, :] = dqp * (GLA_DK ** -0.5)
            dk_ref[rows, :] = dkd * w
            e = dkd * k * w
            dbe = _sum0(e) + _sum0(ds * s_prev) * a
            dla = dbe - _exact_dot(up, e)
            dpre = dla * (1.0 / 16.0) * _sigmoid(-pre)
            db_ref[0:1, :] += _sum0(dpre)
            dw2_ref[...] += _dot(tl, dpre, TN)
            dt_ref[rows, :] = _dot(dpre, w2, NT)
            ds_acc[...] = ds * a

    rev = lambda i: nb - 1 - i
    sh = lambda w: jax.ShapeDtypeStruct((T, w), F32)
    return _pcall(
        body, name="gla_bwd", grid=(nb,),
        out_shape=[sh(GLA_W), sh(GLA_W), sh(BW), sh(128), jax.ShapeDtypeStruct((128, GLA_W), F32),
                   jax.ShapeDtypeStruct((8, GLA_W), F32)],
        in_specs=[pl.BlockSpec((GLA_ROWS, GLA_W), lambda i: (rev(i), GQ_BLK)),
                  pl.BlockSpec((GLA_ROWS, GLA_W), lambda i: (rev(i), GK_BLK)),
                  pl.BlockSpec((GLA_ROWS, BW), lambda i: (rev(i), GV_BLK)),
                  pl.BlockSpec((GLA_ROWS, 128), lambda i: (rev(i), 0)),
                  pl.BlockSpec((128, GLA_W), lambda i: (0, 0)),
                  pl.BlockSpec((1, GLA_W), lambda i: (0, 0)),
                  pl.BlockSpec((GLA_CPB, HD, GLA_W), lambda i: (rev(i), 0, 0)),
                  pl.BlockSpec((1, HD, GLA_W), lambda i: (jnp.maximum(rev(i) * GLA_CPB - 1, 0), 0, 0)),
                  pl.BlockSpec((GLA_ROWS, BW), lambda i: (rev(i), 0))],
        out_specs=[pl.BlockSpec((GLA_ROWS, GLA_W), lambda i: (rev(i), 0)),
                   pl.BlockSpec((GLA_ROWS, GLA_W), lambda i: (rev(i), 0)),
                   pl.BlockSpec((GLA_ROWS, BW), lambda i: (rev(i), 0)),
                   pl.BlockSpec((GLA_ROWS, 128), lambda i: (rev(i), 0)),
                   pl.BlockSpec((128, GLA_W), lambda i: (0, 0)),
                   pl.BlockSpec((8, GLA_W), lambda i: (0, 0))],
        scratch_shapes=[pltpu.VMEM((HD, GLA_W), F32)],
        sem=("arbitrary",), args=(p, p, p, tail, w2pad, b, states, states, do), duty=duty)


FQ_BLK, FK_BLK = 7, 8
V_FOX_BLK = 36


def _fox_prep_fwd(p, tail, qg, kg, btail, duty=None):
    def body(q_ref, k_ref, t_ref, qg_ref, kg_ref, bt_ref, o_ref, cum_ref, carry):
        @pl.when(pl.program_id(0) == 0)
        def _():
            carry[...] = jnp.zeros_like(carry)

        for src, gr, off in ((q_ref, qg_ref, 0), (k_ref, kg_ref, BW)):
            for h in range(N_HEADS):
                xv = src[:, h * HD:(h + 1) * HD].astype(F32)
                r = lax.rsqrt(_mean1(xv * xv) + EPS)
                o_ref[:, off + h * HD:off + (h + 1) * HD] = (xv * r * gr[...]).astype(BF16)
        logf = _log_sigmoid(t_ref[...] + bt_ref[...])
        cum = _exact_dot(_tri(ROW_TILE, False), logf) + carry[...]
        cum_ref[...] = cum
        carry[...] = cum[ROW_TILE - 1:ROW_TILE, :]

    return _pcall(
        body, name="fox_prep_fwd", grid=(T // ROW_TILE,),
        out_shape=[jax.ShapeDtypeStruct((T, 2 * BW), BF16), jax.ShapeDtypeStruct((T, 128), F32)],
        in_specs=[_row_spec(BW, FQ_BLK), _row_spec(BW, FK_BLK), _row_spec(128),
                  _vec_spec(HD), _vec_spec(HD), _vec_spec(128)],
        out_specs=[_row_spec(2 * BW), _row_spec(128)],
        scratch_shapes=[pltpu.VMEM((1, 128), F32)],
        sem=("arbitrary",), args=(p, p, tail, qg, kg, btail), duty=duty)


def _fox_prep_bwd(p, tail, qg, kg, btail, dqn, dkn, dcum, duty=None):
    nt = T // ROW_TILE

    def body(q_ref, k_ref, t_ref, qg_ref, kg_ref, bt_ref, dq_ref, dk_ref, dc_ref, o_ref, dt_ref, st_ref, carry):
        @pl.when(pl.program_id(0) == 0)
        def _():
            carry[...] = jnp.zeros_like(carry)
            st_ref[...] = jnp.zeros_like(st_ref)

        for row, (src, gr, dsrc, off) in enumerate(((q_ref, qg_ref, dq_ref, 0), (k_ref, kg_ref, dk_ref, BW))):
            for h in range(N_HEADS):
                xv = src[:, h * HD:(h + 1) * HD].astype(F32)
                dy = dsrc[:, h * HD:(h + 1) * HD]
                r = lax.rsqrt(_mean1(xv * xv) + EPS)
                n = xv * r
                dn = dy * gr[...]
                o_ref[:, off + h * HD:off + (h + 1) * HD] = r * (dn - n * _mean1(dn * n))
                st_ref[row:row + 1, :] += _sum0(dy * n)
        z = t_ref[...] + bt_ref[...]
        dlogf = _exact_dot(_tri(ROW_TILE, True), dc_ref[...]) + carry[...]
        carry[...] = dlogf[0:1, :]
        lane = lax.broadcasted_iota(jnp.int32, (1, 128), 1)
        keep = (lane >= FF_LANE0) & (lane < FF_LANE0 + N_HEADS)
        dz = jnp.where(keep, dlogf * _sigmoid(-z), 0.0)
        dt_ref[...] = dz
        st_ref[2:3, :] += _sum0(dz)

    rs = lambda w, col=0: pl.BlockSpec((ROW_TILE, w), lambda i: (nt - 1 - i, col))
    return _pcall(
        body, name="fox_prep_bwd", grid=(nt,),
        out_shape=[jax.ShapeDtypeStruct((T, 2 * BW), F32), jax.ShapeDtypeStruct((T, 128), F32),
                   jax.ShapeDtypeStruct((8, 128), F32)],
        in_specs=[rs(BW, FQ_BLK), rs(BW, FK_BLK), rs(128), _vec_spec(HD), _vec_spec(HD), _vec_spec(128),
                  rs(BW), rs(BW), rs(128)],
        out_specs=[rs(2 * BW), rs(128), pl.BlockSpec((8, 128), lambda i: (0, 0))],
        scratch_shapes=[pltpu.VMEM((1, 128), F32)],
        sem=("arbitrary",), args=(p, p, tail, qg, kg, btail, dqn, dkn, dcum), duty=duty)


def _fox_logits(q_ref, k_ref, cc_ref, cr_ref, i, kl):
    rows, cols = _block_iotas(i, kl)
    s = _dot(q_ref[...], k_ref[0:kl, :], NT) * (HD ** -0.5) + cc_ref[0] - cr_ref[0, :, 0:kl]
    return jnp.where(cols <= rows, s, -1e30)


def _fox_specs():
    q_spec = pl.BlockSpec((TQ, HD), lambda h, i: (i, h))
    k_spec = pl.BlockSpec((T, HD), lambda h, i: (0, N_HEADS + h))
    v_spec = pl.BlockSpec((T, HD), lambda h, i: (0, V_FOX_BLK + h))
    col_spec = pl.BlockSpec((1, TQ, 1), lambda h, i: (h, i, 0))
    row_spec = pl.BlockSpec((1, 1, T), lambda h, i: (h, 0, 0))
    return q_spec, k_spec, v_spec, col_spec, row_spec


def _fox_fwd(qkn, p, cumcol, cumrow, duty=None):
    def body(q_ref, k_ref, v_ref, cc_ref, cr_ref, o_ref, lse_ref):
        i = pl.program_id(1)

        def visible(kl):
            s = _fox_logits(q_ref, k_ref, cc_ref, cr_ref, i, kl)
            m = jnp.max(s, axis=-1, keepdims=True)
            e = jnp.exp(s - m)
            l = jnp.sum(e, axis=-1, keepdims=True)
            o_ref[...] = _dot(e / l, v_ref[0:kl, :], NN)
            lse_ref[0] = m + jnp.log(l)

        _per_query_block(i, visible)

    q_spec, k_spec, v_spec, col_spec, row_spec = _fox_specs()
    return _pcall(
        body, name="fox_fwd", grid=(N_HEADS, T // TQ),
        out_shape=[jax.ShapeDtypeStruct((T, BW), F32), jax.ShapeDtypeStruct((N_HEADS, T, 1), F32)],
        in_specs=[q_spec, k_spec, v_spec, col_spec, row_spec], out_specs=[q_spec, col_spec],
        sem=("parallel", "parallel"), args=(qkn, qkn, p, cumcol, cumrow), duty=duty)


def _fox_bwd(qkn, p, cumcol, cumrow, lse, o, do, duty=None):
    def body(q_ref, k_ref, v_ref, cc_ref, cr_ref, lse_ref, o_ref, do_ref, dq_ref, dk_ref, dv_ref, dr_ref, dc_ref):
        i = pl.program_id(1)
        @pl.when(i == 0)
        def _():
            dk_ref[...] = jnp.zeros_like(dk_ref)
            dv_ref[...] = jnp.zeros_like(dv_ref)
            dc_ref[...] = jnp.zeros_like(dc_ref)

        def visible(kl):
            q, dov = q_ref[...], do_ref[...]
            pm = jnp.exp(_fox_logits(q_ref, k_ref, cc_ref, cr_ref, i, kl) - lse_ref[0])
            delta = jnp.sum(o_ref[...] * dov, axis=-1, keepdims=True)
            ds = pm * (_dot(dov, v_ref[0:kl, :], NT) - delta)
            dq_ref[...] = _dot(ds, k_ref[0:kl, :], NN) * (HD ** -0.5)
            dr_ref[0] = jnp.sum(ds, axis=-1, keepdims=True)
            dk_ref[0:kl, :] += _dot(ds, q, TN) * (HD ** -0.5)
            dv_ref[0:kl, :] += _dot(pm, dov, TN)
            dc_ref[0, :, 0:kl] += _sum0(ds)

        _per_query_block(i, visible)

    q_spec, k_spec, v_spec, col_spec, row_spec = _fox_specs()
    acc_spec = pl.BlockSpec((T, HD), lambda h, i: (0, h))
    sh = jax.ShapeDtypeStruct((T, BW), F32)
    return _pcall(
        body, name="fox_bwd", grid=(N_HEADS, T // TQ),
        out_shape=[sh, sh, sh, jax.ShapeDtypeStruct((N_HEADS, T, 1), F32), jax.ShapeDtypeStruct((N_HEADS, 1, T), F32)],
        in_specs=[q_spec, k_spec, v_spec, col_spec, row_spec, col_spec, q_spec, q_spec],
        out_specs=[q_spec, acc_spec, acc_spec, col_spec, row_spec],
        sem=("parallel", "arbitrary"), args=(qkn, qkn, p, cumcol, cumrow, lse, o, do), duty=duty)


def _mix_fwd(gpre, b_mg, y0, y1, y2, duty=None):
    def body(g_ref, b_ref, y0_ref, y1_ref, y2_ref, o_ref):
        acc = None
        for n, y_ref in enumerate((y0_ref, y1_ref, y2_ref)):
            sl = slice(n * D, (n + 1) * D)
            t = _sigmoid(g_ref[:, sl].astype(F32) + b_ref[:, sl]) * y_ref[...].astype(F32)
            acc = t if acc is None else acc + t
        o_ref[...] = acc.astype(BF16)

    return _pcall(
        body, name="mix_fwd", grid=(T // ROW_TILE,), out_shape=jax.ShapeDtypeStruct((T, D), BF16),
        in_specs=[_row_spec(3 * D), _vec_spec(3 * D), _row_spec(), _row_spec(), _row_spec()],
        out_specs=_row_spec(), sem=("parallel",), args=(gpre, b_mg, y0, y1, y2), duty=duty)


def _mix_bwd(gpre, b_mg, y0, y1, y2, dmi, duty=None):
    def body(g_ref, b_ref, y0_ref, y1_ref, y2_ref, d_ref, dy0_ref, dy1_ref, dy2_ref, dg_ref, db_ref):
        @pl.when(pl.program_id(0) == 0)
        def _():
            db_ref[...] = jnp.zeros_like(db_ref)

        dv = d_ref[...]
        for n, (y_ref, dy_ref) in enumerate(((y0_ref, dy0_ref), (y1_ref, dy1_ref), (y2_ref, dy2_ref))):
            sl = slice(n * D, (n + 1) * D)
            sg = _sigmoid(g_ref[:, sl].astype(F32) + b_ref[:, sl])
            dy_ref[...] = (dv * sg).astype(BF16)
            dpre = dv * y_ref[...].astype(F32) * (sg * (1.0 - sg))
            dg_ref[:, sl] = dpre.astype(BF16)
            db_ref[0:1, sl] += _sum0(dpre)

    shb = jax.ShapeDtypeStruct((T, D), BF16)
    return _pcall(
        body, name="mix_bwd", grid=(T // ROW_TILE,),
        out_shape=[shb, shb, shb, jax.ShapeDtypeStruct((T, 3 * D), BF16), jax.ShapeDtypeStruct((8, 3 * D), F32)],
        in_specs=[_row_spec(3 * D), _vec_spec(3 * D), _row_spec(), _row_spec(), _row_spec(), _row_spec()],
        out_specs=[_row_spec(), _row_spec(), _row_spec(), _row_spec(3 * D), pl.BlockSpec((8, 3 * D), lambda i: (0, 0))],
        sem=("arbitrary",), args=(gpre, b_mg, y0, y1, y2, dmi), duty=duty)


FF_COLS = 256
FF_NBLK = D_FF // FF_COLS


def _shift_rows(a, n):
    rows = lax.broadcasted_iota(jnp.int32, a.shape, 0)
    rolled = pltpu.roll(a, n % T, 0)
    return jnp.where((rows >= n) if n > 0 else (rows < T + n), rolled, 0.0)


def _ffn_act_fwd(uu, w_conv, b_conv, duty=None):
    def body(u_ref, g_ref, w_ref, b_ref, o_ref):
        u = u_ref[...].astype(F32)
        w = w_ref[...]
        uc = b_ref[...] + w[0:1, :] * _shift_rows(u, 2) + w[1:2, :] * _shift_rows(u, 1) + w[2:3, :] * u
        o_ref[...] = (uc * _sigmoid(uc) * g_ref[...].astype(F32)).astype(BF16)

    return _pcall(
        body, name="ffn_act_fwd", grid=(FF_NBLK,), out_shape=jax.ShapeDtypeStruct((T, D_FF), BF16),
        in_specs=[pl.BlockSpec((T, FF_COLS), lambda j: (0, j)), pl.BlockSpec((T, FF_COLS), lambda j: (0, FF_NBLK + j)),
                  pl.BlockSpec((3, FF_COLS), lambda j: (0, j)), pl.BlockSpec((1, FF_COLS), lambda j: (0, j))],
        out_specs=pl.BlockSpec((T, FF_COLS), lambda j: (0, j)),
        sem=("parallel",), args=(uu, uu, w_conv, b_conv), duty=duty)


def _ffn_act_bwd(uu, w_conv, b_conv, da, duty=None):
    def body(u_ref, g_ref, w_ref, b_ref, da_ref, d_ref, st_ref):
        u, w, dav = u_ref[...].astype(F32), w_ref[...], da_ref[...]
        u1, u2 = _shift_rows(u, 1), _shift_rows(u, 2)
        uc = b_ref[...] + w[0:1, :] * u2 + w[1:2, :] * u1 + w[2:3, :] * u
        sg = _sigmoid(uc)
        d_ref[1] = (dav * (uc * sg)).astype(BF16)
        duc = dav * g_ref[...].astype(F32) * (sg * (1.0 + uc * (1.0 - sg)))
        du = w[2:3, :] * duc + w[1:2, :] * _shift_rows(duc, -1) + w[0:1, :] * _shift_rows(duc, -2)
        d_ref[0] = du.astype(BF16)
        st_ref[...] = jnp.zeros_like(st_ref)
        st_ref[0:1, :] = _sum0(duc * u2)
        st_ref[1:2, :] = _sum0(duc * u1)
        st_ref[2:3, :] = _sum0(duc * u)
        st_ref[3:4, :] = _sum0(duc)

    cb = lambda rows=T, off=0: pl.BlockSpec((rows, FF_COLS), lambda j: (0, off + j))
    return _pcall(
        body, name="ffn_act_bwd", grid=(FF_NBLK,),
        out_shape=[jax.ShapeDtypeStruct((2, T, D_FF), BF16), jax.ShapeDtypeStruct((8, D_FF), F32)],
        in_specs=[cb(), cb(T, FF_NBLK), cb(3), cb(1), cb()],
        out_specs=[pl.BlockSpec((2, T, FF_COLS), lambda j: (0, 0, j)), cb(8)],
        sem=("parallel",), args=(uu, uu, w_conv, b_conv, da), duty=duty)


def _adamw(g, w, m, v, tr, name, tc=None):
    partial = g.ndim == 3
    R, C = w.shape
    tr = R if tr is None else tr
    tc = C if tc is None else tc
    assert R % tr == 0 and C % tc == 0 and (tr == R or tc == C)

    def body(g_ref, w_ref, m_ref, v_ref, go_ref, d_ref, mo_ref, vo_ref):
        if partial:
            gv = g_ref[0].astype(F32)
            for j in range(1, g.shape[0]):
                gv = gv + g_ref[j].astype(F32)
        else:
            gv = g_ref[...]
        go_ref[...] = gv
        mn = ADAM_B1 * m_ref[...] + (1.0 - ADAM_B1) * gv
        vn = ADAM_B2 * v_ref[...] + (1.0 - ADAM_B2) * (gv * gv)
        mo_ref[...] = mn
        vo_ref[...] = vn
        m_hat = mn / (1.0 - ADAM_B1 ** ADAM_STEP)
        v_hat = vn / (1.0 - ADAM_B2 ** ADAM_STEP)
        d_ref[...] = -ADAM_LR * (m_hat / (jnp.sqrt(v_hat) + ADAM_EPS) + ADAM_WD * w_ref[...])

    by_rows = tc == C
    spec = pl.BlockSpec((tr, tc), (lambda i: (i, 0)) if by_rows else (lambda i: (0, i)))
    g_spec = pl.BlockSpec((g.shape[0], tr, tc), (lambda i: (0, i, 0)) if by_rows else (lambda i: (0, 0, i))) if partial else spec
    sh = jax.ShapeDtypeStruct((R, C), F32)
    return pl.pallas_call(
        body, name=name, grid=(R // tr if by_rows else C // tc,), out_shape=[sh, sh, sh, sh],
        in_specs=[g_spec, spec, spec, spec], out_specs=[spec, spec, spec, spec],
        compiler_params=_cp("parallel"))(g, w, m, v)


def _chip_sum(dw, stage, name):
    n_chip, n, C = stage.shape

    def body(d_ref, s_ref, o_ref):
        mc = lax.axis_index("c")
        mine = jnp.where(mc == 0, d_ref[0, 0].astype(F32), d_ref[0, 1].astype(F32))
        o_ref[0] = (mine + s_ref[0].astype(F32)).astype(BF16)

    return pl.pallas_call(
        body, name=name, grid=(n_chip,), out_shape=jax.ShapeDtypeStruct(stage.shape, BF16),
        in_specs=[pl.BlockSpec((1, 2, n, C), lambda p: (p, 0, 0, 0)), pl.BlockSpec((1, n, C), lambda p: (p, 0, 0))],
        out_specs=pl.BlockSpec((1, n, C), lambda p: (p, 0, 0)), compiler_params=_cp("parallel"),
    )(dw.reshape(n_chip, 2, n, C), stage)


def _sum_partials(g, name, tr=None):
    n_part, R, C = g.shape
    tr = R if tr is None else tr
    assert R % tr == 0

    def body(g_ref, o_ref):
        acc = g_ref[0].astype(F32)
        for j in range(1, n_part):
            acc = acc + g_ref[j].astype(F32)
        o_ref[...] = acc

    return pl.pallas_call(
        body, name=name, grid=(R // tr,), out_shape=jax.ShapeDtypeStruct((R, C), F32),
        in_specs=[pl.BlockSpec((n_part, tr, C), lambda i: (0, i, 0))], out_specs=pl.BlockSpec((tr, C), lambda i: (i, 0)),
        compiler_params=_cp("parallel"))(g)


def _permute_in(w):
    pad = jnp.zeros(w.shape[:-1] + (NP - IN_W,), w.dtype)
    return jnp.concatenate([w[..., :3072], w[..., 3088:5136], w[..., 3072:3088], w[..., 5136:5140], pad], axis=-1)


def _unpermute_in(w):
    return jnp.concatenate([w[..., :3072], w[..., 5120:5136], w[..., 3072:5120], w[..., 5136:5140]], axis=-1)


def _flat_pack(arrs):
    flat = jnp.concatenate([a.reshape(-1).astype(F32) for a in arrs])
    n = flat.shape[0]
    rows = -(-n // 1024) * 8
    return jnp.pad(flat, (0, rows * 128 - n)).reshape(rows, 128)


def _flat_unpack(buf, shapes):
    flat = buf.reshape(-1)
    out, off = [], 0
    for s in shapes:
        n = int(np.prod(s))
        out.append(flat[off:off + n].reshape(s))
        off += n
    return out


GRAD_CHUNKS = dict(w_in=(128, 8), w_o=(128, 1), w_down=(352, 2), w_br0=(128, 1), w_br1=(128, 1), w_br2=(128, 1),
                   w_mg=(384, 4), w_up=(704, 11))
GRAD_VIA_CHIP = ("w_in", "w_down", "w_mg", "w_up")


def _send_grad(xfer, layer, k, g):
    if xfer is None:
        return g
    n, parts = GRAD_CHUNKS[k]
    row_bytes = g.shape[1] * 2
    if k not in GRAD_VIA_CHIP:
        for c in range(parts):
            xfer.add(k, ("a2a", 0, n, (layer,), c * (n // parts), n // parts), g, nbytes=n // parts * row_bytes)
        return g
    stage = ("stage", layer, k)
    xfer.lands[stage] = lax.empty((N_DEV // 2, n, g.shape[1]), BF16)
    xfer.add(stage, ("to_other_core", 0, n, (), 0, n), g, group=stage, nbytes=n * row_bytes)

    def both_halves_here():
        chip = _chip_sum(g, xfer.lands[stage], "chip_sum_" + k)
        for c in range(parts):
            xfer.add(k, ("a2a_chip", 0, n, (layer,), c * (n // parts), n // parts), chip, nbytes=n // parts * row_bytes)

    xfer.when_done(stage, both_halves_here)
    return g


def _weight(wl, k):
    return wl[k]() if callable(wl[k]) else wl[k]


MIN_CARRIER_US = 19.0


def _taker(xfer, fill=1.0):
    if xfer is None:
        return lambda us: None
    return lambda us: xfer.take_for(us * fill) if us >= MIN_CARRIER_US else None


def _layer_fwd(x0, wl, consts, xfer=None, fill=1.25):
    cosf, sinf, logg = consts
    row = lambda a: a.reshape(1, -1)
    take = _taker(xfer, fill)
    h = _norm_fwd(x0, row(wl["norm1_g"]), row(wl["scale1"]), row(wl["shift1"]), "norm1_fwd", duty=take(9))
    w_in = _weight(wl, "w_in")
    p = _matmul(h, w_in, "nn", "in_proj", out_dtype=BF16, n_blocks=(0, TAIL0 // 512), duty=take(30))
    tail = _matmul(h, w_in, "nn", "in_tail", tn=128, n_blocks=(TAIL_BLK, 1))
    qk = _rope_fwd(p, cosf, sinf, duty=take(10))
    ret_raw = _ret_fwd(qk, p, logg, duty=take(27))
    br0 = _branch_post_fwd(ret_raw, p, row(wl["ret_norm_g"]), RG_BLK, True, "ret_post_fwd", duty=take(9))
    gla_raw, states = _gla_fwd(p, tail, wl["w2pad"], row(wl["b_gla_a"]), duty=take(26))
    br1 = _branch_post_fwd(gla_raw, p, row(wl["gla_norm_g"]), GG_BLK, False, "gla_post_fwd", duty=take(9))
    qkn, cum = _fox_prep_fwd(p, tail, row(wl["q_norm_g"]), row(wl["k_norm_g"]), row(wl["btail"]), duty=take(10))
    cum4 = cum[:, FF_LANE0:FF_LANE0 + N_HEADS].T
    cumcol, cumrow = cum4.reshape(N_HEADS, T, 1), cum4.reshape(N_HEADS, 1, T)
    fox_o, lse = _fox_fwd(qkn, p, cumcol, cumrow, duty=take(30))
    w_br_t = _weight(wl, "w_br_t")
    ys = [_matmul(b, w_br_t[n], "nt", "br_proj%d" % n, out_dtype=BF16) for n, b in enumerate((br0, br1, fox_o))]
    gpre = _matmul(h, _weight(wl, "w_mg_t"), "nt", "gate_proj", out_dtype=BF16, duty=take(20))
    mixed_in = _mix_fwd(gpre, row(wl["b_mg"]), *ys, duty=take(21))
    mixed = _matmul(mixed_in, _weight(wl, "w_o"), "nn", "o_proj", duty=take(10))
    x1, h2 = _norm_fwd(x0, row(wl["norm2_g"]), row(wl["scale2"]), row(wl["shift2"]), "norm2_fwd",
                       m=mixed, gate=row(wl["gate1"]), duty=take(13))
    uu = _matmul(h2, _weight(wl, "w_up_t"), "nt", "up_proj", out_dtype=BF16, duty=take(30))
    act = _ffn_act_fwd(uu, wl["w_conv"], row(wl["b_conv"]), duty=take(25))
    y = _matmul(act, _weight(wl, "w_down"), "nn", "down_proj", tk=1408, duty=take(24))
    x2 = _axpy(x1, y, row(wl["gate2"]), "resid2", duty=take(11))
    saved = dict(x0=x0, h=h, p=p, tail=tail, qk=qk, ret_raw=ret_raw, br0=br0, gla_raw=gla_raw, states=states, br1=br1,
                 qkn=qkn, cumcol=cumcol, cumrow=cumrow, fox_o=fox_o, lse=lse, y0=ys[0], y1=ys[1], y2=ys[2],
                 gpre=gpre, mixed_in=mixed_in, mixed=mixed, x1=x1, h2=h2, uu=uu, act=act, y=y)
    return x2, saved


def _layer_bwd(dx2, wl, sv, consts, xfer=None, layer=0):
    cosf, sinf, logg = consts
    row = lambda a: a.reshape(1, -1)
    take = _taker(xfer)

    send = functools.partial(_send_grad, xfer, layer)

    dy, st_g2 = _gate_bwd(dx2, sv["y"], row(wl["gate2"]), "gate2_bwd", duty=take(10))
    dact = _matmul(dy, _weight(wl, "w_down"), "nt", "down_dx", tn=1408, duty=take(21))
    d_down = send("w_down", _matmul(sv["act"], dy, "tn", "down_dw", out_dtype=BF16, tm=1408, duty=take(19)))
    duu, st_conv = _ffn_act_bwd(sv["uu"], wl["w_conv"], row(wl["b_conv"]), dact, duty=take(40))
    dh2 = _matmul(duu, _weight(wl, "w_up_t"), "nn", "up_dx", tk=1408, duty=take(42))
    d_up_t = send("w_up", _matmul(duu, sv["h2"], "tn", "up_dw", out_dtype=BF16, tm=1408, duty=take(33)))
    dx1, st_n2 = _norm_bwd(sv["x1"], dh2, dx2, row(wl["norm2_g"]), row(wl["scale2"]), row(wl["shift2"]), "norm2_bwd",
                           duty=take(15))
    dmixed, st_g1 = _gate_bwd(dx1, sv["mixed"], row(wl["gate1"]), "gate1_bwd", duty=take(10))
    dmi = _matmul(dmixed, _weight(wl, "w_o"), "nt", "o_dx", duty=take(11))
    d_o = send("w_o", _matmul(sv["mixed_in"], dmixed, "tn", "o_dw", out_dtype=BF16, duty=take(9)))
    dy0, dy1, dy2, dgpre, st_bmg = _mix_bwd(sv["gpre"], row(wl["b_mg"]), sv["y0"], sv["y1"], sv["y2"], dmi,
                                             duty=take(31))
    brs = (sv["br0"], sv["br1"], sv["fox_o"])
    w_br_t = _weight(wl, "w_br_t")
    dbr = [_matmul(d, w_br_t[n], "nn", "br_dx%d" % n) for n, d in enumerate((dy0, dy1, dy2))]
    d_br_t = [send("w_br%d" % n, _matmul(d, brs[n], "tn", "br_dw%d" % n, out_dtype=BF16))
              for n, d in enumerate((dy0, dy1, dy2))]
    dh = _matmul(dgpre, _weight(wl, "w_mg_t"), "nn", "gate_dx", tk=1024, duty=take(29))
    d_mg_t = send("w_mg", _matmul(dgpre, sv["h"], "tn", "gate_dw", out_dtype=BF16, duty=take(21)))
    p, tail = sv["p"], sv["tail"]
    dqn, dkn, dfv, drow, dcol = _fox_bwd(sv["qkn"], p, sv["cumcol"], sv["cumrow"], sv["lse"], sv["fox_o"], dbr[2],
                                         duty=take(50))
    dcum4 = drow.reshape(N_HEADS, T) - dcol.reshape(N_HEADS, T)
    dcum = jnp.pad(dcum4.T, ((0, 0), (FF_LANE0, 128 - FF_LANE0 - N_HEADS)))
    dfqk, dtail_fox, st_fox = _fox_prep_bwd(p, tail, row(wl["q_norm_g"]), row(wl["k_norm_g"]), row(wl["btail"]), dqn, dkn, dcum,
                                            duty=take(15))
    dgla_raw, dgg, st_gn = _branch_post_bwd(sv["gla_raw"], p, row(wl["gla_norm_g"]), dbr[1], GG_BLK, False, "gla_post_bwd",
                                            duty=take(12))
    dgq, dgk, dgv, dtail_gla, dw2pad, st_bg = _gla_bwd(p, tail, wl["w2pad"], row(wl["b_gla_a"]), sv["states"], dgla_raw,
                                                       duty=take(30))
    dret_raw, drg, st_rn = _branch_post_bwd(sv["ret_raw"], p, row(wl["ret_norm_g"]), dbr[0], RG_BLK, True, "ret_post_bwd",
                                            duty=take(13))
    dqr, dkr, drv = _ret_bwd(sv["qk"], p, logg, dret_raw, duty=take(50))
    drqk = _rope_bwd(dqr, dkr, cosf, sinf, duty=take(11))
    dp = jnp.concatenate([a.astype(BF16) for a in (drqk, drv, drg, dgq, dgk, dgv, dgg, dfqk, dfv, dtail_fox + dtail_gla)]
                         + [jnp.zeros((T, NP - TAIL0 - 128), BF16)], axis=1)
    dh = _matmul(dp, _weight(wl, "w_in"), "nt", "in_dx", tk=1408, add=dh, duty=take(45))
    d_in = send("w_in", _matmul(sv["h"], dp, "tn", "in_dw", out_dtype=BF16, duty=take(32)))
    dx0, st_n1 = _norm_bwd(sv["x0"], dh, dx1, row(wl["norm1_g"]), row(wl["scale1"]), row(wl["shift1"]), "norm1_bwd",
                           duty=take(15))
    big = dict(w_in=d_in, w_o=d_o, w_down=d_down, w_br0=d_br_t[0], w_br1=d_br_t[1], w_br2=d_br_t[2], w_mg=d_mg_t,
               w_up=d_up_t)
    dmod = jnp.concatenate([st_n1[2], st_n1[1], st_g1[0], st_n2[2], st_n2[1], st_g2[0]])
    small = dict(norm1_g=st_n1[0], norm2_g=st_n2[0], b_gla_a=st_bg[0], b_fox_f=st_fox[2, FF_LANE0:FF_LANE0 + N_HEADS],
                 ret_norm_g=st_rn[0], gla_norm_g=st_gn[0], q_norm_g=st_fox[0], k_norm_g=st_fox[1], b_mg=st_bmg[0],
                 b_conv=st_conv[3], w_gla_a2=dw2pad[:LR_LANES], w_conv=st_conv[0:3])
    return dx0, big, dmod, small


SMALL_REPL = ("norm1_g", "norm2_g", "b_ada", "b_gla_a", "b_fox_f", "ret_norm_g", "gla_norm_g", "q_norm_g", "k_norm_g",
              "b_mg", "b_conv")
SMALL_SHARDED = ("w_gla_a2", "w_conv")
BIG = ("w_in", "w_o", "w_down", "w_br", "w_mg", "w_up")
WEIGHTS = ("norm1_g", "norm2_g", "w_ada", "b_ada", "w_in", "w_gla_a2", "b_gla_a", "b_fox_f", "ret_norm_g", "gla_norm_g",
           "q_norm_g", "k_norm_g", "w_br", "w_mg", "b_mg", "w_o", "w_up", "w_conv", "b_conv", "w_down")


def kernel(x, c, norm1_g, norm2_g, w_ada, b_ada, w_in, w_gla_a2, b_gla_a, b_fox_f, ret_norm_g, gla_norm_g, q_norm_g, k_norm_g, w_br, w_mg, b_mg, w_o, w_up, w_conv, b_conv, w_down, loss_target, m_norm1_g, m_norm2_g, m_w_ada, m_b_ada, m_w_in, m_w_gla_a2, m_b_gla_a, m_b_fox_f, m_ret_norm_g, m_gla_norm_g, m_q_norm_g, m_k_norm_g, m_w_br, m_w_mg, m_b_mg, m_w_o, m_w_up, m_w_conv, m_b_conv, m_w_down, v_norm1_g, v_norm2_g, v_w_ada, v_b_ada, v_w_in, v_w_gla_a2, v_b_gla_a, v_b_fox_f, v_ret_norm_g, v_gla_norm_g, v_q_norm_g, v_k_norm_g, v_w_br, v_w_mg, v_b_mg, v_w_o, v_w_up, v_w_conv, v_b_conv, v_w_down):
    W = dict(norm1_g=norm1_g, norm2_g=norm2_g, w_ada=w_ada, b_ada=b_ada, w_in=w_in, w_gla_a2=w_gla_a2, b_gla_a=b_gla_a,
             b_fox_f=b_fox_f, ret_norm_g=ret_norm_g, gla_norm_g=gla_norm_g, q_norm_g=q_norm_g, k_norm_g=k_norm_g,
             w_br=w_br, w_mg=w_mg, b_mg=b_mg, w_o=w_o, w_up=w_up, w_conv=w_conv, b_conv=b_conv, w_down=w_down)
    M = dict(norm1_g=m_norm1_g, norm2_g=m_norm2_g, w_ada=m_w_ada, b_ada=m_b_ada, w_in=m_w_in, w_gla_a2=m_w_gla_a2,
             b_gla_a=m_b_gla_a, b_fox_f=m_b_fox_f, ret_norm_g=m_ret_norm_g, gla_norm_g=m_gla_norm_g, q_norm_g=m_q_norm_g,
             k_norm_g=m_k_norm_g, w_br=m_w_br, w_mg=m_w_mg, b_mg=m_b_mg, w_o=m_w_o, w_up=m_w_up, w_conv=m_w_conv,
             b_conv=m_b_conv, w_down=m_w_down)
    V = dict(norm1_g=v_norm1_g, norm2_g=v_norm2_g, w_ada=v_w_ada, b_ada=v_b_ada, w_in=v_w_in, w_gla_a2=v_w_gla_a2,
             b_gla_a=v_b_gla_a, b_fox_f=v_b_fox_f, ret_norm_g=v_ret_norm_g, gla_norm_g=v_gla_norm_g, q_norm_g=v_q_norm_g,
             k_norm_g=v_k_norm_g, w_br=v_w_br, w_mg=v_w_mg, b_mg=v_b_mg, w_o=v_w_o, w_up=v_w_up, w_conv=v_w_conv,
             b_conv=v_b_conv, w_down=v_w_down)
    me = 4 * lax.axis_index("x") + 2 * lax.axis_index("y") + lax.axis_index("c")
    x2d, tgt = x.reshape(T, D), loss_target.reshape(T, D)

    sm = _flat_pack([c, w_gla_a2, w_conv])
    sm_all = _exchange(sm, True, "gather_small")
    parts = [_flat_unpack(sm_all[j], [(D,), (DEPTH, LR_LANES, 32), (DEPTH, 3, 352)]) for j in range(N_DEV)]
    c_all = jnp.stack([q[0] for q in parts])
    w_gla_full = jnp.concatenate([q[1] for q in parts], axis=2)
    w_conv_full = jnp.concatenate([q[2] for q in parts], axis=2)

    n_ada = w_ada.shape[2]
    b_loc = lax.dynamic_slice_in_dim(b_ada, me * n_ada, n_ada, axis=1).reshape(DEPTH, 1, n_ada)
    mod_all = _ada_fwd(c_all, w_ada, b_loc)
    mod_recv = _exchange(jnp.swapaxes(mod_all, 0, 1), False, "a2a_mod")
    mod = jnp.swapaxes(mod_recv, 0, 1).reshape(DEPTH, 6, D)

    loc = dict(w_in=_permute_in(w_in), w_o=w_o, w_down=w_down, w_br=jnp.swapaxes(w_br, 2, 3),
               w_mg=jnp.swapaxes(w_mg, 1, 2), w_up=jnp.swapaxes(w_up, 1, 2))
    loc = {k: v.astype(BF16) for k, v in loc.items()}
    w_full = dict(w_in=(D, NP), w_o=(D, D), w_down=(D_FF, D), w_br=(3, D, BW), w_mg=(3 * D, D), w_up=(2 * D_FF, D))
    w_parts = dict(w_in=8, w_br=2, w_mg=4, w_o=1, w_up=11, w_down=2)
    gather, units = _Transfers("gather"), []
    for l in range(DEPTH):
        for k, parts in w_parts.items():
            axis = 1 if k == "w_br" else 0
            n = w_full[k][axis] // N_DEV
            gather.lands[(l, k)] = lax.empty(w_full[k], BF16)
            shard = loc[k][l]
            nbytes = shard.size * 2 // parts
            units += [((l, k), (axis, n, (), c * (n // parts), n // parts), shard, nbytes) for c in range(parts)]
    first, lag = w_parts["w_in"], 4
    order = [("cross", i) for i in range(first)] + [("pass", i) for i in range(first)]
    for i in range(first, len(units) + lag):
        order += [("cross", i)] if i < len(units) else []
        order += [("pass", i - lag)] if i - lag >= first else []
    for what, i in order:
        key, where, shard, nbytes = units[i]
        if what == "cross":
            gather.add(key, ("gather_chip",) + where, shard, uid=i, nbytes=nbytes)
        else:
            gather.add(key, ("pass_on",) + where, after=i, nbytes=nbytes)

    w2pad = jnp.pad(w_gla_full, ((0, 0), (0, 128 - LR_LANES), (0, 0)))
    btail = jnp.pad(b_fox_f, ((0, 0), (FF_LANE0, 128 - FF_LANE0 - N_HEADS)))
    stacked = dict(norm1_g=norm1_g, norm2_g=norm2_g, b_gla_a=b_gla_a, ret_norm_g=ret_norm_g, gla_norm_g=gla_norm_g,
                   q_norm_g=q_norm_g, k_norm_g=k_norm_g, b_mg=b_mg, b_conv=b_conv, w_conv=w_conv_full, w2pad=w2pad,
                   btail=btail, shift1=mod[:, 0], scale1=mod[:, 1], gate1=mod[:, 2], shift2=mod[:, 3], scale2=mod[:, 4],
                   gate2=mod[:, 5])
    landed = lambda l, k: functools.partial(gather.get, (l, k))
    layers = [dict({k: v[l] for k, v in stacked.items()}, w_in=landed(l, "w_in"), w_o=landed(l, "w_o"),
                   w_down=landed(l, "w_down"), w_br_t=landed(l, "w_br"), w_mg_t=landed(l, "w_mg"), w_up_t=landed(l, "w_up"))
              for l in range(DEPTH)]
    consts = _rope_tables() + (_ret_logg(),)

    xc, saved = x2d, []
    for l in range(DEPTH):
        xc, sv = _layer_fwd(xc, layers[l], consts, gather, 1.7 if l == 0 else 1.45)
        saved.append(sv)
    loss_part, dxc = _loss_fwd_bwd(xc, tgt)
    loss = lax.psum(loss_part[0, 0], ("x", "y", "c"))

    grad_names = ("w_in", "w_o", "w_down", "w_br0", "w_br1", "w_br2", "w_mg", "w_up")
    blk_rows = dict(w_in=(128, NP), w_o=(128, D), w_down=(352, D), w_br0=(128, BW), w_br1=(128, BW), w_br2=(128, BW),
                    w_mg=(384, D), w_up=(704, D))
    grads = _Transfers("grads")
    for k in grad_names:
        grads.lands[k] = lax.empty((N_DEV // 2 if k in GRAD_VIA_CHIP else N_DEV, DEPTH) + blk_rows[k], BF16)
    dmod, small_g = [None] * DEPTH, [None] * DEPTH
    for l in reversed(range(DEPTH)):
        dxc, _, dmod[l], small_g[l] = _layer_bwd(dxc, layers[l], saved[l], consts, grads, l)
    grad_x = dxc
    dmod = jnp.stack(dmod)
    small_g = {k: jnp.stack([s[k] for s in small_g]) for k in small_g[0]}
    grads.drain()
    recv = {k: grads.get(k) for k in grad_names}

    dmod_send = jnp.swapaxes(dmod.reshape(DEPTH, N_DEV, n_ada), 0, 1)
    dmod_all = jnp.swapaxes(_exchange(dmod_send, False, "a2a_dmod"), 0, 1)
    g_ada = _ada_bwd(c_all, dmod_all)

    def flat(a, k):
        return a.reshape((-1, W[k].shape[-1]))

    def adam_nat(k, g, tr):
        outs = _adamw(g, flat(W[k], k), flat(M[k], k), flat(V[k], k), tr, "adamw_" + k)
        return [o.reshape(W[k].shape) for o in outs]

    def summed(k, tr):
        r = recv[k]
        return _sum_partials(r.reshape(r.shape[0], DEPTH * r.shape[2], r.shape[3]), "sum_" + k, tr).reshape((DEPTH,) + r.shape[2:])

    def adam_as_stored(k, g, perm, tr, tc=None):
        shape_t = tuple(W[k].shape[a] for a in perm)
        view = lambda a: jnp.transpose(a, perm).reshape(g.shape[-2:])
        outs = _adamw(g, view(W[k]), view(M[k]), view(V[k]), tr, "adamw_" + k, tc)
        return [jnp.transpose(o.reshape(shape_t), tuple(np.argsort(perm))) for o in outs]

    big_out = dict(
        w_in=adam_nat("w_in", flat(_unpermute_in(summed("w_in", 64)), "w_in"), 64),
        w_o=adam_nat("w_o", recv["w_o"].reshape(N_DEV, DEPTH * 128, D), 128),
        w_down=adam_nat("w_down", recv["w_down"].reshape(N_DEV // 2, DEPTH * 352, D), 352),
        w_br=adam_nat("w_br", flat(jnp.swapaxes(jnp.stack([summed("w_br%d" % n, 128) for n in range(3)], axis=1), 2, 3),
                                   "w_br"), 1024),
        w_mg=adam_nat("w_mg", flat(jnp.swapaxes(summed("w_mg", 384), 1, 2), "w_mg"), 512),
        w_up=adam_as_stored("w_up", recv["w_up"].reshape(N_DEV // 2, DEPTH * 704, D), (0, 2, 1), 352))
    ada_out = [o.reshape(DEPTH, D, n_ada) for o in _adamw(
        g_ada.reshape(DEPTH * D, n_ada), w_ada.reshape(DEPTH * D, n_ada), m_w_ada.reshape(DEPTH * D, n_ada),
        v_w_ada.reshape(DEPTH * D, n_ada), 512, "adamw_ada")]

    small_g = dict(small_g, b_ada=dmod)
    names = SMALL_REPL + SMALL_SHARDED
    full_shapes = [W[n].shape for n in SMALL_REPL] + [(DEPTH, LR_LANES, 256), (DEPTH, 3, D_FF)]
    part = _flat_pack([small_g[n] for n in names])
    total = _flat_unpack(_sum_partials(_exchange(part, True, "gather_small_grads"), "sum_small"), full_shapes)
    total = dict(zip(names, total))
    total["w_gla_a2"] = lax.dynamic_slice_in_dim(total["w_gla_a2"], me * 32, 32, axis=2)
    total["w_conv"] = lax.dynamic_slice_in_dim(total["w_conv"], me * 352, 352, axis=2)
    shapes = [W[n].shape for n in names]
    small_out = _adamw(_flat_pack([total[n] for n in names]), _flat_pack([W[n] for n in names]),
                       _flat_pack([M[n] for n in names]), _flat_pack([V[n] for n in names]), None, "adamw_small")
    small_out = [dict(zip(names, _flat_unpack(o, shapes))) for o in small_out]

    outs = []
    for k in range(4):
        d = dict(small_out[k])
        d.update({n: big_out[n][k] for n in BIG})
        d["w_ada"] = ada_out[k]
        outs.append([d[n] for n in WEIGHTS])
    return (loss, grad_x.reshape(1, T, D), *outs[0], *outs[1], *outs[2], *outs[3])
```

```python
import functools

import numpy as np
import jax
import jax.numpy as jnp
from jax import lax
from jax.experimental import pallas as pl
from jax.experimental.pallas import tpu as pltpu

F32 = jnp.float32
BF16 = jnp.bfloat16

N_DEV = 8
T = 2048
D = 1024
DEPTH = 4
N_HEADS = 4
HD = 128
BW = 512
D_FF = 2816
CHUNK = 64
EPS = 1e-6
IN_W = 5140
NP = 5632
TAIL0 = 5120
LR_LANES = 16
FF_LANE0 = 16
PACK_W = 1024
SEG_ROWS = (704, 128, 352, 192, 384, 704)
LAYER_ROWS = sum(SEG_ROWS)
VMEM_LIMIT_V7X = 56 * 1024 * 1024

ADAM_LR, ADAM_B1, ADAM_B2, ADAM_EPS, ADAM_WD, ADAM_STEP = 0.001, 0.9, 0.999, 1e-08, 0.01, 10

MESH_ID = pl.DeviceIdType.MESH


def _cp(*sem):
    return pltpu.CompilerParams(dimension_semantics=sem if sem else None, vmem_limit_bytes=VMEM_LIMIT_V7X)


def _sigmoid(z):
    return 1.0 / (1.0 + jnp.exp(-z))


def _log_sigmoid(z):
    return jnp.minimum(z, 0.0) - jnp.log(1.0 + jnp.exp(-jnp.abs(z)))


def _sum0(a):
    return jnp.sum(a, axis=0, keepdims=True)


def _mean1(a):
    return jnp.mean(a, axis=-1, keepdims=True)


def _dot(a, b, dims):
    return lax.dot_general(a.astype(BF16), b.astype(BF16), (dims, ((), ())), preferred_element_type=F32)


NN = ((1,), (0,))
NT = ((1,), (1,))
TN = ((0,), (0,))


def _exact_dot(m01, a):
    a1 = a.astype(BF16)
    r1 = a - a1.astype(F32)
    a2 = r1.astype(BF16)
    a3 = (r1 - a2.astype(F32)).astype(BF16)
    d = lambda z: jnp.dot(m01, z, preferred_element_type=F32)
    return d(a1) + d(a2) + d(a3)


def _tri(n, upper):
    r = lax.broadcasted_iota(jnp.int32, (n, n), 0)
    c = lax.broadcasted_iota(jnp.int32, (n, n), 1)
    return jnp.where((c >= r) if upper else (c <= r), 1.0, 0.0).astype(BF16)


def _exchange(x, gather, name):
    blk = x.shape if gather else x.shape[1:]

    def body(x_ref, o_ref, send_sems, recv_sems, loc_sem):
        mx, my, mc = lax.axis_index("x"), lax.axis_index("y"), lax.axis_index("c")
        me = 4 * mx + 2 * my + mc
        loc = pltpu.make_async_copy(x_ref if gather else x_ref.at[me], o_ref.at[me], loc_sem)
        loc.start()
        copies = []
        for k in range(1, N_DEV):
            px = mx ^ (k >> 2) if (k >> 2) else mx
            py = my ^ ((k >> 1) & 1) if ((k >> 1) & 1) else my
            pc = mc ^ (k & 1) if (k & 1) else mc
            peer = 4 * px + 2 * py + pc
            cp = pltpu.make_async_remote_copy(
                src_ref=x_ref if gather else x_ref.at[peer], dst_ref=o_ref.at[me],
                send_sem=send_sems.at[k - 1], recv_sem=recv_sems.at[k - 1],
                device_id=(px, py, pc), device_id_type=MESH_ID)
            cp.start()
            copies.append(cp)
        for cp in copies:
            cp.wait()
        loc.wait()

    return pl.pallas_call(
        body, name=name,
        out_shape=jax.ShapeDtypeStruct((N_DEV,) + tuple(blk), x.dtype),
        in_specs=[pl.BlockSpec(memory_space=pl.ANY)],
        out_specs=pl.BlockSpec(memory_space=pl.ANY),
        scratch_shapes=[pltpu.SemaphoreType.DMA((N_DEV - 1,)), pltpu.SemaphoreType.DMA((N_DEV - 1,)),
                        pltpu.SemaphoreType.DMA],
        compiler_params=pltpu.CompilerParams(has_side_effects=True),
    )(x)


def _blk(ref, axis, j, n, r0=0, nr=None):
    return ref.at[(slice(None),) * axis + (pl.ds(j * n + r0, n if nr is None else nr),)]


def _comm_copies(items, srcs, lands, send_sems, recv_sems, loc_sems):
    mx, my, mc = lax.axis_index("x"), lax.axis_index("y"), lax.axis_index("c")
    me = 4 * mx + 2 * my + mc
    local, remote = [], []
    for t, (kind, axis, n, sel, r0, nr, si, li) in enumerate(items):
        if kind == "pass_on":
            for q in (2, 4, 6):
                px = 1 - mx if q & 4 else mx
                py = 1 - my if q & 2 else my
                rows = _blk(lands[li], axis, 4 * px + 2 * py + mc, n, r0, nr)
                remote.append(pltpu.make_async_remote_copy(
                    src_ref=rows, dst_ref=rows, send_sem=send_sems.at[t * (N_DEV - 1) + q - 1],
                    recv_sem=recv_sems.at[t * (N_DEV - 1) + q - 1], device_id=(mx, my, 1 - mc), device_id_type=MESH_ID))
            continue
        if kind == "to_other_core":
            for p in range(N_DEV // 2):
                remote.append(pltpu.make_async_remote_copy(
                    src_ref=_blk(srcs[si], axis, 2 * p + 1 - mc, n, r0, nr), dst_ref=lands[li].at[p, pl.ds(r0, nr)],
                    send_sem=send_sems.at[t * (N_DEV - 1) + p], recv_sem=recv_sems.at[t * (N_DEV - 1) + p],
                    device_id=(mx, my, 1 - mc), device_id_type=MESH_ID))
            continue
        if kind == "a2a_chip":
            pm = 2 * mx + my
            mine = lands[li].at[(pm,) + tuple(sel) + (pl.ds(r0, nr),)]
            local.append(pltpu.make_async_copy(srcs[si].at[pm, pl.ds(r0, nr)], mine, loc_sems.at[t]))
            for q in (2, 4, 6):
                px = 1 - mx if q & 4 else mx
                py = 1 - my if q & 2 else my
                remote.append(pltpu.make_async_remote_copy(
                    src_ref=srcs[si].at[2 * px + py, pl.ds(r0, nr)], dst_ref=mine,
                    send_sem=send_sems.at[t * (N_DEV - 1) + q - 1], recv_sem=recv_sems.at[t * (N_DEV - 1) + q - 1],
                    device_id=(px, py, mc), device_id_type=MESH_ID))
            continue
        if kind == "a2a":
            mine = lands[li].at[(me,) + tuple(sel) + (pl.ds(r0, nr),)]
            own = _blk(srcs[si], axis, me, n, r0, nr)
        else:
            mine = _blk(lands[li], axis, me, n, r0, nr)
            own = _blk(srcs[si], axis, 0, n, r0, nr)
        local.append(pltpu.make_async_copy(own, mine, loc_sems.at[t]))
        for k in ((1, 2, 4, 6) if kind == "gather_chip" else range(1, N_DEV)):
            px = 1 - mx if k & 4 else mx
            py = 1 - my if k & 2 else my
            pc = 1 - mc if k & 1 else mc
            src = _blk(srcs[si], axis, 4 * px + 2 * py + pc, n, r0, nr) if kind == "a2a" else own
            remote.append(pltpu.make_async_remote_copy(
                src_ref=src, dst_ref=mine, send_sem=send_sems.at[t * (N_DEV - 1) + k - 1],
                recv_sem=recv_sems.at[t * (N_DEV - 1) + k - 1], device_id=(px, py, pc), device_id_type=MESH_ID))
    return local, remote


def _comm_scratch(n_items):
    return [pltpu.SemaphoreType.DMA((n_items * (N_DEV - 1),)), pltpu.SemaphoreType.DMA((n_items * (N_DEV - 1),)),
            pltpu.SemaphoreType.DMA((n_items,))]


LINK_BYTES_PER_US = dict(gather_chip=23e3, a2a_chip=23e3, a2a=11.5e3, gather=11.5e3, pass_on=200e3, to_other_core=150e3)
CALL_EXCHANGE_US = 3.0


class _Duty:
    def __init__(self, items, srcs, lands, done):
        self.items, self.srcs, self.lands, self.done = items, srcs, lands, done


def _pcall(body, name, grid, in_specs, out_specs, out_shape, args, scratch_shapes=(), sem=(), duty=None):
    if duty is None:
        return pl.pallas_call(body, name=name, grid=grid, in_specs=list(in_specs), out_specs=out_specs,
                              out_shape=out_shape, scratch_shapes=list(scratch_shapes), compiler_params=_cp(*sem))(*args)
    single = not isinstance(out_shape, (list, tuple))
    o_shape = [out_shape] if single else list(out_shape)
    o_specs = [out_specs] if single else list(out_specs)
    n_in, n_out, n_scr = len(in_specs), len(o_shape), len(scratch_shapes)
    n_src, n_land, n_items = len(duty.srcs), len(duty.lands), len(duty.items)
    a0 = n_in + n_src + n_land

    def wrapped(*refs):
        srcs = refs[n_in:n_in + n_src]
        lands = refs[a0 + n_out:a0 + n_out + n_land]
        core = refs[:n_in] + refs[a0:a0 + n_out] + refs[a0 + n_out + n_land:a0 + n_out + n_land + n_scr]
        sems = refs[a0 + n_out + n_land + n_scr:]
        first = functools.reduce(jnp.logical_and, [pl.program_id(a) == 0 for a in range(len(grid))])
        last = functools.reduce(jnp.logical_and, [pl.program_id(a) == g - 1 for a, g in enumerate(grid)])

        @pl.when(first)
        def _():
            local, remote = _comm_copies(duty.items, srcs, lands, *sems)
            for cp in local + remote:
                cp.start()

        body(*core)

        @pl.when(last)
        def _():
            local, remote = _comm_copies(duty.items, srcs, lands, *sems)
            for cp in remote + local:
                cp.wait()

    hbm = pl.BlockSpec(memory_space=pl.ANY)
    res = pl.pallas_call(
        wrapped, name=name, grid=grid,
        in_specs=list(in_specs) + [hbm] * (n_src + n_land), out_specs=o_specs + [hbm] * n_land,
        out_shape=o_shape + [jax.ShapeDtypeStruct(a.shape, a.dtype) for a in duty.lands],
        input_output_aliases={n_in + n_src + t: n_out + t for t in range(n_land)},
        scratch_shapes=list(scratch_shapes) + _comm_scratch(n_items),
        compiler_params=pltpu.CompilerParams(dimension_semantics=("arbitrary",) * len(grid),
                                             vmem_limit_bytes=VMEM_LIMIT_V7X, has_side_effects=True),
    )(*args, *duty.srcs, *duty.lands)
    duty.done(res[n_out:])
    return res[0] if single else res[:n_out]


def _comm(duty, name):
    n_src, n_land = len(duty.srcs), len(duty.lands)

    def body(*refs):
        local, remote = _comm_copies(duty.items, refs[:n_src], refs[n_src + n_land:n_src + 2 * n_land],
                                     *refs[n_src + 2 * n_land:])
        for cp in local + remote:
            cp.start()
        for cp in remote + local:
            cp.wait()

    hbm = pl.BlockSpec(memory_space=pl.ANY)
    duty.done(pl.pallas_call(
        body, name=name, out_shape=[jax.ShapeDtypeStruct(a.shape, a.dtype) for a in duty.lands],
        in_specs=[hbm] * (n_src + n_land), out_specs=[hbm] * n_land,
        input_output_aliases={n_src + t: t for t in range(n_land)},
        scratch_shapes=_comm_scratch(len(duty.items)), compiler_params=pltpu.CompilerParams(has_side_effects=True),
    )(*duty.srcs, *duty.lands))


class _Transfers:
    def __init__(self, name):
        self.name, self.queue, self.lands, self.flushes, self.groups = name, [], {}, 0, {}

    def add(self, key, item, src=None, uid=None, after=None, group=None, nbytes=0):
        self.queue.append((key, item, src, uid, after, group, nbytes / LINK_BYTES_PER_US[item[0]]))
        if group is not None:
            self.groups[group] = [self.groups.get(group, [0, None])[0] + 1, None]

    def when_done(self, group, fn):
        self.groups[group][1] = fn

    def take_for(self, us):
        count, busy = 0, CALL_EXCHANGE_US
        while count < len(self.queue) and busy + self.queue[count][6] <= us:
            busy += self.queue[count][6]
            count += 1
        return self.take(count) if count else None

    def take(self, count):
        units = []
        while self.queue and len(units) < count:
            after = self.queue[0][4]
            if after is not None and any(u[3] == after for u in units):
                break
            units.append(self.queue.pop(0))
        if not units:
            return None
        keys, srcs, items = [], [], []
        for key, item, src, _, _, _, _ in units:
            if key not in keys:
                keys.append(key)
            if src is not None and not any(src is s for s in srcs):
                srcs.append(src)
            si = [i for i, s in enumerate(srcs) if s is src][0] if src is not None else -1
            items.append(tuple(item) + (si, keys.index(key)))

        def done(new_lands):
            for key, arr in zip(keys, new_lands):
                self.lands[key] = arr
            for u in units:
                if u[5] is not None:
                    self.groups[u[5]][0] -= 1
                    if self.groups[u[5]][0] == 0:
                        self.groups[u[5]][1]()

        return _Duty(items, srcs, [self.lands[k] for k in keys], done)

    def drain(self, upto=None):
        count = upto
        while self.queue if upto is None else count > 0:
            duty = self.take(len(self.queue) if upto is None else count)
            count = None if upto is None else count - len(duty.items)
            self.flushes += 1
            _comm(duty, "%s_flush%d" % (self.name, self.flushes))

    def get(self, key):
        pending = [i for i, u in enumerate(self.queue) if u[0] == key]
        if pending:
            self.drain(pending[-1] + 1)
        return self.lands[key]


def _matmul(a, b, mode, name, out_dtype=F32, tm=1024, tn=512, tk=None, add=None, n_blocks=None, duty=None):
    halves = a.ndim == 3
    if mode == "tn":
        K, M = a.shape[-2], a.shape[-1] * (2 if halves else 1)
        N = b.shape[1]
    else:
        M, K = a.shape[-2], a.shape[-1] * (2 if halves else 1)
        N = b.shape[0] if mode == "nt" else b.shape[1]
    tm, tn = min(tm, M), min(tn, N)
    j0 = 0
    if n_blocks is not None:
        j0, N = n_blocks[0], n_blocks[1] * tn
    tk = K if tk is None else tk
    nk = K // tk
    assert M % tm == 0 and N % tn == 0 and K % tk == 0, (name, M, N, K, tm, tn, tk)
    dims = {"nn": NN, "nt": NT, "tn": TN}[mode]
    has_add = add is not None

    def body(*refs):
        a_ref, b_ref = refs[:2]
        add_ref = refs[2] if has_add else None
        o_ref = refs[3 if has_add else 2]
        part = _dot(a_ref[...], b_ref[...], dims)

        def finish(total):
            if has_add:
                total = total + add_ref[...]
            o_ref[...] = total.astype(o_ref.dtype)

        if nk == 1:
            finish(part)
            return
        acc_ref = refs[-1]
        k = pl.program_id(2)

        @pl.when(k == 0)
        def _():
            acc_ref[...] = part

        @pl.when((k > 0) & (k < nk - 1))
        def _():
            acc_ref[...] += part

        @pl.when(k == nk - 1)
        def _():
            finish(acc_ref[...] + part)

    if halves and mode == "tn":
        per = a.shape[-1] // tm
        a_spec = pl.BlockSpec((None, tk, tm), lambda i, j, k: (i // per, k, i % per))
    elif halves:
        per = a.shape[-1] // tk
        a_spec = pl.BlockSpec((None, tm, tk), lambda i, j, k: (k // per, i, k % per))
    elif mode == "tn":
        a_spec = pl.BlockSpec((tk, tm), lambda i, j, k: (k, i))
    else:
        a_spec = pl.BlockSpec((tm, tk), lambda i, j, k: (i, k))
    if mode == "nt":
        b_spec = pl.BlockSpec((tn, tk), lambda i, j, k: (j0 + j, k))
    else:
        b_spec = pl.BlockSpec((tk, tn), lambda i, j, k: (k, j0 + j))
    o_spec = pl.BlockSpec((tm, tn), lambda i, j, k: (i, j))
    in_specs = [a_spec, b_spec] + ([o_spec] if has_add else [])
    args = (a, b) + ((add,) if has_add else ())
    return _pcall(
        body, name=name, grid=(M // tm, N // tn, nk),
        out_shape=jax.ShapeDtypeStruct((M, N), out_dtype),
        in_specs=in_specs, out_specs=o_spec,
        scratch_shapes=[pltpu.VMEM((tm, tn), F32)] if nk > 1 else [],
        sem=("parallel", "parallel", "arbitrary"), args=args, duty=duty)


def _ada_fwd(c_all, w_ada, b_loc):
    n = w_ada.shape[2]

    def body(c_ref, w_ref, b_ref, o_ref):
        c = c_ref[...]
        o_ref[0] = _dot(c * _sigmoid(c), w_ref[0], NN) + b_ref[0]

    return pl.pallas_call(
        body, name="ada_fwd", grid=(DEPTH,),
        out_shape=jax.ShapeDtypeStruct((DEPTH, N_DEV, n), F32),
        in_specs=[pl.BlockSpec((N_DEV, D), lambda l: (0, 0)),
                  pl.BlockSpec((1, D, n), lambda l: (l, 0, 0)),
                  pl.BlockSpec((1, 1, n), lambda l: (l, 0, 0))],
        out_specs=pl.BlockSpec((1, N_DEV, n), lambda l: (l, 0, 0)),
        compiler_params=_cp("parallel"),
    )(c_all, w_ada, b_loc)


def _ada_bwd(c_all, dmod_all):
    n = dmod_all.shape[2]

    def body(c_ref, d_ref, o_ref):
        c = c_ref[...]
        o_ref[0] = _dot(c * _sigmoid(c), d_ref[0], TN)

    return pl.pallas_call(
        body, name="ada_bwd", grid=(DEPTH,),
        out_shape=jax.ShapeDtypeStruct((DEPTH, D, n), F32),
        in_specs=[pl.BlockSpec((N_DEV, D), lambda l: (0, 0)),
                  pl.BlockSpec((1, N_DEV, n), lambda l: (l, 0, 0))],
        out_specs=pl.BlockSpec((1, D, n), lambda l: (l, 0, 0)),
        compiler_params=_cp("parallel"),
    )(c_all, dmod_all)


ROW_TILE = 512


def _row_spec(w=D, col=0):
    return pl.BlockSpec((ROW_TILE, w), lambda i: (i, col))


def _vec_spec(w=D):
    return pl.BlockSpec((1, w), lambda i: (0, 0))


def _norm_fwd(x, g, scale, shift, name, m=None, gate=None, duty=None):
    has_res = m is not None

    def body(*refs):
        if has_res:
            x_ref, m_ref, gate_ref, g_ref, sc_ref, sh_ref, xo_ref, h_ref = refs
            xv = x_ref[...] + gate_ref[...] * m_ref[...]
            xo_ref[...] = xv
        else:
            x_ref, g_ref, sc_ref, sh_ref, h_ref = refs
            xv = x_ref[...]
        r = lax.rsqrt(_mean1(xv * xv) + EPS)
        h_ref[...] = ((xv * r * g_ref[...]) * (1.0 + sc_ref[...]) + sh_ref[...]).astype(BF16)

    ins = [x] + ([m, gate] if has_res else []) + [g, scale, shift]
    in_specs = [_row_spec()] + ([_row_spec(), _vec_spec()] if has_res else []) + [_vec_spec()] * 3
    out_shape = [jax.ShapeDtypeStruct((T, D), BF16)]
    out_specs = [_row_spec()]
    if has_res:
        out_shape = [jax.ShapeDtypeStruct((T, D), F32)] + out_shape
        out_specs = [_row_spec()] + out_specs
    out = _pcall(body, name=name, grid=(T // ROW_TILE,), out_shape=out_shape, in_specs=in_specs,
                 out_specs=out_specs, sem=("parallel",), args=ins, duty=duty)
    return out if has_res else out[0]


def _norm_bwd(x, dh, dres, g, scale, shift, name, duty=None):
    def body(x_ref, dh_ref, dres_ref, g_ref, sc_ref, sh_ref, dx_ref, st_ref):
        xv, dh_v, gv = x_ref[...], dh_ref[...], g_ref[...]
        r = lax.rsqrt(_mean1(xv * xv) + EPS)
        n = xv * r
        dy = dh_v * (1.0 + sc_ref[...])
        dn = dy * gv
        dx_ref[...] = r * (dn - n * _mean1(dn * n)) + dres_ref[...]

        @pl.when(pl.program_id(0) == 0)
        def _():
            st_ref[...] = jnp.zeros_like(st_ref)

        st_ref[0:1, :] += _sum0(dy * n)
        st_ref[1:2, :] += _sum0(dh_v * (n * gv))
        st_ref[2:3, :] += _sum0(dh_v)

    return _pcall(
        body, name=name, grid=(T // ROW_TILE,),
        out_shape=[jax.ShapeDtypeStruct((T, D), F32), jax.ShapeDtypeStruct((8, D), F32)],
        in_specs=[_row_spec(), _row_spec(), _row_spec(), _vec_spec(), _vec_spec(), _vec_spec()],
        out_specs=[_row_spec(), pl.BlockSpec((8, D), lambda i: (0, 0))],
        sem=("arbitrary",), args=(x, dh, dres, g, scale, shift), duty=duty)


def _axpy(x, m, gate, name, duty=None):
    def body(x_ref, m_ref, gate_ref, o_ref):
        o_ref[...] = x_ref[...] + gate_ref[...] * m_ref[...]

    return _pcall(
        body, name=name, grid=(T // ROW_TILE,), out_shape=jax.ShapeDtypeStruct((T, D), F32),
        in_specs=[_row_spec(), _row_spec(), _vec_spec()], out_specs=_row_spec(),
        sem=("parallel",), args=(x, m, gate), duty=duty)


def _gate_bwd(dx, m, gate, name, duty=None):
    def body(dx_ref, m_ref, gate_ref, dm_ref, st_ref):
        dxv = dx_ref[...]
        dm_ref[...] = (gate_ref[...] * dxv).astype(BF16)

        @pl.when(pl.program_id(0) == 0)
        def _():
            st_ref[...] = jnp.zeros_like(st_ref)

        st_ref[0:1, :] += _sum0(dxv * m_ref[...])

    return _pcall(
        body, name=name, grid=(T // ROW_TILE,),
        out_shape=[jax.ShapeDtypeStruct((T, D), BF16), jax.ShapeDtypeStruct((8, D), F32)],
        in_specs=[_row_spec(), _row_spec(), _vec_spec()],
        out_specs=[_row_spec(), pl.BlockSpec((8, D), lambda i: (0, 0))],
        sem=("arbitrary",), args=(dx, m, gate), duty=duty)


def _loss_fwd_bwd(y, target):
    def body(y_ref, t_ref, l_ref, d_ref):
        e = y_ref[...] - t_ref[...]
        d_ref[...] = e * (1.0 / D)

        @pl.when(pl.program_id(0) == 0)
        def _():
            l_ref[...] = jnp.zeros_like(l_ref)

        l_ref[...] += jnp.sum(_sum0(e * e), axis=1, keepdims=True) * (0.5 / D)

    return pl.pallas_call(
        body, name="loss", grid=(T // ROW_TILE,),
        out_shape=[jax.ShapeDtypeStruct((8, 128), F32), jax.ShapeDtypeStruct((T, D), F32)],
        in_specs=[_row_spec(), _row_spec()],
        out_specs=[pl.BlockSpec((8, 128), lambda i: (0, 0)), _row_spec()],
        compiler_params=_cp("arbitrary"))(y, target)


def _rope_tables():
    half = HD // 2
    inv_freq = 10000.0 ** (-jnp.arange(half, dtype=F32) / half)
    ang = jnp.arange(T, dtype=F32)[:, None] * inv_freq[None, :]
    cos, sin = jnp.cos(ang), jnp.sin(ang)
    return jnp.concatenate([cos, cos], axis=1), jnp.concatenate([-sin, sin], axis=1)


def _rope_fwd(p, cosf, sinf, duty=None):
    def body(p_ref, c_ref, s_ref, o_ref):
        cv, sv = c_ref[...], s_ref[...]
        for j in range(2 * N_HEADS):
            xv = p_ref[:, j * HD:(j + 1) * HD].astype(F32)
            rot = xv * cv + pltpu.roll(xv, HD // 2, 1) * sv
            if j >= N_HEADS:
                rot = rot * (HD ** -0.5)
            o_ref[:, j * HD:(j + 1) * HD] = rot.astype(BF16)

    return _pcall(
        body, name="rope_fwd", grid=(T // ROW_TILE,),
        out_shape=jax.ShapeDtypeStruct((T, 2 * BW), BF16),
        in_specs=[_row_spec(2 * BW), _row_spec(HD), _row_spec(HD)], out_specs=_row_spec(2 * BW),
        sem=("parallel",), args=(p, cosf, sinf), duty=duty)


def _rope_bwd(dq, dk, cosf, sinf, duty=None):
    def body(dq_ref, dk_ref, c_ref, s_ref, o_ref):
        cv, sv = c_ref[...], s_ref[...]
        for j in range(2 * N_HEADS):
            h = j % N_HEADS
            d = dq_ref[:, h * HD:(h + 1) * HD] if j < N_HEADS else dk_ref[:, h * HD:(h + 1) * HD] * (HD ** -0.5)
            o_ref[:, j * HD:(j + 1) * HD] = d * cv + pltpu.roll(d * sv, HD // 2, 1)

    return _pcall(
        body, name="rope_bwd", grid=(T // ROW_TILE,),
        out_shape=jax.ShapeDtypeStruct((T, 2 * BW), F32),
        in_specs=[_row_spec(BW), _row_spec(BW), _row_spec(HD), _row_spec(HD)], out_specs=_row_spec(2 * BW),
        sem=("parallel",), args=(dq, dk, cosf, sinf), duty=duty)


TQ = 256
V_RET_BLK = 8


def _ret_logg():
    lg = jnp.log1p(-jnp.exp2(-5.0 - jnp.arange(N_HEADS, dtype=F32)))
    return jnp.broadcast_to(lg[:, None, None], (N_HEADS, 1, 128))


def _block_iotas(i, kl):
    rows = lax.broadcasted_iota(jnp.int32, (TQ, kl), 0) + i * TQ
    cols = lax.broadcasted_iota(jnp.int32, (TQ, kl), 1)
    return rows, cols


def _ret_weight(lg_ref, i, kl):
    rows, cols = _block_iotas(i, kl)
    dist = jnp.abs(rows - cols).astype(F32)
    w = jnp.exp(dist * lg_ref[0][:, 0:1])
    return jnp.where((cols >> 6) <= (rows >> 6), w, 0.0)


def _per_query_block(i, fn):
    for n in range(1, T // TQ + 1):
        pl.when(i == n - 1)(functools.partial(fn, n * TQ))


def _ret_specs():
    q_spec = pl.BlockSpec((TQ, HD), lambda h, i: (i, h))
    k_spec = pl.BlockSpec((T, HD), lambda h, i: (0, N_HEADS + h))
    v_spec = pl.BlockSpec((T, HD), lambda h, i: (0, V_RET_BLK + h))
    lg_spec = pl.BlockSpec((1, 1, 128), lambda h, i: (h, 0, 0))
    return q_spec, k_spec, v_spec, lg_spec


def _ret_fwd(qk, p, logg, duty=None):
    def body(q_ref, k_ref, v_ref, lg_ref, o_ref):
        i = pl.program_id(1)

        def visible(kl):
            s = _dot(q_ref[...], k_ref[0:kl, :], NT) * _ret_weight(lg_ref, i, kl)
            o_ref[...] = _dot(s, v_ref[0:kl, :], NN)

        _per_query_block(i, visible)

    q_spec, k_spec, v_spec, lg_spec = _ret_specs()
    return _pcall(
        body, name="ret_fwd", grid=(N_HEADS, T // TQ),
        out_shape=jax.ShapeDtypeStruct((T, BW), F32),
        in_specs=[q_spec, k_spec, v_spec, lg_spec], out_specs=q_spec,
        sem=("parallel", "parallel"), args=(qk, qk, p, logg), duty=duty)


def _ret_bwd(qk, p, logg, do, duty=None):
    def body(q_ref, k_ref, v_ref, lg_ref, do_ref, dq_ref, dk_ref, dv_ref):
        i = pl.program_id(1)
        q, dov = q_ref[...], do_ref[...]

        @pl.when(i == 0)
        def _():
            dk_ref[...] = jnp.zeros_like(dk_ref)
            dv_ref[...] = jnp.zeros_like(dv_ref)

        def visible(kl):
            w = _ret_weight(lg_ref, i, kl)
            k = k_ref[0:kl, :]
            s = _dot(q, k, NT) * w
            ds = _dot(dov, v_ref[0:kl, :], NT) * w
            dk_ref[0:kl, :] += _dot(ds, q, TN)
            dv_ref[0:kl, :] += _dot(s, dov, TN)
            dq_ref[...] = _dot(ds, k, NN)

        _per_query_block(i, visible)

    q_spec, k_spec, v_spec, lg_spec = _ret_specs()
    acc_spec = pl.BlockSpec((T, HD), lambda h, i: (0, h))
    sh = jax.ShapeDtypeStruct((T, BW), F32)
    return _pcall(
        body, name="ret_bwd", grid=(N_HEADS, T // TQ),
        out_shape=[sh, sh, sh],
        in_specs=[q_spec, k_spec, v_spec, lg_spec, q_spec], out_specs=[q_spec, acc_spec, acc_spec],
        sem=("parallel", "arbitrary"), args=(qk, qk, p, logg, do), duty=duty)


def _post_norm(xv, gv, centered):
    if centered:
        xv = xv - _mean1(xv)
    r = lax.rsqrt(_mean1(xv * xv) + EPS)
    return xv * r, r


def _branch_post_fwd(raw, p, g, gate_blk, centered, name, duty=None):
    def body(raw_ref, z_ref, g_ref, o_ref):
        for h in range(N_HEADS):
            sl = slice(h * HD, (h + 1) * HD)
            gv = g_ref[:, sl] if centered else g_ref[...]
            xh, _ = _post_norm(raw_ref[:, sl], gv, centered)
            z = z_ref[:, sl].astype(F32)
            o_ref[:, sl] = (z * _sigmoid(z) * (xh * gv)).astype(BF16)

    return _pcall(
        body, name=name, grid=(T // ROW_TILE,),
        out_shape=jax.ShapeDtypeStruct((T, BW), BF16),
        in_specs=[_row_spec(BW), _row_spec(BW, gate_blk), _vec_spec(BW if centered else HD)],
        out_specs=_row_spec(BW), sem=("parallel",), args=(raw, p, g), duty=duty)


def _branch_post_bwd(raw, p, g, dout, gate_blk, centered, name, duty=None):
    gw = BW if centered else HD

    def body(raw_ref, z_ref, g_ref, do_ref, dr_ref, dz_ref, dg_ref):
        @pl.when(pl.program_id(0) == 0)
        def _():
            dg_ref[...] = jnp.zeros_like(dg_ref)

        for h in range(N_HEADS):
            sl = slice(h * HD, (h + 1) * HD)
            gsl = sl if centered else slice(0, HD)
            gv, z, dov = g_ref[:, gsl], z_ref[:, sl].astype(F32), do_ref[:, sl]
            xh, r = _post_norm(raw_ref[:, sl], gv, centered)
            sg = _sigmoid(z)
            dyn = dov * (z * sg)
            dz_ref[:, sl] = dov * (xh * gv) * (sg * (1.0 + z * (1.0 - sg)))
            dxh = dyn * gv
            t = dxh - xh * _mean1(dxh * xh)
            if centered:
                t = t - _mean1(dxh)
            dr_ref[:, sl] = r * t
            dg_ref[0:1, gsl] += _sum0(dyn * xh)

    return _pcall(
        body, name=name, grid=(T // ROW_TILE,),
        out_shape=[jax.ShapeDtypeStruct((T, BW), F32), jax.ShapeDtypeStruct((T, BW), F32),
                   jax.ShapeDtypeStruct((8, gw), F32)],
        in_specs=[_row_spec(BW), _row_spec(BW, gate_blk), _vec_spec(gw), _row_spec(BW)],
        out_specs=[_row_spec(BW), _row_spec(BW), pl.BlockSpec((8, gw), lambda i: (0, 0))],
        sem=("arbitrary",), args=(raw, p, g, dout), duty=duty)


GLA_ROWS = 256
GLA_CPB = GLA_ROWS // CHUNK
GLA_DK = 64
GLA_W = N_HEADS * GLA_DK
GQ_BLK, GK_BLK, GV_BLK, GG_BLK = 8, 9, 5, 6
TAIL_BLK = TAIL0 // 128
RG_BLK = 3


def _gla_chunk_common(tl, w2, bv, kv):
    pre = _dot(tl, w2, NN) + bv
    la = _log_sigmoid(pre) * (1.0 / 16.0)
    bc = _exact_dot(_tri(CHUNK, False), la)
    be = bc[CHUNK - 1:CHUNK, :]
    w = jnp.exp(be - bc)
    return pre, w, jnp.exp(be), kv * w


def _head_masks():
    lane = lax.broadcasted_iota(jnp.int32, (1, GLA_W), 1)
    return [jnp.where((lane // GLA_DK) == h, 1.0, 0.0) for h in range(N_HEADS)]


def _gla_fwd(p, tail, w2pad, b, duty=None):
    nb = T // GLA_ROWS

    def body(q_ref, k_ref, v_ref, t_ref, w2_ref, b_ref, o_ref, st_ref, s_acc):
        @pl.when(pl.program_id(0) == 0)
        def _():
            s_acc[...] = jnp.zeros_like(s_acc)

        masks = _head_masks()
        for c in range(GLA_CPB):
            rows = slice(c * CHUNK, (c + 1) * CHUNK)
            _, _, a, kd = _gla_chunk_common(t_ref[rows, :], w2_ref[...], b_ref[...], k_ref[rows, :].astype(F32))
            q = q_ref[rows, :].astype(F32) * (GLA_DK ** -0.5)
            kv = None
            for h in range(N_HEADS):
                t = _dot(v_ref[rows, h * HD:(h + 1) * HD], kd * masks[h], TN)
                kv = t if kv is None else kv + t
            s_new = s_acc[...] * a + kv
            s_acc[...] = s_new
            st_ref[c] = s_new
            for h in range(N_HEADS):
                o_ref[rows, h * HD:(h + 1) * HD] = _dot(q * masks[h], s_new, NT)

    return _pcall(
        body, name="gla_fwd", grid=(nb,),
        out_shape=[jax.ShapeDtypeStruct((T, BW), F32), jax.ShapeDtypeStruct((T // CHUNK, HD, GLA_W), F32)],
        in_specs=[pl.BlockSpec((GLA_ROWS, GLA_W), lambda i: (i, GQ_BLK)),
                  pl.BlockSpec((GLA_ROWS, GLA_W), lambda i: (i, GK_BLK)),
                  pl.BlockSpec((GLA_ROWS, BW), lambda i: (i, GV_BLK)),
                  pl.BlockSpec((GLA_ROWS, 128), lambda i: (i, 0)),
                  pl.BlockSpec((128, GLA_W), lambda i: (0, 0)),
                  pl.BlockSpec((1, GLA_W), lambda i: (0, 0))],
        out_specs=[pl.BlockSpec((GLA_ROWS, BW), lambda i: (i, 0)),
                   pl.BlockSpec((GLA_CPB, HD, GLA_W), lambda i: (i, 0, 0))],
        scratch_shapes=[pltpu.VMEM((HD, GLA_W), F32)],
        sem=("arbitrary",), args=(p, p, p, tail, w2pad, b), duty=duty)


def _gla_bwd(p, tail, w2pad, b, states, do, duty=None):
    nb = T // GLA_ROWS

    def body(q_ref, k_ref, v_ref, t_ref, w2_ref, b_ref, st_ref, prev_ref, do_ref,
             dq_ref, dk_ref, dv_ref, dt_ref, dw2_ref, db_ref, ds_acc):
        step = pl.program_id(0)

        @pl.when(step == 0)
        def _():
            ds_acc[...] = jnp.zeros_like(ds_acc)
            dw2_ref[...] = jnp.zeros_like(dw2_ref)
            db_ref[...] = jnp.zeros_like(db_ref)

        masks = _head_masks()
        up = _tri(CHUNK, True)
        has_prev = jnp.where(step == nb - 1, 0.0, 1.0)
        for c in reversed(range(GLA_CPB)):
            rows = slice(c * CHUNK, (c + 1) * CHUNK)
            tl, w2, k = t_ref[rows, :], w2_ref[...], k_ref[rows, :].astype(F32)
            pre, w, a, kd = _gla_chunk_common(tl, w2, b_ref[...], k)
            q = q_ref[rows, :].astype(F32) * (GLA_DK ** -0.5)
            s_n = st_ref[c]
            s_prev = st_ref[c - 1] if c > 0 else prev_ref[0] * has_prev
            ds = ds_acc[...]
            dos = [do_ref[rows, h * HD:(h + 1) * HD] for h in range(N_HEADS)]
            for h in range(N_HEADS):
                ds = ds + _dot(dos[h], q * masks[h], TN)
            dqp = jnp.zeros((CHUNK, GLA_W), F32)
            dkd = jnp.zeros((CHUNK, GLA_W), F32)
            for h in range(N_HEADS):
                dqp = dqp + masks[h] * _dot(dos[h], s_n, NN)
                dkd = dkd + masks[h] * _dot(v_ref[rows, h * HD:(h + 1) * HD], ds, NN)
                dv_ref[rows, h * HD:(h + 1) * HD] = _dot(kd * masks[h], ds, NT)
            dq_ref[rows, :] = dqp * (GLA_DK ** -0.5)
            dk_ref[rows, :] = dkd * w
            e = dkd * k * w
            dbe = _sum0(e) + _sum0(ds * s_prev) * a
            dla = dbe - _exact_dot(up, e)
            dpre = dla * (1.0 / 16.0) * _sigmoid(-pre)
            db_ref[0:1, :] += _sum0(dpre)
            dw2_ref[...] += _dot(tl, dpre, TN)
            dt_ref[rows, :] = _dot(dpre, w2, NT)
            ds_acc[...] = ds * a

    rev = lambda i: nb - 1 - i
    sh = lambda w: jax.ShapeDtypeStruct((T, w), F32)
    return _pcall(
        body, name="gla_bwd", grid=(nb,),
        out_shape=[sh(GLA_W), sh(GLA_W), sh(BW), sh(128), jax.ShapeDtypeStruct((128, GLA_W), F32),
                   jax.ShapeDtypeStruct((8, GLA_W), F32)],
        in_specs=[pl.BlockSpec((GLA_ROWS, GLA_W), lambda i: (rev(i), GQ_BLK)),
                  pl.BlockSpec((GLA_ROWS, GLA_W), lambda i: (rev(i), GK_BLK)),
                  pl.BlockSpec((GLA_ROWS, BW), lambda i: (rev(i), GV_BLK)),
                  pl.BlockSpec((GLA_ROWS, 128), lambda i: (rev(i), 0)),
                  pl.BlockSpec((128, GLA_W), lambda i: (0, 0)),
                  pl.BlockSpec((1, GLA_W), lambda i: (0, 0)),
                  pl.BlockSpec((GLA_CPB, HD, GLA_W), lambda i: (rev(i), 0, 0)),
                  pl.BlockSpec((1, HD, GLA_W), lambda i: (jnp.maximum(rev(i) * GLA_CPB - 1, 0), 0, 0)),
                  pl.BlockSpec((GLA_ROWS, BW), lambda i: (rev(i), 0))],
        out_specs=[pl.BlockSpec((GLA_ROWS, GLA_W), lambda i: (rev(i), 0)),
                   pl.BlockSpec((GLA_ROWS, GLA_W), lambda i: (rev(i), 0)),
                   pl.BlockSpec((GLA_ROWS, BW), lambda i: (rev(i), 0)),
                   pl.BlockSpec((GLA_ROWS, 128), lambda i: (rev(i), 0)),
                   pl.BlockSpec((128, GLA_W), lambda i: (0, 0)),
                   pl.BlockSpec((8, GLA_W), lambda i: (0, 0))],
        scratch_shapes=[pltpu.VMEM((HD, GLA_W), F32)],
        sem=("arbitrary",), args=(p, p, p, tail, w2pad, b, states, states, do), duty=duty)


FQ_BLK, FK_BLK = 7, 8
V_FOX_BLK = 36


def _fox_prep_fwd(p, tail, qg, kg, btail, duty=None):
    def body(q_ref, k_ref, t_ref, qg_ref, kg_ref, bt_ref, o_ref, cum_ref, carry):
        @pl.when(pl.program_id(0) == 0)
        def _():
            carry[...] = jnp.zeros_like(carry)

        for src, gr, off in ((q_ref, qg_ref, 0), (k_ref, kg_ref, BW)):
            for h in range(N_HEADS):
                xv = src[:, h * HD:(h + 1) * HD].astype(F32)
                r = lax.rsqrt(_mean1(xv * xv) + EPS)
                o_ref[:, off + h * HD:off + (h + 1) * HD] = (xv * r * gr[...]).astype(BF16)
        logf = _log_sigmoid(t_ref[...] + bt_ref[...])
        cum = _exact_dot(_tri(ROW_TILE, False), logf) + carry[...]
        cum_ref[...] = cum
        carry[...] = cum[ROW_TILE - 1:ROW_TILE, :]

    return _pcall(
        body, name="fox_prep_fwd", grid=(T // ROW_TILE,),
        out_shape=[jax.ShapeDtypeStruct((T, 2 * BW), BF16), jax.ShapeDtypeStruct((T, 128), F32)],
        in_specs=[_row_spec(BW, FQ_BLK), _row_spec(BW, FK_BLK), _row_spec(128),
                  _vec_spec(HD), _vec_spec(HD), _vec_spec(128)],
        out_specs=[_row_spec(2 * BW), _row_spec(128)],
        scratch_shapes=[pltpu.VMEM((1, 128), F32)],
        sem=("arbitrary",), args=(p, p, tail, qg, kg, btail), duty=duty)


def _fox_prep_bwd(p, tail, qg, kg, btail, dqn, dkn, dcum, duty=None):
    nt = T // ROW_TILE

    def body(q_ref, k_ref, t_ref, qg_ref, kg_ref, bt_ref, dq_ref, dk_ref, dc_ref, o_ref, dt_ref, st_ref, carry):
        @pl.when(pl.program_id(0) == 0)
        def _():
            carry[...] = jnp.zeros_like(carry)
            st_ref[...] = jnp.zeros_like(st_ref)

        for row, (src, gr, dsrc, off) in enumerate(((q_ref, qg_ref, dq_ref, 0), (k_ref, kg_ref, dk_ref, BW))):
            for h in range(N_HEADS):
                xv = src[:, h * HD:(h + 1) * HD].astype(F32)
                dy = dsrc[:, h * HD:(h + 1) * HD]
                r = lax.rsqrt(_mean1(xv * xv) + EPS)
                n = xv * r
                dn = dy * gr[...]
                o_ref[:, off + h * HD:off + (h + 1) * HD] = r * (dn - n * _mean1(dn * n))
                st_ref[row:row + 1, :] += _sum0(dy * n)
        z = t_ref[...] + bt_ref[...]
        dlogf = _exact_dot(_tri(ROW_TILE, True), dc_ref[...]) + carry[...]
        carry[...] = dlogf[0:1, :]
        lane = lax.broadcasted_iota(jnp.int32, (1, 128), 1)
        keep = (lane >= FF_LANE0) & (lane < FF_LANE0 + N_HEADS)
        dz = jnp.where(keep, dlogf * _sigmoid(-z), 0.0)
        dt_ref[...] = dz
        st_ref[2:3, :] += _sum0(dz)

    rs = lambda w, col=0: pl.BlockSpec((ROW_TILE, w), lambda i: (nt - 1 - i, col))
    return _pcall(
        body, name="fox_prep_bwd", grid=(nt,),
        out_shape=[jax.ShapeDtypeStruct((T, 2 * BW), F32), jax.ShapeDtypeStruct((T, 128), F32),
                   jax.ShapeDtypeStruct((8, 128), F32)],
        in_specs=[rs(BW, FQ_BLK), rs(BW, FK_BLK), rs(128), _vec_spec(HD), _vec_spec(HD), _vec_spec(128),
                  rs(BW), rs(BW), rs(128)],
        out_specs=[rs(2 * BW), rs(128), pl.BlockSpec((8, 128), lambda i: (0, 0))],
        scratch_shapes=[pltpu.VMEM((1, 128), F32)],
        sem=("arbitrary",), args=(p, p, tail, qg, kg, btail, dqn, dkn, dcum), duty=duty)


def _fox_logits(q_ref, k_ref, cc_ref, cr_ref, i, kl):
    rows, cols = _block_iotas(i, kl)
    s = _dot(q_ref[...], k_ref[0:kl, :], NT) * (HD ** -0.5) + cc_ref[0] - cr_ref[0, :, 0:kl]
    return jnp.where(cols <= rows, s, -1e30)


def _fox_specs():
    q_spec = pl.BlockSpec((TQ, HD), lambda h, i: (i, h))
    k_spec = pl.BlockSpec((T, HD), lambda h, i: (0, N_HEADS + h))
    v_spec = pl.BlockSpec((T, HD), lambda h, i: (0, V_FOX_BLK + h))
    col_spec = pl.BlockSpec((1, TQ, 1), lambda h, i: (h, i, 0))
    row_spec = pl.BlockSpec((1, 1, T), lambda h, i: (h, 0, 0))
    return q_spec, k_spec, v_spec, col_spec, row_spec


def _fox_fwd(qkn, p, cumcol, cumrow, duty=None):
    def body(q_ref, k_ref, v_ref, cc_ref, cr_ref, o_ref, lse_ref):
        i = pl.program_id(1)

        def visible(kl):
            s = _fox_logits(q_ref, k_ref, cc_ref, cr_ref, i, kl)
            m = jnp.max(s, axis=-1, keepdims=True)
            e = jnp.exp(s - m)
            l = jnp.sum(e, axis=-1, keepdims=True)
            o_ref[...] = _dot(e / l, v_ref[0:kl, :], NN)
            lse_ref[0] = m + jnp.log(l)

        _per_query_block(i, visible)

    q_spec, k_spec, v_spec, col_spec, row_spec = _fox_specs()
    return _pcall(
        body, name="fox_fwd", grid=(N_HEADS, T // TQ),
        out_shape=[jax.ShapeDtypeStruct((T, BW), F32), jax.ShapeDtypeStruct((N_HEADS, T, 1), F32)],
        in_specs=[q_spec, k_spec, v_spec, col_spec, row_spec], out_specs=[q_spec, col_spec],
        sem=("parallel", "parallel"), args=(qkn, qkn, p, cumcol, cumrow), duty=duty)


def _fox_bwd(qkn, p, cumcol, cumrow, lse, o, do, duty=None):
    def body(q_ref, k_ref, v_ref, cc_ref, cr_ref, lse_ref, o_ref, do_ref, dq_ref, dk_ref, dv_ref, dr_ref, dc_ref):
        i = pl.program_id(1)
        @pl.when(i == 0)
        def _():
            dk_ref[...] = jnp.zeros_like(dk_ref)
            dv_ref[...] = jnp.zeros_like(dv_ref)
            dc_ref[...] = jnp.zeros_like(dc_ref)

        def visible(kl):
            q, dov = q_ref[...], do_ref[...]
            pm = jnp.exp(_fox_logits(q_ref, k_ref, cc_ref, cr_ref, i, kl) - lse_ref[0])
            delta = jnp.sum(o_ref[...] * dov, axis=-1, keepdims=True)
            ds = pm * (_dot(dov, v_ref[0:kl, :], NT) - delta)
            dq_ref[...] = _dot(ds, k_ref[0:kl, :], NN) * (HD ** -0.5)
            dr_ref[0] = jnp.sum(ds, axis=-1, keepdims=True)
            dk_ref[0:kl, :] += _dot(ds, q, TN) * (HD ** -0.5)
            dv_ref[0:kl, :] += _dot(pm, dov, TN)
            dc_ref[0, :, 0:kl] += _sum0(ds)

        _per_query_block(i, visible)

    q_spec, k_spec, v_spec, col_spec, row_spec = _fox_specs()
    acc_spec = pl.BlockSpec((T, HD), lambda h, i: (0, h))
    sh = jax.ShapeDtypeStruct((T, BW), F32)
    return _pcall(
        body, name="fox_bwd", grid=(N_HEADS, T // TQ),
        out_shape=[sh, sh, sh, jax.ShapeDtypeStruct((N_HEADS, T, 1), F32), jax.ShapeDtypeStruct((N_HEADS, 1, T), F32)],
        in_specs=[q_spec, k_spec, v_spec, col_spec, row_spec, col_spec, q_spec, q_spec],
        out_specs=[q_spec, acc_spec, acc_spec, col_spec, row_spec],
        sem=("parallel", "arbitrary"), args=(qkn, qkn, p, cumcol, cumrow, lse, o, do), duty=duty)


def _mix_fwd(gpre, b_mg, y0, y1, y2, duty=None):
    def body(g_ref, b_ref, y0_ref, y1_ref, y2_ref, o_ref):
        acc = None
        for n, y_ref in enumerate((y0_ref, y1_ref, y2_ref)):
            sl = slice(n * D, (n + 1) * D)
            t = _sigmoid(g_ref[:, sl].astype(F32) + b_ref[:, sl]) * y_ref[...].astype(F32)
            acc = t if acc is None else acc + t
        o_ref[...] = acc.astype(BF16)

    return _pcall(
        body, name="mix_fwd", grid=(T // ROW_TILE,), out_shape=jax.ShapeDtypeStruct((T, D), BF16),
        in_specs=[_row_spec(3 * D), _vec_spec(3 * D), _row_spec(), _row_spec(), _row_spec()],
        out_specs=_row_spec(), sem=("parallel",), args=(gpre, b_mg, y0, y1, y2), duty=duty)


def _mix_bwd(gpre, b_mg, y0, y1, y2, dmi, duty=None):
    def body(g_ref, b_ref, y0_ref, y1_ref, y2_ref, d_ref, dy0_ref, dy1_ref, dy2_ref, dg_ref, db_ref):
        @pl.when(pl.program_id(0) == 0)
        def _():
            db_ref[...] = jnp.zeros_like(db_ref)

        dv = d_ref[...]
        for n, (y_ref, dy_ref) in enumerate(((y0_ref, dy0_ref), (y1_ref, dy1_ref), (y2_ref, dy2_ref))):
            sl = slice(n * D, (n + 1) * D)
            sg = _sigmoid(g_ref[:, sl].astype(F32) + b_ref[:, sl])
            dy_ref[...] = (dv * sg).astype(BF16)
            dpre = dv * y_ref[...].astype(F32) * (sg * (1.0 - sg))
            dg_ref[:, sl] = dpre.astype(BF16)
            db_ref[0:1, sl] += _sum0(dpre)

    shb = jax.ShapeDtypeStruct((T, D), BF16)
    return _pcall(
        body, name="mix_bwd", grid=(T // ROW_TILE,),
        out_shape=[shb, shb, shb, jax.ShapeDtypeStruct((T, 3 * D), BF16), jax.ShapeDtypeStruct((8, 3 * D), F32)],
        in_specs=[_row_spec(3 * D), _vec_spec(3 * D), _row_spec(), _row_spec(), _row_spec(), _row_spec()],
        out_specs=[_row_spec(), _row_spec(), _row_spec(), _row_spec(3 * D), pl.BlockSpec((8, 3 * D), lambda i: (0, 0))],
        sem=("arbitrary",), args=(gpre, b_mg, y0, y1, y2, dmi), duty=duty)


FF_COLS = 256
FF_NBLK = D_FF // FF_COLS


def _shift_rows(a, n):
    rows = lax.broadcasted_iota(jnp.int32, a.shape, 0)
    rolled = pltpu.roll(a, n % T, 0)
    return jnp.where((rows >= n) if n > 0 else (rows < T + n), rolled, 0.0)


def _ffn_act_fwd(uu, w_conv, b_conv, duty=None):
    def body(u_ref, g_ref, w_ref, b_ref, o_ref):
        u = u_ref[...].astype(F32)
        w = w_ref[...]
        uc = b_ref[...] + w[0:1, :] * _shift_rows(u, 2) + w[1:2, :] * _shift_rows(u, 1) + w[2:3, :] * u
        o_ref[...] = (uc * _sigmoid(uc) * g_ref[...].astype(F32)).astype(BF16)

    return _pcall(
        body, name="ffn_act_fwd", grid=(FF_NBLK,), out_shape=jax.ShapeDtypeStruct((T, D_FF), BF16),
        in_specs=[pl.BlockSpec((T, FF_COLS), lambda j: (0, j)), pl.BlockSpec((T, FF_COLS), lambda j: (0, FF_NBLK + j)),
                  pl.BlockSpec((3, FF_COLS), lambda j: (0, j)), pl.BlockSpec((1, FF_COLS), lambda j: (0, j))],
        out_specs=pl.BlockSpec((T, FF_COLS), lambda j: (0, j)),
        sem=("parallel",), args=(uu, uu, w_conv, b_conv), duty=duty)


def _ffn_act_bwd(uu, w_conv, b_conv, da, duty=None):
    def body(u_ref, g_ref, w_ref, b_ref, da_ref, d_ref, st_ref):
        u, w, dav = u_ref[...].astype(F32), w_ref[...], da_ref[...]
        u1, u2 = _shift_rows(u, 1), _shift_rows(u, 2)
        uc = b_ref[...] + w[0:1, :] * u2 + w[1:2, :] * u1 + w[2:3, :] * u
        sg = _sigmoid(uc)
        d_ref[1] = (dav * (uc * sg)).astype(BF16)
        duc = dav * g_ref[...].astype(F32) * (sg * (1.0 + uc * (1.0 - sg)))
        du = w[2:3, :] * duc + w[1:2, :] * _shift_rows(duc, -1) + w[0:1, :] * _shift_rows(duc, -2)
        d_ref[0] = du.astype(BF16)
        st_ref[...] = jnp.zeros_like(st_ref)
        st_ref[0:1, :] = _sum0(duc * u2)
        st_ref[1:2, :] = _sum0(duc * u1)
        st_ref[2:3, :] = _sum0(duc * u)
        st_ref[3:4, :] = _sum0(duc)

    cb = lambda rows=T, off=0: pl.BlockSpec((rows, FF_COLS), lambda j: (0, off + j))
    return _pcall(
        body, name="ffn_act_bwd", grid=(FF_NBLK,),
        out_shape=[jax.ShapeDtypeStruct((2, T, D_FF), BF16), jax.ShapeDtypeStruct((8, D_FF), F32)],
        in_specs=[cb(), cb(T, FF_NBLK), cb(3), cb(1), cb()],
        out_specs=[pl.BlockSpec((2, T, FF_COLS), lambda j: (0, 0, j)), cb(8)],
        sem=("parallel",), args=(uu, uu, w_conv, b_conv, da), duty=duty)


def _adamw(g, w, m, v, tr, name, tc=None):
    partial = g.ndim == 3
    R, C = w.shape
    tr = R if tr is None else tr
    tc = C if tc is None else tc
    assert R % tr == 0 and C % tc == 0 and (tr == R or tc == C)

    def body(g_ref, w_ref, m_ref, v_ref, go_ref, d_ref, mo_ref, vo_ref):
        if partial:
            gv = g_ref[0].astype(F32)
            for j in range(1, g.shape[0]):
                gv = gv + g_ref[j].astype(F32)
        else:
            gv = g_ref[...]
        go_ref[...] = gv
        mn = ADAM_B1 * m_ref[...] + (1.0 - ADAM_B1) * gv
        vn = ADAM_B2 * v_ref[...] + (1.0 - ADAM_B2) * (gv * gv)
        mo_ref[...] = mn
        vo_ref[...] = vn
        m_hat = mn / (1.0 - ADAM_B1 ** ADAM_STEP)
        v_hat = vn / (1.0 - ADAM_B2 ** ADAM_STEP)
        d_ref[...] = -ADAM_LR * (m_hat / (jnp.sqrt(v_hat) + ADAM_EPS) + ADAM_WD * w_ref[...])

    by_rows = tc == C
    spec = pl.BlockSpec((tr, tc), (lambda i: (i, 0)) if by_rows else (lambda i: (0, i)))
    g_spec = pl.BlockSpec((g.shape[0], tr, tc), (lambda i: (0, i, 0)) if by_rows else (lambda i: (0, 0, i))) if partial else spec
    sh = jax.ShapeDtypeStruct((R, C), F32)
    return pl.pallas_call(
        body, name=name, grid=(R // tr if by_rows else C // tc,), out_shape=[sh, sh, sh, sh],
        in_specs=[g_spec, spec, spec, spec], out_specs=[spec, spec, spec, spec],
        compiler_params=_cp("parallel"))(g, w, m, v)


def _chip_sum(dw, stage, name):
    n_chip, n, C = stage.shape

    def body(d_ref, s_ref, o_ref):
        mc = lax.axis_index("c")
        mine = jnp.where(mc == 0, d_ref[0, 0].astype(F32), d_ref[0, 1].astype(F32))
        o_ref[0] = (mine + s_ref[0].astype(F32)).astype(BF16)

    return pl.pallas_call(
        body, name=name, grid=(n_chip,), out_shape=jax.ShapeDtypeStruct(stage.shape, BF16),
        in_specs=[pl.BlockSpec((1, 2, n, C), lambda p: (p, 0, 0, 0)), pl.BlockSpec((1, n, C), lambda p: (p, 0, 0))],
        out_specs=pl.BlockSpec((1, n, C), lambda p: (p, 0, 0)), compiler_params=_cp("parallel"),
    )(dw.reshape(n_chip, 2, n, C), stage)


def _sum_partials(g, name, tr=None):
    n_part, R, C = g.shape
    tr = R if tr is None else tr
    assert R % tr == 0

    def body(g_ref, o_ref):
        acc = g_ref[0].astype(F32)
        for j in range(1, n_part):
            acc = acc + g_ref[j].astype(F32)
        o_ref[...] = acc

    return pl.pallas_call(
        body, name=name, grid=(R // tr,), out_shape=jax.ShapeDtypeStruct((R, C), F32),
        in_specs=[pl.BlockSpec((n_part, tr, C), lambda i: (0, i, 0))], out_specs=pl.BlockSpec((tr, C), lambda i: (i, 0)),
        compiler_params=_cp("parallel"))(g)


def _permute_in(w):
    pad = jnp.zeros(w.shape[:-1] + (NP - IN_W,), w.dtype)
    return jnp.concatenate([w[..., :3072], w[..., 3088:5136], w[..., 3072:3088], w[..., 5136:5140], pad], axis=-1)


def _unpermute_in(w):
    return jnp.concatenate([w[..., :3072], w[..., 5120:5136], w[..., 3072:5120], w[..., 5136:5140]], axis=-1)


def _flat_pack(arrs):
    flat = jnp.concatenate([a.reshape(-1).astype(F32) for a in arrs])
    n = flat.shape[0]
    rows = -(-n // 1024) * 8
    return jnp.pad(flat, (0, rows * 128 - n)).reshape(rows, 128)


def _flat_unpack(buf, shapes):
    flat = buf.reshape(-1)
    out, off = [], 0
    for s in shapes:
        n = int(np.prod(s))
        out.append(flat[off:off + n].reshape(s))
        off += n
    return out


GRAD_CHUNKS = dict(w_in=(128, 8), w_o=(128, 1), w_down=(352, 2), w_br0=(128, 1), w_br1=(128, 1), w_br2=(128, 1),
                   w_mg=(384, 4), w_up=(704, 11))
GRAD_VIA_CHIP = ("w_in", "w_down", "w_mg", "w_up")


def _send_grad(xfer, layer, k, g):
    if xfer is None:
        return g
    n, parts = GRAD_CHUNKS[k]
    row_bytes = g.shape[1] * 2
    if k not in GRAD_VIA_CHIP:
        for c in range(parts):
            xfer.add(k, ("a2a", 0, n, (layer,), c * (n // parts), n // parts), g, nbytes=n // parts * row_bytes)
        return g
    stage = ("stage", layer, k)
    xfer.lands[stage] = lax.empty((N_DEV // 2, n, g.shape[1]), BF16)
    xfer.add(stage, ("to_other_core", 0, n, (), 0, n), g, group=stage, nbytes=n * row_bytes)

    def both_halves_here():
        chip = _chip_sum(g, xfer.lands[stage], "chip_sum_" + k)
        for c in range(parts):
            xfer.add(k, ("a2a_chip", 0, n, (layer,), c * (n // parts), n // parts), chip, nbytes=n // parts * row_bytes)

    xfer.when_done(stage, both_halves_here)
    return g


def _weight(wl, k):
    return wl[k]() if callable(wl[k]) else wl[k]


MIN_CARRIER_US = 19.0


def _taker(xfer, fill=1.0):
    if xfer is None:
        return lambda us: None
    return lambda us: xfer.take_for(us * fill) if us >= MIN_CARRIER_US else None


def _layer_fwd(x0, wl, consts, xfer=None, fill=1.25):
    cosf, sinf, logg = consts
    row = lambda a: a.reshape(1, -1)
    take = _taker(xfer, fill)
    h = _norm_fwd(x0, row(wl["norm1_g"]), row(wl["scale1"]), row(wl["shift1"]), "norm1_fwd", duty=take(9))
    w_in = _weight(wl, "w_in")
    p = _matmul(h, w_in, "nn", "in_proj", out_dtype=BF16, n_blocks=(0, TAIL0 // 512), duty=take(30))
    tail = _matmul(h, w_in, "nn", "in_tail", tn=128, n_blocks=(TAIL_BLK, 1))
    qk = _rope_fwd(p, cosf, sinf, duty=take(10))
    ret_raw = _ret_fwd(qk, p, logg, duty=take(27))
    br0 = _branch_post_fwd(ret_raw, p, row(wl["ret_norm_g"]), RG_BLK, True, "ret_post_fwd", duty=take(9))
    gla_raw, states = _gla_fwd(p, tail, wl["w2pad"], row(wl["b_gla_a"]), duty=take(26))
    br1 = _branch_post_fwd(gla_raw, p, row(wl["gla_norm_g"]), GG_BLK, False, "gla_post_fwd", duty=take(9))
    qkn, cum = _fox_prep_fwd(p, tail, row(wl["q_norm_g"]), row(wl["k_norm_g"]), row(wl["btail"]), duty=take(10))
    cum4 = cum[:, FF_LANE0:FF_LANE0 + N_HEADS].T
    cumcol, cumrow = cum4.reshape(N_HEADS, T, 1), cum4.reshape(N_HEADS, 1, T)
    fox_o, lse = _fox_fwd(qkn, p, cumcol, cumrow, duty=take(30))
    w_br_t = _weight(wl, "w_br_t")
    ys = [_matmul(b, w_br_t[n], "nt", "br_proj%d" % n, out_dtype=BF16) for n, b in enumerate((br0, br1, fox_o))]
    gpre = _matmul(h, _weight(wl, "w_mg_t"), "nt", "gate_proj", out_dtype=BF16, duty=take(20))
    mixed_in = _mix_fwd(gpre, row(wl["b_mg"]), *ys, duty=take(21))
    mixed = _matmul(mixed_in, _weight(wl, "w_o"), "nn", "o_proj", duty=take(10))
    x1, h2 = _norm_fwd(x0, row(wl["norm2_g"]), row(wl["scale2"]), row(wl["shift2"]), "norm2_fwd",
                       m=mixed, gate=row(wl["gate1"]), duty=take(13))
    uu = _matmul(h2, _weight(wl, "w_up_t"), "nt", "up_proj", out_dtype=BF16, duty=take(30))
    act = _ffn_act_fwd(uu, wl["w_conv"], row(wl["b_conv"]), duty=take(25))
    y = _matmul(act, _weight(wl, "w_down"), "nn", "down_proj", tk=1408, duty=take(24))
    x2 = _axpy(x1, y, row(wl["gate2"]), "resid2", duty=take(11))
    saved = dict(x0=x0, h=h, p=p, tail=tail, qk=qk, ret_raw=ret_raw, br0=br0, gla_raw=gla_raw, states=states, br1=br1,
                 qkn=qkn, cumcol=cumcol, cumrow=cumrow, fox_o=fox_o, lse=lse, y0=ys[0], y1=ys[1], y2=ys[2],
                 gpre=gpre, mixed_in=mixed_in, mixed=mixed, x1=x1, h2=h2, uu=uu, act=act, y=y)
    return x2, saved


def _layer_bwd(dx2, wl, sv, consts, xfer=None, layer=0):
    cosf, sinf, logg = consts
    row = lambda a: a.reshape(1, -1)
    take = _taker(xfer)

    send = functools.partial(_send_grad, xfer, layer)

    dy, st_g2 = _gate_bwd(dx2, sv["y"], row(wl["gate2"]), "gate2_bwd", duty=take(10))
    dact = _matmul(dy, _weight(wl, "w_down"), "nt", "down_dx", tn=1408, duty=take(21))
    d_down = send("w_down", _matmul(sv["act"], dy, "tn", "down_dw", out_dtype=BF16, tm=1408, duty=take(19)))
    duu, st_conv = _ffn_act_bwd(sv["uu"], wl["w_conv"], row(wl["b_conv"]), dact, duty=take(40))
    dh2 = _matmul(duu, _weight(wl, "w_up_t"), "nn", "up_dx", tk=1408, duty=take(42))
    d_up_t = send("w_up", _matmul(duu, sv["h2"], "tn", "up_dw", out_dtype=BF16, tm=1408, duty=take(33)))
    dx1, st_n2 = _norm_bwd(sv["x1"], dh2, dx2, row(wl["norm2_g"]), row(wl["scale2"]), row(wl["shift2"]), "norm2_bwd",
                           duty=take(15))
    dmixed, st_g1 = _gate_bwd(dx1, sv["mixed"], row(wl["gate1"]), "gate1_bwd", duty=take(10))
    dmi = _matmul(dmixed, _weight(wl, "w_o"), "nt", "o_dx", duty=take(11))
    d_o = send("w_o", _matmul(sv["mixed_in"], dmixed, "tn", "o_dw", out_dtype=BF16, duty=take(9)))
    dy0, dy1, dy2, dgpre, st_bmg = _mix_bwd(sv["gpre"], row(wl["b_mg"]), sv["y0"], sv["y1"], sv["y2"], dmi,
                                             duty=take(31))
    brs = (sv["br0"], sv["br1"], sv["fox_o"])
    w_br_t = _weight(wl, "w_br_t")
    dbr = [_matmul(d, w_br_t[n], "nn", "br_dx%d" % n) for n, d in enumerate((dy0, dy1, dy2))]
    d_br_t = [send("w_br%d" % n, _matmul(d, brs[n], "tn", "br_dw%d" % n, out_dtype=BF16))
              for n, d in enumerate((dy0, dy1, dy2))]
    dh = _matmul(dgpre, _weight(wl, "w_mg_t"), "nn", "gate_dx", tk=1024, duty=take(29))
    d_mg_t = send("w_mg", _matmul(dgpre, sv["h"], "tn", "gate_dw", out_dtype=BF16, duty=take(21)))
    p, tail = sv["p"], sv["tail"]
    dqn, dkn, dfv, drow, dcol = _fox_bwd(sv["qkn"], p, sv["cumcol"], sv["cumrow"], sv["lse"], sv["fox_o"], dbr[2],
                                         duty=take(50))
    dcum4 = drow.reshape(N_HEADS, T) - dcol.reshape(N_HEADS, T)
    dcum = jnp.pad(dcum4.T, ((0, 0), (FF_LANE0, 128 - FF_LANE0 - N_HEADS)))
    dfqk, dtail_fox, st_fox = _fox_prep_bwd(p, tail, row(wl["q_norm_g"]), row(wl["k_norm_g"]), row(wl["btail"]), dqn, dkn, dcum,
                                            duty=take(15))
    dgla_raw, dgg, st_gn = _branch_post_bwd(sv["gla_raw"], p, row(wl["gla_norm_g"]), dbr[1], GG_BLK, False, "gla_post_bwd",
                                            duty=take(12))
    dgq, dgk, dgv, dtail_gla, dw2pad, st_bg = _gla_bwd(p, tail, wl["w2pad"], row(wl["b_gla_a"]), sv["states"], dgla_raw,
                                                       duty=take(30))
    dret_raw, drg, st_rn = _branch_post_bwd(sv["ret_raw"], p, row(wl["ret_norm_g"]), dbr[0], RG_BLK, True, "ret_post_bwd",
                                            duty=take(13))
    dqr, dkr, drv = _ret_bwd(sv["qk"], p, logg, dret_raw, duty=take(50))
    drqk = _rope_bwd(dqr, dkr, cosf, sinf, duty=take(11))
    dp = jnp.concatenate([a.astype(BF16) for a in (drqk, drv, drg, dgq, dgk, dgv, dgg, dfqk, dfv, dtail_fox + dtail_gla)]
                         + [jnp.zeros((T, NP - TAIL0 - 128), BF16)], axis=1)
    dh = _matmul(dp, _weight(wl, "w_in"), "nt", "in_dx", tk=1408, add=dh, duty=take(45))
    d_in = send("w_in", _matmul(sv["h"], dp, "tn", "in_dw", out_dtype=BF16, duty=take(32)))
    dx0, st_n1 = _norm_bwd(sv["x0"], dh, dx1, row(wl["norm1_g"]), row(wl["scale1"]), row(wl["shift1"]), "norm1_bwd",
                           duty=take(15))
    big = dict(w_in=d_in, w_o=d_o, w_down=d_down, w_br0=d_br_t[0], w_br1=d_br_t[1], w_br2=d_br_t[2], w_mg=d_mg_t,
               w_up=d_up_t)
    dmod = jnp.concatenate([st_n1[2], st_n1[1], st_g1[0], st_n2[2], st_n2[1], st_g2[0]])
    small = dict(norm1_g=st_n1[0], norm2_g=st_n2[0], b_gla_a=st_bg[0], b_fox_f=st_fox[2, FF_LANE0:FF_LANE0 + N_HEADS],
                 ret_norm_g=st_rn[0], gla_norm_g=st_gn[0], q_norm_g=st_fox[0], k_norm_g=st_fox[1], b_mg=st_bmg[0],
                 b_conv=st_conv[3], w_gla_a2=dw2pad[:LR_LANES], w_conv=st_conv[0:3])
    return dx0, big, dmod, small


SMALL_REPL = ("norm1_g", "norm2_g", "b_ada", "b_gla_a", "b_fox_f", "ret_norm_g", "gla_norm_g", "q_norm_g", "k_norm_g",
              "b_mg", "b_conv")
SMALL_SHARDED = ("w_gla_a2", "w_conv")
BIG = ("w_in", "w_o", "w_down", "w_br", "w_mg", "w_up")
WEIGHTS = ("norm1_g", "norm2_g", "w_ada", "b_ada", "w_in", "w_gla_a2", "b_gla_a", "b_fox_f", "ret_norm_g", "gla_norm_g",
           "q_norm_g", "k_norm_g", "w_br", "w_mg", "b_mg", "w_o", "w_up", "w_conv", "b_conv", "w_down")


def kernel(x, c, norm1_g, norm2_g, w_ada, b_ada, w_in, w_gla_a2, b_gla_a, b_fox_f, ret_norm_g, gla_norm_g, q_norm_g, k_norm_g, w_br, w_mg, b_mg, w_o, w_up, w_conv, b_conv, w_down, loss_target, m_norm1_g, m_norm2_g, m_w_ada, m_b_ada, m_w_in, m_w_gla_a2, m_b_gla_a, m_b_fox_f, m_ret_norm_g, m_gla_norm_g, m_q_norm_g, m_k_norm_g, m_w_br, m_w_mg, m_b_mg, m_w_o, m_w_up, m_w_conv, m_b_conv, m_w_down, v_norm1_g, v_norm2_g, v_w_ada, v_b_ada, v_w_in, v_w_gla_a2, v_b_gla_a, v_b_fox_f, v_ret_norm_g, v_gla_norm_g, v_q_norm_g, v_k_norm_g, v_w_br, v_w_mg, v_b_mg, v_w_o, v_w_up, v_w_conv, v_b_conv, v_w_down):
    W = dict(norm1_g=norm1_g, norm2_g=norm2_g, w_ada=w_ada, b_ada=b_ada, w_in=w_in, w_gla_a2=w_gla_a2, b_gla_a=b_gla_a,
             b_fox_f=b_fox_f, ret_norm_g=ret_norm_g, gla_norm_g=gla_norm_g, q_norm_g=q_norm_g, k_norm_g=k_norm_g,
             w_br=w_br, w_mg=w_mg, b_mg=b_mg, w_o=w_o, w_up=w_up, w_conv=w_conv, b_conv=b_conv, w_down=w_down)
    M = dict(norm1_g=m_norm1_g, norm2_g=m_norm2_g, w_ada=m_w_ada, b_ada=m_b_ada, w_in=m_w_in, w_gla_a2=m_w_gla_a2,
             b_gla_a=m_b_gla_a, b_fox_f=m_b_fox_f, ret_norm_g=m_ret_norm_g, gla_norm_g=m_gla_norm_g, q_norm_g=m_q_norm_g,
             k_norm_g=m_k_norm_g, w_br=m_w_br, w_mg=m_w_mg, b_mg=m_b_mg, w_o=m_w_o, w_up=m_w_up, w_conv=m_w_conv,
             b_conv=m_b_conv, w_down=m_w_down)
    V = dict(norm1_g=v_norm1_g, norm2_g=v_norm2_g, w_ada=v_w_ada, b_ada=v_b_ada, w_in=v_w_in, w_gla_a2=v_w_gla_a2,
             b_gla_a=v_b_gla_a, b_fox_f=v_b_fox_f, ret_norm_g=v_ret_norm_g, gla_norm_g=v_gla_norm_g, q_norm_g=v_q_norm_g,
             k_norm_g=v_k_norm_g, w_br=v_w_br, w_mg=v_w_mg, b_mg=v_b_mg, w_o=v_w_o, w_up=v_w_up, w_conv=v_w_conv,
             b_conv=v_b_conv, w_down=v_w_down)
    me = 4 * lax.axis_index("x") + 2 * lax.axis_index("y") + lax.axis_index("c")
    x2d, tgt = x.reshape(T, D), loss_target.reshape(T, D)

    sm = _flat_pack([c, w_gla_a2, w_conv])
    sm_all = _exchange(sm, True, "gather_small")
    parts = [_flat_unpack(sm_all[j], [(D,), (DEPTH, LR_LANES, 32), (DEPTH, 3, 352)]) for j in range(N_DEV)]
    c_all = jnp.stack([q[0] for q in parts])
    w_gla_full = jnp.concatenate([q[1] for q in parts], axis=2)
    w_conv_full = jnp.concatenate([q[2] for q in parts], axis=2)

    n_ada = w_ada.shape[2]
    b_loc = lax.dynamic_slice_in_dim(b_ada, me * n_ada, n_ada, axis=1).reshape(DEPTH, 1, n_ada)
    mod_all = _ada_fwd(c_all, w_ada, b_loc)
    mod_recv = _exchange(jnp.swapaxes(mod_all, 0, 1), False, "a2a_mod")
    mod = jnp.swapaxes(mod_recv, 0, 1).reshape(DEPTH, 6, D)

    loc = dict(w_in=_permute_in(w_in), w_o=w_o, w_down=w_down, w_br=jnp.swapaxes(w_br, 2, 3),
               w_mg=jnp.swapaxes(w_mg, 1, 2), w_up=jnp.swapaxes(w_up, 1, 2))
    loc = {k: v.astype(BF16) for k, v in loc.items()}
    w_full = dict(w_in=(D, NP), w_o=(D, D), w_down=(D_FF, D), w_br=(3, D, BW), w_mg=(3 * D, D), w_up=(2 * D_FF, D))
    w_parts = dict(w_in=8, w_br=2, w_mg=4, w_o=1, w_up=11, w_down=2)
    gather, units = _Transfers("gather"), []
    for l in range(DEPTH):
        for k, parts in w_parts.items():
            axis = 1 if k == "w_br" else 0
            n = w_full[k][axis] // N_DEV
            gather.lands[(l, k)] = lax.empty(w_full[k], BF16)
            shard = loc[k][l]
            nbytes = shard.size * 2 // parts
            units += [((l, k), (axis, n, (), c * (n // parts), n // parts), shard, nbytes) for c in range(parts)]
    first, lag = w_parts["w_in"], 4
    order = [("cross", i) for i in range(first)] + [("pass", i) for i in range(first)]
    for i in range(first, len(units) + lag):
        order += [("cross", i)] if i < len(units) else []
        order += [("pass", i - lag)] if i - lag >= first else []
    for what, i in order:
        key, where, shard, nbytes = units[i]
        if what == "cross":
            gather.add(key, ("gather_chip",) + where, shard, uid=i, nbytes=nbytes)
        else:
            gather.add(key, ("pass_on",) + where, after=i, nbytes=nbytes)

    w2pad = jnp.pad(w_gla_full, ((0, 0), (0, 128 - LR_LANES), (0, 0)))
    btail = jnp.pad(b_fox_f, ((0, 0), (FF_LANE0, 128 - FF_LANE0 - N_HEADS)))
    stacked = dict(norm1_g=norm1_g, norm2_g=norm2_g, b_gla_a=b_gla_a, ret_norm_g=ret_norm_g, gla_norm_g=gla_norm_g,
                   q_norm_g=q_norm_g, k_norm_g=k_norm_g, b_mg=b_mg, b_conv=b_conv, w_conv=w_conv_full, w2pad=w2pad,
                   btail=btail, shift1=mod[:, 0], scale1=mod[:, 1], gate1=mod[:, 2], shift2=mod[:, 3], scale2=mod[:, 4],
                   gate2=mod[:, 5])
    landed = lambda l, k: functools.partial(gather.get, (l, k))
    layers = [dict({k: v[l] for k, v in stacked.items()}, w_in=landed(l, "w_in"), w_o=landed(l, "w_o"),
                   w_down=landed(l, "w_down"), w_br_t=landed(l, "w_br"), w_mg_t=landed(l, "w_mg"), w_up_t=landed(l, "w_up"))
              for l in range(DEPTH)]
    consts = _rope_tables() + (_ret_logg(),)

    xc, saved = x2d, []
    for l in range(DEPTH):
        xc, sv = _layer_fwd(xc, layers[l], consts, gather, 1.7 if l == 0 else 1.45)
        saved.append(sv)
    loss_part, dxc = _loss_fwd_bwd(xc, tgt)
    loss = lax.psum(loss_part[0, 0], ("x", "y", "c"))

    grad_names = ("w_in", "w_o", "w_down", "w_br0", "w_br1", "w_br2", "w_mg", "w_up")
    blk_rows = dict(w_in=(128, NP), w_o=(128, D), w_down=(352, D), w_br0=(128, BW), w_br1=(128, BW), w_br2=(128, BW),
                    w_mg=(384, D), w_up=(704, D))
    grads = _Transfers("grads")
    for k in grad_names:
        grads.lands[k] = lax.empty((N_DEV // 2 if k in GRAD_VIA_CHIP else N_DEV, DEPTH) + blk_rows[k], BF16)
    dmod, small_g = [None] * DEPTH, [None] * DEPTH
    for l in reversed(range(DEPTH)):
        dxc, _, dmod[l], small_g[l] = _layer_bwd(dxc, layers[l], saved[l], consts, grads, l)
    grad_x = dxc
    dmod = jnp.stack(dmod)
    small_g = {k: jnp.stack([s[k] for s in small_g]) for k in small_g[0]}
    grads.drain()
    recv = {k: grads.get(k) for k in grad_names}

    dmod_send = jnp.swapaxes(dmod.reshape(DEPTH, N_DEV, n_ada), 0, 1)
    dmod_all = jnp.swapaxes(_exchange(dmod_send, False, "a2a_dmod"), 0, 1)
    g_ada = _ada_bwd(c_all, dmod_all)

    def flat(a, k):
        return a.reshape((-1, W[k].shape[-1]))

    def adam_nat(k, g, tr):
        outs = _adamw(g, flat(W[k], k), flat(M[k], k), flat(V[k], k), tr, "adamw_" + k)
        return [o.reshape(W[k].shape) for o in outs]

    def summed(k, tr):
        r = recv[k]
        return _sum_partials(r.reshape(r.shape[0], DEPTH * r.shape[2], r.shape[3]), "sum_" + k, tr).reshape((DEPTH,) + r.shape[2:])

    def adam_as_stored(k, g, perm, tr, tc=None):
        shape_t = tuple(W[k].shape[a] for a in perm)
        view = lambda a: jnp.transpose(a, perm).reshape(g.shape[-2:])
        outs = _adamw(g, view(W[k]), view(M[k]), view(V[k]), tr, "adamw_" + k, tc)
        return [jnp.transpose(o.reshape(shape_t), tuple(np.argsort(perm))) for o in outs]

    big_out = dict(
        w_in=adam_nat("w_in", flat(_unpermute_in(summed("w_in", 64)), "w_in"), 64),
        w_o=adam_nat("w_o", recv["w_o"].reshape(N_DEV, DEPTH * 128, D), 128),
        w_down=adam_nat("w_down", recv["w_down"].reshape(N_DEV // 2, DEPTH * 352, D), 352),
        w_br=adam_nat("w_br", flat(jnp.swapaxes(jnp.stack([summed("w_br%d" % n, 128) for n in range(3)], axis=1), 2, 3),
                                   "w_br"), 1024),
        w_mg=adam_nat("w_mg", flat(jnp.swapaxes(summed("w_mg", 384), 1, 2), "w_mg"), 512),
        w_up=adam_as_stored("w_up", recv["w_up"].reshape(N_DEV // 2, DEPTH * 704, D), (0, 2, 1), 352))
    ada_out = [o.reshape(DEPTH, D, n_ada) for o in _adamw(
        g_ada.reshape(DEPTH * D, n_ada), w_ada.reshape(DEPTH * D, n_ada), m_w_ada.reshape(DEPTH * D, n_ada),
        v_w_ada.reshape(DEPTH * D, n_ada), 512, "adamw_ada")]

    small_g = dict(small_g, b_ada=dmod)
    names = SMALL_REPL + SMALL_SHARDED
    full_shapes = [W[n].shape for n in SMALL_REPL] + [(DEPTH, LR_LANES, 256), (DEPTH, 3, D_FF)]
    part = _flat_pack([small_g[n] for n in names])
    total = _flat_unpack(_sum_partials(_exchange(part, True, "gather_small_grads"), "sum_small"), full_shapes)
    total = dict(zip(names, total))
    total["w_gla_a2"] = lax.dynamic_slice_in_dim(total["w_gla_a2"], me * 32, 32, axis=2)
    total["w_conv"] = lax.dynamic_slice_in_dim(total["w_conv"], me * 352, 352, axis=2)
    shapes = [W[n].shape for n in names]
    small_out = _adamw(_flat_pack([total[n] for n in names]), _flat_pack([W[n] for n in names]),
                       _flat_pack([M[n] for n in names]), _flat_pack([V[n] for n in names]), None, "adamw_small")
    small_out = [dict(zip(names, _flat_unpack(o, shapes))) for o in small_out]

    outs = []
    for k in range(4):
        d = dict(small_out[k])
        d.update({n: big_out[n][k] for n in BIG})
        d["w_ada"] = ada_out[k]
        outs.append([d[n] for n in WEIGHTS])
    return (loss, grad_x.reshape(1, T, D), *outs[0], *outs[1], *outs[2], *outs[3])
```

```python
import functools

import numpy as np
import jax
import jax.numpy as jnp
from jax import lax
from jax.experimental import pallas as pl
from jax.experimental.pallas import tpu as pltpu

F32 = jnp.float32
BF16 = jnp.bfloat16

N_DEV = 8
T = 2048
D = 1024
DEPTH = 4
N_HEADS = 4
HD = 128
BW = 512
D_FF = 2816
CHUNK = 64
EPS = 1e-6
IN_W = 5140
NP = 5632
TAIL0 = 5120
LR_LANES = 16
FF_LANE0 = 16
PACK_W = 1024
SEG_ROWS = (704, 128, 352, 192, 384, 704)
LAYER_ROWS = sum(SEG_ROWS)
VMEM_LIMIT_V7X = 56 * 1024 * 1024

ADAM_LR, ADAM_B1, ADAM_B2, ADAM_EPS, ADAM_WD, ADAM_STEP = 0.001, 0.9, 0.999, 1e-08, 0.01, 10

MESH_ID = pl.DeviceIdType.MESH


def _cp(*sem):
    return pltpu.CompilerParams(dimension_semantics=sem if sem else None, vmem_limit_bytes=VMEM_LIMIT_V7X)


def _sigmoid(z):
    return 1.0 / (1.0 + jnp.exp(-z))


def _log_sigmoid(z):
    return jnp.minimum(z, 0.0) - jnp.log(1.0 + jnp.exp(-jnp.abs(z)))


def _sum0(a):
    return jnp.sum(a, axis=0, keepdims=True)


def _mean1(a):
    return jnp.mean(a, axis=-1, keepdims=True)


def _dot(a, b, dims):
    return lax.dot_general(a.astype(BF16), b.astype(BF16), (dims, ((), ())), preferred_element_type=F32)


NN = ((1,), (0,))
NT = ((1,), (1,))
TN = ((0,), (0,))


def _exact_dot(m01, a):
    a1 = a.astype(BF16)
    r1 = a - a1.astype(F32)
    a2 = r1.astype(BF16)
    a3 = (r1 - a2.astype(F32)).astype(BF16)
    d = lambda z: jnp.dot(m01, z, preferred_element_type=F32)
    return d(a1) + d(a2) + d(a3)


def _tri(n, upper):
    r = lax.broadcasted_iota(jnp.int32, (n, n), 0)
    c = lax.broadcasted_iota(jnp.int32, (n, n), 1)
    return jnp.where((c >= r) if upper else (c <= r), 1.0, 0.0).astype(BF16)


def _exchange(x, gather, name):
    blk = x.shape if gather else x.shape[1:]

    def body(x_ref, o_ref, send_sems, recv_sems, loc_sem):
        mx, my, mc = lax.axis_index("x"), lax.axis_index("y"), lax.axis_index("c")
        me = 4 * mx + 2 * my + mc
        loc = pltpu.make_async_copy(x_ref if gather else x_ref.at[me], o_ref.at[me], loc_sem)
        loc.start()
        copies = []
        for k in range(1, N_DEV):
            px = mx ^ (k >> 2) if (k >> 2) else mx
            py = my ^ ((k >> 1) & 1) if ((k >> 1) & 1) else my
            pc = mc ^ (k & 1) if (k & 1) else mc
            peer = 4 * px + 2 * py + pc
            cp = pltpu.make_async_remote_copy(
                src_ref=x_ref if gather else x_ref.at[peer], dst_ref=o_ref.at[me],
                send_sem=send_sems.at[k - 1], recv_sem=recv_sems.at[k - 1],
                device_id=(px, py, pc), device_id_type=MESH_ID)
            cp.start()
            copies.append(cp)
        for cp in copies:
            cp.wait()
        loc.wait()

    return pl.pallas_call(
        body, name=name,
        out_shape=jax.ShapeDtypeStruct((N_DEV,) + tuple(blk), x.dtype),
        in_specs=[pl.BlockSpec(memory_space=pl.ANY)],
        out_specs=pl.BlockSpec(memory_space=pl.ANY),
        scratch_shapes=[pltpu.SemaphoreType.DMA((N_DEV - 1,)), pltpu.SemaphoreType.DMA((N_DEV - 1,)),
                        pltpu.SemaphoreType.DMA],
        compiler_params=pltpu.CompilerParams(has_side_effects=True),
    )(x)


def _blk(ref, axis, j, n, r0=0, nr=None):
    return ref.at[(slice(None),) * axis + (pl.ds(j * n + r0, n if nr is None else nr),)]


def _comm_copies(items, srcs, lands, send_sems, recv_sems, loc_sems):
    mx, my, mc = lax.axis_index("x"), lax.axis_index("y"), lax.axis_index("c")
    me = 4 * mx + 2 * my + mc
    local, remote = [], []
    for t, (kind, axis, n, sel, r0, nr, si, li) in enumerate(items):
        if kind == "pass_on":
            for q in (2, 4, 6):
                px = 1 - mx if q & 4 else mx
                py = 1 - my if q & 2 else my
                rows = _blk(lands[li], axis, 4 * px + 2 * py + mc, n, r0, nr)
                remote.append(pltpu.make_async_remote_copy(
                    src_ref=rows, dst_ref=rows, send_sem=send_sems.at[t * (N_DEV - 1) + q - 1],
                    recv_sem=recv_sems.at[t * (N_DEV - 1) + q - 1], device_id=(mx, my, 1 - mc), device_id_type=MESH_ID))
            continue
        if kind == "to_other_core":
            for p in range(N_DEV // 2):
                remote.append(pltpu.make_async_remote_copy(
                    src_ref=_blk(srcs[si], axis, 2 * p + 1 - mc, n, r0, nr), dst_ref=lands[li].at[p, pl.ds(r0, nr)],
                    send_sem=send_sems.at[t * (N_DEV - 1) + p], recv_sem=recv_sems.at[t * (N_DEV - 1) + p],
                    device_id=(mx, my, 1 - mc), device_id_type=MESH_ID))
            continue
        if kind == "a2a_chip":
            pm = 2 * mx + my
            mine = lands[li].at[(pm,) + tuple(sel) + (pl.ds(r0, nr),)]
            local.append(pltpu.make_async_copy(srcs[si].at[pm, pl.ds(r0, nr)], mine, loc_sems.at[t]))
            for q in (2, 4, 6):
                px = 1 - mx if q & 4 else mx
                py = 1 - my if q & 2 else my
                remote.append(pltpu.make_async_remote_copy(
                    src_ref=srcs[si].at[2 * px + py, pl.ds(r0, nr)], dst_ref=mine,
                    send_sem=send_sems.at[t * (N_DEV - 1) + q - 1], recv_sem=recv_sems.at[t * (N_DEV - 1) + q - 1],
                    device_id=(px, py, mc), device_id_type=MESH_ID))
            continue
        if kind == "a2a":
            mine = lands[li].at[(me,) + tuple(sel) + (pl.ds(r0, nr),)]
            own = _blk(srcs[si], axis, me, n, r0, nr)
        else:
            mine = _blk(lands[li], axis, me, n, r0, nr)
            own = _blk(srcs[si], axis, 0, n, r0, nr)
        local.append(pltpu.make_async_copy(own, mine, loc_sems.at[t]))
        for k in ((1, 2, 4, 6) if kind == "gather_chip" else range(1, N_DEV)):
            px = 1 - mx if k & 4 else mx
            py = 1 - my if k & 2 else my
            pc = 1 - mc if k & 1 else mc
            src = _blk(srcs[si], axis, 4 * px + 2 * py + pc, n, r0, nr) if kind == "a2a" else own
            remote.append(pltpu.make_async_remote_copy(
                src_ref=src, dst_ref=mine, send_sem=send_sems.at[t * (N_DEV - 1) + k - 1],
                recv_sem=recv_sems.at[t * (N_DEV - 1) + k - 1], device_id=(px, py, pc), device_id_type=MESH_ID))
    return local, remote


def _comm_scratch(n_items):
    return [pltpu.SemaphoreType.DMA((n_items * (N_DEV - 1),)), pltpu.SemaphoreType.DMA((n_items * (N_DEV - 1),)),
            pltpu.SemaphoreType.DMA((n_items,))]


LINK_BYTES_PER_US = dict(gather_chip=23e3, a2a_chip=23e3, a2a=11.5e3, gather=11.5e3, pass_on=200e3, to_other_core=150e3)
CALL_EXCHANGE_US = 3.0


class _Duty:
    def __init__(self, items, srcs, lands, done):
        self.items, self.srcs, self.lands, self.done = items, srcs, lands, done


def _pcall(body, name, grid, in_specs, out_specs, out_shape, args, scratch_shapes=(), sem=(), duty=None):
    if duty is None:
        return pl.pallas_call(body, name=name, grid=grid, in_specs=list(in_specs), out_specs=out_specs,
                              out_shape=out_shape, scratch_shapes=list(scratch_shapes), compiler_params=_cp(*sem))(*args)
    single = not isinstance(out_shape, (list, tuple))
    o_shape = [out_shape] if single else list(out_shape)
    o_specs = [out_specs] if single else list(out_specs)
    n_in, n_out, n_scr = len(in_specs), len(o_shape), len(scratch_shapes)
    n_src, n_land, n_items = len(duty.srcs), len(duty.lands), len(duty.items)
    a0 = n_in + n_src + n_land

    def wrapped(*refs):
        srcs = refs[n_in:n_in + n_src]
        lands = refs[a0 + n_out:a0 + n_out + n_land]
        core = refs[:n_in] + refs[a0:a0 + n_out] + refs[a0 + n_out + n_land:a0 + n_out + n_land + n_scr]
        sems = refs[a0 + n_out + n_land + n_scr:]
        first = functools.reduce(jnp.logical_and, [pl.program_id(a) == 0 for a in range(len(grid))])
        last = functools.reduce(jnp.logical_and, [pl.program_id(a) == g - 1 for a, g in enumerate(grid)])

        @pl.when(first)
        def _():
            local, remote = _comm_copies(duty.items, srcs, lands, *sems)
            for cp in local + remote:
                cp.start()

        body(*core)

        @pl.when(last)
        def _():
            local, remote = _comm_copies(duty.items, srcs, lands, *sems)
            for cp in remote + local:
                cp.wait()

    hbm = pl.BlockSpec(memory_space=pl.ANY)
    res = pl.pallas_call(
        wrapped, name=name, grid=grid,
        in_specs=list(in_specs) + [hbm] * (n_src + n_land), out_specs=o_specs + [hbm] * n_land,
        out_shape=o_shape + [jax.ShapeDtypeStruct(a.shape, a.dtype) for a in duty.lands],
        input_output_aliases={n_in + n_src + t: n_out + t for t in range(n_land)},
        scratch_shapes=list(scratch_shapes) + _comm_scratch(n_items),
        compiler_params=pltpu.CompilerParams(dimension_semantics=("arbitrary",) * len(grid),
                                             vmem_limit_bytes=VMEM_LIMIT_V7X, has_side_effects=True),
    )(*args, *duty.srcs, *duty.lands)
    duty.done(res[n_out:])
    return res[0] if single else res[:n_out]


def _comm(duty, name):
    n_src, n_land = len(duty.srcs), len(duty.lands)

    def body(*refs):
        local, remote = _comm_copies(duty.items, refs[:n_src], refs[n_src + n_land:n_src + 2 * n_land],
                                     *refs[n_src + 2 * n_land:])
        for cp in local + remote:
            cp.start()
        for cp in remote + local:
            cp.wait()

    hbm = pl.BlockSpec(memory_space=pl.ANY)
    duty.done(pl.pallas_call(
        body, name=name, out_shape=[jax.ShapeDtypeStruct(a.shape, a.dtype) for a in duty.lands],
        in_specs=[hbm] * (n_src + n_land), out_specs=[hbm] * n_land,
        input_output_aliases={n_src + t: t for t in range(n_land)},
        scratch_shapes=_comm_scratch(len(duty.items)), compiler_params=pltpu.CompilerParams(has_side_effects=True),
    )(*duty.srcs, *duty.lands))


class _Transfers:
    def __init__(self, name):
        self.name, self.queue, self.lands, self.flushes, self.groups = name, [], {}, 0, {}

    def add(self, key, item, src=None, uid=None, after=None, group=None, nbytes=0):
        self.queue.append((key, item, src, uid, after, group, nbytes / LINK_BYTES_PER_US[item[0]]))
        if group is not None:
            self.groups[group] = [self.groups.get(group, [0, None])[0] + 1, None]

    def when_done(self, group, fn):
        self.groups[group][1] = fn

    def take_for(self, us):
        count, busy = 0, CALL_EXCHANGE_US
        while count < len(self.queue) and busy + self.queue[count][6] <= us:
            busy += self.queue[count][6]
            count += 1
        return self.take(count) if count else None

    def take(self, count):
        units = []
        while self.queue and len(units) < count:
            after = self.queue[0][4]
            if after is not None and any(u[3] == after for u in units):
                break
            units.append(self.queue.pop(0))
        if not units:
            return None
        keys, srcs, items = [], [], []
        for key, item, src, _, _, _, _ in units:
            if key not in keys:
                keys.append(key)
            if src is not None and not any(src is s for s in srcs):
                srcs.append(src)
            si = [i for i, s in enumerate(srcs) if s is src][0] if src is not None else -1
            items.append(tuple(item) + (si, keys.index(key)))

        def done(new_lands):
            for key, arr in zip(keys, new_lands):
                self.lands[key] = arr
            for u in units:
                if u[5] is not None:
                    self.groups[u[5]][0] -= 1
                    if self.groups[u[5]][0] == 0:
                        self.groups[u[5]][1]()

        return _Duty(items, srcs, [self.lands[k] for k in keys], done)

    def drain(self, upto=None):
        count = upto
        while self.queue if upto is None else count > 0:
            duty = self.take(len(self.queue) if upto is None else count)
            count = None if upto is None else count - len(duty.items)
            self.flushes += 1
            _comm(duty, "%s_flush%d" % (self.name, self.flushes))

    def get(self, key):
        pending = [i for i, u in enumerate(self.queue) if u[0] == key]
        if pending:
            self.drain(pending[-1] + 1)
        return self.lands[key]


def _matmul(a, b, mode, name, out_dtype=F32, tm=1024, tn=512, tk=None, add=None, n_blocks=None, duty=None):
    halves = a.ndim == 3
    if mode == "tn":
        K, M = a.shape[-2], a.shape[-1] * (2 if halves else 1)
        N = b.shape[1]
    else:
        M, K = a.shape[-2], a.shape[-1] * (2 if halves else 1)
        N = b.shape[0] if mode == "nt" else b.shape[1]
    tm, tn = min(tm, M), min(tn, N)
    j0 = 0
    if n_blocks is not None:
        j0, N = n_blocks[0], n_blocks[1] * tn
    tk = K if tk is None else tk
    nk = K // tk
    assert M % tm == 0 and N % tn == 0 and K % tk == 0, (name, M, N, K, tm, tn, tk)
    dims = {"nn": NN, "nt": NT, "tn": TN}[mode]
    has_add = add is not None

    def body(*refs):
        a_ref, b_ref = refs[:2]
        add_ref = refs[2] if has_add else None
        o_ref = refs[3 if has_add else 2]
        part = _dot(a_ref[...], b_ref[...], dims)

        def finish(total):
            if has_add:
                total = total + add_ref[...]
            o_ref[...] = total.astype(o_ref.dtype)

        if nk == 1:
            finish(part)
            return
        acc_ref = refs[-1]
        k = pl.program_id(2)

        @pl.when(k == 0)
        def _():
            acc_ref[...] = part

        @pl.when((k > 0) & (k < nk - 1))
        def _():
            acc_ref[...] += part

        @pl.when(k == nk - 1)
        def _():
            finish(acc_ref[...] + part)

    if halves and mode == "tn":
        per = a.shape[-1] // tm
        a_spec = pl.BlockSpec((None, tk, tm), lambda i, j, k: (i // per, k, i % per))
    elif halves:
        per = a.shape[-1] // tk
        a_spec = pl.BlockSpec((None, tm, tk), lambda i, j, k: (k // per, i, k % per))
    elif mode == "tn":
        a_spec = pl.BlockSpec((tk, tm), lambda i, j, k: (k, i))
    else:
        a_spec = pl.BlockSpec((tm, tk), lambda i, j, k: (i, k))
    if mode == "nt":
        b_spec = pl.BlockSpec((tn, tk), lambda i, j, k: (j0 + j, k))
    else:
        b_spec = pl.BlockSpec((tk, tn), lambda i, j, k: (k, j0 + j))
    o_spec = pl.BlockSpec((tm, tn), lambda i, j, k: (i, j))
    in_specs = [a_spec, b_spec] + ([o_spec] if has_add else [])
    args = (a, b) + ((add,) if has_add else ())
    return _pcall(
        body, name=name, grid=(M // tm, N // tn, nk),
        out_shape=jax.ShapeDtypeStruct((M, N), out_dtype),
        in_specs=in_specs, out_specs=o_spec,
        scratch_shapes=[pltpu.VMEM((tm, tn), F32)] if nk > 1 else [],
        sem=("parallel", "parallel", "arbitrary"), args=args, duty=duty)


def _ada_fwd(c_all, w_ada, b_loc):
    n = w_ada.shape[2]

    def body(c_ref, w_ref, b_ref, o_ref):
        c = c_ref[...]
        o_ref[0] = _dot(c * _sigmoid(c), w_ref[0], NN) + b_ref[0]

    return pl.pallas_call(
        body, name="ada_fwd", grid=(DEPTH,),
        out_shape=jax.ShapeDtypeStruct((DEPTH, N_DEV, n), F32),
        in_specs=[pl.BlockSpec((N_DEV, D), lambda l: (0, 0)),
                  pl.BlockSpec((1, D, n), lambda l: (l, 0, 0)),
                  pl.BlockSpec((1, 1, n), lambda l: (l, 0, 0))],
        out_specs=pl.BlockSpec((1, N_DEV, n), lambda l: (l, 0, 0)),
        compiler_params=_cp("parallel"),
    )(c_all, w_ada, b_loc)


def _ada_bwd(c_all, dmod_all):
    n = dmod_all.shape[2]

    def body(c_ref, d_ref, o_ref):
        c = c_ref[...]
        o_ref[0] = _dot(c * _sigmoid(c), d_ref[0], TN)

    return pl.pallas_call(
        body, name="ada_bwd", grid=(DEPTH,),
        out_shape=jax.ShapeDtypeStruct((DEPTH, D, n), F32),
        in_specs=[pl.BlockSpec((N_DEV, D), lambda l: (0, 0)),
                  pl.BlockSpec((1, N_DEV, n), lambda l: (l, 0, 0))],
        out_specs=pl.BlockSpec((1, D, n), lambda l: (l, 0, 0)),
        compiler_params=_cp("parallel"),
    )(c_all, dmod_all)


ROW_TILE = 512


def _row_spec(w=D, col=0):
    return pl.BlockSpec((ROW_TILE, w), lambda i: (i, col))


def _vec_spec(w=D):
    return pl.BlockSpec((1, w), lambda i: (0, 0))


def _norm_fwd(x, g, scale, shift, name, m=None, gate=None, duty=None):
    has_res = m is not None

    def body(*refs):
        if has_res:
            x_ref, m_ref, gate_ref, g_ref, sc_ref, sh_ref, xo_ref, h_ref = refs
            xv = x_ref[...] + gate_ref[...] * m_ref[...]
            xo_ref[...] = xv
        else:
            x_ref, g_ref, sc_ref, sh_ref, h_ref = refs
            xv = x_ref[...]
        r = lax.rsqrt(_mean1(xv * xv) + EPS)
        h_ref[...] = ((xv * r * g_ref[...]) * (1.0 + sc_ref[...]) + sh_ref[...]).astype(BF16)

    ins = [x] + ([m, gate] if has_res else []) + [g, scale, shift]
    in_specs = [_row_spec()] + ([_row_spec(), _vec_spec()] if has_res else []) + [_vec_spec()] * 3
    out_shape = [jax.ShapeDtypeStruct((T, D), BF16)]
    out_specs = [_row_spec()]
    if has_res:
        out_shape = [jax.ShapeDtypeStruct((T, D), F32)] + out_shape
        out_specs = [_row_spec()] + out_specs
    out = _pcall(body, name=name, grid=(T // ROW_TILE,), out_shape=out_shape, in_specs=in_specs,
                 out_specs=out_specs, sem=("parallel",), args=ins, duty=duty)
    return out if has_res else out[0]


def _norm_bwd(x, dh, dres, g, scale, shift, name, duty=None):
    def body(x_ref, dh_ref, dres_ref, g_ref, sc_ref, sh_ref, dx_ref, st_ref):
        xv, dh_v, gv = x_ref[...], dh_ref[...], g_ref[...]
        r = lax.rsqrt(_mean1(xv * xv) + EPS)
        n = xv * r
        dy = dh_v * (1.0 + sc_ref[...])
        dn = dy * gv
        dx_ref[...] = r * (dn - n * _mean1(dn * n)) + dres_ref[...]

        @pl.when(pl.program_id(0) == 0)
        def _():
            st_ref[...] = jnp.zeros_like(st_ref)

        st_ref[0:1, :] += _sum0(dy * n)
        st_ref[1:2, :] += _sum0(dh_v * (n * gv))
        st_ref[2:3, :] += _sum0(dh_v)

    return _pcall(
        body, name=name, grid=(T // ROW_TILE,),
        out_shape=[jax.ShapeDtypeStruct((T, D), F32), jax.ShapeDtypeStruct((8, D), F32)],
        in_specs=[_row_spec(), _row_spec(), _row_spec(), _vec_spec(), _vec_spec(), _vec_spec()],
        out_specs=[_row_spec(), pl.BlockSpec((8, D), lambda i: (0, 0))],
        sem=("arbitrary",), args=(x, dh, dres, g, scale, shift), duty=duty)


def _axpy(x, m, gate, name, duty=None):
    def body(x_ref, m_ref, gate_ref, o_ref):
        o_ref[...] = x_ref[...] + gate_ref[...] * m_ref[...]

    return _pcall(
        body, name=name, grid=(T // ROW_TILE,), out_shape=jax.ShapeDtypeStruct((T, D), F32),
        in_specs=[_row_spec(), _row_spec(), _vec_spec()], out_specs=_row_spec(),
        sem=("parallel",), args=(x, m, gate), duty=duty)


def _gate_bwd(dx, m, gate, name, duty=None):
    def body(dx_ref, m_ref, gate_ref, dm_ref, st_ref):
        dxv = dx_ref[...]
        dm_ref[...] = (gate_ref[...] * dxv).astype(BF16)

        @pl.when(pl.program_id(0) == 0)
        def _():
            st_ref[...] = jnp.zeros_like(st_ref)

        st_ref[0:1, :] += _sum0(dxv * m_ref[...])

    return _pcall(
        body, name=name, grid=(T // ROW_TILE,),
        out_shape=[jax.ShapeDtypeStruct((T, D), BF16), jax.ShapeDtypeStruct((8, D), F32)],
        in_specs=[_row_spec(), _row_spec(), _vec_spec()],
        out_specs=[_row_spec(), pl.BlockSpec((8, D), lambda i: (0, 0))],
        sem=("arbitrary",), args=(dx, m, gate), duty=duty)


def _loss_fwd_bwd(y, target):
    def body(y_ref, t_ref, l_ref, d_ref):
        e = y_ref[...] - t_ref[...]
        d_ref[...] = e * (1.0 / D)

        @pl.when(pl.program_id(0) == 0)
        def _():
            l_ref[...] = jnp.zeros_like(l_ref)

        l_ref[...] += jnp.sum(_sum0(e * e), axis=1, keepdims=True) * (0.5 / D)

    return pl.pallas_call(
        body, name="loss", grid=(T // ROW_TILE,),
        out_shape=[jax.ShapeDtypeStruct((8, 128), F32), jax.ShapeDtypeStruct((T, D), F32)],
        in_specs=[_row_spec(), _row_spec()],
        out_specs=[pl.BlockSpec((8, 128), lambda i: (0, 0)), _row_spec()],
        compiler_params=_cp("arbitrary"))(y, target)


def _rope_tables():
    half = HD // 2
    inv_freq = 10000.0 ** (-jnp.arange(half, dtype=F32) / half)
    ang = jnp.arange(T, dtype=F32)[:, None] * inv_freq[None, :]
    cos, sin = jnp.cos(ang), jnp.sin(ang)
    return jnp.concatenate([cos, cos], axis=1), jnp.concatenate([-sin, sin], axis=1)


def _rope_fwd(p, cosf, sinf, duty=None):
    def body(p_ref, c_ref, s_ref, o_ref):
        cv, sv = c_ref[...], s_ref[...]
        for j in range(2 * N_HEADS):
            xv = p_ref[:, j * HD:(j + 1) * HD].astype(F32)
            rot = xv * cv + pltpu.roll(xv, HD // 2, 1) * sv
            if j >= N_HEADS:
                rot = rot * (HD ** -0.5)
            o_ref[:, j * HD:(j + 1) * HD] = rot.astype(BF16)

    return _pcall(
        body, name="rope_fwd", grid=(T // ROW_TILE,),
        out_shape=jax.ShapeDtypeStruct((T, 2 * BW), BF16),
        in_specs=[_row_spec(2 * BW), _row_spec(HD), _row_spec(HD)], out_specs=_row_spec(2 * BW),
        sem=("parallel",), args=(p, cosf, sinf), duty=duty)


def _rope_bwd(dq, dk, cosf, sinf, duty=None):
    def body(dq_ref, dk_ref, c_ref, s_ref, o_ref):
        cv, sv = c_ref[...], s_ref[...]
        for j in range(2 * N_HEADS):
            h = j % N_HEADS
            d = dq_ref[:, h * HD:(h + 1) * HD] if j < N_HEADS else dk_ref[:, h * HD:(h + 1) * HD] * (HD ** -0.5)
            o_ref[:, j * HD:(j + 1) * HD] = d * cv + pltpu.roll(d * sv, HD // 2, 1)

    return _pcall(
        body, name="rope_bwd", grid=(T // ROW_TILE,),
        out_shape=jax.ShapeDtypeStruct((T, 2 * BW), F32),
        in_specs=[_row_spec(BW), _row_spec(BW), _row_spec(HD), _row_spec(HD)], out_specs=_row_spec(2 * BW),
        sem=("parallel",), args=(dq, dk, cosf, sinf), duty=duty)


TQ = 512
V_RET_BLK = 8


def _ret_logg():
    lg = jnp.log1p(-jnp.exp2(-5.0 - jnp.arange(N_HEADS, dtype=F32)))
    return jnp.broadcast_to(lg[:, None, None], (N_HEADS, 1, 128))


def _block_iotas(i, kl):
    rows = lax.broadcasted_iota(jnp.int32, (TQ, kl), 0) + i * TQ
    cols = lax.broadcasted_iota(jnp.int32, (TQ, kl), 1)
    return rows, cols


def _ret_weight(lg_ref, i, kl):
    rows, cols = _block_iotas(i, kl)
    dist = jnp.abs(rows - cols).astype(F32)
    w = jnp.exp(dist * lg_ref[0][:, 0:1])
    return jnp.where((cols >> 6) <= (rows >> 6), w, 0.0)


def _per_query_block(i, fn):
    for n in range(1, T // TQ + 1):
        pl.when(i == n - 1)(functools.partial(fn, n * TQ))


def _ret_specs():
    q_spec = pl.BlockSpec((TQ, HD), lambda h, i: (i, h))
    k_spec = pl.BlockSpec((T, HD), lambda h, i: (0, N_HEADS + h))
    v_spec = pl.BlockSpec((T, HD), lambda h, i: (0, V_RET_BLK + h))
    lg_spec = pl.BlockSpec((1, 1, 128), lambda h, i: (h, 0, 0))
    return q_spec, k_spec, v_spec, lg_spec


def _ret_fwd(qk, p, logg, duty=None):
    def body(q_ref, k_ref, v_ref, lg_ref, o_ref):
        i = pl.program_id(1)

        def visible(kl):
            s = _dot(q_ref[...], k_ref[0:kl, :], NT) * _ret_weight(lg_ref, i, kl)
            o_ref[...] = _dot(s, v_ref[0:kl, :], NN)

        _per_query_block(i, visible)

    q_spec, k_spec, v_spec, lg_spec = _ret_specs()
    return _pcall(
        body, name="ret_fwd", grid=(N_HEADS, T // TQ),
        out_shape=jax.ShapeDtypeStruct((T, BW), F32),
        in_specs=[q_spec, k_spec, v_spec, lg_spec], out_specs=q_spec,
        sem=("parallel", "parallel"), args=(qk, qk, p, logg), duty=duty)


def _ret_bwd(qk, p, logg, do, duty=None):
    def body(q_ref, k_ref, v_ref, lg_ref, do_ref, dq_ref, dk_ref, dv_ref):
        i = pl.program_id(1)
        q, dov = q_ref[...], do_ref[...]

        @pl.when(i == 0)
        def _():
            dk_ref[...] = jnp.zeros_like(dk_ref)
            dv_ref[...] = jnp.zeros_like(dv_ref)

        def visible(kl):
            w = _ret_weight(lg_ref, i, kl)
            k = k_ref[0:kl, :]
            s = _dot(q, k, NT) * w
            ds = _dot(dov, v_ref[0:kl, :], NT) * w
            dk_ref[0:kl, :] += _dot(ds, q, TN)
            dv_ref[0:kl, :] += _dot(s, dov, TN)
            dq_ref[...] = _dot(ds, k, NN)

        _per_query_block(i, visible)

    q_spec, k_spec, v_spec, lg_spec = _ret_specs()
    acc_spec = pl.BlockSpec((T, HD), lambda h, i: (0, h))
    sh = jax.ShapeDtypeStruct((T, BW), F32)
    return _pcall(
        body, name="ret_bwd", grid=(N_HEADS, T // TQ),
        out_shape=[sh, sh, sh],
        in_specs=[q_spec, k_spec, v_spec, lg_spec, q_spec], out_specs=[q_spec, acc_spec, acc_spec],
        sem=("parallel", "arbitrary"), args=(qk, qk, p, logg, do), duty=duty)


def _post_norm(xv, gv, centered):
    if centered:
        xv = xv - _mean1(xv)
    r = lax.rsqrt(_mean1(xv * xv) + EPS)
    return xv * r, r


def _branch_post_fwd(raw, p, g, gate_blk, centered, name, duty=None):
    def body(raw_ref, z_ref, g_ref, o_ref):
        for h in range(N_HEADS):
            sl = slice(h * HD, (h + 1) * HD)
            gv = g_ref[:, sl] if centered else g_ref[...]
            xh, _ = _post_norm(raw_ref[:, sl], gv, centered)
            z = z_ref[:, sl].astype(F32)
            o_ref[:, sl] = (z * _sigmoid(z) * (xh * gv)).astype(BF16)

    return _pcall(
        body, name=name, grid=(T // ROW_TILE,),
        out_shape=jax.ShapeDtypeStruct((T, BW), BF16),
        in_specs=[_row_spec(BW), _row_spec(BW, gate_blk), _vec_spec(BW if centered else HD)],
        out_specs=_row_spec(BW), sem=("parallel",), args=(raw, p, g), duty=duty)


def _branch_post_bwd(raw, p, g, dout, gate_blk, centered, name, duty=None):
    gw = BW if centered else HD

    def body(raw_ref, z_ref, g_ref, do_ref, dr_ref, dz_ref, dg_ref):
        @pl.when(pl.program_id(0) == 0)
        def _():
            dg_ref[...] = jnp.zeros_like(dg_ref)

        for h in range(N_HEADS):
            sl = slice(h * HD, (h + 1) * HD)
            gsl = sl if centered else slice(0, HD)
            gv, z, dov = g_ref[:, gsl], z_ref[:, sl].astype(F32), do_ref[:, sl]
            xh, r = _post_norm(raw_ref[:, sl], gv, centered)
            sg = _sigmoid(z)
            dyn = dov * (z * sg)
            dz_ref[:, sl] = dov * (xh * gv) * (sg * (1.0 + z * (1.0 - sg)))
            dxh = dyn * gv
            t = dxh - xh * _mean1(dxh * xh)
            if centered:
                t = t - _mean1(dxh)
            dr_ref[:, sl] = r * t
            dg_ref[0:1, gsl] += _sum0(dyn * xh)

    return _pcall(
        body, name=name, grid=(T // ROW_TILE,),
        out_shape=[jax.ShapeDtypeStruct((T, BW), F32), jax.ShapeDtypeStruct((T, BW), F32),
                   jax.ShapeDtypeStruct((8, gw), F32)],
        in_specs=[_row_spec(BW), _row_spec(BW, gate_blk), _vec_spec(gw), _row_spec(BW)],
        out_specs=[_row_spec(BW), _row_spec(BW), pl.BlockSpec((8, gw), lambda i: (0, 0))],
        sem=("arbitrary",), args=(raw, p, g, dout), duty=duty)


GLA_ROWS = 256
GLA_CPB = GLA_ROWS // CHUNK
GLA_DK = 64
GLA_W = N_HEADS * GLA_DK
GQ_BLK, GK_BLK, GV_BLK, GG_BLK = 8, 9, 5, 6
TAIL_BLK = TAIL0 // 128
RG_BLK = 3


def _gla_chunk_common(tl, w2, bv, kv):
    pre = _dot(tl, w2, NN) + bv
    la = _log_sigmoid(pre) * (1.0 / 16.0)
    bc = _exact_dot(_tri(CHUNK, False), la)
    be = bc[CHUNK - 1:CHUNK, :]
    w = jnp.exp(be - bc)
    return pre, w, jnp.exp(be), kv * w


def _head_masks():
    lane = lax.broadcasted_iota(jnp.int32, (1, GLA_W), 1)
    return [jnp.where((lane // GLA_DK) == h, 1.0, 0.0) for h in range(N_HEADS)]


def _gla_fwd(p, tail, w2pad, b, duty=None):
    nb = T // GLA_ROWS

    def body(q_ref, k_ref, v_ref, t_ref, w2_ref, b_ref, o_ref, st_ref, s_acc):
        @pl.when(pl.program_id(0) == 0)
        def _():
            s_acc[...] = jnp.zeros_like(s_acc)

        masks = _head_masks()
        for c in range(GLA_CPB):
            rows = slice(c * CHUNK, (c + 1) * CHUNK)
            _, _, a, kd = _gla_chunk_common(t_ref[rows, :], w2_ref[...], b_ref[...], k_ref[rows, :].astype(F32))
            q = q_ref[rows, :].astype(F32) * (GLA_DK ** -0.5)
            kv = None
            for h in range(N_HEADS):
                t = _dot(v_ref[rows, h * HD:(h + 1) * HD], kd * masks[h], TN)
                kv = t if kv is None else kv + t
            s_new = s_acc[...] * a + kv
            s_acc[...] = s_new
            st_ref[c] = s_new
            for h in range(N_HEADS):
                o_ref[rows, h * HD:(h + 1) * HD] = _dot(q * masks[h], s_new, NT)

    return _pcall(
        body, name="gla_fwd", grid=(nb,),
        out_shape=[jax.ShapeDtypeStruct((T, BW), F32), jax.ShapeDtypeStruct((T // CHUNK, HD, GLA_W), F32)],
        in_specs=[pl.BlockSpec((GLA_ROWS, GLA_W), lambda i: (i, GQ_BLK)),
                  pl.BlockSpec((GLA_ROWS, GLA_W), lambda i: (i, GK_BLK)),
                  pl.BlockSpec((GLA_ROWS, BW), lambda i: (i, GV_BLK)),
                  pl.BlockSpec((GLA_ROWS, 128), lambda i: (i, 0)),
                  pl.BlockSpec((128, GLA_W), lambda i: (0, 0)),
                  pl.BlockSpec((1, GLA_W), lambda i: (0, 0))],
        out_specs=[pl.BlockSpec((GLA_ROWS, BW), lambda i: (i, 0)),
                   pl.BlockSpec((GLA_CPB, HD, GLA_W), lambda i: (i, 0, 0))],
        scratch_shapes=[pltpu.VMEM((HD, GLA_W), F32)],
        sem=("arbitrary",), args=(p, p, p, tail, w2pad, b), duty=duty)


def _gla_bwd(p, tail, w2pad, b, states, do, duty=None):
    nb = T // GLA_ROWS

    def body(q_ref, k_ref, v_ref, t_ref, w2_ref, b_ref, st_ref, prev_ref, do_ref,
             dq_ref, dk_ref, dv_ref, dt_ref, dw2_ref, db_ref, ds_acc):
        step = pl.program_id(0)

        @pl.when(step == 0)
        def _():
            ds_acc[...] = jnp.zeros_like(ds_acc)
            dw2_ref[...] = jnp.zeros_like(dw2_ref)
            db_ref[...] = jnp.zeros_like(db_ref)

        masks = _head_masks()
        up = _tri(CHUNK, True)
        has_prev = jnp.where(step == nb - 1, 0.0, 1.0)
        for c in reversed(range(GLA_CPB)):
            rows = slice(c * CHUNK, (c + 1) * CHUNK)
            tl, w2, k = t_ref[rows, :], w2_ref[...], k_ref[rows, :].astype(F32)
            pre, w, a, kd = _gla_chunk_common(tl, w2, b_ref[...], k)
            q = q_ref[rows, :].astype(F32) * (GLA_DK ** -0.5)
            s_n = st_ref[c]
            s_prev = st_ref[c - 1] if c > 0 else prev_ref[0] * has_prev
            ds = ds_acc[...]
            dos = [do_ref[rows, h * HD:(h + 1) * HD] for h in range(N_HEADS)]
            for h in range(N_HEADS):
                ds = ds + _dot(dos[h], q * masks[h], TN)
            dqp = jnp.zeros((CHUNK, GLA_W), F32)
            dkd = jnp.zeros((CHUNK, GLA_W), F32)
            for h in range(N_HEADS):
                dqp = dqp + masks[h] * _dot(dos[h], s_n, NN)
                dkd = dkd + masks[h] * _dot(v_ref[rows, h * HD:(h + 1) * HD], ds, NN)
                dv_ref[rows, h * HD:(h + 1) * HD] = _dot(kd * masks[h], ds, NT)
            dq_ref[rows, :] = dqp * (GLA_DK ** -0.5)
            dk_ref[rows, :] = dkd * w
            e = dkd * k * w
            dbe = _sum0(e) + _sum0(ds * s_prev) * a
            dla = dbe - _exact_dot(up, e)
            dpre = dla * (1.0 / 16.0) * _sigmoid(-pre)
            db_ref[0:1, :] += _sum0(dpre)
            dw2_ref[...] += _dot(tl, dpre, TN)
            dt_ref[rows, :] = _dot(dpre, w2, NT)
            ds_acc[...] = ds * a

    rev = lambda i: nb - 1 - i
    sh = lambda w: jax.ShapeDtypeStruct((T, w), F32)
    return _pcall(
        body, name="gla_bwd", grid=(nb,),
        out_shape=[sh(GLA_W), sh(GLA_W), sh(BW), sh(128), jax.ShapeDtypeStruct((128, GLA_W), F32),
                   jax.ShapeDtypeStruct((8, GLA_W), F32)],
        in_specs=[pl.BlockSpec((GLA_ROWS, GLA_W), lambda i: (rev(i), GQ_BLK)),
                  pl.BlockSpec((GLA_ROWS, GLA_W), lambda i: (rev(i), GK_BLK)),
                  pl.BlockSpec((GLA_ROWS, BW), lambda i: (rev(i), GV_BLK)),
                  pl.BlockSpec((GLA_ROWS, 128), lambda i: (rev(i), 0)),
                  pl.BlockSpec((128, GLA_W), lambda i: (0, 0)),
                  pl.BlockSpec((1, GLA_W), lambda i: (0, 0)),
                  pl.BlockSpec((GLA_CPB, HD, GLA_W), lambda i: (rev(i), 0, 0)),
                  pl.BlockSpec((1, HD, GLA_W), lambda i: (jnp.maximum(rev(i) * GLA_CPB - 1, 0), 0, 0)),
                  pl.BlockSpec((GLA_ROWS, BW), lambda i: (rev(i), 0))],
        out_specs=[pl.BlockSpec((GLA_ROWS, GLA_W), lambda i: (rev(i), 0)),
                   pl.BlockSpec((GLA_ROWS, GLA_W), lambda i: (rev(i), 0)),
                   pl.BlockSpec((GLA_ROWS, BW), lambda i: (rev(i), 0)),
                   pl.BlockSpec((GLA_ROWS, 128), lambda i: (rev(i), 0)),
                   pl.BlockSpec((128, GLA_W), lambda i: (0, 0)),
                   pl.BlockSpec((8, GLA_W), lambda i: (0, 0))],
        scratch_shapes=[pltpu.VMEM((HD, GLA_W), F32)],
        sem=("arbitrary",), args=(p, p, p, tail, w2pad, b, states, states, do), duty=duty)


FQ_BLK, FK_BLK = 7, 8
V_FOX_BLK = 36


def _fox_prep_fwd(p, tail, qg, kg, btail, duty=None):
    def body(q_ref, k_ref, t_ref, qg_ref, kg_ref, bt_ref, o_ref, cum_ref, carry):
        @pl.when(pl.program_id(0) == 0)
        def _():
            carry[...] = jnp.zeros_like(carry)

        for src, gr, off in ((q_ref, qg_ref, 0), (k_ref, kg_ref, BW)):
            for h in range(N_HEADS):
                xv = src[:, h * HD:(h + 1) * HD].astype(F32)
                r = lax.rsqrt(_mean1(xv * xv) + EPS)
                o_ref[:, off + h * HD:off + (h + 1) * HD] = (xv * r * gr[...]).astype(BF16)
        logf = _log_sigmoid(t_ref[...] + bt_ref[...])
        cum = _exact_dot(_tri(ROW_TILE, False), logf) + carry[...]
        cum_ref[...] = cum
        carry[...] = cum[ROW_TILE - 1:ROW_TILE, :]

    return _pcall(
        body, name="fox_prep_fwd", grid=(T // ROW_TILE,),
        out_shape=[jax.ShapeDtypeStruct((T, 2 * BW), BF16), jax.ShapeDtypeStruct((T, 128), F32)],
        in_specs=[_row_spec(BW, FQ_BLK), _row_spec(BW, FK_BLK), _row_spec(128),
                  _vec_spec(HD), _vec_spec(HD), _vec_spec(128)],
        out_specs=[_row_spec(2 * BW), _row_spec(128)],
        scratch_shapes=[pltpu.VMEM((1, 128), F32)],
        sem=("arbitrary",), args=(p, p, tail, qg, kg, btail), duty=duty)


def _fox_prep_bwd(p, tail, qg, kg, btail, dqn, dkn, dcum, duty=None):
    nt = T // ROW_TILE

    def body(q_ref, k_ref, t_ref, qg_ref, kg_ref, bt_ref, dq_ref, dk_ref, dc_ref, o_ref, dt_ref, st_ref, carry):
        @pl.when(pl.program_id(0) == 0)
        def _():
            carry[...] = jnp.zeros_like(carry)
            st_ref[...] = jnp.zeros_like(st_ref)

        for row, (src, gr, dsrc, off) in enumerate(((q_ref, qg_ref, dq_ref, 0), (k_ref, kg_ref, dk_ref, BW))):
            for h in range(N_HEADS):
                xv = src[:, h * HD:(h + 1) * HD].astype(F32)
                dy = dsrc[:, h * HD:(h + 1) * HD]
                r = lax.rsqrt(_mean1(xv * xv) + EPS)
                n = xv * r
                dn = dy * gr[...]
                o_ref[:, off + h * HD:off + (h + 1) * HD] = r * (dn - n * _mean1(dn * n))
                st_ref[row:row + 1, :] += _sum0(dy * n)
        z = t_ref[...] + bt_ref[...]
        dlogf = _exact_dot(_tri(ROW_TILE, True), dc_ref[...]) + carry[...]
        carry[...] = dlogf[0:1, :]
        lane = lax.broadcasted_iota(jnp.int32, (1, 128), 1)
        keep = (lane >= FF_LANE0) & (lane < FF_LANE0 + N_HEADS)
        dz = jnp.where(keep, dlogf * _sigmoid(-z), 0.0)
        dt_ref[...] = dz
        st_ref[2:3, :] += _sum0(dz)

    rs = lambda w, col=0: pl.BlockSpec((ROW_TILE, w), lambda i: (nt - 1 - i, col))
    return _pcall(
        body, name="fox_prep_bwd", grid=(nt,),
        out_shape=[jax.ShapeDtypeStruct((T, 2 * BW), F32), jax.ShapeDtypeStruct((T, 128), F32),
                   jax.ShapeDtypeStruct((8, 128), F32)],
        in_specs=[rs(BW, FQ_BLK), rs(BW, FK_BLK), rs(128), _vec_spec(HD), _vec_spec(HD), _vec_spec(128),
                  rs(BW), rs(BW), rs(128)],
        out_specs=[rs(2 * BW), rs(128), pl.BlockSpec((8, 128), lambda i: (0, 0))],
        scratch_shapes=[pltpu.VMEM((1, 128), F32)],
        sem=("arbitrary",), args=(p, p, tail, qg, kg, btail, dqn, dkn, dcum), duty=duty)


def _fox_logits(q_ref, k_ref, cc_ref, cr_ref, i, kl):
    rows, cols = _block_iotas(i, kl)
    s = _dot(q_ref[...], k_ref[0:kl, :], NT) * (HD ** -0.5) + cc_ref[0] - cr_ref[0, :, 0:kl]
    return jnp.where(cols <= rows, s, -1e30)


def _fox_specs():
    q_spec = pl.BlockSpec((TQ, HD), lambda h, i: (i, h))
    k_spec = pl.BlockSpec((T, HD), lambda h, i: (0, N_HEADS + h))
    v_spec = pl.BlockSpec((T, HD), lambda h, i: (0, V_FOX_BLK + h))
    col_spec = pl.BlockSpec((1, TQ, 1), lambda h, i: (h, i, 0))
    row_spec = pl.BlockSpec((1, 1, T), lambda h, i: (h, 0, 0))
    return q_spec, k_spec, v_spec, col_spec, row_spec


def _fox_fwd(qkn, p, cumcol, cumrow, duty=None):
    def body(q_ref, k_ref, v_ref, cc_ref, cr_ref, o_ref, lse_ref):
        i = pl.program_id(1)

        def visible(kl):
            s = _fox_logits(q_ref, k_ref, cc_ref, cr_ref, i, kl)
            m = jnp.max(s, axis=-1, keepdims=True)
            e = jnp.exp(s - m)
            l = jnp.sum(e, axis=-1, keepdims=True)
            o_ref[...] = _dot(e / l, v_ref[0:kl, :], NN)
            lse_ref[0] = m + jnp.log(l)

        _per_query_block(i, visible)

    q_spec, k_spec, v_spec, col_spec, row_spec = _fox_specs()
    return _pcall(
        body, name="fox_fwd", grid=(N_HEADS, T // TQ),
        out_shape=[jax.ShapeDtypeStruct((T, BW), F32), jax.ShapeDtypeStruct((N_HEADS, T, 1), F32)],
        in_specs=[q_spec, k_spec, v_spec, col_spec, row_spec], out_specs=[q_spec, col_spec],
        sem=("parallel", "parallel"), args=(qkn, qkn, p, cumcol, cumrow), duty=duty)


def _fox_bwd(qkn, p, cumcol, cumrow, lse, o, do, duty=None):
    def body(q_ref, k_ref, v_ref, cc_ref, cr_ref, lse_ref, o_ref, do_ref, dq_ref, dk_ref, dv_ref, dr_ref, dc_ref):
        i = pl.program_id(1)
        @pl.when(i == 0)
        def _():
            dk_ref[...] = jnp.zeros_like(dk_ref)
            dv_ref[...] = jnp.zeros_like(dv_ref)
            dc_ref[...] = jnp.zeros_like(dc_ref)

        def visible(kl):
            q, dov = q_ref[...], do_ref[...]
            pm = jnp.exp(_fox_logits(q_ref, k_ref, cc_ref, cr_ref, i, kl) - lse_ref[0])
            delta = jnp.sum(o_ref[...] * dov, axis=-1, keepdims=True)
            ds = pm * (_dot(dov, v_ref[0:kl, :], NT) - delta)
            dq_ref[...] = _dot(ds, k_ref[0:kl, :], NN) * (HD ** -0.5)
            dr_ref[0] = jnp.sum(ds, axis=-1, keepdims=True)
            dk_ref[0:kl, :] += _dot(ds, q, TN) * (HD ** -0.5)
            dv_ref[0:kl, :] += _dot(pm, dov, TN)
            dc_ref[0, :, 0:kl] += _sum0(ds)

        _per_query_block(i, visible)

    q_spec, k_spec, v_spec, col_spec, row_spec = _fox_specs()
    acc_spec = pl.BlockSpec((T, HD), lambda h, i: (0, h))
    sh = jax.ShapeDtypeStruct((T, BW), F32)
    return _pcall(
        body, name="fox_bwd", grid=(N_HEADS, T // TQ),
        out_shape=[sh, sh, sh, jax.ShapeDtypeStruct((N_HEADS, T, 1), F32), jax.ShapeDtypeStruct((N_HEADS, 1, T), F32)],
        in_specs=[q_spec, k_spec, v_spec, col_spec, row_spec, col_spec, q_spec, q_spec],
        out_specs=[q_spec, acc_spec, acc_spec, col_spec, row_spec],
        sem=("parallel", "arbitrary"), args=(qkn, qkn, p, cumcol, cumrow, lse, o, do), duty=duty)


def _mix_fwd(gpre, b_mg, y0, y1, y2, duty=None):
    def body(g_ref, b_ref, y0_ref, y1_ref, y2_ref, o_ref):
        acc = None
        for n, y_ref in enumerate((y0_ref, y1_ref, y2_ref)):
            sl = slice(n * D, (n + 1) * D)
            t = _sigmoid(g_ref[:, sl].astype(F32) + b_ref[:, sl]) * y_ref[...].astype(F32)
            acc = t if acc is None else acc + t
        o_ref[...] = acc.astype(BF16)

    return _pcall(
        body, name="mix_fwd", grid=(T // ROW_TILE,), out_shape=jax.ShapeDtypeStruct((T, D), BF16),
        in_specs=[_row_spec(3 * D), _vec_spec(3 * D), _row_spec(), _row_spec(), _row_spec()],
        out_specs=_row_spec(), sem=("parallel",), args=(gpre, b_mg, y0, y1, y2), duty=duty)


def _mix_bwd(gpre, b_mg, y0, y1, y2, dmi, duty=None):
    def body(g_ref, b_ref, y0_ref, y1_ref, y2_ref, d_ref, dy0_ref, dy1_ref, dy2_ref, dg_ref, db_ref):
        @pl.when(pl.program_id(0) == 0)
        def _():
            db_ref[...] = jnp.zeros_like(db_ref)

        dv = d_ref[...]
        for n, (y_ref, dy_ref) in enumerate(((y0_ref, dy0_ref), (y1_ref, dy1_ref), (y2_ref, dy2_ref))):
            sl = slice(n * D, (n + 1) * D)
            sg = _sigmoid(g_ref[:, sl].astype(F32) + b_ref[:, sl])
            dy_ref[...] = (dv * sg).astype(BF16)
            dpre = dv * y_ref[...].astype(F32) * (sg * (1.0 - sg))
            dg_ref[:, sl] = dpre.astype(BF16)
            db_ref[0:1, sl] += _sum0(dpre)

    shb = jax.ShapeDtypeStruct((T, D), BF16)
    return _pcall(
        body, name="mix_bwd", grid=(T // ROW_TILE,),
        out_shape=[shb, shb, shb, jax.ShapeDtypeStruct((T, 3 * D), BF16), jax.ShapeDtypeStruct((8, 3 * D), F32)],
        in_specs=[_row_spec(3 * D), _vec_spec(3 * D), _row_spec(), _row_spec(), _row_spec(), _row_spec()],
        out_specs=[_row_spec(), _row_spec(), _row_spec(), _row_spec(3 * D), pl.BlockSpec((8, 3 * D), lambda i: (0, 0))],
        sem=("arbitrary",), args=(gpre, b_mg, y0, y1, y2, dmi), duty=duty)


FF_COLS = 256
FF_NBLK = D_FF // FF_COLS


def _shift_rows(a, n):
    rows = lax.broadcasted_iota(jnp.int32, a.shape, 0)
    rolled = pltpu.roll(a, n % T, 0)
    return jnp.where((rows >= n) if n > 0 else (rows < T + n), rolled, 0.0)


def _ffn_act_fwd(uu, w_conv, b_conv, duty=None):
    def body(u_ref, g_ref, w_ref, b_ref, o_ref):
        u = u_ref[...].astype(F32)
        w = w_ref[...]
        uc = b_ref[...] + w[0:1, :] * _shift_rows(u, 2) + w[1:2, :] * _shift_rows(u, 1) + w[2:3, :] * u
        o_ref[...] = (uc * _sigmoid(uc) * g_ref[...].astype(F32)).astype(BF16)

    return _pcall(
        body, name="ffn_act_fwd", grid=(FF_NBLK,), out_shape=jax.ShapeDtypeStruct((T, D_FF), BF16),
        in_specs=[pl.BlockSpec((T, FF_COLS), lambda j: (0, j)), pl.BlockSpec((T, FF_COLS), lambda j: (0, FF_NBLK + j)),
                  pl.BlockSpec((3, FF_COLS), lambda j: (0, j)), pl.BlockSpec((1, FF_COLS), lambda j: (0, j))],
        out_specs=pl.BlockSpec((T, FF_COLS), lambda j: (0, j)),
        sem=("parallel",), args=(uu, uu, w_conv, b_conv), duty=duty)


def _ffn_act_bwd(uu, w_conv, b_conv, da, duty=None):
    def body(u_ref, g_ref, w_ref, b_ref, da_ref, d_ref, st_ref):
        u, w, dav = u_ref[...].astype(F32), w_ref[...], da_ref[...]
        u1, u2 = _shift_rows(u, 1), _shift_rows(u, 2)
        uc = b_ref[...] + w[0:1, :] * u2 + w[1:2, :] * u1 + w[2:3, :] * u
        sg = _sigmoid(uc)
        d_ref[1] = (dav * (uc * sg)).astype(BF16)
        duc = dav * g_ref[...].astype(F32) * (sg * (1.0 + uc * (1.0 - sg)))
        du = w[2:3, :] * duc + w[1:2, :] * _shift_rows(duc, -1) + w[0:1, :] * _shift_rows(duc, -2)
        d_ref[0] = du.astype(BF16)
        st_ref[...] = jnp.zeros_like(st_ref)
        st_ref[0:1, :] = _sum0(duc * u2)
        st_ref[1:2, :] = _sum0(duc * u1)
        st_ref[2:3, :] = _sum0(duc * u)
        st_ref[3:4, :] = _sum0(duc)

    cb = lambda rows=T, off=0: pl.BlockSpec((rows, FF_COLS), lambda j: (0, off + j))
    return _pcall(
        body, name="ffn_act_bwd", grid=(FF_NBLK,),
        out_shape=[jax.ShapeDtypeStruct((2, T, D_FF), BF16), jax.ShapeDtypeStruct((8, D_FF), F32)],
        in_specs=[cb(), cb(T, FF_NBLK), cb(3), cb(1), cb()],
        out_specs=[pl.BlockSpec((2, T, FF_COLS), lambda j: (0, 0, j)), cb(8)],
        sem=("parallel",), args=(uu, uu, w_conv, b_conv, da), duty=duty)


def _adamw(g, w, m, v, tr, name, tc=None):
    partial = g.ndim == 3
    R, C = w.shape
    tr = R if tr is None else tr
    tc = C if tc is None else tc
    assert R % tr == 0 and C % tc == 0 and (tr == R or tc == C)

    def body(g_ref, w_ref, m_ref, v_ref, go_ref, d_ref, mo_ref, vo_ref):
        if partial:
            gv = g_ref[0].astype(F32)
            for j in range(1, g.shape[0]):
                gv = gv + g_ref[j].astype(F32)
        else:
            gv = g_ref[...]
        go_ref[...] = gv
        mn = ADAM_B1 * m_ref[...] + (1.0 - ADAM_B1) * gv
        vn = ADAM_B2 * v_ref[...] + (1.0 - ADAM_B2) * (gv * gv)
        mo_ref[...] = mn
        vo_ref[...] = vn
        m_hat = mn / (1.0 - ADAM_B1 ** ADAM_STEP)
        v_hat = vn / (1.0 - ADAM_B2 ** ADAM_STEP)
        d_ref[...] = -ADAM_LR * (m_hat / (jnp.sqrt(v_hat) + ADAM_EPS) + ADAM_WD * w_ref[...])

    by_rows = tc == C
    spec = pl.BlockSpec((tr, tc), (lambda i: (i, 0)) if by_rows else (lambda i: (0, i)))
    g_spec = pl.BlockSpec((g.shape[0], tr, tc), (lambda i: (0, i, 0)) if by_rows else (lambda i: (0, 0, i))) if partial else spec
    sh = jax.ShapeDtypeStruct((R, C), F32)
    return pl.pallas_call(
        body, name=name, grid=(R // tr if by_rows else C // tc,), out_shape=[sh, sh, sh, sh],
        in_specs=[g_spec, spec, spec, spec], out_specs=[spec, spec, spec, spec],
        compiler_params=_cp("parallel"))(g, w, m, v)


def _chip_sum(dw, stage, name):
    n_chip, n, C = stage.shape

    def body(d_ref, s_ref, o_ref):
        mc = lax.axis_index("c")
        mine = jnp.where(mc == 0, d_ref[0, 0].astype(F32), d_ref[0, 1].astype(F32))
        o_ref[0] = (mine + s_ref[0].astype(F32)).astype(BF16)

    return pl.pallas_call(
        body, name=name, grid=(n_chip,), out_shape=jax.ShapeDtypeStruct(stage.shape, BF16),
        in_specs=[pl.BlockSpec((1, 2, n, C), lambda p: (p, 0, 0, 0)), pl.BlockSpec((1, n, C), lambda p: (p, 0, 0))],
        out_specs=pl.BlockSpec((1, n, C), lambda p: (p, 0, 0)), compiler_params=_cp("parallel"),
    )(dw.reshape(n_chip, 2, n, C), stage)


def _sum_partials(g, name, tr=None):
    n_part, R, C = g.shape
    tr = R if tr is None else tr
    assert R % tr == 0

    def body(g_ref, o_ref):
        acc = g_ref[0].astype(F32)
        for j in range(1, n_part):
            acc = acc + g_ref[j].astype(F32)
        o_ref[...] = acc

    return pl.pallas_call(
        body, name=name, grid=(R // tr,), out_shape=jax.ShapeDtypeStruct((R, C), F32),
        in_specs=[pl.BlockSpec((n_part, tr, C), lambda i: (0, i, 0))], out_specs=pl.BlockSpec((tr, C), lambda i: (i, 0)),
        compiler_params=_cp("parallel"))(g)


def _permute_in(w):
    pad = jnp.zeros(w.shape[:-1] + (NP - IN_W,), w.dtype)
    return jnp.concatenate([w[..., :3072], w[..., 3088:5136], w[..., 3072:3088], w[..., 5136:5140], pad], axis=-1)


def _unpermute_in(w):
    return jnp.concatenate([w[..., :3072], w[..., 5120:5136], w[..., 3072:5120], w[..., 5136:5140]], axis=-1)


def _flat_pack(arrs):
    flat = jnp.concatenate([a.reshape(-1).astype(F32) for a in arrs])
    n = flat.shape[0]
    rows = -(-n // 1024) * 8
    return jnp.pad(flat, (0, rows * 128 - n)).reshape(rows, 128)


def _flat_unpack(buf, shapes):
    flat = buf.reshape(-1)
    out, off = [], 0
    for s in shapes:
        n = int(np.prod(s))
        out.append(flat[off:off + n].reshape(s))
        off += n
    return out


GRAD_CHUNKS = dict(w_in=(128, 8), w_o=(128, 1), w_down=(352, 2), w_br0=(128, 1), w_br1=(128, 1), w_br2=(128, 1),
                   w_mg=(384, 4), w_up=(704, 11))
GRAD_VIA_CHIP = ("w_in", "w_down", "w_mg", "w_up")


def _send_grad(xfer, layer, k, g):
    if xfer is None:
        return g
    n, parts = GRAD_CHUNKS[k]
    row_bytes = g.shape[1] * 2
    if k not in GRAD_VIA_CHIP:
        for c in range(parts):
            xfer.add(k, ("a2a", 0, n, (layer,), c * (n // parts), n // parts), g, nbytes=n // parts * row_bytes)
        return g
    stage = ("stage", layer, k)
    xfer.lands[stage] = lax.empty((N_DEV // 2, n, g.shape[1]), BF16)
    xfer.add(stage, ("to_other_core", 0, n, (), 0, n), g, group=stage, nbytes=n * row_bytes)

    def both_halves_here():
        chip = _chip_sum(g, xfer.lands[stage], "chip_sum_" + k)
        for c in range(parts):
            xfer.add(k, ("a2a_chip", 0, n, (layer,), c * (n // parts), n // parts), chip, nbytes=n // parts * row_bytes)

    xfer.when_done(stage, both_halves_here)
    return g


def _weight(wl, k):
    return wl[k]() if callable(wl[k]) else wl[k]


MIN_CARRIER_US = 19.0


def _taker(xfer, fill=1.0):
    if xfer is None:
        return lambda us: None
    return lambda us: xfer.take_for(us * fill) if us >= MIN_CARRIER_US else None


def _layer_fwd(x0, wl, consts, xfer=None, fill=1.25):
    cosf, sinf, logg = consts
    row = lambda a: a.reshape(1, -1)
    take = _taker(xfer, fill)
    h = _norm_fwd(x0, row(wl["norm1_g"]), row(wl["scale1"]), row(wl["shift1"]), "norm1_fwd", duty=take(9))
    w_in = _weight(wl, "w_in")
    p = _matmul(h, w_in, "nn", "in_proj", out_dtype=BF16, n_blocks=(0, TAIL0 // 512), duty=take(30))
    tail = _matmul(h, w_in, "nn", "in_tail", tn=128, n_blocks=(TAIL_BLK, 1))
    qk = _rope_fwd(p, cosf, sinf, duty=take(10))
    ret_raw = _ret_fwd(qk, p, logg, duty=take(27))
    br0 = _branch_post_fwd(ret_raw, p, row(wl["ret_norm_g"]), RG_BLK, True, "ret_post_fwd", duty=take(9))
    gla_raw, states = _gla_fwd(p, tail, wl["w2pad"], row(wl["b_gla_a"]), duty=take(26))
    br1 = _branch_post_fwd(gla_raw, p, row(wl["gla_norm_g"]), GG_BLK, False, "gla_post_fwd", duty=take(9))
    qkn, cum = _fox_prep_fwd(p, tail, row(wl["q_norm_g"]), row(wl["k_norm_g"]), row(wl["btail"]), duty=take(10))
    cum4 = cum[:, FF_LANE0:FF_LANE0 + N_HEADS].T
    cumcol, cumrow = cum4.reshape(N_HEADS, T, 1), cum4.reshape(N_HEADS, 1, T)
    fox_o, lse = _fox_fwd(qkn, p, cumcol, cumrow, duty=take(30))
    w_br_t = _weight(wl, "w_br_t")
    ys = [_matmul(b, w_br_t[n], "nt", "br_proj%d" % n, out_dtype=BF16) for n, b in enumerate((br0, br1, fox_o))]
    gpre = _matmul(h, _weight(wl, "w_mg_t"), "nt", "gate_proj", out_dtype=BF16, duty=take(20))
    mixed_in = _mix_fwd(gpre, row(wl["b_mg"]), *ys, duty=take(21))
    mixed = _matmul(mixed_in, _weight(wl, "w_o"), "nn", "o_proj", duty=take(10))
    x1, h2 = _norm_fwd(x0, row(wl["norm2_g"]), row(wl["scale2"]), row(wl["shift2"]), "norm2_fwd",
                       m=mixed, gate=row(wl["gate1"]), duty=take(13))
    uu = _matmul(h2, _weight(wl, "w_up_t"), "nt", "up_proj", out_dtype=BF16, duty=take(30))
    act = _ffn_act_fwd(uu, wl["w_conv"], row(wl["b_conv"]), duty=take(25))
    y = _matmul(act, _weight(wl, "w_down"), "nn", "down_proj", tk=1408, duty=take(24))
    x2 = _axpy(x1, y, row(wl["gate2"]), "resid2", duty=take(11))
    saved = dict(x0=x0, h=h, p=p, tail=tail, qk=qk, ret_raw=ret_raw, br0=br0, gla_raw=gla_raw, states=states, br1=br1,
                 qkn=qkn, cumcol=cumcol, cumrow=cumrow, fox_o=fox_o, lse=lse, y0=ys[0], y1=ys[1], y2=ys[2],
                 gpre=gpre, mixed_in=mixed_in, mixed=mixed, x1=x1, h2=h2, uu=uu, act=act, y=y)
    return x2, saved


def _layer_bwd(dx2, wl, sv, consts, xfer=None, layer=0):
    cosf, sinf, logg = consts
    row = lambda a: a.reshape(1, -1)
    take = _taker(xfer)

    send = functools.partial(_send_grad, xfer, layer)

    dy, st_g2 = _gate_bwd(dx2, sv["y"], row(wl["gate2"]), "gate2_bwd", duty=take(10))
    dact = _matmul(dy, _weight(wl, "w_down"), "nt", "down_dx", tn=1408, duty=take(21))
    d_down = send("w_down", _matmul(sv["act"], dy, "tn", "down_dw", out_dtype=BF16, tm=1408, duty=take(19)))
    duu, st_conv = _ffn_act_bwd(sv["uu"], wl["w_conv"], row(wl["b_conv"]), dact, duty=take(40))
    dh2 = _matmul(duu, _weight(wl, "w_up_t"), "nn", "up_dx", tk=1408, duty=take(42))
    d_up_t = send("w_up", _matmul(duu, sv["h2"], "tn", "up_dw", out_dtype=BF16, tm=1408, duty=take(33)))
    dx1, st_n2 = _norm_bwd(sv["x1"], dh2, dx2, row(wl["norm2_g"]), row(wl["scale2"]), row(wl["shift2"]), "norm2_bwd",
                           duty=take(15))
    dmixed, st_g1 = _gate_bwd(dx1, sv["mixed"], row(wl["gate1"]), "gate1_bwd", duty=take(10))
    dmi = _matmul(dmixed, _weight(wl, "w_o"), "nt", "o_dx", duty=take(11))
    d_o = send("w_o", _matmul(sv["mixed_in"], dmixed, "tn", "o_dw", out_dtype=BF16, duty=take(9)))
    dy0, dy1, dy2, dgpre, st_bmg = _mix_bwd(sv["gpre"], row(wl["b_mg"]), sv["y0"], sv["y1"], sv["y2"], dmi,
                                             duty=take(31))
    brs = (sv["br0"], sv["br1"], sv["fox_o"])
    w_br_t = _weight(wl, "w_br_t")
    dbr = [_matmul(d, w_br_t[n], "nn", "br_dx%d" % n) for n, d in enumerate((dy0, dy1, dy2))]
    d_br_t = [send("w_br%d" % n, _matmul(d, brs[n], "tn", "br_dw%d" % n, out_dtype=BF16))
              for n, d in enumerate((dy0, dy1, dy2))]
    dh = _matmul(dgpre, _weight(wl, "w_mg_t"), "nn", "gate_dx", tk=1024, duty=take(29))
    d_mg_t = send("w_mg", _matmul(dgpre, sv["h"], "tn", "gate_dw", out_dtype=BF16, duty=take(21)))
    p, tail = sv["p"], sv["tail"]
    dqn, dkn, dfv, drow, dcol = _fox_bwd(sv["qkn"], p, sv["cumcol"], sv["cumrow"], sv["lse"], sv["fox_o"], dbr[2],
                                         duty=take(50))
    dcum4 = drow.reshape(N_HEADS, T) - dcol.reshape(N_HEADS, T)
    dcum = jnp.pad(dcum4.T, ((0, 0), (FF_LANE0, 128 - FF_LANE0 - N_HEADS)))
    dfqk, dtail_fox, st_fox = _fox_prep_bwd(p, tail, row(wl["q_norm_g"]), row(wl["k_norm_g"]), row(wl["btail"]), dqn, dkn, dcum,
                                            duty=take(15))
    dgla_raw, dgg, st_gn = _branch_post_bwd(sv["gla_raw"], p, row(wl["gla_norm_g"]), dbr[1], GG_BLK, False, "gla_post_bwd",
                                            duty=take(12))
    dgq, dgk, dgv, dtail_gla, dw2pad, st_bg = _gla_bwd(p, tail, wl["w2pad"], row(wl["b_gla_a"]), sv["states"], dgla_raw,
                                                       duty=take(30))
    dret_raw, drg, st_rn = _branch_post_bwd(sv["ret_raw"], p, row(wl["ret_norm_g"]), dbr[0], RG_BLK, True, "ret_post_bwd",
                                            duty=take(13))
    dqr, dkr, drv = _ret_bwd(sv["qk"], p, logg, dret_raw, duty=take(50))
    drqk = _rope_bwd(dqr, dkr, cosf, sinf, duty=take(11))
    dp = jnp.concatenate([a.astype(BF16) for a in (drqk, drv, drg, dgq, dgk, dgv, dgg, dfqk, dfv, dtail_fox + dtail_gla)]
                         + [jnp.zeros((T, NP - TAIL0 - 128), BF16)], axis=1)
    dh = _matmul(dp, _weight(wl, "w_in"), "nt", "in_dx", tk=1408, add=dh, duty=take(45))
    d_in = send("w_in", _matmul(sv["h"], dp, "tn", "in_dw", out_dtype=BF16, duty=take(32)))
    dx0, st_n1 = _norm_bwd(sv["x0"], dh, dx1, row(wl["norm1_g"]), row(wl["scale1"]), row(wl["shift1"]), "norm1_bwd",
                           duty=take(15))
    big = dict(w_in=d_in, w_o=d_o, w_down=d_down, w_br0=d_br_t[0], w_br1=d_br_t[1], w_br2=d_br_t[2], w_mg=d_mg_t,
               w_up=d_up_t)
    dmod = jnp.concatenate([st_n1[2], st_n1[1], st_g1[0], st_n2[2], st_n2[1], st_g2[0]])
    small = dict(norm1_g=st_n1[0], norm2_g=st_n2[0], b_gla_a=st_bg[0], b_fox_f=st_fox[2, FF_LANE0:FF_LANE0 + N_HEADS],
                 ret_norm_g=st_rn[0], gla_norm_g=st_gn[0], q_norm_g=st_fox[0], k_norm_g=st_fox[1], b_mg=st_bmg[0],
                 b_conv=st_conv[3], w_gla_a2=dw2pad[:LR_LANES], w_conv=st_conv[0:3])
    return dx0, big, dmod, small


SMALL_REPL = ("norm1_g", "norm2_g", "b_ada", "b_gla_a", "b_fox_f", "ret_norm_g", "gla_norm_g", "q_norm_g", "k_norm_g",
              "b_mg", "b_conv")
SMALL_SHARDED = ("w_gla_a2", "w_conv")
BIG = ("w_in", "w_o", "w_down", "w_br", "w_mg", "w_up")
WEIGHTS = ("norm1_g", "norm2_g", "w_ada", "b_ada", "w_in", "w_gla_a2", "b_gla_a", "b_fox_f", "ret_norm_g", "gla_norm_g",
           "q_norm_g", "k_norm_g", "w_br", "w_mg", "b_mg", "w_o", "w_up", "w_conv", "b_conv", "w_down")


def kernel(x, c, norm1_g, norm2_g, w_ada, b_ada, w_in, w_gla_a2, b_gla_a, b_fox_f, ret_norm_g, gla_norm_g, q_norm_g, k_norm_g, w_br, w_mg, b_mg, w_o, w_up, w_conv, b_conv, w_down, loss_target, m_norm1_g, m_norm2_g, m_w_ada, m_b_ada, m_w_in, m_w_gla_a2, m_b_gla_a, m_b_fox_f, m_ret_norm_g, m_gla_norm_g, m_q_norm_g, m_k_norm_g, m_w_br, m_w_mg, m_b_mg, m_w_o, m_w_up, m_w_conv, m_b_conv, m_w_down, v_norm1_g, v_norm2_g, v_w_ada, v_b_ada, v_w_in, v_w_gla_a2, v_b_gla_a, v_b_fox_f, v_ret_norm_g, v_gla_norm_g, v_q_norm_g, v_k_norm_g, v_w_br, v_w_mg, v_b_mg, v_w_o, v_w_up, v_w_conv, v_b_conv, v_w_down):
    W = dict(norm1_g=norm1_g, norm2_g=norm2_g, w_ada=w_ada, b_ada=b_ada, w_in=w_in, w_gla_a2=w_gla_a2, b_gla_a=b_gla_a,
             b_fox_f=b_fox_f, ret_norm_g=ret_norm_g, gla_norm_g=gla_norm_g, q_norm_g=q_norm_g, k_norm_g=k_norm_g,
             w_br=w_br, w_mg=w_mg, b_mg=b_mg, w_o=w_o, w_up=w_up, w_conv=w_conv, b_conv=b_conv, w_down=w_down)
    M = dict(norm1_g=m_norm1_g, norm2_g=m_norm2_g, w_ada=m_w_ada, b_ada=m_b_ada, w_in=m_w_in, w_gla_a2=m_w_gla_a2,
             b_gla_a=m_b_gla_a, b_fox_f=m_b_fox_f, ret_norm_g=m_ret_norm_g, gla_norm_g=m_gla_norm_g, q_norm_g=m_q_norm_g,
             k_norm_g=m_k_norm_g, w_br=m_w_br, w_mg=m_w_mg, b_mg=m_b_mg, w_o=m_w_o, w_up=m_w_up, w_conv=m_w_conv,
             b_conv=m_b_conv, w_down=m_w_down)
    V = dict(norm1_g=v_norm1_g, norm2_g=v_norm2_g, w_ada=v_w_ada, b_ada=v_b_ada, w_in=v_w_in, w_gla_a2=v_w_gla_a2,
             b_gla_a=v_b_gla_a, b_fox_f=v_b_fox_f, ret_norm_g=v_ret_norm_g, gla_norm_g=v_gla_norm_g, q_norm_g=v_q_norm_g,
             k_norm_g=v_k_norm_g, w_br=v_w_br, w_mg=v_w_mg, b_mg=v_b_mg, w_o=v_w_o, w_up=v_w_up, w_conv=v_w_conv,
             b_conv=v_b_conv, w_down=v_w_down)
    me = 4 * lax.axis_index("x") + 2 * lax.axis_index("y") + lax.axis_index("c")
    x2d, tgt = x.reshape(T, D), loss_target.reshape(T, D)

    sm = _flat_pack([c, w_gla_a2, w_conv])
    sm_all = _exchange(sm, True, "gather_small")
    parts = [_flat_unpack(sm_all[j], [(D,), (DEPTH, LR_LANES, 32), (DEPTH, 3, 352)]) for j in range(N_DEV)]
    c_all = jnp.stack([q[0] for q in parts])
    w_gla_full = jnp.concatenate([q[1] for q in parts], axis=2)
    w_conv_full = jnp.concatenate([q[2] for q in parts], axis=2)

    n_ada = w_ada.shape[2]
    b_loc = lax.dynamic_slice_in_dim(b_ada, me * n_ada, n_ada, axis=1).reshape(DEPTH, 1, n_ada)
    mod_all = _ada_fwd(c_all, w_ada, b_loc)
    mod_recv = _exchange(jnp.swapaxes(mod_all, 0, 1), False, "a2a_mod")
    mod = jnp.swapaxes(mod_recv, 0, 1).reshape(DEPTH, 6, D)

    loc = dict(w_in=_permute_in(w_in), w_o=w_o, w_down=w_down, w_br=jnp.swapaxes(w_br, 2, 3),
               w_mg=jnp.swapaxes(w_mg, 1, 2), w_up=jnp.swapaxes(w_up, 1, 2))
    loc = {k: v.astype(BF16) for k, v in loc.items()}
    w_full = dict(w_in=(D, NP), w_o=(D, D), w_down=(D_FF, D), w_br=(3, D, BW), w_mg=(3 * D, D), w_up=(2 * D_FF, D))
    w_parts = dict(w_in=8, w_br=2, w_mg=4, w_o=1, w_up=11, w_down=2)
    gather, units = _Transfers("gather"), []
    for l in range(DEPTH):
        for k, parts in w_parts.items():
            axis = 1 if k == "w_br" else 0
            n = w_full[k][axis] // N_DEV
            gather.lands[(l, k)] = lax.empty(w_full[k], BF16)
            shard = loc[k][l]
            nbytes = shard.size * 2 // parts
            units += [((l, k), (axis, n, (), c * (n // parts), n // parts), shard, nbytes) for c in range(parts)]
    first, lag = w_parts["w_in"], 4
    order = [("cross", i) for i in range(first)] + [("pass", i) for i in range(first)]
    for i in range(first, len(units) + lag):
        order += [("cross", i)] if i < len(units) else []
        order += [("pass", i - lag)] if i - lag >= first else []
    for what, i in order:
        key, where, shard, nbytes = units[i]
        if what == "cross":
            gather.add(key, ("gather_chip",) + where, shard, uid=i, nbytes=nbytes)
        else:
            gather.add(key, ("pass_on",) + where, after=i, nbytes=nbytes)

    w2pad = jnp.pad(w_gla_full, ((0, 0), (0, 128 - LR_LANES), (0, 0)))
    btail = jnp.pad(b_fox_f, ((0, 0), (FF_LANE0, 128 - FF_LANE0 - N_HEADS)))
    stacked = dict(norm1_g=norm1_g, norm2_g=norm2_g, b_gla_a=b_gla_a, ret_norm_g=ret_norm_g, gla_norm_g=gla_norm_g,
                   q_norm_g=q_norm_g, k_norm_g=k_norm_g, b_mg=b_mg, b_conv=b_conv, w_conv=w_conv_full, w2pad=w2pad,
                   btail=btail, shift1=mod[:, 0], scale1=mod[:, 1], gate1=mod[:, 2], shift2=mod[:, 3], scale2=mod[:, 4],
                   gate2=mod[:, 5])
    landed = lambda l, k: functools.partial(gather.get, (l, k))
    layers = [dict({k: v[l] for k, v in stacked.items()}, w_in=landed(l, "w_in"), w_o=landed(l, "w_o"),
                   w_down=landed(l, "w_down"), w_br_t=landed(l, "w_br"), w_mg_t=landed(l, "w_mg"), w_up_t=landed(l, "w_up"))
              for l in range(DEPTH)]
    consts = _rope_tables() + (_ret_logg(),)

    xc, saved = x2d, []
    for l in range(DEPTH):
        xc, sv = _layer_fwd(xc, layers[l], consts, gather, 1.7 if l == 0 else 1.45)
        saved.append(sv)
    loss_part, dxc = _loss_fwd_bwd(xc, tgt)
    loss = lax.psum(loss_part[0, 0], ("x", "y", "c"))

    grad_names = ("w_in", "w_o", "w_down", "w_br0", "w_br1", "w_br2", "w_mg", "w_up")
    blk_rows = dict(w_in=(128, NP), w_o=(128, D), w_down=(352, D), w_br0=(128, BW), w_br1=(128, BW), w_br2=(128, BW),
                    w_mg=(384, D), w_up=(704, D))
    grads = _Transfers("grads")
    for k in grad_names:
        grads.lands[k] = lax.empty((N_DEV // 2 if k in GRAD_VIA_CHIP else N_DEV, DEPTH) + blk_rows[k], BF16)
    dmod, small_g = [None] * DEPTH, [None] * DEPTH
    for l in reversed(range(DEPTH)):
        dxc, _, dmod[l], small_g[l] = _layer_bwd(dxc, layers[l], saved[l], consts, grads, l)
    grad_x = dxc
    dmod = jnp.stack(dmod)
    small_g = {k: jnp.stack([s[k] for s in small_g]) for k in small_g[0]}
    grads.drain()
    recv = {k: grads.get(k) for k in grad_names}

    dmod_send = jnp.swapaxes(dmod.reshape(DEPTH, N_DEV, n_ada), 0, 1)
    dmod_all = jnp.swapaxes(_exchange(dmod_send, False, "a2a_dmod"), 0, 1)
    g_ada = _ada_bwd(c_all, dmod_all)

    def flat(a, k):
        return a.reshape((-1, W[k].shape[-1]))

    def adam_nat(k, g, tr):
        outs = _adamw(g, flat(W[k], k), flat(M[k], k), flat(V[k], k), tr, "adamw_" + k)
        return [o.reshape(W[k].shape) for o in outs]

    def summed(k, tr):
        r = recv[k]
        return _sum_partials(r.reshape(r.shape[0], DEPTH * r.shape[2], r.shape[3]), "sum_" + k, tr).reshape((DEPTH,) + r.shape[2:])

    def adam_as_stored(k, g, perm, tr, tc=None):
        shape_t = tuple(W[k].shape[a] for a in perm)
        view = lambda a: jnp.transpose(a, perm).reshape(g.shape[-2:])
        outs = _adamw(g, view(W[k]), view(M[k]), view(V[k]), tr, "adamw_" + k, tc)
        return [jnp.transpose(o.reshape(shape_t), tuple(np.argsort(perm))) for o in outs]

    big_out = dict(
        w_in=adam_nat("w_in", flat(_unpermute_in(summed("w_in", 64)), "w_in"), 64),
        w_o=adam_nat("w_o", recv["w_o"].reshape(N_DEV, DEPTH * 128, D), 128),
        w_down=adam_nat("w_down", recv["w_down"].reshape(N_DEV // 2, DEPTH * 352, D), 352),
        w_br=adam_nat("w_br", flat(jnp.swapaxes(jnp.stack([summed("w_br%d" % n, 128) for n in range(3)], axis=1), 2, 3),
                                   "w_br"), 1024),
        w_mg=adam_nat("w_mg", flat(jnp.swapaxes(summed("w_mg", 384), 1, 2), "w_mg"), 512),
        w_up=adam_as_stored("w_up", recv["w_up"].reshape(N_DEV // 2, DEPTH * 704, D), (0, 2, 1), 352))
    ada_out = [o.reshape(DEPTH, D, n_ada) for o in _adamw(
        g_ada.reshape(DEPTH * D, n_ada), w_ada.reshape(DEPTH * D, n_ada), m_w_ada.reshape(DEPTH * D, n_ada),
        v_w_ada.reshape(DEPTH * D, n_ada), 512, "adamw_ada")]

    small_g = dict(small_g, b_ada=dmod)
    names = SMALL_REPL + SMALL_SHARDED
    full_shapes = [W[n].shape for n in SMALL_REPL] + [(DEPTH, LR_LANES, 256), (DEPTH, 3, D_FF)]
    part = _flat_pack([small_g[n] for n in names])
    total = _flat_unpack(_sum_partials(_exchange(part, True, "gather_small_grads"), "sum_small"), full_shapes)
    total = dict(zip(names, total))
    total["w_gla_a2"] = lax.dynamic_slice_in_dim(total["w_gla_a2"], me * 32, 32, axis=2)
    total["w_conv"] = lax.dynamic_slice_in_dim(total["w_conv"], me * 352, 352, axis=2)
    shapes = [W[n].shape for n in names]
    small_out = _adamw(_flat_pack([total[n] for n in names]), _flat_pack([W[n] for n in names]),
                       _flat_pack([M[n] for n in names]), _flat_pack([V[n] for n in names]), None, "adamw_small")
    small_out = [dict(zip(names, _flat_unpack(o, shapes))) for o in small_out]

    outs = []
    for k in range(4):
        d = dict(small_out[k])
        d.update({n: big_out[n][k] for n in BIG})
        d["w_ada"] = ada_out[k]
        outs.append([d[n] for n in WEIGHTS])
    return (loss, grad_x.reshape(1, T, D), *outs[0], *outs[1], *outs[2], *outs[3])
```
